```python
import jax, jax.numpy as jnp
from jax import lax
import numpy as np

D_MODEL = 1024
BATCH = 8
SEQ = 4096
DEPTH = 4

N_MIXERS = 3
SHORT_CONV_W = 3
POOL_WINDOWS = (2, 4, 8, 16)
N_POOL_GROUPS = len(POOL_WINDOWS)
POOL_GROUP_W = D_MODEL // N_POOL_GROUPS
CONF_CONV_W = 31
D_FF = ((8 * D_MODEL // 3 + 255) // 256) * 256
RMS_EPS = 1e-6
LN_EPS = 1e-5

kernel_name = "hybrid_interleaved_conv_pool_conformer"


def rmsnorm(x, g):
    xf = x.astype(jnp.float32)
    y = xf * lax.rsqrt(jnp.mean(xf * xf, axis=-1, keepdims=True) + RMS_EPS)
    return (y * g.astype(jnp.float32)).astype(x.dtype)


def layernorm(x, g, b):
    xf = x.astype(jnp.float32)
    mu = jnp.mean(xf, axis=-1, keepdims=True)
    xc = xf - mu
    var = jnp.mean(xc * xc, axis=-1, keepdims=True)
    y = xc * lax.rsqrt(var + LN_EPS) * g.astype(jnp.float32) + b.astype(jnp.float32)
    return y.astype(x.dtype)


def causal_depthwise_conv(u, w):
    k, c = w.shape
    return lax.conv_general_dilated(
        u, w[:, None, :].astype(u.dtype),
        window_strides=(1,), padding=((k - 1, 0),),
        dimension_numbers=("NWC", "WIO", "NWC"),
        feature_group_count=c)


def short_conv_mixer(u, w_in, conv_w, w_out):
    b, c, v = jnp.split(u @ w_in, 3, axis=-1)
    return (b * causal_depthwise_conv(c * v, conv_w)) @ w_out


def multiscale_pool_mixer(u, w_grp, scale):
    bn, s, d = u.shape
    ug = u.reshape(bn, s, N_POOL_GROUPS, POOL_GROUP_W).astype(jnp.float32)
    cs = jnp.cumsum(ug, axis=1)
    t = jnp.arange(s)
    pooled = []
    for g, w in enumerate(POOL_WINDOWS):
        c = cs[:, :, g, :]
        lag = jnp.pad(c, ((0, 0), (w, 0), (0, 0)))[:, :s]
        cnt = jnp.minimum(t + 1, w).astype(jnp.float32)[None, :, None]
        pooled.append((c - lag) / cnt)
    mixed = (jnp.stack(pooled, axis=2) - ug).astype(u.dtype)
    y = jnp.einsum("bsgc,gcd->bsgd", mixed, w_grp)
    return y.reshape(bn, s, d) * scale


def conformer_conv_module(u, w_pw1, b_pw1, w_dw, b_dw, ln_g, ln_b, w_pw2, b_pw2):
    a, gate = jnp.split(u @ w_pw1 + b_pw1, 2, axis=-1)
    h = a * jax.nn.sigmoid(gate)
    h = causal_depthwise_conv(h, w_dw) + b_dw
    h = jax.nn.silu(layernorm(h, ln_g, ln_b))
    return h @ w_pw2 + b_pw2


def swiglu_ffn(u, w_gu, w_down):
    g, up = jnp.split(u @ w_gu, 2, axis=-1)
    return (jax.nn.silu(g) * up) @ w_down


def _fwd_setup_inputs(seed: int = 0) -> dict:
    key = jax.random.key(seed)
    keys = iter(jax.random.split(key, 64))
    d, f = D_MODEL, D_FF

    def nrm(shape, scale):
        return jax.random.normal(next(keys), shape, jnp.float32) * scale

    def gain(n):
        return 1.0 + nrm((n,), 0.05)

    def short_conv_params(i):
        return {f"a{i}_w_in": nrm((d, 3 * d), d ** -0.5),
                f"a{i}_conv": nrm((SHORT_CONV_W, d), SHORT_CONV_W ** -0.5),
                f"a{i}_w_out": nrm((d, d), d ** -0.5)}

    def ffn_params(i):
        return {f"ln2_{i}": gain(d),
                f"ffn{i}_w_gu": nrm((d, 2 * f), d ** -0.5),
                f"ffn{i}_w_down": nrm((f, d), f ** -0.5)}

    p = {"x": nrm((BATCH, SEQ, d), 1.0)}
    p["ln1_0"] = gain(d)
    p.update(short_conv_params(0))
    p.update(ffn_params(0))
    p["ln1_1"] = gain(d)
    p["b1_w_grp"] = nrm((N_POOL_GROUPS, POOL_GROUP_W, POOL_GROUP_W), POOL_GROUP_W ** -0.5)
    p["b1_scale"] = 1.0 + nrm((d,), 0.1)
    p.update(ffn_params(1))
    p["ln1_2"] = gain(d)
    p["c2_w_pw1"] = nrm((d, 2 * d), d ** -0.5)
    p["c2_b_pw1"] = nrm((2 * d,), 0.02)
    p["c2_dw"] = nrm((CONF_CONV_W, d), CONF_CONV_W ** -0.5)
    p["c2_b_dw"] = nrm((d,), 0.02)
    p["c2_ln_g"] = gain(d)
    p["c2_ln_b"] = nrm((d,), 0.02)
    p["c2_w_pw2"] = nrm((d, d), d ** -0.5)
    p["c2_b_pw2"] = nrm((d,), 0.02)
    p.update(ffn_params(2))
    p["ln1_3"] = gain(d)
    p.update(short_conv_params(3))
    p.update(ffn_params(3))
    p["ln_f"] = gain(d)
    return p


def _fwd_reference(x,
              ln1_0, a0_w_in, a0_conv, a0_w_out, ln2_0, ffn0_w_gu, ffn0_w_down,
              ln1_1, b1_w_grp, b1_scale, ln2_1, ffn1_w_gu, ffn1_w_down,
              ln1_2, c2_w_pw1, c2_b_pw1, c2_dw, c2_b_dw, c2_ln_g, c2_ln_b, c2_w_pw2, c2_b_pw2,
              ln2_2, ffn2_w_gu, ffn2_w_down,
              ln1_3, a3_w_in, a3_conv, a3_w_out, ln2_3, ffn3_w_gu, ffn3_w_down,
              ln_f):
    mixer_fns = (short_conv_mixer, multiscale_pool_mixer, conformer_conv_module)
    mixer_params = (
        (a0_w_in, a0_conv, a0_w_out),
        (b1_w_grp, b1_scale),
        (c2_w_pw1, c2_b_pw1, c2_dw, c2_b_dw, c2_ln_g, c2_ln_b, c2_w_pw2, c2_b_pw2),
        (a3_w_in, a3_conv, a3_w_out),
    )
    pre_mix_norms = (ln1_0, ln1_1, ln1_2, ln1_3)
    pre_ffn_norms = (ln2_0, ln2_1, ln2_2, ln2_3)
    ffn_params = ((ffn0_w_gu, ffn0_w_down), (ffn1_w_gu, ffn1_w_down),
                  (ffn2_w_gu, ffn2_w_down), (ffn3_w_gu, ffn3_w_down))

    h = x
    for i in range(DEPTH):
        mixer = mixer_fns[i % N_MIXERS]
        h = h + mixer(rmsnorm(h, pre_mix_norms[i]), *mixer_params[i])
        h = h + swiglu_ffn(rmsnorm(h, pre_ffn_norms[i]), *ffn_params[i])
    return rmsnorm(h, ln_f)


import jax as _jax
import jax.numpy as _jnp

TWIN_FORMAT = 'train_step'
FWD_PARAMS = ['x', 'ln1_0', 'a0_w_in', 'a0_conv', 'a0_w_out', 'ln2_0', 'ffn0_w_gu', 'ffn0_w_down', 'ln1_1', 'b1_w_grp', 'b1_scale', 'ln2_1', 'ffn1_w_gu', 'ffn1_w_down', 'ln1_2', 'c2_w_pw1', 'c2_b_pw1', 'c2_dw', 'c2_b_dw', 'c2_ln_g', 'c2_ln_b', 'c2_w_pw2', 'c2_b_pw2', 'ln2_2', 'ffn2_w_gu', 'ffn2_w_down', 'ln1_3', 'a3_w_in', 'a3_conv', 'a3_w_out', 'ln2_3', 'ffn3_w_gu', 'ffn3_w_down', 'ln_f']
TWIN_WEIGHTS = ['ln1_0', 'a0_w_in', 'a0_conv', 'a0_w_out', 'ln2_0', 'ffn0_w_gu', 'ffn0_w_down', 'ln1_1', 'b1_w_grp', 'b1_scale', 'ln2_1', 'ffn1_w_gu', 'ffn1_w_down', 'ln1_2', 'c2_w_pw1', 'c2_b_pw1', 'c2_dw', 'c2_b_dw', 'c2_ln_g', 'c2_ln_b', 'c2_w_pw2', 'c2_b_pw2', 'ln2_2', 'ffn2_w_gu', 'ffn2_w_down', 'ln1_3', 'a3_w_in', 'a3_conv', 'a3_w_out', 'ln2_3', 'ffn3_w_gu', 'ffn3_w_down', 'ln_f']
TWIN_DIFF_INPUT = 'x'
TWIN_INPUTS = ['x', 'ln1_0', 'a0_w_in', 'a0_conv', 'a0_w_out', 'ln2_0', 'ffn0_w_gu', 'ffn0_w_down', 'ln1_1', 'b1_w_grp', 'b1_scale', 'ln2_1', 'ffn1_w_gu', 'ffn1_w_down', 'ln1_2', 'c2_w_pw1', 'c2_b_pw1', 'c2_dw', 'c2_b_dw', 'c2_ln_g', 'c2_ln_b', 'c2_w_pw2', 'c2_b_pw2', 'ln2_2', 'ffn2_w_gu', 'ffn2_w_down', 'ln1_3', 'a3_w_in', 'a3_conv', 'a3_w_out', 'ln2_3', 'ffn3_w_gu', 'ffn3_w_down', 'ln_f', 'loss_target', 'm_ln1_0', 'm_a0_w_in', 'm_a0_conv', 'm_a0_w_out', 'm_ln2_0', 'm_ffn0_w_gu', 'm_ffn0_w_down', 'm_ln1_1', 'm_b1_w_grp', 'm_b1_scale', 'm_ln2_1', 'm_ffn1_w_gu', 'm_ffn1_w_down', 'm_ln1_2', 'm_c2_w_pw1', 'm_c2_b_pw1', 'm_c2_dw', 'm_c2_b_dw', 'm_c2_ln_g', 'm_c2_ln_b', 'm_c2_w_pw2', 'm_c2_b_pw2', 'm_ln2_2', 'm_ffn2_w_gu', 'm_ffn2_w_down', 'm_ln1_3', 'm_a3_w_in', 'm_a3_conv', 'm_a3_w_out', 'm_ln2_3', 'm_ffn3_w_gu', 'm_ffn3_w_down', 'm_ln_f', 'v_ln1_0', 'v_a0_w_in', 'v_a0_conv', 'v_a0_w_out', 'v_ln2_0', 'v_ffn0_w_gu', 'v_ffn0_w_down', 'v_ln1_1', 'v_b1_w_grp', 'v_b1_scale', 'v_ln2_1', 'v_ffn1_w_gu', 'v_ffn1_w_down', 'v_ln1_2', 'v_c2_w_pw1', 'v_c2_b_pw1', 'v_c2_dw', 'v_c2_b_dw', 'v_c2_ln_g', 'v_c2_ln_b', 'v_c2_w_pw2', 'v_c2_b_pw2', 'v_ln2_2', 'v_ffn2_w_gu', 'v_ffn2_w_down', 'v_ln1_3', 'v_a3_w_in', 'v_a3_conv', 'v_a3_w_out', 'v_ln2_3', 'v_ffn3_w_gu', 'v_ffn3_w_down', 'v_ln_f']
TWIN_OUTPUTS = ['loss', 'grad_x', 'grad_ln1_0', 'grad_a0_w_in', 'grad_a0_conv', 'grad_a0_w_out', 'grad_ln2_0', 'grad_ffn0_w_gu', 'grad_ffn0_w_down', 'grad_ln1_1', 'grad_b1_w_grp', 'grad_b1_scale', 'grad_ln2_1', 'grad_ffn1_w_gu', 'grad_ffn1_w_down', 'grad_ln1_2', 'grad_c2_w_pw1', 'grad_c2_b_pw1', 'grad_c2_dw', 'grad_c2_b_dw', 'grad_c2_ln_g', 'grad_c2_ln_b', 'grad_c2_w_pw2', 'grad_c2_b_pw2', 'grad_ln2_2', 'grad_ffn2_w_gu', 'grad_ffn2_w_down', 'grad_ln1_3', 'grad_a3_w_in', 'grad_a3_conv', 'grad_a3_w_out', 'grad_ln2_3', 'grad_ffn3_w_gu', 'grad_ffn3_w_down', 'grad_ln_f', 'delta_ln1_0', 'delta_a0_w_in', 'delta_a0_conv', 'delta_a0_w_out', 'delta_ln2_0', 'delta_ffn0_w_gu', 'delta_ffn0_w_down', 'delta_ln1_1', 'delta_b1_w_grp', 'delta_b1_scale', 'delta_ln2_1', 'delta_ffn1_w_gu', 'delta_ffn1_w_down', 'delta_ln1_2', 'delta_c2_w_pw1', 'delta_c2_b_pw1', 'delta_c2_dw', 'delta_c2_b_dw', 'delta_c2_ln_g', 'delta_c2_ln_b', 'delta_c2_w_pw2', 'delta_c2_b_pw2', 'delta_ln2_2', 'delta_ffn2_w_gu', 'delta_ffn2_w_down', 'delta_ln1_3', 'delta_a3_w_in', 'delta_a3_conv', 'delta_a3_w_out', 'delta_ln2_3', 'delta_ffn3_w_gu', 'delta_ffn3_w_down', 'delta_ln_f', 'new_m_ln1_0', 'new_m_a0_w_in', 'new_m_a0_conv', 'new_m_a0_w_out', 'new_m_ln2_0', 'new_m_ffn0_w_gu', 'new_m_ffn0_w_down', 'new_m_ln1_1', 'new_m_b1_w_grp', 'new_m_b1_scale', 'new_m_ln2_1', 'new_m_ffn1_w_gu', 'new_m_ffn1_w_down', 'new_m_ln1_2', 'new_m_c2_w_pw1', 'new_m_c2_b_pw1', 'new_m_c2_dw', 'new_m_c2_b_dw', 'new_m_c2_ln_g', 'new_m_c2_ln_b', 'new_m_c2_w_pw2', 'new_m_c2_b_pw2', 'new_m_ln2_2', 'new_m_ffn2_w_gu', 'new_m_ffn2_w_down', 'new_m_ln1_3', 'new_m_a3_w_in', 'new_m_a3_conv', 'new_m_a3_w_out', 'new_m_ln2_3', 'new_m_ffn3_w_gu', 'new_m_ffn3_w_down', 'new_m_ln_f', 'new_v_ln1_0', 'new_v_a0_w_in', 'new_v_a0_conv', 'new_v_a0_w_out', 'new_v_ln2_0', 'new_v_ffn0_w_gu', 'new_v_ffn0_w_down', 'new_v_ln1_1', 'new_v_b1_w_grp', 'new_v_b1_scale', 'new_v_ln2_1', 'new_v_ffn1_w_gu', 'new_v_ffn1_w_down', 'new_v_ln1_2', 'new_v_c2_w_pw1', 'new_v_c2_b_pw1', 'new_v_c2_dw', 'new_v_c2_b_dw', 'new_v_c2_ln_g', 'new_v_c2_ln_b', 'new_v_c2_w_pw2', 'new_v_c2_b_pw2', 'new_v_ln2_2', 'new_v_ffn2_w_gu', 'new_v_ffn2_w_down', 'new_v_ln1_3', 'new_v_a3_w_in', 'new_v_a3_conv', 'new_v_a3_w_out', 'new_v_ln2_3', 'new_v_ffn3_w_gu', 'new_v_ffn3_w_down', 'new_v_ln_f']
TWIN_LEAF_KINDS = {'loss': 'loss', 'grad_x': 'grad_x', 'grad_ln1_0': 'grad_w', 'grad_a0_w_in': 'grad_w', 'grad_a0_conv': 'grad_w', 'grad_a0_w_out': 'grad_w', 'grad_ln2_0': 'grad_w', 'grad_ffn0_w_gu': 'grad_w', 'grad_ffn0_w_down': 'grad_w', 'grad_ln1_1': 'grad_w', 'grad_b1_w_grp': 'grad_w', 'grad_b1_scale': 'grad_w', 'grad_ln2_1': 'grad_w', 'grad_ffn1_w_gu': 'grad_w', 'grad_ffn1_w_down': 'grad_w', 'grad_ln1_2': 'grad_w', 'grad_c2_w_pw1': 'grad_w', 'grad_c2_b_pw1': 'grad_w', 'grad_c2_dw': 'grad_w', 'grad_c2_b_dw': 'grad_w', 'grad_c2_ln_g': 'grad_w', 'grad_c2_ln_b': 'grad_w', 'grad_c2_w_pw2': 'grad_w', 'grad_c2_b_pw2': 'grad_w', 'grad_ln2_2': 'grad_w', 'grad_ffn2_w_gu': 'grad_w', 'grad_ffn2_w_down': 'grad_w', 'grad_ln1_3': 'grad_w', 'grad_a3_w_in': 'grad_w', 'grad_a3_conv': 'grad_w', 'grad_a3_w_out': 'grad_w', 'grad_ln2_3': 'grad_w', 'grad_ffn3_w_gu': 'grad_w', 'grad_ffn3_w_down': 'grad_w', 'grad_ln_f': 'grad_w', 'delta_ln1_0': 'delta_w', 'delta_a0_w_in': 'delta_w', 'delta_a0_conv': 'delta_w', 'delta_a0_w_out': 'delta_w', 'delta_ln2_0': 'delta_w', 'delta_ffn0_w_gu': 'delta_w', 'delta_ffn0_w_down': 'delta_w', 'delta_ln1_1': 'delta_w', 'delta_b1_w_grp': 'delta_w', 'delta_b1_scale': 'delta_w', 'delta_ln2_1': 'delta_w', 'delta_ffn1_w_gu': 'delta_w', 'delta_ffn1_w_down': 'delta_w', 'delta_ln1_2': 'delta_w', 'delta_c2_w_pw1': 'delta_w', 'delta_c2_b_pw1': 'delta_w', 'delta_c2_dw': 'delta_w', 'delta_c2_b_dw': 'delta_w', 'delta_c2_ln_g': 'delta_w', 'delta_c2_ln_b': 'delta_w', 'delta_c2_w_pw2': 'delta_w', 'delta_c2_b_pw2': 'delta_w', 'delta_ln2_2': 'delta_w', 'delta_ffn2_w_gu': 'delta_w', 'delta_ffn2_w_down': 'delta_w', 'delta_ln1_3': 'delta_w', 'delta_a3_w_in': 'delta_w', 'delta_a3_conv': 'delta_w', 'delta_a3_w_out': 'delta_w', 'delta_ln2_3': 'delta_w', 'delta_ffn3_w_gu': 'delta_w', 'delta_ffn3_w_down': 'delta_w', 'delta_ln_f': 'delta_w', 'new_m_ln1_0': 'new_m', 'new_m_a0_w_in': 'new_m', 'new_m_a0_conv': 'new_m', 'new_m_a0_w_out': 'new_m', 'new_m_ln2_0': 'new_m', 'new_m_ffn0_w_gu': 'new_m', 'new_m_ffn0_w_down': 'new_m', 'new_m_ln1_1': 'new_m', 'new_m_b1_w_grp': 'new_m', 'new_m_b1_scale': 'new_m', 'new_m_ln2_1': 'new_m', 'new_m_ffn1_w_gu': 'new_m', 'new_m_ffn1_w_down': 'new_m', 'new_m_ln1_2': 'new_m', 'new_m_c2_w_pw1': 'new_m', 'new_m_c2_b_pw1': 'new_m', 'new_m_c2_dw': 'new_m', 'new_m_c2_b_dw': 'new_m', 'new_m_c2_ln_g': 'new_m', 'new_m_c2_ln_b': 'new_m', 'new_m_c2_w_pw2': 'new_m', 'new_m_c2_b_pw2': 'new_m', 'new_m_ln2_2': 'new_m', 'new_m_ffn2_w_gu': 'new_m', 'new_m_ffn2_w_down': 'new_m', 'new_m_ln1_3': 'new_m', 'new_m_a3_w_in': 'new_m', 'new_m_a3_conv': 'new_m', 'new_m_a3_w_out': 'new_m', 'new_m_ln2_3': 'new_m', 'new_m_ffn3_w_gu': 'new_m', 'new_m_ffn3_w_down': 'new_m', 'new_m_ln_f': 'new_m', 'new_v_ln1_0': 'new_v', 'new_v_a0_w_in': 'new_v', 'new_v_a0_conv': 'new_v', 'new_v_a0_w_out': 'new_v', 'new_v_ln2_0': 'new_v', 'new_v_ffn0_w_gu': 'new_v', 'new_v_ffn0_w_down': 'new_v', 'new_v_ln1_1': 'new_v', 'new_v_b1_w_grp': 'new_v', 'new_v_b1_scale': 'new_v', 'new_v_ln2_1': 'new_v', 'new_v_ffn1_w_gu': 'new_v', 'new_v_ffn1_w_down': 'new_v', 'new_v_ln1_2': 'new_v', 'new_v_c2_w_pw1': 'new_v', 'new_v_c2_b_pw1': 'new_v', 'new_v_c2_dw': 'new_v', 'new_v_c2_b_dw': 'new_v', 'new_v_c2_ln_g': 'new_v', 'new_v_c2_ln_b': 'new_v', 'new_v_c2_w_pw2': 'new_v', 'new_v_c2_b_pw2': 'new_v', 'new_v_ln2_2': 'new_v', 'new_v_ffn2_w_gu': 'new_v', 'new_v_ffn2_w_down': 'new_v', 'new_v_ln1_3': 'new_v', 'new_v_a3_w_in': 'new_v', 'new_v_a3_conv': 'new_v', 'new_v_a3_w_out': 'new_v', 'new_v_ln2_3': 'new_v', 'new_v_ffn3_w_gu': 'new_v', 'new_v_ffn3_w_down': 'new_v', 'new_v_ln_f': 'new_v'}


def _forward(args):
    return _fwd_reference(*[args[k] for k in FWD_PARAMS])


def _output_shape():
    def fwd():
        inp = _fwd_setup_inputs(0)
        return _fwd_reference(*[inp[k] for k in FWD_PARAMS])
    out = _jax.eval_shape(fwd)
    return out.shape, out.dtype

N_MICROBATCH = 1
ADAM_LR = 0.001
ADAM_B1 = 0.9
ADAM_B2 = 0.999
ADAM_EPS = 1e-08
ADAM_WD = 0.01
ADAM_STEP = 10
PER_EXAMPLE_BATCH_AXIS = {'x': 0, 'loss_target': 0}
SHARED_INPUTS = []
_WEIGHT_DTYPES = {'ln1_0': _jnp.float32, 'a0_w_in': _jnp.float32, 'a0_conv': _jnp.float32, 'a0_w_out': _jnp.float32, 'ln2_0': _jnp.float32, 'ffn0_w_gu': _jnp.float32, 'ffn0_w_down': _jnp.float32, 'ln1_1': _jnp.float32, 'b1_w_grp': _jnp.float32, 'b1_scale': _jnp.float32, 'ln2_1': _jnp.float32, 'ffn1_w_gu': _jnp.float32, 'ffn1_w_down': _jnp.float32, 'ln1_2': _jnp.float32, 'c2_w_pw1': _jnp.float32, 'c2_b_pw1': _jnp.float32, 'c2_dw': _jnp.float32, 'c2_b_dw': _jnp.float32, 'c2_ln_g': _jnp.float32, 'c2_ln_b': _jnp.float32, 'c2_w_pw2': _jnp.float32, 'c2_b_pw2': _jnp.float32, 'ln2_2': _jnp.float32, 'ffn2_w_gu': _jnp.float32, 'ffn2_w_down': _jnp.float32, 'ln1_3': _jnp.float32, 'a3_w_in': _jnp.float32, 'a3_conv': _jnp.float32, 'a3_w_out': _jnp.float32, 'ln2_3': _jnp.float32, 'ffn3_w_gu': _jnp.float32, 'ffn3_w_down': _jnp.float32, 'ln_f': _jnp.float32}
MOMENT_SCALE = {'ln1_0': 3.195674e-01, 'a0_w_in': 1.831579e-01, 'a0_conv': 1.849653e-01, 'a0_w_out': 1.838954e-01, 'ln2_0': 1.437257e-01, 'ffn0_w_gu': 5.827211e-02, 'ffn0_w_down': 9.506336e-02, 'ln1_1': 1.305592e-01, 'b1_w_grp': 1.263849e-01, 'b1_scale': 3.854357e-01, 'ln2_1': 1.079313e-01, 'ffn1_w_gu': 4.531695e-02, 'ffn1_w_down': 7.403233e-02, 'ln1_2': 8.004344e-02, 'c2_w_pw1': 5.463781e-02, 'c2_b_pw1': 7.365627e-02, 'c2_dw': 7.187236e-02, 'c2_b_dw': 1.563569e-01, 'c2_ln_g': 8.671286e-02, 'c2_ln_b': 8.214776e-02, 'c2_w_pw2': 7.244729e-02, 'c2_b_pw2': 1.518841e-01, 'ln2_2': 9.881677e-02, 'ffn2_w_gu': 3.883515e-02, 'ffn2_w_down': 6.363324e-02, 'ln1_3': 1.417888e-01, 'a3_w_in': 8.129740e-02, 'a3_conv': 8.463672e-02, 'a3_w_out': 8.296133e-02, 'ln2_3': 6.543103e-02, 'ffn3_w_gu': 2.780783e-02, 'ffn3_w_down': 4.566320e-02, 'ln_f': 3.205508e+01}


def _to_microbatches(a, axis):
    t = _jnp.moveaxis(a, axis, 0)
    t = t.reshape((N_MICROBATCH, t.shape[0] // N_MICROBATCH) + t.shape[1:])
    return _jnp.moveaxis(t, 1, axis + 1)


def setup_inputs(seed: int = 0) -> dict:
    inp = _fwd_setup_inputs(seed)
    key = _jax.random.fold_in(_jax.random.key(seed), 7919)
    shape, _ = _output_shape()
    out = dict(inp)
    out["loss_target"] = _jax.random.normal(_jax.random.fold_in(key, 0), shape, _jnp.float32)
    for i, name in enumerate(TWIN_WEIGHTS):
        w = inp[name].astype(_jnp.float32)
        if MOMENT_SCALE is None:
            s = _jnp.sqrt(_jnp.mean(_jnp.square(w)) + 1e-30)
        else:
            s = MOMENT_SCALE[name]
        km, kv = _jax.random.split(_jax.random.fold_in(key, i + 1))
        out[name] = w
        out["m_" + name] = s * _jax.random.normal(km, w.shape, _jnp.float32)
        out["v_" + name] = (s * s) * _jax.random.uniform(kv, w.shape, _jnp.float32, 0.5, 1.5)
    if N_MICROBATCH > 1:
        for name, axis in PER_EXAMPLE_BATCH_AXIS.items():
            out[name] = _to_microbatches(out[name], axis)
    return {'x': out['x'], 'ln1_0': out['ln1_0'], 'a0_w_in': out['a0_w_in'], 'a0_conv': out['a0_conv'], 'a0_w_out': out['a0_w_out'], 'ln2_0': out['ln2_0'], 'ffn0_w_gu': out['ffn0_w_gu'], 'ffn0_w_down': out['ffn0_w_down'], 'ln1_1': out['ln1_1'], 'b1_w_grp': out['b1_w_grp'], 'b1_scale': out['b1_scale'], 'ln2_1': out['ln2_1'], 'ffn1_w_gu': out['ffn1_w_gu'], 'ffn1_w_down': out['ffn1_w_down'], 'ln1_2': out['ln1_2'], 'c2_w_pw1': out['c2_w_pw1'], 'c2_b_pw1': out['c2_b_pw1'], 'c2_dw': out['c2_dw'], 'c2_b_dw': out['c2_b_dw'], 'c2_ln_g': out['c2_ln_g'], 'c2_ln_b': out['c2_ln_b'], 'c2_w_pw2': out['c2_w_pw2'], 'c2_b_pw2': out['c2_b_pw2'], 'ln2_2': out['ln2_2'], 'ffn2_w_gu': out['ffn2_w_gu'], 'ffn2_w_down': out['ffn2_w_down'], 'ln1_3': out['ln1_3'], 'a3_w_in': out['a3_w_in'], 'a3_conv': out['a3_conv'], 'a3_w_out': out['a3_w_out'], 'ln2_3': out['ln2_3'], 'ffn3_w_gu': out['ffn3_w_gu'], 'ffn3_w_down': out['ffn3_w_down'], 'ln_f': out['ln_f'], 'loss_target': out['loss_target'], 'm_ln1_0': out['m_ln1_0'], 'm_a0_w_in': out['m_a0_w_in'], 'm_a0_conv': out['m_a0_conv'], 'm_a0_w_out': out['m_a0_w_out'], 'm_ln2_0': out['m_ln2_0'], 'm_ffn0_w_gu': out['m_ffn0_w_gu'], 'm_ffn0_w_down': out['m_ffn0_w_down'], 'm_ln1_1': out['m_ln1_1'], 'm_b1_w_grp': out['m_b1_w_grp'], 'm_b1_scale': out['m_b1_scale'], 'm_ln2_1': out['m_ln2_1'], 'm_ffn1_w_gu': out['m_ffn1_w_gu'], 'm_ffn1_w_down': out['m_ffn1_w_down'], 'm_ln1_2': out['m_ln1_2'], 'm_c2_w_pw1': out['m_c2_w_pw1'], 'm_c2_b_pw1': out['m_c2_b_pw1'], 'm_c2_dw': out['m_c2_dw'], 'm_c2_b_dw': out['m_c2_b_dw'], 'm_c2_ln_g': out['m_c2_ln_g'], 'm_c2_ln_b': out['m_c2_ln_b'], 'm_c2_w_pw2': out['m_c2_w_pw2'], 'm_c2_b_pw2': out['m_c2_b_pw2'], 'm_ln2_2': out['m_ln2_2'], 'm_ffn2_w_gu': out['m_ffn2_w_gu'], 'm_ffn2_w_down': out['m_ffn2_w_down'], 'm_ln1_3': out['m_ln1_3'], 'm_a3_w_in': out['m_a3_w_in'], 'm_a3_conv': out['m_a3_conv'], 'm_a3_w_out': out['m_a3_w_out'], 'm_ln2_3': out['m_ln2_3'], 'm_ffn3_w_gu': out['m_ffn3_w_gu'], 'm_ffn3_w_down': out['m_ffn3_w_down'], 'm_ln_f': out['m_ln_f'], 'v_ln1_0': out['v_ln1_0'], 'v_a0_w_in': out['v_a0_w_in'], 'v_a0_conv': out['v_a0_conv'], 'v_a0_w_out': out['v_a0_w_out'], 'v_ln2_0': out['v_ln2_0'], 'v_ffn0_w_gu': out['v_ffn0_w_gu'], 'v_ffn0_w_down': out['v_ffn0_w_down'], 'v_ln1_1': out['v_ln1_1'], 'v_b1_w_grp': out['v_b1_w_grp'], 'v_b1_scale': out['v_b1_scale'], 'v_ln2_1': out['v_ln2_1'], 'v_ffn1_w_gu': out['v_ffn1_w_gu'], 'v_ffn1_w_down': out['v_ffn1_w_down'], 'v_ln1_2': out['v_ln1_2'], 'v_c2_w_pw1': out['v_c2_w_pw1'], 'v_c2_b_pw1': out['v_c2_b_pw1'], 'v_c2_dw': out['v_c2_dw'], 'v_c2_b_dw': out['v_c2_b_dw'], 'v_c2_ln_g': out['v_c2_ln_g'], 'v_c2_ln_b': out['v_c2_ln_b'], 'v_c2_w_pw2': out['v_c2_w_pw2'], 'v_c2_b_pw2': out['v_c2_b_pw2'], 'v_ln2_2': out['v_ln2_2'], 'v_ffn2_w_gu': out['v_ffn2_w_gu'], 'v_ffn2_w_down': out['v_ffn2_w_down'], 'v_ln1_3': out['v_ln1_3'], 'v_a3_w_in': out['v_a3_w_in'], 'v_a3_conv': out['v_a3_conv'], 'v_a3_w_out': out['v_a3_w_out'], 'v_ln2_3': out['v_ln2_3'], 'v_ffn3_w_gu': out['v_ffn3_w_gu'], 'v_ffn3_w_down': out['v_ffn3_w_down'], 'v_ln_f': out['v_ln_f']}


def _loss(weights, diff, rest, loss_target):
    with _jax.named_scope("forward"):
        args = {**rest, TWIN_DIFF_INPUT: diff, **{k: w.astype(_WEIGHT_DTYPES[k]) for k, w in weights.items()}}
        y = _forward(args)
    with _jax.named_scope("loss_head"):
        err = _jnp.square(y.astype(_jnp.float32) - loss_target)
        return 0.5 * _jnp.sum(_jnp.mean(err, axis=-1)) if err.ndim else 0.5 * err


def _adamw(w, g, m, v):
    m = ADAM_B1 * m + (1.0 - ADAM_B1) * g
    v = ADAM_B2 * v + (1.0 - ADAM_B2) * _jnp.square(g)
    m_hat = m / (1.0 - ADAM_B1 ** ADAM_STEP)
    v_hat = v / (1.0 - ADAM_B2 ** ADAM_STEP)
    delta = -ADAM_LR * (m_hat / (_jnp.sqrt(v_hat) + ADAM_EPS) + ADAM_WD * w)
    return delta, m, v


def reference(x, ln1_0, a0_w_in, a0_conv, a0_w_out, ln2_0, ffn0_w_gu, ffn0_w_down, ln1_1, b1_w_grp, b1_scale, ln2_1, ffn1_w_gu, ffn1_w_down, ln1_2, c2_w_pw1, c2_b_pw1, c2_dw, c2_b_dw, c2_ln_g, c2_ln_b, c2_w_pw2, c2_b_pw2, ln2_2, ffn2_w_gu, ffn2_w_down, ln1_3, a3_w_in, a3_conv, a3_w_out, ln2_3, ffn3_w_gu, ffn3_w_down, ln_f, loss_target, m_ln1_0, m_a0_w_in, m_a0_conv, m_a0_w_out, m_ln2_0, m_ffn0_w_gu, m_ffn0_w_down, m_ln1_1, m_b1_w_grp, m_b1_scale, m_ln2_1, m_ffn1_w_gu, m_ffn1_w_down, m_ln1_2, m_c2_w_pw1, m_c2_b_pw1, m_c2_dw, m_c2_b_dw, m_c2_ln_g, m_c2_ln_b, m_c2_w_pw2, m_c2_b_pw2, m_ln2_2, m_ffn2_w_gu, m_ffn2_w_down, m_ln1_3, m_a3_w_in, m_a3_conv, m_a3_w_out, m_ln2_3, m_ffn3_w_gu, m_ffn3_w_down, m_ln_f, v_ln1_0, v_a0_w_in, v_a0_conv, v_a0_w_out, v_ln2_0, v_ffn0_w_gu, v_ffn0_w_down, v_ln1_1, v_b1_w_grp, v_b1_scale, v_ln2_1, v_ffn1_w_gu, v_ffn1_w_down, v_ln1_2, v_c2_w_pw1, v_c2_b_pw1, v_c2_dw, v_c2_b_dw, v_c2_ln_g, v_c2_ln_b, v_c2_w_pw2, v_c2_b_pw2, v_ln2_2, v_ffn2_w_gu, v_ffn2_w_down, v_ln1_3, v_a3_w_in, v_a3_conv, v_a3_w_out, v_ln2_3, v_ffn3_w_gu, v_ffn3_w_down, v_ln_f):
    given = dict(x=x, ln1_0=ln1_0, a0_w_in=a0_w_in, a0_conv=a0_conv, a0_w_out=a0_w_out, ln2_0=ln2_0, ffn0_w_gu=ffn0_w_gu, ffn0_w_down=ffn0_w_down, ln1_1=ln1_1, b1_w_grp=b1_w_grp, b1_scale=b1_scale, ln2_1=ln2_1, ffn1_w_gu=ffn1_w_gu, ffn1_w_down=ffn1_w_down, ln1_2=ln1_2, c2_w_pw1=c2_w_pw1, c2_b_pw1=c2_b_pw1, c2_dw=c2_dw, c2_b_dw=c2_b_dw, c2_ln_g=c2_ln_g, c2_ln_b=c2_ln_b, c2_w_pw2=c2_w_pw2, c2_b_pw2=c2_b_pw2, ln2_2=ln2_2, ffn2_w_gu=ffn2_w_gu, ffn2_w_down=ffn2_w_down, ln1_3=ln1_3, a3_w_in=a3_w_in, a3_conv=a3_conv, a3_w_out=a3_w_out, ln2_3=ln2_3, ffn3_w_gu=ffn3_w_gu, ffn3_w_down=ffn3_w_down, ln_f=ln_f, loss_target=loss_target, m_ln1_0=m_ln1_0, m_a0_w_in=m_a0_w_in, m_a0_conv=m_a0_conv, m_a0_w_out=m_a0_w_out, m_ln2_0=m_ln2_0, m_ffn0_w_gu=m_ffn0_w_gu, m_ffn0_w_down=m_ffn0_w_down, m_ln1_1=m_ln1_1, m_b1_w_grp=m_b1_w_grp, m_b1_scale=m_b1_scale, m_ln2_1=m_ln2_1, m_ffn1_w_gu=m_ffn1_w_gu, m_ffn1_w_down=m_ffn1_w_down, m_ln1_2=m_ln1_2, m_c2_w_pw1=m_c2_w_pw1, m_c2_b_pw1=m_c2_b_pw1, m_c2_dw=m_c2_dw, m_c2_b_dw=m_c2_b_dw, m_c2_ln_g=m_c2_ln_g, m_c2_ln_b=m_c2_ln_b, m_c2_w_pw2=m_c2_w_pw2, m_c2_b_pw2=m_c2_b_pw2, m_ln2_2=m_ln2_2, m_ffn2_w_gu=m_ffn2_w_gu, m_ffn2_w_down=m_ffn2_w_down, m_ln1_3=m_ln1_3, m_a3_w_in=m_a3_w_in, m_a3_conv=m_a3_conv, m_a3_w_out=m_a3_w_out, m_ln2_3=m_ln2_3, m_ffn3_w_gu=m_ffn3_w_gu, m_ffn3_w_down=m_ffn3_w_down, m_ln_f=m_ln_f, v_ln1_0=v_ln1_0, v_a0_w_in=v_a0_w_in, v_a0_conv=v_a0_conv, v_a0_w_out=v_a0_w_out, v_ln2_0=v_ln2_0, v_ffn0_w_gu=v_ffn0_w_gu, v_ffn0_w_down=v_ffn0_w_down, v_ln1_1=v_ln1_1, v_b1_w_grp=v_b1_w_grp, v_b1_scale=v_b1_scale, v_ln2_1=v_ln2_1, v_ffn1_w_gu=v_ffn1_w_gu, v_ffn1_w_down=v_ffn1_w_down, v_ln1_2=v_ln1_2, v_c2_w_pw1=v_c2_w_pw1, v_c2_b_pw1=v_c2_b_pw1, v_c2_dw=v_c2_dw, v_c2_b_dw=v_c2_b_dw, v_c2_ln_g=v_c2_ln_g, v_c2_ln_b=v_c2_ln_b, v_c2_w_pw2=v_c2_w_pw2, v_c2_b_pw2=v_c2_b_pw2, v_ln2_2=v_ln2_2, v_ffn2_w_gu=v_ffn2_w_gu, v_ffn2_w_down=v_ffn2_w_down, v_ln1_3=v_ln1_3, v_a3_w_in=v_a3_w_in, v_a3_conv=v_a3_conv, v_a3_w_out=v_a3_w_out, v_ln2_3=v_ln2_3, v_ffn3_w_gu=v_ffn3_w_gu, v_ffn3_w_down=v_ffn3_w_down, v_ln_f=v_ln_f)
    weights = {n: given[n] for n in TWIN_WEIGHTS}
    shared = {n: given[n] for n in SHARED_INPUTS}
    per_example = {n: given[n] for n in ['x']}
    grad_fn = _jax.value_and_grad(_loss, argnums=(0, 1))

    def one_microbatch(ex, loss_target):
        ex = dict(ex)
        diff = ex.pop(TWIN_DIFF_INPUT)
        return grad_fn(weights, diff, {**shared, **ex}, loss_target)

    if N_MICROBATCH == 1:
        loss, (grad_w, grad_x) = one_microbatch(per_example, given["loss_target"])
    else:
        def body(carry, xs):
            loss_sum, grad_sum = carry
            l_k, (gw_k, gx_k) = one_microbatch(xs[0], xs[1])
            with _jax.named_scope("update"):
                return (loss_sum + l_k, _jax.tree.map(_jnp.add, grad_sum, gw_k)), gx_k

        init = (_jnp.zeros((), _jnp.float32), _jax.tree.map(_jnp.zeros_like, weights))
        (loss, grad_w), grad_x = _jax.lax.scan(body, init, (per_example, given["loss_target"]))
    with _jax.named_scope("update"):
        delta_w, new_m, new_v = {}, {}, {}
        for n in TWIN_WEIGHTS:
            delta_w[n], new_m[n], new_v[n] = _adamw(weights[n], grad_w[n], given["m_" + n], given["v_" + n])
    return (loss, grad_x, *[grad_w[n] for n in TWIN_WEIGHTS], *[delta_w[n] for n in TWIN_WEIGHTS],
            *[new_m[n] for n in TWIN_WEIGHTS], *[new_v[n] for n in TWIN_WEIGHTS])
```

```python
import functools

import jax
import jax.numpy as jnp
from jax import lax
from jax.experimental import pallas as pl
from jax.experimental.pallas import tpu as pltpu

F32 = jnp.float32
BF16 = jnp.bfloat16

N_DEV = 8
SEQ = 4096
D = 1024
FF = 2816
FC = FF // 4
RMS_EPS = 1e-6
LN_EPS = 1e-5
TM = 512
HALO = 32
VMEM_LIMIT = 56 * 1024 * 1024

NT = (((1,), (1,)), ((), ()))
TN = (((0,), (0,)), ((), ()))


def _cp(*sem):
    return pltpu.CompilerParams(dimension_semantics=sem, vmem_limit_bytes=VMEM_LIMIT)


def _dot(a, b):
    return jnp.dot(a, b, preferred_element_type=F32)


def _dot_nt(a, b):
    return lax.dot_general(a, b, NT, preferred_element_type=F32)


def _dot_tn(a, b):
    return lax.dot_general(a, b, TN, preferred_element_type=F32)


def _rms_fwd(x, gain):
    r = lax.rsqrt(jnp.mean(x * x, axis=-1, keepdims=True) + RMS_EPS)
    return x * r * gain


def _rms_bwd(x, gain, du):
    r = lax.rsqrt(jnp.mean(x * x, axis=-1, keepdims=True) + RMS_EPS)
    xhat = x * r
    dgain = jnp.sum(du * xhat, axis=0, keepdims=True)
    dxhat = du * gain
    dx = r * (dxhat - xhat * jnp.mean(dxhat * xhat, axis=-1, keepdims=True))
    return dx, dgain


def ffn_fwd(h, gain, wgu, wd):
    t = h.shape[0]

    def body(h_ref, g_ref, wgu_ref, wd_ref, hn_ref, z_ref, u_ref, acc):
        k = pl.program_id(1)

        @pl.when(k == 0)
        def _():
            u_ref[...] = _rms_fwd(h_ref[...], g_ref[...]).astype(BF16)
            acc[...] = jnp.zeros_like(acc)

        u = u_ref[...]
        g = _dot(u, wgu_ref[0, 0])
        up = _dot(u, wgu_ref[1, 0])
        z_ref[0, 0] = g.astype(BF16)
        z_ref[1, 0] = up.astype(BF16)
        a = g * jax.nn.sigmoid(g) * up
        acc[...] += _dot(a.astype(BF16), wd_ref[0])

        @pl.when(k == 3)
        def _():
            hn_ref[...] = h_ref[...] + acc[...]

    return pl.pallas_call(
        body, name="ffn_fwd", grid=(t // TM, 4),
        in_specs=[pl.BlockSpec((TM, D), lambda i, k: (i, 0)),
                  pl.BlockSpec((1, D), lambda i, k: (0, 0)),
                  pl.BlockSpec((2, 1, D, FC), lambda i, k: (0, k, 0, 0)),
                  pl.BlockSpec((1, FC, D), lambda i, k: (k, 0, 0))],
        out_specs=[pl.BlockSpec((TM, D), lambda i, k: (i, 0)),
                   pl.BlockSpec((2, 1, TM, FC), lambda i, k: (0, k, i, 0)),
                   pl.BlockSpec((TM, D), lambda i, k: (i, 0))],
        out_shape=[jax.ShapeDtypeStruct((t, D), F32),
                   jax.ShapeDtypeStruct((2, 4, t, FC), BF16),
                   jax.ShapeDtypeStruct((t, D), BF16)],
        scratch_shapes=[pltpu.VMEM((TM, D), F32)],
        compiler_params=_cp("arbitrary", "arbitrary"),
    )(h, gain, wgu, wd)


def ffn_bwd_x(dh, h, gain, z, wgu, wd):
    t = h.shape[0]

    def body(dh_ref, h_ref, g_ref, z_ref, wgu_ref, wd_ref, dhp_ref, dz_ref, a_ref, dgain_ref, dhb, du):
        i, k = pl.program_id(0), pl.program_id(1)

        @pl.when(k == 0)
        def _():
            dhb[...] = dh_ref[...].astype(BF16)
            du[...] = jnp.zeros_like(du)

        @pl.when((k == 0) & (i == 0))
        def _():
            dgain_ref[...] = jnp.zeros_like(dgain_ref)

        da = _dot_nt(dhb[...], wd_ref[0])
        g = z_ref[0, 0].astype(F32)
        up = z_ref[1, 0].astype(F32)
        sg = jax.nn.sigmoid(g)
        silu = g * sg
        a_ref[0] = (silu * up).astype(BF16)
        dg = (da * up * (sg * (1.0 + g * (1.0 - sg)))).astype(BF16)
        dup = (da * silu).astype(BF16)
        dz_ref[0, 0] = dg
        dz_ref[1, 0] = dup
        du[...] += _dot_nt(dg, wgu_ref[0, 0]) + _dot_nt(dup, wgu_ref[1, 0])

        @pl.when(k == 3)
        def _():
            dx, dgain = _rms_bwd(h_ref[...], g_ref[...], du[...])
            dhp_ref[...] = dh_ref[...] + dx
            dgain_ref[...] += dgain

    return pl.pallas_call(
        body, name="ffn_bwd_x", grid=(t // TM, 4),
        in_specs=[pl.BlockSpec((TM, D), lambda i, k: (i, 0)),
                  pl.BlockSpec((TM, D), lambda i, k: (i, 0)),
                  pl.BlockSpec((1, D), lambda i, k: (0, 0)),
                  pl.BlockSpec((2, 1, TM, FC), lambda i, k: (0, k, i, 0)),
                  pl.BlockSpec((2, 1, D, FC), lambda i, k: (0, k, 0, 0)),
                  pl.BlockSpec((1, FC, D), lambda i, k: (k, 0, 0))],
        out_specs=[pl.BlockSpec((TM, D), lambda i, k: (i, 0)),
                   pl.BlockSpec((2, 1, TM, FC), lambda i, k: (0, k, i, 0)),
                   pl.BlockSpec((1, TM, FC), lambda i, k: (k, i, 0)),
                   pl.BlockSpec((1, D), lambda i, k: (0, 0))],
        out_shape=[jax.ShapeDtypeStruct((t, D), F32),
                   jax.ShapeDtypeStruct((2, 4, t, FC), BF16),
                   jax.ShapeDtypeStruct((4, t, FC), BF16),
                   jax.ShapeDtypeStruct((1, D), F32)],
        scratch_shapes=[pltpu.VMEM((TM, D), BF16), pltpu.VMEM((TM, D), F32)],
        compiler_params=_cp("arbitrary", "arbitrary"),
    )(dh, h, gain, z, wgu, wd)


def ffn_bwd_w(u, dz, a, dh):
    t = u.shape[0]

    def body(u_ref, dz_ref, a_ref, dh_ref, dwgu_ref, dwd_ref):
        @pl.when(pl.program_id(1) == 0)
        def _():
            dwgu_ref[...] = jnp.zeros_like(dwgu_ref)
            dwd_ref[...] = jnp.zeros_like(dwd_ref)

        ub = u_ref[...]
        dwgu_ref[0, 0] += _dot_tn(ub, dz_ref[0, 0])
        dwgu_ref[1, 0] += _dot_tn(ub, dz_ref[1, 0])
        dwd_ref[0] += _dot_tn(a_ref[0], dh_ref[...].astype(BF16))

    return pl.pallas_call(
        body, name="ffn_bwd_w", grid=(4, t // TM),
        in_specs=[pl.BlockSpec((TM, D), lambda k, j: (j, 0)),
                  pl.BlockSpec((2, 1, TM, FC), lambda k, j: (0, k, j, 0)),
                  pl.BlockSpec((1, TM, FC), lambda k, j: (k, j, 0)),
                  pl.BlockSpec((TM, D), lambda k, j: (j, 0))],
        out_specs=[pl.BlockSpec((2, 1, D, FC), lambda k, j: (0, k, 0, 0)),
                   pl.BlockSpec((1, FC, D), lambda k, j: (k, 0, 0))],
        out_shape=[jax.ShapeDtypeStruct((2, 4, D, FC), F32),
                   jax.ShapeDtypeStruct((4, FC, D), F32)],
        compiler_params=_cp("arbitrary", "arbitrary"),
    )(u, dz, a, dh)


def _prev_halo(i):
    return jnp.maximum(i * (TM // HALO) - 1, 0)


def _next_halo(i, t):
    return jnp.minimum((i + 1) * (TM // HALO), t // HALO - 1)


def rms_matmul(h, gain, w, bias):
    t = h.shape[0]
    nc = w.shape[2]

    def body(h_ref, g_ref, w_ref, b_ref, z_ref, u_ref):
        @pl.when(pl.program_id(1) == 0)
        def _():
            u_ref[...] = _rms_fwd(h_ref[...], g_ref[...]).astype(BF16)

        z_ref[...] = (_dot(u_ref[...], w_ref[0]) + b_ref[...]).astype(BF16)

    return pl.pallas_call(
        body, name=f"rms_matmul_{nc}", grid=(t // TM, N_DEV),
        in_specs=[pl.BlockSpec((TM, D), lambda i, j: (i, 0)),
                  pl.BlockSpec((1, D), lambda i, j: (0, 0)),
                  pl.BlockSpec((1, D, nc), lambda i, j: (j, 0, 0)),
                  pl.BlockSpec((1, nc), lambda i, j: (0, j))],
        out_specs=[pl.BlockSpec((TM, nc), lambda i, j: (i, j)),
                   pl.BlockSpec((TM, D), lambda i, j: (i, 0))],
        out_shape=[jax.ShapeDtypeStruct((t, N_DEV * nc), BF16),
                   jax.ShapeDtypeStruct((t, D), BF16)],
        compiler_params=_cp("arbitrary", "arbitrary"),
    )(h, gain, w, bias)


def in_proj_bwd_x(dz, w, h, gain, dh):
    t = h.shape[0]
    nc = w.shape[2]

    def body(dz_ref, w_ref, h_ref, g_ref, dh_ref, dhp_ref, dgain_ref, dbias_ref):
        @pl.when(pl.program_id(0) == 0)
        def _():
            dgain_ref[...] = jnp.zeros_like(dgain_ref)
            dbias_ref[...] = jnp.zeros_like(dbias_ref)

        du = jnp.zeros((TM, D), F32)
        for j in range(N_DEV):
            du += _dot_nt(dz_ref[:, j * nc:(j + 1) * nc], w_ref[j])
        dx, dgain = _rms_bwd(h_ref[...], g_ref[...], du)
        dhp_ref[...] = dh_ref[...] + dx
        dgain_ref[...] += dgain
        dbias_ref[...] += jnp.sum(dz_ref[...].astype(F32), axis=0, keepdims=True)

    return pl.pallas_call(
        body, name=f"in_proj_bwd_x_{nc}", grid=(t // TM,),
        in_specs=[pl.BlockSpec((TM, N_DEV * nc), lambda i: (i, 0)),
                  pl.BlockSpec((N_DEV, D, nc), lambda i: (0, 0, 0)),
                  pl.BlockSpec((TM, D), lambda i: (i, 0)),
                  pl.BlockSpec((1, D), lambda i: (0, 0)),
                  pl.BlockSpec((TM, D), lambda i: (i, 0))],
        out_specs=[pl.BlockSpec((TM, D), lambda i: (i, 0)),
                   pl.BlockSpec((1, D), lambda i: (0, 0)),
                   pl.BlockSpec((1, N_DEV * nc), lambda i: (0, 0))],
        out_shape=[jax.ShapeDtypeStruct((t, D), F32),
                   jax.ShapeDtypeStruct((1, D), F32),
                   jax.ShapeDtypeStruct((1, N_DEV * nc), F32)],
        compiler_params=_cp("arbitrary"),
    )(dz, w, h, gain, dh)


def in_proj_bwd_w(u, dz, nc):
    t = u.shape[0]

    def body(u_ref, dz_ref, dw_ref):
        @pl.when(pl.program_id(1) == 0)
        def _():
            dw_ref[...] = jnp.zeros_like(dw_ref)

        dw_ref[0] += _dot_tn(u_ref[...], dz_ref[...])

    return pl.pallas_call(
        body, name=f"in_proj_bwd_w_{nc}", grid=(N_DEV, t // TM),
        in_specs=[pl.BlockSpec((TM, D), lambda j, s: (s, 0)),
                  pl.BlockSpec((TM, nc), lambda j, s: (s, j))],
        out_specs=pl.BlockSpec((1, D, nc), lambda j, s: (j, 0, 0)),
        out_shape=jax.ShapeDtypeStruct((N_DEV, D, nc), F32),
        compiler_params=_cp("arbitrary", "arbitrary"),
    )(u, dz)


A_TAPS = 3


def a_mix_fwd(z, h, conv, wout):
    t = h.shape[0]

    def body(z_ref, zp_ref, h_ref, cw_ref, wo_ref, hn_ref, pad):
        i = pl.program_id(0)
        ph = zp_ref[:, D:2 * D].astype(F32) * zp_ref[:, 2 * D:].astype(F32)
        pad[0:HALO, :] = jnp.where(i == 0, 0.0, ph)
        pad[HALO:, :] = z_ref[:, D:2 * D].astype(F32) * z_ref[:, 2 * D:].astype(F32)
        q = jnp.zeros((TM, D), F32)
        for k in range(A_TAPS):
            off = HALO - (A_TAPS - 1) + k
            q += cw_ref[k:k + 1, :] * pad[off:off + TM, :]
        r = z_ref[:, 0:D].astype(F32) * q
        hn_ref[...] = h_ref[...] + _dot(r.astype(BF16), wo_ref[...])

    return pl.pallas_call(
        body, name="a_mix_fwd", grid=(t // TM,),
        in_specs=[pl.BlockSpec((TM, 3 * D), lambda i: (i, 0)),
                  pl.BlockSpec((HALO, 3 * D), lambda i: (_prev_halo(i), 0)),
                  pl.BlockSpec((TM, D), lambda i: (i, 0)),
                  pl.BlockSpec((A_TAPS, D), lambda i: (0, 0)),
                  pl.BlockSpec((D, D), lambda i: (0, 0))],
        out_specs=pl.BlockSpec((TM, D), lambda i: (i, 0)),
        out_shape=jax.ShapeDtypeStruct((t, D), F32),
        scratch_shapes=[pltpu.VMEM((HALO + TM, D), F32)],
        compiler_params=_cp("arbitrary"),
    )(z, z, h, conv, wout)


def a_mix_bwd(dh, z, conv, wout):
    t = dh.shape[0]

    def body(dh_ref, dhn_ref, z_ref, zp_ref, zn_ref, cw_ref, wo_ref, dz_ref, dwo_ref, dcw_ref, pad, dqpad):
        i = pl.program_id(0)
        last = i == pl.num_programs(0) - 1

        @pl.when(i == 0)
        def _():
            dwo_ref[...] = jnp.zeros_like(dwo_ref)
            dcw_ref[...] = jnp.zeros_like(dcw_ref)

        ph = zp_ref[:, D:2 * D].astype(F32) * zp_ref[:, 2 * D:].astype(F32)
        pad[0:HALO, :] = jnp.where(i == 0, 0.0, ph)
        c = z_ref[:, D:2 * D].astype(F32)
        v = z_ref[:, 2 * D:].astype(F32)
        pad[HALO:, :] = c * v
        q = jnp.zeros((TM, D), F32)
        for k in range(A_TAPS):
            off = HALO - (A_TAPS - 1) + k
            q += cw_ref[k:k + 1, :] * pad[off:off + TM, :]
        b = z_ref[:, 0:D].astype(F32)
        dhb = dh_ref[...].astype(BF16)
        dwo_ref[...] += _dot_tn((b * q).astype(BF16), dhb)
        dr = _dot_nt(dhb, wo_ref[...])
        dz_ref[:, 0:D] = (dr * q).astype(BF16)
        dq = dr * b
        drn = _dot_nt(dhn_ref[...].astype(BF16), wo_ref[...])
        dqpad[0:TM, :] = dq
        dqpad[TM:, :] = jnp.where(last, 0.0, drn * zn_ref[:, 0:D].astype(F32))
        dp = jnp.zeros((TM, D), F32)
        for k in range(A_TAPS):
            off = A_TAPS - 1 - k
            dp += cw_ref[k:k + 1, :] * dqpad[off:off + TM, :]
            poff = HALO - (A_TAPS - 1) + k
            dcw_ref[k:k + 1, :] += jnp.sum(dq * pad[poff:poff + TM, :], axis=0, keepdims=True)
        dz_ref[:, D:2 * D] = (dp * v).astype(BF16)
        dz_ref[:, 2 * D:] = (dp * c).astype(BF16)

    return pl.pallas_call(
        body, name="a_mix_bwd", grid=(t // TM,),
        in_specs=[pl.BlockSpec((TM, D), lambda i: (i, 0)),
                  pl.BlockSpec((HALO, D), lambda i: (_next_halo(i, t), 0)),
                  pl.BlockSpec((TM, 3 * D), lambda i: (i, 0)),
                  pl.BlockSpec((HALO, 3 * D), lambda i: (_prev_halo(i), 0)),
                  pl.BlockSpec((HALO, 3 * D), lambda i: (_next_halo(i, t), 0)),
                  pl.BlockSpec((A_TAPS, D), lambda i: (0, 0)),
                  pl.BlockSpec((D, D), lambda i: (0, 0))],
        out_specs=[pl.BlockSpec((TM, 3 * D), lambda i: (i, 0)),
                   pl.BlockSpec((D, D), lambda i: (0, 0)),
                   pl.BlockSpec((A_TAPS, D), lambda i: (0, 0))],
        out_shape=[jax.ShapeDtypeStruct((t, 3 * D), BF16),
                   jax.ShapeDtypeStruct((D, D), F32),
                   jax.ShapeDtypeStruct((A_TAPS, D), F32)],
        scratch_shapes=[pltpu.VMEM((HALO + TM, D), F32), pltpu.VMEM((TM + HALO, D), F32)],
        compiler_params=_cp("arbitrary"),
    )(dh, dh, z, z, z, conv, wout)


C_TAPS = 31


def _glu(zr):
    return zr[:, 0:D].astype(F32) * jax.nn.sigmoid(zr[:, D:].astype(F32))


def _ln_silu(h2, lg, lb):
    mu = jnp.mean(h2, axis=-1, keepdims=True)
    xc = h2 - mu
    rstd = lax.rsqrt(jnp.mean(xc * xc, axis=-1, keepdims=True) + LN_EPS)
    xn = xc * rstd
    h3 = xn * lg + lb
    s3 = jax.nn.sigmoid(h3)
    return xn, rstd, h3, s3


def _ln_silu_bwd(h2, lg, lb, dh4):
    xn, rstd, h3, s3 = _ln_silu(h2, lg, lb)
    dh3 = dh4 * (s3 * (1.0 + h3 * (1.0 - s3)))
    dxn = dh3 * lg
    dh2 = rstd * (dxn - jnp.mean(dxn, axis=-1, keepdims=True) - xn * jnp.mean(dxn * xn, axis=-1, keepdims=True))
    return dh2, dh3, xn, h3 * s3


def c_mix_fwd(z, h, dw, bdw, lg, lb, w2, b2):
    t = h.shape[0]

    def body(z_ref, zp_ref, h_ref, dw_ref, bdw_ref, lg_ref, lb_ref, w2_ref, b2_ref, hn_ref, h2_ref, pad):
        i = pl.program_id(0)
        pad[0:HALO, :] = jnp.where(i == 0, 0.0, _glu(zp_ref))
        pad[HALO:, :] = _glu(z_ref)
        h2 = jnp.zeros((TM, D), F32) + bdw_ref[...]
        for k in range(C_TAPS):
            off = HALO - (C_TAPS - 1) + k
            h2 += dw_ref[k:k + 1, :] * pad[off:off + TM, :]
        h2_ref[...] = h2
        _, _, h3, s3 = _ln_silu(h2, lg_ref[...], lb_ref[...])
        hn_ref[...] = h_ref[...] + _dot((h3 * s3).astype(BF16), w2_ref[...]) + b2_ref[...]

    vec = pl.BlockSpec((1, D), lambda i: (0, 0))
    return pl.pallas_call(
        body, name="c_mix_fwd", grid=(t // TM,),
        in_specs=[pl.BlockSpec((TM, 2 * D), lambda i: (i, 0)),
                  pl.BlockSpec((HALO, 2 * D), lambda i: (_prev_halo(i), 0)),
                  pl.BlockSpec((TM, D), lambda i: (i, 0)),
                  pl.BlockSpec((C_TAPS, D), lambda i: (0, 0)),
                  vec, vec, vec,
                  pl.BlockSpec((D, D), lambda i: (0, 0)),
                  vec],
        out_specs=[pl.BlockSpec((TM, D), lambda i: (i, 0)),
                   pl.BlockSpec((TM, D), lambda i: (i, 0))],
        out_shape=[jax.ShapeDtypeStruct((t, D), F32),
                   jax.ShapeDtypeStruct((t, D), F32)],
        scratch_shapes=[pltpu.VMEM((HALO + TM, D), F32)],
        compiler_params=_cp("arbitrary"),
    )(z, z, h, dw, bdw, lg, lb, w2, b2)


def c_mix_bwd(dh, z, h2, dw, lg, lb, w2):
    t = dh.shape[0]

    def body(dh_ref, dhn_ref, z_ref, zp_ref, h2_ref, h2n_ref, dw_ref, lg_ref, lb_ref, w2_ref,
             dz_ref, dw2_ref, db2_ref, dlg_ref, dlb_ref, dbdw_ref, ddw_ref, pad, dpad):
        i = pl.program_id(0)
        last = i == pl.num_programs(0) - 1

        @pl.when(i == 0)
        def _():
            for r in (dw2_ref, db2_ref, dlg_ref, dlb_ref, dbdw_ref, ddw_ref):
                r[...] = jnp.zeros_like(r)

        lg, lb = lg_ref[...], lb_ref[...]
        dh = dh_ref[...]
        dhb = dh.astype(BF16)
        dh2, dh3, xn, h4 = _ln_silu_bwd(h2_ref[...], lg, lb, _dot_nt(dhb, w2_ref[...]))
        dw2_ref[...] += _dot_tn(h4.astype(BF16), dhb)
        db2_ref[...] += jnp.sum(dh, axis=0, keepdims=True)
        dlg_ref[...] += jnp.sum(dh3 * xn, axis=0, keepdims=True)
        dlb_ref[...] += jnp.sum(dh3, axis=0, keepdims=True)
        dbdw_ref[...] += jnp.sum(dh2, axis=0, keepdims=True)
        dh2n, _, _, _ = _ln_silu_bwd(h2n_ref[...], lg, lb, _dot_nt(dhn_ref[...].astype(BF16), w2_ref[...]))
        dpad[0:TM, :] = dh2
        dpad[TM:, :] = jnp.where(last, 0.0, dh2n)
        pad[0:HALO, :] = jnp.where(i == 0, 0.0, _glu(zp_ref))
        pad[HALO:, :] = _glu(z_ref)
        dh1 = jnp.zeros((TM, D), F32)
        for k in range(C_TAPS):
            off = C_TAPS - 1 - k
            dh1 += dw_ref[k:k + 1, :] * dpad[off:off + TM, :]
            poff = HALO - (C_TAPS - 1) + k
            ddw_ref[k:k + 1, :] += jnp.sum(dh2 * pad[poff:poff + TM, :], axis=0, keepdims=True)
        a = z_ref[:, 0:D].astype(F32)
        sg = jax.nn.sigmoid(z_ref[:, D:].astype(F32))
        dz_ref[:, 0:D] = (dh1 * sg).astype(BF16)
        dz_ref[:, D:] = (dh1 * a * sg * (1.0 - sg)).astype(BF16)

    vec = pl.BlockSpec((1, D), lambda i: (0, 0))
    return pl.pallas_call(
        body, name="c_mix_bwd", grid=(t // TM,),
        in_specs=[pl.BlockSpec((TM, D), lambda i: (i, 0)),
                  pl.BlockSpec((HALO, D), lambda i: (_next_halo(i, t), 0)),
                  pl.BlockSpec((TM, 2 * D), lambda i: (i, 0)),
                  pl.BlockSpec((HALO, 2 * D), lambda i: (_prev_halo(i), 0)),
                  pl.BlockSpec((TM, D), lambda i: (i, 0)),
                  pl.BlockSpec((HALO, D), lambda i: (_next_halo(i, t), 0)),
                  pl.BlockSpec((C_TAPS, D), lambda i: (0, 0)),
                  vec, vec,
                  pl.BlockSpec((D, D), lambda i: (0, 0))],
        out_specs=[pl.BlockSpec((TM, 2 * D), lambda i: (i, 0)),
                   pl.BlockSpec((D, D), lambda i: (0, 0)),
                   vec, vec, vec, vec,
                   pl.BlockSpec((C_TAPS, D), lambda i: (0, 0))],
        out_shape=[jax.ShapeDtypeStruct((t, 2 * D), BF16),
                   jax.ShapeDtypeStruct((D, D), F32)]
                  + [jax.ShapeDtypeStruct((1, D), F32)] * 4
                  + [jax.ShapeDtypeStruct((C_TAPS, D), F32)],
        scratch_shapes=[pltpu.VMEM((HALO + TM, D), F32), pltpu.VMEM((TM + HALO, D), F32)],
        compiler_params=_cp("arbitrary"),
    )(dh, dh, z, z, h2, h2, dw, lg, lb, w2)


POOL_WINDOWS = (2, 4, 8, 16)
GW = D // len(POOL_WINDOWS)


def _pool_mixed(pad, g, w, inv_cnt):
    cols = slice(g * GW, (g + 1) * GW)
    s = pad[HALO:HALO + TM, cols]
    u = s
    for j in range(1, w):
        s = s + pad[HALO - j:HALO - j + TM, cols]
    return s * inv_cnt - u


def _inv_cnt(i, w):
    row = i * TM + lax.broadcasted_iota(jnp.int32, (TM, 1), 0)
    return 1.0 / jnp.minimum(row + 1, w).astype(F32)


def b_mix_fwd(h, gain, wg, scale):
    t = h.shape[0]

    def body(h_ref, hp_ref, g_ref, wg_ref, sc_ref, hn_ref, pad):
        i = pl.program_id(0)
        gain = g_ref[...]
        pad[0:HALO, :] = jnp.where(i == 0, 0.0, _rms_fwd(hp_ref[...], gain))
        pad[HALO:, :] = _rms_fwd(h_ref[...], gain)
        for g, w in enumerate(POOL_WINDOWS):
            cols = slice(g * GW, (g + 1) * GW)
            mixed = _pool_mixed(pad, g, w, _inv_cnt(i, w))
            y = _dot(mixed.astype(BF16), wg_ref[g])
            hn_ref[:, cols] = h_ref[:, cols] + y * sc_ref[:, cols]

    return pl.pallas_call(
        body, name="b_mix_fwd", grid=(t // TM,),
        in_specs=[pl.BlockSpec((TM, D), lambda i: (i, 0)),
                  pl.BlockSpec((HALO, D), lambda i: (_prev_halo(i), 0)),
                  pl.BlockSpec((1, D), lambda i: (0, 0)),
                  pl.BlockSpec((4, GW, GW), lambda i: (0, 0, 0)),
                  pl.BlockSpec((1, D), lambda i: (0, 0))],
        out_specs=pl.BlockSpec((TM, D), lambda i: (i, 0)),
        out_shape=jax.ShapeDtypeStruct((t, D), F32),
        scratch_shapes=[pltpu.VMEM((HALO + TM, D), F32)],
        compiler_params=_cp("arbitrary"),
    )(h, h, gain, wg, scale)


def b_mix_bwd(dh, h, gain, wg, scale):
    t = h.shape[0]

    def body(dh_ref, dhn_ref, h_ref, hp_ref, g_ref, wg_ref, sc_ref, dhp_ref, dgain_ref, dwg_ref, dsc_ref, pad, dpad, du):
        i = pl.program_id(0)
        last = i == pl.num_programs(0) - 1

        @pl.when(i == 0)
        def _():
            for r in (dgain_ref, dwg_ref, dsc_ref):
                r[...] = jnp.zeros_like(r)

        gain = g_ref[...]
        pad[0:HALO, :] = jnp.where(i == 0, 0.0, _rms_fwd(hp_ref[...], gain))
        pad[HALO:, :] = _rms_fwd(h_ref[...], gain)
        for g, w in enumerate(POOL_WINDOWS):
            cols = slice(g * GW, (g + 1) * GW)
            inv_cnt = _inv_cnt(i, w)
            mixed = _pool_mixed(pad, g, w, inv_cnt).astype(BF16)
            dh = dh_ref[:, cols]
            dsc_ref[:, cols] += jnp.sum(dh * _dot(mixed, wg_ref[g]), axis=0, keepdims=True)
            dy = (dh * sc_ref[:, cols]).astype(BF16)
            dwg_ref[g] += _dot_tn(mixed, dy)
            dm = _dot_nt(dy, wg_ref[g])
            dmn = _dot_nt((dhn_ref[:, cols] * sc_ref[:, cols]).astype(BF16), wg_ref[g])
            dpad[0:TM, cols] = dm * inv_cnt
            dpad[TM:, cols] = jnp.where(last, 0.0, dmn * (1.0 / w))
            s = dpad[0:TM, cols]
            for j in range(1, w):
                s = s + dpad[j:j + TM, cols]
            du[:, cols] = s - dm
        dx, dgain = _rms_bwd(h_ref[...], gain, du[...])
        dhp_ref[...] = dh_ref[...] + dx
        dgain_ref[...] += dgain

    return pl.pallas_call(
        body, name="b_mix_bwd", grid=(t // TM,),
        in_specs=[pl.BlockSpec((TM, D), lambda i: (i, 0)),
                  pl.BlockSpec((HALO, D), lambda i: (_next_halo(i, t), 0)),
                  pl.BlockSpec((TM, D), lambda i: (i, 0)),
                  pl.BlockSpec((HALO, D), lambda i: (_prev_halo(i), 0)),
                  pl.BlockSpec((1, D), lambda i: (0, 0)),
                  pl.BlockSpec((4, GW, GW), lambda i: (0, 0, 0)),
                  pl.BlockSpec((1, D), lambda i: (0, 0))],
        out_specs=[pl.BlockSpec((TM, D), lambda i: (i, 0)),
                   pl.BlockSpec((1, D), lambda i: (0, 0)),
                   pl.BlockSpec((4, GW, GW), lambda i: (0, 0, 0)),
                   pl.BlockSpec((1, D), lambda i: (0, 0))],
        out_shape=[jax.ShapeDtypeStruct((t, D), F32),
                   jax.ShapeDtypeStruct((1, D), F32),
                   jax.ShapeDtypeStruct((4, GW, GW), F32),
                   jax.ShapeDtypeStruct((1, D), F32)],
        scratch_shapes=[pltpu.VMEM((HALO + TM, D), F32), pltpu.VMEM((TM + HALO, D), F32), pltpu.VMEM((TM, D), F32)],
        compiler_params=_cp("arbitrary"),
    )(dh, dh, h, h, gain, wg, scale)


LOSS_LANES = 128


def loss_head(h, gain, target):
    t = h.shape[0]

    def body(h_ref, g_ref, tg_ref, loss_ref, dh_ref, dgain_ref):
        @pl.when(pl.program_id(0) == 0)
        def _():
            loss_ref[...] = jnp.zeros_like(loss_ref)
            dgain_ref[...] = jnp.zeros_like(dgain_ref)

        x, gain = h_ref[...], g_ref[...]
        err = _rms_fwd(x, gain) - tg_ref[...]
        per_row = jnp.mean(err * err, axis=-1, keepdims=True)
        loss_ref[...] += jnp.broadcast_to(0.5 * jnp.sum(per_row, axis=0, keepdims=True), (1, LOSS_LANES))
        dx, dgain = _rms_bwd(x, gain, err * (1.0 / D))
        dh_ref[...] = dx
        dgain_ref[...] += dgain

    return pl.pallas_call(
        body, name="loss_head", grid=(t // TM,),
        in_specs=[pl.BlockSpec((TM, D), lambda i: (i, 0)),
                  pl.BlockSpec((1, D), lambda i: (0, 0)),
                  pl.BlockSpec((TM, D), lambda i: (i, 0))],
        out_specs=[pl.BlockSpec((1, LOSS_LANES), lambda i: (0, 0)),
                   pl.BlockSpec((TM, D), lambda i: (i, 0)),
                   pl.BlockSpec((1, D), lambda i: (0, 0))],
        out_shape=[jax.ShapeDtypeStruct((1, LOSS_LANES), F32),
                   jax.ShapeDtypeStruct((t, D), F32),
                   jax.ShapeDtypeStruct((1, D), F32)],
        compiler_params=_cp("arbitrary"),
    )(h, gain, target)


ADAM_LR = 0.001
ADAM_B1 = 0.9
ADAM_B2 = 0.999
ADAM_EPS = 1e-08
ADAM_WD = 0.01
ADAM_STEP = 10


def adamw(w, m, v, gparts, rb):
    r, c = w.shape

    def body(w_ref, m_ref, v_ref, gp_ref, g_ref, d_ref, nm_ref, nv_ref):
        g = gp_ref[0].astype(F32)
        for s in range(1, N_DEV):
            g = g + gp_ref[s].astype(F32)
        g_ref[...] = g
        m = ADAM_B1 * m_ref[...] + (1.0 - ADAM_B1) * g
        v = ADAM_B2 * v_ref[...] + (1.0 - ADAM_B2) * (g * g)
        nm_ref[...] = m
        nv_ref[...] = v
        m_hat = m / (1.0 - ADAM_B1 ** ADAM_STEP)
        v_hat = v / (1.0 - ADAM_B2 ** ADAM_STEP)
        d_ref[...] = -ADAM_LR * (m_hat / (jnp.sqrt(v_hat) + ADAM_EPS) + ADAM_WD * w_ref[...])

    blk = pl.BlockSpec((rb, c), lambda i: (i, 0))
    return pl.pallas_call(
        body, name=f"adamw_{r}x{c}", grid=(r // rb,),
        in_specs=[blk, blk, blk, pl.BlockSpec((N_DEV, rb, c), lambda i: (0, i, 0))],
        out_specs=[blk] * 4,
        out_shape=[jax.ShapeDtypeStruct((r, c), F32)] * 4,
        compiler_params=_cp("arbitrary"),
    )(w, m, v, gparts)


MESH = pl.DeviceIdType.MESH
ANY = pl.BlockSpec(memory_space=pl.ANY)
N_PEERS = N_DEV - 1


def _dev_index(p):
    return 4 * p[0] + 2 * p[1] + p[2]


def all_gather_many(shards):
    n = len(shards)

    def body(*refs):
        ins, outs = refs[:n], refs[n:2 * n]
        send_sems, recv_sems, local_sems = refs[2 * n:]
        x, y, c = lax.axis_index("x"), lax.axis_index("y"), lax.axis_index("c")
        me, sibling = (x, y, c), (x, y, 1 - c)
        chips = [(1 - x, y), (x, 1 - y), (1 - x, 1 - y)]

        def copy(a, k, block, to, src=None):
            slot = outs[a].at[_dev_index(block)]
            return pltpu.make_async_remote_copy(
                src_ref=slot if src is None else src, dst_ref=slot,
                send_sem=send_sems.at[a * N_PEERS + k], recv_sem=recv_sems.at[a * N_PEERS + k],
                device_id=to, device_id_type=MESH)

        started = []
        mine = []
        for a in range(n):
            mine.append(pltpu.make_async_copy(ins[a], outs[a].at[_dev_index(me)], local_sems.at[a]))
            mine[-1].start()
            first = [copy(a, 0, me, sibling, src=ins[a])]
            first += [copy(a, 1 + j, me, (*chip, c), src=ins[a]) for j, chip in enumerate(chips)]
            for cp in first:
                cp.start()
            started += first
        for a in range(n):
            for j, chip in enumerate(chips):
                copy(a, 1 + j, (*chip, c), me).wait_recv()
                passed = copy(a, 4 + j, (*chip, c), sibling)
                passed.start()
                started.append(passed)
        for a in range(n):
            copy(a, 0, sibling, me).wait_recv()
            for j, chip in enumerate(chips):
                copy(a, 4 + j, (*chip, 1 - c), me).wait_recv()
        for cp in started:
            cp.wait_send()
        for cp in mine:
            cp.wait()

    return pl.pallas_call(
        body, name="all_gather_many",
        in_specs=[ANY] * n, out_specs=[ANY] * n,
        out_shape=[jax.ShapeDtypeStruct((N_DEV, *s.shape), s.dtype) for s in shards],
        scratch_shapes=[pltpu.SemaphoreType.DMA((n * N_PEERS,)), pltpu.SemaphoreType.DMA((n * N_PEERS,)),
                        pltpu.SemaphoreType.DMA((n,))],
    )(*shards)


def exchange_many(parts):
    n = len(parts)

    def body(*refs):
        ins, outs = refs[:n], refs[n:2 * n]
        send_sems, recv_sems, local_sems = refs[2 * n:]
        x, y, c = lax.axis_index("x"), lax.axis_index("y"), lax.axis_index("c")
        me = _dev_index((x, y, c))
        peers = []
        for k in range(1, N_DEV):
            kx, ky, kc = (k >> 2) & 1, (k >> 1) & 1, k & 1
            peers.append((1 - x if kx else x, 1 - y if ky else y, 1 - c if kc else c))

        def copy(a, k, peer):
            return pltpu.make_async_remote_copy(
                src_ref=ins[a].at[_dev_index(peer)], dst_ref=outs[a].at[me],
                send_sem=send_sems.at[a * N_PEERS + k], recv_sem=recv_sems.at[a * N_PEERS + k],
                device_id=peer, device_id_type=MESH)

        def arrival(a, k, peer):
            slot = outs[a].at[_dev_index(peer)]
            return pltpu.make_async_remote_copy(
                src_ref=slot, dst_ref=slot,
                send_sem=send_sems.at[a * N_PEERS + k], recv_sem=recv_sems.at[a * N_PEERS + k],
                device_id=peer, device_id_type=MESH)

        started, mine = [], []
        for a in range(n):
            mine.append(pltpu.make_async_copy(ins[a].at[me], outs[a].at[me], local_sems.at[a]))
            mine[-1].start()
            for k, peer in enumerate(peers):
                started.append(copy(a, k, peer))
                started[-1].start()
        for a in range(n):
            for k, peer in enumerate(peers):
                arrival(a, k, peer).wait_recv()
        for cp in started:
            cp.wait_send()
        for cp in mine:
            cp.wait()

    return pl.pallas_call(
        body, name="exchange_many",
        in_specs=[ANY] * n, out_specs=[ANY] * n,
        out_shape=[jax.ShapeDtypeStruct(p.shape, p.dtype) for p in parts],
        scratch_shapes=[pltpu.SemaphoreType.DMA((n * N_PEERS,)), pltpu.SemaphoreType.DMA((n * N_PEERS,)),
                        pltpu.SemaphoreType.DMA((n,))],
    )(*parts)


WEIGHTS = ["ln1_0", "a0_w_in", "a0_conv", "a0_w_out", "ln2_0", "ffn0_w_gu", "ffn0_w_down",
           "ln1_1", "b1_w_grp", "b1_scale", "ln2_1", "ffn1_w_gu", "ffn1_w_down",
           "ln1_2", "c2_w_pw1", "c2_b_pw1", "c2_dw", "c2_b_dw", "c2_ln_g", "c2_ln_b", "c2_w_pw2", "c2_b_pw2",
           "ln2_2", "ffn2_w_gu", "ffn2_w_down",
           "ln1_3", "a3_w_in", "a3_conv", "a3_w_out", "ln2_3", "ffn3_w_gu", "ffn3_w_down", "ln_f"]
COL_W = {"a0_w_in": 384, "ffn0_w_gu": FC, "ffn1_w_gu": FC, "c2_w_pw1": 256, "ffn2_w_gu": FC, "a3_w_in": 384, "ffn3_w_gu": FC}
CONV_ROWS = 16
ROW_W = {"a0_w_out": 128, "a3_w_out": 128, "c2_w_pw2": 128,
         "ffn0_w_down": 352, "ffn1_w_down": 352, "ffn2_w_down": 352, "ffn3_w_down": 352,
         "b1_w_grp": 32, "a0_conv": CONV_ROWS, "a3_conv": CONV_ROWS, "c2_dw": CONV_ROWS}
ROW_OFF = {}
for _n, _r in ROW_W.items():
    ROW_OFF[_n] = sum(ROW_W[k] for k in list(ROW_W)[:list(ROW_W).index(_n)])
ROW_TOTAL = sum(ROW_W.values())
ROW_BLOCK = 208
CONV_SHAPE = {"a0_conv": (A_TAPS, 128), "a3_conv": (A_TAPS, 128), "c2_dw": (C_TAPS, 128)}
REPL = ["ln1_0", "ln2_0", "ln1_1", "b1_scale", "ln2_1", "ln1_2", "c2_b_pw1", "c2_b_dw", "c2_ln_g", "c2_ln_b",
        "c2_b_pw2", "ln2_2", "ln1_3", "ln2_3", "ln_f"]
REPL_ROWS = 16


def _shard_to_rows(name, s):
    if name in CONV_SHAPE:
        flat = s.reshape(1, -1)
        return jnp.pad(flat, ((0, 0), (0, CONV_ROWS * D - flat.shape[1]))).reshape(CONV_ROWS, D)
    return s.reshape(ROW_W[name], D)


def _rows_to_shard(name, rows):
    if name in CONV_SHAPE:
        k, c = CONV_SHAPE[name]
        return rows.reshape(-1)[:k * c].reshape(k, c)
    if name == "b1_w_grp":
        return rows.reshape(4, 32, GW)
    return rows


def _pack_rows(get):
    return jnp.concatenate([_shard_to_rows(n, get(n)) for n in ROW_W], axis=0)


def _grad_to_row_chunks(name, g):
    if name in CONV_SHAPE:
        k = g.shape[0]
        per_dev = g.reshape(k, N_DEV, 128).transpose(1, 0, 2).reshape(N_DEV, k * 128)
        return jnp.pad(per_dev, ((0, 0), (0, CONV_ROWS * D - k * 128))).reshape(N_DEV, CONV_ROWS, D)
    if name == "b1_w_grp":
        return g.reshape(4, N_DEV, 32, GW).transpose(1, 0, 2, 3).reshape(N_DEV, 32, D)
    return g.reshape(N_DEV, ROW_W[name], D)


def _repl_pack(get):
    return jnp.concatenate([get(n).reshape(-1, D) for n in REPL], axis=0)


def _repl_unpack(rows):
    out, r = {}, 0
    for n in REPL:
        k = 2 if n == "c2_b_pw1" else 1
        out[n] = rows[r:r + k].reshape(k * D)
        r += k
    return out


def _step(p):
    vec = lambda n: p[n].reshape(1, -1)
    x, target = p["x"][0], p["loss_target"][0]

    cols = list(COL_W)
    gathered = all_gather_many([p[n].astype(BF16) for n in cols] + [_pack_rows(lambda n: p[n]).astype(BF16)])
    wc = dict(zip(cols, gathered[:-1]))
    wrows = gathered[-1]

    def rows_of(n):
        return wrows[:, ROW_OFF[n]:ROW_OFF[n] + ROW_W[n], :]

    def conv_full(n):
        k, c = CONV_SHAPE[n]
        return rows_of(n).reshape(N_DEV, -1)[:, :k * c].reshape(N_DEV, k, c).transpose(1, 0, 2).reshape(k, D).astype(F32)

    wgu = [wc[f"ffn{i}_w_gu"].reshape(2, 4, D, FC) for i in range(4)]
    wd = [rows_of(f"ffn{i}_w_down").reshape(4, FC, D) for i in range(4)]
    wout = {i: rows_of(f"a{i}_w_out").reshape(D, D) for i in (0, 3)}
    conv = {i: conv_full(f"a{i}_conv") for i in (0, 3)}
    wpw2 = rows_of("c2_w_pw2").reshape(D, D)
    cdw = conv_full("c2_dw")
    wgrp = rows_of("b1_w_grp").reshape(N_DEV, 4, 32, GW).transpose(1, 0, 2, 3).reshape(4, GW, GW)
    no_bias = jnp.zeros((1, 3 * D), F32)

    h = [x]
    saved = {}
    for i in (0, 1, 2, 3):
        if i in (0, 3):
            z, u = rms_matmul(h[-1], vec(f"ln1_{i}"), wc[f"a{i}_w_in"], no_bias)
            hm = a_mix_fwd(z, h[-1], conv[i], wout[i])
            saved[f"mix{i}"] = (z, u)
        elif i == 1:
            hm = b_mix_fwd(h[-1], vec("ln1_1"), wgrp, vec("b1_scale"))
        else:
            z, u = rms_matmul(h[-1], vec("ln1_2"), wc["c2_w_pw1"], vec("c2_b_pw1"))
            hm, h2 = c_mix_fwd(z, h[-1], cdw, vec("c2_b_dw"), vec("c2_ln_g"), vec("c2_ln_b"), wpw2, vec("c2_b_pw2"))
            saved["mix2"] = (z, u, h2)
        h.append(hm)
        hn, zf, uf = ffn_fwd(hm, vec(f"ln2_{i}"), wgu[i], wd[i])
        saved[f"ffn{i}"] = (zf, uf)
        h.append(hn)
    loss_lanes, dh, g_lnf = loss_head(h[-1], vec("ln_f"), target)

    g = {"ln_f": g_lnf}
    for i in (3, 2, 1, 0):
        zf, uf = saved[f"ffn{i}"]
        dh_in = dh
        dh, dzf, a, g[f"ln2_{i}"] = ffn_bwd_x(dh_in, h[2 * i + 1], vec(f"ln2_{i}"), zf, wgu[i], wd[i])
        g[f"ffn{i}_w_gu"], g[f"ffn{i}_w_down"] = ffn_bwd_w(uf, dzf, a, dh_in)
        hin = h[2 * i]
        if i in (0, 3):
            z, u = saved[f"mix{i}"]
            dz, g[f"a{i}_w_out"], g[f"a{i}_conv"] = a_mix_bwd(dh, z, conv[i], wout[i])
            g[f"a{i}_w_in"] = in_proj_bwd_w(u, dz, 384)
            dh, g[f"ln1_{i}"], _ = in_proj_bwd_x(dz, wc[f"a{i}_w_in"], hin, vec(f"ln1_{i}"), dh)
        elif i == 1:
            dh, g["ln1_1"], g["b1_w_grp"], g["b1_scale"] = b_mix_bwd(dh, hin, vec("ln1_1"), wgrp, vec("b1_scale"))
        else:
            z, u, h2 = saved["mix2"]
            (dz, g["c2_w_pw2"], g["c2_b_pw2"], g["c2_ln_g"], g["c2_ln_b"], g["c2_b_dw"], g["c2_dw"]) = c_mix_bwd(
                dh, z, h2, cdw, vec("c2_ln_g"), vec("c2_ln_b"), wpw2)
            g["c2_w_pw1"] = in_proj_bwd_w(u, dz, 256)
            dh, g["ln1_2"], g["c2_b_pw1"] = in_proj_bwd_x(dz, wc["c2_w_pw1"], hin, vec("ln1_2"), dh)
    grad_x = dh[None]

    parts = [g[n].reshape(N_DEV, D, COL_W[n]).astype(BF16) for n in cols]
    parts.append(jnp.concatenate([_grad_to_row_chunks(n, g[n]) for n in ROW_W], axis=1).astype(BF16))
    repl_rows = _repl_pack(lambda n: g[n])
    parts.append(jnp.broadcast_to(repl_rows[None], (N_DEV, REPL_ROWS, D)))
    recv = exchange_many(parts)

    grad, delta, new_m, new_v = {}, {}, {}, {}
    for n, gp in zip(cols, recv):
        grad[n], delta[n], new_m[n], new_v[n] = adamw(p[n], p["m_" + n], p["v_" + n], gp, 256)
    packs = [_pack_rows(lambda n, pre=pre: p[pre + n]) for pre in ("", "m_", "v_")]
    outs = adamw(*packs, recv[len(cols)], ROW_BLOCK)
    for res, o in zip((grad, delta, new_m, new_v), outs):
        for n in ROW_W:
            res[n] = _rows_to_shard(n, o[ROW_OFF[n]:ROW_OFF[n] + ROW_W[n]]).reshape(p[n].shape)
    packs = [_repl_pack(lambda n, pre=pre: p[pre + n]) for pre in ("", "m_", "v_")]
    outs = adamw(*packs, recv[len(cols) + 1], REPL_ROWS)
    for res, o in zip((grad, delta, new_m, new_v), outs):
        res.update(_repl_unpack(o))

    loss = lax.psum(loss_lanes[0, 0], ("x", "y", "c"))
    return (loss, grad_x, *[grad[n] for n in WEIGHTS], *[delta[n] for n in WEIGHTS],
            *[new_m[n] for n in WEIGHTS], *[new_v[n] for n in WEIGHTS])


def kernel(x, ln1_0, a0_w_in, a0_conv, a0_w_out, ln2_0, ffn0_w_gu, ffn0_w_down, ln1_1, b1_w_grp, b1_scale, ln2_1, ffn1_w_gu, ffn1_w_down, ln1_2, c2_w_pw1, c2_b_pw1, c2_dw, c2_b_dw, c2_ln_g, c2_ln_b, c2_w_pw2, c2_b_pw2, ln2_2, ffn2_w_gu, ffn2_w_down, ln1_3, a3_w_in, a3_conv, a3_w_out, ln2_3, ffn3_w_gu, ffn3_w_down, ln_f, loss_target, m_ln1_0, m_a0_w_in, m_a0_conv, m_a0_w_out, m_ln2_0, m_ffn0_w_gu, m_ffn0_w_down, m_ln1_1, m_b1_w_grp, m_b1_scale, m_ln2_1, m_ffn1_w_gu, m_ffn1_w_down, m_ln1_2, m_c2_w_pw1, m_c2_b_pw1, m_c2_dw, m_c2_b_dw, m_c2_ln_g, m_c2_ln_b, m_c2_w_pw2, m_c2_b_pw2, m_ln2_2, m_ffn2_w_gu, m_ffn2_w_down, m_ln1_3, m_a3_w_in, m_a3_conv, m_a3_w_out, m_ln2_3, m_ffn3_w_gu, m_ffn3_w_down, m_ln_f, v_ln1_0, v_a0_w_in, v_a0_conv, v_a0_w_out, v_ln2_0, v_ffn0_w_gu, v_ffn0_w_down, v_ln1_1, v_b1_w_grp, v_b1_scale, v_ln2_1, v_ffn1_w_gu, v_ffn1_w_down, v_ln1_2, v_c2_w_pw1, v_c2_b_pw1, v_c2_dw, v_c2_b_dw, v_c2_ln_g, v_c2_ln_b, v_c2_w_pw2, v_c2_b_pw2, v_ln2_2, v_ffn2_w_gu, v_ffn2_w_down, v_ln1_3, v_a3_w_in, v_a3_conv, v_a3_w_out, v_ln2_3, v_ffn3_w_gu, v_ffn3_w_down, v_ln_f):
    return _step(dict(locals()))
```

```python
import jax
import jax.numpy as jnp
from jax import lax
from jax.experimental import pallas as pl
from jax.experimental.pallas import tpu as pltpu

F32 = jnp.float32
BF16 = jnp.bfloat16

N_DEV = 8
D = 1024
FF = 2816
FC = FF // 4
RMS_EPS = 1e-6
LN_EPS = 1e-5
TM = 512
HALO = 32
VMEM_LIMIT = 60 * 1024 * 1024

NT = (((1,), (1,)), ((), ()))
TN = (((0,), (0,)), ((), ()))
MESH = pl.DeviceIdType.MESH
ANY = pl.BlockSpec(memory_space=pl.ANY)
N_PEERS = N_DEV - 1


def _dot(a, b):
    return jnp.dot(a, b, preferred_element_type=F32)


def _dot_nt(a, b):
    return lax.dot_general(a, b, NT, preferred_element_type=F32)


def _dot_tn(a, b):
    return lax.dot_general(a, b, TN, preferred_element_type=F32)


def _rms_fwd(x, gain):
    r = lax.rsqrt(jnp.mean(x * x, axis=-1, keepdims=True) + RMS_EPS)
    return x * r * gain


def _rms_bwd(x, gain, du):
    r = lax.rsqrt(jnp.mean(x * x, axis=-1, keepdims=True) + RMS_EPS)
    xhat = x * r
    dgain = jnp.sum(du * xhat, axis=0, keepdims=True)
    dxhat = du * gain
    dx = r * (dxhat - xhat * jnp.mean(dxhat * xhat, axis=-1, keepdims=True))
    return dx, dgain


def _dev_index(p):
    return 4 * p[0] + 2 * p[1] + p[2]


def _place():
    return lax.axis_index("x"), lax.axis_index("y"), lax.axis_index("c")


class Ride:
    def __init__(self, ins, out_shapes, start, finish):
        self.ins, self.out_shapes, self.start, self.finish = list(ins), list(out_shapes), start, finish
        n = len(self.ins)
        self.sems = [pltpu.SemaphoreType.DMA((n * N_PEERS,)), pltpu.SemaphoreType.DMA((n * N_PEERS,)),
                     pltpu.SemaphoreType.DMA((n,))]


def gather_ride(shards):
    n = len(shards)

    def setup(ins, outs, sems):
        send_sems, recv_sems, local_sems = sems
        x, y, c = _place()
        chips = [(1 - x, y), (x, 1 - y), (1 - x, 1 - y)]

        def copy(a, k, block, to, src=None):
            slot = outs[a].at[_dev_index(block)]
            return pltpu.make_async_remote_copy(
                src_ref=slot if src is None else src, dst_ref=slot,
                send_sem=send_sems.at[a * N_PEERS + k], recv_sem=recv_sems.at[a * N_PEERS + k],
                device_id=to, device_id_type=MESH)

        def mine(a):
            return pltpu.make_async_copy(ins[a], outs[a].at[_dev_index((x, y, c))], local_sems.at[a])

        def first(a):
            return [copy(a, 0, (x, y, c), (x, y, 1 - c), src=ins[a])] + [
                copy(a, 1 + j, (x, y, c), (*chip, c), src=ins[a]) for j, chip in enumerate(chips)]

        return (x, y, c), chips, copy, mine, first

    def start(ins, outs, sems):
        _, _, _, mine, first = setup(ins, outs, sems)
        for a in range(n):
            mine(a).start()
            for cp in first(a):
                cp.start()

    def finish(ins, outs, sems):
        (x, y, c), chips, copy, mine, first = setup(ins, outs, sems)
        me, sibling = (x, y, c), (x, y, 1 - c)
        passed = []
        for a in range(n):
            for j, chip in enumerate(chips):
                copy(a, 1 + j, (*chip, c), me).wait_recv()
                passed.append(copy(a, 4 + j, (*chip, c), sibling))
                passed[-1].start()
        for a in range(n):
            copy(a, 0, sibling, me).wait_recv()
            for j, chip in enumerate(chips):
                copy(a, 4 + j, (*chip, 1 - c), me).wait_recv()
        for a in range(n):
            for cp in first(a):
                cp.wait_send()
        for cp in passed:
            cp.wait_send()
        for a in range(n):
            mine(a).wait()

    return Ride(shards, [jax.ShapeDtypeStruct((N_DEV, *s.shape), s.dtype) for s in shards], start, finish)


def _chunk(ref, kind, j):
    if kind == "lead":
        return ref.at[j]
    if kind == "rows":
        r = ref.shape[0] // N_DEV
        return ref.at[pl.ds(j * r, r)]
    if kind == "mid":
        r = ref.shape[1] // N_DEV
        return ref.at[:, pl.ds(j * r, r), :]
    if kind == "cols":
        c = ref.shape[1] // N_DEV
        return ref.at[:, pl.ds(j * c, c)]
    return ref


def _chunk_shape(shape, kind):
    if kind == "lead":
        return tuple(shape[1:])
    if kind == "rows":
        return (shape[0] // N_DEV, *shape[1:])
    if kind == "mid":
        return (shape[0], shape[1] // N_DEV, shape[2])
    if kind == "cols":
        return (shape[0], shape[1] // N_DEV)
    return tuple(shape)


def scatter_ride(parts):
    n = len(parts)
    kinds = [k for _, k in parts]

    def setup(ins, outs, sems):
        send_sems, recv_sems, local_sems = sems
        x, y, c = _place()
        me = _dev_index((x, y, c))
        peers = []
        for k in range(1, N_DEV):
            kx, ky, kc = (k >> 2) & 1, (k >> 1) & 1, k & 1
            peers.append((1 - x if kx else x, 1 - y if ky else y, 1 - c if kc else c))

        def copy(a, k, peer):
            return pltpu.make_async_remote_copy(
                src_ref=_chunk(ins[a], kinds[a], _dev_index(peer)), dst_ref=outs[a].at[me],
                send_sem=send_sems.at[a * N_PEERS + k], recv_sem=recv_sems.at[a * N_PEERS + k],
                device_id=peer, device_id_type=MESH)

        def arrival(a, k, peer):
            slot = outs[a].at[_dev_index(peer)]
            return pltpu.make_async_remote_copy(
                src_ref=slot, dst_ref=slot,
                send_sem=send_sems.at[a * N_PEERS + k], recv_sem=recv_sems.at[a * N_PEERS + k],
                device_id=peer, device_id_type=MESH)

        def mine(a):
            return pltpu.make_async_copy(_chunk(ins[a], kinds[a], me), outs[a].at[me], local_sems.at[a])

        return peers, copy, arrival, mine

    def start(ins, outs, sems):
        peers, copy, _, mine = setup(ins, outs, sems)
        for a in range(n):
            mine(a).start()
            for k, peer in enumerate(peers):
                copy(a, k, peer).start()

    def finish(ins, outs, sems):
        peers, copy, arrival, mine = setup(ins, outs, sems)
        for a in range(n):
            for k, peer in enumerate(peers):
                arrival(a, k, peer).wait_recv()
        for a in range(n):
            for k, peer in enumerate(peers):
                copy(a, k, peer).wait_send()
            mine(a).wait()

    shapes = [jax.ShapeDtypeStruct((N_DEV, *_chunk_shape(arr.shape, kind)), arr.dtype) for arr, kind in parts]
    return Ride([arr for arr, _ in parts], shapes, start, finish)


def run_ride(ride, name):
    n_in, n_out = len(ride.ins), len(ride.out_shapes)

    def body(*refs):
        ins, outs, sems = refs[:n_in], refs[n_in:n_in + n_out], refs[n_in + n_out:]
        ride.start(ins, outs, sems)
        ride.finish(ins, outs, sems)

    return pl.pallas_call(
        body, name=name, in_specs=[ANY] * n_in, out_specs=[ANY] * n_out, out_shape=ride.out_shapes,
        scratch_shapes=ride.sems,
    )(*ride.ins)


def _call(body, *, name, grid, in_specs, out_specs, out_shape, args, scratch_shapes=(), ride=None):
    params = pltpu.CompilerParams(dimension_semantics=("arbitrary",) * len(grid), vmem_limit_bytes=VMEM_LIMIT)
    if ride is None:
        outs = pl.pallas_call(body, name=name, grid=grid, in_specs=in_specs, out_specs=out_specs, out_shape=out_shape,
                              scratch_shapes=list(scratch_shapes), compiler_params=params)(*args)
        return outs, []
    n_in, n_out, n_scr = len(in_specs), len(out_specs), len(scratch_shapes)
    r_in, r_out = len(ride.ins), len(ride.out_shapes)

    def hosted(*refs):
        ins, refs = refs[:n_in], refs[n_in:]
        rins, refs = refs[:r_in], refs[r_in:]
        outs, refs = refs[:n_out], refs[n_out:]
        routs, refs = refs[:r_out], refs[r_out:]
        scratch, sems = refs[:n_scr], refs[n_scr:]
        first = pl.program_id(0) == 0
        last = pl.program_id(0) == grid[0] - 1
        for d in range(1, len(grid)):
            first &= pl.program_id(d) == 0
            last &= pl.program_id(d) == grid[d] - 1

        @pl.when(first)
        def _():
            ride.start(rins, routs, sems)

        body(*ins, *outs, *scratch)

        @pl.when(last)
        def _():
            ride.finish(rins, routs, sems)

    outs = pl.pallas_call(
        hosted, name=name + "_ride", grid=grid,
        in_specs=list(in_specs) + [ANY] * r_in, out_specs=list(out_specs) + [ANY] * r_out,
        out_shape=list(out_shape) + ride.out_shapes,
        scratch_shapes=list(scratch_shapes) + ride.sems, compiler_params=params,
    )(*args, *ride.ins)
    return outs[:n_out], outs[n_out:]


def ffn_fwd(h, gain, wgu, wd, ride=None):
    t = h.shape[0]

    def body(h_ref, g_ref, wgu_ref, wd_ref, hn_ref, z_ref, u_ref, acc):
        k = pl.program_id(1)

        @pl.when(k == 0)
        def _():
            u_ref[...] = _rms_fwd(h_ref[...], g_ref[...]).astype(BF16)
            acc[...] = jnp.zeros_like(acc)

        u = u_ref[...]
        g = _dot(u, wgu_ref[0, 0])
        up = _dot(u, wgu_ref[1, 0])
        z_ref[0, 0] = g.astype(BF16)
        z_ref[1, 0] = up.astype(BF16)
        a = g * jax.nn.sigmoid(g) * up
        acc[...] += _dot(a.astype(BF16), wd_ref[0])

        @pl.when(k == 3)
        def _():
            hn_ref[...] = h_ref[...] + acc[...]

    return _call(
        body, name="ffn_fwd", grid=(t // TM, 4), ride=ride,
        in_specs=[pl.BlockSpec((TM, D), lambda i, k: (i, 0)),
                  pl.BlockSpec((1, D), lambda i, k: (0, 0)),
                  pl.BlockSpec((2, 1, D, FC), lambda i, k: (0, k, 0, 0)),
                  pl.BlockSpec((1, FC, D), lambda i, k: (k, 0, 0))],
        out_specs=[pl.BlockSpec((TM, D), lambda i, k: (i, 0)),
                   pl.BlockSpec((2, 1, TM, FC), lambda i, k: (0, k, i, 0)),
                   pl.BlockSpec((TM, D), lambda i, k: (i, 0))],
        out_shape=[jax.ShapeDtypeStruct((t, D), F32),
                   jax.ShapeDtypeStruct((2, 4, t, FC), BF16),
                   jax.ShapeDtypeStruct((t, D), BF16)],
        scratch_shapes=[pltpu.VMEM((TM, D), F32)],
        args=(h, gain, wgu, wd))


def ffn_bwd_x(dh, h, gain, z, wgu, wd, ride=None):
    t = h.shape[0]

    def body(dh_ref, h_ref, g_ref, z_ref, wgu_ref, wd_ref, dhp_ref, dz_ref, a_ref, dgain_ref, dhb, du):
        i, k = pl.program_id(0), pl.program_id(1)

        @pl.when(k == 0)
        def _():
            dhb[...] = dh_ref[...].astype(BF16)
            du[...] = jnp.zeros_like(du)

        @pl.when((k == 0) & (i == 0))
        def _():
            dgain_ref[...] = jnp.zeros_like(dgain_ref)

        da = _dot_nt(dhb[...], wd_ref[0])
        g = z_ref[0, 0].astype(F32)
        up = z_ref[1, 0].astype(F32)
        sg = jax.nn.sigmoid(g)
        silu = g * sg
        a_ref[0] = (silu * up).astype(BF16)
        dg = (da * up * (sg * (1.0 + g * (1.0 - sg)))).astype(BF16)
        dup = (da * silu).astype(BF16)
        dz_ref[0, 0] = dg
        dz_ref[1, 0] = dup
        du[...] += _dot_nt(dg, wgu_ref[0, 0]) + _dot_nt(dup, wgu_ref[1, 0])

        @pl.when(k == 3)
        def _():
            dx, dgain = _rms_bwd(h_ref[...], g_ref[...], du[...])
            dhp_ref[...] = dh_ref[...] + dx
            dgain_ref[...] += dgain

    return _call(
        body, name="ffn_bwd_x", grid=(t // TM, 4), ride=ride,
        in_specs=[pl.BlockSpec((TM, D), lambda i, k: (i, 0)),
                  pl.BlockSpec((TM, D), lambda i, k: (i, 0)),
                  pl.BlockSpec((1, D), lambda i, k: (0, 0)),
                  pl.BlockSpec((2, 1, TM, FC), lambda i, k: (0, k, i, 0)),
                  pl.BlockSpec((2, 1, D, FC), lambda i, k: (0, k, 0, 0)),
                  pl.BlockSpec((1, FC, D), lambda i, k: (k, 0, 0))],
        out_specs=[pl.BlockSpec((TM, D), lambda i, k: (i, 0)),
                   pl.BlockSpec((2, 1, TM, FC), lambda i, k: (0, k, i, 0)),
                   pl.BlockSpec((1, TM, FC), lambda i, k: (k, i, 0)),
                   pl.BlockSpec((1, D), lambda i, k: (0, 0))],
        out_shape=[jax.ShapeDtypeStruct((t, D), F32),
                   jax.ShapeDtypeStruct((2, 4, t, FC), BF16),
                   jax.ShapeDtypeStruct((4, t, FC), BF16),
                   jax.ShapeDtypeStruct((1, D), F32)],
        scratch_shapes=[pltpu.VMEM((TM, D), BF16), pltpu.VMEM((TM, D), F32)],
        args=(dh, h, gain, z, wgu, wd))


def ffn_bwd_w(u, dz, a, dh, ride=None):
    t = u.shape[0]
    steps = t // TM

    def body(u_ref, dz_ref, a_ref, dh_ref, dwgu_ref, dwd_ref, acc_gu, acc_d):
        j = pl.program_id(1)

        @pl.when(j == 0)
        def _():
            acc_gu[...] = jnp.zeros_like(acc_gu)
            acc_d[...] = jnp.zeros_like(acc_d)

        ub = u_ref[...]
        acc_gu[0] += _dot_tn(ub, dz_ref[0, 0])
        acc_gu[1] += _dot_tn(ub, dz_ref[1, 0])
        acc_d[...] += _dot_tn(a_ref[0], dh_ref[...].astype(BF16))

        @pl.when(j == steps - 1)
        def _():
            dwgu_ref[:, 0] = acc_gu[...].astype(BF16)
            dwd_ref[0] = acc_d[...].astype(BF16)

    return _call(
        body, name="ffn_bwd_w", grid=(4, steps), ride=ride,
        in_specs=[pl.BlockSpec((TM, D), lambda k, j: (j, 0)),
                  pl.BlockSpec((2, 1, TM, FC), lambda k, j: (0, k, j, 0)),
                  pl.BlockSpec((1, TM, FC), lambda k, j: (k, j, 0)),
                  pl.BlockSpec((TM, D), lambda k, j: (j, 0))],
        out_specs=[pl.BlockSpec((2, 1, D, FC), lambda k, j: (0, k, 0, 0)),
                   pl.BlockSpec((1, FC, D), lambda k, j: (k, 0, 0))],
        out_shape=[jax.ShapeDtypeStruct((2, 4, D, FC), BF16),
                   jax.ShapeDtypeStruct((4, FC, D), BF16)],
        scratch_shapes=[pltpu.VMEM((2, D, FC), F32), pltpu.VMEM((FC, D), F32)],
        args=(u, dz, a, dh))


def _prev_halo(i):
    return jnp.maximum(i * (TM // HALO) - 1, 0)


def _next_halo(i, t):
    return jnp.minimum((i + 1) * (TM // HALO), t // HALO - 1)


def rms_matmul(h, gain, w, bias, ride=None):
    t = h.shape[0]
    nc = w.shape[2]

    def body(h_ref, g_ref, w_ref, b_ref, z_ref, u_ref):
        @pl.when(pl.program_id(1) == 0)
        def _():
            u_ref[...] = _rms_fwd(h_ref[...], g_ref[...]).astype(BF16)

        z_ref[...] = (_dot(u_ref[...], w_ref[0]) + b_ref[...]).astype(BF16)

    return _call(
        body, name=f"rms_matmul_{nc}", grid=(t // TM, N_DEV), ride=ride,
        in_specs=[pl.BlockSpec((TM, D), lambda i, j: (i, 0)),
                  pl.BlockSpec((1, D), lambda i, j: (0, 0)),
                  pl.BlockSpec((1, D, nc), lambda i, j: (j, 0, 0)),
                  pl.BlockSpec((1, nc), lambda i, j: (0, j))],
        out_specs=[pl.BlockSpec((TM, nc), lambda i, j: (i, j)),
                   pl.BlockSpec((TM, D), lambda i, j: (i, 0))],
        out_shape=[jax.ShapeDtypeStruct((t, N_DEV * nc), BF16),
                   jax.ShapeDtypeStruct((t, D), BF16)],
        args=(h, gain, w, bias))


def in_proj_bwd_x(dz, w, h, gain, dh, ride=None):
    t = h.shape[0]
    nc = w.shape[2]

    def body(dz_ref, w_ref, h_ref, g_ref, dh_ref, dhp_ref, dgain_ref, dbias_ref):
        @pl.when(pl.program_id(0) == 0)
        def _():
            dgain_ref[...] = jnp.zeros_like(dgain_ref)
            dbias_ref[...] = jnp.zeros_like(dbias_ref)

        du = jnp.zeros((TM, D), F32)
        for j in range(N_DEV):
            du += _dot_nt(dz_ref[:, j * nc:(j + 1) * nc], w_ref[j])
        dx, dgain = _rms_bwd(h_ref[...], g_ref[...], du)
        dhp_ref[...] = dh_ref[...] + dx
        dgain_ref[...] += dgain
        dbias_ref[...] += jnp.sum(dz_ref[...].astype(F32), axis=0, keepdims=True)

    return _call(
        body, name=f"in_proj_bwd_x_{nc}", grid=(t // TM,), ride=ride,
        in_specs=[pl.BlockSpec((TM, N_DEV * nc), lambda i: (i, 0)),
                  pl.BlockSpec((N_DEV, D, nc), lambda i: (0, 0, 0)),
                  pl.BlockSpec((TM, D), lambda i: (i, 0)),
                  pl.BlockSpec((1, D), lambda i: (0, 0)),
                  pl.BlockSpec((TM, D), lambda i: (i, 0))],
        out_specs=[pl.BlockSpec((TM, D), lambda i: (i, 0)),
                   pl.BlockSpec((1, D), lambda i: (0, 0)),
                   pl.BlockSpec((1, N_DEV * nc), lambda i: (0, 0))],
        out_shape=[jax.ShapeDtypeStruct((t, D), F32),
                   jax.ShapeDtypeStruct((1, D), F32),
                   jax.ShapeDtypeStruct((1, N_DEV * nc), F32)],
        args=(dz, w, h, gain, dh))


def in_proj_bwd_w(u, dz, nc, ride=None):
    t = u.shape[0]
    steps = t // TM

    def body(u_ref, dz_ref, dw_ref, acc):
        s = pl.program_id(1)

        @pl.when(s == 0)
        def _():
            acc[...] = jnp.zeros_like(acc)

        acc[...] += _dot_tn(u_ref[...], dz_ref[...])

        @pl.when(s == steps - 1)
        def _():
            dw_ref[0] = acc[...].astype(BF16)

    return _call(
        body, name=f"in_proj_bwd_w_{nc}", grid=(N_DEV, steps), ride=ride,
        in_specs=[pl.BlockSpec((TM, D), lambda j, s: (s, 0)),
                  pl.BlockSpec((TM, nc), lambda j, s: (s, j))],
        out_specs=[pl.BlockSpec((1, D, nc), lambda j, s: (j, 0, 0))],
        out_shape=[jax.ShapeDtypeStruct((N_DEV, D, nc), BF16)],
        scratch_shapes=[pltpu.VMEM((D, nc), F32)],
        args=(u, dz))


A_TAPS = 3


def a_mix_fwd(z, h, conv, wout, ride=None):
    t = h.shape[0]

    def body(z_ref, zp_ref, h_ref, cw_ref, wo_ref, hn_ref, pad):
        i = pl.program_id(0)
        ph = zp_ref[:, D:2 * D].astype(F32) * zp_ref[:, 2 * D:].astype(F32)
        pad[0:HALO, :] = jnp.where(i == 0, 0.0, ph)
        pad[HALO:, :] = z_ref[:, D:2 * D].astype(F32) * z_ref[:, 2 * D:].astype(F32)
        q = jnp.zeros((TM, D), F32)
        for k in range(A_TAPS):
            off = HALO - (A_TAPS - 1) + k
            q += cw_ref[k:k + 1, :] * pad[off:off + TM, :]
        r = z_ref[:, 0:D].astype(F32) * q
        hn_ref[...] = h_ref[...] + _dot(r.astype(BF16), wo_ref[...])

    return _call(
        body, name="a_mix_fwd", grid=(t // TM,), ride=ride,
        in_specs=[pl.BlockSpec((TM, 3 * D), lambda i: (i, 0)),
                  pl.BlockSpec((HALO, 3 * D), lambda i: (_prev_halo(i), 0)),
                  pl.BlockSpec((TM, D), lambda i: (i, 0)),
                  pl.BlockSpec((A_TAPS, D), lambda i: (0, 0)),
                  pl.BlockSpec((D, D), lambda i: (0, 0))],
        out_specs=[pl.BlockSpec((TM, D), lambda i: (i, 0))],
        out_shape=[jax.ShapeDtypeStruct((t, D), F32)],
        scratch_shapes=[pltpu.VMEM((HALO + TM, D), F32)],
        args=(z, z, h, conv, wout))


def a_mix_bwd(dh, z, conv, wout, ride=None):
    t = dh.shape[0]
    steps = t // TM

    def body(dh_ref, dhn_ref, z_ref, zp_ref, zn_ref, cw_ref, wo_ref, dz_ref, dwo_ref, dcw_ref, pad, dqpad, dwo):
        i = pl.program_id(0)
        last = i == steps - 1

        @pl.when(i == 0)
        def _():
            dwo[...] = jnp.zeros_like(dwo)
            dcw_ref[...] = jnp.zeros_like(dcw_ref)

        ph = zp_ref[:, D:2 * D].astype(F32) * zp_ref[:, 2 * D:].astype(F32)
        pad[0:HALO, :] = jnp.where(i == 0, 0.0, ph)
        c = z_ref[:, D:2 * D].astype(F32)
        v = z_ref[:, 2 * D:].astype(F32)
        pad[HALO:, :] = c * v
        q = jnp.zeros((TM, D), F32)
        for k in range(A_TAPS):
            off = HALO - (A_TAPS - 1) + k
            q += cw_ref[k:k + 1, :] * pad[off:off + TM, :]
        b = z_ref[:, 0:D].astype(F32)
        dhb = dh_ref[...].astype(BF16)
        dwo[...] += _dot_tn((b * q).astype(BF16), dhb)
        dr = _dot_nt(dhb, wo_ref[...])
        dz_ref[:, 0:D] = (dr * q).astype(BF16)
        dq = dr * b
        drn = _dot_nt(dhn_ref[...].astype(BF16), wo_ref[...])
        dqpad[0:TM, :] = dq
        dqpad[TM:, :] = jnp.where(last, 0.0, drn * zn_ref[:, 0:D].astype(F32))
        dp = jnp.zeros((TM, D), F32)
        for k in range(A_TAPS):
            off = A_TAPS - 1 - k
            dp += cw_ref[k:k + 1, :] * dqpad[off:off + TM, :]
            poff = HALO - (A_TAPS - 1) + k
            dcw_ref[k:k + 1, :] += jnp.sum(dq * pad[poff:poff + TM, :], axis=0, keepdims=True)
        dz_ref[:, D:2 * D] = (dp * v).astype(BF16)
        dz_ref[:, 2 * D:] = (dp * c).astype(BF16)

        @pl.when(last)
        def _():
            dwo_ref[...] = dwo[...].astype(BF16)

    return _call(
        body, name="a_mix_bwd", grid=(steps,), ride=ride,
        in_specs=[pl.BlockSpec((TM, D), lambda i: (i, 0)),
                  pl.BlockSpec((HALO, D), lambda i: (_next_halo(i, t), 0)),
                  pl.BlockSpec((TM, 3 * D), lambda i: (i, 0)),
                  pl.BlockSpec((HALO, 3 * D), lambda i: (_prev_halo(i), 0)),
                  pl.BlockSpec((HALO, 3 * D), lambda i: (_next_halo(i, t), 0)),
                  pl.BlockSpec((A_TAPS, D), lambda i: (0, 0)),
                  pl.BlockSpec((D, D), lambda i: (0, 0))],
        out_specs=[pl.BlockSpec((TM, 3 * D), lambda i: (i, 0)),
                   pl.BlockSpec((D, D), lambda i: (0, 0)),
                   pl.BlockSpec((A_TAPS, D), lambda i: (0, 0))],
        out_shape=[jax.ShapeDtypeStruct((t, 3 * D), BF16),
                   jax.ShapeDtypeStruct((D, D), BF16),
                   jax.ShapeDtypeStruct((A_TAPS, D), F32)],
        scratch_shapes=[pltpu.VMEM((HALO + TM, D), F32), pltpu.VMEM((TM + HALO, D), F32), pltpu.VMEM((D, D), F32)],
        args=(dh, dh, z, z, z, conv, wout))


C_TAPS = 31


def _glu(zr):
    return zr[:, 0:D].astype(F32) * jax.nn.sigmoid(zr[:, D:].astype(F32))


def _ln_silu(h2, lg, lb):
    mu = jnp.mean(h2, axis=-1, keepdims=True)
    xc = h2 - mu
    rstd = lax.rsqrt(jnp.mean(xc * xc, axis=-1, keepdims=True) + LN_EPS)
    xn = xc * rstd
    h3 = xn * lg + lb
    s3 = jax.nn.sigmoid(h3)
    return xn, rstd, h3, s3


def _ln_silu_bwd(h2, lg, lb, dh4):
    xn, rstd, h3, s3 = _ln_silu(h2, lg, lb)
    dh3 = dh4 * (s3 * (1.0 + h3 * (1.0 - s3)))
    dxn = dh3 * lg
    dh2 = rstd * (dxn - jnp.mean(dxn, axis=-1, keepdims=True) - xn * jnp.mean(dxn * xn, axis=-1, keepdims=True))
    return dh2, dh3, xn, h3 * s3


def c_mix_fwd(z, h, dw, bdw, lg, lb, w2, b2, ride=None):
    t = h.shape[0]

    def body(z_ref, zp_ref, h_ref, dw_ref, bdw_ref, lg_ref, lb_ref, w2_ref, b2_ref, hn_ref, h2_ref, pad):
        i = pl.program_id(0)
        pad[0:HALO, :] = jnp.where(i == 0, 0.0, _glu(zp_ref))
        pad[HALO:, :] = _glu(z_ref)
        h2 = jnp.zeros((TM, D), F32) + bdw_ref[...]
        for k in range(C_TAPS):
            off = HALO - (C_TAPS - 1) + k
            h2 += dw_ref[k:k + 1, :] * pad[off:off + TM, :]
        h2_ref[...] = h2
        _, _, h3, s3 = _ln_silu(h2, lg_ref[...], lb_ref[...])
        hn_ref[...] = h_ref[...] + _dot((h3 * s3).astype(BF16), w2_ref[...]) + b2_ref[...]

    vec = pl.BlockSpec((1, D), lambda i: (0, 0))
    return _call(
        body, name="c_mix_fwd", grid=(t // TM,), ride=ride,
        in_specs=[pl.BlockSpec((TM, 2 * D), lambda i: (i, 0)),
                  pl.BlockSpec((HALO, 2 * D), lambda i: (_prev_halo(i), 0)),
                  pl.BlockSpec((TM, D), lambda i: (i, 0)),
                  pl.BlockSpec((C_TAPS, D), lambda i: (0, 0)),
                  vec, vec, vec,
                  pl.BlockSpec((D, D), lambda i: (0, 0)),
                  vec],
        out_specs=[pl.BlockSpec((TM, D), lambda i: (i, 0)),
                   pl.BlockSpec((TM, D), lambda i: (i, 0))],
        out_shape=[jax.ShapeDtypeStruct((t, D), F32),
                   jax.ShapeDtypeStruct((t, D), F32)],
        scratch_shapes=[pltpu.VMEM((HALO + TM, D), F32)],
        args=(z, z, h, dw, bdw, lg, lb, w2, b2))


def c_mix_bwd(dh, z, h2, dw, lg, lb, w2, ride=None):
    t = dh.shape[0]
    steps = t // TM

    def body(dh_ref, dhn_ref, z_ref, zp_ref, h2_ref, h2n_ref, dw_ref, lg_ref, lb_ref, w2_ref,
             dz_ref, dw2_ref, db2_ref, dlg_ref, dlb_ref, dbdw_ref, ddw_ref, pad, dpad, dw2):
        i = pl.program_id(0)
        last = i == steps - 1

        @pl.when(i == 0)
        def _():
            for r in (dw2, db2_ref, dlg_ref, dlb_ref, dbdw_ref, ddw_ref):
                r[...] = jnp.zeros_like(r)

        lg, lb = lg_ref[...], lb_ref[...]
        dh = dh_ref[...]
        dhb = dh.astype(BF16)
        dh2, dh3, xn, h4 = _ln_silu_bwd(h2_ref[...], lg, lb, _dot_nt(dhb, w2_ref[...]))
        dw2[...] += _dot_tn(h4.astype(BF16), dhb)
        db2_ref[...] += jnp.sum(dh, axis=0, keepdims=True)
        dlg_ref[...] += jnp.sum(dh3 * xn, axis=0, keepdims=True)
        dlb_ref[...] += jnp.sum(dh3, axis=0, keepdims=True)
        dbdw_ref[...] += jnp.sum(dh2, axis=0, keepdims=True)
        dh2n, _, _, _ = _ln_silu_bwd(h2n_ref[...], lg, lb, _dot_nt(dhn_ref[...].astype(BF16), w2_ref[...]))
        dpad[0:TM, :] = dh2
        dpad[TM:, :] = jnp.where(last, 0.0, dh2n)
        pad[0:HALO, :] = jnp.where(i == 0, 0.0, _glu(zp_ref))
        pad[HALO:, :] = _glu(z_ref)
        dh1 = jnp.zeros((TM, D), F32)
        for k in range(C_TAPS):
            off = C_TAPS - 1 - k
            dh1 += dw_ref[k:k + 1, :] * dpad[off:off + TM, :]
            poff = HALO - (C_TAPS - 1) + k
            ddw_ref[k:k + 1, :] += jnp.sum(dh2 * pad[poff:poff + TM, :], axis=0, keepdims=True)
        a = z_ref[:, 0:D].astype(F32)
        sg = jax.nn.sigmoid(z_ref[:, D:].astype(F32))
        dz_ref[:, 0:D] = (dh1 * sg).astype(BF16)
        dz_ref[:, D:] = (dh1 * a * sg * (1.0 - sg)).astype(BF16)

        @pl.when(last)
        def _():
            dw2_ref[...] = dw2[...].astype(BF16)

    vec = pl.BlockSpec((1, D), lambda i: (0, 0))
    return _call(
        body, name="c_mix_bwd", grid=(steps,), ride=ride,
        in_specs=[pl.BlockSpec((TM, D), lambda i: (i, 0)),
                  pl.BlockSpec((HALO, D), lambda i: (_next_halo(i, t), 0)),
                  pl.BlockSpec((TM, 2 * D), lambda i: (i, 0)),
                  pl.BlockSpec((HALO, 2 * D), lambda i: (_prev_halo(i), 0)),
                  pl.BlockSpec((TM, D), lambda i: (i, 0)),
                  pl.BlockSpec((HALO, D), lambda i: (_next_halo(i, t), 0)),
                  pl.BlockSpec((C_TAPS, D), lambda i: (0, 0)),
                  vec, vec,
                  pl.BlockSpec((D, D), lambda i: (0, 0), pipeline_mode=pl.Buffered(1))],
        out_specs=[pl.BlockSpec((TM, 2 * D), lambda i: (i, 0)),
                   pl.BlockSpec((D, D), lambda i: (0, 0)),
                   vec, vec, vec, vec,
                   pl.BlockSpec((C_TAPS, D), lambda i: (0, 0))],
        out_shape=[jax.ShapeDtypeStruct((t, 2 * D), BF16),
                   jax.ShapeDtypeStruct((D, D), BF16)]
                  + [jax.ShapeDtypeStruct((1, D), F32)] * 4
                  + [jax.ShapeDtypeStruct((C_TAPS, D), F32)],
        scratch_shapes=[pltpu.VMEM((HALO + TM, D), F32), pltpu.VMEM((TM + HALO, D), F32), pltpu.VMEM((D, D), F32)],
        args=(dh, dh, z, z, h2, h2, dw, lg, lb, w2))


POOL_WINDOWS = (2, 4, 8, 16)
GW = D // len(POOL_WINDOWS)


def _pool_mixed(pad, g, w, inv_cnt):
    cols = slice(g * GW, (g + 1) * GW)
    s = pad[HALO:HALO + TM, cols]
    u = s
    for j in range(1, w):
        s = s + pad[HALO - j:HALO - j + TM, cols]
    return s * inv_cnt - u


def _inv_cnt(i, w):
    row = i * TM + lax.broadcasted_iota(jnp.int32, (TM, 1), 0)
    return 1.0 / jnp.minimum(row + 1, w).astype(F32)


def b_mix_fwd(h, gain, wg, scale, ride=None):
    t = h.shape[0]

    def body(h_ref, hp_ref, g_ref, wg_ref, sc_ref, hn_ref, pad):
        i = pl.program_id(0)
        gain = g_ref[...]
        pad[0:HALO, :] = jnp.where(i == 0, 0.0, _rms_fwd(hp_ref[...], gain))
        pad[HALO:, :] = _rms_fwd(h_ref[...], gain)
        for g, w in enumerate(POOL_WINDOWS):
            cols = slice(g * GW, (g + 1) * GW)
            mixed = _pool_mixed(pad, g, w, _inv_cnt(i, w))
            y = _dot(mixed.astype(BF16), wg_ref[g])
            hn_ref[:, cols] = h_ref[:, cols] + y * sc_ref[:, cols]

    return _call(
        body, name="b_mix_fwd", grid=(t // TM,), ride=ride,
        in_specs=[pl.BlockSpec((TM, D), lambda i: (i, 0)),
                  pl.BlockSpec((HALO, D), lambda i: (_prev_halo(i), 0)),
                  pl.BlockSpec((1, D), lambda i: (0, 0)),
                  pl.BlockSpec((4, GW, GW), lambda i: (0, 0, 0)),
                  pl.BlockSpec((1, D), lambda i: (0, 0))],
        out_specs=[pl.BlockSpec((TM, D), lambda i: (i, 0))],
        out_shape=[jax.ShapeDtypeStruct((t, D), F32)],
        scratch_shapes=[pltpu.VMEM((HALO + TM, D), F32)],
        args=(h, h, gain, wg, scale))


def b_mix_bwd(dh, h, gain, wg, scale, ride=None):
    t = h.shape[0]
    steps = t // TM

    def body(dh_ref, dhn_ref, h_ref, hp_ref, g_ref, wg_ref, sc_ref, dhp_ref, dgain_ref, dwg_ref, dsc_ref, pad, dpad, du):
        i = pl.program_id(0)
        last = i == steps - 1

        @pl.when(i == 0)
        def _():
            for r in (dgain_ref, dwg_ref, dsc_ref):
                r[...] = jnp.zeros_like(r)

        gain = g_ref[...]
        pad[0:HALO, :] = jnp.where(i == 0, 0.0, _rms_fwd(hp_ref[...], gain))
        pad[HALO:, :] = _rms_fwd(h_ref[...], gain)
        for g, w in enumerate(POOL_WINDOWS):
            cols = slice(g * GW, (g + 1) * GW)
            inv_cnt = _inv_cnt(i, w)
            mixed = _pool_mixed(pad, g, w, inv_cnt).astype(BF16)
            dh = dh_ref[:, cols]
            dsc_ref[:, cols] += jnp.sum(dh * _dot(mixed, wg_ref[g]), axis=0, keepdims=True)
            dy = (dh * sc_ref[:, cols]).astype(BF16)
            dwg_ref[g] += _dot_tn(mixed, dy)
            dm = _dot_nt(dy, wg_ref[g])
            dmn = _dot_nt((dhn_ref[:, cols] * sc_ref[:, cols]).astype(BF16), wg_ref[g])
            dpad[0:TM, cols] = dm * inv_cnt
            dpad[TM:, cols] = jnp.where(last, 0.0, dmn * (1.0 / w))
            s = dpad[0:TM, cols]
            for j in range(1, w):
                s = s + dpad[j:j + TM, cols]
            du[:, cols] = s - dm
        dx, dgain = _rms_bwd(h_ref[...], gain, du[...])
        dhp_ref[...] = dh_ref[...] + dx
        dgain_ref[...] += dgain

    return _call(
        body, name="b_mix_bwd", grid=(steps,), ride=ride,
        in_specs=[pl.BlockSpec((TM, D), lambda i: (i, 0)),
                  pl.BlockSpec((HALO, D), lambda i: (_next_halo(i, t), 0)),
                  pl.BlockSpec((TM, D), lambda i: (i, 0)),
                  pl.BlockSpec((HALO, D), lambda i: (_prev_halo(i), 0)),
                  pl.BlockSpec((1, D), lambda i: (0, 0)),
                  pl.BlockSpec((4, GW, GW), lambda i: (0, 0, 0)),
                  pl.BlockSpec((1, D), lambda i: (0, 0))],
        out_specs=[pl.BlockSpec((TM, D), lambda i: (i, 0)),
                   pl.BlockSpec((1, D), lambda i: (0, 0)),
                   pl.BlockSpec((4, GW, GW), lambda i: (0, 0, 0)),
                   pl.BlockSpec((1, D), lambda i: (0, 0))],
        out_shape=[jax.ShapeDtypeStruct((t, D), F32),
                   jax.ShapeDtypeStruct((1, D), F32),
                   jax.ShapeDtypeStruct((4, GW, GW), F32),
                   jax.ShapeDtypeStruct((1, D), F32)],
        scratch_shapes=[pltpu.VMEM((HALO + TM, D), F32), pltpu.VMEM((TM + HALO, D), F32), pltpu.VMEM((TM, D), F32)],
        args=(dh, dh, h, h, gain, wg, scale))


LOSS_LANES = 128


def loss_head(h, gain, target):
    t = h.shape[0]

    def body(h_ref, g_ref, tg_ref, loss_ref, dh_ref, dgain_ref):
        @pl.when(pl.program_id(0) == 0)
        def _():
            loss_ref[...] = jnp.zeros_like(loss_ref)
            dgain_ref[...] = jnp.zeros_like(dgain_ref)

        x, gain = h_ref[...], g_ref[...]
        err = _rms_fwd(x, gain) - tg_ref[...]
        per_row = jnp.mean(err * err, axis=-1, keepdims=True)
        loss_ref[...] += jnp.broadcast_to(0.5 * jnp.sum(per_row, axis=0, keepdims=True), (1, LOSS_LANES))
        dx, dgain = _rms_bwd(x, gain, err * (1.0 / D))
        dh_ref[...] = dx
        dgain_ref[...] += dgain

    outs, _ = _call(
        body, name="loss_head", grid=(t // TM,),
        in_specs=[pl.BlockSpec((TM, D), lambda i: (i, 0)),
                  pl.BlockSpec((1, D), lambda i: (0, 0)),
                  pl.BlockSpec((TM, D), lambda i: (i, 0))],
        out_specs=[pl.BlockSpec((1, LOSS_LANES), lambda i: (0, 0)),
                   pl.BlockSpec((TM, D), lambda i: (i, 0)),
                   pl.BlockSpec((1, D), lambda i: (0, 0))],
        out_shape=[jax.ShapeDtypeStruct((1, LOSS_LANES), F32),
                   jax.ShapeDtypeStruct((t, D), F32),
                   jax.ShapeDtypeStruct((1, D), F32)],
        args=(h, gain, target))
    return outs


ADAM_LR = 0.001
ADAM_B1 = 0.9
ADAM_B2 = 0.999
ADAM_EPS = 1e-08
ADAM_WD = 0.01
ADAM_STEP = 10


def cast_all(arrays):
    def body(*refs):
        for src, dst in zip(refs[:len(arrays)], refs[len(arrays):]):
            dst[...] = src[...].astype(BF16)

    return pl.pallas_call(
        body, name="cast_all", out_shape=[jax.ShapeDtypeStruct(a.shape, BF16) for a in arrays],
        compiler_params=pltpu.CompilerParams(vmem_limit_bytes=VMEM_LIMIT),
    )(*arrays)


def _adam_math(w, m, v, g):
    m = ADAM_B1 * m + (1.0 - ADAM_B1) * g
    v = ADAM_B2 * v + (1.0 - ADAM_B2) * (g * g)
    m_hat = m / (1.0 - ADAM_B1 ** ADAM_STEP)
    v_hat = v / (1.0 - ADAM_B2 ** ADAM_STEP)
    return -ADAM_LR * (m_hat / (jnp.sqrt(v_hat) + ADAM_EPS) + ADAM_WD * w), m, v


def adamw(w, m, v, gparts, rb):
    r, c = w.shape

    def body(w_ref, m_ref, v_ref, gp_ref, g_ref, d_ref, nm_ref, nv_ref):
        g = gp_ref[0].astype(F32)
        for s in range(1, N_DEV):
            g = g + gp_ref[s].astype(F32)
        g_ref[...] = g
        d_ref[...], nm_ref[...], nv_ref[...] = _adam_math(w_ref[...], m_ref[...], v_ref[...], g)

    blk = pl.BlockSpec((rb, c), lambda i: (i, 0))
    outs, _ = _call(
        body, name=f"adamw_{r}x{c}", grid=(r // rb,),
        in_specs=[blk, blk, blk, pl.BlockSpec((N_DEV, rb, c), lambda i: (0, i, 0))],
        out_specs=[blk] * 4,
        out_shape=[jax.ShapeDtypeStruct((r, c), F32)] * 4,
        args=(w, m, v, gparts))
    return outs


def adamw_vectors(ws, ms, vs, gparts):
    nv = len(ws)

    def body(*refs):
        w_refs, m_refs, v_refs = refs[:nv], refs[nv:2 * nv], refs[2 * nv:3 * nv]
        gp_ref = refs[3 * nv]
        outs = refs[3 * nv + 1:]
        g_refs, d_refs, nm_refs, nv_refs = outs[:nv], outs[nv:2 * nv], outs[2 * nv:3 * nv], outs[3 * nv:]
        row = 0
        for i in range(nv):
            for part in range(w_refs[i].shape[1] // D):
                cols = slice(part * D, (part + 1) * D)
                g = gp_ref[0, row:row + 1, :]
                for s in range(1, N_DEV):
                    g = g + gp_ref[s, row:row + 1, :]
                g_refs[i][:, cols] = g
                d_refs[i][:, cols], nm_refs[i][:, cols], nv_refs[i][:, cols] = _adam_math(
                    w_refs[i][:, cols], m_refs[i][:, cols], v_refs[i][:, cols], g)
                row += 1

    shapes = [jax.ShapeDtypeStruct(w.shape, F32) for w in ws]
    outs = pl.pallas_call(body, name="adamw_vectors", out_shape=shapes * 4)(*ws, *ms, *vs, gparts)
    return outs[:nv], outs[nv:2 * nv], outs[2 * nv:3 * nv], outs[3 * nv:]


WEIGHTS = ["ln1_0", "a0_w_in", "a0_conv", "a0_w_out", "ln2_0", "ffn0_w_gu", "ffn0_w_down",
           "ln1_1", "b1_w_grp", "b1_scale", "ln2_1", "ffn1_w_gu", "ffn1_w_down",
           "ln1_2", "c2_w_pw1", "c2_b_pw1", "c2_dw", "c2_b_dw", "c2_ln_g", "c2_ln_b", "c2_w_pw2", "c2_b_pw2",
           "ln2_2", "ffn2_w_gu", "ffn2_w_down",
           "ln1_3", "a3_w_in", "a3_conv", "a3_w_out", "ln2_3", "ffn3_w_gu", "ffn3_w_down", "ln_f"]
SHARDED = {"a0_w_in": ("lead", 256), "a0_conv": ("cols", A_TAPS), "a0_w_out": ("rows", 128),
           "ffn0_w_gu": ("lead", 256), "ffn0_w_down": ("rows", 176),
           "b1_w_grp": ("mid", 128),
           "ffn1_w_gu": ("lead", 256), "ffn1_w_down": ("rows", 176),
           "c2_w_pw1": ("lead", 256), "c2_dw": ("cols", C_TAPS), "c2_w_pw2": ("rows", 128),
           "ffn2_w_gu": ("lead", 256), "ffn2_w_down": ("rows", 176),
           "a3_w_in": ("lead", 256), "a3_conv": ("cols", A_TAPS), "a3_w_out": ("rows", 128),
           "ffn3_w_gu": ("lead", 256), "ffn3_w_down": ("rows", 176)}
REPL = [n for n in WEIGHTS if n not in SHARDED]
REPL_ROWS = 16
GATHER_PLAN = {"first": ["a0_w_in", "a0_w_out", "a0_conv"],
               "in0": ["ffn0_w_gu"], "mix0": ["ffn0_w_down"],
               "ffn0": ["b1_w_grp", "ffn1_w_gu", "ffn1_w_down"],
               "mix1": ["c2_w_pw1"],
               "ffn1": ["c2_w_pw2", "c2_dw", "ffn2_w_gu", "ffn2_w_down"],
               "in2": ["a3_w_in", "a3_w_out", "a3_conv"],
               "mix2": ["ffn3_w_gu", "ffn3_w_down"]}
SCATTER_PLAN = {"mixb3": ["ffn3_w_down"], "inw3": ["a3_w_out", "a3_conv"],
                "ffnx2": ["ffn3_w_gu"], "ffnw2": ["a3_w_in"],
                "mixb2": ["ffn2_w_gu", "ffn2_w_down"], "inw2": ["c2_w_pw2", "c2_dw"],
                "ffnx1": ["c2_w_pw1"], "mixb1": ["ffn1_w_down"],
                "ffnx0": ["ffn1_w_gu", "b1_w_grp"],
                "mixb0": ["ffn0_w_down"], "inw0": ["ffn0_w_gu"], "inx0": ["a0_w_out", "a0_conv"],
                "last": ["a0_w_in", "repl"]}


def _step(p):
    vec = lambda n: p[n].reshape(1, -1)
    x, target = p["x"][0], p["loss_target"][0]

    names = list(SHARDED)
    shard = dict(zip(names, cast_all([p[n] for n in names])))
    full = {}

    def gather(slot):
        return gather_ride([shard[n] for n in GATHER_PLAN[slot]])

    def landed(slot, outs):
        full.update(zip(GATHER_PLAN[slot], outs))

    def conv_full(n):
        k = full[n].shape[1]
        return full[n].transpose(1, 0, 2).reshape(k, D).astype(F32)

    def wgu(i):
        return full[f"ffn{i}_w_gu"].reshape(2, 4, D, FC)

    def wd(i):
        return full[f"ffn{i}_w_down"].reshape(4, FC, D)

    landed("first", run_ride(gather("first"), "gather_first"))
    no_bias = jnp.zeros((1, 3 * D), F32)
    h = [x]
    saved = {}
    conv, wout = {}, {}

    (z, u), got = rms_matmul(h[-1], vec("ln1_0"), full["a0_w_in"], no_bias, ride=gather("in0"))
    landed("in0", got)
    conv[0], wout[0] = conv_full("a0_conv"), full["a0_w_out"].reshape(D, D)
    (hm,), got = a_mix_fwd(z, h[-1], conv[0], wout[0], ride=gather("mix0"))
    landed("mix0", got)
    saved["mix0"] = (z, u)
    h.append(hm)
    (hn, zf, uf), got = ffn_fwd(hm, vec("ln2_0"), wgu(0), wd(0), ride=gather("ffn0"))
    landed("ffn0", got)
    saved["ffn0"] = (zf, uf)
    h.append(hn)

    wgrp = full["b1_w_grp"].transpose(1, 0, 2, 3).reshape(4, GW, GW)
    (hm,), got = b_mix_fwd(h[-1], vec("ln1_1"), wgrp, vec("b1_scale"), ride=gather("mix1"))
    landed("mix1", got)
    h.append(hm)
    (hn, zf, uf), got = ffn_fwd(hm, vec("ln2_1"), wgu(1), wd(1), ride=gather("ffn1"))
    landed("ffn1", got)
    saved["ffn1"] = (zf, uf)
    h.append(hn)

    (z, u), got = rms_matmul(h[-1], vec("ln1_2"), full["c2_w_pw1"], vec("c2_b_pw1"), ride=gather("in2"))
    landed("in2", got)
    cdw, wpw2 = conv_full("c2_dw"), full["c2_w_pw2"].reshape(D, D)
    (hm, h2), got = c_mix_fwd(z, h[-1], cdw, vec("c2_b_dw"), vec("c2_ln_g"), vec("c2_ln_b"), wpw2, vec("c2_b_pw2"),
                              ride=gather("mix2"))
    landed("mix2", got)
    saved["mix2"] = (z, u, h2)
    h.append(hm)
    (hn, zf, uf), _ = ffn_fwd(hm, vec("ln2_2"), wgu(2), wd(2))
    saved["ffn2"] = (zf, uf)
    h.append(hn)

    (z, u), _ = rms_matmul(h[-1], vec("ln1_3"), full["a3_w_in"], no_bias)
    conv[3], wout[3] = conv_full("a3_conv"), full["a3_w_out"].reshape(D, D)
    (hm,), _ = a_mix_fwd(z, h[-1], conv[3], wout[3])
    saved["mix3"] = (z, u)
    h.append(hm)
    (hn, zf, uf), _ = ffn_fwd(hm, vec("ln2_3"), wgu(3), wd(3))
    saved["ffn3"] = (zf, uf)
    h.append(hn)

    loss_lanes, dh, g_lnf = loss_head(h[-1], vec("ln_f"), target)

    g = {"ln_f": g_lnf}
    recv = {}

    def repl_rows():
        return jnp.concatenate([g[n].reshape(-1, D) for n in REPL], axis=0)

    def scatter(slot):
        parts = []
        for n in SCATTER_PLAN[slot]:
            parts.append((repl_rows(), "all") if n == "repl" else (g[n], SHARDED[n][0]))
        return scatter_ride(parts)

    def arrived(slot, outs):
        recv.update(zip(SCATTER_PLAN[slot], outs))

    for i in (3, 2, 1, 0):
        zf, uf = saved[f"ffn{i}"]
        dh_in = dh
        (dh, dzf, a, g[f"ln2_{i}"]), got = ffn_bwd_x(dh_in, h[2 * i + 1], vec(f"ln2_{i}"), zf, wgu(i), wd(i),
                                                      ride=scatter(f"ffnx{i}") if f"ffnx{i}" in SCATTER_PLAN else None)
        if got:
            arrived(f"ffnx{i}", got)
        (dwgu, dwd), got = ffn_bwd_w(uf, dzf, a, dh_in, ride=scatter(f"ffnw{i}") if f"ffnw{i}" in SCATTER_PLAN else None)
        if got:
            arrived(f"ffnw{i}", got)
        g[f"ffn{i}_w_gu"], g[f"ffn{i}_w_down"] = dwgu.reshape(N_DEV, D, FC), dwd.reshape(FF, D)
        hin = h[2 * i]
        if i in (0, 3):
            z, u = saved[f"mix{i}"]
            (dz, g[f"a{i}_w_out"], g[f"a{i}_conv"]), got = a_mix_bwd(dh, z, conv[i], wout[i], ride=scatter(f"mixb{i}"))
            arrived(f"mixb{i}", got)
            (g[f"a{i}_w_in"],), got = in_proj_bwd_w(u, dz, 384, ride=scatter(f"inw{i}"))
            arrived(f"inw{i}", got)
            (dh, g[f"ln1_{i}"], _), got = in_proj_bwd_x(dz, full[f"a{i}_w_in"], hin, vec(f"ln1_{i}"), dh,
                                                       ride=scatter(f"inx{i}") if f"inx{i}" in SCATTER_PLAN else None)
            if got:
                arrived(f"inx{i}", got)
        elif i == 1:
            (dh, g["ln1_1"], g["b1_w_grp"], g["b1_scale"]), got = b_mix_bwd(dh, hin, vec("ln1_1"), wgrp, vec("b1_scale"),
                                                                             ride=scatter("mixb1"))
            arrived("mixb1", got)
        else:
            z, u, h2 = saved["mix2"]
            (dz, g["c2_w_pw2"], g["c2_b_pw2"], g["c2_ln_g"], g["c2_ln_b"], g["c2_b_dw"], g["c2_dw"]), got = c_mix_bwd(
                dh, z, h2, cdw, vec("c2_ln_g"), vec("c2_ln_b"), wpw2, ride=scatter("mixb2"))
            arrived("mixb2", got)
            (g["c2_w_pw1"],), got = in_proj_bwd_w(u, dz, 256, ride=scatter("inw2"))
            arrived("inw2", got)
            (dh, g["ln1_2"], g["c2_b_pw1"]), _ = in_proj_bwd_x(dz, full["c2_w_pw1"], hin, vec("ln1_2"), dh)
    grad_x = dh[None]
    arrived("last", run_ride(scatter("last"), "scatter_last"))

    grad, delta, new_m, new_v = {}, {}, {}, {}
    for n, (_, rb) in SHARDED.items():
        shape = p[n].shape
        two_d = (-1, shape[-1])
        outs = adamw(p[n].reshape(two_d), p["m_" + n].reshape(two_d), p["v_" + n].reshape(two_d),
                     recv[n].reshape(N_DEV, *p[n].reshape(two_d).shape), rb)
        grad[n], delta[n], new_m[n], new_v[n] = [o.reshape(shape) for o in outs]
    outs = adamw_vectors([vec(n) for n in REPL], [vec("m_" + n) for n in REPL], [vec("v_" + n) for n in REPL], recv["repl"])
    for res, o in zip((grad, delta, new_m, new_v), outs):
        res.update({n: a.reshape(p[n].shape) for n, a in zip(REPL, o)})

    loss = lax.psum(loss_lanes[0, 0], ("x", "y", "c"))
    return (loss, grad_x, *[grad[n] for n in WEIGHTS], *[delta[n] for n in WEIGHTS],
            *[new_m[n] for n in WEIGHTS], *[new_v[n] for n in WEIGHTS])


def kernel(x, ln1_0, a0_w_in, a0_conv, a0_w_out, ln2_0, ffn0_w_gu, ffn0_w_down, ln1_1, b1_w_grp, b1_scale, ln2_1, ffn1_w_gu, ffn1_w_down, ln1_2, c2_w_pw1, c2_b_pw1, c2_dw, c2_b_dw, c2_ln_g, c2_ln_b, c2_w_pw2, c2_b_pw2, ln2_2, ffn2_w_gu, ffn2_w_down, ln1_3, a3_w_in, a3_conv, a3_w_out, ln2_3, ffn3_w_gu, ffn3_w_down, ln_f, loss_target, m_ln1_0, m_a0_w_in, m_a0_conv, m_a0_w_out, m_ln2_0, m_ffn0_w_gu, m_ffn0_w_down, m_ln1_1, m_b1_w_grp, m_b1_scale, m_ln2_1, m_ffn1_w_gu, m_ffn1_w_down, m_ln1_2, m_c2_w_pw1, m_c2_b_pw1, m_c2_dw, m_c2_b_dw, m_c2_ln_g, m_c2_ln_b, m_c2_w_pw2, m_c2_b_pw2, m_ln2_2, m_ffn2_w_gu, m_ffn2_w_down, m_ln1_3, m_a3_w_in, m_a3_conv, m_a3_w_out, m_ln2_3, m_ffn3_w_gu, m_ffn3_w_down, m_ln_f, v_ln1_0, v_a0_w_in, v_a0_conv, v_a0_w_out, v_ln2_0, v_ffn0_w_gu, v_ffn0_w_down, v_ln1_1, v_b1_w_grp, v_b1_scale, v_ln2_1, v_ffn1_w_gu, v_ffn1_w_down, v_ln1_2, v_c2_w_pw1, v_c2_b_pw1, v_c2_dw, v_c2_b_dw, v_c2_ln_g, v_c2_ln_b, v_c2_w_pw2, v_c2_b_pw2, v_ln2_2, v_ffn2_w_gu, v_ffn2_w_down, v_ln1_3, v_a3_w_in, v_a3_conv, v_a3_w_out, v_ln2_3, v_ffn3_w_gu, v_ffn3_w_down, v_ln_f):
    return _step(dict(locals()))
```

```python
import jax
import jax.numpy as jnp
from jax import lax
from jax.experimental import pallas as pl
from jax.experimental.pallas import tpu as pltpu

F32 = jnp.float32
BF16 = jnp.bfloat16

N_DEV = 8
D = 1024
FF = 2816
FC = FF // 4
RMS_EPS = 1e-6
LN_EPS = 1e-5
TM = 512
HALO = 32
VMEM_LIMIT = 60 * 1024 * 1024

NT = (((1,), (1,)), ((), ()))
TN = (((0,), (0,)), ((), ()))
MESH = pl.DeviceIdType.MESH
ANY = pl.BlockSpec(memory_space=pl.ANY)
N_PEERS = N_DEV - 1


def _dot(a, b):
    return jnp.dot(a, b, preferred_element_type=F32)


def _dot_nt(a, b):
    return lax.dot_general(a, b, NT, preferred_element_type=F32)


def _dot_tn(a, b):
    return lax.dot_general(a, b, TN, preferred_element_type=F32)


def _rms_fwd(x, gain):
    r = lax.rsqrt(jnp.mean(x * x, axis=-1, keepdims=True) + RMS_EPS)
    return x * r * gain


def _rms_bwd(x, gain, du):
    r = lax.rsqrt(jnp.mean(x * x, axis=-1, keepdims=True) + RMS_EPS)
    xhat = x * r
    dgain = jnp.sum(du * xhat, axis=0, keepdims=True)
    dxhat = du * gain
    dx = r * (dxhat - xhat * jnp.mean(dxhat * xhat, axis=-1, keepdims=True))
    return dx, dgain


def _dev_index(p):
    return 4 * p[0] + 2 * p[1] + p[2]


def _place():
    return lax.axis_index("x"), lax.axis_index("y"), lax.axis_index("c")


class Ride:
    def __init__(self, ins, out_shapes, start, finish):
        self.ins, self.out_shapes, self.start, self.finish = list(ins), list(out_shapes), start, finish
        n = len(self.ins)
        self.sems = [pltpu.SemaphoreType.DMA((n * N_PEERS,)), pltpu.SemaphoreType.DMA((n * N_PEERS,)),
                     pltpu.SemaphoreType.DMA((n,))]


def gather_ride(shards):
    n = len(shards)

    def setup(ins, outs, sems):
        send_sems, recv_sems, local_sems = sems
        x, y, c = _place()
        chips = [(1 - x, y), (x, 1 - y), (1 - x, 1 - y)]

        def copy(a, k, block, to, src=None):
            slot = outs[a].at[_dev_index(block)]
            return pltpu.make_async_remote_copy(
                src_ref=slot if src is None else src, dst_ref=slot,
                send_sem=send_sems.at[a * N_PEERS + k], recv_sem=recv_sems.at[a * N_PEERS + k],
                device_id=to, device_id_type=MESH)

        def mine(a):
            return pltpu.make_async_copy(ins[a], outs[a].at[_dev_index((x, y, c))], local_sems.at[a])

        def first(a):
            return [copy(a, 0, (x, y, c), (x, y, 1 - c), src=ins[a])] + [
                copy(a, 1 + j, (x, y, c), (*chip, c), src=ins[a]) for j, chip in enumerate(chips)]

        return (x, y, c), chips, copy, mine, first

    def start(ins, outs, sems):
        _, _, _, mine, first = setup(ins, outs, sems)
        for a in range(n):
            mine(a).start()
            for cp in first(a):
                cp.start()

    def finish(ins, outs, sems):
        (x, y, c), chips, copy, mine, first = setup(ins, outs, sems)
        me, sibling = (x, y, c), (x, y, 1 - c)
        passed = []
        for a in range(n):
            for j, chip in enumerate(chips):
                copy(a, 1 + j, (*chip, c), me).wait_recv()
                passed.append(copy(a, 4 + j, (*chip, c), sibling))
                passed[-1].start()
        for a in range(n):
            copy(a, 0, sibling, me).wait_recv()
            for j, chip in enumerate(chips):
                copy(a, 4 + j, (*chip, 1 - c), me).wait_recv()
        for a in range(n):
            for cp in first(a):
                cp.wait_send()
        for cp in passed:
            cp.wait_send()
        for a in range(n):
            mine(a).wait()

    return Ride(shards, [jax.ShapeDtypeStruct((N_DEV, *s.shape), s.dtype) for s in shards], start, finish)


def _chunk(ref, kind, j):
    if kind == "lead":
        return ref.at[j]
    if kind == "rows":
        r = ref.shape[0] // N_DEV
        return ref.at[pl.ds(j * r, r)]
    if kind == "mid":
        r = ref.shape[1] // N_DEV
        return ref.at[:, pl.ds(j * r, r), :]
    if kind == "cols":
        c = ref.shape[1] // N_DEV
        return ref.at[:, pl.ds(j * c, c)]
    return ref


def _chunk_shape(shape, kind):
    if kind == "lead":
        return tuple(shape[1:])
    if kind == "rows":
        return (shape[0] // N_DEV, *shape[1:])
    if kind == "mid":
        return (shape[0], shape[1] // N_DEV, shape[2])
    if kind == "cols":
        return (shape[0], shape[1] // N_DEV)
    return tuple(shape)


def scatter_ride(parts):
    n = len(parts)
    kinds = [k for _, k in parts]

    def setup(ins, outs, sems):
        send_sems, recv_sems, local_sems = sems
        x, y, c = _place()
        me = _dev_index((x, y, c))
        peers = []
        for k in range(1, N_DEV):
            kx, ky, kc = (k >> 2) & 1, (k >> 1) & 1, k & 1
            peers.append((1 - x if kx else x, 1 - y if ky else y, 1 - c if kc else c))

        def copy(a, k, peer):
            return pltpu.make_async_remote_copy(
                src_ref=_chunk(ins[a], kinds[a], _dev_index(peer)), dst_ref=outs[a].at[me],
                send_sem=send_sems.at[a * N_PEERS + k], recv_sem=recv_sems.at[a * N_PEERS + k],
                device_id=peer, device_id_type=MESH)

        def arrival(a, k, peer):
            slot = outs[a].at[_dev_index(peer)]
            return pltpu.make_async_remote_copy(
                src_ref=slot, dst_ref=slot,
                send_sem=send_sems.at[a * N_PEERS + k], recv_sem=recv_sems.at[a * N_PEERS + k],
                device_id=peer, device_id_type=MESH)

        def mine(a):
            return pltpu.make_async_copy(_chunk(ins[a], kinds[a], me), outs[a].at[me], local_sems.at[a])

        return peers, copy, arrival, mine

    def start(ins, outs, sems):
        peers, copy, _, mine = setup(ins, outs, sems)
        for a in range(n):
            mine(a).start()
            for k, peer in enumerate(peers):
                copy(a, k, peer).start()

    def finish(ins, outs, sems):
        peers, copy, arrival, mine = setup(ins, outs, sems)
        for a in range(n):
            for k, peer in enumerate(peers):
                arrival(a, k, peer).wait_recv()
        for a in range(n):
            for k, peer in enumerate(peers):
                copy(a, k, peer).wait_send()
            mine(a).wait()

    shapes = [jax.ShapeDtypeStruct((N_DEV, *_chunk_shape(arr.shape, kind)), arr.dtype) for arr, kind in parts]
    return Ride([arr for arr, _ in parts], shapes, start, finish)


def run_ride(ride, name):
    n_in, n_out = len(ride.ins), len(ride.out_shapes)

    def body(*refs):
        ins, outs, sems = refs[:n_in], refs[n_in:n_in + n_out], refs[n_in + n_out:]
        ride.start(ins, outs, sems)
        ride.finish(ins, outs, sems)

    return pl.pallas_call(
        body, name=name, in_specs=[ANY] * n_in, out_specs=[ANY] * n_out, out_shape=ride.out_shapes,
        scratch_shapes=ride.sems,
    )(*ride.ins)


def _call(body, *, name, grid, in_specs, out_specs, out_shape, args, scratch_shapes=(), ride=None):
    params = pltpu.CompilerParams(dimension_semantics=("arbitrary",) * len(grid), vmem_limit_bytes=VMEM_LIMIT)
    if ride is None:
        outs = pl.pallas_call(body, name=name, grid=grid, in_specs=in_specs, out_specs=out_specs, out_shape=out_shape,
                              scratch_shapes=list(scratch_shapes), compiler_params=params)(*args)
        return outs, []
    n_in, n_out, n_scr = len(in_specs), len(out_specs), len(scratch_shapes)
    r_in, r_out = len(ride.ins), len(ride.out_shapes)

    def hosted(*refs):
        ins, refs = refs[:n_in], refs[n_in:]
        rins, refs = refs[:r_in], refs[r_in:]
        outs, refs = refs[:n_out], refs[n_out:]
        routs, refs = refs[:r_out], refs[r_out:]
        scratch, sems = refs[:n_scr], refs[n_scr:]
        first = pl.program_id(0) == 0
        last = pl.program_id(0) == grid[0] - 1
        for d in range(1, len(grid)):
            first &= pl.program_id(d) == 0
            last &= pl.program_id(d) == grid[d] - 1

        @pl.when(first)
        def _():
            ride.start(rins, routs, sems)

        body(*ins, *outs, *scratch)

        @pl.when(last)
        def _():
            ride.finish(rins, routs, sems)

    outs = pl.pallas_call(
        hosted, name=name + "_ride", grid=grid,
        in_specs=list(in_specs) + [ANY] * r_in, out_specs=list(out_specs) + [ANY] * r_out,
        out_shape=list(out_shape) + ride.out_shapes,
        scratch_shapes=list(scratch_shapes) + ride.sems, compiler_params=params,
    )(*args, *ride.ins)
    return outs[:n_out], outs[n_out:]


def ffn_fwd(h, gain, wgu, wd, ride=None):
    t = h.shape[0]

    def body(h_ref, g_ref, wgu_ref, wd_ref, hn_ref, z_ref, u_ref, acc):
        k = pl.program_id(1)

        @pl.when(k == 0)
        def _():
            u_ref[...] = _rms_fwd(h_ref[...], g_ref[...]).astype(BF16)
            acc[...] = jnp.zeros_like(acc)

        u = u_ref[...]
        g = _dot_nt(u, wgu_ref[0, 0])
        up = _dot_nt(u, wgu_ref[1, 0])
        z_ref[0, 0] = g.astype(BF16)
        z_ref[1, 0] = up.astype(BF16)
        a = g * jax.nn.sigmoid(g) * up
        acc[...] += _dot(a.astype(BF16), wd_ref[0])

        @pl.when(k == 3)
        def _():
            hn_ref[...] = h_ref[...] + acc[...]

    return _call(
        body, name="ffn_fwd", grid=(t // TM, 4), ride=ride,
        in_specs=[pl.BlockSpec((TM, D), lambda i, k: (i, 0)),
                  pl.BlockSpec((1, D), lambda i, k: (0, 0)),
                  pl.BlockSpec((2, 1, FC, D), lambda i, k: (0, k, 0, 0)),
                  pl.BlockSpec((1, FC, D), lambda i, k: (k, 0, 0))],
        out_specs=[pl.BlockSpec((TM, D), lambda i, k: (i, 0)),
                   pl.BlockSpec((2, 1, TM, FC), lambda i, k: (0, k, i, 0)),
                   pl.BlockSpec((TM, D), lambda i, k: (i, 0))],
        out_shape=[jax.ShapeDtypeStruct((t, D), F32),
                   jax.ShapeDtypeStruct((2, 4, t, FC), BF16),
                   jax.ShapeDtypeStruct((t, D), BF16)],
        scratch_shapes=[pltpu.VMEM((TM, D), F32)],
        args=(h, gain, wgu, wd))


def ffn_bwd_x(dh, h, gain, z, wgu, wd, ride=None):
    t = h.shape[0]

    def body(dh_ref, h_ref, g_ref, z_ref, wgu_ref, wd_ref, dhp_ref, dz_ref, a_ref, dgain_ref, dhb, du):
        i, k = pl.program_id(0), pl.program_id(1)

        @pl.when(k == 0)
        def _():
            dhb[...] = dh_ref[...].astype(BF16)
            du[...] = jnp.zeros_like(du)

        @pl.when((k == 0) & (i == 0))
        def _():
            dgain_ref[...] = jnp.zeros_like(dgain_ref)

        da = _dot_nt(dhb[...], wd_ref[0])
        g = z_ref[0, 0].astype(F32)
        up = z_ref[1, 0].astype(F32)
        sg = jax.nn.sigmoid(g)
        silu = g * sg
        a_ref[0] = (silu * up).astype(BF16)
        dg = (da * up * (sg * (1.0 + g * (1.0 - sg)))).astype(BF16)
        dup = (da * silu).astype(BF16)
        dz_ref[0, 0] = dg
        dz_ref[1, 0] = dup
        du[...] += _dot(dg, wgu_ref[0, 0]) + _dot(dup, wgu_ref[1, 0])

        @pl.when(k == 3)
        def _():
            dx, dgain = _rms_bwd(h_ref[...], g_ref[...], du[...])
            dhp_ref[...] = dh_ref[...] + dx
            dgain_ref[...] += dgain

    return _call(
        body, name="ffn_bwd_x", grid=(t // TM, 4), ride=ride,
        in_specs=[pl.BlockSpec((TM, D), lambda i, k: (i, 0)),
                  pl.BlockSpec((TM, D), lambda i, k: (i, 0)),
                  pl.BlockSpec((1, D), lambda i, k: (0, 0)),
                  pl.BlockSpec((2, 1, TM, FC), lambda i, k: (0, k, i, 0)),
                  pl.BlockSpec((2, 1, FC, D), lambda i, k: (0, k, 0, 0)),
                  pl.BlockSpec((1, FC, D), lambda i, k: (k, 0, 0))],
        out_specs=[pl.BlockSpec((TM, D), lambda i, k: (i, 0)),
                   pl.BlockSpec((2, 1, TM, FC), lambda i, k: (0, k, i, 0)),
                   pl.BlockSpec((1, TM, FC), lambda i, k: (k, i, 0)),
                   pl.BlockSpec((1, D), lambda i, k: (0, 0))],
        out_shape=[jax.ShapeDtypeStruct((t, D), F32),
                   jax.ShapeDtypeStruct((2, 4, t, FC), BF16),
                   jax.ShapeDtypeStruct((4, t, FC), BF16),
                   jax.ShapeDtypeStruct((1, D), F32)],
        scratch_shapes=[pltpu.VMEM((TM, D), BF16), pltpu.VMEM((TM, D), F32)],
        args=(dh, h, gain, z, wgu, wd))


def ffn_bwd_w(u, dz, a, dh, ride=None):
    t = u.shape[0]
    steps = t // TM

    def body(u_ref, dz_ref, a_ref, dh_ref, dwgu_ref, dwd_ref, acc_gu, acc_d):
        j = pl.program_id(1)

        @pl.when(j == 0)
        def _():
            acc_gu[...] = jnp.zeros_like(acc_gu)
            acc_d[...] = jnp.zeros_like(acc_d)

        ub = u_ref[...]
        acc_gu[0] += _dot_tn(dz_ref[0, 0], ub)
        acc_gu[1] += _dot_tn(dz_ref[1, 0], ub)
        acc_d[...] += _dot_tn(a_ref[0], dh_ref[...].astype(BF16))

        @pl.when(j == steps - 1)
        def _():
            dwgu_ref[:, 0] = acc_gu[...].astype(BF16)
            dwd_ref[0] = acc_d[...].astype(BF16)

    return _call(
        body, name="ffn_bwd_w", grid=(4, steps), ride=ride,
        in_specs=[pl.BlockSpec((TM, D), lambda k, j: (j, 0)),
                  pl.BlockSpec((2, 1, TM, FC), lambda k, j: (0, k, j, 0)),
                  pl.BlockSpec((1, TM, FC), lambda k, j: (k, j, 0)),
                  pl.BlockSpec((TM, D), lambda k, j: (j, 0))],
        out_specs=[pl.BlockSpec((2, 1, FC, D), lambda k, j: (0, k, 0, 0)),
                   pl.BlockSpec((1, FC, D), lambda k, j: (k, 0, 0))],
        out_shape=[jax.ShapeDtypeStruct((2, 4, FC, D), BF16),
                   jax.ShapeDtypeStruct((4, FC, D), BF16)],
        scratch_shapes=[pltpu.VMEM((2, FC, D), F32), pltpu.VMEM((FC, D), F32)],
        args=(u, dz, a, dh))


def _prev_halo(i):
    return jnp.maximum(i * (TM // HALO) - 1, 0)


def _next_halo(i, t):
    return jnp.minimum((i + 1) * (TM // HALO), t // HALO - 1)


def rms_matmul(h, gain, w, bias, ride=None):
    t = h.shape[0]
    nc = w.shape[2]

    def body(h_ref, g_ref, w_ref, b_ref, z_ref, u_ref):
        u = _rms_fwd(h_ref[...], g_ref[...]).astype(BF16)
        u_ref[...] = u
        for j in range(N_DEV):
            cols = slice(j * nc, (j + 1) * nc)
            z_ref[:, cols] = (_dot(u, w_ref[j]) + b_ref[:, cols]).astype(BF16)

    return _call(
        body, name=f"rms_matmul_{nc}", grid=(t // TM,), ride=ride,
        in_specs=[pl.BlockSpec((TM, D), lambda i: (i, 0)),
                  pl.BlockSpec((1, D), lambda i: (0, 0)),
                  pl.BlockSpec((N_DEV, D, nc), lambda i: (0, 0, 0)),
                  pl.BlockSpec((1, N_DEV * nc), lambda i: (0, 0))],
        out_specs=[pl.BlockSpec((TM, N_DEV * nc), lambda i: (i, 0)),
                   pl.BlockSpec((TM, D), lambda i: (i, 0))],
        out_shape=[jax.ShapeDtypeStruct((t, N_DEV * nc), BF16),
                   jax.ShapeDtypeStruct((t, D), BF16)],
        args=(h, gain, w, bias))


def in_proj_bwd_x(dz, w, h, gain, dh, ride=None):
    t = h.shape[0]
    nc = w.shape[2]

    def body(dz_ref, w_ref, h_ref, g_ref, dh_ref, dhp_ref, dgain_ref, dbias_ref):
        @pl.when(pl.program_id(0) == 0)
        def _():
            dgain_ref[...] = jnp.zeros_like(dgain_ref)
            dbias_ref[...] = jnp.zeros_like(dbias_ref)

        du = jnp.zeros((TM, D), F32)
        for j in range(N_DEV):
            du += _dot_nt(dz_ref[:, j * nc:(j + 1) * nc], w_ref[j])
        dx, dgain = _rms_bwd(h_ref[...], g_ref[...], du)
        dhp_ref[...] = dh_ref[...] + dx
        dgain_ref[...] += dgain
        dbias_ref[...] += jnp.sum(dz_ref[...].astype(F32), axis=0, keepdims=True)

    return _call(
        body, name=f"in_proj_bwd_x_{nc}", grid=(t // TM,), ride=ride,
        in_specs=[pl.BlockSpec((TM, N_DEV * nc), lambda i: (i, 0)),
                  pl.BlockSpec((N_DEV, D, nc), lambda i: (0, 0, 0)),
                  pl.BlockSpec((TM, D), lambda i: (i, 0)),
                  pl.BlockSpec((1, D), lambda i: (0, 0)),
                  pl.BlockSpec((TM, D), lambda i: (i, 0))],
        out_specs=[pl.BlockSpec((TM, D), lambda i: (i, 0)),
                   pl.BlockSpec((1, D), lambda i: (0, 0)),
                   pl.BlockSpec((1, N_DEV * nc), lambda i: (0, 0))],
        out_shape=[jax.ShapeDtypeStruct((t, D), F32),
                   jax.ShapeDtypeStruct((1, D), F32),
                   jax.ShapeDtypeStruct((1, N_DEV * nc), F32)],
        args=(dz, w, h, gain, dh))


def in_proj_bwd_w(u, dz, nc, ride=None):
    t = u.shape[0]
    steps = t // TM

    def body(u_ref, dz_ref, dw_ref, acc):
        s = pl.program_id(0)

        @pl.when(s == 0)
        def _():
            acc[...] = jnp.zeros_like(acc)

        ub = u_ref[...]
        for j in range(N_DEV):
            acc[j] += _dot_tn(ub, dz_ref[:, j * nc:(j + 1) * nc])

        @pl.when(s == steps - 1)
        def _():
            dw_ref[...] = acc[...].astype(BF16)

    return _call(
        body, name=f"in_proj_bwd_w_{nc}", grid=(steps,), ride=ride,
        in_specs=[pl.BlockSpec((TM, D), lambda s: (s, 0)),
                  pl.BlockSpec((TM, N_DEV * nc), lambda s: (s, 0))],
        out_specs=[pl.BlockSpec((N_DEV, D, nc), lambda s: (0, 0, 0))],
        out_shape=[jax.ShapeDtypeStruct((N_DEV, D, nc), BF16)],
        scratch_shapes=[pltpu.VMEM((N_DEV, D, nc), F32)],
        args=(u, dz))


A_TAPS = 3


def a_mix_fwd(z, h, conv, wout, ride=None):
    t = h.shape[0]

    def body(z_ref, zp_ref, h_ref, cw_ref, wo_ref, hn_ref, pad):
        i = pl.program_id(0)
        ph = zp_ref[:, D:2 * D].astype(F32) * zp_ref[:, 2 * D:].astype(F32)
        pad[0:HALO, :] = jnp.where(i == 0, 0.0, ph)
        pad[HALO:, :] = z_ref[:, D:2 * D].astype(F32) * z_ref[:, 2 * D:].astype(F32)
        q = jnp.zeros((TM, D), F32)
        for k in range(A_TAPS):
            off = HALO - (A_TAPS - 1) + k
            q += cw_ref[k:k + 1, :] * pad[off:off + TM, :]
        r = z_ref[:, 0:D].astype(F32) * q
        hn_ref[...] = h_ref[...] + _dot(r.astype(BF16), wo_ref[...])

    return _call(
        body, name="a_mix_fwd", grid=(t // TM,), ride=ride,
        in_specs=[pl.BlockSpec((TM, 3 * D), lambda i: (i, 0)),
                  pl.BlockSpec((HALO, 3 * D), lambda i: (_prev_halo(i), 0)),
                  pl.BlockSpec((TM, D), lambda i: (i, 0)),
                  pl.BlockSpec((A_TAPS, D), lambda i: (0, 0)),
                  pl.BlockSpec((D, D), lambda i: (0, 0))],
        out_specs=[pl.BlockSpec((TM, D), lambda i: (i, 0))],
        out_shape=[jax.ShapeDtypeStruct((t, D), F32)],
        scratch_shapes=[pltpu.VMEM((HALO + TM, D), F32)],
        args=(z, z, h, conv, wout))


def a_mix_bwd(dh, z, conv, wout, ride=None):
    t = dh.shape[0]
    steps = t // TM

    def body(dh_ref, dhn_ref, z_ref, zp_ref, zn_ref, cw_ref, wo_ref, dz_ref, dwo_ref, dcw_ref, pad, dqpad, dwo):
        i = pl.program_id(0)
        last = i == steps - 1

        @pl.when(i == 0)
        def _():
            dwo[...] = jnp.zeros_like(dwo)
            dcw_ref[...] = jnp.zeros_like(dcw_ref)

        ph = zp_ref[:, D:2 * D].astype(F32) * zp_ref[:, 2 * D:].astype(F32)
        pad[0:HALO, :] = jnp.where(i == 0, 0.0, ph)
        c = z_ref[:, D:2 * D].astype(F32)
        v = z_ref[:, 2 * D:].astype(F32)
        pad[HALO:, :] = c * v
        q = jnp.zeros((TM, D), F32)
        for k in range(A_TAPS):
            off = HALO - (A_TAPS - 1) + k
            q += cw_ref[k:k + 1, :] * pad[off:off + TM, :]
        b = z_ref[:, 0:D].astype(F32)
        dhb = dh_ref[...].astype(BF16)
        dwo[...] += _dot_tn((b * q).astype(BF16), dhb)
        dr = _dot_nt(dhb, wo_ref[...])
        dz_ref[:, 0:D] = (dr * q).astype(BF16)
        dq = dr * b
        drn = _dot_nt(dhn_ref[...].astype(BF16), wo_ref[...])
        dqpad[0:TM, :] = dq
        dqpad[TM:, :] = jnp.where(last, 0.0, drn * zn_ref[:, 0:D].astype(F32))
        dp = jnp.zeros((TM, D), F32)
        for k in range(A_TAPS):
            off = A_TAPS - 1 - k
            dp += cw_ref[k:k + 1, :] * dqpad[off:off + TM, :]
            poff = HALO - (A_TAPS - 1) + k
            dcw_ref[k:k + 1, :] += jnp.sum(dq * pad[poff:poff + TM, :], axis=0, keepdims=True)
        dz_ref[:, D:2 * D] = (dp * v).astype(BF16)
        dz_ref[:, 2 * D:] = (dp * c).astype(BF16)

        @pl.when(last)
        def _():
            dwo_ref[...] = dwo[...].astype(BF16)

    return _call(
        body, name="a_mix_bwd", grid=(steps,), ride=ride,
        in_specs=[pl.BlockSpec((TM, D), lambda i: (i, 0)),
                  pl.BlockSpec((HALO, D), lambda i: (_next_halo(i, t), 0)),
                  pl.BlockSpec((TM, 3 * D), lambda i: (i, 0)),
                  pl.BlockSpec((HALO, 3 * D), lambda i: (_prev_halo(i), 0)),
                  pl.BlockSpec((HALO, 3 * D), lambda i: (_next_halo(i, t), 0)),
                  pl.BlockSpec((A_TAPS, D), lambda i: (0, 0)),
                  pl.BlockSpec((D, D), lambda i: (0, 0))],
        out_specs=[pl.BlockSpec((TM, 3 * D), lambda i: (i, 0)),
                   pl.BlockSpec((D, D), lambda i: (0, 0)),
                   pl.BlockSpec((A_TAPS, D), lambda i: (0, 0))],
        out_shape=[jax.ShapeDtypeStruct((t, 3 * D), BF16),
                   jax.ShapeDtypeStruct((D, D), BF16),
                   jax.ShapeDtypeStruct((A_TAPS, D), F32)],
        scratch_shapes=[pltpu.VMEM((HALO + TM, D), F32), pltpu.VMEM((TM + HALO, D), F32), pltpu.VMEM((D, D), F32)],
        args=(dh, dh, z, z, z, conv, wout))


C_TAPS = 31


def _glu(zr):
    return zr[:, 0:D].astype(F32) * jax.nn.sigmoid(zr[:, D:].astype(F32))


def _ln_silu(h2, lg, lb):
    mu = jnp.mean(h2, axis=-1, keepdims=True)
    xc = h2 - mu
    rstd = lax.rsqrt(jnp.mean(xc * xc, axis=-1, keepdims=True) + LN_EPS)
    xn = xc * rstd
    h3 = xn * lg + lb
    s3 = jax.nn.sigmoid(h3)
    return xn, rstd, h3, s3


def _ln_silu_bwd(h2, lg, lb, dh4):
    xn, rstd, h3, s3 = _ln_silu(h2, lg, lb)
    dh3 = dh4 * (s3 * (1.0 + h3 * (1.0 - s3)))
    dxn = dh3 * lg
    dh2 = rstd * (dxn - jnp.mean(dxn, axis=-1, keepdims=True) - xn * jnp.mean(dxn * xn, axis=-1, keepdims=True))
    return dh2, dh3, xn, h3 * s3


def c_mix_fwd(z, h, dw, bdw, lg, lb, w2, b2, ride=None):
    t = h.shape[0]

    def body(z_ref, zp_ref, h_ref, dw_ref, bdw_ref, lg_ref, lb_ref, w2_ref, b2_ref, hn_ref, h2_ref, pad):
        i = pl.program_id(0)
        pad[0:HALO, :] = jnp.where(i == 0, 0.0, _glu(zp_ref))
        pad[HALO:, :] = _glu(z_ref)
        h2 = jnp.zeros((TM, D), F32) + bdw_ref[...]
        for k in range(C_TAPS):
            off = HALO - (C_TAPS - 1) + k
            h2 += dw_ref[k:k + 1, :] * pad[off:off + TM, :]
        h2_ref[...] = h2
        _, _, h3, s3 = _ln_silu(h2, lg_ref[...], lb_ref[...])
        hn_ref[...] = h_ref[...] + _dot((h3 * s3).astype(BF16), w2_ref[...]) + b2_ref[...]

    vec = pl.BlockSpec((1, D), lambda i: (0, 0))
    return _call(
        body, name="c_mix_fwd", grid=(t // TM,), ride=ride,
        in_specs=[pl.BlockSpec((TM, 2 * D), lambda i: (i, 0)),
                  pl.BlockSpec((HALO, 2 * D), lambda i: (_prev_halo(i), 0)),
                  pl.BlockSpec((TM, D), lambda i: (i, 0)),
                  pl.BlockSpec((C_TAPS, D), lambda i: (0, 0)),
                  vec, vec, vec,
                  pl.BlockSpec((D, D), lambda i: (0, 0)),
                  vec],
        out_specs=[pl.BlockSpec((TM, D), lambda i: (i, 0)),
                   pl.BlockSpec((TM, D), lambda i: (i, 0))],
        out_shape=[jax.ShapeDtypeStruct((t, D), F32),
                   jax.ShapeDtypeStruct((t, D), F32)],
        scratch_shapes=[pltpu.VMEM((HALO + TM, D), F32)],
        args=(z, z, h, dw, bdw, lg, lb, w2, b2))


def c_mix_bwd(dh, z, h2, dw, lg, lb, w2, ride=None):
    t = dh.shape[0]
    steps = t // TM

    def body(dh_ref, dhn_ref, z_ref, zp_ref, h2_ref, h2n_ref, dw_ref, lg_ref, lb_ref, w2_ref,
             dz_ref, dw2_ref, db2_ref, dlg_ref, dlb_ref, dbdw_ref, ddw_ref, pad, dpad, dw2):
        i = pl.program_id(0)
        last = i == steps - 1

        @pl.when(i == 0)
        def _():
            for r in (dw2, db2_ref, dlg_ref, dlb_ref, dbdw_ref, ddw_ref):
                r[...] = jnp.zeros_like(r)

        lg, lb = lg_ref[...], lb_ref[...]
        dh = dh_ref[...]
        dhb = dh.astype(BF16)
        dh2, dh3, xn, h4 = _ln_silu_bwd(h2_ref[...], lg, lb, _dot_nt(dhb, w2_ref[...]))
        dw2[...] += _dot_tn(h4.astype(BF16), dhb)
        db2_ref[...] += jnp.sum(dh, axis=0, keepdims=True)
        dlg_ref[...] += jnp.sum(dh3 * xn, axis=0, keepdims=True)
        dlb_ref[...] += jnp.sum(dh3, axis=0, keepdims=True)
        dbdw_ref[...] += jnp.sum(dh2, axis=0, keepdims=True)
        dh2n, _, _, _ = _ln_silu_bwd(h2n_ref[...], lg, lb, _dot_nt(dhn_ref[...].astype(BF16), w2_ref[...]))
        dpad[0:TM, :] = dh2
        dpad[TM:, :] = jnp.where(last, 0.0, dh2n)
        pad[0:HALO, :] = jnp.where(i == 0, 0.0, _glu(zp_ref))
        pad[HALO:, :] = _glu(z_ref)
        dh1 = jnp.zeros((TM, D), F32)
        for k in range(C_TAPS):
            off = C_TAPS - 1 - k
            dh1 += dw_ref[k:k + 1, :] * dpad[off:off + TM, :]
            poff = HALO - (C_TAPS - 1) + k
            ddw_ref[k:k + 1, :] += jnp.sum(dh2 * pad[poff:poff + TM, :], axis=0, keepdims=True)
        a = z_ref[:, 0:D].astype(F32)
        sg = jax.nn.sigmoid(z_ref[:, D:].astype(F32))
        dz_ref[:, 0:D] = (dh1 * sg).astype(BF16)
        dz_ref[:, D:] = (dh1 * a * sg * (1.0 - sg)).astype(BF16)

        @pl.when(last)
        def _():
            dw2_ref[...] = dw2[...].astype(BF16)

    vec = pl.BlockSpec((1, D), lambda i: (0, 0))
    return _call(
        body, name="c_mix_bwd", grid=(steps,), ride=ride,
        in_specs=[pl.BlockSpec((TM, D), lambda i: (i, 0)),
                  pl.BlockSpec((HALO, D), lambda i: (_next_halo(i, t), 0)),
                  pl.BlockSpec((TM, 2 * D), lambda i: (i, 0)),
                  pl.BlockSpec((HALO, 2 * D), lambda i: (_prev_halo(i), 0)),
                  pl.BlockSpec((TM, D), lambda i: (i, 0)),
                  pl.BlockSpec((HALO, D), lambda i: (_next_halo(i, t), 0)),
                  pl.BlockSpec((C_TAPS, D), lambda i: (0, 0)),
                  vec, vec,
                  pl.BlockSpec((D, D), lambda i: (0, 0), pipeline_mode=pl.Buffered(1))],
        out_specs=[pl.BlockSpec((TM, 2 * D), lambda i: (i, 0)),
                   pl.BlockSpec((D, D), lambda i: (0, 0)),
                   vec, vec, vec, vec,
                   pl.BlockSpec((C_TAPS, D), lambda i: (0, 0))],
        out_shape=[jax.ShapeDtypeStruct((t, 2 * D), BF16),
                   jax.ShapeDtypeStruct((D, D), BF16)]
                  + [jax.ShapeDtypeStruct((1, D), F32)] * 4
                  + [jax.ShapeDtypeStruct((C_TAPS, D), F32)],
        scratch_shapes=[pltpu.VMEM((HALO + TM, D), F32), pltpu.VMEM((TM + HALO, D), F32), pltpu.VMEM((D, D), F32)],
        args=(dh, dh, z, z, h2, h2, dw, lg, lb, w2))


POOL_WINDOWS = (2, 4, 8, 16)
GW = D // len(POOL_WINDOWS)


def _pool_mixed(pad, g, w, inv_cnt):
    cols = slice(g * GW, (g + 1) * GW)
    s = pad[HALO:HALO + TM, cols]
    u = s
    for j in range(1, w):
        s = s + pad[HALO - j:HALO - j + TM, cols]
    return s * inv_cnt - u


def _inv_cnt(i, w):
    row = i * TM + lax.broadcasted_iota(jnp.int32, (TM, 1), 0)
    return 1.0 / jnp.minimum(row + 1, w).astype(F32)


def b_mix_fwd(h, gain, wg, scale, ride=None):
    t = h.shape[0]

    def body(h_ref, hp_ref, g_ref, wg_ref, sc_ref, hn_ref, pad):
        i = pl.program_id(0)
        gain = g_ref[...]
        pad[0:HALO, :] = jnp.where(i == 0, 0.0, _rms_fwd(hp_ref[...], gain))
        pad[HALO:, :] = _rms_fwd(h_ref[...], gain)
        for g, w in enumerate(POOL_WINDOWS):
            cols = slice(g * GW, (g + 1) * GW)
            mixed = _pool_mixed(pad, g, w, _inv_cnt(i, w))
            y = _dot(mixed.astype(BF16), wg_ref[g])
            hn_ref[:, cols] = h_ref[:, cols] + y * sc_ref[:, cols]

    return _call(
        body, name="b_mix_fwd", grid=(t // TM,), ride=ride,
        in_specs=[pl.BlockSpec((TM, D), lambda i: (i, 0)),
                  pl.BlockSpec((HALO, D), lambda i: (_prev_halo(i), 0)),
                  pl.BlockSpec((1, D), lambda i: (0, 0)),
                  pl.BlockSpec((4, GW, GW), lambda i: (0, 0, 0)),
                  pl.BlockSpec((1, D), lambda i: (0, 0))],
        out_specs=[pl.BlockSpec((TM, D), lambda i: (i, 0))],
        out_shape=[jax.ShapeDtypeStruct((t, D), F32)],
        scratch_shapes=[pltpu.VMEM((HALO + TM, D), F32)],
        args=(h, h, gain, wg, scale))


def b_mix_bwd(dh, h, gain, wg, scale, ride=None):
    t = h.shape[0]
    steps = t // TM

    def body(dh_ref, dhn_ref, h_ref, hp_ref, g_ref, wg_ref, sc_ref, dhp_ref, dgain_ref, dwg_ref, dsc_ref, pad, dpad, du):
        i = pl.program_id(0)
        last = i == steps - 1

        @pl.when(i == 0)
        def _():
            for r in (dgain_ref, dwg_ref, dsc_ref):
                r[...] = jnp.zeros_like(r)

        gain = g_ref[...]
        pad[0:HALO, :] = jnp.where(i == 0, 0.0, _rms_fwd(hp_ref[...], gain))
        pad[HALO:, :] = _rms_fwd(h_ref[...], gain)
        for g, w in enumerate(POOL_WINDOWS):
            cols = slice(g * GW, (g + 1) * GW)
            inv_cnt = _inv_cnt(i, w)
            mixed = _pool_mixed(pad, g, w, inv_cnt).astype(BF16)
            dh = dh_ref[:, cols]
            dsc_ref[:, cols] += jnp.sum(dh * _dot(mixed, wg_ref[g]), axis=0, keepdims=True)
            dy = (dh * sc_ref[:, cols]).astype(BF16)
            dwg_ref[g] += _dot_tn(mixed, dy)
            dm = _dot_nt(dy, wg_ref[g])
            dmn = _dot_nt((dhn_ref[:, cols] * sc_ref[:, cols]).astype(BF16), wg_ref[g])
            dpad[0:TM, cols] = dm * inv_cnt
            dpad[TM:, cols] = jnp.where(last, 0.0, dmn * (1.0 / w))
            s = dpad[0:TM, cols]
            for j in range(1, w):
                s = s + dpad[j:j + TM, cols]
            du[:, cols] = s - dm
        dx, dgain = _rms_bwd(h_ref[...], gain, du[...])
        dhp_ref[...] = dh_ref[...] + dx
        dgain_ref[...] += dgain

    return _call(
        body, name="b_mix_bwd", grid=(steps,), ride=ride,
        in_specs=[pl.BlockSpec((TM, D), lambda i: (i, 0)),
                  pl.BlockSpec((HALO, D), lambda i: (_next_halo(i, t), 0)),
                  pl.BlockSpec((TM, D), lambda i: (i, 0)),
                  pl.BlockSpec((HALO, D), lambda i: (_prev_halo(i), 0)),
                  pl.BlockSpec((1, D), lambda i: (0, 0)),
                  pl.BlockSpec((4, GW, GW), lambda i: (0, 0, 0)),
                  pl.BlockSpec((1, D), lambda i: (0, 0))],
        out_specs=[pl.BlockSpec((TM, D), lambda i: (i, 0)),
                   pl.BlockSpec((1, D), lambda i: (0, 0)),
                   pl.BlockSpec((4, GW, GW), lambda i: (0, 0, 0)),
                   pl.BlockSpec((1, D), lambda i: (0, 0))],
        out_shape=[jax.ShapeDtypeStruct((t, D), F32),
                   jax.ShapeDtypeStruct((1, D), F32),
                   jax.ShapeDtypeStruct((4, GW, GW), F32),
                   jax.ShapeDtypeStruct((1, D), F32)],
        scratch_shapes=[pltpu.VMEM((HALO + TM, D), F32), pltpu.VMEM((TM + HALO, D), F32), pltpu.VMEM((TM, D), F32)],
        args=(dh, dh, h, h, gain, wg, scale))


LOSS_LANES = 128


def loss_head(h, gain, target):
    t = h.shape[0]

    def body(h_ref, g_ref, tg_ref, loss_ref, dh_ref, dgain_ref):
        @pl.when(pl.program_id(0) == 0)
        def _():
            loss_ref[...] = jnp.zeros_like(loss_ref)
            dgain_ref[...] = jnp.zeros_like(dgain_ref)

        x, gain = h_ref[...], g_ref[...]
        err = _rms_fwd(x, gain) - tg_ref[...]
        per_row = jnp.mean(err * err, axis=-1, keepdims=True)
        loss_ref[...] += jnp.broadcast_to(0.5 * jnp.sum(per_row, axis=0, keepdims=True), (1, LOSS_LANES))
        dx, dgain = _rms_bwd(x, gain, err * (1.0 / D))
        dh_ref[...] = dx
        dgain_ref[...] += dgain

    outs, _ = _call(
        body, name="loss_head", grid=(t // TM,),
        in_specs=[pl.BlockSpec((TM, D), lambda i: (i, 0)),
                  pl.BlockSpec((1, D), lambda i: (0, 0)),
                  pl.BlockSpec((TM, D), lambda i: (i, 0))],
        out_specs=[pl.BlockSpec((1, LOSS_LANES), lambda i: (0, 0)),
                   pl.BlockSpec((TM, D), lambda i: (i, 0)),
                   pl.BlockSpec((1, D), lambda i: (0, 0))],
        out_shape=[jax.ShapeDtypeStruct((1, LOSS_LANES), F32),
                   jax.ShapeDtypeStruct((t, D), F32),
                   jax.ShapeDtypeStruct((1, D), F32)],
        args=(h, gain, target))
    return outs


ADAM_LR = 0.001
ADAM_B1 = 0.9
ADAM_B2 = 0.999
ADAM_EPS = 1e-08
ADAM_WD = 0.01
ADAM_STEP = 10


def cast_all(arrays):
    def body(*refs):
        for src, dst in zip(refs[:len(arrays)], refs[len(arrays):]):
            dst[...] = src[...].astype(BF16)

    return pl.pallas_call(
        body, name="cast_all", out_shape=[jax.ShapeDtypeStruct(a.shape, BF16) for a in arrays],
        compiler_params=pltpu.CompilerParams(vmem_limit_bytes=VMEM_LIMIT),
    )(*arrays)


def _adam_math(w, m, v, g):
    m = ADAM_B1 * m + (1.0 - ADAM_B1) * g
    v = ADAM_B2 * v + (1.0 - ADAM_B2) * (g * g)
    m_hat = m / (1.0 - ADAM_B1 ** ADAM_STEP)
    v_hat = v / (1.0 - ADAM_B2 ** ADAM_STEP)
    return -ADAM_LR * (m_hat / (jnp.sqrt(v_hat) + ADAM_EPS) + ADAM_WD * w), m, v


def adamw(w, m, v, gparts, rb):
    r, c = w.shape

    def body(w_ref, m_ref, v_ref, gp_ref, g_ref, d_ref, nm_ref, nv_ref):
        g = gp_ref[0].astype(F32)
        for s in range(1, N_DEV):
            g = g + gp_ref[s].astype(F32)
        g_ref[...] = g
        d_ref[...], nm_ref[...], nv_ref[...] = _adam_math(w_ref[...], m_ref[...], v_ref[...], g)

    blk = pl.BlockSpec((rb, c), lambda i: (i, 0))
    outs, _ = _call(
        body, name=f"adamw_{r}x{c}", grid=(r // rb,),
        in_specs=[blk, blk, blk, pl.BlockSpec((N_DEV, rb, c), lambda i: (0, i, 0))],
        out_specs=[blk] * 4,
        out_shape=[jax.ShapeDtypeStruct((r, c), F32)] * 4,
        args=(w, m, v, gparts))
    return outs


def adamw_vectors(ws, ms, vs, gparts):
    nv = len(ws)

    def body(*refs):
        w_refs, m_refs, v_refs = refs[:nv], refs[nv:2 * nv], refs[2 * nv:3 * nv]
        gp_ref = refs[3 * nv]
        outs = refs[3 * nv + 1:]
        g_refs, d_refs, nm_refs, nv_refs = outs[:nv], outs[nv:2 * nv], outs[2 * nv:3 * nv], outs[3 * nv:]
        row = 0
        for i in range(nv):
            for part in range(w_refs[i].shape[1] // D):
                cols = slice(part * D, (part + 1) * D)
                g = gp_ref[0, row:row + 1, :]
                for s in range(1, N_DEV):
                    g = g + gp_ref[s, row:row + 1, :]
                g_refs[i][:, cols] = g
                d_refs[i][:, cols], nm_refs[i][:, cols], nv_refs[i][:, cols] = _adam_math(
                    w_refs[i][:, cols], m_refs[i][:, cols], v_refs[i][:, cols], g)
                row += 1

    shapes = [jax.ShapeDtypeStruct(w.shape, F32) for w in ws]
    outs = pl.pallas_call(body, name="adamw_vectors", out_shape=shapes * 4)(*ws, *ms, *vs, gparts)
    return outs[:nv], outs[nv:2 * nv], outs[2 * nv:3 * nv], outs[3 * nv:]


WEIGHTS = ["ln1_0", "a0_w_in", "a0_conv", "a0_w_out", "ln2_0", "ffn0_w_gu", "ffn0_w_down",
           "ln1_1", "b1_w_grp", "b1_scale", "ln2_1", "ffn1_w_gu", "ffn1_w_down",
           "ln1_2", "c2_w_pw1", "c2_b_pw1", "c2_dw", "c2_b_dw", "c2_ln_g", "c2_ln_b", "c2_w_pw2", "c2_b_pw2",
           "ln2_2", "ffn2_w_gu", "ffn2_w_down",
           "ln1_3", "a3_w_in", "a3_conv", "a3_w_out", "ln2_3", "ffn3_w_gu", "ffn3_w_down", "ln_f"]
SHARDED = {"a0_w_in": ("lead", 256), "a0_conv": ("cols", A_TAPS), "a0_w_out": ("rows", 128),
           "ffn0_w_gu": ("lead", 176), "ffn0_w_down": ("rows", 176),
           "b1_w_grp": ("mid", 128),
           "ffn1_w_gu": ("lead", 176), "ffn1_w_down": ("rows", 176),
           "c2_w_pw1": ("lead", 256), "c2_dw": ("cols", C_TAPS), "c2_w_pw2": ("rows", 128),
           "ffn2_w_gu": ("lead", 176), "ffn2_w_down": ("rows", 176),
           "a3_w_in": ("lead", 256), "a3_conv": ("cols", A_TAPS), "a3_w_out": ("rows", 128),
           "ffn3_w_gu": ("lead", 176), "ffn3_w_down": ("rows", 176)}
REPL = [n for n in WEIGHTS if n not in SHARDED]
REPL_ROWS = 16
GATHER_PLAN = {"first": ["a0_w_in", "a0_w_out", "a0_conv"],
               "in0": ["ffn0_w_gu"], "mix0": ["ffn0_w_down"],
               "ffn0": ["b1_w_grp", "ffn1_w_gu", "ffn1_w_down"],
               "mix1": ["c2_w_pw1"],
               "ffn1": ["c2_w_pw2", "c2_dw", "ffn2_w_gu", "ffn2_w_down"],
               "in2": ["a3_w_in", "a3_w_out", "a3_conv"],
               "mix2": ["ffn3_w_gu", "ffn3_w_down"]}
SCATTER_PLAN = {"mixb3": ["ffn3_w_down"], "inw3": ["a3_w_out", "a3_conv"],
                "ffnx2": ["ffn3_w_gu"], "ffnw2": ["a3_w_in"],
                "mixb2": ["ffn2_w_gu", "ffn2_w_down"], "inw2": ["c2_w_pw2", "c2_dw"],
                "ffnx1": ["c2_w_pw1"],
                "ffnx0": ["ffn1_w_gu", "b1_w_grp"], "ffnw0": ["ffn1_w_down"],
                "mixb0": ["ffn0_w_down"], "inw0": ["ffn0_w_gu"], "inx0": ["a0_w_out", "a0_conv", "a0_w_in"],
                "last": ["repl"]}


def _step(p):
    vec = lambda n: p[n].reshape(1, -1)
    x, target = p["x"][0], p["loss_target"][0]

    names = list(SHARDED)
    stored = lambda n, a: a.T if n.endswith("w_gu") else a
    shard = dict(zip(names, cast_all([stored(n, p[n]) for n in names])))
    full = {}

    def gather(slot):
        return gather_ride([shard[n] for n in GATHER_PLAN[slot]])

    def landed(slot, outs):
        full.update(zip(GATHER_PLAN[slot], outs))

    def conv_full(n):
        k = full[n].shape[1]
        return full[n].transpose(1, 0, 2).reshape(k, D).astype(F32)

    def wgu(i):
        return full[f"ffn{i}_w_gu"].reshape(2, 4, FC, D)

    def wd(i):
        return full[f"ffn{i}_w_down"].reshape(4, FC, D)

    landed("first", run_ride(gather("first"), "gather_first"))
    no_bias = jnp.zeros((1, 3 * D), F32)
    h = [x]
    saved = {}
    conv, wout = {}, {}

    (z, u), got = rms_matmul(h[-1], vec("ln1_0"), full["a0_w_in"], no_bias, ride=gather("in0"))
    landed("in0", got)
    conv[0], wout[0] = conv_full("a0_conv"), full["a0_w_out"].reshape(D, D)
    (hm,), got = a_mix_fwd(z, h[-1], conv[0], wout[0], ride=gather("mix0"))
    landed("mix0", got)
    saved["mix0"] = (z, u)
    h.append(hm)
    (hn, zf, uf), got = ffn_fwd(hm, vec("ln2_0"), wgu(0), wd(0), ride=gather("ffn0"))
    landed("ffn0", got)
    saved["ffn0"] = (zf, uf)
    h.append(hn)

    wgrp = full["b1_w_grp"].transpose(1, 0, 2, 3).reshape(4, GW, GW)
    (hm,), got = b_mix_fwd(h[-1], vec("ln1_1"), wgrp, vec("b1_scale"), ride=gather("mix1"))
    landed("mix1", got)
    h.append(hm)
    (hn, zf, uf), got = ffn_fwd(hm, vec("ln2_1"), wgu(1), wd(1), ride=gather("ffn1"))
    landed("ffn1", got)
    saved["ffn1"] = (zf, uf)
    h.append(hn)

    (z, u), got = rms_matmul(h[-1], vec("ln1_2"), full["c2_w_pw1"], vec("c2_b_pw1"), ride=gather("in2"))
    landed("in2", got)
    cdw, wpw2 = conv_full("c2_dw"), full["c2_w_pw2"].reshape(D, D)
    (hm, h2), got = c_mix_fwd(z, h[-1], cdw, vec("c2_b_dw"), vec("c2_ln_g"), vec("c2_ln_b"), wpw2, vec("c2_b_pw2"),
                              ride=gather("mix2"))
    landed("mix2", got)
    saved["mix2"] = (z, u, h2)
    h.append(hm)
    (hn, zf, uf), _ = ffn_fwd(hm, vec("ln2_2"), wgu(2), wd(2))
    saved["ffn2"] = (zf, uf)
    h.append(hn)

    (z, u), _ = rms_matmul(h[-1], vec("ln1_3"), full["a3_w_in"], no_bias)
    conv[3], wout[3] = conv_full("a3_conv"), full["a3_w_out"].reshape(D, D)
    (hm,), _ = a_mix_fwd(z, h[-1], conv[3], wout[3])
    saved["mix3"] = (z, u)
    h.append(hm)
    (hn, zf, uf), _ = ffn_fwd(hm, vec("ln2_3"), wgu(3), wd(3))
    saved["ffn3"] = (zf, uf)
    h.append(hn)

    loss_lanes, dh, g_lnf = loss_head(h[-1], vec("ln_f"), target)

    g = {"ln_f": g_lnf}
    recv = {}

    def repl_rows():
        return jnp.concatenate([g[n].reshape(-1, D) for n in REPL], axis=0)

    def scatter(slot):
        parts = []
        for n in SCATTER_PLAN[slot]:
            parts.append((repl_rows(), "all") if n == "repl" else (g[n], SHARDED[n][0]))
        return scatter_ride(parts)

    def arrived(slot, outs):
        recv.update(zip(SCATTER_PLAN[slot], outs))

    for i in (3, 2, 1, 0):
        zf, uf = saved[f"ffn{i}"]
        dh_in = dh
        (dh, dzf, a, g[f"ln2_{i}"]), got = ffn_bwd_x(dh_in, h[2 * i + 1], vec(f"ln2_{i}"), zf, wgu(i), wd(i),
                                                      ride=scatter(f"ffnx{i}") if f"ffnx{i}" in SCATTER_PLAN else None)
        if got:
            arrived(f"ffnx{i}", got)
        (dwgu, dwd), got = ffn_bwd_w(uf, dzf, a, dh_in, ride=scatter(f"ffnw{i}") if f"ffnw{i}" in SCATTER_PLAN else None)
        if got:
            arrived(f"ffnw{i}", got)
        g[f"ffn{i}_w_gu"], g[f"ffn{i}_w_down"] = dwgu.reshape(N_DEV, FC, D), dwd.reshape(FF, D)
        hin = h[2 * i]
        if i in (0, 3):
            z, u = saved[f"mix{i}"]
            (dz, g[f"a{i}_w_out"], g[f"a{i}_conv"]), got = a_mix_bwd(dh, z, conv[i], wout[i], ride=scatter(f"mixb{i}"))
            arrived(f"mixb{i}", got)
            (g[f"a{i}_w_in"],), got = in_proj_bwd_w(u, dz, 384, ride=scatter(f"inw{i}"))
            arrived(f"inw{i}", got)
            (dh, g[f"ln1_{i}"], _), got = in_proj_bwd_x(dz, full[f"a{i}_w_in"], hin, vec(f"ln1_{i}"), dh,
                                                       ride=scatter(f"inx{i}") if f"inx{i}" in SCATTER_PLAN else None)
            if got:
                arrived(f"inx{i}", got)
        elif i == 1:
            (dh, g["ln1_1"], g["b1_w_grp"], g["b1_scale"]), _ = b_mix_bwd(dh, hin, vec("ln1_1"), wgrp, vec("b1_scale"))
        else:
            z, u, h2 = saved["mix2"]
            (dz, g["c2_w_pw2"], g["c2_b_pw2"], g["c2_ln_g"], g["c2_ln_b"], g["c2_b_dw"], g["c2_dw"]), got = c_mix_bwd(
                dh, z, h2, cdw, vec("c2_ln_g"), vec("c2_ln_b"), wpw2, ride=scatter("mixb2"))
            arrived("mixb2", got)
            (g["c2_w_pw1"],), got = in_proj_bwd_w(u, dz, 256, ride=scatter("inw2"))
            arrived("inw2", got)
            (dh, g["ln1_2"], g["c2_b_pw1"]), _ = in_proj_bwd_x(dz, full["c2_w_pw1"], hin, vec("ln1_2"), dh)
    grad_x = dh[None]
    arrived("last", run_ride(scatter("last"), "scatter_last"))

    grad, delta, new_m, new_v = {}, {}, {}, {}
    for n, (_, rb) in SHARDED.items():
        shape = p[n].shape
        two_d = lambda a: stored(n, a.reshape(-1, shape[-1]))
        w2 = two_d(p[n])
        outs = adamw(w2, two_d(p["m_" + n]), two_d(p["v_" + n]), recv[n].reshape(N_DEV, *w2.shape), rb)
        grad[n], delta[n], new_m[n], new_v[n] = [stored(n, o).reshape(shape) for o in outs]
    outs = adamw_vectors([vec(n) for n in REPL], [vec("m_" + n) for n in REPL], [vec("v_" + n) for n in REPL], recv["repl"])
    for res, o in zip((grad, delta, new_m, new_v), outs):
        res.update({n: a.reshape(p[n].shape) for n, a in zip(REPL, o)})

    loss = lax.psum(loss_lanes[0, 0], ("x", "y", "c"))
    return (loss, grad_x, *[grad[n] for n in WEIGHTS], *[delta[n] for n in WEIGHTS],
            *[new_m[n] for n in WEIGHTS], *[new_v[n] for n in WEIGHTS])


def kernel(x, ln1_0, a0_w_in, a0_conv, a0_w_out, ln2_0, ffn0_w_gu, ffn0_w_down, ln1_1, b1_w_grp, b1_scale, ln2_1, ffn1_w_gu, ffn1_w_down, ln1_2, c2_w_pw1, c2_b_pw1, c2_dw, c2_b_dw, c2_ln_g, c2_ln_b, c2_w_pw2, c2_b_pw2, ln2_2, ffn2_w_gu, ffn2_w_down, ln1_3, a3_w_in, a3_conv, a3_w_out, ln2_3, ffn3_w_gu, ffn3_w_down, ln_f, loss_target, m_ln1_0, m_a0_w_in, m_a0_conv, m_a0_w_out, m_ln2_0, m_ffn0_w_gu, m_ffn0_w_down, m_ln1_1, m_b1_w_grp, m_b1_scale, m_ln2_1, m_ffn1_w_gu, m_ffn1_w_down, m_ln1_2, m_c2_w_pw1, m_c2_b_pw1, m_c2_dw, m_c2_b_dw, m_c2_ln_g, m_c2_ln_b, m_c2_w_pw2, m_c2_b_pw2, m_ln2_2, m_ffn2_w_gu, m_ffn2_w_down, m_ln1_3, m_a3_w_in, m_a3_conv, m_a3_w_out, m_ln2_3, m_ffn3_w_gu, m_ffn3_w_down, m_ln_f, v_ln1_0, v_a0_w_in, v_a0_conv, v_a0_w_out, v_ln2_0, v_ffn0_w_gu, v_ffn0_w_down, v_ln1_1, v_b1_w_grp, v_b1_scale, v_ln2_1, v_ffn1_w_gu, v_ffn1_w_down, v_ln1_2, v_c2_w_pw1, v_c2_b_pw1, v_c2_dw, v_c2_b_dw, v_c2_ln_g, v_c2_ln_b, v_c2_w_pw2, v_c2_b_pw2, v_ln2_2, v_ffn2_w_gu, v_ffn2_w_down, v_ln1_3, v_a3_w_in, v_a3_conv, v_a3_w_out, v_ln2_3, v_ffn3_w_gu, v_ffn3_w_down, v_ln_f):
    return _step(dict(locals()))
```

```python
import jax
import jax.numpy as jnp
from jax import lax
from jax.experimental import pallas as pl
from jax.experimental.pallas import tpu as pltpu

F32 = jnp.float32
BF16 = jnp.bfloat16

N_DEV = 8
D = 1024
FF = 2816
FC = FF // 4
RMS_EPS = 1e-6
LN_EPS = 1e-5
TM = 512
HALO = 32
VMEM_LIMIT = 60 * 1024 * 1024

NT = (((1,), (1,)), ((), ()))
TN = (((0,), (0,)), ((), ()))
MESH = pl.DeviceIdType.MESH
ANY = pl.BlockSpec(memory_space=pl.ANY)
N_PEERS = N_DEV - 1


def _dot(a, b):
    return jnp.dot(a, b, preferred_element_type=F32)


def _dot_nt(a, b):
    return lax.dot_general(a, b, NT, preferred_element_type=F32)


def _dot_tn(a, b):
    return lax.dot_general(a, b, TN, preferred_element_type=F32)


def _rms_fwd(x, gain):
    r = lax.rsqrt(jnp.mean(x * x, axis=-1, keepdims=True) + RMS_EPS)
    return x * r * gain


def _rms_bwd(x, gain, du):
    r = lax.rsqrt(jnp.mean(x * x, axis=-1, keepdims=True) + RMS_EPS)
    xhat = x * r
    dgain = jnp.sum(du * xhat, axis=0, keepdims=True)
    dxhat = du * gain
    dx = r * (dxhat - xhat * jnp.mean(dxhat * xhat, axis=-1, keepdims=True))
    return dx, dgain


def _dev_index(p):
    return 4 * p[0] + 2 * p[1] + p[2]


def _place():
    return lax.axis_index("x"), lax.axis_index("y"), lax.axis_index("c")


class Ride:
    def __init__(self, ins, out_shapes, start, finish):
        self.ins, self.out_shapes, self.start, self.finish = list(ins), list(out_shapes), start, finish
        n = len(self.ins)
        self.sems = [pltpu.SemaphoreType.DMA((n * N_PEERS,)), pltpu.SemaphoreType.DMA((n * N_PEERS,)),
                     pltpu.SemaphoreType.DMA((n,))]


def gather_ride(shards, kinds):
    n = len(shards)

    def setup(ins, outs, sems):
        send_sems, recv_sems, local_sems = sems
        x, y, c = _place()
        chips = [(1 - x, y), (x, 1 - y), (1 - x, 1 - y)]

        def copy(a, k, block, to, src=None):
            slot = _chunk(outs[a], kinds[a], _dev_index(block))
            return pltpu.make_async_remote_copy(
                src_ref=slot if src is None else src, dst_ref=slot,
                send_sem=send_sems.at[a * N_PEERS + k], recv_sem=recv_sems.at[a * N_PEERS + k],
                device_id=to, device_id_type=MESH)

        def mine(a):
            return pltpu.make_async_copy(ins[a], _chunk(outs[a], kinds[a], _dev_index((x, y, c))), local_sems.at[a])

        def first(a):
            return [copy(a, 0, (x, y, c), (x, y, 1 - c), src=ins[a])] + [
                copy(a, 1 + j, (x, y, c), (*chip, c), src=ins[a]) for j, chip in enumerate(chips)]

        return (x, y, c), chips, copy, mine, first

    def start(ins, outs, sems):
        _, _, _, mine, first = setup(ins, outs, sems)
        for a in range(n):
            mine(a).start()
            for cp in first(a):
                cp.start()

    def finish(ins, outs, sems):
        (x, y, c), chips, copy, mine, first = setup(ins, outs, sems)
        me, sibling = (x, y, c), (x, y, 1 - c)
        passed = []
        for a in range(n):
            for j, chip in enumerate(chips):
                copy(a, 1 + j, (*chip, c), me).wait_recv()
                passed.append(copy(a, 4 + j, (*chip, c), sibling))
                passed[-1].start()
        for a in range(n):
            copy(a, 0, sibling, me).wait_recv()
            for j, chip in enumerate(chips):
                copy(a, 4 + j, (*chip, 1 - c), me).wait_recv()
        for a in range(n):
            for cp in first(a):
                cp.wait_send()
        for cp in passed:
            cp.wait_send()
        for a in range(n):
            mine(a).wait()

    shapes = [(N_DEV, *s.shape) if kind == "lead" else (s.shape[0], N_DEV * s.shape[1]) for s, kind in zip(shards, kinds)]
    return Ride(shards, [jax.ShapeDtypeStruct(shape, s.dtype) for shape, s in zip(shapes, shards)], start, finish)


def _chunk(ref, kind, j):
    if kind == "lead":
        return ref.at[j]
    if kind == "rows":
        r = ref.shape[0] // N_DEV
        return ref.at[pl.ds(j * r, r)]
    if kind == "mid":
        r = ref.shape[1] // N_DEV
        return ref.at[:, pl.ds(j * r, r), :]
    if kind == "cols":
        c = ref.shape[1] // N_DEV
        return ref.at[:, pl.ds(j * c, c)]
    return ref


def _chunk_shape(shape, kind):
    if kind == "lead":
        return tuple(shape[1:])
    if kind == "rows":
        return (shape[0] // N_DEV, *shape[1:])
    if kind == "mid":
        return (shape[0], shape[1] // N_DEV, shape[2])
    if kind == "cols":
        return (shape[0], shape[1] // N_DEV)
    return tuple(shape)


def scatter_ride(parts):
    n = len(parts)
    kinds = [k for _, k in parts]

    def setup(ins, outs, sems):
        send_sems, recv_sems, local_sems = sems
        x, y, c = _place()
        me = _dev_index((x, y, c))
        peers = []
        for k in range(1, N_DEV):
            kx, ky, kc = (k >> 2) & 1, (k >> 1) & 1, k & 1
            peers.append((1 - x if kx else x, 1 - y if ky else y, 1 - c if kc else c))

        def copy(a, k, peer):
            return pltpu.make_async_remote_copy(
                src_ref=_chunk(ins[a], kinds[a], _dev_index(peer)), dst_ref=outs[a].at[me],
                send_sem=send_sems.at[a * N_PEERS + k], recv_sem=recv_sems.at[a * N_PEERS + k],
                device_id=peer, device_id_type=MESH)

        def arrival(a, k, peer):
            slot = outs[a].at[_dev_index(peer)]
            return pltpu.make_async_remote_copy(
                src_ref=slot, dst_ref=slot,
                send_sem=send_sems.at[a * N_PEERS + k], recv_sem=recv_sems.at[a * N_PEERS + k],
                device_id=peer, device_id_type=MESH)

        def mine(a):
            return pltpu.make_async_copy(_chunk(ins[a], kinds[a], me), outs[a].at[me], local_sems.at[a])

        return peers, copy, arrival, mine

    def start(ins, outs, sems):
        peers, copy, _, mine = setup(ins, outs, sems)
        for a in range(n):
            mine(a).start()
            for k, peer in enumerate(peers):
                copy(a, k, peer).start()

    def finish(ins, outs, sems):
        peers, copy, arrival, mine = setup(ins, outs, sems)
        for a in range(n):
            for k, peer in enumerate(peers):
                arrival(a, k, peer).wait_recv()
        for a in range(n):
            for k, peer in enumerate(peers):
                copy(a, k, peer).wait_send()
            mine(a).wait()

    shapes = [jax.ShapeDtypeStruct((N_DEV, *_chunk_shape(arr.shape, kind)), arr.dtype) for arr, kind in parts]
    return Ride([arr for arr, _ in parts], shapes, start, finish)


def run_ride(ride, name):
    n_in, n_out = len(ride.ins), len(ride.out_shapes)

    def body(*refs):
        ins, outs, sems = refs[:n_in], refs[n_in:n_in + n_out], refs[n_in + n_out:]
        ride.start(ins, outs, sems)
        ride.finish(ins, outs, sems)

    return pl.pallas_call(
        body, name=name, in_specs=[ANY] * n_in, out_specs=[ANY] * n_out, out_shape=ride.out_shapes,
        scratch_shapes=ride.sems,
    )(*ride.ins)


def _call(body, *, name, grid, in_specs, out_specs, out_shape, args, scratch_shapes=(), ride=None):
    params = pltpu.CompilerParams(dimension_semantics=("arbitrary",) * len(grid), vmem_limit_bytes=VMEM_LIMIT)
    if ride is None:
        outs = pl.pallas_call(body, name=name, grid=grid, in_specs=in_specs, out_specs=out_specs, out_shape=out_shape,
                              scratch_shapes=list(scratch_shapes), compiler_params=params)(*args)
        return outs, []
    n_in, n_out, n_scr = len(in_specs), len(out_specs), len(scratch_shapes)
    r_in, r_out = len(ride.ins), len(ride.out_shapes)

    def hosted(*refs):
        ins, refs = refs[:n_in], refs[n_in:]
        rins, refs = refs[:r_in], refs[r_in:]
        outs, refs = refs[:n_out], refs[n_out:]
        routs, refs = refs[:r_out], refs[r_out:]
        scratch, sems = refs[:n_scr], refs[n_scr:]
        first = pl.program_id(0) == 0
        last = pl.program_id(0) == grid[0] - 1
        for d in range(1, len(grid)):
            first &= pl.program_id(d) == 0
            last &= pl.program_id(d) == grid[d] - 1

        @pl.when(first)
        def _():
            ride.start(rins, routs, sems)

        body(*ins, *outs, *scratch)

        @pl.when(last)
        def _():
            ride.finish(rins, routs, sems)

    outs = pl.pallas_call(
        hosted, name=name + "_ride", grid=grid,
        in_specs=list(in_specs) + [ANY] * r_in, out_specs=list(out_specs) + [ANY] * r_out,
        out_shape=list(out_shape) + ride.out_shapes,
        scratch_shapes=list(scratch_shapes) + ride.sems, compiler_params=params,
    )(*args, *ride.ins)
    return outs[:n_out], outs[n_out:]


def ffn_fwd(h, gain, wgu, wd, ride=None):
    t = h.shape[0]

    def body(h_ref, g_ref, wgu_ref, wd_ref, hn_ref, z_ref, u_ref, acc):
        k = pl.program_id(1)

        @pl.when(k == 0)
        def _():
            u_ref[...] = _rms_fwd(h_ref[...], g_ref[...]).astype(BF16)
            acc[...] = jnp.zeros_like(acc)

        u = u_ref[...]
        g = _dot_nt(u, wgu_ref[0, 0])
        up = _dot_nt(u, wgu_ref[1, 0])
        z_ref[0, 0] = g.astype(BF16)
        z_ref[1, 0] = up.astype(BF16)
        a = g * jax.nn.sigmoid(g) * up
        acc[...] += _dot(a.astype(BF16), wd_ref[0])

        @pl.when(k == 3)
        def _():
            hn_ref[...] = h_ref[...] + acc[...]

    return _call(
        body, name="ffn_fwd", grid=(t // TM, 4), ride=ride,
        in_specs=[pl.BlockSpec((TM, D), lambda i, k: (i, 0)),
                  pl.BlockSpec((1, D), lambda i, k: (0, 0)),
                  pl.BlockSpec((2, 1, FC, D), lambda i, k: (0, k, 0, 0)),
                  pl.BlockSpec((1, FC, D), lambda i, k: (k, 0, 0))],
        out_specs=[pl.BlockSpec((TM, D), lambda i, k: (i, 0)),
                   pl.BlockSpec((2, 1, TM, FC), lambda i, k: (0, k, i, 0)),
                   pl.BlockSpec((TM, D), lambda i, k: (i, 0))],
        out_shape=[jax.ShapeDtypeStruct((t, D), F32),
                   jax.ShapeDtypeStruct((2, 4, t, FC), BF16),
                   jax.ShapeDtypeStruct((t, D), BF16)],
        scratch_shapes=[pltpu.VMEM((TM, D), F32)],
        args=(h, gain, wgu, wd))


def ffn_bwd_x(dh, h, gain, z, wgu, wd, ride=None):
    t = h.shape[0]

    def body(dh_ref, h_ref, g_ref, z_ref, wgu_ref, wd_ref, dhp_ref, dz_ref, a_ref, dgain_ref, dhb, du):
        i, k = pl.program_id(0), pl.program_id(1)

        @pl.when(k == 0)
        def _():
            dhb[...] = dh_ref[...].astype(BF16)
            du[...] = jnp.zeros_like(du)

        @pl.when((k == 0) & (i == 0))
        def _():
            dgain_ref[...] = jnp.zeros_like(dgain_ref)

        da = _dot_nt(dhb[...], wd_ref[0])
        g = z_ref[0, 0].astype(F32)
        up = z_ref[1, 0].astype(F32)
        sg = jax.nn.sigmoid(g)
        silu = g * sg
        a_ref[0] = (silu * up).astype(BF16)
        dg = (da * up * (sg * (1.0 + g * (1.0 - sg)))).astype(BF16)
        dup = (da * silu).astype(BF16)
        dz_ref[0, 0] = dg
        dz_ref[1, 0] = dup
        for n in range(2):
            cols = slice(n * (D // 2), (n + 1) * (D // 2))
            du[:, cols] += _dot(dg, wgu_ref[0, 0, :, cols]) + _dot(dup, wgu_ref[1, 0, :, cols])

        @pl.when(k == 3)
        def _():
            dx, dgain = _rms_bwd(h_ref[...], g_ref[...], du[...])
            dhp_ref[...] = dh_ref[...] + dx
            dgain_ref[...] += dgain

    return _call(
        body, name="ffn_bwd_x", grid=(t // TM, 4), ride=ride,
        in_specs=[pl.BlockSpec((TM, D), lambda i, k: (i, 0)),
                  pl.BlockSpec((TM, D), lambda i, k: (i, 0)),
                  pl.BlockSpec((1, D), lambda i, k: (0, 0)),
                  pl.BlockSpec((2, 1, TM, FC), lambda i, k: (0, k, i, 0)),
                  pl.BlockSpec((2, 1, FC, D), lambda i, k: (0, k, 0, 0)),
                  pl.BlockSpec((1, FC, D), lambda i, k: (k, 0, 0))],
        out_specs=[pl.BlockSpec((TM, D), lambda i, k: (i, 0)),
                   pl.BlockSpec((2, 1, TM, FC), lambda i, k: (0, k, i, 0)),
                   pl.BlockSpec((1, TM, FC), lambda i, k: (k, i, 0)),
                   pl.BlockSpec((1, D), lambda i, k: (0, 0)),
                   pl.BlockSpec((TM, D), lambda i, k: (i, 0))],
        out_shape=[jax.ShapeDtypeStruct((t, D), F32),
                   jax.ShapeDtypeStruct((2, 4, t, FC), BF16),
                   jax.ShapeDtypeStruct((4, t, FC), BF16),
                   jax.ShapeDtypeStruct((1, D), F32),
                   jax.ShapeDtypeStruct((t, D), BF16)],
        scratch_shapes=[pltpu.VMEM((TM, D), F32)],
        args=(dh, h, gain, z, wgu, wd))


TW = 2048


def ffn_bwd_w(u, dz, a, dhb, ride=None):
    t = u.shape[0]
    tw = min(TW, t)
    steps = t // tw

    def body(u_ref, dz_ref, a_ref, dh_ref, dwgu_ref, dwd_ref, acc_gu, acc_d):
        j = pl.program_id(1)

        @pl.when(j == 0)
        def _():
            acc_gu[...] = jnp.zeros_like(acc_gu)
            acc_d[...] = jnp.zeros_like(acc_d)

        ub = u_ref[...]
        acc_gu[0] += _dot_tn(dz_ref[0, 0], ub)
        acc_gu[1] += _dot_tn(dz_ref[1, 0], ub)
        acc_d[...] += _dot_tn(a_ref[0], dh_ref[...])

        @pl.when(j == steps - 1)
        def _():
            dwgu_ref[:, 0] = acc_gu[...].astype(BF16)
            dwd_ref[0] = acc_d[...].astype(BF16)

    return _call(
        body, name="ffn_bwd_w", grid=(4, steps), ride=ride,
        in_specs=[pl.BlockSpec((tw, D), lambda k, j: (j, 0)),
                  pl.BlockSpec((2, 1, tw, FC), lambda k, j: (0, k, j, 0)),
                  pl.BlockSpec((1, tw, FC), lambda k, j: (k, j, 0)),
                  pl.BlockSpec((tw, D), lambda k, j: (j, 0))],
        out_specs=[pl.BlockSpec((2, 1, FC, D), lambda k, j: (0, k, 0, 0)),
                   pl.BlockSpec((1, FC, D), lambda k, j: (k, 0, 0))],
        out_shape=[jax.ShapeDtypeStruct((2, 4, FC, D), BF16),
                   jax.ShapeDtypeStruct((4, FC, D), BF16)],
        scratch_shapes=[pltpu.VMEM((2, FC, D), F32), pltpu.VMEM((FC, D), F32)],
        args=(u, dz, a, dhb))


def _prev_halo(i):
    return jnp.maximum(i * (TM // HALO) - 1, 0)


def _next_halo(i, t):
    return jnp.minimum((i + 1) * (TM // HALO), t // HALO - 1)


def rms_matmul(h, gain, w, bias, ride=None):
    t = h.shape[0]
    n = w.shape[1]

    def body(h_ref, g_ref, w_ref, b_ref, z_ref, u_ref):
        u = _rms_fwd(h_ref[...], g_ref[...]).astype(BF16)
        u_ref[...] = u
        z_ref[...] = (_dot(u, w_ref[...]) + b_ref[...]).astype(BF16)

    return _call(
        body, name=f"rms_matmul_{n}", grid=(t // TM,), ride=ride,
        in_specs=[pl.BlockSpec((TM, D), lambda i: (i, 0)),
                  pl.BlockSpec((1, D), lambda i: (0, 0)),
                  pl.BlockSpec((D, n), lambda i: (0, 0)),
                  pl.BlockSpec((1, n), lambda i: (0, 0))],
        out_specs=[pl.BlockSpec((TM, n), lambda i: (i, 0)),
                   pl.BlockSpec((TM, D), lambda i: (i, 0))],
        out_shape=[jax.ShapeDtypeStruct((t, n), BF16),
                   jax.ShapeDtypeStruct((t, D), BF16)],
        args=(h, gain, w, bias))


def in_proj_bwd_x(dz, w, h, gain, dh, ride=None):
    t = h.shape[0]
    n = w.shape[1]

    def body(dz_ref, w_ref, h_ref, g_ref, dh_ref, dhp_ref, dgain_ref, dbias_ref):
        @pl.when(pl.program_id(0) == 0)
        def _():
            dgain_ref[...] = jnp.zeros_like(dgain_ref)
            dbias_ref[...] = jnp.zeros_like(dbias_ref)

        du = _dot_nt(dz_ref[...], w_ref[...])
        dx, dgain = _rms_bwd(h_ref[...], g_ref[...], du)
        dhp_ref[...] = dh_ref[...] + dx
        dgain_ref[...] += dgain
        dbias_ref[...] += jnp.sum(dz_ref[...].astype(F32), axis=0, keepdims=True)

    return _call(
        body, name=f"in_proj_bwd_x_{n}", grid=(t // TM,), ride=ride,
        in_specs=[pl.BlockSpec((TM, n), lambda i: (i, 0)),
                  pl.BlockSpec((D, n), lambda i: (0, 0)),
                  pl.BlockSpec((TM, D), lambda i: (i, 0)),
                  pl.BlockSpec((1, D), lambda i: (0, 0)),
                  pl.BlockSpec((TM, D), lambda i: (i, 0))],
        out_specs=[pl.BlockSpec((TM, D), lambda i: (i, 0)),
                   pl.BlockSpec((1, D), lambda i: (0, 0)),
                   pl.BlockSpec((1, n), lambda i: (0, 0))],
        out_shape=[jax.ShapeDtypeStruct((t, D), F32),
                   jax.ShapeDtypeStruct((1, D), F32),
                   jax.ShapeDtypeStruct((1, n), F32)],
        args=(dz, w, h, gain, dh))


def in_proj_bwd_w(u, dz, ride=None):
    t = u.shape[0]
    n = dz.shape[1]
    steps = t // TM

    def body(u_ref, dz_ref, dw_ref, acc):
        s = pl.program_id(0)

        @pl.when(s == 0)
        def _():
            acc[...] = jnp.zeros_like(acc)

        acc[...] += _dot_tn(u_ref[...], dz_ref[...])

        @pl.when(s == steps - 1)
        def _():
            dw_ref[...] = acc[...].astype(BF16)

    return _call(
        body, name=f"in_proj_bwd_w_{n}", grid=(steps,), ride=ride,
        in_specs=[pl.BlockSpec((TM, D), lambda s: (s, 0)),
                  pl.BlockSpec((TM, n), lambda s: (s, 0))],
        out_specs=[pl.BlockSpec((D, n), lambda s: (0, 0))],
        out_shape=[jax.ShapeDtypeStruct((D, n), BF16)],
        scratch_shapes=[pltpu.VMEM((D, n), F32)],
        args=(u, dz))


A_TAPS = 3


def a_mix_fwd(z, h, conv, wout, ride=None):
    t = h.shape[0]

    def body(z_ref, zp_ref, h_ref, cw_ref, wo_ref, hn_ref, pad):
        i = pl.program_id(0)
        ph = zp_ref[:, D:2 * D].astype(F32) * zp_ref[:, 2 * D:].astype(F32)
        pad[0:HALO, :] = jnp.where(i == 0, 0.0, ph)
        pad[HALO:, :] = z_ref[:, D:2 * D].astype(F32) * z_ref[:, 2 * D:].astype(F32)
        q = jnp.zeros((TM, D), F32)
        for k in range(A_TAPS):
            off = HALO - (A_TAPS - 1) + k
            q += cw_ref[k:k + 1, :] * pad[off:off + TM, :]
        r = z_ref[:, 0:D].astype(F32) * q
        hn_ref[...] = h_ref[...] + _dot(r.astype(BF16), wo_ref[...])

    return _call(
        body, name="a_mix_fwd", grid=(t // TM,), ride=ride,
        in_specs=[pl.BlockSpec((TM, 3 * D), lambda i: (i, 0)),
                  pl.BlockSpec((HALO, 3 * D), lambda i: (_prev_halo(i), 0)),
                  pl.BlockSpec((TM, D), lambda i: (i, 0)),
                  pl.BlockSpec((A_TAPS, D), lambda i: (0, 0)),
                  pl.BlockSpec((D, D), lambda i: (0, 0))],
        out_specs=[pl.BlockSpec((TM, D), lambda i: (i, 0))],
        out_shape=[jax.ShapeDtypeStruct((t, D), F32)],
        scratch_shapes=[pltpu.VMEM((HALO + TM, D), F32)],
        args=(z, z, h, conv, wout))


def a_mix_bwd(dh, z, conv, wout, ride=None):
    t = dh.shape[0]
    steps = t // TM

    def body(dh_ref, dhn_ref, z_ref, zp_ref, zn_ref, cw_ref, wo_ref, dz_ref, dwo_ref, dcw_ref, pad, dqpad, dwo):
        i = pl.program_id(0)
        last = i == steps - 1

        @pl.when(i == 0)
        def _():
            dwo[...] = jnp.zeros_like(dwo)
            dcw_ref[...] = jnp.zeros_like(dcw_ref)

        ph = zp_ref[:, D:2 * D].astype(F32) * zp_ref[:, 2 * D:].astype(F32)
        pad[0:HALO, :] = jnp.where(i == 0, 0.0, ph)
        c = z_ref[:, D:2 * D].astype(F32)
        v = z_ref[:, 2 * D:].astype(F32)
        pad[HALO:, :] = c * v
        q = jnp.zeros((TM, D), F32)
        for k in range(A_TAPS):
            off = HALO - (A_TAPS - 1) + k
            q += cw_ref[k:k + 1, :] * pad[off:off + TM, :]
        b = z_ref[:, 0:D].astype(F32)
        dhb = dh_ref[...].astype(BF16)
        dwo[...] += _dot_tn((b * q).astype(BF16), dhb)
        dr = _dot_nt(dhb, wo_ref[...])
        dz_ref[:, 0:D] = (dr * q).astype(BF16)
        dq = dr * b
        drn = _dot_nt(dhn_ref[...].astype(BF16), wo_ref[...])
        dqpad[0:TM, :] = dq
        dqpad[TM:, :] = jnp.where(last, 0.0, drn * zn_ref[:, 0:D].astype(F32))
        dp = jnp.zeros((TM, D), F32)
        for k in range(A_TAPS):
            off = A_TAPS - 1 - k
            dp += cw_ref[k:k + 1, :] * dqpad[off:off + TM, :]
            poff = HALO - (A_TAPS - 1) + k
            dcw_ref[k:k + 1, :] += jnp.sum(dq * pad[poff:poff + TM, :], axis=0, keepdims=True)
        dz_ref[:, D:2 * D] = (dp * v).astype(BF16)
        dz_ref[:, 2 * D:] = (dp * c).astype(BF16)

        @pl.when(last)
        def _():
            dwo_ref[...] = dwo[...].astype(BF16)

    return _call(
        body, name="a_mix_bwd", grid=(steps,), ride=ride,
        in_specs=[pl.BlockSpec((TM, D), lambda i: (i, 0)),
                  pl.BlockSpec((HALO, D), lambda i: (_next_halo(i, t), 0)),
                  pl.BlockSpec((TM, 3 * D), lambda i: (i, 0)),
                  pl.BlockSpec((HALO, 3 * D), lambda i: (_prev_halo(i), 0)),
                  pl.BlockSpec((HALO, 3 * D), lambda i: (_next_halo(i, t), 0)),
                  pl.BlockSpec((A_TAPS, D), lambda i: (0, 0)),
                  pl.BlockSpec((D, D), lambda i: (0, 0))],
        out_specs=[pl.BlockSpec((TM, 3 * D), lambda i: (i, 0)),
                   pl.BlockSpec((D, D), lambda i: (0, 0)),
                   pl.BlockSpec((A_TAPS, D), lambda i: (0, 0))],
        out_shape=[jax.ShapeDtypeStruct((t, 3 * D), BF16),
                   jax.ShapeDtypeStruct((D, D), BF16),
                   jax.ShapeDtypeStruct((A_TAPS, D), F32)],
        scratch_shapes=[pltpu.VMEM((HALO + TM, D), F32), pltpu.VMEM((TM + HALO, D), F32), pltpu.VMEM((D, D), F32)],
        args=(dh, dh, z, z, z, conv, wout))


C_TAPS = 31


def _glu(zr):
    return zr[:, 0:D].astype(F32) * jax.nn.sigmoid(zr[:, D:].astype(F32))


def _ln_silu(h2, lg, lb):
    mu = jnp.mean(h2, axis=-1, keepdims=True)
    xc = h2 - mu
    rstd = lax.rsqrt(jnp.mean(xc * xc, axis=-1, keepdims=True) + LN_EPS)
    xn = xc * rstd
    h3 = xn * lg + lb
    s3 = jax.nn.sigmoid(h3)
    return xn, rstd, h3, s3


def _ln_silu_bwd(h2, lg, lb, dh4):
    xn, rstd, h3, s3 = _ln_silu(h2, lg, lb)
    dh3 = dh4 * (s3 * (1.0 + h3 * (1.0 - s3)))
    dxn = dh3 * lg
    dh2 = rstd * (dxn - jnp.mean(dxn, axis=-1, keepdims=True) - xn * jnp.mean(dxn * xn, axis=-1, keepdims=True))
    return dh2, dh3, xn, h3 * s3


def c_mix_fwd(z, h, dw, bdw, lg, lb, w2, b2, ride=None):
    t = h.shape[0]

    def body(z_ref, zp_ref, h_ref, dw_ref, bdw_ref, lg_ref, lb_ref, w2_ref, b2_ref, hn_ref, h2_ref, pad):
        i = pl.program_id(0)
        pad[0:HALO, :] = jnp.where(i == 0, 0.0, _glu(zp_ref))
        pad[HALO:, :] = _glu(z_ref)
        h2 = jnp.zeros((TM, D), F32) + bdw_ref[...]
        for k in range(C_TAPS):
            off = HALO - (C_TAPS - 1) + k
            h2 += dw_ref[k:k + 1, :] * pad[off:off + TM, :]
        h2_ref[...] = h2
        _, _, h3, s3 = _ln_silu(h2, lg_ref[...], lb_ref[...])
        hn_ref[...] = h_ref[...] + _dot((h3 * s3).astype(BF16), w2_ref[...]) + b2_ref[...]

    vec = pl.BlockSpec((1, D), lambda i: (0, 0))
    return _call(
        body, name="c_mix_fwd", grid=(t // TM,), ride=ride,
        in_specs=[pl.BlockSpec((TM, 2 * D), lambda i: (i, 0)),
                  pl.BlockSpec((HALO, 2 * D), lambda i: (_prev_halo(i), 0)),
                  pl.BlockSpec((TM, D), lambda i: (i, 0)),
                  pl.BlockSpec((C_TAPS, D), lambda i: (0, 0)),
                  vec, vec, vec,
                  pl.BlockSpec((D, D), lambda i: (0, 0)),
                  vec],
        out_specs=[pl.BlockSpec((TM, D), lambda i: (i, 0)),
                   pl.BlockSpec((TM, D), lambda i: (i, 0))],
        out_shape=[jax.ShapeDtypeStruct((t, D), F32),
                   jax.ShapeDtypeStruct((t, D), F32)],
        scratch_shapes=[pltpu.VMEM((HALO + TM, D), F32)],
        args=(z, z, h, dw, bdw, lg, lb, w2, b2))


def c_mix_bwd(dh, z, h2, dw, lg, lb, w2, ride=None):
    t = dh.shape[0]
    steps = t // TM

    def body(dh_ref, dhn_ref, z_ref, zp_ref, h2_ref, h2n_ref, dw_ref, lg_ref, lb_ref, w2_ref,
             dz_ref, dw2_ref, db2_ref, dlg_ref, dlb_ref, dbdw_ref, ddw_ref, pad, dpad, dw2):
        i = pl.program_id(0)
        last = i == steps - 1

        @pl.when(i == 0)
        def _():
            for r in (dw2, db2_ref, dlg_ref, dlb_ref, dbdw_ref, ddw_ref):
                r[...] = jnp.zeros_like(r)

        lg, lb = lg_ref[...], lb_ref[...]
        dh = dh_ref[...]
        dhb = dh.astype(BF16)
        dh2, dh3, xn, h4 = _ln_silu_bwd(h2_ref[...], lg, lb, _dot_nt(dhb, w2_ref[...]))
        dw2[...] += _dot_tn(h4.astype(BF16), dhb)
        db2_ref[...] += jnp.sum(dh, axis=0, keepdims=True)
        dlg_ref[...] += jnp.sum(dh3 * xn, axis=0, keepdims=True)
        dlb_ref[...] += jnp.sum(dh3, axis=0, keepdims=True)
        dbdw_ref[...] += jnp.sum(dh2, axis=0, keepdims=True)
        dh2n, _, _, _ = _ln_silu_bwd(h2n_ref[...], lg, lb, _dot_nt(dhn_ref[...].astype(BF16), w2_ref[...]))
        dpad[0:TM, :] = dh2
        dpad[TM:, :] = jnp.where(last, 0.0, dh2n)
        pad[0:HALO, :] = jnp.where(i == 0, 0.0, _glu(zp_ref))
        pad[HALO:, :] = _glu(z_ref)
        dh1 = jnp.zeros((TM, D), F32)
        for k in range(C_TAPS):
            off = C_TAPS - 1 - k
            dh1 += dw_ref[k:k + 1, :] * dpad[off:off + TM, :]
            poff = HALO - (C_TAPS - 1) + k
            ddw_ref[k:k + 1, :] += jnp.sum(dh2 * pad[poff:poff + TM, :], axis=0, keepdims=True)
        a = z_ref[:, 0:D].astype(F32)
        sg = jax.nn.sigmoid(z_ref[:, D:].astype(F32))
        dz_ref[:, 0:D] = (dh1 * sg).astype(BF16)
        dz_ref[:, D:] = (dh1 * a * sg * (1.0 - sg)).astype(BF16)

        @pl.when(last)
        def _():
            dw2_ref[...] = dw2[...].astype(BF16)

    vec = pl.BlockSpec((1, D), lambda i: (0, 0))
    return _call(
        body, name="c_mix_bwd", grid=(steps,), ride=ride,
        in_specs=[pl.BlockSpec((TM, D), lambda i: (i, 0)),
                  pl.BlockSpec((HALO, D), lambda i: (_next_halo(i, t), 0)),
                  pl.BlockSpec((TM, 2 * D), lambda i: (i, 0)),
                  pl.BlockSpec((HALO, 2 * D), lambda i: (_prev_halo(i), 0)),
                  pl.BlockSpec((TM, D), lambda i: (i, 0)),
                  pl.BlockSpec((HALO, D), lambda i: (_next_halo(i, t), 0)),
                  pl.BlockSpec((C_TAPS, D), lambda i: (0, 0)),
                  vec, vec,
                  pl.BlockSpec((D, D), lambda i: (0, 0), pipeline_mode=pl.Buffered(1))],
        out_specs=[pl.BlockSpec((TM, 2 * D), lambda i: (i, 0)),
                   pl.BlockSpec((D, D), lambda i: (0, 0)),
                   vec, vec, vec, vec,
                   pl.BlockSpec((C_TAPS, D), lambda i: (0, 0))],
        out_shape=[jax.ShapeDtypeStruct((t, 2 * D), BF16),
                   jax.ShapeDtypeStruct((D, D), BF16)]
                  + [jax.ShapeDtypeStruct((1, D), F32)] * 4
                  + [jax.ShapeDtypeStruct((C_TAPS, D), F32)],
        scratch_shapes=[pltpu.VMEM((HALO + TM, D), F32), pltpu.VMEM((TM + HALO, D), F32), pltpu.VMEM((D, D), F32)],
        args=(dh, dh, z, z, h2, h2, dw, lg, lb, w2))


POOL_WINDOWS = (2, 4, 8, 16)
GW = D // len(POOL_WINDOWS)


def _pool_mixed(pad, g, w, inv_cnt):
    cols = slice(g * GW, (g + 1) * GW)
    s = pad[HALO:HALO + TM, cols]
    u = s
    for j in range(1, w):
        s = s + pad[HALO - j:HALO - j + TM, cols]
    return s * inv_cnt - u


def _inv_cnt(i, w):
    row = i * TM + lax.broadcasted_iota(jnp.int32, (TM, 1), 0)
    return 1.0 / jnp.minimum(row + 1, w).astype(F32)


def b_mix_fwd(h, gain, wg, scale, ride=None):
    t = h.shape[0]

    def body(h_ref, hp_ref, g_ref, wg_ref, sc_ref, hn_ref, pad):
        i = pl.program_id(0)
        gain = g_ref[...]
        pad[0:HALO, :] = jnp.where(i == 0, 0.0, _rms_fwd(hp_ref[...], gain))
        pad[HALO:, :] = _rms_fwd(h_ref[...], gain)
        for g, w in enumerate(POOL_WINDOWS):
            cols = slice(g * GW, (g + 1) * GW)
            mixed = _pool_mixed(pad, g, w, _inv_cnt(i, w))
            y = _dot(mixed.astype(BF16), wg_ref[g])
            hn_ref[:, cols] = h_ref[:, cols] + y * sc_ref[:, cols]

    return _call(
        body, name="b_mix_fwd", grid=(t // TM,), ride=ride,
        in_specs=[pl.BlockSpec((TM, D), lambda i: (i, 0)),
                  pl.BlockSpec((HALO, D), lambda i: (_prev_halo(i), 0)),
                  pl.BlockSpec((1, D), lambda i: (0, 0)),
                  pl.BlockSpec((4, GW, GW), lambda i: (0, 0, 0)),
                  pl.BlockSpec((1, D), lambda i: (0, 0))],
        out_specs=[pl.BlockSpec((TM, D), lambda i: (i, 0))],
        out_shape=[jax.ShapeDtypeStruct((t, D), F32)],
        scratch_shapes=[pltpu.VMEM((HALO + TM, D), F32)],
        args=(h, h, gain, wg, scale))


def b_mix_bwd(dh, h, gain, wg, scale, ride=None):
    t = h.shape[0]
    steps = t // TM

    def body(dh_ref, dhn_ref, h_ref, hp_ref, g_ref, wg_ref, sc_ref, dhp_ref, dgain_ref, dwg_ref, dsc_ref, pad, dpad, du):
        i = pl.program_id(0)
        last = i == steps - 1

        @pl.when(i == 0)
        def _():
            for r in (dgain_ref, dwg_ref, dsc_ref):
                r[...] = jnp.zeros_like(r)

        gain = g_ref[...]
        pad[0:HALO, :] = jnp.where(i == 0, 0.0, _rms_fwd(hp_ref[...], gain))
        pad[HALO:, :] = _rms_fwd(h_ref[...], gain)
        for g, w in enumerate(POOL_WINDOWS):
            cols = slice(g * GW, (g + 1) * GW)
            inv_cnt = _inv_cnt(i, w)
            mixed = _pool_mixed(pad, g, w, inv_cnt).astype(BF16)
            dh = dh_ref[:, cols]
            dsc_ref[:, cols] += jnp.sum(dh * _dot(mixed, wg_ref[g]), axis=0, keepdims=True)
            dy = (dh * sc_ref[:, cols]).astype(BF16)
            dwg_ref[g] += _dot_tn(mixed, dy)
            dm = _dot_nt(dy, wg_ref[g])
            dmn = _dot_nt((dhn_ref[:, cols] * sc_ref[:, cols]).astype(BF16), wg_ref[g])
            dpad[0:TM, cols] = dm * inv_cnt
            dpad[TM:, cols] = jnp.where(last, 0.0, dmn * (1.0 / w))
            s = dpad[0:TM, cols]
            for j in range(1, w):
                s = s + dpad[j:j + TM, cols]
            du[:, cols] = s - dm
        dx, dgain = _rms_bwd(h_ref[...], gain, du[...])
        dhp_ref[...] = dh_ref[...] + dx
        dgain_ref[...] += dgain

    return _call(
        body, name="b_mix_bwd", grid=(steps,), ride=ride,
        in_specs=[pl.BlockSpec((TM, D), lambda i: (i, 0)),
                  pl.BlockSpec((HALO, D), lambda i: (_next_halo(i, t), 0)),
                  pl.BlockSpec((TM, D), lambda i: (i, 0)),
                  pl.BlockSpec((HALO, D), lambda i: (_prev_halo(i), 0)),
                  pl.BlockSpec((1, D), lambda i: (0, 0)),
                  pl.BlockSpec((4, GW, GW), lambda i: (0, 0, 0)),
                  pl.BlockSpec((1, D), lambda i: (0, 0))],
        out_specs=[pl.BlockSpec((TM, D), lambda i: (i, 0)),
                   pl.BlockSpec((1, D), lambda i: (0, 0)),
                   pl.BlockSpec((4, GW, GW), lambda i: (0, 0, 0)),
                   pl.BlockSpec((1, D), lambda i: (0, 0))],
        out_shape=[jax.ShapeDtypeStruct((t, D), F32),
                   jax.ShapeDtypeStruct((1, D), F32),
                   jax.ShapeDtypeStruct((4, GW, GW), F32),
                   jax.ShapeDtypeStruct((1, D), F32)],
        scratch_shapes=[pltpu.VMEM((HALO + TM, D), F32), pltpu.VMEM((TM + HALO, D), F32), pltpu.VMEM((TM, D), F32)],
        args=(dh, dh, h, h, gain, wg, scale))


LOSS_LANES = 128


def loss_head(h, gain, target):
    t = h.shape[0]

    def body(h_ref, g_ref, tg_ref, loss_ref, dh_ref, dgain_ref):
        @pl.when(pl.program_id(0) == 0)
        def _():
            loss_ref[...] = jnp.zeros_like(loss_ref)
            dgain_ref[...] = jnp.zeros_like(dgain_ref)

        x, gain = h_ref[...], g_ref[...]
        err = _rms_fwd(x, gain) - tg_ref[...]
        per_row = jnp.mean(err * err, axis=-1, keepdims=True)
        loss_ref[...] += jnp.broadcast_to(0.5 * jnp.sum(per_row, axis=0, keepdims=True), (1, LOSS_LANES))
        dx, dgain = _rms_bwd(x, gain, err * (1.0 / D))
        dh_ref[...] = dx
        dgain_ref[...] += dgain

    outs, _ = _call(
        body, name="loss_head", grid=(t // TM,),
        in_specs=[pl.BlockSpec((TM, D), lambda i: (i, 0)),
                  pl.BlockSpec((1, D), lambda i: (0, 0)),
                  pl.BlockSpec((TM, D), lambda i: (i, 0))],
        out_specs=[pl.BlockSpec((1, LOSS_LANES), lambda i: (0, 0)),
                   pl.BlockSpec((TM, D), lambda i: (i, 0)),
                   pl.BlockSpec((1, D), lambda i: (0, 0))],
        out_shape=[jax.ShapeDtypeStruct((1, LOSS_LANES), F32),
                   jax.ShapeDtypeStruct((t, D), F32),
                   jax.ShapeDtypeStruct((1, D), F32)],
        args=(h, gain, target))
    return outs


ADAM_LR = 0.001
ADAM_B1 = 0.9
ADAM_B2 = 0.999
ADAM_EPS = 1e-08
ADAM_WD = 0.01
ADAM_STEP = 10


def cast_all(arrays):
    def body(*refs):
        for src, dst in zip(refs[:len(arrays)], refs[len(arrays):]):
            dst[...] = src[...].astype(BF16)

    return pl.pallas_call(
        body, name="cast_all", out_shape=[jax.ShapeDtypeStruct(a.shape, BF16) for a in arrays],
        compiler_params=pltpu.CompilerParams(vmem_limit_bytes=VMEM_LIMIT),
    )(*arrays)


def _adam_math(w, m, v, g):
    m = ADAM_B1 * m + (1.0 - ADAM_B1) * g
    v = ADAM_B2 * v + (1.0 - ADAM_B2) * (g * g)
    m_hat = m / (1.0 - ADAM_B1 ** ADAM_STEP)
    v_hat = v / (1.0 - ADAM_B2 ** ADAM_STEP)
    return -ADAM_LR * (m_hat / (jnp.sqrt(v_hat) + ADAM_EPS) + ADAM_WD * w), m, v


def adamw(w, m, v, gparts, rb):
    r, c = w.shape

    def body(w_ref, m_ref, v_ref, gp_ref, g_ref, d_ref, nm_ref, nv_ref):
        g = gp_ref[0].astype(F32)
        for s in range(1, N_DEV):
            g = g + gp_ref[s].astype(F32)
        g_ref[...] = g
        d_ref[...], nm_ref[...], nv_ref[...] = _adam_math(w_ref[...], m_ref[...], v_ref[...], g)

    blk = pl.BlockSpec((rb, c), lambda i: (i, 0))
    outs, _ = _call(
        body, name=f"adamw_{r}x{c}", grid=(r // rb,),
        in_specs=[blk, blk, blk, pl.BlockSpec((N_DEV, rb, c), lambda i: (0, i, 0))],
        out_specs=[blk] * 4,
        out_shape=[jax.ShapeDtypeStruct((r, c), F32)] * 4,
        args=(w, m, v, gparts))
    return outs


def adamw_vectors(ws, ms, vs, gparts):
    nv = len(ws)

    def body(*refs):
        w_refs, m_refs, v_refs = refs[:nv], refs[nv:2 * nv], refs[2 * nv:3 * nv]
        gp_ref = refs[3 * nv]
        outs = refs[3 * nv + 1:]
        g_refs, d_refs, nm_refs, nv_refs = outs[:nv], outs[nv:2 * nv], outs[2 * nv:3 * nv], outs[3 * nv:]
        row = 0
        for i in range(nv):
            for part in range(w_refs[i].shape[1] // D):
                cols = slice(part * D, (part + 1) * D)
                g = gp_ref[0, row:row + 1, :]
                for s in range(1, N_DEV):
                    g = g + gp_ref[s, row:row + 1, :]
                g_refs[i][:, cols] = g
                d_refs[i][:, cols], nm_refs[i][:, cols], nv_refs[i][:, cols] = _adam_math(
                    w_refs[i][:, cols], m_refs[i][:, cols], v_refs[i][:, cols], g)
                row += 1

    shapes = [jax.ShapeDtypeStruct(w.shape, F32) for w in ws]
    outs = pl.pallas_call(body, name="adamw_vectors", out_shape=shapes * 4)(*ws, *ms, *vs, gparts)
    return outs[:nv], outs[nv:2 * nv], outs[2 * nv:3 * nv], outs[3 * nv:]


WEIGHTS = ["ln1_0", "a0_w_in", "a0_conv", "a0_w_out", "ln2_0", "ffn0_w_gu", "ffn0_w_down",
           "ln1_1", "b1_w_grp", "b1_scale", "ln2_1", "ffn1_w_gu", "ffn1_w_down",
           "ln1_2", "c2_w_pw1", "c2_b_pw1", "c2_dw", "c2_b_dw", "c2_ln_g", "c2_ln_b", "c2_w_pw2", "c2_b_pw2",
           "ln2_2", "ffn2_w_gu", "ffn2_w_down",
           "ln1_3", "a3_w_in", "a3_conv", "a3_w_out", "ln2_3", "ffn3_w_gu", "ffn3_w_down", "ln_f"]
SHARDED = {"a0_w_in": ("cols", 256), "a0_conv": ("cols", A_TAPS), "a0_w_out": ("rows", 128),
           "ffn0_w_gu": ("lead", 176), "ffn0_w_down": ("rows", 176),
           "b1_w_grp": ("mid", 128),
           "ffn1_w_gu": ("lead", 176), "ffn1_w_down": ("rows", 176),
           "c2_w_pw1": ("cols", 256), "c2_dw": ("cols", C_TAPS), "c2_w_pw2": ("rows", 128),
           "ffn2_w_gu": ("lead", 176), "ffn2_w_down": ("rows", 176),
           "a3_w_in": ("cols", 256), "a3_conv": ("cols", A_TAPS), "a3_w_out": ("rows", 128),
           "ffn3_w_gu": ("lead", 176), "ffn3_w_down": ("rows", 176)}
IN_PROJ = ("a0_w_in", "c2_w_pw1", "a3_w_in")
REPL = [n for n in WEIGHTS if n not in SHARDED]
REPL_ROWS = 16
GATHER_PLAN = {"first": ["a0_w_in", "a0_w_out", "a0_conv"],
               "in0": ["ffn0_w_gu"], "mix0": ["ffn0_w_down"],
               "ffn0": ["b1_w_grp", "ffn1_w_gu", "ffn1_w_down"],
               "mix1": ["c2_w_pw1"],
               "ffn1": ["c2_w_pw2", "c2_dw", "ffn2_w_gu", "ffn2_w_down"],
               "in2": ["a3_w_in", "a3_w_out", "a3_conv"],
               "mix2": ["ffn3_w_gu", "ffn3_w_down"]}
SCATTER_PLAN = {"mixb3": ["ffn3_w_down"], "inw3": ["a3_w_out", "a3_conv"],
                "ffnx2": ["ffn3_w_gu"], "ffnw2": ["a3_w_in"],
                "inw2": ["c2_w_pw2", "c2_dw"],
                "ffnx1": ["c2_w_pw1", "ffn2_w_gu"], "ffnw1": ["ffn2_w_down"],
                "ffnx0": ["ffn1_w_gu", "b1_w_grp"], "ffnw0": ["ffn1_w_down"],
                "mixb0": ["ffn0_w_down"], "inw0": ["ffn0_w_gu"], "inx0": ["a0_w_out", "a0_conv", "a0_w_in"],
                "last": ["repl"]}


def _step(p):
    vec = lambda n: p[n].reshape(1, -1)
    x, target = p["x"][0], p["loss_target"][0]

    names = list(SHARDED)
    stored = lambda n, a: a.T if n.endswith("w_gu") else a
    shard = dict(zip(names, cast_all([stored(n, p[n]) for n in names])))
    full = {}

    def gather(slot):
        names = GATHER_PLAN[slot]
        return gather_ride([shard[n] for n in names], ["cols" if n in IN_PROJ else "lead" for n in names])

    def landed(slot, outs):
        full.update(zip(GATHER_PLAN[slot], outs))

    def conv_full(n):
        k = full[n].shape[1]
        return full[n].transpose(1, 0, 2).reshape(k, D).astype(F32)

    def wgu(i):
        return full[f"ffn{i}_w_gu"].reshape(2, 4, FC, D)

    def wd(i):
        return full[f"ffn{i}_w_down"].reshape(4, FC, D)

    landed("first", run_ride(gather("first"), "gather_first"))
    no_bias = jnp.zeros((1, 3 * D), F32)
    h = [x]
    saved = {}
    conv, wout = {}, {}

    (z, u), got = rms_matmul(h[-1], vec("ln1_0"), full["a0_w_in"], no_bias, ride=gather("in0"))
    landed("in0", got)
    conv[0], wout[0] = conv_full("a0_conv"), full["a0_w_out"].reshape(D, D)
    (hm,), got = a_mix_fwd(z, h[-1], conv[0], wout[0], ride=gather("mix0"))
    landed("mix0", got)
    saved["mix0"] = (z, u)
    h.append(hm)
    (hn, zf, uf), got = ffn_fwd(hm, vec("ln2_0"), wgu(0), wd(0), ride=gather("ffn0"))
    landed("ffn0", got)
    saved["ffn0"] = (zf, uf)
    h.append(hn)

    wgrp = full["b1_w_grp"].transpose(1, 0, 2, 3).reshape(4, GW, GW)
    (hm,), got = b_mix_fwd(h[-1], vec("ln1_1"), wgrp, vec("b1_scale"), ride=gather("mix1"))
    landed("mix1", got)
    h.append(hm)
    (hn, zf, uf), got = ffn_fwd(hm, vec("ln2_1"), wgu(1), wd(1), ride=gather("ffn1"))
    landed("ffn1", got)
    saved["ffn1"] = (zf, uf)
    h.append(hn)

    (z, u), got = rms_matmul(h[-1], vec("ln1_2"), full["c2_w_pw1"], vec("c2_b_pw1"), ride=gather("in2"))
    landed("in2", got)
    cdw, wpw2 = conv_full("c2_dw"), full["c2_w_pw2"].reshape(D, D)
    (hm, h2), got = c_mix_fwd(z, h[-1], cdw, vec("c2_b_dw"), vec("c2_ln_g"), vec("c2_ln_b"), wpw2, vec("c2_b_pw2"),
                              ride=gather("mix2"))
    landed("mix2", got)
    saved["mix2"] = (z, u, h2)
    h.append(hm)
    (hn, zf, uf), _ = ffn_fwd(hm, vec("ln2_2"), wgu(2), wd(2))
    saved["ffn2"] = (zf, uf)
    h.append(hn)

    (z, u), _ = rms_matmul(h[-1], vec("ln1_3"), full["a3_w_in"], no_bias)
    conv[3], wout[3] = conv_full("a3_conv"), full["a3_w_out"].reshape(D, D)
    (hm,), _ = a_mix_fwd(z, h[-1], conv[3], wout[3])
    saved["mix3"] = (z, u)
    h.append(hm)
    (hn, zf, uf), _ = ffn_fwd(hm, vec("ln2_3"), wgu(3), wd(3))
    saved["ffn3"] = (zf, uf)
    h.append(hn)

    loss_lanes, dh, g_lnf = loss_head(h[-1], vec("ln_f"), target)

    g = {"ln_f": g_lnf}
    recv = {}

    def repl_rows():
        return jnp.concatenate([g[n].reshape(-1, D) for n in REPL], axis=0)

    def scatter(slot):
        parts = []
        for n in SCATTER_PLAN[slot]:
            parts.append((repl_rows(), "all") if n == "repl" else (g[n], SHARDED[n][0]))
        return scatter_ride(parts)

    def arrived(slot, outs):
        recv.update(zip(SCATTER_PLAN[slot], outs))

    for i in (3, 2, 1, 0):
        zf, uf = saved[f"ffn{i}"]
        dh_in = dh
        (dh, dzf, a, g[f"ln2_{i}"], dhb), got = ffn_bwd_x(dh_in, h[2 * i + 1], vec(f"ln2_{i}"), zf, wgu(i), wd(i),
                                                      ride=scatter(f"ffnx{i}") if f"ffnx{i}" in SCATTER_PLAN else None)
        if got:
            arrived(f"ffnx{i}", got)
        (dwgu, dwd), got = ffn_bwd_w(uf, dzf, a, dhb, ride=scatter(f"ffnw{i}") if f"ffnw{i}" in SCATTER_PLAN else None)
        if got:
            arrived(f"ffnw{i}", got)
        g[f"ffn{i}_w_gu"], g[f"ffn{i}_w_down"] = dwgu.reshape(N_DEV, FC, D), dwd.reshape(FF, D)
        hin = h[2 * i]
        if i in (0, 3):
            z, u = saved[f"mix{i}"]
            (dz, g[f"a{i}_w_out"], g[f"a{i}_conv"]), got = a_mix_bwd(dh, z, conv[i], wout[i], ride=scatter(f"mixb{i}"))
            arrived(f"mixb{i}", got)
            (g[f"a{i}_w_in"],), got = in_proj_bwd_w(u, dz, ride=scatter(f"inw{i}"))
            arrived(f"inw{i}", got)
            (dh, g[f"ln1_{i}"], _), got = in_proj_bwd_x(dz, full[f"a{i}_w_in"], hin, vec(f"ln1_{i}"), dh,
                                                       ride=scatter(f"inx{i}") if f"inx{i}" in SCATTER_PLAN else None)
            if got:
                arrived(f"inx{i}", got)
        elif i == 1:
            (dh, g["ln1_1"], g["b1_w_grp"], g["b1_scale"]), _ = b_mix_bwd(dh, hin, vec("ln1_1"), wgrp, vec("b1_scale"))
        else:
            z, u, h2 = saved["mix2"]
            (dz, g["c2_w_pw2"], g["c2_b_pw2"], g["c2_ln_g"], g["c2_ln_b"], g["c2_b_dw"], g["c2_dw"]), _ = c_mix_bwd(
                dh, z, h2, cdw, vec("c2_ln_g"), vec("c2_ln_b"), wpw2)
            (g["c2_w_pw1"],), got = in_proj_bwd_w(u, dz, ride=scatter("inw2"))
            arrived("inw2", got)
            (dh, g["ln1_2"], g["c2_b_pw1"]), _ = in_proj_bwd_x(dz, full["c2_w_pw1"], hin, vec("ln1_2"), dh)
    grad_x = dh[None]
    arrived("last", run_ride(scatter("last"), "scatter_last"))

    grad, delta, new_m, new_v = {}, {}, {}, {}
    for n, (_, rb) in SHARDED.items():
        shape = p[n].shape
        two_d = lambda a: stored(n, a.reshape(-1, shape[-1]))
        w2 = two_d(p[n])
        outs = adamw(w2, two_d(p["m_" + n]), two_d(p["v_" + n]), recv[n].reshape(N_DEV, *w2.shape), rb)
        grad[n], delta[n], new_m[n], new_v[n] = [stored(n, o).reshape(shape) for o in outs]
    outs = adamw_vectors([vec(n) for n in REPL], [vec("m_" + n) for n in REPL], [vec("v_" + n) for n in REPL], recv["repl"])
    for res, o in zip((grad, delta, new_m, new_v), outs):
        res.update({n: a.reshape(p[n].shape) for n, a in zip(REPL, o)})

    loss = lax.psum(loss_lanes[0, 0], ("x", "y", "c"))
    return (loss, grad_x, *[grad[n] for n in WEIGHTS], *[delta[n] for n in WEIGHTS],
            *[new_m[n] for n in WEIGHTS], *[new_v[n] for n in WEIGHTS])


def kernel(x, ln1_0, a0_w_in, a0_conv, a0_w_out, ln2_0, ffn0_w_gu, ffn0_w_down, ln1_1, b1_w_grp, b1_scale, ln2_1, ffn1_w_gu, ffn1_w_down, ln1_2, c2_w_pw1, c2_b_pw1, c2_dw, c2_b_dw, c2_ln_g, c2_ln_b, c2_w_pw2, c2_b_pw2, ln2_2, ffn2_w_gu, ffn2_w_down, ln1_3, a3_w_in, a3_conv, a3_w_out, ln2_3, ffn3_w_gu, ffn3_w_down, ln_f, loss_target, m_ln1_0, m_a0_w_in, m_a0_conv, m_a0_w_out, m_ln2_0, m_ffn0_w_gu, m_ffn0_w_down, m_ln1_1, m_b1_w_grp, m_b1_scale, m_ln2_1, m_ffn1_w_gu, m_ffn1_w_down, m_ln1_2, m_c2_w_pw1, m_c2_b_pw1, m_c2_dw, m_c2_b_dw, m_c2_ln_g, m_c2_ln_b, m_c2_w_pw2, m_c2_b_pw2, m_ln2_2, m_ffn2_w_gu, m_ffn2_w_down, m_ln1_3, m_a3_w_in, m_a3_conv, m_a3_w_out, m_ln2_3, m_ffn3_w_gu, m_ffn3_w_down, m_ln_f, v_ln1_0, v_a0_w_in, v_a0_conv, v_a0_w_out, v_ln2_0, v_ffn0_w_gu, v_ffn0_w_down, v_ln1_1, v_b1_w_grp, v_b1_scale, v_ln2_1, v_ffn1_w_gu, v_ffn1_w_down, v_ln1_2, v_c2_w_pw1, v_c2_b_pw1, v_c2_dw, v_c2_b_dw, v_c2_ln_g, v_c2_ln_b, v_c2_w_pw2, v_c2_b_pw2, v_ln2_2, v_ffn2_w_gu, v_ffn2_w_down, v_ln1_3, v_a3_w_in, v_a3_conv, v_a3_w_out, v_ln2_3, v_ffn3_w_gu, v_ffn3_w_down, v_ln_f):
    return _step(dict(locals()))
```

```python
import jax
import jax.numpy as jnp
from jax import lax
from jax.experimental import pallas as pl
from jax.experimental.pallas import tpu as pltpu

F32 = jnp.float32
BF16 = jnp.bfloat16

N_DEV = 8
D = 1024
FF = 2816
FC = FF // 4
RMS_EPS = 1e-6
LN_EPS = 1e-5
TM = 512
HALO = 32
VMEM_LIMIT = 60 * 1024 * 1024

NT = (((1,), (1,)), ((), ()))
TN = (((0,), (0,)), ((), ()))
MESH = pl.DeviceIdType.MESH
ANY = pl.BlockSpec(memory_space=pl.ANY)
N_PEERS = N_DEV - 1


def _dot(a, b):
    return jnp.dot(a, b, preferred_element_type=F32)


def _dot_nt(a, b):
    return lax.dot_general(a, b, NT, preferred_element_type=F32)


def _dot_tn(a, b):
    return lax.dot_general(a, b, TN, preferred_element_type=F32)


def _rms_fwd(x, gain):
    r = lax.rsqrt(jnp.mean(x * x, axis=-1, keepdims=True) + RMS_EPS)
    return x * r * gain


def _rms_bwd(x, gain, du):
    r = lax.rsqrt(jnp.mean(x * x, axis=-1, keepdims=True) + RMS_EPS)
    xhat = x * r
    dgain = jnp.sum(du * xhat, axis=0, keepdims=True)
    dxhat = du * gain
    dx = r * (dxhat - xhat * jnp.mean(dxhat * xhat, axis=-1, keepdims=True))
    return dx, dgain


def _dev_index(p):
    return 4 * p[0] + 2 * p[1] + p[2]


def _place():
    return lax.axis_index("x"), lax.axis_index("y"), lax.axis_index("c")


class Ride:
    def __init__(self, ins, out_shapes, start, finish):
        self.ins, self.out_shapes, self.start, self.finish = list(ins), list(out_shapes), start, finish
        n = len(self.ins)
        self.sems = [pltpu.SemaphoreType.DMA((n * N_PEERS,)), pltpu.SemaphoreType.DMA((n * N_PEERS,)),
                     pltpu.SemaphoreType.DMA((n,))]


def gather_ride(shards, kinds):
    n = len(shards)

    def setup(ins, outs, sems):
        send_sems, recv_sems, local_sems = sems
        x, y, c = _place()
        chips = [(1 - x, y), (x, 1 - y), (1 - x, 1 - y)]

        def copy(a, k, block, to, src=None):
            slot = _chunk(outs[a], kinds[a], _dev_index(block))
            return pltpu.make_async_remote_copy(
                src_ref=slot if src is None else src, dst_ref=slot,
                send_sem=send_sems.at[a * N_PEERS + k], recv_sem=recv_sems.at[a * N_PEERS + k],
                device_id=to, device_id_type=MESH)

        def mine(a):
            return pltpu.make_async_copy(ins[a], _chunk(outs[a], kinds[a], _dev_index((x, y, c))), local_sems.at[a])

        def first(a):
            return [copy(a, 0, (x, y, c), (x, y, 1 - c), src=ins[a])] + [
                copy(a, 1 + j, (x, y, c), (*chip, c), src=ins[a]) for j, chip in enumerate(chips)]

        return (x, y, c), chips, copy, mine, first

    def start(ins, outs, sems):
        _, _, _, mine, first = setup(ins, outs, sems)
        for a in range(n):
            mine(a).start()
            for cp in first(a):
                cp.start()

    def finish(ins, outs, sems):
        (x, y, c), chips, copy, mine, first = setup(ins, outs, sems)
        me, sibling = (x, y, c), (x, y, 1 - c)
        passed = []
        for a in range(n):
            for j, chip in enumerate(chips):
                copy(a, 1 + j, (*chip, c), me).wait_recv()
                passed.append(copy(a, 4 + j, (*chip, c), sibling))
                passed[-1].start()
        for a in range(n):
            copy(a, 0, sibling, me).wait_recv()
            for j, chip in enumerate(chips):
                copy(a, 4 + j, (*chip, 1 - c), me).wait_recv()
        for a in range(n):
            for cp in first(a):
                cp.wait_send()
        for cp in passed:
            cp.wait_send()
        for a in range(n):
            mine(a).wait()

    shapes = [(N_DEV, *s.shape) if kind == "lead" else (s.shape[0], N_DEV * s.shape[1]) for s, kind in zip(shards, kinds)]
    return Ride(shards, [jax.ShapeDtypeStruct(shape, s.dtype) for shape, s in zip(shapes, shards)], start, finish)


def _chunk(ref, kind, j):
    if kind == "lead":
        return ref.at[j]
    if kind == "rows":
        r = ref.shape[0] // N_DEV
        return ref.at[pl.ds(j * r, r)]
    if kind == "mid":
        r = ref.shape[1] // N_DEV
        return ref.at[:, pl.ds(j * r, r), :]
    if kind == "cols":
        c = ref.shape[1] // N_DEV
        return ref.at[:, pl.ds(j * c, c)]
    return ref


def _chunk_shape(shape, kind):
    if kind == "lead":
        return tuple(shape[1:])
    if kind == "rows":
        return (shape[0] // N_DEV, *shape[1:])
    if kind == "mid":
        return (shape[0], shape[1] // N_DEV, shape[2])
    if kind == "cols":
        return (shape[0], shape[1] // N_DEV)
    return tuple(shape)


def scatter_ride(parts):
    n = len(parts)
    kinds = [k for _, k in parts]

    def setup(ins, outs, sems):
        send_sems, recv_sems, local_sems = sems
        x, y, c = _place()
        me = _dev_index((x, y, c))
        peers = []
        for k in range(1, N_DEV):
            kx, ky, kc = (k >> 2) & 1, (k >> 1) & 1, k & 1
            peers.append((1 - x if kx else x, 1 - y if ky else y, 1 - c if kc else c))

        def copy(a, k, peer):
            return pltpu.make_async_remote_copy(
                src_ref=_chunk(ins[a], kinds[a], _dev_index(peer)), dst_ref=outs[a].at[me],
                send_sem=send_sems.at[a * N_PEERS + k], recv_sem=recv_sems.at[a * N_PEERS + k],
                device_id=peer, device_id_type=MESH)

        def arrival(a, k, peer):
            slot = outs[a].at[_dev_index(peer)]
            return pltpu.make_async_remote_copy(
                src_ref=slot, dst_ref=slot,
                send_sem=send_sems.at[a * N_PEERS + k], recv_sem=recv_sems.at[a * N_PEERS + k],
                device_id=peer, device_id_type=MESH)

        def mine(a):
            return pltpu.make_async_copy(_chunk(ins[a], kinds[a], me), outs[a].at[me], local_sems.at[a])

        return peers, copy, arrival, mine

    def start(ins, outs, sems):
        peers, copy, _, mine = setup(ins, outs, sems)
        for a in range(n):
            mine(a).start()
            for k, peer in enumerate(peers):
                copy(a, k, peer).start()

    def finish(ins, outs, sems):
        peers, copy, arrival, mine = setup(ins, outs, sems)
        for a in range(n):
            for k, peer in enumerate(peers):
                arrival(a, k, peer).wait_recv()
        for a in range(n):
            for k, peer in enumerate(peers):
                copy(a, k, peer).wait_send()
            mine(a).wait()

    shapes = [jax.ShapeDtypeStruct((N_DEV, *_chunk_shape(arr.shape, kind)), arr.dtype) for arr, kind in parts]
    return Ride([arr for arr, _ in parts], shapes, start, finish)


def run_ride(ride, name):
    n_in, n_out = len(ride.ins), len(ride.out_shapes)

    def body(*refs):
        ins, outs, sems = refs[:n_in], refs[n_in:n_in + n_out], refs[n_in + n_out:]
        ride.start(ins, outs, sems)
        ride.finish(ins, outs, sems)

    return pl.pallas_call(
        body, name=name, in_specs=[ANY] * n_in, out_specs=[ANY] * n_out, out_shape=ride.out_shapes,
        scratch_shapes=ride.sems,
    )(*ride.ins)


def _call(body, *, name, grid, in_specs, out_specs, out_shape, args, scratch_shapes=(), ride=None):
    params = pltpu.CompilerParams(dimension_semantics=("arbitrary",) * len(grid), vmem_limit_bytes=VMEM_LIMIT)
    if ride is None:
        outs = pl.pallas_call(body, name=name, grid=grid, in_specs=in_specs, out_specs=out_specs, out_shape=out_shape,
                              scratch_shapes=list(scratch_shapes), compiler_params=params)(*args)
        return outs, []
    n_in, n_out, n_scr = len(in_specs), len(out_specs), len(scratch_shapes)
    r_in, r_out = len(ride.ins), len(ride.out_shapes)

    def hosted(*refs):
        ins, refs = refs[:n_in], refs[n_in:]
        rins, refs = refs[:r_in], refs[r_in:]
        outs, refs = refs[:n_out], refs[n_out:]
        routs, refs = refs[:r_out], refs[r_out:]
        scratch, sems = refs[:n_scr], refs[n_scr:]
        first = pl.program_id(0) == 0
        last = pl.program_id(0) == grid[0] - 1
        for d in range(1, len(grid)):
            first &= pl.program_id(d) == 0
            last &= pl.program_id(d) == grid[d] - 1

        @pl.when(first)
        def _():
            ride.start(rins, routs, sems)

        body(*ins, *outs, *scratch)

        @pl.when(last)
        def _():
            ride.finish(rins, routs, sems)

    outs = pl.pallas_call(
        hosted, name=name + "_ride", grid=grid,
        in_specs=list(in_specs) + [ANY] * r_in, out_specs=list(out_specs) + [ANY] * r_out,
        out_shape=list(out_shape) + ride.out_shapes,
        scratch_shapes=list(scratch_shapes) + ride.sems, compiler_params=params,
    )(*args, *ride.ins)
    return outs[:n_out], outs[n_out:]


def ffn_fwd(h, gain, wgu, wd, ride=None):
    t = h.shape[0]

    def body(h_ref, g_ref, wgu_ref, wd_ref, hn_ref, z_ref, u_ref, acc):
        k = pl.program_id(1)

        @pl.when(k == 0)
        def _():
            u_ref[...] = _rms_fwd(h_ref[...], g_ref[...]).astype(BF16)
            acc[...] = jnp.zeros_like(acc)

        u = u_ref[...]
        g = _dot_nt(u, wgu_ref[0, 0])
        up = _dot_nt(u, wgu_ref[1, 0])
        z_ref[0, 0] = g.astype(BF16)
        z_ref[1, 0] = up.astype(BF16)
        a = g * jax.nn.sigmoid(g) * up
        acc[...] += _dot(a.astype(BF16), wd_ref[0])

        @pl.when(k == 3)
        def _():
            hn_ref[...] = h_ref[...] + acc[...]

    return _call(
        body, name="ffn_fwd", grid=(t // TM, 4), ride=ride,
        in_specs=[pl.BlockSpec((TM, D), lambda i, k: (i, 0)),
                  pl.BlockSpec((1, D), lambda i, k: (0, 0)),
                  pl.BlockSpec((2, 1, FC, D), lambda i, k: (0, k, 0, 0)),
                  pl.BlockSpec((1, FC, D), lambda i, k: (k, 0, 0))],
        out_specs=[pl.BlockSpec((TM, D), lambda i, k: (i, 0)),
                   pl.BlockSpec((2, 1, TM, FC), lambda i, k: (0, k, i, 0)),
                   pl.BlockSpec((TM, D), lambda i, k: (i, 0))],
        out_shape=[jax.ShapeDtypeStruct((t, D), F32),
                   jax.ShapeDtypeStruct((2, 4, t, FC), BF16),
                   jax.ShapeDtypeStruct((t, D), BF16)],
        scratch_shapes=[pltpu.VMEM((TM, D), F32)],
        args=(h, gain, wgu, wd))


def ffn_bwd_x(dh, h, gain, z, wgu, wd, ride=None):
    t = h.shape[0]

    def body(dh_ref, h_ref, g_ref, z_ref, wgu_ref, wd_ref, dhp_ref, dz_ref, a_ref, dgain_ref, dhb, du):
        i, k = pl.program_id(0), pl.program_id(1)

        @pl.when(k == 0)
        def _():
            dhb[...] = dh_ref[...].astype(BF16)
            du[...] = jnp.zeros_like(du)

        @pl.when((k == 0) & (i == 0))
        def _():
            dgain_ref[...] = jnp.zeros_like(dgain_ref)

        da = _dot_nt(dhb[...], wd_ref[0])
        g = z_ref[0, 0].astype(F32)
        up = z_ref[1, 0].astype(F32)
        sg = jax.nn.sigmoid(g)
        silu = g * sg
        a_ref[0] = (silu * up).astype(BF16)
        dg = (da * up * (sg * (1.0 + g * (1.0 - sg)))).astype(BF16)
        dup = (da * silu).astype(BF16)
        dz_ref[0, 0] = dg
        dz_ref[1, 0] = dup
        for n in range(2):
            cols = slice(n * (D // 2), (n + 1) * (D // 2))
            du[:, cols] += _dot(dg, wgu_ref[0, 0, :, cols]) + _dot(dup, wgu_ref[1, 0, :, cols])

        @pl.when(k == 3)
        def _():
            dx, dgain = _rms_bwd(h_ref[...], g_ref[...], du[...])
            dhp_ref[...] = dh_ref[...] + dx
            dgain_ref[...] += dgain

    return _call(
        body, name="ffn_bwd_x", grid=(t // TM, 4), ride=ride,
        in_specs=[pl.BlockSpec((TM, D), lambda i, k: (i, 0)),
                  pl.BlockSpec((TM, D), lambda i, k: (i, 0)),
                  pl.BlockSpec((1, D), lambda i, k: (0, 0)),
                  pl.BlockSpec((2, 1, TM, FC), lambda i, k: (0, k, i, 0)),
                  pl.BlockSpec((2, 1, FC, D), lambda i, k: (0, k, 0, 0)),
                  pl.BlockSpec((1, FC, D), lambda i, k: (k, 0, 0))],
        out_specs=[pl.BlockSpec((TM, D), lambda i, k: (i, 0)),
                   pl.BlockSpec((2, 1, TM, FC), lambda i, k: (0, k, i, 0)),
                   pl.BlockSpec((1, TM, FC), lambda i, k: (k, i, 0)),
                   pl.BlockSpec((1, D), lambda i, k: (0, 0)),
                   pl.BlockSpec((TM, D), lambda i, k: (i, 0))],
        out_shape=[jax.ShapeDtypeStruct((t, D), F32),
                   jax.ShapeDtypeStruct((2, 4, t, FC), BF16),
                   jax.ShapeDtypeStruct((4, t, FC), BF16),
                   jax.ShapeDtypeStruct((1, D), F32),
                   jax.ShapeDtypeStruct((t, D), BF16)],
        scratch_shapes=[pltpu.VMEM((TM, D), F32)],
        args=(dh, h, gain, z, wgu, wd))


TW = 2048


def ffn_bwd_w(u, dz, a, dhb, ride=None):
    t = u.shape[0]
    tw = min(TW, t)
    steps = t // tw

    def body(u_ref, dz_ref, a_ref, dh_ref, dwgu_ref, dwd_ref, acc_gu, acc_d):
        j = pl.program_id(1)

        @pl.when(j == 0)
        def _():
            acc_gu[...] = jnp.zeros_like(acc_gu)
            acc_d[...] = jnp.zeros_like(acc_d)

        ub = u_ref[...]
        acc_gu[0] += _dot_tn(dz_ref[0, 0], ub)
        acc_gu[1] += _dot_tn(dz_ref[1, 0], ub)
        acc_d[...] += _dot_tn(a_ref[0], dh_ref[...])

        @pl.when(j == steps - 1)
        def _():
            dwgu_ref[:, 0] = acc_gu[...].astype(BF16)
            dwd_ref[0] = acc_d[...].astype(BF16)

    return _call(
        body, name="ffn_bwd_w", grid=(4, steps), ride=ride,
        in_specs=[pl.BlockSpec((tw, D), lambda k, j: (j, 0)),
                  pl.BlockSpec((2, 1, tw, FC), lambda k, j: (0, k, j, 0)),
                  pl.BlockSpec((1, tw, FC), lambda k, j: (k, j, 0)),
                  pl.BlockSpec((tw, D), lambda k, j: (j, 0))],
        out_specs=[pl.BlockSpec((2, 1, FC, D), lambda k, j: (0, k, 0, 0)),
                   pl.BlockSpec((1, FC, D), lambda k, j: (k, 0, 0))],
        out_shape=[jax.ShapeDtypeStruct((2, 4, FC, D), BF16),
                   jax.ShapeDtypeStruct((4, FC, D), BF16)],
        scratch_shapes=[pltpu.VMEM((2, FC, D), F32), pltpu.VMEM((FC, D), F32)],
        args=(u, dz, a, dhb))


def _prev_halo(i, tile=TM):
    return jnp.maximum(i * (tile // HALO) - 1, 0)


def _next_halo(i, t, tile=TM):
    return jnp.minimum((i + 1) * (tile // HALO), t // HALO - 1)


def rms_matmul(h, gain, w, bias, ride=None):
    t = h.shape[0]
    n = w.shape[1]

    def body(h_ref, g_ref, w_ref, b_ref, z_ref, u_ref):
        u = _rms_fwd(h_ref[...], g_ref[...]).astype(BF16)
        u_ref[...] = u
        z_ref[...] = (_dot(u, w_ref[...]) + b_ref[...]).astype(BF16)

    return _call(
        body, name=f"rms_matmul_{n}", grid=(t // TM,), ride=ride,
        in_specs=[pl.BlockSpec((TM, D), lambda i: (i, 0)),
                  pl.BlockSpec((1, D), lambda i: (0, 0)),
                  pl.BlockSpec((D, n), lambda i: (0, 0)),
                  pl.BlockSpec((1, n), lambda i: (0, 0))],
        out_specs=[pl.BlockSpec((TM, n), lambda i: (i, 0)),
                   pl.BlockSpec((TM, D), lambda i: (i, 0))],
        out_shape=[jax.ShapeDtypeStruct((t, n), BF16),
                   jax.ShapeDtypeStruct((t, D), BF16)],
        args=(h, gain, w, bias))


def in_proj_bwd_x(dz, w, h, gain, dh, ride=None):
    t = h.shape[0]
    n = w.shape[1]

    def body(dz_ref, w_ref, h_ref, g_ref, dh_ref, dhp_ref, dgain_ref, dbias_ref):
        @pl.when(pl.program_id(0) == 0)
        def _():
            dgain_ref[...] = jnp.zeros_like(dgain_ref)
            dbias_ref[...] = jnp.zeros_like(dbias_ref)

        du = _dot_nt(dz_ref[...], w_ref[...])
        dx, dgain = _rms_bwd(h_ref[...], g_ref[...], du)
        dhp_ref[...] = dh_ref[...] + dx
        dgain_ref[...] += dgain
        dbias_ref[...] += jnp.sum(dz_ref[...].astype(F32), axis=0, keepdims=True)

    return _call(
        body, name=f"in_proj_bwd_x_{n}", grid=(t // TM,), ride=ride,
        in_specs=[pl.BlockSpec((TM, n), lambda i: (i, 0)),
                  pl.BlockSpec((D, n), lambda i: (0, 0)),
                  pl.BlockSpec((TM, D), lambda i: (i, 0)),
                  pl.BlockSpec((1, D), lambda i: (0, 0)),
                  pl.BlockSpec((TM, D), lambda i: (i, 0))],
        out_specs=[pl.BlockSpec((TM, D), lambda i: (i, 0)),
                   pl.BlockSpec((1, D), lambda i: (0, 0)),
                   pl.BlockSpec((1, n), lambda i: (0, 0))],
        out_shape=[jax.ShapeDtypeStruct((t, D), F32),
                   jax.ShapeDtypeStruct((1, D), F32),
                   jax.ShapeDtypeStruct((1, n), F32)],
        args=(dz, w, h, gain, dh))


def in_proj_bwd_w(u, dz, ride=None):
    t = u.shape[0]
    n = dz.shape[1]
    steps = t // TM

    def body(u_ref, dz_ref, dw_ref, acc):
        s = pl.program_id(0)

        @pl.when(s == 0)
        def _():
            acc[...] = jnp.zeros_like(acc)

        acc[...] += _dot_tn(u_ref[...], dz_ref[...])

        @pl.when(s == steps - 1)
        def _():
            dw_ref[...] = acc[...].astype(BF16)

    return _call(
        body, name=f"in_proj_bwd_w_{n}", grid=(steps,), ride=ride,
        in_specs=[pl.BlockSpec((TM, D), lambda s: (s, 0)),
                  pl.BlockSpec((TM, n), lambda s: (s, 0))],
        out_specs=[pl.BlockSpec((D, n), lambda s: (0, 0))],
        out_shape=[jax.ShapeDtypeStruct((D, n), BF16)],
        scratch_shapes=[pltpu.VMEM((D, n), F32)],
        args=(u, dz))


A_TAPS = 3


def a_mix_fwd(z, h, conv, wout, ride=None):
    t = h.shape[0]

    def body(z_ref, zp_ref, h_ref, cw_ref, wo_ref, hn_ref, pad):
        i = pl.program_id(0)
        ph = zp_ref[:, D:2 * D].astype(F32) * zp_ref[:, 2 * D:].astype(F32)
        pad[0:HALO, :] = jnp.where(i == 0, 0.0, ph)
        pad[HALO:, :] = z_ref[:, D:2 * D].astype(F32) * z_ref[:, 2 * D:].astype(F32)
        q = jnp.zeros((TM, D), F32)
        for k in range(A_TAPS):
            off = HALO - (A_TAPS - 1) + k
            q += cw_ref[k:k + 1, :] * pad[off:off + TM, :]
        r = z_ref[:, 0:D].astype(F32) * q
        hn_ref[...] = h_ref[...] + _dot(r.astype(BF16), wo_ref[...])

    return _call(
        body, name="a_mix_fwd", grid=(t // TM,), ride=ride,
        in_specs=[pl.BlockSpec((TM, 3 * D), lambda i: (i, 0)),
                  pl.BlockSpec((HALO, 3 * D), lambda i: (_prev_halo(i), 0)),
                  pl.BlockSpec((TM, D), lambda i: (i, 0)),
                  pl.BlockSpec((A_TAPS, D), lambda i: (0, 0)),
                  pl.BlockSpec((D, D), lambda i: (0, 0))],
        out_specs=[pl.BlockSpec((TM, D), lambda i: (i, 0))],
        out_shape=[jax.ShapeDtypeStruct((t, D), F32)],
        scratch_shapes=[pltpu.VMEM((HALO + TM, D), F32)],
        args=(z, z, h, conv, wout))


def a_mix_bwd(dh, z, conv, wout, ride=None):
    t = dh.shape[0]
    steps = t // TM

    def body(dh_ref, dhn_ref, z_ref, zp_ref, zn_ref, cw_ref, wo_ref, dz_ref, dwo_ref, dcw_ref, pad, dqpad, dwo):
        i = pl.program_id(0)
        last = i == steps - 1

        @pl.when(i == 0)
        def _():
            dwo[...] = jnp.zeros_like(dwo)
            dcw_ref[...] = jnp.zeros_like(dcw_ref)

        ph = zp_ref[:, D:2 * D].astype(F32) * zp_ref[:, 2 * D:].astype(F32)
        pad[0:HALO, :] = jnp.where(i == 0, 0.0, ph)
        c = z_ref[:, D:2 * D].astype(F32)
        v = z_ref[:, 2 * D:].astype(F32)
        pad[HALO:, :] = c * v
        q = jnp.zeros((TM, D), F32)
        for k in range(A_TAPS):
            off = HALO - (A_TAPS - 1) + k
            q += cw_ref[k:k + 1, :] * pad[off:off + TM, :]
        b = z_ref[:, 0:D].astype(F32)
        dhb = dh_ref[...].astype(BF16)
        dwo[...] += _dot_tn((b * q).astype(BF16), dhb)
        dr = _dot_nt(dhb, wo_ref[...])
        dz_ref[:, 0:D] = (dr * q).astype(BF16)
        dq = dr * b
        drn = _dot_nt(dhn_ref[...].astype(BF16), wo_ref[...])
        dqpad[0:TM, :] = dq
        dqpad[TM:, :] = jnp.where(last, 0.0, drn * zn_ref[:, 0:D].astype(F32))
        dp = jnp.zeros((TM, D), F32)
        for k in range(A_TAPS):
            off = A_TAPS - 1 - k
            dp += cw_ref[k:k + 1, :] * dqpad[off:off + TM, :]
            poff = HALO - (A_TAPS - 1) + k
            dcw_ref[k:k + 1, :] += jnp.sum(dq * pad[poff:poff + TM, :], axis=0, keepdims=True)
        dz_ref[:, D:2 * D] = (dp * v).astype(BF16)
        dz_ref[:, 2 * D:] = (dp * c).astype(BF16)

        @pl.when(last)
        def _():
            dwo_ref[...] = dwo[...].astype(BF16)

    return _call(
        body, name="a_mix_bwd", grid=(steps,), ride=ride,
        in_specs=[pl.BlockSpec((TM, D), lambda i: (i, 0)),
                  pl.BlockSpec((HALO, D), lambda i: (_next_halo(i, t), 0)),
                  pl.BlockSpec((TM, 3 * D), lambda i: (i, 0)),
                  pl.BlockSpec((HALO, 3 * D), lambda i: (_prev_halo(i), 0)),
                  pl.BlockSpec((HALO, 3 * D), lambda i: (_next_halo(i, t), 0)),
                  pl.BlockSpec((A_TAPS, D), lambda i: (0, 0)),
                  pl.BlockSpec((D, D), lambda i: (0, 0))],
        out_specs=[pl.BlockSpec((TM, 3 * D), lambda i: (i, 0)),
                   pl.BlockSpec((D, D), lambda i: (0, 0)),
                   pl.BlockSpec((A_TAPS, D), lambda i: (0, 0))],
        out_shape=[jax.ShapeDtypeStruct((t, 3 * D), BF16),
                   jax.ShapeDtypeStruct((D, D), BF16),
                   jax.ShapeDtypeStruct((A_TAPS, D), F32)],
        scratch_shapes=[pltpu.VMEM((HALO + TM, D), F32), pltpu.VMEM((TM + HALO, D), F32), pltpu.VMEM((D, D), F32)],
        args=(dh, dh, z, z, z, conv, wout))


C_TAPS = 31


def _glu(zr):
    return zr[:, 0:D].astype(F32) * jax.nn.sigmoid(zr[:, D:].astype(F32))


def _ln_silu(h2, lg, lb):
    mu = jnp.mean(h2, axis=-1, keepdims=True)
    xc = h2 - mu
    rstd = lax.rsqrt(jnp.mean(xc * xc, axis=-1, keepdims=True) + LN_EPS)
    xn = xc * rstd
    h3 = xn * lg + lb
    s3 = jax.nn.sigmoid(h3)
    return xn, rstd, h3, s3


def _ln_silu_bwd(h2, lg, lb, dh4):
    xn, rstd, h3, s3 = _ln_silu(h2, lg, lb)
    dh3 = dh4 * (s3 * (1.0 + h3 * (1.0 - s3)))
    dxn = dh3 * lg
    dh2 = rstd * (dxn - jnp.mean(dxn, axis=-1, keepdims=True) - xn * jnp.mean(dxn * xn, axis=-1, keepdims=True))
    return dh2, dh3, xn, h3 * s3


TC = 256
RB = 64
LANES = 128
SHIFTS = 7


def _shifted_copies(src, sh, rows):
    for b in range(1, SHIFTS + 1):
        sh[b - 1, 0:rows, :] = src[b:b + rows, :]


def _window(src, sh, o, r0, lanes):
    a, b = divmod(o, 8)
    ref = src if b == 0 else sh.at[b - 1]
    return ref[8 * a + r0:8 * a + r0 + RB, lanes]


def c_mix_fwd(z, h, dw, bdw, lg, lb, w2, b2, ride=None):
    t = h.shape[0]

    def body(z_ref, zp_ref, h_ref, dw_ref, bdw_ref, lg_ref, lb_ref, w2_ref, b2_ref, hn_ref, h2_ref, pad, sh):
        i = pl.program_id(0)
        pad[0:HALO, :] = jnp.where(i == 0, 0.0, _glu(zp_ref))
        pad[HALO:, :] = _glu(z_ref)
        _shifted_copies(pad, sh, TC + 24)
        for l in range(D // LANES):
            lanes = slice(l * LANES, (l + 1) * LANES)
            for r0 in range(0, TC, RB):
                acc = jnp.zeros((RB, LANES), F32) + bdw_ref[:, lanes]
                for k in range(C_TAPS):
                    acc += dw_ref[k:k + 1, lanes] * _window(pad, sh, HALO - (C_TAPS - 1) + k, r0, lanes)
                h2_ref[r0:r0 + RB, lanes] = acc
        _, _, h3, s3 = _ln_silu(h2_ref[...], lg_ref[...], lb_ref[...])
        hn_ref[...] = h_ref[...] + _dot((h3 * s3).astype(BF16), w2_ref[...]) + b2_ref[...]

    vec = pl.BlockSpec((1, D), lambda i: (0, 0))
    return _call(
        body, name="c_mix_fwd", grid=(t // TC,), ride=ride,
        in_specs=[pl.BlockSpec((TC, 2 * D), lambda i: (i, 0)),
                  pl.BlockSpec((HALO, 2 * D), lambda i: (_prev_halo(i, TC), 0)),
                  pl.BlockSpec((TC, D), lambda i: (i, 0)),
                  pl.BlockSpec((C_TAPS, D), lambda i: (0, 0)),
                  vec, vec, vec,
                  pl.BlockSpec((D, D), lambda i: (0, 0)),
                  vec],
        out_specs=[pl.BlockSpec((TC, D), lambda i: (i, 0)),
                   pl.BlockSpec((TC, D), lambda i: (i, 0))],
        out_shape=[jax.ShapeDtypeStruct((t, D), F32),
                   jax.ShapeDtypeStruct((t, D), F32)],
        scratch_shapes=[pltpu.VMEM((HALO + TC, D), F32), pltpu.VMEM((SHIFTS, TC + 24, D), F32)],
        args=(z, z, h, dw, bdw, lg, lb, w2, b2))


def c_mix_bwd(dh, z, h2, dw, lg, lb, w2, ride=None):
    t = dh.shape[0]
    steps = t // TC

    def body(dh_ref, dhn_ref, z_ref, zp_ref, h2_ref, h2n_ref, dw_ref, lg_ref, lb_ref, w2_ref,
             dz_ref, dw2_ref, db2_ref, dlg_ref, dlb_ref, dbdw_ref, ddw_ref, pad, dpad, dw2, sh, dh1):
        i = pl.program_id(0)
        last = i == steps - 1

        @pl.when(i == 0)
        def _():
            for r in (dw2, db2_ref, dlg_ref, dlb_ref, dbdw_ref, ddw_ref):
                r[...] = jnp.zeros_like(r)

        lg, lb = lg_ref[...], lb_ref[...]
        dh = dh_ref[...]
        dhb = dh.astype(BF16)
        dh2, dh3, xn, h4 = _ln_silu_bwd(h2_ref[...], lg, lb, _dot_nt(dhb, w2_ref[...]))
        dw2[...] += _dot_tn(h4.astype(BF16), dhb)
        db2_ref[...] += jnp.sum(dh, axis=0, keepdims=True)
        dlg_ref[...] += jnp.sum(dh3 * xn, axis=0, keepdims=True)
        dlb_ref[...] += jnp.sum(dh3, axis=0, keepdims=True)
        dbdw_ref[...] += jnp.sum(dh2, axis=0, keepdims=True)
        dh2n, _, _, _ = _ln_silu_bwd(h2n_ref[...], lg, lb, _dot_nt(dhn_ref[...].astype(BF16), w2_ref[...]))
        dpad[0:TC, :] = dh2
        dpad[TC:, :] = jnp.where(last, 0.0, dh2n)
        _shifted_copies(dpad, sh, TC + 24)
        for l in range(D // LANES):
            lanes = slice(l * LANES, (l + 1) * LANES)
            for r0 in range(0, TC, RB):
                acc = jnp.zeros((RB, LANES), F32)
                for k in range(C_TAPS):
                    acc += dw_ref[k:k + 1, lanes] * _window(dpad, sh, C_TAPS - 1 - k, r0, lanes)
                dh1[r0:r0 + RB, lanes] = acc
        pad[0:HALO, :] = jnp.where(i == 0, 0.0, _glu(zp_ref))
        pad[HALO:, :] = _glu(z_ref)
        _shifted_copies(pad, sh, TC + 24)
        for l in range(D // LANES):
            lanes = slice(l * LANES, (l + 1) * LANES)
            accs = [jnp.zeros((8, LANES), F32) for _ in range(C_TAPS)]
            for r0 in range(0, TC, RB):
                d = dpad[r0:r0 + RB, lanes]
                for k in range(C_TAPS):
                    prod = d * _window(pad, sh, HALO - (C_TAPS - 1) + k, r0, lanes)
                    accs[k] += jnp.sum(prod.reshape(RB // 8, 8, LANES), axis=0)
            for k in range(C_TAPS):
                ddw_ref[k:k + 1, lanes] += jnp.sum(accs[k], axis=0, keepdims=True)
        a = z_ref[:, 0:D].astype(F32)
        sg = jax.nn.sigmoid(z_ref[:, D:].astype(F32))
        d1 = dh1[...]
        dz_ref[:, 0:D] = (d1 * sg).astype(BF16)
        dz_ref[:, D:] = (d1 * a * sg * (1.0 - sg)).astype(BF16)

        @pl.when(last)
        def _():
            dw2_ref[...] = dw2[...].astype(BF16)

    vec = pl.BlockSpec((1, D), lambda i: (0, 0))
    return _call(
        body, name="c_mix_bwd", grid=(steps,), ride=ride,
        in_specs=[pl.BlockSpec((TC, D), lambda i: (i, 0)),
                  pl.BlockSpec((HALO, D), lambda i: (_next_halo(i, t, TC), 0)),
                  pl.BlockSpec((TC, 2 * D), lambda i: (i, 0)),
                  pl.BlockSpec((HALO, 2 * D), lambda i: (_prev_halo(i, TC), 0)),
                  pl.BlockSpec((TC, D), lambda i: (i, 0)),
                  pl.BlockSpec((HALO, D), lambda i: (_next_halo(i, t, TC), 0)),
                  pl.BlockSpec((C_TAPS, D), lambda i: (0, 0)),
                  vec, vec,
                  pl.BlockSpec((D, D), lambda i: (0, 0))],
        out_specs=[pl.BlockSpec((TC, 2 * D), lambda i: (i, 0)),
                   pl.BlockSpec((D, D), lambda i: (0, 0)),
                   vec, vec, vec, vec,
                   pl.BlockSpec((C_TAPS, D), lambda i: (0, 0))],
        out_shape=[jax.ShapeDtypeStruct((t, 2 * D), BF16),
                   jax.ShapeDtypeStruct((D, D), BF16)]
                  + [jax.ShapeDtypeStruct((1, D), F32)] * 4
                  + [jax.ShapeDtypeStruct((C_TAPS, D), F32)],
        scratch_shapes=[pltpu.VMEM((HALO + TC, D), F32), pltpu.VMEM((TC + HALO, D), F32), pltpu.VMEM((D, D), F32),
                        pltpu.VMEM((SHIFTS, TC + 24, D), F32), pltpu.VMEM((TC, D), F32)],
        args=(dh, dh, z, z, h2, h2, dw, lg, lb, w2))


POOL_WINDOWS = (2, 4, 8, 16)
GW = D // len(POOL_WINDOWS)


def _pool_mixed(pad, g, w, inv_cnt):
    cols = slice(g * GW, (g + 1) * GW)
    s = pad[HALO:HALO + TM, cols]
    u = s
    for j in range(1, w):
        s = s + pad[HALO - j:HALO - j + TM, cols]
    return s * inv_cnt - u


def _inv_cnt(i, w):
    row = i * TM + lax.broadcasted_iota(jnp.int32, (TM, 1), 0)
    return 1.0 / jnp.minimum(row + 1, w).astype(F32)


def b_mix_fwd(h, gain, wg, scale, ride=None):
    t = h.shape[0]

    def body(h_ref, hp_ref, g_ref, wg_ref, sc_ref, hn_ref, pad):
        i = pl.program_id(0)
        gain = g_ref[...]
        pad[0:HALO, :] = jnp.where(i == 0, 0.0, _rms_fwd(hp_ref[...], gain))
        pad[HALO:, :] = _rms_fwd(h_ref[...], gain)
        for g, w in enumerate(POOL_WINDOWS):
            cols = slice(g * GW, (g + 1) * GW)
            mixed = _pool_mixed(pad, g, w, _inv_cnt(i, w))
            y = _dot(mixed.astype(BF16), wg_ref[g])
            hn_ref[:, cols] = h_ref[:, cols] + y * sc_ref[:, cols]

    return _call(
        body, name="b_mix_fwd", grid=(t // TM,), ride=ride,
        in_specs=[pl.BlockSpec((TM, D), lambda i: (i, 0)),
                  pl.BlockSpec((HALO, D), lambda i: (_prev_halo(i), 0)),
                  pl.BlockSpec((1, D), lambda i: (0, 0)),
                  pl.BlockSpec((4, GW, GW), lambda i: (0, 0, 0)),
                  pl.BlockSpec((1, D), lambda i: (0, 0))],
        out_specs=[pl.BlockSpec((TM, D), lambda i: (i, 0))],
        out_shape=[jax.ShapeDtypeStruct((t, D), F32)],
        scratch_shapes=[pltpu.VMEM((HALO + TM, D), F32)],
        args=(h, h, gain, wg, scale))


def b_mix_bwd(dh, h, gain, wg, scale, ride=None):
    t = h.shape[0]
    steps = t // TM

    def body(dh_ref, dhn_ref, h_ref, hp_ref, g_ref, wg_ref, sc_ref, dhp_ref, dgain_ref, dwg_ref, dsc_ref, pad, dpad, du):
        i = pl.program_id(0)
        last = i == steps - 1

        @pl.when(i == 0)
        def _():
            for r in (dgain_ref, dwg_ref, dsc_ref):
                r[...] = jnp.zeros_like(r)

        gain = g_ref[...]
        pad[0:HALO, :] = jnp.where(i == 0, 0.0, _rms_fwd(hp_ref[...], gain))
        pad[HALO:, :] = _rms_fwd(h_ref[...], gain)
        for g, w in enumerate(POOL_WINDOWS):
            cols = slice(g * GW, (g + 1) * GW)
            inv_cnt = _inv_cnt(i, w)
            mixed = _pool_mixed(pad, g, w, inv_cnt).astype(BF16)
            dh = dh_ref[:, cols]
            dsc_ref[:, cols] += jnp.sum(dh * _dot(mixed, wg_ref[g]), axis=0, keepdims=True)
            dy = (dh * sc_ref[:, cols]).astype(BF16)
            dwg_ref[g] += _dot_tn(mixed, dy)
            dm = _dot_nt(dy, wg_ref[g])
            dmn = _dot_nt((dhn_ref[:, cols] * sc_ref[:, cols]).astype(BF16), wg_ref[g])
            dpad[0:TM, cols] = dm * inv_cnt
            dpad[TM:, cols] = jnp.where(last, 0.0, dmn * (1.0 / w))
            s = dpad[0:TM, cols]
            for j in range(1, w):
                s = s + dpad[j:j + TM, cols]
            du[:, cols] = s - dm
        dx, dgain = _rms_bwd(h_ref[...], gain, du[...])
        dhp_ref[...] = dh_ref[...] + dx
        dgain_ref[...] += dgain

    return _call(
        body, name="b_mix_bwd", grid=(steps,), ride=ride,
        in_specs=[pl.BlockSpec((TM, D), lambda i: (i, 0)),
                  pl.BlockSpec((HALO, D), lambda i: (_next_halo(i, t), 0)),
                  pl.BlockSpec((TM, D), lambda i: (i, 0)),
                  pl.BlockSpec((HALO, D), lambda i: (_prev_halo(i), 0)),
                  pl.BlockSpec((1, D), lambda i: (0, 0)),
                  pl.BlockSpec((4, GW, GW), lambda i: (0, 0, 0)),
                  pl.BlockSpec((1, D), lambda i: (0, 0))],
        out_specs=[pl.BlockSpec((TM, D), lambda i: (i, 0)),
                   pl.BlockSpec((1, D), lambda i: (0, 0)),
                   pl.BlockSpec((4, GW, GW), lambda i: (0, 0, 0)),
                   pl.BlockSpec((1, D), lambda i: (0, 0))],
        out_shape=[jax.ShapeDtypeStruct((t, D), F32),
                   jax.ShapeDtypeStruct((1, D), F32),
                   jax.ShapeDtypeStruct((4, GW, GW), F32),
                   jax.ShapeDtypeStruct((1, D), F32)],
        scratch_shapes=[pltpu.VMEM((HALO + TM, D), F32), pltpu.VMEM((TM + HALO, D), F32), pltpu.VMEM((TM, D), F32)],
        args=(dh, dh, h, h, gain, wg, scale))


LOSS_LANES = 128


def loss_head(h, gain, target):
    t = h.shape[0]

    def body(h_ref, g_ref, tg_ref, loss_ref, dh_ref, dgain_ref):
        @pl.when(pl.program_id(0) == 0)
        def _():
            loss_ref[...] = jnp.zeros_like(loss_ref)
            dgain_ref[...] = jnp.zeros_like(dgain_ref)

        x, gain = h_ref[...], g_ref[...]
        err = _rms_fwd(x, gain) - tg_ref[...]
        per_row = jnp.mean(err * err, axis=-1, keepdims=True)
        loss_ref[...] += jnp.broadcast_to(0.5 * jnp.sum(per_row, axis=0, keepdims=True), (1, LOSS_LANES))
        dx, dgain = _rms_bwd(x, gain, err * (1.0 / D))
        dh_ref[...] = dx
        dgain_ref[...] += dgain

    outs, _ = _call(
        body, name="loss_head", grid=(t // TM,),
        in_specs=[pl.BlockSpec((TM, D), lambda i: (i, 0)),
                  pl.BlockSpec((1, D), lambda i: (0, 0)),
                  pl.BlockSpec((TM, D), lambda i: (i, 0))],
        out_specs=[pl.BlockSpec((1, LOSS_LANES), lambda i: (0, 0)),
                   pl.BlockSpec((TM, D), lambda i: (i, 0)),
                   pl.BlockSpec((1, D), lambda i: (0, 0))],
        out_shape=[jax.ShapeDtypeStruct((1, LOSS_LANES), F32),
                   jax.ShapeDtypeStruct((t, D), F32),
                   jax.ShapeDtypeStruct((1, D), F32)],
        args=(h, gain, target))
    return outs


ADAM_LR = 0.001
ADAM_B1 = 0.9
ADAM_B2 = 0.999
ADAM_EPS = 1e-08
ADAM_WD = 0.01
ADAM_STEP = 10


def cast_all(arrays):
    def body(*refs):
        for src, dst in zip(refs[:len(arrays)], refs[len(arrays):]):
            dst[...] = src[...].astype(BF16)

    return pl.pallas_call(
        body, name="cast_all", out_shape=[jax.ShapeDtypeStruct(a.shape, BF16) for a in arrays],
        compiler_params=pltpu.CompilerParams(vmem_limit_bytes=VMEM_LIMIT),
    )(*arrays)


def _adam_math(w, m, v, g):
    m = ADAM_B1 * m + (1.0 - ADAM_B1) * g
    v = ADAM_B2 * v + (1.0 - ADAM_B2) * (g * g)
    m_hat = m / (1.0 - ADAM_B1 ** ADAM_STEP)
    v_hat = v / (1.0 - ADAM_B2 ** ADAM_STEP)
    return -ADAM_LR * (m_hat / (jnp.sqrt(v_hat) + ADAM_EPS) + ADAM_WD * w), m, v


def adamw(w, m, v, gparts, rb):
    r, c = w.shape

    def body(w_ref, m_ref, v_ref, gp_ref, g_ref, d_ref, nm_ref, nv_ref):
        g = gp_ref[0].astype(F32)
        for s in range(1, N_DEV):
            g = g + gp_ref[s].astype(F32)
        g_ref[...] = g
        d_ref[...], nm_ref[...], nv_ref[...] = _adam_math(w_ref[...], m_ref[...], v_ref[...], g)

    blk = pl.BlockSpec((rb, c), lambda i: (i, 0))
    outs, _ = _call(
        body, name=f"adamw_{r}x{c}", grid=(r // rb,),
        in_specs=[blk, blk, blk, pl.BlockSpec((N_DEV, rb, c), lambda i: (0, i, 0))],
        out_specs=[blk] * 4,
        out_shape=[jax.ShapeDtypeStruct((r, c), F32)] * 4,
        args=(w, m, v, gparts))
    return outs


def adamw_vectors(ws, ms, vs, gparts):
    nv = len(ws)

    def body(*refs):
        w_refs, m_refs, v_refs = refs[:nv], refs[nv:2 * nv], refs[2 * nv:3 * nv]
        gp_ref = refs[3 * nv]
        outs = refs[3 * nv + 1:]
        g_refs, d_refs, nm_refs, nv_refs = outs[:nv], outs[nv:2 * nv], outs[2 * nv:3 * nv], outs[3 * nv:]
        row = 0
        for i in range(nv):
            for part in range(w_refs[i].shape[1] // D):
                cols = slice(part * D, (part + 1) * D)
                g = gp_ref[0, row:row + 1, :]
                for s in range(1, N_DEV):
                    g = g + gp_ref[s, row:row + 1, :]
                g_refs[i][:, cols] = g
                d_refs[i][:, cols], nm_refs[i][:, cols], nv_refs[i][:, cols] = _adam_math(
                    w_refs[i][:, cols], m_refs[i][:, cols], v_refs[i][:, cols], g)
                row += 1

    shapes = [jax.ShapeDtypeStruct(w.shape, F32) for w in ws]
    outs = pl.pallas_call(body, name="adamw_vectors", out_shape=shapes * 4)(*ws, *ms, *vs, gparts)
    return outs[:nv], outs[nv:2 * nv], outs[2 * nv:3 * nv], outs[3 * nv:]


WEIGHTS = ["ln1_0", "a0_w_in", "a0_conv", "a0_w_out", "ln2_0", "ffn0_w_gu", "ffn0_w_down",
           "ln1_1", "b1_w_grp", "b1_scale", "ln2_1", "ffn1_w_gu", "ffn1_w_down",
           "ln1_2", "c2_w_pw1", "c2_b_pw1", "c2_dw", "c2_b_dw", "c2_ln_g", "c2_ln_b", "c2_w_pw2", "c2_b_pw2",
           "ln2_2", "ffn2_w_gu", "ffn2_w_down",
           "ln1_3", "a3_w_in", "a3_conv", "a3_w_out", "ln2_3", "ffn3_w_gu", "ffn3_w_down", "ln_f"]
SHARDED = {"a0_w_in": ("cols", 256), "a0_conv": ("cols", A_TAPS), "a0_w_out": ("rows", 128),
           "ffn0_w_gu": ("lead", 176), "ffn0_w_down": ("rows", 176),
           "b1_w_grp": ("mid", 128),
           "ffn1_w_gu": ("lead", 176), "ffn1_w_down": ("rows", 176),
           "c2_w_pw1": ("cols", 256), "c2_dw": ("cols", C_TAPS), "c2_w_pw2": ("rows", 128),
           "ffn2_w_gu": ("lead", 176), "ffn2_w_down": ("rows", 176),
           "a3_w_in": ("cols", 256), "a3_conv": ("cols", A_TAPS), "a3_w_out": ("rows", 128),
           "ffn3_w_gu": ("lead", 176), "ffn3_w_down": ("rows", 176)}
IN_PROJ = ("a0_w_in", "c2_w_pw1", "a3_w_in")
REPL = [n for n in WEIGHTS if n not in SHARDED]
REPL_ROWS = 16
GATHER_PLAN = {"first": ["a0_w_in", "a0_w_out", "a0_conv"],
               "in0": ["ffn0_w_gu"], "mix0": ["ffn0_w_down"],
               "ffn0": ["b1_w_grp", "ffn1_w_gu", "ffn1_w_down"],
               "ffn1": ["c2_w_pw1", "c2_w_pw2", "c2_dw", "ffn2_w_gu"],
               "in2": ["ffn2_w_down"],
               "mix2": ["a3_w_in", "a3_w_out", "a3_conv"],
               "ffn2": ["ffn3_w_gu", "ffn3_w_down"]}
SCATTER_PLAN = {"mixb3": ["ffn3_w_down"], "inw3": ["a3_w_out", "a3_conv"],
                "ffnx2": ["ffn3_w_gu"], "ffnw2": ["a3_w_in"],
                "mixb2": ["ffn2_w_gu", "ffn2_w_down"], "inw2": ["c2_w_pw2", "c2_dw"],
                "ffnx1": ["c2_w_pw1"],
                "ffnx0": ["ffn1_w_gu"], "ffnw0": ["ffn1_w_down", "b1_w_grp"],
                "mixb0": ["ffn0_w_down"], "inw0": ["ffn0_w_gu"], "inx0": ["a0_w_out", "a0_conv", "a0_w_in"],
                "last": ["repl"]}


def _step(p):
    vec = lambda n: p[n].reshape(1, -1)
    x, target = p["x"][0], p["loss_target"][0]

    names = list(SHARDED)
    stored = lambda n, a: a.T if n.endswith("w_gu") else a
    shard = dict(zip(names, cast_all([stored(n, p[n]) for n in names])))
    full = {}

    def gather(slot):
        names = GATHER_PLAN[slot]
        return gather_ride([shard[n] for n in names], ["cols" if n in IN_PROJ else "lead" for n in names])

    def landed(slot, outs):
        full.update(zip(GATHER_PLAN[slot], outs))

    def conv_full(n):
        k = full[n].shape[1]
        return full[n].transpose(1, 0, 2).reshape(k, D).astype(F32)

    def wgu(i):
        return full[f"ffn{i}_w_gu"].reshape(2, 4, FC, D)

    def wd(i):
        return full[f"ffn{i}_w_down"].reshape(4, FC, D)

    landed("first", run_ride(gather("first"), "gather_first"))
    no_bias = jnp.zeros((1, 3 * D), F32)
    h = [x]
    saved = {}
    conv, wout = {}, {}

    (z, u), got = rms_matmul(h[-1], vec("ln1_0"), full["a0_w_in"], no_bias, ride=gather("in0"))
    landed("in0", got)
    conv[0], wout[0] = conv_full("a0_conv"), full["a0_w_out"].reshape(D, D)
    (hm,), got = a_mix_fwd(z, h[-1], conv[0], wout[0], ride=gather("mix0"))
    landed("mix0", got)
    saved["mix0"] = (z, u)
    h.append(hm)
    (hn, zf, uf), got = ffn_fwd(hm, vec("ln2_0"), wgu(0), wd(0), ride=gather("ffn0"))
    landed("ffn0", got)
    saved["ffn0"] = (zf, uf)
    h.append(hn)

    wgrp = full["b1_w_grp"].transpose(1, 0, 2, 3).reshape(4, GW, GW)
    (hm,), _ = b_mix_fwd(h[-1], vec("ln1_1"), wgrp, vec("b1_scale"))
    h.append(hm)
    (hn, zf, uf), got = ffn_fwd(hm, vec("ln2_1"), wgu(1), wd(1), ride=gather("ffn1"))
    landed("ffn1", got)
    saved["ffn1"] = (zf, uf)
    h.append(hn)

    (z, u), got = rms_matmul(h[-1], vec("ln1_2"), full["c2_w_pw1"], vec("c2_b_pw1"), ride=gather("in2"))
    landed("in2", got)
    cdw, wpw2 = conv_full("c2_dw"), full["c2_w_pw2"].reshape(D, D)
    (hm, h2), got = c_mix_fwd(z, h[-1], cdw, vec("c2_b_dw"), vec("c2_ln_g"), vec("c2_ln_b"), wpw2, vec("c2_b_pw2"),
                              ride=gather("mix2"))
    landed("mix2", got)
    saved["mix2"] = (z, u, h2)
    h.append(hm)
    (hn, zf, uf), got = ffn_fwd(hm, vec("ln2_2"), wgu(2), wd(2), ride=gather("ffn2"))
    landed("ffn2", got)
    saved["ffn2"] = (zf, uf)
    h.append(hn)

    (z, u), _ = rms_matmul(h[-1], vec("ln1_3"), full["a3_w_in"], no_bias)
    conv[3], wout[3] = conv_full("a3_conv"), full["a3_w_out"].reshape(D, D)
    (hm,), _ = a_mix_fwd(z, h[-1], conv[3], wout[3])
    saved["mix3"] = (z, u)
    h.append(hm)
    (hn, zf, uf), _ = ffn_fwd(hm, vec("ln2_3"), wgu(3), wd(3))
    saved["ffn3"] = (zf, uf)
    h.append(hn)

    loss_lanes, dh, g_lnf = loss_head(h[-1], vec("ln_f"), target)

    g = {"ln_f": g_lnf}
    recv = {}

    def repl_rows():
        return jnp.concatenate([g[n].reshape(-1, D) for n in REPL], axis=0)

    def scatter(slot):
        parts = []
        for n in SCATTER_PLAN[slot]:
            parts.append((repl_rows(), "all") if n == "repl" else (g[n], SHARDED[n][0]))
        return scatter_ride(parts)

    def arrived(slot, outs):
        recv.update(zip(SCATTER_PLAN[slot], outs))

    for i in (3, 2, 1, 0):
        zf, uf = saved[f"ffn{i}"]
        dh_in = dh
        (dh, dzf, a, g[f"ln2_{i}"], dhb), got = ffn_bwd_x(dh_in, h[2 * i + 1], vec(f"ln2_{i}"), zf, wgu(i), wd(i),
                                                      ride=scatter(f"ffnx{i}") if f"ffnx{i}" in SCATTER_PLAN else None)
        if got:
            arrived(f"ffnx{i}", got)
        (dwgu, dwd), got = ffn_bwd_w(uf, dzf, a, dhb, ride=scatter(f"ffnw{i}") if f"ffnw{i}" in SCATTER_PLAN else None)
        if got:
            arrived(f"ffnw{i}", got)
        g[f"ffn{i}_w_gu"], g[f"ffn{i}_w_down"] = dwgu.reshape(N_DEV, FC, D), dwd.reshape(FF, D)
        hin = h[2 * i]
        if i in (0, 3):
            z, u = saved[f"mix{i}"]
            (dz, g[f"a{i}_w_out"], g[f"a{i}_conv"]), got = a_mix_bwd(dh, z, conv[i], wout[i], ride=scatter(f"mixb{i}"))
            arrived(f"mixb{i}", got)
            (g[f"a{i}_w_in"],), got = in_proj_bwd_w(u, dz, ride=scatter(f"inw{i}"))
            arrived(f"inw{i}", got)
            (dh, g[f"ln1_{i}"], _), got = in_proj_bwd_x(dz, full[f"a{i}_w_in"], hin, vec(f"ln1_{i}"), dh,
                                                       ride=scatter(f"inx{i}") if f"inx{i}" in SCATTER_PLAN else None)
            if got:
                arrived(f"inx{i}", got)
        elif i == 1:
            (dh, g["ln1_1"], g["b1_w_grp"], g["b1_scale"]), _ = b_mix_bwd(dh, hin, vec("ln1_1"), wgrp, vec("b1_scale"))
        else:
            z, u, h2 = saved["mix2"]
            (dz, g["c2_w_pw2"], g["c2_b_pw2"], g["c2_ln_g"], g["c2_ln_b"], g["c2_b_dw"], g["c2_dw"]), got = c_mix_bwd(
                dh, z, h2, cdw, vec("c2_ln_g"), vec("c2_ln_b"), wpw2, ride=scatter("mixb2"))
            arrived("mixb2", got)
            (g["c2_w_pw1"],), got = in_proj_bwd_w(u, dz, ride=scatter("inw2"))
            arrived("inw2", got)
            (dh, g["ln1_2"], g["c2_b_pw1"]), _ = in_proj_bwd_x(dz, full["c2_w_pw1"], hin, vec("ln1_2"), dh)
    grad_x = dh[None]
    arrived("last", run_ride(scatter("last"), "scatter_last"))

    grad, delta, new_m, new_v = {}, {}, {}, {}
    for n, (_, rb) in SHARDED.items():
        shape = p[n].shape
        two_d = lambda a: stored(n, a.reshape(-1, shape[-1]))
        w2 = two_d(p[n])
        outs = adamw(w2, two_d(p["m_" + n]), two_d(p["v_" + n]), recv[n].reshape(N_DEV, *w2.shape), rb)
        grad[n], delta[n], new_m[n], new_v[n] = [stored(n, o).reshape(shape) for o in outs]
    outs = adamw_vectors([vec(n) for n in REPL], [vec("m_" + n) for n in REPL], [vec("v_" + n) for n in REPL], recv["repl"])
    for res, o in zip((grad, delta, new_m, new_v), outs):
        res.update({n: a.reshape(p[n].shape) for n, a in zip(REPL, o)})

    loss = lax.psum(loss_lanes[0, 0], ("x", "y", "c"))
    return (loss, grad_x, *[grad[n] for n in WEIGHTS], *[delta[n] for n in WEIGHTS],
            *[new_m[n] for n in WEIGHTS], *[new_v[n] for n in WEIGHTS])


def kernel(x, ln1_0, a0_w_in, a0_conv, a0_w_out, ln2_0, ffn0_w_gu, ffn0_w_down, ln1_1, b1_w_grp, b1_scale, ln2_1, ffn1_w_gu, ffn1_w_down, ln1_2, c2_w_pw1, c2_b_pw1, c2_dw, c2_b_dw, c2_ln_g, c2_ln_b, c2_w_pw2, c2_b_pw2, ln2_2, ffn2_w_gu, ffn2_w_down, ln1_3, a3_w_in, a3_conv, a3_w_out, ln2_3, ffn3_w_gu, ffn3_w_down, ln_f, loss_target, m_ln1_0, m_a0_w_in, m_a0_conv, m_a0_w_out, m_ln2_0, m_ffn0_w_gu, m_ffn0_w_down, m_ln1_1, m_b1_w_grp, m_b1_scale, m_ln2_1, m_ffn1_w_gu, m_ffn1_w_down, m_ln1_2, m_c2_w_pw1, m_c2_b_pw1, m_c2_dw, m_c2_b_dw, m_c2_ln_g, m_c2_ln_b, m_c2_w_pw2, m_c2_b_pw2, m_ln2_2, m_ffn2_w_gu, m_ffn2_w_down, m_ln1_3, m_a3_w_in, m_a3_conv, m_a3_w_out, m_ln2_3, m_ffn3_w_gu, m_ffn3_w_down, m_ln_f, v_ln1_0, v_a0_w_in, v_a0_conv, v_a0_w_out, v_ln2_0, v_ffn0_w_gu, v_ffn0_w_down, v_ln1_1, v_b1_w_grp, v_b1_scale, v_ln2_1, v_ffn1_w_gu, v_ffn1_w_down, v_ln1_2, v_c2_w_pw1, v_c2_b_pw1, v_c2_dw, v_c2_b_dw, v_c2_ln_g, v_c2_ln_b, v_c2_w_pw2, v_c2_b_pw2, v_ln2_2, v_ffn2_w_gu, v_ffn2_w_down, v_ln1_3, v_a3_w_in, v_a3_conv, v_a3_w_out, v_ln2_3, v_ffn3_w_gu, v_ffn3_w_down, v_ln_f):
    return _step(dict(locals()))
```

```python
import jax
import jax.numpy as jnp
from jax import lax
from jax.experimental import pallas as pl
from jax.experimental.pallas import tpu as pltpu

F32 = jnp.float32
BF16 = jnp.bfloat16

N_DEV = 8
D = 1024
FF = 2816
FC = FF // 4
RMS_EPS = 1e-6
LN_EPS = 1e-5
TM = 512
HALO = 32
VMEM_LIMIT = 60 * 1024 * 1024

NT = (((1,), (1,)), ((), ()))
TN = (((0,), (0,)), ((), ()))
MESH = pl.DeviceIdType.MESH
ANY = pl.BlockSpec(memory_space=pl.ANY)
N_PEERS = N_DEV - 1


def _dot(a, b):
    return jnp.dot(a, b, preferred_element_type=F32)


def _dot_nt(a, b):
    return lax.dot_general(a, b, NT, preferred_element_type=F32)


def _dot_tn(a, b):
    return lax.dot_general(a, b, TN, preferred_element_type=F32)


def _rms_fwd(x, gain):
    r = lax.rsqrt(jnp.mean(x * x, axis=-1, keepdims=True) + RMS_EPS)
    return x * r * gain


def _rms_bwd(x, gain, du):
    r = lax.rsqrt(jnp.mean(x * x, axis=-1, keepdims=True) + RMS_EPS)
    xhat = x * r
    dgain = jnp.sum(du * xhat, axis=0, keepdims=True)
    dxhat = du * gain
    dx = r * (dxhat - xhat * jnp.mean(dxhat * xhat, axis=-1, keepdims=True))
    return dx, dgain


def _dev_index(p):
    return 4 * p[0] + 2 * p[1] + p[2]


def _place():
    return lax.axis_index("x"), lax.axis_index("y"), lax.axis_index("c")


class Ride:
    def __init__(self, ins, out_shapes, start, finish, mid=None):
        self.ins, self.out_shapes, self.start, self.finish, self.mid = list(ins), list(out_shapes), start, finish, mid
        n = len(self.ins)
        self.sems = [pltpu.SemaphoreType.DMA((n * N_PEERS,)), pltpu.SemaphoreType.DMA((n * N_PEERS,)),
                     pltpu.SemaphoreType.DMA((n,))]


def gather_ride(shards, kinds):
    n = len(shards)

    def setup(ins, outs, sems):
        send_sems, recv_sems, local_sems = sems
        x, y, c = _place()
        chips = [(1 - x, y), (x, 1 - y), (1 - x, 1 - y)]

        def copy(a, k, block, to, src=None):
            slot = _chunk(outs[a], kinds[a], _dev_index(block))
            return pltpu.make_async_remote_copy(
                src_ref=slot if src is None else src, dst_ref=slot,
                send_sem=send_sems.at[a * N_PEERS + k], recv_sem=recv_sems.at[a * N_PEERS + k],
                device_id=to, device_id_type=MESH)

        def mine(a):
            return pltpu.make_async_copy(ins[a], _chunk(outs[a], kinds[a], _dev_index((x, y, c))), local_sems.at[a])

        def first(a):
            return [copy(a, 0, (x, y, c), (x, y, 1 - c), src=ins[a])] + [
                copy(a, 1 + j, (x, y, c), (*chip, c), src=ins[a]) for j, chip in enumerate(chips)]

        return (x, y, c), chips, copy, mine, first

    def start(ins, outs, sems):
        _, _, _, mine, first = setup(ins, outs, sems)
        for a in range(n):
            mine(a).start()
            for cp in first(a):
                cp.start()

    def mid(ins, outs, sems):
        (x, y, c), chips, copy, _, _ = setup(ins, outs, sems)
        for a in range(n):
            for j, chip in enumerate(chips):
                copy(a, 1 + j, (*chip, c), (x, y, c)).wait_recv()
                copy(a, 4 + j, (*chip, c), (x, y, 1 - c)).start()

    def finish(ins, outs, sems):
        (x, y, c), chips, copy, mine, first = setup(ins, outs, sems)
        me, sibling = (x, y, c), (x, y, 1 - c)
        for a in range(n):
            copy(a, 0, sibling, me).wait_recv()
            for j, chip in enumerate(chips):
                copy(a, 4 + j, (*chip, 1 - c), me).wait_recv()
        for a in range(n):
            for cp in first(a):
                cp.wait_send()
            for j, chip in enumerate(chips):
                copy(a, 4 + j, (*chip, c), sibling).wait_send()
        for a in range(n):
            mine(a).wait()

    shapes = [(N_DEV, *s.shape) if kind == "lead" else (s.shape[0], N_DEV * s.shape[1]) for s, kind in zip(shards, kinds)]
    return Ride(shards, [jax.ShapeDtypeStruct(shape, s.dtype) for shape, s in zip(shapes, shards)], start, finish, mid)


def _chunk(ref, kind, j):
    if kind == "lead":
        return ref.at[j]
    if kind == "rows":
        r = ref.shape[0] // N_DEV
        return ref.at[pl.ds(j * r, r)]
    if kind == "mid":
        r = ref.shape[1] // N_DEV
        return ref.at[:, pl.ds(j * r, r), :]
    if kind == "cols":
        c = ref.shape[1] // N_DEV
        return ref.at[:, pl.ds(j * c, c)]
    return ref


def _chunk_shape(shape, kind):
    if kind == "lead":
        return tuple(shape[1:])
    if kind == "rows":
        return (shape[0] // N_DEV, *shape[1:])
    if kind == "mid":
        return (shape[0], shape[1] // N_DEV, shape[2])
    if kind == "cols":
        return (shape[0], shape[1] // N_DEV)
    return tuple(shape)


def scatter_ride(parts):
    n = len(parts)
    kinds = [k for _, k in parts]

    def setup(ins, outs, sems):
        send_sems, recv_sems, local_sems = sems
        x, y, c = _place()
        me = _dev_index((x, y, c))
        peers = []
        for k in range(1, N_DEV):
            kx, ky, kc = (k >> 2) & 1, (k >> 1) & 1, k & 1
            peers.append((1 - x if kx else x, 1 - y if ky else y, 1 - c if kc else c))

        def copy(a, k, peer):
            return pltpu.make_async_remote_copy(
                src_ref=_chunk(ins[a], kinds[a], _dev_index(peer)), dst_ref=outs[a].at[me],
                send_sem=send_sems.at[a * N_PEERS + k], recv_sem=recv_sems.at[a * N_PEERS + k],
                device_id=peer, device_id_type=MESH)

        def arrival(a, k, peer):
            slot = outs[a].at[_dev_index(peer)]
            return pltpu.make_async_remote_copy(
                src_ref=slot, dst_ref=slot,
                send_sem=send_sems.at[a * N_PEERS + k], recv_sem=recv_sems.at[a * N_PEERS + k],
                device_id=peer, device_id_type=MESH)

        def mine(a):
            return pltpu.make_async_copy(_chunk(ins[a], kinds[a], me), outs[a].at[me], local_sems.at[a])

        return peers, copy, arrival, mine

    def start(ins, outs, sems):
        peers, copy, _, mine = setup(ins, outs, sems)
        for a in range(n):
            mine(a).start()
            for k, peer in enumerate(peers):
                copy(a, k, peer).start()

    def finish(ins, outs, sems):
        peers, copy, arrival, mine = setup(ins, outs, sems)
        for a in range(n):
            for k, peer in enumerate(peers):
                arrival(a, k, peer).wait_recv()
        for a in range(n):
            for k, peer in enumerate(peers):
                copy(a, k, peer).wait_send()
            mine(a).wait()

    shapes = [jax.ShapeDtypeStruct((N_DEV, *_chunk_shape(arr.shape, kind)), arr.dtype) for arr, kind in parts]
    return Ride([arr for arr, _ in parts], shapes, start, finish)


def run_ride(ride, name):
    n_in, n_out = len(ride.ins), len(ride.out_shapes)

    def body(*refs):
        ins, outs, sems = refs[:n_in], refs[n_in:n_in + n_out], refs[n_in + n_out:]
        ride.start(ins, outs, sems)
        if ride.mid is not None:
            ride.mid(ins, outs, sems)
        ride.finish(ins, outs, sems)

    return pl.pallas_call(
        body, name=name, in_specs=[ANY] * n_in, out_specs=[ANY] * n_out, out_shape=ride.out_shapes,
        scratch_shapes=ride.sems,
    )(*ride.ins)


def _call(body, *, name, grid, in_specs, out_specs, out_shape, args, scratch_shapes=(), ride=None):
    params = pltpu.CompilerParams(dimension_semantics=("arbitrary",) * len(grid), vmem_limit_bytes=VMEM_LIMIT)
    if ride is None:
        outs = pl.pallas_call(body, name=name, grid=grid, in_specs=in_specs, out_specs=out_specs, out_shape=out_shape,
                              scratch_shapes=list(scratch_shapes), compiler_params=params)(*args)
        return outs, []
    n_in, n_out, n_scr = len(in_specs), len(out_specs), len(scratch_shapes)
    r_in, r_out = len(ride.ins), len(ride.out_shapes)

    def hosted(*refs):
        ins, refs = refs[:n_in], refs[n_in:]
        rins, refs = refs[:r_in], refs[r_in:]
        outs, refs = refs[:n_out], refs[n_out:]
        routs, refs = refs[:r_out], refs[r_out:]
        scratch, sems = refs[:n_scr], refs[n_scr:]
        step, n_steps = pl.program_id(0), grid[0]
        for d in range(1, len(grid)):
            step, n_steps = step * grid[d] + pl.program_id(d), n_steps * grid[d]

        @pl.when(step == 0)
        def _():
            ride.start(rins, routs, sems)

        if ride.mid is not None:
            @pl.when(step == (3 * n_steps) // 4)
            def _():
                ride.mid(rins, routs, sems)

        body(*ins, *outs, *scratch)

        @pl.when(step == n_steps - 1)
        def _():
            ride.finish(rins, routs, sems)

    outs = pl.pallas_call(
        hosted, name=name + "_ride", grid=grid,
        in_specs=list(in_specs) + [ANY] * r_in, out_specs=list(out_specs) + [ANY] * r_out,
        out_shape=list(out_shape) + ride.out_shapes,
        scratch_shapes=list(scratch_shapes) + ride.sems, compiler_params=params,
    )(*args, *ride.ins)
    return outs[:n_out], outs[n_out:]


def ffn_fwd(h, gain, wgu, wd, ride=None):
    t = h.shape[0]

    def body(h_ref, g_ref, wgu_ref, wd_ref, hn_ref, z_ref, u_ref, acc):
        k = pl.program_id(1)

        @pl.when(k == 0)
        def _():
            u_ref[...] = _rms_fwd(h_ref[...], g_ref[...]).astype(BF16)
            acc[...] = jnp.zeros_like(acc)

        u = u_ref[...]
        g = _dot_nt(u, wgu_ref[0, 0])
        up = _dot_nt(u, wgu_ref[1, 0])
        z_ref[0, 0] = g.astype(BF16)
        z_ref[1, 0] = up.astype(BF16)
        a = g * jax.nn.sigmoid(g) * up
        acc[...] += _dot(a.astype(BF16), wd_ref[0])

        @pl.when(k == 3)
        def _():
            hn_ref[...] = h_ref[...] + acc[...]

    return _call(
        body, name="ffn_fwd", grid=(t // TM, 4), ride=ride,
        in_specs=[pl.BlockSpec((TM, D), lambda i, k: (i, 0)),
                  pl.BlockSpec((1, D), lambda i, k: (0, 0)),
                  pl.BlockSpec((2, 1, FC, D), lambda i, k: (0, k, 0, 0)),
                  pl.BlockSpec((1, FC, D), lambda i, k: (k, 0, 0))],
        out_specs=[pl.BlockSpec((TM, D), lambda i, k: (i, 0)),
                   pl.BlockSpec((2, 1, TM, FC), lambda i, k: (0, k, i, 0)),
                   pl.BlockSpec((TM, D), lambda i, k: (i, 0))],
        out_shape=[jax.ShapeDtypeStruct((t, D), F32),
                   jax.ShapeDtypeStruct((2, 4, t, FC), BF16),
                   jax.ShapeDtypeStruct((t, D), BF16)],
        scratch_shapes=[pltpu.VMEM((TM, D), F32)],
        args=(h, gain, wgu, wd))


def ffn_bwd_x(dh, h, gain, z, wgu, wd, ride=None):
    t = h.shape[0]

    def body(dh_ref, h_ref, g_ref, z_ref, wgu_ref, wd_ref, dhp_ref, dz_ref, a_ref, dgain_ref, dhb, du):
        i, k = pl.program_id(0), pl.program_id(1)

        @pl.when(k == 0)
        def _():
            dhb[...] = dh_ref[...].astype(BF16)
            du[...] = jnp.zeros_like(du)

        @pl.when((k == 0) & (i == 0))
        def _():
            dgain_ref[...] = jnp.zeros_like(dgain_ref)

        da = _dot_nt(dhb[...], wd_ref[0])
        g = z_ref[0, 0].astype(F32)
        up = z_ref[1, 0].astype(F32)
        sg = jax.nn.sigmoid(g)
        silu = g * sg
        a_ref[0] = (silu * up).astype(BF16)
        dg = (da * up * (sg * (1.0 + g * (1.0 - sg)))).astype(BF16)
        dup = (da * silu).astype(BF16)
        dz_ref[0, 0] = dg
        dz_ref[1, 0] = dup
        for n in range(2):
            cols = slice(n * (D // 2), (n + 1) * (D // 2))
            du[:, cols] += _dot(dg, wgu_ref[0, 0, :, cols]) + _dot(dup, wgu_ref[1, 0, :, cols])

        @pl.when(k == 3)
        def _():
            dx, dgain = _rms_bwd(h_ref[...], g_ref[...], du[...])
            dhp_ref[...] = dh_ref[...] + dx
            dgain_ref[...] += dgain

    return _call(
        body, name="ffn_bwd_x", grid=(t // TM, 4), ride=ride,
        in_specs=[pl.BlockSpec((TM, D), lambda i, k: (i, 0)),
                  pl.BlockSpec((TM, D), lambda i, k: (i, 0)),
                  pl.BlockSpec((1, D), lambda i, k: (0, 0)),
                  pl.BlockSpec((2, 1, TM, FC), lambda i, k: (0, k, i, 0)),
                  pl.BlockSpec((2, 1, FC, D), lambda i, k: (0, k, 0, 0)),
                  pl.BlockSpec((1, FC, D), lambda i, k: (k, 0, 0))],
        out_specs=[pl.BlockSpec((TM, D), lambda i, k: (i, 0)),
                   pl.BlockSpec((2, 1, TM, FC), lambda i, k: (0, k, i, 0)),
                   pl.BlockSpec((1, TM, FC), lambda i, k: (k, i, 0)),
                   pl.BlockSpec((1, D), lambda i, k: (0, 0)),
                   pl.BlockSpec((TM, D), lambda i, k: (i, 0))],
        out_shape=[jax.ShapeDtypeStruct((t, D), F32),
                   jax.ShapeDtypeStruct((2, 4, t, FC), BF16),
                   jax.ShapeDtypeStruct((4, t, FC), BF16),
                   jax.ShapeDtypeStruct((1, D), F32),
                   jax.ShapeDtypeStruct((t, D), BF16)],
        scratch_shapes=[pltpu.VMEM((TM, D), F32)],
        args=(dh, h, gain, z, wgu, wd))


TW = 2048


def ffn_bwd_w(u, dz, a, dhb, ride=None):
    t = u.shape[0]
    tw = min(TW, t)
    steps = t // tw

    def body(u_ref, dz_ref, a_ref, dh_ref, dwgu_ref, dwd_ref, acc_gu, acc_d):
        j = pl.program_id(1)

        @pl.when(j == 0)
        def _():
            acc_gu[...] = jnp.zeros_like(acc_gu)
            acc_d[...] = jnp.zeros_like(acc_d)

        ub = u_ref[...]
        acc_gu[0] += _dot_tn(dz_ref[0, 0], ub)
        acc_gu[1] += _dot_tn(dz_ref[1, 0], ub)
        acc_d[...] += _dot_tn(a_ref[0], dh_ref[...])

        @pl.when(j == steps - 1)
        def _():
            dwgu_ref[:, 0] = acc_gu[...].astype(BF16)
            dwd_ref[0] = acc_d[...].astype(BF16)

    return _call(
        body, name="ffn_bwd_w", grid=(4, steps), ride=ride,
        in_specs=[pl.BlockSpec((tw, D), lambda k, j: (j, 0)),
                  pl.BlockSpec((2, 1, tw, FC), lambda k, j: (0, k, j, 0)),
                  pl.BlockSpec((1, tw, FC), lambda k, j: (k, j, 0)),
                  pl.BlockSpec((tw, D), lambda k, j: (j, 0))],
        out_specs=[pl.BlockSpec((2, 1, FC, D), lambda k, j: (0, k, 0, 0)),
                   pl.BlockSpec((1, FC, D), lambda k, j: (k, 0, 0))],
        out_shape=[jax.ShapeDtypeStruct((2, 4, FC, D), BF16),
                   jax.ShapeDtypeStruct((4, FC, D), BF16)],
        scratch_shapes=[pltpu.VMEM((2, FC, D), F32), pltpu.VMEM((FC, D), F32)],
        args=(u, dz, a, dhb))


def _prev_halo(i, tile=TM):
    return jnp.maximum(i * (tile // HALO) - 1, 0)


def _next_halo(i, t, tile=TM):
    return jnp.minimum((i + 1) * (tile // HALO), t // HALO - 1)


def rms_matmul(h, gain, w, bias, ride=None):
    t = h.shape[0]
    n = w.shape[1]

    def body(h_ref, g_ref, w_ref, b_ref, z_ref, u_ref):
        u = _rms_fwd(h_ref[...], g_ref[...]).astype(BF16)
        u_ref[...] = u
        z_ref[...] = (_dot(u, w_ref[...]) + b_ref[...]).astype(BF16)

    return _call(
        body, name=f"rms_matmul_{n}", grid=(t // TM,), ride=ride,
        in_specs=[pl.BlockSpec((TM, D), lambda i: (i, 0)),
                  pl.BlockSpec((1, D), lambda i: (0, 0)),
                  pl.BlockSpec((D, n), lambda i: (0, 0)),
                  pl.BlockSpec((1, n), lambda i: (0, 0))],
        out_specs=[pl.BlockSpec((TM, n), lambda i: (i, 0)),
                   pl.BlockSpec((TM, D), lambda i: (i, 0))],
        out_shape=[jax.ShapeDtypeStruct((t, n), BF16),
                   jax.ShapeDtypeStruct((t, D), BF16)],
        args=(h, gain, w, bias))


def in_proj_bwd_x(dz, w, h, gain, dh, ride=None):
    t = h.shape[0]
    n = w.shape[1]

    def body(dz_ref, w_ref, h_ref, g_ref, dh_ref, dhp_ref, dgain_ref, dbias_ref):
        @pl.when(pl.program_id(0) == 0)
        def _():
            dgain_ref[...] = jnp.zeros_like(dgain_ref)
            dbias_ref[...] = jnp.zeros_like(dbias_ref)

        du = _dot_nt(dz_ref[...], w_ref[...])
        dx, dgain = _rms_bwd(h_ref[...], g_ref[...], du)
        dhp_ref[...] = dh_ref[...] + dx
        dgain_ref[...] += dgain
        dbias_ref[...] += jnp.sum(dz_ref[...].astype(F32), axis=0, keepdims=True)

    return _call(
        body, name=f"in_proj_bwd_x_{n}", grid=(t // TM,), ride=ride,
        in_specs=[pl.BlockSpec((TM, n), lambda i: (i, 0)),
                  pl.BlockSpec((D, n), lambda i: (0, 0)),
                  pl.BlockSpec((TM, D), lambda i: (i, 0)),
                  pl.BlockSpec((1, D), lambda i: (0, 0)),
                  pl.BlockSpec((TM, D), lambda i: (i, 0))],
        out_specs=[pl.BlockSpec((TM, D), lambda i: (i, 0)),
                   pl.BlockSpec((1, D), lambda i: (0, 0)),
                   pl.BlockSpec((1, n), lambda i: (0, 0))],
        out_shape=[jax.ShapeDtypeStruct((t, D), F32),
                   jax.ShapeDtypeStruct((1, D), F32),
                   jax.ShapeDtypeStruct((1, n), F32)],
        args=(dz, w, h, gain, dh))


def in_proj_bwd_w(u, dz, ride=None):
    t = u.shape[0]
    n = dz.shape[1]
    steps = t // TM

    def body(u_ref, dz_ref, dw_ref, acc):
        s = pl.program_id(0)

        @pl.when(s == 0)
        def _():
            acc[...] = jnp.zeros_like(acc)

        acc[...] += _dot_tn(u_ref[...], dz_ref[...])

        @pl.when(s == steps - 1)
        def _():
            dw_ref[...] = acc[...].astype(BF16)

    return _call(
        body, name=f"in_proj_bwd_w_{n}", grid=(steps,), ride=ride,
        in_specs=[pl.BlockSpec((TM, D), lambda s: (s, 0)),
                  pl.BlockSpec((TM, n), lambda s: (s, 0))],
        out_specs=[pl.BlockSpec((D, n), lambda s: (0, 0))],
        out_shape=[jax.ShapeDtypeStruct((D, n), BF16)],
        scratch_shapes=[pltpu.VMEM((D, n), F32)],
        args=(u, dz))


A_TAPS = 3


def a_mix_fwd(z, h, conv, wout, ride=None):
    t = h.shape[0]

    def body(z_ref, zp_ref, h_ref, cw_ref, wo_ref, hn_ref, pad):
        i = pl.program_id(0)
        ph = zp_ref[:, D:2 * D].astype(F32) * zp_ref[:, 2 * D:].astype(F32)
        pad[0:HALO, :] = jnp.where(i == 0, 0.0, ph)
        pad[HALO:, :] = z_ref[:, D:2 * D].astype(F32) * z_ref[:, 2 * D:].astype(F32)
        q = jnp.zeros((TM, D), F32)
        for k in range(A_TAPS):
            off = HALO - (A_TAPS - 1) + k
            q += cw_ref[k:k + 1, :] * pad[off:off + TM, :]
        r = z_ref[:, 0:D].astype(F32) * q
        hn_ref[...] = h_ref[...] + _dot(r.astype(BF16), wo_ref[...])

    return _call(
        body, name="a_mix_fwd", grid=(t // TM,), ride=ride,
        in_specs=[pl.BlockSpec((TM, 3 * D), lambda i: (i, 0)),
                  pl.BlockSpec((HALO, 3 * D), lambda i: (_prev_halo(i), 0)),
                  pl.BlockSpec((TM, D), lambda i: (i, 0)),
                  pl.BlockSpec((A_TAPS, D), lambda i: (0, 0)),
                  pl.BlockSpec((D, D), lambda i: (0, 0))],
        out_specs=[pl.BlockSpec((TM, D), lambda i: (i, 0))],
        out_shape=[jax.ShapeDtypeStruct((t, D), F32)],
        scratch_shapes=[pltpu.VMEM((HALO + TM, D), F32)],
        args=(z, z, h, conv, wout))


def a_mix_bwd(dh, z, conv, wout, ride=None):
    t = dh.shape[0]
    steps = t // TM

    def body(dh_ref, dhn_ref, z_ref, zp_ref, zn_ref, cw_ref, wo_ref, dz_ref, dwo_ref, dcw_ref, pad, dqpad, dwo):
        i = pl.program_id(0)
        last = i == steps - 1

        @pl.when(i == 0)
        def _():
            dwo[...] = jnp.zeros_like(dwo)
            dcw_ref[...] = jnp.zeros_like(dcw_ref)

        ph = zp_ref[:, D:2 * D].astype(F32) * zp_ref[:, 2 * D:].astype(F32)
        pad[0:HALO, :] = jnp.where(i == 0, 0.0, ph)
        c = z_ref[:, D:2 * D].astype(F32)
        v = z_ref[:, 2 * D:].astype(F32)
        pad[HALO:, :] = c * v
        q = jnp.zeros((TM, D), F32)
        for k in range(A_TAPS):
            off = HALO - (A_TAPS - 1) + k
            q += cw_ref[k:k + 1, :] * pad[off:off + TM, :]
        b = z_ref[:, 0:D].astype(F32)
        dhb = dh_ref[...].astype(BF16)
        dwo[...] += _dot_tn((b * q).astype(BF16), dhb)
        dr = _dot_nt(dhb, wo_ref[...])
        dz_ref[:, 0:D] = (dr * q).astype(BF16)
        dq = dr * b
        drn = _dot_nt(dhn_ref[...].astype(BF16), wo_ref[...])
        dqpad[0:TM, :] = dq
        dqpad[TM:, :] = jnp.where(last, 0.0, drn * zn_ref[:, 0:D].astype(F32))
        dp = jnp.zeros((TM, D), F32)
        for k in range(A_TAPS):
            off = A_TAPS - 1 - k
            dp += cw_ref[k:k + 1, :] * dqpad[off:off + TM, :]
            poff = HALO - (A_TAPS - 1) + k
            dcw_ref[k:k + 1, :] += jnp.sum(dq * pad[poff:poff + TM, :], axis=0, keepdims=True)
        dz_ref[:, D:2 * D] = (dp * v).astype(BF16)
        dz_ref[:, 2 * D:] = (dp * c).astype(BF16)

        @pl.when(last)
        def _():
            dwo_ref[...] = dwo[...].astype(BF16)

    return _call(
        body, name="a_mix_bwd", grid=(steps,), ride=ride,
        in_specs=[pl.BlockSpec((TM, D), lambda i: (i, 0)),
                  pl.BlockSpec((HALO, D), lambda i: (_next_halo(i, t), 0)),
                  pl.BlockSpec((TM, 3 * D), lambda i: (i, 0)),
                  pl.BlockSpec((HALO, 3 * D), lambda i: (_prev_halo(i), 0)),
                  pl.BlockSpec((HALO, 3 * D), lambda i: (_next_halo(i, t), 0)),
                  pl.BlockSpec((A_TAPS, D), lambda i: (0, 0)),
                  pl.BlockSpec((D, D), lambda i: (0, 0))],
        out_specs=[pl.BlockSpec((TM, 3 * D), lambda i: (i, 0)),
                   pl.BlockSpec((D, D), lambda i: (0, 0)),
                   pl.BlockSpec((A_TAPS, D), lambda i: (0, 0))],
        out_shape=[jax.ShapeDtypeStruct((t, 3 * D), BF16),
                   jax.ShapeDtypeStruct((D, D), BF16),
                   jax.ShapeDtypeStruct((A_TAPS, D), F32)],
        scratch_shapes=[pltpu.VMEM((HALO + TM, D), F32), pltpu.VMEM((TM + HALO, D), F32), pltpu.VMEM((D, D), F32)],
        args=(dh, dh, z, z, z, conv, wout))


C_TAPS = 31


def _glu(zr):
    return zr[:, 0:D].astype(F32) * jax.nn.sigmoid(zr[:, D:].astype(F32))


def _ln_silu(h2, lg, lb):
    mu = jnp.mean(h2, axis=-1, keepdims=True)
    xc = h2 - mu
    rstd = lax.rsqrt(jnp.mean(xc * xc, axis=-1, keepdims=True) + LN_EPS)
    xn = xc * rstd
    h3 = xn * lg + lb
    s3 = jax.nn.sigmoid(h3)
    return xn, rstd, h3, s3


def _ln_silu_bwd(h2, lg, lb, dh4):
    xn, rstd, h3, s3 = _ln_silu(h2, lg, lb)
    dh3 = dh4 * (s3 * (1.0 + h3 * (1.0 - s3)))
    dxn = dh3 * lg
    dh2 = rstd * (dxn - jnp.mean(dxn, axis=-1, keepdims=True) - xn * jnp.mean(dxn * xn, axis=-1, keepdims=True))
    return dh2, dh3, xn, h3 * s3


TC = 256
RB = 64
LANES = 128
SHIFTS = 7


def _shifted_copies(src, sh, rows):
    for b in range(1, SHIFTS + 1):
        sh[b - 1, 0:rows, :] = src[b:b + rows, :]


def _window(src, sh, o, r0, lanes):
    a, b = divmod(o, 8)
    ref = src if b == 0 else sh.at[b - 1]
    return ref[8 * a + r0:8 * a + r0 + RB, lanes]


def c_mix_fwd(z, h, dw, bdw, lg, lb, w2, b2, ride=None):
    t = h.shape[0]

    def body(z_ref, zp_ref, h_ref, dw_ref, bdw_ref, lg_ref, lb_ref, w2_ref, b2_ref, hn_ref, h2_ref, pad, sh):
        i = pl.program_id(0)
        pad[0:HALO, :] = jnp.where(i == 0, 0.0, _glu(zp_ref))
        pad[HALO:, :] = _glu(z_ref)
        _shifted_copies(pad, sh, TC + 24)
        for l in range(D // LANES):
            lanes = slice(l * LANES, (l + 1) * LANES)
            for r0 in range(0, TC, RB):
                acc = jnp.zeros((RB, LANES), F32) + bdw_ref[:, lanes]
                for k in range(C_TAPS):
                    acc += dw_ref[k:k + 1, lanes] * _window(pad, sh, HALO - (C_TAPS - 1) + k, r0, lanes)
                h2_ref[r0:r0 + RB, lanes] = acc
        _, _, h3, s3 = _ln_silu(h2_ref[...], lg_ref[...], lb_ref[...])
        hn_ref[...] = h_ref[...] + _dot((h3 * s3).astype(BF16), w2_ref[...]) + b2_ref[...]

    vec = pl.BlockSpec((1, D), lambda i: (0, 0))
    return _call(
        body, name="c_mix_fwd", grid=(t // TC,), ride=ride,
        in_specs=[pl.BlockSpec((TC, 2 * D), lambda i: (i, 0)),
                  pl.BlockSpec((HALO, 2 * D), lambda i: (_prev_halo(i, TC), 0)),
                  pl.BlockSpec((TC, D), lambda i: (i, 0)),
                  pl.BlockSpec((C_TAPS, D), lambda i: (0, 0)),
                  vec, vec, vec,
                  pl.BlockSpec((D, D), lambda i: (0, 0)),
                  vec],
        out_specs=[pl.BlockSpec((TC, D), lambda i: (i, 0)),
                   pl.BlockSpec((TC, D), lambda i: (i, 0))],
        out_shape=[jax.ShapeDtypeStruct((t, D), F32),
                   jax.ShapeDtypeStruct((t, D), F32)],
        scratch_shapes=[pltpu.VMEM((HALO + TC, D), F32), pltpu.VMEM((SHIFTS, TC + 24, D), F32)],
        args=(z, z, h, dw, bdw, lg, lb, w2, b2))


def c_mix_bwd(dh, z, h2, dw, lg, lb, w2, ride=None):
    t = dh.shape[0]
    steps = t // TC

    def body(dh_ref, dhn_ref, z_ref, zp_ref, h2_ref, h2n_ref, dw_ref, lg_ref, lb_ref, w2_ref,
             dz_ref, dw2_ref, db2_ref, dlg_ref, dlb_ref, dbdw_ref, ddw_ref, pad, dpad, dw2, sh, dh1):
        i = pl.program_id(0)
        last = i == steps - 1

        @pl.when(i == 0)
        def _():
            for r in (dw2, db2_ref, dlg_ref, dlb_ref, dbdw_ref, ddw_ref):
                r[...] = jnp.zeros_like(r)

        lg, lb = lg_ref[...], lb_ref[...]
        dh = dh_ref[...]
        dhb = dh.astype(BF16)
        dh2, dh3, xn, h4 = _ln_silu_bwd(h2_ref[...], lg, lb, _dot_nt(dhb, w2_ref[...]))
        dw2[...] += _dot_tn(h4.astype(BF16), dhb)
        db2_ref[...] += jnp.sum(dh, axis=0, keepdims=True)
        dlg_ref[...] += jnp.sum(dh3 * xn, axis=0, keepdims=True)
        dlb_ref[...] += jnp.sum(dh3, axis=0, keepdims=True)
        dbdw_ref[...] += jnp.sum(dh2, axis=0, keepdims=True)
        dh2n, _, _, _ = _ln_silu_bwd(h2n_ref[...], lg, lb, _dot_nt(dhn_ref[...].astype(BF16), w2_ref[...]))
        dpad[0:TC, :] = dh2
        dpad[TC:, :] = jnp.where(last, 0.0, dh2n)
        _shifted_copies(dpad, sh, TC + 24)
        for l in range(D // LANES):
            lanes = slice(l * LANES, (l + 1) * LANES)
            for r0 in range(0, TC, RB):
                acc = jnp.zeros((RB, LANES), F32)
                for k in range(C_TAPS):
                    acc += dw_ref[k:k + 1, lanes] * _window(dpad, sh, C_TAPS - 1 - k, r0, lanes)
                dh1[r0:r0 + RB, lanes] = acc
        pad[0:HALO, :] = jnp.where(i == 0, 0.0, _glu(zp_ref))
        pad[HALO:, :] = _glu(z_ref)
        _shifted_copies(pad, sh, TC + 24)
        for l in range(D // LANES):
            lanes = slice(l * LANES, (l + 1) * LANES)
            accs = [jnp.zeros((8, LANES), F32) for _ in range(C_TAPS)]
            for r0 in range(0, TC, RB):
                d = dpad[r0:r0 + RB, lanes]
                for k in range(C_TAPS):
                    prod = d * _window(pad, sh, HALO - (C_TAPS - 1) + k, r0, lanes)
                    accs[k] += jnp.sum(prod.reshape(RB // 8, 8, LANES), axis=0)
            for k in range(C_TAPS):
                ddw_ref[k:k + 1, lanes] += jnp.sum(accs[k], axis=0, keepdims=True)
        a = z_ref[:, 0:D].astype(F32)
        sg = jax.nn.sigmoid(z_ref[:, D:].astype(F32))
        d1 = dh1[...]
        dz_ref[:, 0:D] = (d1 * sg).astype(BF16)
        dz_ref[:, D:] = (d1 * a * sg * (1.0 - sg)).astype(BF16)

        @pl.when(last)
        def _():
            dw2_ref[...] = dw2[...].astype(BF16)

    vec = pl.BlockSpec((1, D), lambda i: (0, 0))
    return _call(
        body, name="c_mix_bwd", grid=(steps,), ride=ride,
        in_specs=[pl.BlockSpec((TC, D), lambda i: (i, 0)),
                  pl.BlockSpec((HALO, D), lambda i: (_next_halo(i, t, TC), 0)),
                  pl.BlockSpec((TC, 2 * D), lambda i: (i, 0)),
                  pl.BlockSpec((HALO, 2 * D), lambda i: (_prev_halo(i, TC), 0)),
                  pl.BlockSpec((TC, D), lambda i: (i, 0)),
                  pl.BlockSpec((HALO, D), lambda i: (_next_halo(i, t, TC), 0)),
                  pl.BlockSpec((C_TAPS, D), lambda i: (0, 0)),
                  vec, vec,
                  pl.BlockSpec((D, D), lambda i: (0, 0))],
        out_specs=[pl.BlockSpec((TC, 2 * D), lambda i: (i, 0)),
                   pl.BlockSpec((D, D), lambda i: (0, 0)),
                   vec, vec, vec, vec,
                   pl.BlockSpec((C_TAPS, D), lambda i: (0, 0))],
        out_shape=[jax.ShapeDtypeStruct((t, 2 * D), BF16),
                   jax.ShapeDtypeStruct((D, D), BF16)]
                  + [jax.ShapeDtypeStruct((1, D), F32)] * 4
                  + [jax.ShapeDtypeStruct((C_TAPS, D), F32)],
        scratch_shapes=[pltpu.VMEM((HALO + TC, D), F32), pltpu.VMEM((TC + HALO, D), F32), pltpu.VMEM((D, D), F32),
                        pltpu.VMEM((SHIFTS, TC + 24, D), F32), pltpu.VMEM((TC, D), F32)],
        args=(dh, dh, z, z, h2, h2, dw, lg, lb, w2))


POOL_WINDOWS = (2, 4, 8, 16)
GW = D // len(POOL_WINDOWS)


def _pool_mixed(pad, g, w, inv_cnt):
    cols = slice(g * GW, (g + 1) * GW)
    s = pad[HALO:HALO + TM, cols]
    u = s
    for j in range(1, w):
        s = s + pad[HALO - j:HALO - j + TM, cols]
    return s * inv_cnt - u


def _inv_cnt(i, w):
    row = i * TM + lax.broadcasted_iota(jnp.int32, (TM, 1), 0)
    return 1.0 / jnp.minimum(row + 1, w).astype(F32)


def b_mix_fwd(h, gain, wg, scale, ride=None):
    t = h.shape[0]

    def body(h_ref, hp_ref, g_ref, wg_ref, sc_ref, hn_ref, pad):
        i = pl.program_id(0)
        gain = g_ref[...]
        pad[0:HALO, :] = jnp.where(i == 0, 0.0, _rms_fwd(hp_ref[...], gain))
        pad[HALO:, :] = _rms_fwd(h_ref[...], gain)
        for g, w in enumerate(POOL_WINDOWS):
            cols = slice(g * GW, (g + 1) * GW)
            mixed = _pool_mixed(pad, g, w, _inv_cnt(i, w))
            y = _dot(mixed.astype(BF16), wg_ref[g])
            hn_ref[:, cols] = h_ref[:, cols] + y * sc_ref[:, cols]

    return _call(
        body, name="b_mix_fwd", grid=(t // TM,), ride=ride,
        in_specs=[pl.BlockSpec((TM, D), lambda i: (i, 0)),
                  pl.BlockSpec((HALO, D), lambda i: (_prev_halo(i), 0)),
                  pl.BlockSpec((1, D), lambda i: (0, 0)),
                  pl.BlockSpec((4, GW, GW), lambda i: (0, 0, 0)),
                  pl.BlockSpec((1, D), lambda i: (0, 0))],
        out_specs=[pl.BlockSpec((TM, D), lambda i: (i, 0))],
        out_shape=[jax.ShapeDtypeStruct((t, D), F32)],
        scratch_shapes=[pltpu.VMEM((HALO + TM, D), F32)],
        args=(h, h, gain, wg, scale))


def b_mix_bwd(dh, h, gain, wg, scale, ride=None):
    t = h.shape[0]
    steps = t // TM

    def body(dh_ref, dhn_ref, h_ref, hp_ref, g_ref, wg_ref, sc_ref, dhp_ref, dgain_ref, dwg_ref, dsc_ref, pad, dpad, du):
        i = pl.program_id(0)
        last = i == steps - 1

        @pl.when(i == 0)
        def _():
            for r in (dgain_ref, dwg_ref, dsc_ref):
                r[...] = jnp.zeros_like(r)

        gain = g_ref[...]
        pad[0:HALO, :] = jnp.where(i == 0, 0.0, _rms_fwd(hp_ref[...], gain))
        pad[HALO:, :] = _rms_fwd(h_ref[...], gain)
        for g, w in enumerate(POOL_WINDOWS):
            cols = slice(g * GW, (g + 1) * GW)
            inv_cnt = _inv_cnt(i, w)
            mixed = _pool_mixed(pad, g, w, inv_cnt).astype(BF16)
            dh = dh_ref[:, cols]
            dsc_ref[:, cols] += jnp.sum(dh * _dot(mixed, wg_ref[g]), axis=0, keepdims=True)
            dy = (dh * sc_ref[:, cols]).astype(BF16)
            dwg_ref[g] += _dot_tn(mixed, dy)
            dm = _dot_nt(dy, wg_ref[g])
            dmn = _dot_nt((dhn_ref[:, cols] * sc_ref[:, cols]).astype(BF16), wg_ref[g])
            dpad[0:TM, cols] = dm * inv_cnt
            dpad[TM:, cols] = jnp.where(last, 0.0, dmn * (1.0 / w))
            s = dpad[0:TM, cols]
            for j in range(1, w):
                s = s + dpad[j:j + TM, cols]
            du[:, cols] = s - dm
        dx, dgain = _rms_bwd(h_ref[...], gain, du[...])
        dhp_ref[...] = dh_ref[...] + dx
        dgain_ref[...] += dgain

    return _call(
        body, name="b_mix_bwd", grid=(steps,), ride=ride,
        in_specs=[pl.BlockSpec((TM, D), lambda i: (i, 0)),
                  pl.BlockSpec((HALO, D), lambda i: (_next_halo(i, t), 0)),
                  pl.BlockSpec((TM, D), lambda i: (i, 0)),
                  pl.BlockSpec((HALO, D), lambda i: (_prev_halo(i), 0)),
                  pl.BlockSpec((1, D), lambda i: (0, 0)),
                  pl.BlockSpec((4, GW, GW), lambda i: (0, 0, 0)),
                  pl.BlockSpec((1, D), lambda i: (0, 0))],
        out_specs=[pl.BlockSpec((TM, D), lambda i: (i, 0)),
                   pl.BlockSpec((1, D), lambda i: (0, 0)),
                   pl.BlockSpec((4, GW, GW), lambda i: (0, 0, 0)),
                   pl.BlockSpec((1, D), lambda i: (0, 0))],
        out_shape=[jax.ShapeDtypeStruct((t, D), F32),
                   jax.ShapeDtypeStruct((1, D), F32),
                   jax.ShapeDtypeStruct((4, GW, GW), F32),
                   jax.ShapeDtypeStruct((1, D), F32)],
        scratch_shapes=[pltpu.VMEM((HALO + TM, D), F32), pltpu.VMEM((TM + HALO, D), F32), pltpu.VMEM((TM, D), F32)],
        args=(dh, dh, h, h, gain, wg, scale))


LOSS_LANES = 128


def loss_head(h, gain, target):
    t = h.shape[0]

    def body(h_ref, g_ref, tg_ref, loss_ref, dh_ref, dgain_ref):
        @pl.when(pl.program_id(0) == 0)
        def _():
            loss_ref[...] = jnp.zeros_like(loss_ref)
            dgain_ref[...] = jnp.zeros_like(dgain_ref)

        x, gain = h_ref[...], g_ref[...]
        err = _rms_fwd(x, gain) - tg_ref[...]
        per_row = jnp.mean(err * err, axis=-1, keepdims=True)
        loss_ref[...] += jnp.broadcast_to(0.5 * jnp.sum(per_row, axis=0, keepdims=True), (1, LOSS_LANES))
        dx, dgain = _rms_bwd(x, gain, err * (1.0 / D))
        dh_ref[...] = dx
        dgain_ref[...] += dgain

    outs, _ = _call(
        body, name="loss_head", grid=(t // TM,),
        in_specs=[pl.BlockSpec((TM, D), lambda i: (i, 0)),
                  pl.BlockSpec((1, D), lambda i: (0, 0)),
                  pl.BlockSpec((TM, D), lambda i: (i, 0))],
        out_specs=[pl.BlockSpec((1, LOSS_LANES), lambda i: (0, 0)),
                   pl.BlockSpec((TM, D), lambda i: (i, 0)),
                   pl.BlockSpec((1, D), lambda i: (0, 0))],
        out_shape=[jax.ShapeDtypeStruct((1, LOSS_LANES), F32),
                   jax.ShapeDtypeStruct((t, D), F32),
                   jax.ShapeDtypeStruct((1, D), F32)],
        args=(h, gain, target))
    return outs


ADAM_LR = 0.001
ADAM_B1 = 0.9
ADAM_B2 = 0.999
ADAM_EPS = 1e-08
ADAM_WD = 0.01
ADAM_STEP = 10
ADAM_VMEM = 40 * 1024 * 1024


def cast_all(arrays):
    def body(*refs):
        for src, dst in zip(refs[:len(arrays)], refs[len(arrays):]):
            dst[...] = src[...].astype(BF16)

    return pl.pallas_call(
        body, name="cast_all", out_shape=[jax.ShapeDtypeStruct(a.shape, BF16) for a in arrays],
        compiler_params=pltpu.CompilerParams(vmem_limit_bytes=VMEM_LIMIT),
    )(*arrays)


def _adam_math(w, m, v, g):
    m = ADAM_B1 * m + (1.0 - ADAM_B1) * g
    v = ADAM_B2 * v + (1.0 - ADAM_B2) * (g * g)
    m_hat = m / (1.0 - ADAM_B1 ** ADAM_STEP)
    v_hat = v / (1.0 - ADAM_B2 ** ADAM_STEP)
    return -ADAM_LR * (m_hat / (jnp.sqrt(v_hat) + ADAM_EPS) + ADAM_WD * w), m, v


def adamw(ws, ms, vs, gps, rb):
    n = len(ws)
    r, c = ws[0].shape
    nb = r // rb

    def body(*refs):
        i = pl.program_id(0)
        for j in range(n):
            w_ref, m_ref, v_ref, gp_ref = (refs[q * n + j] for q in range(4))
            g_ref, d_ref, nm_ref, nv_ref = (refs[(4 + q) * n + j] for q in range(4))

            @pl.when(i // nb == j)
            def _():
                g = gp_ref[0].astype(F32)
                for s in range(1, N_DEV):
                    g = g + gp_ref[s].astype(F32)
                g_ref[...] = g
                d_ref[...], nm_ref[...], nv_ref[...] = _adam_math(w_ref[...], m_ref[...], v_ref[...], g)

    def blk(j):
        return pl.BlockSpec((rb, c), lambda i: (jnp.clip(i - j * nb, 0, nb - 1), 0))

    def gblk(j):
        return pl.BlockSpec((N_DEV, rb, c), lambda i: (0, jnp.clip(i - j * nb, 0, nb - 1), 0))

    outs, _ = _call(
        body, name=f"adamw_{n}x{r}x{c}", grid=(n * nb,),
        in_specs=[blk(j) for _ in range(3) for j in range(n)] + [gblk(j) for j in range(n)],
        out_specs=[blk(j) for _ in range(4) for j in range(n)],
        out_shape=[jax.ShapeDtypeStruct((r, c), F32)] * (4 * n),
        args=(*ws, *ms, *vs, *gps))
    return outs[:n], outs[n:2 * n], outs[2 * n:3 * n], outs[3 * n:]


def adamw_vectors(ws, ms, vs, gparts):
    nv = len(ws)

    def body(*refs):
        w_refs, m_refs, v_refs = refs[:nv], refs[nv:2 * nv], refs[2 * nv:3 * nv]
        gp_ref = refs[3 * nv]
        outs = refs[3 * nv + 1:]
        g_refs, d_refs, nm_refs, nv_refs = outs[:nv], outs[nv:2 * nv], outs[2 * nv:3 * nv], outs[3 * nv:]
        row = 0
        for i in range(nv):
            for part in range(w_refs[i].shape[1] // D):
                cols = slice(part * D, (part + 1) * D)
                g = gp_ref[0, row:row + 1, :]
                for s in range(1, N_DEV):
                    g = g + gp_ref[s, row:row + 1, :]
                g_refs[i][:, cols] = g
                d_refs[i][:, cols], nm_refs[i][:, cols], nv_refs[i][:, cols] = _adam_math(
                    w_refs[i][:, cols], m_refs[i][:, cols], v_refs[i][:, cols], g)
                row += 1

    shapes = [jax.ShapeDtypeStruct(w.shape, F32) for w in ws]
    outs = pl.pallas_call(body, name="adamw_vectors", out_shape=shapes * 4)(*ws, *ms, *vs, gparts)
    return outs[:nv], outs[nv:2 * nv], outs[2 * nv:3 * nv], outs[3 * nv:]


WEIGHTS = ["ln1_0", "a0_w_in", "a0_conv", "a0_w_out", "ln2_0", "ffn0_w_gu", "ffn0_w_down",
           "ln1_1", "b1_w_grp", "b1_scale", "ln2_1", "ffn1_w_gu", "ffn1_w_down",
           "ln1_2", "c2_w_pw1", "c2_b_pw1", "c2_dw", "c2_b_dw", "c2_ln_g", "c2_ln_b", "c2_w_pw2", "c2_b_pw2",
           "ln2_2", "ffn2_w_gu", "ffn2_w_down",
           "ln1_3", "a3_w_in", "a3_conv", "a3_w_out", "ln2_3", "ffn3_w_gu", "ffn3_w_down", "ln_f"]
SHARDED = {"a0_w_in": ("cols", 256), "a0_conv": ("cols", A_TAPS), "a0_w_out": ("rows", 128),
           "ffn0_w_gu": ("lead", 176), "ffn0_w_down": ("rows", 176),
           "b1_w_grp": ("mid", 128),
           "ffn1_w_gu": ("lead", 176), "ffn1_w_down": ("rows", 176),
           "c2_w_pw1": ("cols", 256), "c2_dw": ("cols", C_TAPS), "c2_w_pw2": ("rows", 128),
           "ffn2_w_gu": ("lead", 176), "ffn2_w_down": ("rows", 176),
           "a3_w_in": ("cols", 256), "a3_conv": ("cols", A_TAPS), "a3_w_out": ("rows", 128),
           "ffn3_w_gu": ("lead", 176), "ffn3_w_down": ("rows", 176)}
IN_PROJ = ("a0_w_in", "c2_w_pw1", "a3_w_in")
REPL = [n for n in WEIGHTS if n not in SHARDED]
REPL_ROWS = 16
GATHER_PLAN = {"first": ["a0_w_in", "a0_w_out", "a0_conv"],
               "in0": ["ffn0_w_gu"], "mix0": ["ffn0_w_down"],
               "ffn0": ["b1_w_grp", "ffn1_w_gu", "ffn1_w_down"],
               "ffn1": ["c2_w_pw1", "c2_w_pw2", "c2_dw", "ffn2_w_gu"],
               "in2": ["ffn2_w_down"],
               "mix2": ["a3_w_in", "a3_w_out", "a3_conv"],
               "ffn2": ["ffn3_w_gu", "ffn3_w_down"]}
SCATTER_PLAN = {"mixb3": ["ffn3_w_down"],
                "ffnx2": ["ffn3_w_gu"], "ffnw2": ["a3_w_in", "a3_w_out", "a3_conv"],
                "mixb2": ["ffn2_w_gu", "ffn2_w_down"],
                "ffnx1": ["c2_w_pw1", "c2_w_pw2", "c2_dw"],
                "ffnx0": ["ffn1_w_gu"], "ffnw0": ["ffn1_w_down", "b1_w_grp"],
                "mixb0": ["ffn0_w_down"], "inw0": ["ffn0_w_gu"], "inx0": ["a0_w_out", "a0_conv", "a0_w_in"],
                "last": ["repl"]}


def _step(p):
    vec = lambda n: p[n].reshape(1, -1)
    x, target = p["x"][0], p["loss_target"][0]

    names = list(SHARDED)
    stored = lambda n, a: a.T if n.endswith("w_gu") else a
    shard = dict(zip(names, cast_all([stored(n, p[n]) for n in names])))
    full = {}

    def gather(slot):
        names = GATHER_PLAN[slot]
        return gather_ride([shard[n] for n in names], ["cols" if n in IN_PROJ else "lead" for n in names])

    def landed(slot, outs):
        full.update(zip(GATHER_PLAN[slot], outs))

    def conv_full(n):
        k = full[n].shape[1]
        return full[n].transpose(1, 0, 2).reshape(k, D).astype(F32)

    def wgu(i):
        return full[f"ffn{i}_w_gu"].reshape(2, 4, FC, D)

    def wd(i):
        return full[f"ffn{i}_w_down"].reshape(4, FC, D)

    landed("first", run_ride(gather("first"), "gather_first"))
    no_bias = jnp.zeros((1, 3 * D), F32)
    h = [x]
    saved = {}
    conv, wout = {}, {}

    (z, u), got = rms_matmul(h[-1], vec("ln1_0"), full["a0_w_in"], no_bias, ride=gather("in0"))
    landed("in0", got)
    conv[0], wout[0] = conv_full("a0_conv"), full["a0_w_out"].reshape(D, D)
    (hm,), got = a_mix_fwd(z, h[-1], conv[0], wout[0], ride=gather("mix0"))
    landed("mix0", got)
    saved["mix0"] = (z, u)
    h.append(hm)
    (hn, zf, uf), got = ffn_fwd(hm, vec("ln2_0"), wgu(0), wd(0), ride=gather("ffn0"))
    landed("ffn0", got)
    saved["ffn0"] = (zf, uf)
    h.append(hn)

    wgrp = full["b1_w_grp"].transpose(1, 0, 2, 3).reshape(4, GW, GW)
    (hm,), _ = b_mix_fwd(h[-1], vec("ln1_1"), wgrp, vec("b1_scale"))
    h.append(hm)
    (hn, zf, uf), got = ffn_fwd(hm, vec("ln2_1"), wgu(1), wd(1), ride=gather("ffn1"))
    landed("ffn1", got)
    saved["ffn1"] = (zf, uf)
    h.append(hn)

    (z, u), got = rms_matmul(h[-1], vec("ln1_2"), full["c2_w_pw1"], vec("c2_b_pw1"), ride=gather("in2"))
    landed("in2", got)
    cdw, wpw2 = conv_full("c2_dw"), full["c2_w_pw2"].reshape(D, D)
    (hm, h2), got = c_mix_fwd(z, h[-1], cdw, vec("c2_b_dw"), vec("c2_ln_g"), vec("c2_ln_b"), wpw2, vec("c2_b_pw2"),
                              ride=gather("mix2"))
    landed("mix2", got)
    saved["mix2"] = (z, u, h2)
    h.append(hm)
    (hn, zf, uf), got = ffn_fwd(hm, vec("ln2_2"), wgu(2), wd(2), ride=gather("ffn2"))
    landed("ffn2", got)
    saved["ffn2"] = (zf, uf)
    h.append(hn)

    (z, u), _ = rms_matmul(h[-1], vec("ln1_3"), full["a3_w_in"], no_bias)
    conv[3], wout[3] = conv_full("a3_conv"), full["a3_w_out"].reshape(D, D)
    (hm,), _ = a_mix_fwd(z, h[-1], conv[3], wout[3])
    saved["mix3"] = (z, u)
    h.append(hm)
    (hn, zf, uf), _ = ffn_fwd(hm, vec("ln2_3"), wgu(3), wd(3))
    saved["ffn3"] = (zf, uf)
    h.append(hn)

    loss_lanes, dh, g_lnf = loss_head(h[-1], vec("ln_f"), target)

    g = {"ln_f": g_lnf}
    recv = {}

    def repl_rows():
        return jnp.concatenate([g[n].reshape(-1, D) for n in REPL], axis=0)

    def scatter(slot):
        parts = []
        for n in SCATTER_PLAN.get(slot, []):
            parts.append((repl_rows(), "all") if n == "repl" else (g[n], SHARDED[n][0]))
        return scatter_ride(parts) if parts else None

    def arrived(slot, outs):
        recv.update(zip(SCATTER_PLAN.get(slot, []), outs))

    for i in (3, 2, 1, 0):
        zf, uf = saved[f"ffn{i}"]
        (dh_prev, dzf, a, g[f"ln2_{i}"], dhb), got = ffn_bwd_x(dh, h[2 * i + 1], vec(f"ln2_{i}"), zf, wgu(i), wd(i),
                                                             ride=scatter(f"ffnx{i}"))
        arrived(f"ffnx{i}", got)
        dh = dh_prev
        (dwgu, dwd), got = ffn_bwd_w(uf, dzf, a, dhb, ride=scatter(f"ffnw{i}"))
        arrived(f"ffnw{i}", got)
        g[f"ffn{i}_w_gu"], g[f"ffn{i}_w_down"] = dwgu.reshape(N_DEV, FC, D), dwd.reshape(FF, D)
        hin = h[2 * i]
        if i in (0, 3):
            z, u = saved[f"mix{i}"]
            (dz, g[f"a{i}_w_out"], g[f"a{i}_conv"]), got = a_mix_bwd(dh, z, conv[i], wout[i], ride=scatter(f"mixb{i}"))
            arrived(f"mixb{i}", got)
            (g[f"a{i}_w_in"],), got = in_proj_bwd_w(u, dz, ride=scatter(f"inw{i}"))
            arrived(f"inw{i}", got)
            (dh, g[f"ln1_{i}"], _), got = in_proj_bwd_x(dz, full[f"a{i}_w_in"], hin, vec(f"ln1_{i}"), dh,
                                                       ride=scatter(f"inx{i}"))
            arrived(f"inx{i}", got)
        elif i == 1:
            (dh, g["ln1_1"], g["b1_w_grp"], g["b1_scale"]), got = b_mix_bwd(dh, hin, vec("ln1_1"), wgrp, vec("b1_scale"),
                                                                             ride=scatter("mixb1"))
            arrived("mixb1", got)
        else:
            z, u, h2 = saved["mix2"]
            (dz, g["c2_w_pw2"], g["c2_b_pw2"], g["c2_ln_g"], g["c2_ln_b"], g["c2_b_dw"], g["c2_dw"]), got = c_mix_bwd(
                dh, z, h2, cdw, vec("c2_ln_g"), vec("c2_ln_b"), wpw2, ride=scatter("mixb2"))
            arrived("mixb2", got)
            (g["c2_w_pw1"],), got = in_proj_bwd_w(u, dz, ride=scatter("inw2"))
            arrived("inw2", got)
            (dh, g["ln1_2"], g["c2_b_pw1"]), got = in_proj_bwd_x(dz, full["c2_w_pw1"], hin, vec("ln1_2"), dh,
                                                                ride=scatter("inx2"))
            arrived("inx2", got)
    grad_x = dh[None]
    arrived("last", run_ride(scatter("last"), "scatter_last"))

    grad, delta, new_m, new_v = {}, {}, {}, {}
    two_d = lambda n, a: stored(n, a.reshape(-1, p[n].shape[-1]))
    groups = {}
    for n, (_, rb) in SHARDED.items():
        groups.setdefault((two_d(n, p[n]).shape, rb), []).append(n)
    for (shape, rb), names_ in groups.items():
        per_weight = 2 * rb * shape[1] * (7 * 4 + N_DEV * recv[names_[0]].dtype.itemsize)
        at_once = max(1, (ADAM_VMEM // per_weight))
        for lo in range(0, len(names_), at_once):
            ns = names_[lo:lo + at_once]
            outs = adamw([two_d(n, p[n]) for n in ns], [two_d(n, p["m_" + n]) for n in ns],
                         [two_d(n, p["v_" + n]) for n in ns], [recv[n].reshape(N_DEV, *shape) for n in ns], rb)
            for res, o in zip((grad, delta, new_m, new_v), outs):
                res.update({n: stored(n, a).reshape(p[n].shape) for n, a in zip(ns, o)})
    outs = adamw_vectors([vec(n) for n in REPL], [vec("m_" + n) for n in REPL], [vec("v_" + n) for n in REPL], recv["repl"])
    for res, o in zip((grad, delta, new_m, new_v), outs):
        res.update({n: a.reshape(p[n].shape) for n, a in zip(REPL, o)})

    loss = lax.psum(loss_lanes[0, 0], ("x", "y", "c"))
    return (loss, grad_x, *[grad[n] for n in WEIGHTS], *[delta[n] for n in WEIGHTS],
            *[new_m[n] for n in WEIGHTS], *[new_v[n] for n in WEIGHTS])


def kernel(x, ln1_0, a0_w_in, a0_conv, a0_w_out, ln2_0, ffn0_w_gu, ffn0_w_down, ln1_1, b1_w_grp, b1_scale, ln2_1, ffn1_w_gu, ffn1_w_down, ln1_2, c2_w_pw1, c2_b_pw1, c2_dw, c2_b_dw, c2_ln_g, c2_ln_b, c2_w_pw2, c2_b_pw2, ln2_2, ffn2_w_gu, ffn2_w_down, ln1_3, a3_w_in, a3_conv, a3_w_out, ln2_3, ffn3_w_gu, ffn3_w_down, ln_f, loss_target, m_ln1_0, m_a0_w_in, m_a0_conv, m_a0_w_out, m_ln2_0, m_ffn0_w_gu, m_ffn0_w_down, m_ln1_1, m_b1_w_grp, m_b1_scale, m_ln2_1, m_ffn1_w_gu, m_ffn1_w_down, m_ln1_2, m_c2_w_pw1, m_c2_b_pw1, m_c2_dw, m_c2_b_dw, m_c2_ln_g, m_c2_ln_b, m_c2_w_pw2, m_c2_b_pw2, m_ln2_2, m_ffn2_w_gu, m_ffn2_w_down, m_ln1_3, m_a3_w_in, m_a3_conv, m_a3_w_out, m_ln2_3, m_ffn3_w_gu, m_ffn3_w_down, m_ln_f, v_ln1_0, v_a0_w_in, v_a0_conv, v_a0_w_out, v_ln2_0, v_ffn0_w_gu, v_ffn0_w_down, v_ln1_1, v_b1_w_grp, v_b1_scale, v_ln2_1, v_ffn1_w_gu, v_ffn1_w_down, v_ln1_2, v_c2_w_pw1, v_c2_b_pw1, v_c2_dw, v_c2_b_dw, v_c2_ln_g, v_c2_ln_b, v_c2_w_pw2, v_c2_b_pw2, v_ln2_2, v_ffn2_w_gu, v_ffn2_w_down, v_ln1_3, v_a3_w_in, v_a3_conv, v_a3_w_out, v_ln2_3, v_ffn3_w_gu, v_ffn3_w_down, v_ln_f):
    return _step(dict(locals()))
```

```python
import jax
import jax.numpy as jnp
from jax import lax
from jax.experimental import pallas as pl
from jax.experimental.pallas import tpu as pltpu

F32 = jnp.float32
BF16 = jnp.bfloat16

N_DEV = 8
D = 1024
FF = 2816
FC = FF // 4
RMS_EPS = 1e-6
LN_EPS = 1e-5
TM = 512
HALO = 32
VMEM_LIMIT = 60 * 1024 * 1024

NT = (((1,), (1,)), ((), ()))
TN = (((0,), (0,)), ((), ()))
MESH = pl.DeviceIdType.MESH
ANY = pl.BlockSpec(memory_space=pl.ANY)
N_PEERS = N_DEV - 1


def _dot(a, b):
    return jnp.dot(a, b, preferred_element_type=F32)


def _dot_nt(a, b):
    return lax.dot_general(a, b, NT, preferred_element_type=F32)


def _dot_tn(a, b):
    return lax.dot_general(a, b, TN, preferred_element_type=F32)


def _rms_fwd(x, gain):
    r = lax.rsqrt(jnp.mean(x * x, axis=-1, keepdims=True) + RMS_EPS)
    return x * r * gain


def _rms_bwd(x, gain, du):
    r = lax.rsqrt(jnp.mean(x * x, axis=-1, keepdims=True) + RMS_EPS)
    xhat = x * r
    dgain = jnp.sum(du * xhat, axis=0, keepdims=True)
    dxhat = du * gain
    dx = r * (dxhat - xhat * jnp.mean(dxhat * xhat, axis=-1, keepdims=True))
    return dx, dgain


def _dev_index(p):
    return 4 * p[0] + 2 * p[1] + p[2]


def _place():
    return lax.axis_index("x"), lax.axis_index("y"), lax.axis_index("c")


class Ride:
    def __init__(self, ins, out_shapes, start, finish):
        self.ins, self.out_shapes, self.start, self.finish = list(ins), list(out_shapes), start, finish
        n = len(self.ins)
        self.sems = [pltpu.SemaphoreType.DMA((n * N_PEERS,)), pltpu.SemaphoreType.DMA((n * N_PEERS,)),
                     pltpu.SemaphoreType.DMA((n,))]


def gather_ride(shards, kinds):
    n = len(shards)

    def setup(ins, outs, sems):
        send_sems, recv_sems, local_sems = sems
        x, y, c = _place()
        chips = [(1 - x, y), (x, 1 - y), (1 - x, 1 - y)]

        def copy(a, k, block, to, src=None):
            slot = _chunk(outs[a], kinds[a], _dev_index(block))
            return pltpu.make_async_remote_copy(
                src_ref=slot if src is None else src, dst_ref=slot,
                send_sem=send_sems.at[a * N_PEERS + k], recv_sem=recv_sems.at[a * N_PEERS + k],
                device_id=to, device_id_type=MESH)

        def mine(a):
            return pltpu.make_async_copy(ins[a], _chunk(outs[a], kinds[a], _dev_index((x, y, c))), local_sems.at[a])

        def first(a):
            return [copy(a, 0, (x, y, c), (x, y, 1 - c), src=ins[a])] + [
                copy(a, 1 + j, (x, y, c), (*chip, c), src=ins[a]) for j, chip in enumerate(chips)]

        return (x, y, c), chips, copy, mine, first

    def start(ins, outs, sems):
        _, _, _, mine, first = setup(ins, outs, sems)
        for a in range(n):
            mine(a).start()
            for cp in first(a):
                cp.start()

    def finish(ins, outs, sems):
        (x, y, c), chips, copy, mine, first = setup(ins, outs, sems)
        me, sibling = (x, y, c), (x, y, 1 - c)
        for a in range(n):
            for j, chip in enumerate(chips):
                copy(a, 1 + j, (*chip, c), me).wait_recv()
                copy(a, 4 + j, (*chip, c), sibling).start()
        for a in range(n):
            copy(a, 0, sibling, me).wait_recv()
            for j, chip in enumerate(chips):
                copy(a, 4 + j, (*chip, 1 - c), me).wait_recv()
        for a in range(n):
            for cp in first(a):
                cp.wait_send()
            for j, chip in enumerate(chips):
                copy(a, 4 + j, (*chip, c), sibling).wait_send()
        for a in range(n):
            mine(a).wait()

    shapes = [(N_DEV, *s.shape) if kind == "lead" else (s.shape[0], N_DEV * s.shape[1]) for s, kind in zip(shards, kinds)]
    return Ride(shards, [jax.ShapeDtypeStruct(shape, s.dtype) for shape, s in zip(shapes, shards)], start, finish)


def _chunk(ref, kind, j):
    if kind == "lead":
        return ref.at[j]
    if kind == "rows":
        r = ref.shape[0] // N_DEV
        return ref.at[pl.ds(j * r, r)]
    if kind == "mid":
        r = ref.shape[1] // N_DEV
        return ref.at[:, pl.ds(j * r, r), :]
    if kind == "cols":
        c = ref.shape[1] // N_DEV
        return ref.at[:, pl.ds(j * c, c)]
    return ref


def _chunk_shape(shape, kind):
    if kind == "lead":
        return tuple(shape[1:])
    if kind == "rows":
        return (shape[0] // N_DEV, *shape[1:])
    if kind == "mid":
        return (shape[0], shape[1] // N_DEV, shape[2])
    if kind == "cols":
        return (shape[0], shape[1] // N_DEV)
    return tuple(shape)


def scatter_ride(parts):
    n = len(parts)
    kinds = [k for _, k in parts]

    def setup(ins, outs, sems):
        send_sems, recv_sems, local_sems = sems
        x, y, c = _place()
        me = _dev_index((x, y, c))
        peers = []
        for k in range(1, N_DEV):
            kx, ky, kc = (k >> 2) & 1, (k >> 1) & 1, k & 1
            peers.append((1 - x if kx else x, 1 - y if ky else y, 1 - c if kc else c))

        def copy(a, k, peer):
            return pltpu.make_async_remote_copy(
                src_ref=_chunk(ins[a], kinds[a], _dev_index(peer)), dst_ref=outs[a].at[me],
                send_sem=send_sems.at[a * N_PEERS + k], recv_sem=recv_sems.at[a * N_PEERS + k],
                device_id=peer, device_id_type=MESH)

        def arrival(a, k, peer):
            slot = outs[a].at[_dev_index(peer)]
            return pltpu.make_async_remote_copy(
                src_ref=slot, dst_ref=slot,
                send_sem=send_sems.at[a * N_PEERS + k], recv_sem=recv_sems.at[a * N_PEERS + k],
                device_id=peer, device_id_type=MESH)

        def mine(a):
            return pltpu.make_async_copy(_chunk(ins[a], kinds[a], me), outs[a].at[me], local_sems.at[a])

        return peers, copy, arrival, mine

    def start(ins, outs, sems):
        peers, copy, _, mine = setup(ins, outs, sems)
        for a in range(n):
            mine(a).start()
            for k, peer in enumerate(peers):
                copy(a, k, peer).start()

    def finish(ins, outs, sems):
        peers, copy, arrival, mine = setup(ins, outs, sems)
        for a in range(n):
            for k, peer in enumerate(peers):
                arrival(a, k, peer).wait_recv()
        for a in range(n):
            for k, peer in enumerate(peers):
                copy(a, k, peer).wait_send()
            mine(a).wait()

    shapes = [jax.ShapeDtypeStruct((N_DEV, *_chunk_shape(arr.shape, kind)), arr.dtype) for arr, kind in parts]
    return Ride([arr for arr, _ in parts], shapes, start, finish)


HBM = pl.BlockSpec(memory_space=pltpu.HBM)
SEM = pl.BlockSpec(memory_space=pltpu.SEMAPHORE)
DATAFLOW = pltpu.SideEffectType.DATAFLOW_SIDE_EFFECTING
TOKEN = (8, 128)


def _scatter_copies(kinds, ins, lands, send_sems, recv_sems):
    x, y, c = _place()
    me = _dev_index((x, y, c))
    sends, arrivals = [], []
    for a, kind in enumerate(kinds):
        for k in range(1, N_DEV):
            kx, ky, kc = (k >> 2) & 1, (k >> 1) & 1, k & 1
            peer = (1 - x if kx else x, 1 - y if ky else y, 1 - c if kc else c)
            sem = a * N_PEERS + k - 1
            sends.append(pltpu.make_async_remote_copy(
                src_ref=_chunk(ins[a], kind, _dev_index(peer)), dst_ref=lands[a].at[me],
                send_sem=send_sems.at[sem], recv_sem=recv_sems.at[sem], device_id=peer, device_id_type=MESH))
            slot = lands[a].at[_dev_index(peer)]
            arrivals.append(pltpu.make_async_remote_copy(
                src_ref=slot, dst_ref=slot, send_sem=send_sems.at[sem], recv_sem=recv_sems.at[sem],
                device_id=peer, device_id_type=MESH))
    return me, sends, arrivals


def scatter_start(parts, name):
    n = len(parts)
    kinds = [k for _, k in parts]
    arrays = [pltpu.with_memory_space_constraint(a, pltpu.HBM) for a, _ in parts]
    zones = [pltpu.with_memory_space_constraint(lax.empty((N_DEV, *_chunk_shape(a.shape, k)), a.dtype), pltpu.HBM)
             for a, k in parts]

    def body(*refs):
        ins, lands = refs[:n], refs[n:2 * n]
        send_sems, recv_sems = refs[2 * n], refs[2 * n + 1]
        token, local_sems = refs[4 * n + 2], refs[4 * n + 3]
        me, sends, _ = _scatter_copies(kinds, ins, lands, send_sems, recv_sems)
        own = [pltpu.make_async_copy(_chunk(ins[a], kinds[a], me), lands[a].at[me], local_sems.at[a]) for a in range(n)]
        for cp in own + sends:
            cp.start()
        for cp in own:
            cp.wait()
        token[...] = jnp.zeros_like(token)

    outs = pl.pallas_call(
        body, name=name,
        out_shape=(pltpu.SemaphoreType.DMA((n * N_PEERS,)), pltpu.SemaphoreType.DMA((n * N_PEERS,)),
                   *[pltpu.HBM(a.shape, a.dtype) for a in arrays], *[pltpu.HBM(z.shape, z.dtype) for z in zones],
                   jax.ShapeDtypeStruct(TOKEN, F32)),
        in_specs=[HBM] * (2 * n),
        out_specs=(SEM, SEM, *[HBM] * (2 * n), pl.BlockSpec(memory_space=pltpu.VMEM)),
        input_output_aliases={i: 2 + i for i in range(2 * n)},
        scratch_shapes=[pltpu.SemaphoreType.DMA((n,))],
        compiler_params=pltpu.CompilerParams(has_side_effects=DATAFLOW),
    )(*arrays, *zones)
    return kinds, outs[0], outs[1], outs[2:2 + n], outs[2 + n:2 + 2 * n], outs[2 + 2 * n]


def scatter_wait(started, after, name):
    kinds, send_sems, recv_sems, arrays, zones, _ = started
    n = len(kinds)

    def body(*refs):
        ins, lands = refs[:n], refs[n:2 * n]
        _, sends, arrivals = _scatter_copies(kinds, ins, lands, refs[2 * n], refs[2 * n + 1])
        for cp in sends:
            cp.wait_send()
        for cp in arrivals:
            cp.wait_recv()

    outs = pl.pallas_call(
        body, name=name,
        out_shape=(*[pltpu.HBM(a.shape, a.dtype) for a in arrays], *[pltpu.HBM(z.shape, z.dtype) for z in zones]),
        in_specs=[HBM] * (2 * n) + [SEM, SEM, ANY],
        out_specs=[HBM] * (2 * n),
        input_output_aliases={i: i for i in range(2 * n)},
        compiler_params=pltpu.CompilerParams(has_side_effects=DATAFLOW),
    )(*arrays, *zones, send_sems, recv_sems, after)
    return outs[n:]


def run_ride(ride, name):
    n_in, n_out = len(ride.ins), len(ride.out_shapes)

    def body(*refs):
        ins, outs, sems = refs[:n_in], refs[n_in:n_in + n_out], refs[n_in + n_out:]
        ride.start(ins, outs, sems)
        ride.finish(ins, outs, sems)

    return pl.pallas_call(
        body, name=name, in_specs=[ANY] * n_in, out_specs=[ANY] * n_out, out_shape=ride.out_shapes,
        scratch_shapes=ride.sems,
    )(*ride.ins)


def _call(body, *, name, grid, in_specs, out_specs, out_shape, args, scratch_shapes=(), ride=None):
    params = pltpu.CompilerParams(dimension_semantics=("arbitrary",) * len(grid), vmem_limit_bytes=VMEM_LIMIT)
    if ride is None:
        outs = pl.pallas_call(body, name=name, grid=grid, in_specs=in_specs, out_specs=out_specs, out_shape=out_shape,
                              scratch_shapes=list(scratch_shapes), compiler_params=params)(*args)
        return outs, []
    n_in, n_out, n_scr = len(in_specs), len(out_specs), len(scratch_shapes)
    r_in, r_out = len(ride.ins), len(ride.out_shapes)

    def hosted(*refs):
        ins, refs = refs[:n_in], refs[n_in:]
        rins, refs = refs[:r_in], refs[r_in:]
        outs, refs = refs[:n_out], refs[n_out:]
        routs, refs = refs[:r_out], refs[r_out:]
        scratch, sems = refs[:n_scr], refs[n_scr:]
        step, n_steps = pl.program_id(0), grid[0]
        for d in range(1, len(grid)):
            step, n_steps = step * grid[d] + pl.program_id(d), n_steps * grid[d]

        @pl.when(step == 0)
        def _():
            ride.start(rins, routs, sems)

        body(*ins, *outs, *scratch)

        @pl.when(step == n_steps - 1)
        def _():
            ride.finish(rins, routs, sems)

    outs = pl.pallas_call(
        hosted, name=name + "_ride", grid=grid,
        in_specs=list(in_specs) + [ANY] * r_in, out_specs=list(out_specs) + [ANY] * r_out,
        out_shape=list(out_shape) + ride.out_shapes,
        scratch_shapes=list(scratch_shapes) + ride.sems, compiler_params=params,
    )(*args, *ride.ins)
    return outs[:n_out], outs[n_out:]


def ffn_fwd(h, gain, wgu, wd, ride=None):
    t = h.shape[0]

    def body(h_ref, g_ref, wgu_ref, wd_ref, hn_ref, z_ref, u_ref, acc):
        k = pl.program_id(1)

        @pl.when(k == 0)
        def _():
            u_ref[...] = _rms_fwd(h_ref[...], g_ref[...]).astype(BF16)
            acc[...] = jnp.zeros_like(acc)

        u = u_ref[...]
        g = _dot_nt(u, wgu_ref[0, 0])
        up = _dot_nt(u, wgu_ref[1, 0])
        z_ref[0, 0] = g.astype(BF16)
        z_ref[1, 0] = up.astype(BF16)
        a = g * jax.nn.sigmoid(g) * up
        acc[...] += _dot(a.astype(BF16), wd_ref[0])

        @pl.when(k == 3)
        def _():
            hn_ref[...] = h_ref[...] + acc[...]

    return _call(
        body, name="ffn_fwd", grid=(t // TM, 4), ride=ride,
        in_specs=[pl.BlockSpec((TM, D), lambda i, k: (i, 0)),
                  pl.BlockSpec((1, D), lambda i, k: (0, 0)),
                  pl.BlockSpec((2, 1, FC, D), lambda i, k: (0, k, 0, 0)),
                  pl.BlockSpec((1, FC, D), lambda i, k: (k, 0, 0))],
        out_specs=[pl.BlockSpec((TM, D), lambda i, k: (i, 0)),
                   pl.BlockSpec((2, 1, TM, FC), lambda i, k: (0, k, i, 0)),
                   pl.BlockSpec((TM, D), lambda i, k: (i, 0))],
        out_shape=[jax.ShapeDtypeStruct((t, D), F32),
                   jax.ShapeDtypeStruct((2, 4, t, FC), BF16),
                   jax.ShapeDtypeStruct((t, D), BF16)],
        scratch_shapes=[pltpu.VMEM((TM, D), F32)],
        args=(h, gain, wgu, wd))


def ffn_bwd_x(dh, h, gain, z, wgu, wd, ride=None):
    t = h.shape[0]

    def body(dh_ref, h_ref, g_ref, z_ref, wgu_ref, wd_ref, dhp_ref, dz_ref, a_ref, dgain_ref, dhb, du):
        i, k = pl.program_id(0), pl.program_id(1)

        @pl.when(k == 0)
        def _():
            dhb[...] = dh_ref[...].astype(BF16)
            du[...] = jnp.zeros_like(du)

        @pl.when((k == 0) & (i == 0))
        def _():
            dgain_ref[...] = jnp.zeros_like(dgain_ref)

        da = _dot_nt(dhb[...], wd_ref[0])
        g = z_ref[0, 0].astype(F32)
        up = z_ref[1, 0].astype(F32)
        sg = jax.nn.sigmoid(g)
        silu = g * sg
        a_ref[0] = (silu * up).astype(BF16)
        dg = (da * up * (sg * (1.0 + g * (1.0 - sg)))).astype(BF16)
        dup = (da * silu).astype(BF16)
        dz_ref[0, 0] = dg
        dz_ref[1, 0] = dup
        for n in range(2):
            cols = slice(n * (D // 2), (n + 1) * (D // 2))
            du[:, cols] += _dot(dg, wgu_ref[0, 0, :, cols]) + _dot(dup, wgu_ref[1, 0, :, cols])

        @pl.when(k == 3)
        def _():
            dx, dgain = _rms_bwd(h_ref[...], g_ref[...], du[...])
            dhp_ref[...] = dh_ref[...] + dx
            dgain_ref[...] += dgain

    return _call(
        body, name="ffn_bwd_x", grid=(t // TM, 4), ride=ride,
        in_specs=[pl.BlockSpec((TM, D), lambda i, k: (i, 0)),
                  pl.BlockSpec((TM, D), lambda i, k: (i, 0)),
                  pl.BlockSpec((1, D), lambda i, k: (0, 0)),
                  pl.BlockSpec((2, 1, TM, FC), lambda i, k: (0, k, i, 0)),
                  pl.BlockSpec((2, 1, FC, D), lambda i, k: (0, k, 0, 0)),
                  pl.BlockSpec((1, FC, D), lambda i, k: (k, 0, 0))],
        out_specs=[pl.BlockSpec((TM, D), lambda i, k: (i, 0)),
                   pl.BlockSpec((2, 1, TM, FC), lambda i, k: (0, k, i, 0)),
                   pl.BlockSpec((1, TM, FC), lambda i, k: (k, i, 0)),
                   pl.BlockSpec((1, D), lambda i, k: (0, 0)),
                   pl.BlockSpec((TM, D), lambda i, k: (i, 0))],
        out_shape=[jax.ShapeDtypeStruct((t, D), F32),
                   jax.ShapeDtypeStruct((2, 4, t, FC), BF16),
                   jax.ShapeDtypeStruct((4, t, FC), BF16),
                   jax.ShapeDtypeStruct((1, D), F32),
                   jax.ShapeDtypeStruct((t, D), BF16)],
        scratch_shapes=[pltpu.VMEM((TM, D), F32)],
        args=(dh, h, gain, z, wgu, wd))


TW = 2048


def ffn_bwd_w(u, dz, a, dhb, ride=None):
    t = u.shape[0]
    tw = min(TW, t)
    steps = t // tw

    def body(u_ref, dz_ref, a_ref, dh_ref, dwgu_ref, dwd_ref, acc_gu, acc_d):
        j = pl.program_id(1)

        @pl.when(j == 0)
        def _():
            acc_gu[...] = jnp.zeros_like(acc_gu)
            acc_d[...] = jnp.zeros_like(acc_d)

        ub = u_ref[...]
        acc_gu[0] += _dot_tn(dz_ref[0, 0], ub)
        acc_gu[1] += _dot_tn(dz_ref[1, 0], ub)
        acc_d[...] += _dot_tn(a_ref[0], dh_ref[...])

        @pl.when(j == steps - 1)
        def _():
            dwgu_ref[:, 0] = acc_gu[...].astype(BF16)
            dwd_ref[0] = acc_d[...].astype(BF16)

    return _call(
        body, name="ffn_bwd_w", grid=(4, steps), ride=ride,
        in_specs=[pl.BlockSpec((tw, D), lambda k, j: (j, 0)),
                  pl.BlockSpec((2, 1, tw, FC), lambda k, j: (0, k, j, 0)),
                  pl.BlockSpec((1, tw, FC), lambda k, j: (k, j, 0)),
                  pl.BlockSpec((tw, D), lambda k, j: (j, 0))],
        out_specs=[pl.BlockSpec((2, 1, FC, D), lambda k, j: (0, k, 0, 0)),
                   pl.BlockSpec((1, FC, D), lambda k, j: (k, 0, 0))],
        out_shape=[jax.ShapeDtypeStruct((2, 4, FC, D), BF16),
                   jax.ShapeDtypeStruct((4, FC, D), BF16)],
        scratch_shapes=[pltpu.VMEM((2, FC, D), F32), pltpu.VMEM((FC, D), F32)],
        args=(u, dz, a, dhb))


def _prev_halo(i, tile=TM):
    return jnp.maximum(i * (tile // HALO) - 1, 0)


def _next_halo(i, t, tile=TM):
    return jnp.minimum((i + 1) * (tile // HALO), t // HALO - 1)


def rms_matmul(h, gain, w, bias, ride=None):
    t = h.shape[0]
    n = w.shape[1]

    def body(h_ref, g_ref, w_ref, b_ref, z_ref, u_ref):
        u = _rms_fwd(h_ref[...], g_ref[...]).astype(BF16)
        u_ref[...] = u
        z_ref[...] = (_dot(u, w_ref[...]) + b_ref[...]).astype(BF16)

    return _call(
        body, name=f"rms_matmul_{n}", grid=(t // TM,), ride=ride,
        in_specs=[pl.BlockSpec((TM, D), lambda i: (i, 0)),
                  pl.BlockSpec((1, D), lambda i: (0, 0)),
                  pl.BlockSpec((D, n), lambda i: (0, 0)),
                  pl.BlockSpec((1, n), lambda i: (0, 0))],
        out_specs=[pl.BlockSpec((TM, n), lambda i: (i, 0)),
                   pl.BlockSpec((TM, D), lambda i: (i, 0))],
        out_shape=[jax.ShapeDtypeStruct((t, n), BF16),
                   jax.ShapeDtypeStruct((t, D), BF16)],
        args=(h, gain, w, bias))


def in_proj_bwd_x(dz, w, h, gain, dh, ride=None):
    t = h.shape[0]
    n = w.shape[1]

    def body(dz_ref, w_ref, h_ref, g_ref, dh_ref, dhp_ref, dgain_ref, dbias_ref):
        @pl.when(pl.program_id(0) == 0)
        def _():
            dgain_ref[...] = jnp.zeros_like(dgain_ref)
            dbias_ref[...] = jnp.zeros_like(dbias_ref)

        du = _dot_nt(dz_ref[...], w_ref[...])
        dx, dgain = _rms_bwd(h_ref[...], g_ref[...], du)
        dhp_ref[...] = dh_ref[...] + dx
        dgain_ref[...] += dgain
        dbias_ref[...] += jnp.sum(dz_ref[...].astype(F32), axis=0, keepdims=True)

    return _call(
        body, name=f"in_proj_bwd_x_{n}", grid=(t // TM,), ride=ride,
        in_specs=[pl.BlockSpec((TM, n), lambda i: (i, 0)),
                  pl.BlockSpec((D, n), lambda i: (0, 0)),
                  pl.BlockSpec((TM, D), lambda i: (i, 0)),
                  pl.BlockSpec((1, D), lambda i: (0, 0)),
                  pl.BlockSpec((TM, D), lambda i: (i, 0))],
        out_specs=[pl.BlockSpec((TM, D), lambda i: (i, 0)),
                   pl.BlockSpec((1, D), lambda i: (0, 0)),
                   pl.BlockSpec((1, n), lambda i: (0, 0))],
        out_shape=[jax.ShapeDtypeStruct((t, D), F32),
                   jax.ShapeDtypeStruct((1, D), F32),
                   jax.ShapeDtypeStruct((1, n), F32)],
        args=(dz, w, h, gain, dh))


def in_proj_bwd_w(u, dz, ride=None):
    t = u.shape[0]
    n = dz.shape[1]
    steps = t // TM

    def body(u_ref, dz_ref, dw_ref, acc):
        s = pl.program_id(0)

        @pl.when(s == 0)
        def _():
            acc[...] = jnp.zeros_like(acc)

        acc[...] += _dot_tn(u_ref[...], dz_ref[...])

        @pl.when(s == steps - 1)
        def _():
            dw_ref[...] = acc[...].astype(BF16)

    return _call(
        body, name=f"in_proj_bwd_w_{n}", grid=(steps,), ride=ride,
        in_specs=[pl.BlockSpec((TM, D), lambda s: (s, 0)),
                  pl.BlockSpec((TM, n), lambda s: (s, 0))],
        out_specs=[pl.BlockSpec((D, n), lambda s: (0, 0))],
        out_shape=[jax.ShapeDtypeStruct((D, n), BF16)],
        scratch_shapes=[pltpu.VMEM((D, n), F32)],
        args=(u, dz))


A_TAPS = 3


def a_mix_fwd(z, h, conv, wout, ride=None):
    t = h.shape[0]

    def body(z_ref, zp_ref, h_ref, cw_ref, wo_ref, hn_ref, pad):
        i = pl.program_id(0)
        ph = zp_ref[:, D:2 * D].astype(F32) * zp_ref[:, 2 * D:].astype(F32)
        pad[0:HALO, :] = jnp.where(i == 0, 0.0, ph)
        pad[HALO:, :] = z_ref[:, D:2 * D].astype(F32) * z_ref[:, 2 * D:].astype(F32)
        q = jnp.zeros((TM, D), F32)
        for k in range(A_TAPS):
            off = HALO - (A_TAPS - 1) + k
            q += cw_ref[k:k + 1, :] * pad[off:off + TM, :]
        r = z_ref[:, 0:D].astype(F32) * q
        hn_ref[...] = h_ref[...] + _dot(r.astype(BF16), wo_ref[...])

    return _call(
        body, name="a_mix_fwd", grid=(t // TM,), ride=ride,
        in_specs=[pl.BlockSpec((TM, 3 * D), lambda i: (i, 0)),
                  pl.BlockSpec((HALO, 3 * D), lambda i: (_prev_halo(i), 0)),
                  pl.BlockSpec((TM, D), lambda i: (i, 0)),
                  pl.BlockSpec((A_TAPS, D), lambda i: (0, 0)),
                  pl.BlockSpec((D, D), lambda i: (0, 0))],
        out_specs=[pl.BlockSpec((TM, D), lambda i: (i, 0))],
        out_shape=[jax.ShapeDtypeStruct((t, D), F32)],
        scratch_shapes=[pltpu.VMEM((HALO + TM, D), F32)],
        args=(z, z, h, conv, wout))


def a_mix_bwd(dh, z, conv, wout, ride=None):
    t = dh.shape[0]
    steps = t // TM

    def body(dh_ref, dhn_ref, z_ref, zp_ref, zn_ref, cw_ref, wo_ref, dz_ref, dwo_ref, dcw_ref, pad, dqpad, dwo):
        i = pl.program_id(0)
        last = i == steps - 1

        @pl.when(i == 0)
        def _():
            dwo[...] = jnp.zeros_like(dwo)
            dcw_ref[...] = jnp.zeros_like(dcw_ref)

        ph = zp_ref[:, D:2 * D].astype(F32) * zp_ref[:, 2 * D:].astype(F32)
        pad[0:HALO, :] = jnp.where(i == 0, 0.0, ph)
        c = z_ref[:, D:2 * D].astype(F32)
        v = z_ref[:, 2 * D:].astype(F32)
        pad[HALO:, :] = c * v
        q = jnp.zeros((TM, D), F32)
        for k in range(A_TAPS):
            off = HALO - (A_TAPS - 1) + k
            q += cw_ref[k:k + 1, :] * pad[off:off + TM, :]
        b = z_ref[:, 0:D].astype(F32)
        dhb = dh_ref[...].astype(BF16)
        dwo[...] += _dot_tn((b * q).astype(BF16), dhb)
        dr = _dot_nt(dhb, wo_ref[...])
        dz_ref[:, 0:D] = (dr * q).astype(BF16)
        dq = dr * b
        drn = _dot_nt(dhn_ref[...].astype(BF16), wo_ref[...])
        dqpad[0:TM, :] = dq
        dqpad[TM:, :] = jnp.where(last, 0.0, drn * zn_ref[:, 0:D].astype(F32))
        dp = jnp.zeros((TM, D), F32)
        for k in range(A_TAPS):
            off = A_TAPS - 1 - k
            dp += cw_ref[k:k + 1, :] * dqpad[off:off + TM, :]
            poff = HALO - (A_TAPS - 1) + k
            dcw_ref[k:k + 1, :] += jnp.sum(dq * pad[poff:poff + TM, :], axis=0, keepdims=True)
        dz_ref[:, D:2 * D] = (dp * v).astype(BF16)
        dz_ref[:, 2 * D:] = (dp * c).astype(BF16)

        @pl.when(last)
        def _():
            dwo_ref[...] = dwo[...].astype(BF16)

    return _call(
        body, name="a_mix_bwd", grid=(steps,), ride=ride,
        in_specs=[pl.BlockSpec((TM, D), lambda i: (i, 0)),
                  pl.BlockSpec((HALO, D), lambda i: (_next_halo(i, t), 0)),
                  pl.BlockSpec((TM, 3 * D), lambda i: (i, 0)),
                  pl.BlockSpec((HALO, 3 * D), lambda i: (_prev_halo(i), 0)),
                  pl.BlockSpec((HALO, 3 * D), lambda i: (_next_halo(i, t), 0)),
                  pl.BlockSpec((A_TAPS, D), lambda i: (0, 0)),
                  pl.BlockSpec((D, D), lambda i: (0, 0))],
        out_specs=[pl.BlockSpec((TM, 3 * D), lambda i: (i, 0)),
                   pl.BlockSpec((D, D), lambda i: (0, 0)),
                   pl.BlockSpec((A_TAPS, D), lambda i: (0, 0))],
        out_shape=[jax.ShapeDtypeStruct((t, 3 * D), BF16),
                   jax.ShapeDtypeStruct((D, D), BF16),
                   jax.ShapeDtypeStruct((A_TAPS, D), F32)],
        scratch_shapes=[pltpu.VMEM((HALO + TM, D), F32), pltpu.VMEM((TM + HALO, D), F32), pltpu.VMEM((D, D), F32)],
        args=(dh, dh, z, z, z, conv, wout))


C_TAPS = 31


def _glu(zr):
    return zr[:, 0:D].astype(F32) * jax.nn.sigmoid(zr[:, D:].astype(F32))


def _ln_silu(h2, lg, lb):
    mu = jnp.mean(h2, axis=-1, keepdims=True)
    xc = h2 - mu
    rstd = lax.rsqrt(jnp.mean(xc * xc, axis=-1, keepdims=True) + LN_EPS)
    xn = xc * rstd
    h3 = xn * lg + lb
    s3 = jax.nn.sigmoid(h3)
    return xn, rstd, h3, s3


def _ln_silu_bwd(h2, lg, lb, dh4):
    xn, rstd, h3, s3 = _ln_silu(h2, lg, lb)
    dh3 = dh4 * (s3 * (1.0 + h3 * (1.0 - s3)))
    dxn = dh3 * lg
    dh2 = rstd * (dxn - jnp.mean(dxn, axis=-1, keepdims=True) - xn * jnp.mean(dxn * xn, axis=-1, keepdims=True))
    return dh2, dh3, xn, h3 * s3


TC = 256
RB = 64
LANES = 128
SHIFTS = 7


def _shifted_copies(src, sh, rows):
    for b in range(1, SHIFTS + 1):
        sh[b - 1, 0:rows, :] = src[b:b + rows, :]


def _window(src, sh, o, r0, lanes):
    a, b = divmod(o, 8)
    ref = src if b == 0 else sh.at[b - 1]
    return ref[8 * a + r0:8 * a + r0 + RB, lanes]


def c_mix_fwd(z, h, dw, bdw, lg, lb, w2, b2, ride=None):
    t = h.shape[0]

    def body(z_ref, zp_ref, h_ref, dw_ref, bdw_ref, lg_ref, lb_ref, w2_ref, b2_ref, hn_ref, h2_ref, pad, sh):
        i = pl.program_id(0)
        pad[0:HALO, :] = jnp.where(i == 0, 0.0, _glu(zp_ref))
        pad[HALO:, :] = _glu(z_ref)
        _shifted_copies(pad, sh, TC + 24)
        for l in range(D // LANES):
            lanes = slice(l * LANES, (l + 1) * LANES)
            for r0 in range(0, TC, RB):
                acc = jnp.zeros((RB, LANES), F32) + bdw_ref[:, lanes]
                for k in range(C_TAPS):
                    acc += dw_ref[k:k + 1, lanes] * _window(pad, sh, HALO - (C_TAPS - 1) + k, r0, lanes)
                h2_ref[r0:r0 + RB, lanes] = acc
        _, _, h3, s3 = _ln_silu(h2_ref[...], lg_ref[...], lb_ref[...])
        hn_ref[...] = h_ref[...] + _dot((h3 * s3).astype(BF16), w2_ref[...]) + b2_ref[...]

    vec = pl.BlockSpec((1, D), lambda i: (0, 0))
    return _call(
        body, name="c_mix_fwd", grid=(t // TC,), ride=ride,
        in_specs=[pl.BlockSpec((TC, 2 * D), lambda i: (i, 0)),
                  pl.BlockSpec((HALO, 2 * D), lambda i: (_prev_halo(i, TC), 0)),
                  pl.BlockSpec((TC, D), lambda i: (i, 0)),
                  pl.BlockSpec((C_TAPS, D), lambda i: (0, 0)),
                  vec, vec, vec,
                  pl.BlockSpec((D, D), lambda i: (0, 0)),
                  vec],
        out_specs=[pl.BlockSpec((TC, D), lambda i: (i, 0)),
                   pl.BlockSpec((TC, D), lambda i: (i, 0))],
        out_shape=[jax.ShapeDtypeStruct((t, D), F32),
                   jax.ShapeDtypeStruct((t, D), F32)],
        scratch_shapes=[pltpu.VMEM((HALO + TC, D), F32), pltpu.VMEM((SHIFTS, TC + 24, D), F32)],
        args=(z, z, h, dw, bdw, lg, lb, w2, b2))


def c_mix_bwd(dh, z, h2, dw, lg, lb, w2, ride=None):
    t = dh.shape[0]
    steps = t // TC

    def body(dh_ref, dhn_ref, z_ref, zp_ref, h2_ref, h2n_ref, dw_ref, lg_ref, lb_ref, w2_ref,
             dz_ref, dw2_ref, db2_ref, dlg_ref, dlb_ref, dbdw_ref, ddw_ref, pad, dpad, dw2, sh, dh1):
        i = pl.program_id(0)
        last = i == steps - 1

        @pl.when(i == 0)
        def _():
            for r in (dw2, db2_ref, dlg_ref, dlb_ref, dbdw_ref, ddw_ref):
                r[...] = jnp.zeros_like(r)

        lg, lb = lg_ref[...], lb_ref[...]
        dh = dh_ref[...]
        dhb = dh.astype(BF16)
        dh2, dh3, xn, h4 = _ln_silu_bwd(h2_ref[...], lg, lb, _dot_nt(dhb, w2_ref[...]))
        dw2[...] += _dot_tn(h4.astype(BF16), dhb)
        db2_ref[...] += jnp.sum(dh, axis=0, keepdims=True)
        dlg_ref[...] += jnp.sum(dh3 * xn, axis=0, keepdims=True)
        dlb_ref[...] += jnp.sum(dh3, axis=0, keepdims=True)
        dbdw_ref[...] += jnp.sum(dh2, axis=0, keepdims=True)
        dh2n, _, _, _ = _ln_silu_bwd(h2n_ref[...], lg, lb, _dot_nt(dhn_ref[...].astype(BF16), w2_ref[...]))
        dpad[0:TC, :] = dh2
        dpad[TC:, :] = jnp.where(last, 0.0, dh2n)
        _shifted_copies(dpad, sh, TC + 24)
        for l in range(D // LANES):
            lanes = slice(l * LANES, (l + 1) * LANES)
            for r0 in range(0, TC, RB):
                acc = jnp.zeros((RB, LANES), F32)
                for k in range(C_TAPS):
                    acc += dw_ref[k:k + 1, lanes] * _window(dpad, sh, C_TAPS - 1 - k, r0, lanes)
                dh1[r0:r0 + RB, lanes] = acc
        pad[0:HALO, :] = jnp.where(i == 0, 0.0, _glu(zp_ref))
        pad[HALO:, :] = _glu(z_ref)
        _shifted_copies(pad, sh, TC + 24)
        for l in range(D // LANES):
            lanes = slice(l * LANES, (l + 1) * LANES)
            accs = [jnp.zeros((8, LANES), F32) for _ in range(C_TAPS)]
            for r0 in range(0, TC, RB):
                d = dpad[r0:r0 + RB, lanes]
                for k in range(C_TAPS):
                    prod = d * _window(pad, sh, HALO - (C_TAPS - 1) + k, r0, lanes)
                    accs[k] += jnp.sum(prod.reshape(RB // 8, 8, LANES), axis=0)
            for k in range(C_TAPS):
                ddw_ref[k:k + 1, lanes] += jnp.sum(accs[k], axis=0, keepdims=True)
        a = z_ref[:, 0:D].astype(F32)
        sg = jax.nn.sigmoid(z_ref[:, D:].astype(F32))
        d1 = dh1[...]
        dz_ref[:, 0:D] = (d1 * sg).astype(BF16)
        dz_ref[:, D:] = (d1 * a * sg * (1.0 - sg)).astype(BF16)

        @pl.when(last)
        def _():
            dw2_ref[...] = dw2[...].astype(BF16)

    vec = pl.BlockSpec((1, D), lambda i: (0, 0))
    return _call(
        body, name="c_mix_bwd", grid=(steps,), ride=ride,
        in_specs=[pl.BlockSpec((TC, D), lambda i: (i, 0)),
                  pl.BlockSpec((HALO, D), lambda i: (_next_halo(i, t, TC), 0)),
                  pl.BlockSpec((TC, 2 * D), lambda i: (i, 0)),
                  pl.BlockSpec((HALO, 2 * D), lambda i: (_prev_halo(i, TC), 0)),
                  pl.BlockSpec((TC, D), lambda i: (i, 0)),
                  pl.BlockSpec((HALO, D), lambda i: (_next_halo(i, t, TC), 0)),
                  pl.BlockSpec((C_TAPS, D), lambda i: (0, 0)),
                  vec, vec,
                  pl.BlockSpec((D, D), lambda i: (0, 0))],
        out_specs=[pl.BlockSpec((TC, 2 * D), lambda i: (i, 0)),
                   pl.BlockSpec((D, D), lambda i: (0, 0)),
                   vec, vec, vec, vec,
                   pl.BlockSpec((C_TAPS, D), lambda i: (0, 0))],
        out_shape=[jax.ShapeDtypeStruct((t, 2 * D), BF16),
                   jax.ShapeDtypeStruct((D, D), BF16)]
                  + [jax.ShapeDtypeStruct((1, D), F32)] * 4
                  + [jax.ShapeDtypeStruct((C_TAPS, D), F32)],
        scratch_shapes=[pltpu.VMEM((HALO + TC, D), F32), pltpu.VMEM((TC + HALO, D), F32), pltpu.VMEM((D, D), F32),
                        pltpu.VMEM((SHIFTS, TC + 24, D), F32), pltpu.VMEM((TC, D), F32)],
        args=(dh, dh, z, z, h2, h2, dw, lg, lb, w2))


POOL_WINDOWS = (2, 4, 8, 16)
GW = D // len(POOL_WINDOWS)


def _pool_mixed(pad, g, w, inv_cnt):
    cols = slice(g * GW, (g + 1) * GW)
    s = pad[HALO:HALO + TM, cols]
    u = s
    for j in range(1, w):
        s = s + pad[HALO - j:HALO - j + TM, cols]
    return s * inv_cnt - u


def _inv_cnt(i, w):
    row = i * TM + lax.broadcasted_iota(jnp.int32, (TM, 1), 0)
    return 1.0 / jnp.minimum(row + 1, w).astype(F32)


def b_mix_fwd(h, gain, wg, scale, ride=None):
    t = h.shape[0]

    def body(h_ref, hp_ref, g_ref, wg_ref, sc_ref, hn_ref, pad):
        i = pl.program_id(0)
        gain = g_ref[...]
        pad[0:HALO, :] = jnp.where(i == 0, 0.0, _rms_fwd(hp_ref[...], gain))
        pad[HALO:, :] = _rms_fwd(h_ref[...], gain)
        for g, w in enumerate(POOL_WINDOWS):
            cols = slice(g * GW, (g + 1) * GW)
            mixed = _pool_mixed(pad, g, w, _inv_cnt(i, w))
            y = _dot(mixed.astype(BF16), wg_ref[g])
            hn_ref[:, cols] = h_ref[:, cols] + y * sc_ref[:, cols]

    return _call(
        body, name="b_mix_fwd", grid=(t // TM,), ride=ride,
        in_specs=[pl.BlockSpec((TM, D), lambda i: (i, 0)),
                  pl.BlockSpec((HALO, D), lambda i: (_prev_halo(i), 0)),
                  pl.BlockSpec((1, D), lambda i: (0, 0)),
                  pl.BlockSpec((4, GW, GW), lambda i: (0, 0, 0)),
                  pl.BlockSpec((1, D), lambda i: (0, 0))],
        out_specs=[pl.BlockSpec((TM, D), lambda i: (i, 0))],
        out_shape=[jax.ShapeDtypeStruct((t, D), F32)],
        scratch_shapes=[pltpu.VMEM((HALO + TM, D), F32)],
        args=(h, h, gain, wg, scale))


def b_mix_bwd(dh, h, gain, wg, scale, ride=None):
    t = h.shape[0]
    steps = t // TM

    def body(dh_ref, dhn_ref, h_ref, hp_ref, g_ref, wg_ref, sc_ref, dhp_ref, dgain_ref, dwg_ref, dsc_ref, pad, dpad, du):
        i = pl.program_id(0)
        last = i == steps - 1

        @pl.when(i == 0)
        def _():
            for r in (dgain_ref, dwg_ref, dsc_ref):
                r[...] = jnp.zeros_like(r)

        gain = g_ref[...]
        pad[0:HALO, :] = jnp.where(i == 0, 0.0, _rms_fwd(hp_ref[...], gain))
        pad[HALO:, :] = _rms_fwd(h_ref[...], gain)
        for g, w in enumerate(POOL_WINDOWS):
            cols = slice(g * GW, (g + 1) * GW)
            inv_cnt = _inv_cnt(i, w)
            mixed = _pool_mixed(pad, g, w, inv_cnt).astype(BF16)
            dh = dh_ref[:, cols]
            dsc_ref[:, cols] += jnp.sum(dh * _dot(mixed, wg_ref[g]), axis=0, keepdims=True)
            dy = (dh * sc_ref[:, cols]).astype(BF16)
            dwg_ref[g] += _dot_tn(mixed, dy)
            dm = _dot_nt(dy, wg_ref[g])
            dmn = _dot_nt((dhn_ref[:, cols] * sc_ref[:, cols]).astype(BF16), wg_ref[g])
            dpad[0:TM, cols] = dm * inv_cnt
            dpad[TM:, cols] = jnp.where(last, 0.0, dmn * (1.0 / w))
            s = dpad[0:TM, cols]
            for j in range(1, w):
                s = s + dpad[j:j + TM, cols]
            du[:, cols] = s - dm
        dx, dgain = _rms_bwd(h_ref[...], gain, du[...])
        dhp_ref[...] = dh_ref[...] + dx
        dgain_ref[...] += dgain

    return _call(
        body, name="b_mix_bwd", grid=(steps,), ride=ride,
        in_specs=[pl.BlockSpec((TM, D), lambda i: (i, 0)),
                  pl.BlockSpec((HALO, D), lambda i: (_next_halo(i, t), 0)),
                  pl.BlockSpec((TM, D), lambda i: (i, 0)),
                  pl.BlockSpec((HALO, D), lambda i: (_prev_halo(i), 0)),
                  pl.BlockSpec((1, D), lambda i: (0, 0)),
                  pl.BlockSpec((4, GW, GW), lambda i: (0, 0, 0)),
                  pl.BlockSpec((1, D), lambda i: (0, 0))],
        out_specs=[pl.BlockSpec((TM, D), lambda i: (i, 0)),
                   pl.BlockSpec((1, D), lambda i: (0, 0)),
                   pl.BlockSpec((4, GW, GW), lambda i: (0, 0, 0)),
                   pl.BlockSpec((1, D), lambda i: (0, 0))],
        out_shape=[jax.ShapeDtypeStruct((t, D), F32),
                   jax.ShapeDtypeStruct((1, D), F32),
                   jax.ShapeDtypeStruct((4, GW, GW), F32),
                   jax.ShapeDtypeStruct((1, D), F32)],
        scratch_shapes=[pltpu.VMEM((HALO + TM, D), F32), pltpu.VMEM((TM + HALO, D), F32), pltpu.VMEM((TM, D), F32)],
        args=(dh, dh, h, h, gain, wg, scale))


LOSS_LANES = 128


def loss_head(h, gain, target):
    t = h.shape[0]

    def body(h_ref, g_ref, tg_ref, loss_ref, dh_ref, dgain_ref):
        @pl.when(pl.program_id(0) == 0)
        def _():
            loss_ref[...] = jnp.zeros_like(loss_ref)
            dgain_ref[...] = jnp.zeros_like(dgain_ref)

        x, gain = h_ref[...], g_ref[...]
        err = _rms_fwd(x, gain) - tg_ref[...]
        per_row = jnp.mean(err * err, axis=-1, keepdims=True)
        loss_ref[...] += jnp.broadcast_to(0.5 * jnp.sum(per_row, axis=0, keepdims=True), (1, LOSS_LANES))
        dx, dgain = _rms_bwd(x, gain, err * (1.0 / D))
        dh_ref[...] = dx
        dgain_ref[...] += dgain

    outs, _ = _call(
        body, name="loss_head", grid=(t // TM,),
        in_specs=[pl.BlockSpec((TM, D), lambda i: (i, 0)),
                  pl.BlockSpec((1, D), lambda i: (0, 0)),
                  pl.BlockSpec((TM, D), lambda i: (i, 0))],
        out_specs=[pl.BlockSpec((1, LOSS_LANES), lambda i: (0, 0)),
                   pl.BlockSpec((TM, D), lambda i: (i, 0)),
                   pl.BlockSpec((1, D), lambda i: (0, 0))],
        out_shape=[jax.ShapeDtypeStruct((1, LOSS_LANES), F32),
                   jax.ShapeDtypeStruct((t, D), F32),
                   jax.ShapeDtypeStruct((1, D), F32)],
        args=(h, gain, target))
    return outs


ADAM_LR = 0.001
ADAM_B1 = 0.9
ADAM_B2 = 0.999
ADAM_EPS = 1e-08
ADAM_WD = 0.01
ADAM_STEP = 10
ADAM_VMEM = 40 * 1024 * 1024


def cast_all(arrays):
    def body(*refs):
        for src, dst in zip(refs[:len(arrays)], refs[len(arrays):]):
            dst[...] = src[...].astype(BF16)

    return pl.pallas_call(
        body, name="cast_all", out_shape=[jax.ShapeDtypeStruct(a.shape, BF16) for a in arrays],
        compiler_params=pltpu.CompilerParams(vmem_limit_bytes=VMEM_LIMIT),
    )(*arrays)


def _adam_math(w, m, v, g):
    m = ADAM_B1 * m + (1.0 - ADAM_B1) * g
    v = ADAM_B2 * v + (1.0 - ADAM_B2) * (g * g)
    m_hat = m / (1.0 - ADAM_B1 ** ADAM_STEP)
    v_hat = v / (1.0 - ADAM_B2 ** ADAM_STEP)
    return -ADAM_LR * (m_hat / (jnp.sqrt(v_hat) + ADAM_EPS) + ADAM_WD * w), m, v


def adamw(ws, ms, vs, gps, rb, tokens=()):
    n = len(ws)
    r, c = ws[0].shape
    nb = r // rb

    def body(*refs):
        i = pl.program_id(0)
        outs = refs[4 * n + len(tokens):]
        for j in range(n):
            w_ref, m_ref, v_ref, gp_ref = (refs[q * n + j] for q in range(4))
            g_ref, d_ref, nm_ref, nv_ref = (outs[q * n + j] for q in range(4))

            @pl.when(i // nb == j)
            def _():
                g = gp_ref[0].astype(F32)
                for s in range(1, N_DEV):
                    g = g + gp_ref[s].astype(F32)
                g_ref[...] = g
                d_ref[...], nm_ref[...], nv_ref[...] = _adam_math(w_ref[...], m_ref[...], v_ref[...], g)

    def blk(j):
        return pl.BlockSpec((rb, c), lambda i: (jnp.clip(i - j * nb, 0, nb - 1), 0))

    def gblk(j):
        return pl.BlockSpec((N_DEV, rb, c), lambda i: (0, jnp.clip(i - j * nb, 0, nb - 1), 0))

    outs, _ = _call(
        body, name=f"adamw_{n}x{r}x{c}", grid=(n * nb,),
        in_specs=[blk(j) for _ in range(3) for j in range(n)] + [gblk(j) for j in range(n)] + [ANY] * len(tokens),
        out_specs=[blk(j) for _ in range(4) for j in range(n)],
        out_shape=[jax.ShapeDtypeStruct((r, c), F32)] * (4 * n),
        args=(*ws, *ms, *vs, *gps, *tokens))
    return outs[:n], outs[n:2 * n], outs[2 * n:3 * n], outs[3 * n:]


def adamw_vectors(ws, ms, vs, gparts):
    nv = len(ws)

    def body(*refs):
        w_refs, m_refs, v_refs = refs[:nv], refs[nv:2 * nv], refs[2 * nv:3 * nv]
        gp_ref = refs[3 * nv]
        outs = refs[3 * nv + 1:]
        g_refs, d_refs, nm_refs, nv_refs = outs[:nv], outs[nv:2 * nv], outs[2 * nv:3 * nv], outs[3 * nv:]
        row = 0
        for i in range(nv):
            for part in range(w_refs[i].shape[1] // D):
                cols = slice(part * D, (part + 1) * D)
                g = gp_ref[0, row:row + 1, :]
                for s in range(1, N_DEV):
                    g = g + gp_ref[s, row:row + 1, :]
                g_refs[i][:, cols] = g
                d_refs[i][:, cols], nm_refs[i][:, cols], nv_refs[i][:, cols] = _adam_math(
                    w_refs[i][:, cols], m_refs[i][:, cols], v_refs[i][:, cols], g)
                row += 1

    shapes = [jax.ShapeDtypeStruct(w.shape, F32) for w in ws]
    outs = pl.pallas_call(body, name="adamw_vectors", out_shape=shapes * 4)(*ws, *ms, *vs, gparts)
    return outs[:nv], outs[nv:2 * nv], outs[2 * nv:3 * nv], outs[3 * nv:]


WEIGHTS = ["ln1_0", "a0_w_in", "a0_conv", "a0_w_out", "ln2_0", "ffn0_w_gu", "ffn0_w_down",
           "ln1_1", "b1_w_grp", "b1_scale", "ln2_1", "ffn1_w_gu", "ffn1_w_down",
           "ln1_2", "c2_w_pw1", "c2_b_pw1", "c2_dw", "c2_b_dw", "c2_ln_g", "c2_ln_b", "c2_w_pw2", "c2_b_pw2",
           "ln2_2", "ffn2_w_gu", "ffn2_w_down",
           "ln1_3", "a3_w_in", "a3_conv", "a3_w_out", "ln2_3", "ffn3_w_gu", "ffn3_w_down", "ln_f"]
SHARDED = {"a0_w_in": ("cols", 256), "a0_conv": ("cols", A_TAPS), "a0_w_out": ("rows", 128),
           "ffn0_w_gu": ("lead", 176), "ffn0_w_down": ("rows", 176),
           "b1_w_grp": ("mid", 128),
           "ffn1_w_gu": ("lead", 176), "ffn1_w_down": ("rows", 176),
           "c2_w_pw1": ("cols", 256), "c2_dw": ("cols", C_TAPS), "c2_w_pw2": ("rows", 128),
           "ffn2_w_gu": ("lead", 176), "ffn2_w_down": ("rows", 176),
           "a3_w_in": ("cols", 256), "a3_conv": ("cols", A_TAPS), "a3_w_out": ("rows", 128),
           "ffn3_w_gu": ("lead", 176), "ffn3_w_down": ("rows", 176)}
IN_PROJ = ("a0_w_in", "c2_w_pw1", "a3_w_in")
REPL = [n for n in WEIGHTS if n not in SHARDED]
REPL_ROWS = 16
GATHER_PLAN = {"first": ["a0_w_in", "a0_w_out", "a0_conv"],
               "in0": ["ffn0_w_gu"], "mix0": ["ffn0_w_down"],
               "ffn0": ["b1_w_grp", "ffn1_w_gu", "ffn1_w_down"],
               "ffn1": ["c2_w_pw1", "c2_w_pw2", "c2_dw", "ffn2_w_gu"],
               "in2": ["ffn2_w_down"],
               "mix2": ["a3_w_in", "a3_w_out", "a3_conv"],
               "ffn2": ["ffn3_w_gu", "ffn3_w_down"]}
SCATTER_PLAN = {"mixb3": ["ffn3_w_down"], "inw3": ["a3_w_out", "a3_conv"],
                "ffnx2": ["ffn3_w_gu"], "ffnw2": ["a3_w_in"],
                "mixb2": ["ffn2_w_gu", "ffn2_w_down"], "inw2": ["c2_w_pw2", "c2_dw"],
                "ffnx1": ["c2_w_pw1"],
                "ffnx0": ["ffn1_w_gu"], "ffnw0": ["ffn1_w_down", "b1_w_grp"],
                "last": ["repl"]}
LATE_FFN = ["ffn0_w_gu", "ffn0_w_down"]
LATE_MIX = ["a0_w_in", "a0_w_out", "a0_conv"]


def _step(p):
    vec = lambda n: p[n].reshape(1, -1)
    x, target = p["x"][0], p["loss_target"][0]

    names = list(SHARDED)
    stored = lambda n, a: a.T if n.endswith("w_gu") else a
    shard = dict(zip(names, cast_all([stored(n, p[n]) for n in names])))
    full = {}

    def gather(slot):
        names = GATHER_PLAN[slot]
        return gather_ride([shard[n] for n in names], ["cols" if n in IN_PROJ else "lead" for n in names])

    def landed(slot, outs):
        full.update(zip(GATHER_PLAN[slot], outs))

    def conv_full(n):
        k = full[n].shape[1]
        return full[n].transpose(1, 0, 2).reshape(k, D).astype(F32)

    def wgu(i):
        return full[f"ffn{i}_w_gu"].reshape(2, 4, FC, D)

    def wd(i):
        return full[f"ffn{i}_w_down"].reshape(4, FC, D)

    landed("first", run_ride(gather("first"), "gather_first"))
    no_bias = jnp.zeros((1, 3 * D), F32)
    h = [x]
    saved = {}
    conv, wout = {}, {}

    (z, u), got = rms_matmul(h[-1], vec("ln1_0"), full["a0_w_in"], no_bias, ride=gather("in0"))
    landed("in0", got)
    conv[0], wout[0] = conv_full("a0_conv"), full["a0_w_out"].reshape(D, D)
    (hm,), got = a_mix_fwd(z, h[-1], conv[0], wout[0], ride=gather("mix0"))
    landed("mix0", got)
    saved["mix0"] = (z, u)
    h.append(hm)
    (hn, zf, uf), got = ffn_fwd(hm, vec("ln2_0"), wgu(0), wd(0), ride=gather("ffn0"))
    landed("ffn0", got)
    saved["ffn0"] = (zf, uf)
    h.append(hn)

    wgrp = full["b1_w_grp"].transpose(1, 0, 2, 3).reshape(4, GW, GW)
    (hm,), _ = b_mix_fwd(h[-1], vec("ln1_1"), wgrp, vec("b1_scale"))
    h.append(hm)
    (hn, zf, uf), got = ffn_fwd(hm, vec("ln2_1"), wgu(1), wd(1), ride=gather("ffn1"))
    landed("ffn1", got)
    saved["ffn1"] = (zf, uf)
    h.append(hn)

    (z, u), got = rms_matmul(h[-1], vec("ln1_2"), full["c2_w_pw1"], vec("c2_b_pw1"), ride=gather("in2"))
    landed("in2", got)
    cdw, wpw2 = conv_full("c2_dw"), full["c2_w_pw2"].reshape(D, D)
    (hm, h2), got = c_mix_fwd(z, h[-1], cdw, vec("c2_b_dw"), vec("c2_ln_g"), vec("c2_ln_b"), wpw2, vec("c2_b_pw2"),
                              ride=gather("mix2"))
    landed("mix2", got)
    saved["mix2"] = (z, u, h2)
    h.append(hm)
    (hn, zf, uf), got = ffn_fwd(hm, vec("ln2_2"), wgu(2), wd(2), ride=gather("ffn2"))
    landed("ffn2", got)
    saved["ffn2"] = (zf, uf)
    h.append(hn)

    (z, u), _ = rms_matmul(h[-1], vec("ln1_3"), full["a3_w_in"], no_bias)
    conv[3], wout[3] = conv_full("a3_conv"), full["a3_w_out"].reshape(D, D)
    (hm,), _ = a_mix_fwd(z, h[-1], conv[3], wout[3])
    saved["mix3"] = (z, u)
    h.append(hm)
    (hn, zf, uf), _ = ffn_fwd(hm, vec("ln2_3"), wgu(3), wd(3))
    saved["ffn3"] = (zf, uf)
    h.append(hn)

    loss_lanes, dh, g_lnf = loss_head(h[-1], vec("ln_f"), target)

    g = {"ln_f": g_lnf}
    recv = {}

    def repl_rows():
        return jnp.concatenate([g[n].reshape(-1, D) for n in REPL], axis=0)

    def scatter(slot):
        parts = []
        for n in SCATTER_PLAN.get(slot, []):
            parts.append((repl_rows(), "all") if n == "repl" else (g[n], SHARDED[n][0]))
        return scatter_ride(parts) if parts else None

    def arrived(slot, outs):
        recv.update(zip(SCATTER_PLAN.get(slot, []), outs))

    for i in (3, 2, 1, 0):
        zf, uf = saved[f"ffn{i}"]
        (dh_prev, dzf, a, g[f"ln2_{i}"], dhb), got = ffn_bwd_x(dh, h[2 * i + 1], vec(f"ln2_{i}"), zf, wgu(i), wd(i),
                                                             ride=scatter(f"ffnx{i}"))
        arrived(f"ffnx{i}", got)
        dh = dh_prev
        (dwgu, dwd), got = ffn_bwd_w(uf, dzf, a, dhb, ride=scatter(f"ffnw{i}"))
        arrived(f"ffnw{i}", got)
        g[f"ffn{i}_w_gu"], g[f"ffn{i}_w_down"] = dwgu.reshape(N_DEV, FC, D), dwd.reshape(FF, D)
        if i == 0:
            late_ffn = scatter_start([(g[n], SHARDED[n][0]) for n in LATE_FFN], "late_ffn_start")
        hin = h[2 * i]
        if i in (0, 3):
            z, u = saved[f"mix{i}"]
            (dz, g[f"a{i}_w_out"], g[f"a{i}_conv"]), got = a_mix_bwd(dh, z, conv[i], wout[i], ride=scatter(f"mixb{i}"))
            arrived(f"mixb{i}", got)
            (g[f"a{i}_w_in"],), got = in_proj_bwd_w(u, dz, ride=scatter(f"inw{i}"))
            arrived(f"inw{i}", got)
            if i == 0:
                late_mix = scatter_start([(g[n], SHARDED[n][0]) for n in LATE_MIX], "late_mix_start")
            (dh, g[f"ln1_{i}"], _), got = in_proj_bwd_x(dz, full[f"a{i}_w_in"], hin, vec(f"ln1_{i}"), dh,
                                                       ride=scatter(f"inx{i}"))
            arrived(f"inx{i}", got)
        elif i == 1:
            (dh, g["ln1_1"], g["b1_w_grp"], g["b1_scale"]), got = b_mix_bwd(dh, hin, vec("ln1_1"), wgrp, vec("b1_scale"),
                                                                             ride=scatter("mixb1"))
            arrived("mixb1", got)
        else:
            z, u, h2 = saved["mix2"]
            (dz, g["c2_w_pw2"], g["c2_b_pw2"], g["c2_ln_g"], g["c2_ln_b"], g["c2_b_dw"], g["c2_dw"]), got = c_mix_bwd(
                dh, z, h2, cdw, vec("c2_ln_g"), vec("c2_ln_b"), wpw2, ride=scatter("mixb2"))
            arrived("mixb2", got)
            (g["c2_w_pw1"],), got = in_proj_bwd_w(u, dz, ride=scatter("inw2"))
            arrived("inw2", got)
            (dh, g["ln1_2"], g["c2_b_pw1"]), got = in_proj_bwd_x(dz, full["c2_w_pw1"], hin, vec("ln1_2"), dh,
                                                                ride=scatter("inx2"))
            arrived("inx2", got)
    grad_x = dh[None]
    arrived("last", run_ride(scatter("last"), "scatter_last"))

    grad, delta, new_m, new_v = {}, {}, {}, {}
    two_d = lambda n, a: stored(n, a.reshape(-1, p[n].shape[-1]))

    def adam_calls(names_, tokens):
        groups, last = {}, None
        for n in names_:
            groups.setdefault((two_d(n, p[n]).shape, SHARDED[n][1]), []).append(n)
        for (shape, rb), members in groups.items():
            per_weight = 2 * rb * shape[1] * (7 * 4 + N_DEV * recv[members[0]].dtype.itemsize)
            at_once = max(1, (ADAM_VMEM // per_weight))
            for lo in range(0, len(members), at_once):
                ns = members[lo:lo + at_once]
                outs = adamw([two_d(n, p[n]) for n in ns], [two_d(n, p["m_" + n]) for n in ns],
                             [two_d(n, p["v_" + n]) for n in ns], [recv[n].reshape(N_DEV, *shape) for n in ns], rb, tokens)
                for res, o in zip((grad, delta, new_m, new_v), outs):
                    res.update({n: stored(n, a).reshape(p[n].shape) for n, a in zip(ns, o)})
                last = outs[0][0]
        return last

    early_done = adam_calls([n for n in SHARDED if n not in LATE_FFN + LATE_MIX], (late_ffn[-1], late_mix[-1]))
    outs = adamw_vectors([vec(n) for n in REPL], [vec("m_" + n) for n in REPL], [vec("v_" + n) for n in REPL], recv["repl"])
    for res, o in zip((grad, delta, new_m, new_v), outs):
        res.update({n: a.reshape(p[n].shape) for n, a in zip(REPL, o)})
    recv.update(zip(LATE_FFN, scatter_wait(late_ffn, early_done, "late_ffn_wait")))
    recv.update(zip(LATE_MIX, scatter_wait(late_mix, early_done, "late_mix_wait")))
    adam_calls(LATE_FFN + LATE_MIX, ())

    loss = lax.psum(loss_lanes[0, 0], ("x", "y", "c"))
    return (loss, grad_x, *[grad[n] for n in WEIGHTS], *[delta[n] for n in WEIGHTS],
            *[new_m[n] for n in WEIGHTS], *[new_v[n] for n in WEIGHTS])


def kernel(x, ln1_0, a0_w_in, a0_conv, a0_w_out, ln2_0, ffn0_w_gu, ffn0_w_down, ln1_1, b1_w_grp, b1_scale, ln2_1, ffn1_w_gu, ffn1_w_down, ln1_2, c2_w_pw1, c2_b_pw1, c2_dw, c2_b_dw, c2_ln_g, c2_ln_b, c2_w_pw2, c2_b_pw2, ln2_2, ffn2_w_gu, ffn2_w_down, ln1_3, a3_w_in, a3_conv, a3_w_out, ln2_3, ffn3_w_gu, ffn3_w_down, ln_f, loss_target, m_ln1_0, m_a0_w_in, m_a0_conv, m_a0_w_out, m_ln2_0, m_ffn0_w_gu, m_ffn0_w_down, m_ln1_1, m_b1_w_grp, m_b1_scale, m_ln2_1, m_ffn1_w_gu, m_ffn1_w_down, m_ln1_2, m_c2_w_pw1, m_c2_b_pw1, m_c2_dw, m_c2_b_dw, m_c2_ln_g, m_c2_ln_b, m_c2_w_pw2, m_c2_b_pw2, m_ln2_2, m_ffn2_w_gu, m_ffn2_w_down, m_ln1_3, m_a3_w_in, m_a3_conv, m_a3_w_out, m_ln2_3, m_ffn3_w_gu, m_ffn3_w_down, m_ln_f, v_ln1_0, v_a0_w_in, v_a0_conv, v_a0_w_out, v_ln2_0, v_ffn0_w_gu, v_ffn0_w_down, v_ln1_1, v_b1_w_grp, v_b1_scale, v_ln2_1, v_ffn1_w_gu, v_ffn1_w_down, v_ln1_2, v_c2_w_pw1, v_c2_b_pw1, v_c2_dw, v_c2_b_dw, v_c2_ln_g, v_c2_ln_b, v_c2_w_pw2, v_c2_b_pw2, v_ln2_2, v_ffn2_w_gu, v_ffn2_w_down, v_ln1_3, v_a3_w_in, v_a3_conv, v_a3_w_out, v_ln2_3, v_ffn3_w_gu, v_ffn3_w_down, v_ln_f):
    return _step(dict(locals()))
```

```python
import jax
import jax.numpy as jnp
from jax import lax
from jax.experimental import pallas as pl
from jax.experimental.pallas import tpu as pltpu

F32 = jnp.float32
BF16 = jnp.bfloat16

N_DEV = 8
D = 1024
FF = 2816
FC = FF // 4
RMS_EPS = 1e-6
LN_EPS = 1e-5
TM = 512
HALO = 32
VMEM_LIMIT = 60 * 1024 * 1024

NT = (((1,), (1,)), ((), ()))
TN = (((0,), (0,)), ((), ()))
MESH = pl.DeviceIdType.MESH
ANY = pl.BlockSpec(memory_space=pl.ANY)
N_PEERS = N_DEV - 1


def _dot(a, b):
    return jnp.dot(a, b, preferred_element_type=F32)


def _dot_nt(a, b):
    return lax.dot_general(a, b, NT, preferred_element_type=F32)


def _dot_tn(a, b):
    return lax.dot_general(a, b, TN, preferred_element_type=F32)


def _rms_fwd(x, gain):
    r = lax.rsqrt(jnp.mean(x * x, axis=-1, keepdims=True) + RMS_EPS)
    return x * r * gain


def _rms_bwd(x, gain, du):
    r = lax.rsqrt(jnp.mean(x * x, axis=-1, keepdims=True) + RMS_EPS)
    xhat = x * r
    dgain = jnp.sum(du * xhat, axis=0, keepdims=True)
    dxhat = du * gain
    dx = r * (dxhat - xhat * jnp.mean(dxhat * xhat, axis=-1, keepdims=True))
    return dx, dgain


def _dev_index(p):
    return 4 * p[0] + 2 * p[1] + p[2]


def _place():
    return lax.axis_index("x"), lax.axis_index("y"), lax.axis_index("c")


class Ride:
    def __init__(self, ins, out_shapes, start, finish):
        self.ins, self.out_shapes, self.start, self.finish = list(ins), list(out_shapes), start, finish
        n = len(self.ins)
        self.sems = [pltpu.SemaphoreType.DMA((n * N_PEERS,)), pltpu.SemaphoreType.DMA((n * N_PEERS,)),
                     pltpu.SemaphoreType.DMA((n,))]


def gather_ride(shards, kinds):
    n = len(shards)

    def setup(ins, outs, sems):
        send_sems, recv_sems, local_sems = sems
        x, y, c = _place()
        chips = [(1 - x, y), (x, 1 - y), (1 - x, 1 - y)]

        def copy(a, k, block, to, src=None):
            slot = _chunk(outs[a], kinds[a], _dev_index(block))
            return pltpu.make_async_remote_copy(
                src_ref=slot if src is None else src, dst_ref=slot,
                send_sem=send_sems.at[a * N_PEERS + k], recv_sem=recv_sems.at[a * N_PEERS + k],
                device_id=to, device_id_type=MESH)

        def mine(a):
            return pltpu.make_async_copy(ins[a], _chunk(outs[a], kinds[a], _dev_index((x, y, c))), local_sems.at[a])

        def first(a):
            return [copy(a, 0, (x, y, c), (x, y, 1 - c), src=ins[a])] + [
                copy(a, 1 + j, (x, y, c), (*chip, c), src=ins[a]) for j, chip in enumerate(chips)]

        return (x, y, c), chips, copy, mine, first

    def start(ins, outs, sems):
        _, _, _, mine, first = setup(ins, outs, sems)
        for a in range(n):
            mine(a).start()
            for cp in first(a):
                cp.start()

    def finish(ins, outs, sems):
        (x, y, c), chips, copy, mine, first = setup(ins, outs, sems)
        me, sibling = (x, y, c), (x, y, 1 - c)
        for a in range(n):
            for j, chip in enumerate(chips):
                copy(a, 1 + j, (*chip, c), me).wait_recv()
                copy(a, 4 + j, (*chip, c), sibling).start()
        for a in range(n):
            copy(a, 0, sibling, me).wait_recv()
            for j, chip in enumerate(chips):
                copy(a, 4 + j, (*chip, 1 - c), me).wait_recv()
        for a in range(n):
            for cp in first(a):
                cp.wait_send()
            for j, chip in enumerate(chips):
                copy(a, 4 + j, (*chip, c), sibling).wait_send()
        for a in range(n):
            mine(a).wait()

    shapes = [(N_DEV, *s.shape) if kind == "lead" else (s.shape[0], N_DEV * s.shape[1]) for s, kind in zip(shards, kinds)]
    return Ride(shards, [jax.ShapeDtypeStruct(shape, s.dtype) for shape, s in zip(shapes, shards)], start, finish)


def _chunk(ref, kind, j):
    if kind == "lead":
        return ref.at[j]
    if kind == "rows":
        r = ref.shape[0] // N_DEV
        return ref.at[pl.ds(j * r, r)]
    if kind == "mid":
        r = ref.shape[1] // N_DEV
        return ref.at[:, pl.ds(j * r, r), :]
    if kind == "cols":
        c = ref.shape[1] // N_DEV
        return ref.at[:, pl.ds(j * c, c)]
    return ref


def _chunk_shape(shape, kind):
    if kind == "lead":
        return tuple(shape[1:])
    if kind == "rows":
        return (shape[0] // N_DEV, *shape[1:])
    if kind == "mid":
        return (shape[0], shape[1] // N_DEV, shape[2])
    if kind == "cols":
        return (shape[0], shape[1] // N_DEV)
    return tuple(shape)


def scatter_ride(parts):
    n = len(parts)
    kinds = [k for _, k in parts]

    def setup(ins, outs, sems):
        send_sems, recv_sems, local_sems = sems
        x, y, c = _place()
        me = _dev_index((x, y, c))
        peers = []
        for k in range(1, N_DEV):
            kx, ky, kc = (k >> 2) & 1, (k >> 1) & 1, k & 1
            peers.append((1 - x if kx else x, 1 - y if ky else y, 1 - c if kc else c))

        def copy(a, k, peer):
            return pltpu.make_async_remote_copy(
                src_ref=_chunk(ins[a], kinds[a], _dev_index(peer)), dst_ref=outs[a].at[me],
                send_sem=send_sems.at[a * N_PEERS + k], recv_sem=recv_sems.at[a * N_PEERS + k],
                device_id=peer, device_id_type=MESH)

        def arrival(a, k, peer):
            slot = outs[a].at[_dev_index(peer)]
            return pltpu.make_async_remote_copy(
                src_ref=slot, dst_ref=slot,
                send_sem=send_sems.at[a * N_PEERS + k], recv_sem=recv_sems.at[a * N_PEERS + k],
                device_id=peer, device_id_type=MESH)

        def mine(a):
            return pltpu.make_async_copy(_chunk(ins[a], kinds[a], me), outs[a].at[me], local_sems.at[a])

        return peers, copy, arrival, mine

    def start(ins, outs, sems):
        peers, copy, _, mine = setup(ins, outs, sems)
        for a in range(n):
            mine(a).start()
            for k, peer in enumerate(peers):
                copy(a, k, peer).start()

    def finish(ins, outs, sems):
        peers, copy, arrival, mine = setup(ins, outs, sems)
        for a in range(n):
            for k, peer in enumerate(peers):
                arrival(a, k, peer).wait_recv()
        for a in range(n):
            for k, peer in enumerate(peers):
                copy(a, k, peer).wait_send()
            mine(a).wait()

    shapes = [jax.ShapeDtypeStruct((N_DEV, *_chunk_shape(arr.shape, kind)), arr.dtype) for arr, kind in parts]
    return Ride([arr for arr, _ in parts], shapes, start, finish)


HBM = pl.BlockSpec(memory_space=pltpu.HBM)
SEM = pl.BlockSpec(memory_space=pltpu.SEMAPHORE)
DATAFLOW = pltpu.SideEffectType.DATAFLOW_SIDE_EFFECTING
TOKEN = (8, 128)


def _scatter_copies(kinds, ins, lands, send_sems, recv_sems):
    x, y, c = _place()
    me = _dev_index((x, y, c))
    sends, arrivals = [], []
    for a, kind in enumerate(kinds):
        for k in range(1, N_DEV):
            kx, ky, kc = (k >> 2) & 1, (k >> 1) & 1, k & 1
            peer = (1 - x if kx else x, 1 - y if ky else y, 1 - c if kc else c)
            sem = a * N_PEERS + k - 1
            sends.append(pltpu.make_async_remote_copy(
                src_ref=_chunk(ins[a], kind, _dev_index(peer)), dst_ref=lands[a].at[me],
                send_sem=send_sems.at[sem], recv_sem=recv_sems.at[sem], device_id=peer, device_id_type=MESH))
            slot = lands[a].at[_dev_index(peer)]
            arrivals.append(pltpu.make_async_remote_copy(
                src_ref=slot, dst_ref=slot, send_sem=send_sems.at[sem], recv_sem=recv_sems.at[sem],
                device_id=peer, device_id_type=MESH))
    return me, sends, arrivals


def scatter_start(parts, name):
    n = len(parts)
    kinds = [k for _, k in parts]
    arrays = [pltpu.with_memory_space_constraint(a, pltpu.HBM) for a, _ in parts]
    zones = [pltpu.with_memory_space_constraint(lax.empty((N_DEV, *_chunk_shape(a.shape, k)), a.dtype), pltpu.HBM)
             for a, k in parts]

    def body(*refs):
        ins, lands = refs[:n], refs[n:2 * n]
        send_sems, recv_sems = refs[2 * n], refs[2 * n + 1]
        token = refs[4 * n + 2]
        _, sends, _ = _scatter_copies(kinds, ins, lands, send_sems, recv_sems)
        for cp in sends:
            cp.start()
        token[...] = jnp.zeros_like(token)

    outs = pl.pallas_call(
        body, name=name,
        out_shape=(pltpu.SemaphoreType.DMA((n * N_PEERS,)), pltpu.SemaphoreType.DMA((n * N_PEERS,)),
                   *[pltpu.HBM(a.shape, a.dtype) for a in arrays], *[pltpu.HBM(z.shape, z.dtype) for z in zones],
                   jax.ShapeDtypeStruct(TOKEN, F32)),
        in_specs=[HBM] * (2 * n),
        out_specs=(SEM, SEM, *[HBM] * (2 * n), pl.BlockSpec(memory_space=pltpu.VMEM)),
        input_output_aliases={i: 2 + i for i in range(2 * n)},
        compiler_params=pltpu.CompilerParams(has_side_effects=DATAFLOW),
    )(*arrays, *zones)
    return kinds, outs[0], outs[1], outs[2:2 + n], outs[2 + n:2 + 2 * n], outs[2 + 2 * n]


def scatter_wait(started, after, name):
    kinds, send_sems, recv_sems, arrays, zones, _ = started
    n = len(kinds)

    def body(*refs):
        ins, lands = refs[:n], refs[n:2 * n]
        local_sems = refs[-1]
        me, sends, arrivals = _scatter_copies(kinds, ins, lands, refs[2 * n], refs[2 * n + 1])
        own = [pltpu.make_async_copy(_chunk(ins[a], kinds[a], me), lands[a].at[me], local_sems.at[a]) for a in range(n)]
        for cp in own:
            cp.start()
        for cp in sends:
            cp.wait_send()
        for cp in arrivals:
            cp.wait_recv()
        for cp in own:
            cp.wait()

    outs = pl.pallas_call(
        body, name=name,
        out_shape=(*[pltpu.HBM(a.shape, a.dtype) for a in arrays], *[pltpu.HBM(z.shape, z.dtype) for z in zones]),
        in_specs=[HBM] * (2 * n) + [SEM, SEM, ANY],
        out_specs=[HBM] * (2 * n),
        input_output_aliases={i: i for i in range(2 * n)},
        scratch_shapes=[pltpu.SemaphoreType.DMA((n,))],
        compiler_params=pltpu.CompilerParams(has_side_effects=DATAFLOW),
    )(*arrays, *zones, send_sems, recv_sems, after)
    return outs[n:]


def run_ride(ride, name):
    n_in, n_out = len(ride.ins), len(ride.out_shapes)

    def body(*refs):
        ins, outs, sems = refs[:n_in], refs[n_in:n_in + n_out], refs[n_in + n_out:]
        ride.start(ins, outs, sems)
        ride.finish(ins, outs, sems)

    return pl.pallas_call(
        body, name=name, in_specs=[ANY] * n_in, out_specs=[ANY] * n_out, out_shape=ride.out_shapes,
        scratch_shapes=ride.sems,
    )(*ride.ins)


def _call(body, *, name, grid, in_specs, out_specs, out_shape, args, scratch_shapes=(), ride=None):
    params = pltpu.CompilerParams(dimension_semantics=("arbitrary",) * len(grid), vmem_limit_bytes=VMEM_LIMIT)
    if ride is None:
        outs = pl.pallas_call(body, name=name, grid=grid, in_specs=in_specs, out_specs=out_specs, out_shape=out_shape,
                              scratch_shapes=list(scratch_shapes), compiler_params=params)(*args)
        return outs, []
    n_in, n_out, n_scr = len(in_specs), len(out_specs), len(scratch_shapes)
    r_in, r_out = len(ride.ins), len(ride.out_shapes)

    def hosted(*refs):
        ins, refs = refs[:n_in], refs[n_in:]
        rins, refs = refs[:r_in], refs[r_in:]
        outs, refs = refs[:n_out], refs[n_out:]
        routs, refs = refs[:r_out], refs[r_out:]
        scratch, sems = refs[:n_scr], refs[n_scr:]
        step, n_steps = pl.program_id(0), grid[0]
        for d in range(1, len(grid)):
            step, n_steps = step * grid[d] + pl.program_id(d), n_steps * grid[d]

        @pl.when(step == 0)
        def _():
            ride.start(rins, routs, sems)

        body(*ins, *outs, *scratch)

        @pl.when(step == n_steps - 1)
        def _():
            ride.finish(rins, routs, sems)

    outs = pl.pallas_call(
        hosted, name=name + "_ride", grid=grid,
        in_specs=list(in_specs) + [ANY] * r_in, out_specs=list(out_specs) + [ANY] * r_out,
        out_shape=list(out_shape) + ride.out_shapes,
        scratch_shapes=list(scratch_shapes) + ride.sems, compiler_params=params,
    )(*args, *ride.ins)
    return outs[:n_out], outs[n_out:]


def ffn_fwd(h, gain, wgu, wd, ride=None):
    t = h.shape[0]

    def body(h_ref, g_ref, wgu_ref, wd_ref, hn_ref, z_ref, u_ref, acc):
        k = pl.program_id(1)

        @pl.when(k == 0)
        def _():
            u_ref[...] = _rms_fwd(h_ref[...], g_ref[...]).astype(BF16)
            acc[...] = jnp.zeros_like(acc)

        u = u_ref[...]
        g = _dot_nt(u, wgu_ref[0, 0])
        up = _dot_nt(u, wgu_ref[1, 0])
        z_ref[0, 0] = g.astype(BF16)
        z_ref[1, 0] = up.astype(BF16)
        a = g * jax.nn.sigmoid(g) * up
        acc[...] += _dot(a.astype(BF16), wd_ref[0])

        @pl.when(k == 3)
        def _():
            hn_ref[...] = h_ref[...] + acc[...]

    return _call(
        body, name="ffn_fwd", grid=(t // TM, 4), ride=ride,
        in_specs=[pl.BlockSpec((TM, D), lambda i, k: (i, 0)),
                  pl.BlockSpec((1, D), lambda i, k: (0, 0)),
                  pl.BlockSpec((2, 1, FC, D), lambda i, k: (0, k, 0, 0)),
                  pl.BlockSpec((1, FC, D), lambda i, k: (k, 0, 0))],
        out_specs=[pl.BlockSpec((TM, D), lambda i, k: (i, 0)),
                   pl.BlockSpec((2, 1, TM, FC), lambda i, k: (0, k, i, 0)),
                   pl.BlockSpec((TM, D), lambda i, k: (i, 0))],
        out_shape=[jax.ShapeDtypeStruct((t, D), F32),
                   jax.ShapeDtypeStruct((2, 4, t, FC), BF16),
                   jax.ShapeDtypeStruct((t, D), BF16)],
        scratch_shapes=[pltpu.VMEM((TM, D), F32)],
        args=(h, gain, wgu, wd))


def ffn_bwd_x(dh, h, gain, z, wgu, wd, ride=None):
    t = h.shape[0]

    def body(dh_ref, h_ref, g_ref, z_ref, wgu_ref, wd_ref, dhp_ref, dz_ref, a_ref, dgain_ref, dhb, du):
        i, k = pl.program_id(0), pl.program_id(1)

        @pl.when(k == 0)
        def _():
            dhb[...] = dh_ref[...].astype(BF16)
            du[...] = jnp.zeros_like(du)

        @pl.when((k == 0) & (i == 0))
        def _():
            dgain_ref[...] = jnp.zeros_like(dgain_ref)

        da = _dot_nt(dhb[...], wd_ref[0])
        g = z_ref[0, 0].astype(F32)
        up = z_ref[1, 0].astype(F32)
        sg = jax.nn.sigmoid(g)
        silu = g * sg
        a_ref[0] = (silu * up).astype(BF16)
        dg = (da * up * (sg * (1.0 + g * (1.0 - sg)))).astype(BF16)
        dup = (da * silu).astype(BF16)
        dz_ref[0, 0] = dg
        dz_ref[1, 0] = dup
        for n in range(2):
            cols = slice(n * (D // 2), (n + 1) * (D // 2))
            du[:, cols] += _dot(dg, wgu_ref[0, 0, :, cols]) + _dot(dup, wgu_ref[1, 0, :, cols])

        @pl.when(k == 3)
        def _():
            dx, dgain = _rms_bwd(h_ref[...], g_ref[...], du[...])
            dhp_ref[...] = dh_ref[...] + dx
            dgain_ref[...] += dgain

    return _call(
        body, name="ffn_bwd_x", grid=(t // TM, 4), ride=ride,
        in_specs=[pl.BlockSpec((TM, D), lambda i, k: (i, 0)),
                  pl.BlockSpec((TM, D), lambda i, k: (i, 0)),
                  pl.BlockSpec((1, D), lambda i, k: (0, 0)),
                  pl.BlockSpec((2, 1, TM, FC), lambda i, k: (0, k, i, 0)),
                  pl.BlockSpec((2, 1, FC, D), lambda i, k: (0, k, 0, 0)),
                  pl.BlockSpec((1, FC, D), lambda i, k: (k, 0, 0))],
        out_specs=[pl.BlockSpec((TM, D), lambda i, k: (i, 0)),
                   pl.BlockSpec((2, 1, TM, FC), lambda i, k: (0, k, i, 0)),
                   pl.BlockSpec((1, TM, FC), lambda i, k: (k, i, 0)),
                   pl.BlockSpec((1, D), lambda i, k: (0, 0)),
                   pl.BlockSpec((TM, D), lambda i, k: (i, 0))],
        out_shape=[jax.ShapeDtypeStruct((t, D), F32),
                   jax.ShapeDtypeStruct((2, 4, t, FC), BF16),
                   jax.ShapeDtypeStruct((4, t, FC), BF16),
                   jax.ShapeDtypeStruct((1, D), F32),
                   jax.ShapeDtypeStruct((t, D), BF16)],
        scratch_shapes=[pltpu.VMEM((TM, D), F32)],
        args=(dh, h, gain, z, wgu, wd))


TW = 2048


def ffn_bwd_w(u, dz, a, dhb, ride=None):
    t = u.shape[0]
    tw = min(TW, t)
    steps = t // tw

    def body(u_ref, dz_ref, a_ref, dh_ref, dwgu_ref, dwd_ref, acc_gu, acc_d):
        j = pl.program_id(1)

        @pl.when(j == 0)
        def _():
            acc_gu[...] = jnp.zeros_like(acc_gu)
            acc_d[...] = jnp.zeros_like(acc_d)

        ub = u_ref[...]
        acc_gu[0] += _dot_tn(dz_ref[0, 0], ub)
        acc_gu[1] += _dot_tn(dz_ref[1, 0], ub)
        acc_d[...] += _dot_tn(a_ref[0], dh_ref[...])

        @pl.when(j == steps - 1)
        def _():
            dwgu_ref[:, 0] = acc_gu[...].astype(BF16)
            dwd_ref[0] = acc_d[...].astype(BF16)

    return _call(
        body, name="ffn_bwd_w", grid=(4, steps), ride=ride,
        in_specs=[pl.BlockSpec((tw, D), lambda k, j: (j, 0)),
                  pl.BlockSpec((2, 1, tw, FC), lambda k, j: (0, k, j, 0)),
                  pl.BlockSpec((1, tw, FC), lambda k, j: (k, j, 0)),
                  pl.BlockSpec((tw, D), lambda k, j: (j, 0))],
        out_specs=[pl.BlockSpec((2, 1, FC, D), lambda k, j: (0, k, 0, 0)),
                   pl.BlockSpec((1, FC, D), lambda k, j: (k, 0, 0))],
        out_shape=[jax.ShapeDtypeStruct((2, 4, FC, D), BF16),
                   jax.ShapeDtypeStruct((4, FC, D), BF16)],
        scratch_shapes=[pltpu.VMEM((2, FC, D), F32), pltpu.VMEM((FC, D), F32)],
        args=(u, dz, a, dhb))


def _prev_halo(i, tile=TM):
    return jnp.maximum(i * (tile // HALO) - 1, 0)


def _next_halo(i, t, tile=TM):
    return jnp.minimum((i + 1) * (tile // HALO), t // HALO - 1)


def rms_matmul(h, gain, w, bias, ride=None):
    t = h.shape[0]
    n = w.shape[1]

    def body(h_ref, g_ref, w_ref, b_ref, z_ref, u_ref):
        u = _rms_fwd(h_ref[...], g_ref[...]).astype(BF16)
        u_ref[...] = u
        z_ref[...] = (_dot(u, w_ref[...]) + b_ref[...]).astype(BF16)

    return _call(
        body, name=f"rms_matmul_{n}", grid=(t // TM,), ride=ride,
        in_specs=[pl.BlockSpec((TM, D), lambda i: (i, 0)),
                  pl.BlockSpec((1, D), lambda i: (0, 0)),
                  pl.BlockSpec((D, n), lambda i: (0, 0)),
                  pl.BlockSpec((1, n), lambda i: (0, 0))],
        out_specs=[pl.BlockSpec((TM, n), lambda i: (i, 0)),
                   pl.BlockSpec((TM, D), lambda i: (i, 0))],
        out_shape=[jax.ShapeDtypeStruct((t, n), BF16),
                   jax.ShapeDtypeStruct((t, D), BF16)],
        args=(h, gain, w, bias))


def in_proj_bwd_x(dz, w, h, gain, dh, ride=None):
    t = h.shape[0]
    n = w.shape[1]

    def body(dz_ref, w_ref, h_ref, g_ref, dh_ref, dhp_ref, dgain_ref, dbias_ref):
        @pl.when(pl.program_id(0) == 0)
        def _():
            dgain_ref[...] = jnp.zeros_like(dgain_ref)
            dbias_ref[...] = jnp.zeros_like(dbias_ref)

        du = _dot_nt(dz_ref[...], w_ref[...])
        dx, dgain = _rms_bwd(h_ref[...], g_ref[...], du)
        dhp_ref[...] = dh_ref[...] + dx
        dgain_ref[...] += dgain
        dbias_ref[...] += jnp.sum(dz_ref[...].astype(F32), axis=0, keepdims=True)

    return _call(
        body, name=f"in_proj_bwd_x_{n}", grid=(t // TM,), ride=ride,
        in_specs=[pl.BlockSpec((TM, n), lambda i: (i, 0)),
                  pl.BlockSpec((D, n), lambda i: (0, 0)),
                  pl.BlockSpec((TM, D), lambda i: (i, 0)),
                  pl.BlockSpec((1, D), lambda i: (0, 0)),
                  pl.BlockSpec((TM, D), lambda i: (i, 0))],
        out_specs=[pl.BlockSpec((TM, D), lambda i: (i, 0)),
                   pl.BlockSpec((1, D), lambda i: (0, 0)),
                   pl.BlockSpec((1, n), lambda i: (0, 0))],
        out_shape=[jax.ShapeDtypeStruct((t, D), F32),
                   jax.ShapeDtypeStruct((1, D), F32),
                   jax.ShapeDtypeStruct((1, n), F32)],
        args=(dz, w, h, gain, dh))


def in_proj_bwd_w(u, dz, ride=None):
    t = u.shape[0]
    n = dz.shape[1]
    steps = t // TM

    def body(u_ref, dz_ref, dw_ref, acc):
        s = pl.program_id(0)

        @pl.when(s == 0)
        def _():
            acc[...] = jnp.zeros_like(acc)

        acc[...] += _dot_tn(u_ref[...], dz_ref[...])

        @pl.when(s == steps - 1)
        def _():
            dw_ref[...] = acc[...].astype(BF16)

    return _call(
        body, name=f"in_proj_bwd_w_{n}", grid=(steps,), ride=ride,
        in_specs=[pl.BlockSpec((TM, D), lambda s: (s, 0)),
                  pl.BlockSpec((TM, n), lambda s: (s, 0))],
        out_specs=[pl.BlockSpec((D, n), lambda s: (0, 0))],
        out_shape=[jax.ShapeDtypeStruct((D, n), BF16)],
        scratch_shapes=[pltpu.VMEM((D, n), F32)],
        args=(u, dz))


A_TAPS = 3


def a_mix_fwd(z, h, conv, wout, ride=None):
    t = h.shape[0]

    def body(z_ref, zp_ref, h_ref, cw_ref, wo_ref, hn_ref, pad):
        i = pl.program_id(0)
        ph = zp_ref[:, D:2 * D].astype(F32) * zp_ref[:, 2 * D:].astype(F32)
        pad[0:HALO, :] = jnp.where(i == 0, 0.0, ph)
        pad[HALO:, :] = z_ref[:, D:2 * D].astype(F32) * z_ref[:, 2 * D:].astype(F32)
        q = jnp.zeros((TM, D), F32)
        for k in range(A_TAPS):
            off = HALO - (A_TAPS - 1) + k
            q += cw_ref[k:k + 1, :] * pad[off:off + TM, :]
        r = z_ref[:, 0:D].astype(F32) * q
        hn_ref[...] = h_ref[...] + _dot(r.astype(BF16), wo_ref[...])

    return _call(
        body, name="a_mix_fwd", grid=(t // TM,), ride=ride,
        in_specs=[pl.BlockSpec((TM, 3 * D), lambda i: (i, 0)),
                  pl.BlockSpec((HALO, 3 * D), lambda i: (_prev_halo(i), 0)),
                  pl.BlockSpec((TM, D), lambda i: (i, 0)),
                  pl.BlockSpec((A_TAPS, D), lambda i: (0, 0)),
                  pl.BlockSpec((D, D), lambda i: (0, 0))],
        out_specs=[pl.BlockSpec((TM, D), lambda i: (i, 0))],
        out_shape=[jax.ShapeDtypeStruct((t, D), F32)],
        scratch_shapes=[pltpu.VMEM((HALO + TM, D), F32)],
        args=(z, z, h, conv, wout))


def a_mix_bwd(dh, z, conv, wout, ride=None):
    t = dh.shape[0]
    steps = t // TM

    def body(dh_ref, dhn_ref, z_ref, zp_ref, zn_ref, cw_ref, wo_ref, dz_ref, dwo_ref, dcw_ref, pad, dqpad, dwo):
        i = pl.program_id(0)
        last = i == steps - 1

        @pl.when(i == 0)
        def _():
            dwo[...] = jnp.zeros_like(dwo)
            dcw_ref[...] = jnp.zeros_like(dcw_ref)

        ph = zp_ref[:, D:2 * D].astype(F32) * zp_ref[:, 2 * D:].astype(F32)
        pad[0:HALO, :] = jnp.where(i == 0, 0.0, ph)
        c = z_ref[:, D:2 * D].astype(F32)
        v = z_ref[:, 2 * D:].astype(F32)
        pad[HALO:, :] = c * v
        q = jnp.zeros((TM, D), F32)
        for k in range(A_TAPS):
            off = HALO - (A_TAPS - 1) + k
            q += cw_ref[k:k + 1, :] * pad[off:off + TM, :]
        b = z_ref[:, 0:D].astype(F32)
        dhb = dh_ref[...].astype(BF16)
        dwo[...] += _dot_tn((b * q).astype(BF16), dhb)
        dr = _dot_nt(dhb, wo_ref[...])
        dz_ref[:, 0:D] = (dr * q).astype(BF16)
        dq = dr * b
        drn = _dot_nt(dhn_ref[...].astype(BF16), wo_ref[...])
        dqpad[0:TM, :] = dq
        dqpad[TM:, :] = jnp.where(last, 0.0, drn * zn_ref[:, 0:D].astype(F32))
        dp = jnp.zeros((TM, D), F32)
        for k in range(A_TAPS):
            off = A_TAPS - 1 - k
            dp += cw_ref[k:k + 1, :] * dqpad[off:off + TM, :]
            poff = HALO - (A_TAPS - 1) + k
            dcw_ref[k:k + 1, :] += jnp.sum(dq * pad[poff:poff + TM, :], axis=0, keepdims=True)
        dz_ref[:, D:2 * D] = (dp * v).astype(BF16)
        dz_ref[:, 2 * D:] = (dp * c).astype(BF16)

        @pl.when(last)
        def _():
            dwo_ref[...] = dwo[...].astype(BF16)

    return _call(
        body, name="a_mix_bwd", grid=(steps,), ride=ride,
        in_specs=[pl.BlockSpec((TM, D), lambda i: (i, 0)),
                  pl.BlockSpec((HALO, D), lambda i: (_next_halo(i, t), 0)),
                  pl.BlockSpec((TM, 3 * D), lambda i: (i, 0)),
                  pl.BlockSpec((HALO, 3 * D), lambda i: (_prev_halo(i), 0)),
                  pl.BlockSpec((HALO, 3 * D), lambda i: (_next_halo(i, t), 0)),
                  pl.BlockSpec((A_TAPS, D), lambda i: (0, 0)),
                  pl.BlockSpec((D, D), lambda i: (0, 0))],
        out_specs=[pl.BlockSpec((TM, 3 * D), lambda i: (i, 0)),
                   pl.BlockSpec((D, D), lambda i: (0, 0)),
                   pl.BlockSpec((A_TAPS, D), lambda i: (0, 0))],
        out_shape=[jax.ShapeDtypeStruct((t, 3 * D), BF16),
                   jax.ShapeDtypeStruct((D, D), BF16),
                   jax.ShapeDtypeStruct((A_TAPS, D), F32)],
        scratch_shapes=[pltpu.VMEM((HALO + TM, D), F32), pltpu.VMEM((TM + HALO, D), F32), pltpu.VMEM((D, D), F32)],
        args=(dh, dh, z, z, z, conv, wout))


C_TAPS = 31


def _glu(zr):
    return zr[:, 0:D].astype(F32) * jax.nn.sigmoid(zr[:, D:].astype(F32))


def _ln_silu(h2, lg, lb):
    mu = jnp.mean(h2, axis=-1, keepdims=True)
    xc = h2 - mu
    rstd = lax.rsqrt(jnp.mean(xc * xc, axis=-1, keepdims=True) + LN_EPS)
    xn = xc * rstd
    h3 = xn * lg + lb
    s3 = jax.nn.sigmoid(h3)
    return xn, rstd, h3, s3


def _ln_silu_bwd(h2, lg, lb, dh4):
    xn, rstd, h3, s3 = _ln_silu(h2, lg, lb)
    dh3 = dh4 * (s3 * (1.0 + h3 * (1.0 - s3)))
    dxn = dh3 * lg
    dh2 = rstd * (dxn - jnp.mean(dxn, axis=-1, keepdims=True) - xn * jnp.mean(dxn * xn, axis=-1, keepdims=True))
    return dh2, dh3, xn, h3 * s3


TC = 256
RB = 64
LANES = 128
SHIFTS = 7


def _shifted_copies(src, sh, rows):
    for b in range(1, SHIFTS + 1):
        sh[b - 1, 0:rows, :] = src[b:b + rows, :]


def _window(src, sh, o, r0, lanes):
    a, b = divmod(o, 8)
    ref = src if b == 0 else sh.at[b - 1]
    return ref[8 * a + r0:8 * a + r0 + RB, lanes]


def c_mix_fwd(z, h, dw, bdw, lg, lb, w2, b2, ride=None):
    t = h.shape[0]

    def body(z_ref, zp_ref, h_ref, dw_ref, bdw_ref, lg_ref, lb_ref, w2_ref, b2_ref, hn_ref, h2_ref, pad, sh):
        i = pl.program_id(0)
        pad[0:HALO, :] = jnp.where(i == 0, 0.0, _glu(zp_ref))
        pad[HALO:, :] = _glu(z_ref)
        _shifted_copies(pad, sh, TC + 24)
        for l in range(D // LANES):
            lanes = slice(l * LANES, (l + 1) * LANES)
            for r0 in range(0, TC, RB):
                acc = jnp.zeros((RB, LANES), F32) + bdw_ref[:, lanes]
                for k in range(C_TAPS):
                    acc += dw_ref[k:k + 1, lanes] * _window(pad, sh, HALO - (C_TAPS - 1) + k, r0, lanes)
                h2_ref[r0:r0 + RB, lanes] = acc
        _, _, h3, s3 = _ln_silu(h2_ref[...], lg_ref[...], lb_ref[...])
        hn_ref[...] = h_ref[...] + _dot((h3 * s3).astype(BF16), w2_ref[...]) + b2_ref[...]

    vec = pl.BlockSpec((1, D), lambda i: (0, 0))
    return _call(
        body, name="c_mix_fwd", grid=(t // TC,), ride=ride,
        in_specs=[pl.BlockSpec((TC, 2 * D), lambda i: (i, 0)),
                  pl.BlockSpec((HALO, 2 * D), lambda i: (_prev_halo(i, TC), 0)),
                  pl.BlockSpec((TC, D), lambda i: (i, 0)),
                  pl.BlockSpec((C_TAPS, D), lambda i: (0, 0)),
                  vec, vec, vec,
                  pl.BlockSpec((D, D), lambda i: (0, 0)),
                  vec],
        out_specs=[pl.BlockSpec((TC, D), lambda i: (i, 0)),
                   pl.BlockSpec((TC, D), lambda i: (i, 0))],
        out_shape=[jax.ShapeDtypeStruct((t, D), F32),
                   jax.ShapeDtypeStruct((t, D), F32)],
        scratch_shapes=[pltpu.VMEM((HALO + TC, D), F32), pltpu.VMEM((SHIFTS, TC + 24, D), F32)],
        args=(z, z, h, dw, bdw, lg, lb, w2, b2))


def c_mix_bwd(dh, z, h2, dw, lg, lb, w2, ride=None):
    t = dh.shape[0]
    steps = t // TC

    def body(dh_ref, dhn_ref, z_ref, zp_ref, h2_ref, h2n_ref, dw_ref, lg_ref, lb_ref, w2_ref,
             dz_ref, dw2_ref, db2_ref, dlg_ref, dlb_ref, dbdw_ref, ddw_ref, pad, dpad, dw2, sh, dh1):
        i = pl.program_id(0)
        last = i == steps - 1

        @pl.when(i == 0)
        def _():
            for r in (dw2, db2_ref, dlg_ref, dlb_ref, dbdw_ref, ddw_ref):
                r[...] = jnp.zeros_like(r)

        lg, lb = lg_ref[...], lb_ref[...]
        dh = dh_ref[...]
        dhb = dh.astype(BF16)
        dh2, dh3, xn, h4 = _ln_silu_bwd(h2_ref[...], lg, lb, _dot_nt(dhb, w2_ref[...]))
        dw2[...] += _dot_tn(h4.astype(BF16), dhb)
        db2_ref[...] += jnp.sum(dh, axis=0, keepdims=True)
        dlg_ref[...] += jnp.sum(dh3 * xn, axis=0, keepdims=True)
        dlb_ref[...] += jnp.sum(dh3, axis=0, keepdims=True)
        dbdw_ref[...] += jnp.sum(dh2, axis=0, keepdims=True)
        dh2n, _, _, _ = _ln_silu_bwd(h2n_ref[...], lg, lb, _dot_nt(dhn_ref[...].astype(BF16), w2_ref[...]))
        dpad[0:TC, :] = dh2
        dpad[TC:, :] = jnp.where(last, 0.0, dh2n)
        _shifted_copies(dpad, sh, TC + 24)
        for l in range(D // LANES):
            lanes = slice(l * LANES, (l + 1) * LANES)
            for r0 in range(0, TC, RB):
                acc = jnp.zeros((RB, LANES), F32)
                for k in range(C_TAPS):
                    acc += dw_ref[k:k + 1, lanes] * _window(dpad, sh, C_TAPS - 1 - k, r0, lanes)
                dh1[r0:r0 + RB, lanes] = acc
        pad[0:HALO, :] = jnp.where(i == 0, 0.0, _glu(zp_ref))
        pad[HALO:, :] = _glu(z_ref)
        _shifted_copies(pad, sh, TC + 24)
        for l in range(D // LANES):
            lanes = slice(l * LANES, (l + 1) * LANES)
            accs = [jnp.zeros((8, LANES), F32) for _ in range(C_TAPS)]
            for r0 in range(0, TC, RB):
                d = dpad[r0:r0 + RB, lanes]
                for k in range(C_TAPS):
                    prod = d * _window(pad, sh, HALO - (C_TAPS - 1) + k, r0, lanes)
                    accs[k] += jnp.sum(prod.reshape(RB // 8, 8, LANES), axis=0)
            for k in range(C_TAPS):
                ddw_ref[k:k + 1, lanes] += jnp.sum(accs[k], axis=0, keepdims=True)
        a = z_ref[:, 0:D].astype(F32)
        sg = jax.nn.sigmoid(z_ref[:, D:].astype(F32))
        d1 = dh1[...]
        dz_ref[:, 0:D] = (d1 * sg).astype(BF16)
        dz_ref[:, D:] = (d1 * a * sg * (1.0 - sg)).astype(BF16)

        @pl.when(last)
        def _():
            dw2_ref[...] = dw2[...].astype(BF16)

    vec = pl.BlockSpec((1, D), lambda i: (0, 0))
    return _call(
        body, name="c_mix_bwd", grid=(steps,), ride=ride,
        in_specs=[pl.BlockSpec((TC, D), lambda i: (i, 0)),
                  pl.BlockSpec((HALO, D), lambda i: (_next_halo(i, t, TC), 0)),
                  pl.BlockSpec((TC, 2 * D), lambda i: (i, 0)),
                  pl.BlockSpec((HALO, 2 * D), lambda i: (_prev_halo(i, TC), 0)),
                  pl.BlockSpec((TC, D), lambda i: (i, 0)),
                  pl.BlockSpec((HALO, D), lambda i: (_next_halo(i, t, TC), 0)),
                  pl.BlockSpec((C_TAPS, D), lambda i: (0, 0)),
                  vec, vec,
                  pl.BlockSpec((D, D), lambda i: (0, 0))],
        out_specs=[pl.BlockSpec((TC, 2 * D), lambda i: (i, 0)),
                   pl.BlockSpec((D, D), lambda i: (0, 0)),
                   vec, vec, vec, vec,
                   pl.BlockSpec((C_TAPS, D), lambda i: (0, 0))],
        out_shape=[jax.ShapeDtypeStruct((t, 2 * D), BF16),
                   jax.ShapeDtypeStruct((D, D), BF16)]
                  + [jax.ShapeDtypeStruct((1, D), F32)] * 4
                  + [jax.ShapeDtypeStruct((C_TAPS, D), F32)],
        scratch_shapes=[pltpu.VMEM((HALO + TC, D), F32), pltpu.VMEM((TC + HALO, D), F32), pltpu.VMEM((D, D), F32),
                        pltpu.VMEM((SHIFTS, TC + 24, D), F32), pltpu.VMEM((TC, D), F32)],
        args=(dh, dh, z, z, h2, h2, dw, lg, lb, w2))


POOL_WINDOWS = (2, 4, 8, 16)
GW = D // len(POOL_WINDOWS)


def _pool_mixed(pad, g, w, inv_cnt):
    cols = slice(g * GW, (g + 1) * GW)
    s = pad[HALO:HALO + TM, cols]
    u = s
    for j in range(1, w):
        s = s + pad[HALO - j:HALO - j + TM, cols]
    return s * inv_cnt - u


def _inv_cnt(i, w):
    row = i * TM + lax.broadcasted_iota(jnp.int32, (TM, 1), 0)
    return 1.0 / jnp.minimum(row + 1, w).astype(F32)


def b_mix_fwd(h, gain, wg, scale, ride=None):
    t = h.shape[0]

    def body(h_ref, hp_ref, g_ref, wg_ref, sc_ref, hn_ref, pad):
        i = pl.program_id(0)
        gain = g_ref[...]
        pad[0:HALO, :] = jnp.where(i == 0, 0.0, _rms_fwd(hp_ref[...], gain))
        pad[HALO:, :] = _rms_fwd(h_ref[...], gain)
        for g, w in enumerate(POOL_WINDOWS):
            cols = slice(g * GW, (g + 1) * GW)
            mixed = _pool_mixed(pad, g, w, _inv_cnt(i, w))
            y = _dot(mixed.astype(BF16), wg_ref[g])
            hn_ref[:, cols] = h_ref[:, cols] + y * sc_ref[:, cols]

    return _call(
        body, name="b_mix_fwd", grid=(t // TM,), ride=ride,
        in_specs=[pl.BlockSpec((TM, D), lambda i: (i, 0)),
                  pl.BlockSpec((HALO, D), lambda i: (_prev_halo(i), 0)),
                  pl.BlockSpec((1, D), lambda i: (0, 0)),
                  pl.BlockSpec((4, GW, GW), lambda i: (0, 0, 0)),
                  pl.BlockSpec((1, D), lambda i: (0, 0))],
        out_specs=[pl.BlockSpec((TM, D), lambda i: (i, 0))],
        out_shape=[jax.ShapeDtypeStruct((t, D), F32)],
        scratch_shapes=[pltpu.VMEM((HALO + TM, D), F32)],
        args=(h, h, gain, wg, scale))


def b_mix_bwd(dh, h, gain, wg, scale, ride=None):
    t = h.shape[0]
    steps = t // TM

    def body(dh_ref, dhn_ref, h_ref, hp_ref, g_ref, wg_ref, sc_ref, dhp_ref, dgain_ref, dwg_ref, dsc_ref, pad, dpad, du):
        i = pl.program_id(0)
        last = i == steps - 1

        @pl.when(i == 0)
        def _():
            for r in (dgain_ref, dwg_ref, dsc_ref):
                r[...] = jnp.zeros_like(r)

        gain = g_ref[...]
        pad[0:HALO, :] = jnp.where(i == 0, 0.0, _rms_fwd(hp_ref[...], gain))
        pad[HALO:, :] = _rms_fwd(h_ref[...], gain)
        for g, w in enumerate(POOL_WINDOWS):
            cols = slice(g * GW, (g + 1) * GW)
            inv_cnt = _inv_cnt(i, w)
            mixed = _pool_mixed(pad, g, w, inv_cnt).astype(BF16)
            dh = dh_ref[:, cols]
            dsc_ref[:, cols] += jnp.sum(dh * _dot(mixed, wg_ref[g]), axis=0, keepdims=True)
            dy = (dh * sc_ref[:, cols]).astype(BF16)
            dwg_ref[g] += _dot_tn(mixed, dy)
            dm = _dot_nt(dy, wg_ref[g])
            dmn = _dot_nt((dhn_ref[:, cols] * sc_ref[:, cols]).astype(BF16), wg_ref[g])
            dpad[0:TM, cols] = dm * inv_cnt
            dpad[TM:, cols] = jnp.where(last, 0.0, dmn * (1.0 / w))
            s = dpad[0:TM, cols]
            for j in range(1, w):
                s = s + dpad[j:j + TM, cols]
            du[:, cols] = s - dm
        dx, dgain = _rms_bwd(h_ref[...], gain, du[...])
        dhp_ref[...] = dh_ref[...] + dx
        dgain_ref[...] += dgain

    return _call(
        body, name="b_mix_bwd", grid=(steps,), ride=ride,
        in_specs=[pl.BlockSpec((TM, D), lambda i: (i, 0)),
                  pl.BlockSpec((HALO, D), lambda i: (_next_halo(i, t), 0)),
                  pl.BlockSpec((TM, D), lambda i: (i, 0)),
                  pl.BlockSpec((HALO, D), lambda i: (_prev_halo(i), 0)),
                  pl.BlockSpec((1, D), lambda i: (0, 0)),
                  pl.BlockSpec((4, GW, GW), lambda i: (0, 0, 0)),
                  pl.BlockSpec((1, D), lambda i: (0, 0))],
        out_specs=[pl.BlockSpec((TM, D), lambda i: (i, 0)),
                   pl.BlockSpec((1, D), lambda i: (0, 0)),
                   pl.BlockSpec((4, GW, GW), lambda i: (0, 0, 0)),
                   pl.BlockSpec((1, D), lambda i: (0, 0))],
        out_shape=[jax.ShapeDtypeStruct((t, D), F32),
                   jax.ShapeDtypeStruct((1, D), F32),
                   jax.ShapeDtypeStruct((4, GW, GW), F32),
                   jax.ShapeDtypeStruct((1, D), F32)],
        scratch_shapes=[pltpu.VMEM((HALO + TM, D), F32), pltpu.VMEM((TM + HALO, D), F32), pltpu.VMEM((TM, D), F32)],
        args=(dh, dh, h, h, gain, wg, scale))


LOSS_LANES = 128


def loss_head(h, gain, target):
    t = h.shape[0]

    def body(h_ref, g_ref, tg_ref, loss_ref, dh_ref, dgain_ref):
        @pl.when(pl.program_id(0) == 0)
        def _():
            loss_ref[...] = jnp.zeros_like(loss_ref)
            dgain_ref[...] = jnp.zeros_like(dgain_ref)

        x, gain = h_ref[...], g_ref[...]
        err = _rms_fwd(x, gain) - tg_ref[...]
        per_row = jnp.mean(err * err, axis=-1, keepdims=True)
        loss_ref[...] += jnp.broadcast_to(0.5 * jnp.sum(per_row, axis=0, keepdims=True), (1, LOSS_LANES))
        dx, dgain = _rms_bwd(x, gain, err * (1.0 / D))
        dh_ref[...] = dx
        dgain_ref[...] += dgain

    outs, _ = _call(
        body, name="loss_head", grid=(t // TM,),
        in_specs=[pl.BlockSpec((TM, D), lambda i: (i, 0)),
                  pl.BlockSpec((1, D), lambda i: (0, 0)),
                  pl.BlockSpec((TM, D), lambda i: (i, 0))],
        out_specs=[pl.BlockSpec((1, LOSS_LANES), lambda i: (0, 0)),
                   pl.BlockSpec((TM, D), lambda i: (i, 0)),
                   pl.BlockSpec((1, D), lambda i: (0, 0))],
        out_shape=[jax.ShapeDtypeStruct((1, LOSS_LANES), F32),
                   jax.ShapeDtypeStruct((t, D), F32),
                   jax.ShapeDtypeStruct((1, D), F32)],
        args=(h, gain, target))
    return outs


ADAM_LR = 0.001
ADAM_B1 = 0.9
ADAM_B2 = 0.999
ADAM_EPS = 1e-08
ADAM_WD = 0.01
ADAM_STEP = 10
ADAM_VMEM = 40 * 1024 * 1024


def cast_all(arrays):
    def body(*refs):
        for src, dst in zip(refs[:len(arrays)], refs[len(arrays):]):
            dst[...] = src[...].astype(BF16)

    return pl.pallas_call(
        body, name="cast_all", out_shape=[jax.ShapeDtypeStruct(a.shape, BF16) for a in arrays],
        compiler_params=pltpu.CompilerParams(vmem_limit_bytes=VMEM_LIMIT),
    )(*arrays)


def _adam_math(w, m, v, g):
    m = ADAM_B1 * m + (1.0 - ADAM_B1) * g
    v = ADAM_B2 * v + (1.0 - ADAM_B2) * (g * g)
    m_hat = m / (1.0 - ADAM_B1 ** ADAM_STEP)
    v_hat = v / (1.0 - ADAM_B2 ** ADAM_STEP)
    return -ADAM_LR * (m_hat / (jnp.sqrt(v_hat) + ADAM_EPS) + ADAM_WD * w), m, v


def adamw(ws, ms, vs, gps, rb, tokens=()):
    n = len(ws)
    r, c = ws[0].shape
    nb = r // rb

    def body(*refs):
        i = pl.program_id(0)
        outs = refs[4 * n + len(tokens):]
        for j in range(n):
            w_ref, m_ref, v_ref, gp_ref = (refs[q * n + j] for q in range(4))
            g_ref, d_ref, nm_ref, nv_ref = (outs[q * n + j] for q in range(4))

            @pl.when(i // nb == j)
            def _():
                g = gp_ref[0].astype(F32)
                for s in range(1, N_DEV):
                    g = g + gp_ref[s].astype(F32)
                g_ref[...] = g
                d_ref[...], nm_ref[...], nv_ref[...] = _adam_math(w_ref[...], m_ref[...], v_ref[...], g)

    def blk(j):
        return pl.BlockSpec((rb, c), lambda i: (jnp.clip(i - j * nb, 0, nb - 1), 0))

    def gblk(j):
        return pl.BlockSpec((N_DEV, rb, c), lambda i: (0, jnp.clip(i - j * nb, 0, nb - 1), 0))

    outs, _ = _call(
        body, name=f"adamw_{n}x{r}x{c}", grid=(n * nb,),
        in_specs=[blk(j) for _ in range(3) for j in range(n)] + [gblk(j) for j in range(n)] + [ANY] * len(tokens),
        out_specs=[blk(j) for _ in range(4) for j in range(n)],
        out_shape=[jax.ShapeDtypeStruct((r, c), F32)] * (4 * n),
        args=(*ws, *ms, *vs, *gps, *tokens))
    return outs[:n], outs[n:2 * n], outs[2 * n:3 * n], outs[3 * n:]


def adamw_vectors(ws, ms, vs, gparts):
    nv = len(ws)

    def body(*refs):
        w_refs, m_refs, v_refs = refs[:nv], refs[nv:2 * nv], refs[2 * nv:3 * nv]
        gp_ref = refs[3 * nv]
        outs = refs[3 * nv + 1:]
        g_refs, d_refs, nm_refs, nv_refs = outs[:nv], outs[nv:2 * nv], outs[2 * nv:3 * nv], outs[3 * nv:]
        row = 0
        for i in range(nv):
            for part in range(w_refs[i].shape[1] // D):
                cols = slice(part * D, (part + 1) * D)
                g = gp_ref[0, row:row + 1, :]
                for s in range(1, N_DEV):
                    g = g + gp_ref[s, row:row + 1, :]
                g_refs[i][:, cols] = g
                d_refs[i][:, cols], nm_refs[i][:, cols], nv_refs[i][:, cols] = _adam_math(
                    w_refs[i][:, cols], m_refs[i][:, cols], v_refs[i][:, cols], g)
                row += 1

    shapes = [jax.ShapeDtypeStruct(w.shape, F32) for w in ws]
    outs = pl.pallas_call(body, name="adamw_vectors", out_shape=shapes * 4)(*ws, *ms, *vs, gparts)
    return outs[:nv], outs[nv:2 * nv], outs[2 * nv:3 * nv], outs[3 * nv:]


WEIGHTS = ["ln1_0", "a0_w_in", "a0_conv", "a0_w_out", "ln2_0", "ffn0_w_gu", "ffn0_w_down",
           "ln1_1", "b1_w_grp", "b1_scale", "ln2_1", "ffn1_w_gu", "ffn1_w_down",
           "ln1_2", "c2_w_pw1", "c2_b_pw1", "c2_dw", "c2_b_dw", "c2_ln_g", "c2_ln_b", "c2_w_pw2", "c2_b_pw2",
           "ln2_2", "ffn2_w_gu", "ffn2_w_down",
           "ln1_3", "a3_w_in", "a3_conv", "a3_w_out", "ln2_3", "ffn3_w_gu", "ffn3_w_down", "ln_f"]
SHARDED = {"a0_w_in": ("cols", 256), "a0_conv": ("cols", A_TAPS), "a0_w_out": ("rows", 128),
           "ffn0_w_gu": ("lead", 176), "ffn0_w_down": ("rows", 176),
           "b1_w_grp": ("mid", 128),
           "ffn1_w_gu": ("lead", 176), "ffn1_w_down": ("rows", 176),
           "c2_w_pw1": ("cols", 256), "c2_dw": ("cols", C_TAPS), "c2_w_pw2": ("rows", 128),
           "ffn2_w_gu": ("lead", 176), "ffn2_w_down": ("rows", 176),
           "a3_w_in": ("cols", 256), "a3_conv": ("cols", A_TAPS), "a3_w_out": ("rows", 128),
           "ffn3_w_gu": ("lead", 176), "ffn3_w_down": ("rows", 176)}
IN_PROJ = ("a0_w_in", "c2_w_pw1", "a3_w_in")
REPL = [n for n in WEIGHTS if n not in SHARDED]
REPL_ROWS = 16
GATHER_PLAN = {"first": ["a0_w_in", "a0_w_out", "a0_conv"],
               "in0": ["ffn0_w_gu"], "mix0": ["ffn0_w_down"],
               "ffn0": ["b1_w_grp", "ffn1_w_gu", "ffn1_w_down"],
               "ffn1": ["c2_w_pw1", "c2_w_pw2", "c2_dw", "ffn2_w_gu"],
               "in2": ["ffn2_w_down"],
               "mix2": ["a3_w_in", "a3_w_out", "a3_conv"],
               "ffn2": ["ffn3_w_gu", "ffn3_w_down"]}
SCATTER_PLAN = {"mixb3": ["ffn3_w_down"], "inw3": ["a3_w_out", "a3_conv"],
                "ffnx2": ["ffn3_w_gu"], "ffnw2": ["a3_w_in"],
                "mixb2": ["ffn2_w_gu", "ffn2_w_down"], "inw2": ["c2_w_pw2", "c2_dw"],
                "ffnx1": ["c2_w_pw1"],
                "ffnx0": ["ffn1_w_gu"], "ffnw0": ["ffn1_w_down", "b1_w_grp"],
                "last": ["repl"]}
LATE_FFN = ["ffn0_w_gu", "ffn0_w_down"]
LATE_MIX = ["a0_w_in", "a0_w_out", "a0_conv"]


def _step(p):
    vec = lambda n: p[n].reshape(1, -1)
    x, target = p["x"][0], p["loss_target"][0]

    names = list(SHARDED)
    stored = lambda n, a: a.T if n.endswith("w_gu") else a
    shard = dict(zip(names, cast_all([stored(n, p[n]) for n in names])))
    full = {}

    def gather(slot):
        names = GATHER_PLAN[slot]
        return gather_ride([shard[n] for n in names], ["cols" if n in IN_PROJ else "lead" for n in names])

    def landed(slot, outs):
        full.update(zip(GATHER_PLAN[slot], outs))

    def conv_full(n):
        k = full[n].shape[1]
        return full[n].transpose(1, 0, 2).reshape(k, D).astype(F32)

    def wgu(i):
        return full[f"ffn{i}_w_gu"].reshape(2, 4, FC, D)

    def wd(i):
        return full[f"ffn{i}_w_down"].reshape(4, FC, D)

    landed("first", run_ride(gather("first"), "gather_first"))
    no_bias = jnp.zeros((1, 3 * D), F32)
    h = [x]
    saved = {}
    conv, wout = {}, {}

    (z, u), got = rms_matmul(h[-1], vec("ln1_0"), full["a0_w_in"], no_bias, ride=gather("in0"))
    landed("in0", got)
    conv[0], wout[0] = conv_full("a0_conv"), full["a0_w_out"].reshape(D, D)
    (hm,), got = a_mix_fwd(z, h[-1], conv[0], wout[0], ride=gather("mix0"))
    landed("mix0", got)
    saved["mix0"] = (z, u)
    h.append(hm)
    (hn, zf, uf), got = ffn_fwd(hm, vec("ln2_0"), wgu(0), wd(0), ride=gather("ffn0"))
    landed("ffn0", got)
    saved["ffn0"] = (zf, uf)
    h.append(hn)

    wgrp = full["b1_w_grp"].transpose(1, 0, 2, 3).reshape(4, GW, GW)
    (hm,), _ = b_mix_fwd(h[-1], vec("ln1_1"), wgrp, vec("b1_scale"))
    h.append(hm)
    (hn, zf, uf), got = ffn_fwd(hm, vec("ln2_1"), wgu(1), wd(1), ride=gather("ffn1"))
    landed("ffn1", got)
    saved["ffn1"] = (zf, uf)
    h.append(hn)

    (z, u), got = rms_matmul(h[-1], vec("ln1_2"), full["c2_w_pw1"], vec("c2_b_pw1"), ride=gather("in2"))
    landed("in2", got)
    cdw, wpw2 = conv_full("c2_dw"), full["c2_w_pw2"].reshape(D, D)
    (hm, h2), got = c_mix_fwd(z, h[-1], cdw, vec("c2_b_dw"), vec("c2_ln_g"), vec("c2_ln_b"), wpw2, vec("c2_b_pw2"),
                              ride=gather("mix2"))
    landed("mix2", got)
    saved["mix2"] = (z, u, h2)
    h.append(hm)
    (hn, zf, uf), got = ffn_fwd(hm, vec("ln2_2"), wgu(2), wd(2), ride=gather("ffn2"))
    landed("ffn2", got)
    saved["ffn2"] = (zf, uf)
    h.append(hn)

    (z, u), _ = rms_matmul(h[-1], vec("ln1_3"), full["a3_w_in"], no_bias)
    conv[3], wout[3] = conv_full("a3_conv"), full["a3_w_out"].reshape(D, D)
    (hm,), _ = a_mix_fwd(z, h[-1], conv[3], wout[3])
    saved["mix3"] = (z, u)
    h.append(hm)
    (hn, zf, uf), _ = ffn_fwd(hm, vec("ln2_3"), wgu(3), wd(3))
    saved["ffn3"] = (zf, uf)
    h.append(hn)

    loss_lanes, dh, g_lnf = loss_head(h[-1], vec("ln_f"), target)

    g = {"ln_f": g_lnf}
    recv = {}

    def repl_rows():
        loss_row = jnp.pad(loss_lanes, ((0, 0), (0, D - LOSS_LANES)))
        return jnp.concatenate([g[n].reshape(-1, D) for n in REPL] + [loss_row], axis=0)

    def scatter(slot):
        parts = []
        for n in SCATTER_PLAN.get(slot, []):
            parts.append((repl_rows(), "all") if n == "repl" else (g[n], SHARDED[n][0]))
        return scatter_ride(parts) if parts else None

    def arrived(slot, outs):
        recv.update(zip(SCATTER_PLAN.get(slot, []), outs))

    for i in (3, 2, 1, 0):
        zf, uf = saved[f"ffn{i}"]
        (dh_prev, dzf, a, g[f"ln2_{i}"], dhb), got = ffn_bwd_x(dh, h[2 * i + 1], vec(f"ln2_{i}"), zf, wgu(i), wd(i),
                                                             ride=scatter(f"ffnx{i}"))
        arrived(f"ffnx{i}", got)
        dh = dh_prev
        (dwgu, dwd), got = ffn_bwd_w(uf, dzf, a, dhb, ride=scatter(f"ffnw{i}"))
        arrived(f"ffnw{i}", got)
        g[f"ffn{i}_w_gu"], g[f"ffn{i}_w_down"] = dwgu.reshape(N_DEV, FC, D), dwd.reshape(FF, D)
        if i == 0:
            late_ffn = scatter_start([(g[n], SHARDED[n][0]) for n in LATE_FFN], "late_ffn_start")
        hin = h[2 * i]
        if i in (0, 3):
            z, u = saved[f"mix{i}"]
            (dz, g[f"a{i}_w_out"], g[f"a{i}_conv"]), got = a_mix_bwd(dh, z, conv[i], wout[i], ride=scatter(f"mixb{i}"))
            arrived(f"mixb{i}", got)
            (g[f"a{i}_w_in"],), got = in_proj_bwd_w(u, dz, ride=scatter(f"inw{i}"))
            arrived(f"inw{i}", got)
            if i == 0:
                late_mix = scatter_start([(g[n], SHARDED[n][0]) for n in LATE_MIX], "late_mix_start")
            (dh, g[f"ln1_{i}"], _), got = in_proj_bwd_x(dz, full[f"a{i}_w_in"], hin, vec(f"ln1_{i}"), dh,
                                                       ride=scatter(f"inx{i}"))
            arrived(f"inx{i}", got)
        elif i == 1:
            (dh, g["ln1_1"], g["b1_w_grp"], g["b1_scale"]), got = b_mix_bwd(dh, hin, vec("ln1_1"), wgrp, vec("b1_scale"),
                                                                             ride=scatter("mixb1"))
            arrived("mixb1", got)
        else:
            z, u, h2 = saved["mix2"]
            (dz, g["c2_w_pw2"], g["c2_b_pw2"], g["c2_ln_g"], g["c2_ln_b"], g["c2_b_dw"], g["c2_dw"]), got = c_mix_bwd(
                dh, z, h2, cdw, vec("c2_ln_g"), vec("c2_ln_b"), wpw2, ride=scatter("mixb2"))
            arrived("mixb2", got)
            (g["c2_w_pw1"],), got = in_proj_bwd_w(u, dz, ride=scatter("inw2"))
            arrived("inw2", got)
            (dh, g["ln1_2"], g["c2_b_pw1"]), got = in_proj_bwd_x(dz, full["c2_w_pw1"], hin, vec("ln1_2"), dh,
                                                                ride=scatter("inx2"))
            arrived("inx2", got)
    grad_x = dh[None]
    arrived("last", run_ride(scatter("last"), "scatter_last"))

    grad, delta, new_m, new_v = {}, {}, {}, {}
    two_d = lambda n, a: stored(n, a.reshape(-1, p[n].shape[-1]))

    def adam_calls(names_, tokens):
        groups, last = {}, None
        for n in names_:
            groups.setdefault((two_d(n, p[n]).shape, SHARDED[n][1]), []).append(n)
        for (shape, rb), members in groups.items():
            per_weight = 2 * rb * shape[1] * (7 * 4 + N_DEV * recv[members[0]].dtype.itemsize)
            at_once = max(1, (ADAM_VMEM // per_weight))
            for lo in range(0, len(members), at_once):
                ns = members[lo:lo + at_once]
                outs = adamw([two_d(n, p[n]) for n in ns], [two_d(n, p["m_" + n]) for n in ns],
                             [two_d(n, p["v_" + n]) for n in ns], [recv[n].reshape(N_DEV, *shape) for n in ns], rb, tokens)
                for res, o in zip((grad, delta, new_m, new_v), outs):
                    res.update({n: stored(n, a).reshape(p[n].shape) for n, a in zip(ns, o)})
                last = outs[0][0]
        return last

    early_done = adam_calls([n for n in SHARDED if n not in LATE_FFN + LATE_MIX], (late_ffn[-1], late_mix[-1]))
    outs = adamw_vectors([vec(n) for n in REPL], [vec("m_" + n) for n in REPL], [vec("v_" + n) for n in REPL], recv["repl"])
    for res, o in zip((grad, delta, new_m, new_v), outs):
        res.update({n: a.reshape(p[n].shape) for n, a in zip(REPL, o)})
    recv.update(zip(LATE_FFN, scatter_wait(late_ffn, early_done, "late_ffn_wait")))
    recv.update(zip(LATE_MIX, scatter_wait(late_mix, early_done, "late_mix_wait")))
    adam_calls(LATE_FFN + LATE_MIX, ())

    loss = jnp.sum(recv["repl"][:, REPL_ROWS, 0])
    return (loss, grad_x, *[grad[n] for n in WEIGHTS], *[delta[n] for n in WEIGHTS],
            *[new_m[n] for n in WEIGHTS], *[new_v[n] for n in WEIGHTS])


def kernel(x, ln1_0, a0_w_in, a0_conv, a0_w_out, ln2_0, ffn0_w_gu, ffn0_w_down, ln1_1, b1_w_grp, b1_scale, ln2_1, ffn1_w_gu, ffn1_w_down, ln1_2, c2_w_pw1, c2_b_pw1, c2_dw, c2_b_dw, c2_ln_g, c2_ln_b, c2_w_pw2, c2_b_pw2, ln2_2, ffn2_w_gu, ffn2_w_down, ln1_3, a3_w_in, a3_conv, a3_w_out, ln2_3, ffn3_w_gu, ffn3_w_down, ln_f, loss_target, m_ln1_0, m_a0_w_in, m_a0_conv, m_a0_w_out, m_ln2_0, m_ffn0_w_gu, m_ffn0_w_down, m_ln1_1, m_b1_w_grp, m_b1_scale, m_ln2_1, m_ffn1_w_gu, m_ffn1_w_down, m_ln1_2, m_c2_w_pw1, m_c2_b_pw1, m_c2_dw, m_c2_b_dw, m_c2_ln_g, m_c2_ln_b, m_c2_w_pw2, m_c2_b_pw2, m_ln2_2, m_ffn2_w_gu, m_ffn2_w_down, m_ln1_3, m_a3_w_in, m_a3_conv, m_a3_w_out, m_ln2_3, m_ffn3_w_gu, m_ffn3_w_down, m_ln_f, v_ln1_0, v_a0_w_in, v_a0_conv, v_a0_w_out, v_ln2_0, v_ffn0_w_gu, v_ffn0_w_down, v_ln1_1, v_b1_w_grp, v_b1_scale, v_ln2_1, v_ffn1_w_gu, v_ffn1_w_down, v_ln1_2, v_c2_w_pw1, v_c2_b_pw1, v_c2_dw, v_c2_b_dw, v_c2_ln_g, v_c2_ln_b, v_c2_w_pw2, v_c2_b_pw2, v_ln2_2, v_ffn2_w_gu, v_ffn2_w_down, v_ln1_3, v_a3_w_in, v_a3_conv, v_a3_w_out, v_ln2_3, v_ffn3_w_gu, v_ffn3_w_down, v_ln_f):
    return _step(dict(locals()))
```

```python
import jax
import jax.numpy as jnp
from jax import lax
from jax.experimental import pallas as pl
from jax.experimental.pallas import tpu as pltpu

F32 = jnp.float32
BF16 = jnp.bfloat16

N_DEV = 8
D = 1024
FF = 2816
FC = FF // 4
RMS_EPS = 1e-6
LN_EPS = 1e-5
TM = 512
HALO = 32
VMEM_LIMIT = 60 * 1024 * 1024

NT = (((1,), (1,)), ((), ()))
TN = (((0,), (0,)), ((), ()))
MESH = pl.DeviceIdType.MESH
ANY = pl.BlockSpec(memory_space=pl.ANY)
N_PEERS = N_DEV - 1


def _dot(a, b):
    return jnp.dot(a, b, preferred_element_type=F32)


def _dot_nt(a, b):
    return lax.dot_general(a, b, NT, preferred_element_type=F32)


def _dot_tn(a, b):
    return lax.dot_general(a, b, TN, preferred_element_type=F32)


def _rms_fwd(x, gain):
    r = lax.rsqrt(jnp.mean(x * x, axis=-1, keepdims=True) + RMS_EPS)
    return x * r * gain


def _rms_bwd(x, gain, du):
    r = lax.rsqrt(jnp.mean(x * x, axis=-1, keepdims=True) + RMS_EPS)
    xhat = x * r
    dgain = jnp.sum(du * xhat, axis=0, keepdims=True)
    dxhat = du * gain
    dx = r * (dxhat - xhat * jnp.mean(dxhat * xhat, axis=-1, keepdims=True))
    return dx, dgain


def _dev_index(p):
    return 4 * p[0] + 2 * p[1] + p[2]


def _place():
    return lax.axis_index("x"), lax.axis_index("y"), lax.axis_index("c")


class Ride:
    def __init__(self, ins, out_shapes, start, finish):
        self.ins, self.out_shapes, self.start, self.finish = list(ins), list(out_shapes), start, finish
        n = len(self.ins)
        self.sems = [pltpu.SemaphoreType.DMA((n * N_PEERS,)), pltpu.SemaphoreType.DMA((n * N_PEERS,)),
                     pltpu.SemaphoreType.DMA((n,))]


def gather_ride(shards, kinds):
    n = len(shards)

    def setup(ins, outs, sems):
        send_sems, recv_sems, local_sems = sems
        x, y, c = _place()
        chips = [(1 - x, y), (x, 1 - y), (1 - x, 1 - y)]

        def copy(a, k, block, to, src=None):
            slot = _chunk(outs[a], kinds[a], _dev_index(block))
            return pltpu.make_async_remote_copy(
                src_ref=slot if src is None else src, dst_ref=slot,
                send_sem=send_sems.at[a * N_PEERS + k], recv_sem=recv_sems.at[a * N_PEERS + k],
                device_id=to, device_id_type=MESH)

        def mine(a):
            return pltpu.make_async_copy(ins[a], _chunk(outs[a], kinds[a], _dev_index((x, y, c))), local_sems.at[a])

        def first(a):
            return [copy(a, 0, (x, y, c), (x, y, 1 - c), src=ins[a])] + [
                copy(a, 1 + j, (x, y, c), (*chip, c), src=ins[a]) for j, chip in enumerate(chips)]

        return (x, y, c), chips, copy, mine, first

    def start(ins, outs, sems):
        _, _, _, mine, first = setup(ins, outs, sems)
        for a in range(n):
            mine(a).start()
            for cp in first(a):
                cp.start()

    def finish(ins, outs, sems):
        (x, y, c), chips, copy, mine, first = setup(ins, outs, sems)
        me, sibling = (x, y, c), (x, y, 1 - c)
        for a in range(n):
            for j, chip in enumerate(chips):
                copy(a, 1 + j, (*chip, c), me).wait_recv()
                copy(a, 4 + j, (*chip, c), sibling).start()
        for a in range(n):
            copy(a, 0, sibling, me).wait_recv()
            for j, chip in enumerate(chips):
                copy(a, 4 + j, (*chip, 1 - c), me).wait_recv()
        for a in range(n):
            for cp in first(a):
                cp.wait_send()
            for j, chip in enumerate(chips):
                copy(a, 4 + j, (*chip, c), sibling).wait_send()
        for a in range(n):
            mine(a).wait()

    shapes = [(N_DEV, *s.shape) if kind == "lead" else (s.shape[0], N_DEV * s.shape[1]) for s, kind in zip(shards, kinds)]
    return Ride(shards, [jax.ShapeDtypeStruct(shape, s.dtype) for shape, s in zip(shapes, shards)], start, finish)


def _chunk(ref, kind, j):
    if kind == "lead":
        return ref.at[j]
    if kind == "rows":
        r = ref.shape[0] // N_DEV
        return ref.at[pl.ds(j * r, r)]
    if kind == "mid":
        r = ref.shape[1] // N_DEV
        return ref.at[:, pl.ds(j * r, r), :]
    if kind == "cols":
        c = ref.shape[1] // N_DEV
        return ref.at[:, pl.ds(j * c, c)]
    return ref


def _chunk_shape(shape, kind):
    if kind == "lead":
        return tuple(shape[1:])
    if kind == "rows":
        return (shape[0] // N_DEV, *shape[1:])
    if kind == "mid":
        return (shape[0], shape[1] // N_DEV, shape[2])
    if kind == "cols":
        return (shape[0], shape[1] // N_DEV)
    return tuple(shape)


def scatter_ride(parts):
    n = len(parts)
    kinds = [k for _, k in parts]

    def setup(ins, outs, sems):
        send_sems, recv_sems, local_sems = sems
        x, y, c = _place()
        me = _dev_index((x, y, c))
        peers = []
        for k in range(1, N_DEV):
            kx, ky, kc = (k >> 2) & 1, (k >> 1) & 1, k & 1
            peers.append((1 - x if kx else x, 1 - y if ky else y, 1 - c if kc else c))

        def copy(a, k, peer):
            return pltpu.make_async_remote_copy(
                src_ref=_chunk(ins[a], kinds[a], _dev_index(peer)), dst_ref=outs[a].at[me],
                send_sem=send_sems.at[a * N_PEERS + k], recv_sem=recv_sems.at[a * N_PEERS + k],
                device_id=peer, device_id_type=MESH)

        def arrival(a, k, peer):
            slot = outs[a].at[_dev_index(peer)]
            return pltpu.make_async_remote_copy(
                src_ref=slot, dst_ref=slot,
                send_sem=send_sems.at[a * N_PEERS + k], recv_sem=recv_sems.at[a * N_PEERS + k],
                device_id=peer, device_id_type=MESH)

        def mine(a):
            return pltpu.make_async_copy(_chunk(ins[a], kinds[a], me), outs[a].at[me], local_sems.at[a])

        return peers, copy, arrival, mine

    def start(ins, outs, sems):
        peers, copy, _, mine = setup(ins, outs, sems)
        for a in range(n):
            mine(a).start()
            for k, peer in enumerate(peers):
                copy(a, k, peer).start()

    def finish(ins, outs, sems):
        peers, copy, arrival, mine = setup(ins, outs, sems)
        for a in range(n):
            for k, peer in enumerate(peers):
                arrival(a, k, peer).wait_recv()
        for a in range(n):
            for k, peer in enumerate(peers):
                copy(a, k, peer).wait_send()
            mine(a).wait()

    shapes = [jax.ShapeDtypeStruct((N_DEV, *_chunk_shape(arr.shape, kind)), arr.dtype) for arr, kind in parts]
    return Ride([arr for arr, _ in parts], shapes, start, finish)


HBM = pl.BlockSpec(memory_space=pltpu.HBM)
SEM = pl.BlockSpec(memory_space=pltpu.SEMAPHORE)
DATAFLOW = pltpu.SideEffectType.DATAFLOW_SIDE_EFFECTING
TOKEN = (8, 128)


def _scatter_copies(kinds, ins, lands, send_sems, recv_sems):
    x, y, c = _place()
    me = _dev_index((x, y, c))
    sends, arrivals = [], []
    for a, kind in enumerate(kinds):
        for k in range(1, N_DEV):
            kx, ky, kc = (k >> 2) & 1, (k >> 1) & 1, k & 1
            peer = (1 - x if kx else x, 1 - y if ky else y, 1 - c if kc else c)
            sem = a * N_PEERS + k - 1
            sends.append(pltpu.make_async_remote_copy(
                src_ref=_chunk(ins[a], kind, _dev_index(peer)), dst_ref=lands[a].at[me],
                send_sem=send_sems.at[sem], recv_sem=recv_sems.at[sem], device_id=peer, device_id_type=MESH))
            slot = lands[a].at[_dev_index(peer)]
            arrivals.append(pltpu.make_async_remote_copy(
                src_ref=slot, dst_ref=slot, send_sem=send_sems.at[sem], recv_sem=recv_sems.at[sem],
                device_id=peer, device_id_type=MESH))
    return me, sends, arrivals


def scatter_start(parts, name):
    n = len(parts)
    kinds = [k for _, k in parts]
    arrays = [pltpu.with_memory_space_constraint(a, pltpu.HBM) for a, _ in parts]
    zones = [pltpu.with_memory_space_constraint(lax.empty((N_DEV, *_chunk_shape(a.shape, k)), a.dtype), pltpu.HBM)
             for a, k in parts]

    def body(*refs):
        ins, lands = refs[:n], refs[n:2 * n]
        send_sems, recv_sems = refs[2 * n], refs[2 * n + 1]
        token = refs[4 * n + 2]
        _, sends, _ = _scatter_copies(kinds, ins, lands, send_sems, recv_sems)
        for cp in sends:
            cp.start()
        token[...] = jnp.zeros_like(token)

    outs = pl.pallas_call(
        body, name=name,
        out_shape=(pltpu.SemaphoreType.DMA((n * N_PEERS,)), pltpu.SemaphoreType.DMA((n * N_PEERS,)),
                   *[pltpu.HBM(a.shape, a.dtype) for a in arrays], *[pltpu.HBM(z.shape, z.dtype) for z in zones],
                   jax.ShapeDtypeStruct(TOKEN, F32)),
        in_specs=[HBM] * (2 * n),
        out_specs=(SEM, SEM, *[HBM] * (2 * n), pl.BlockSpec(memory_space=pltpu.VMEM)),
        input_output_aliases={i: 2 + i for i in range(2 * n)},
        compiler_params=pltpu.CompilerParams(has_side_effects=DATAFLOW),
    )(*arrays, *zones)
    return kinds, outs[0], outs[1], outs[2:2 + n], outs[2 + n:2 + 2 * n], outs[2 + 2 * n]


def scatter_wait(started, after, name):
    kinds, send_sems, recv_sems, arrays, zones, _ = started
    n = len(kinds)

    def body(*refs):
        ins, lands = refs[:n], refs[n:2 * n]
        local_sems = refs[-1]
        me, sends, arrivals = _scatter_copies(kinds, ins, lands, refs[2 * n], refs[2 * n + 1])
        own = [pltpu.make_async_copy(_chunk(ins[a], kinds[a], me), lands[a].at[me], local_sems.at[a]) for a in range(n)]
        for cp in own:
            cp.start()
        for cp in sends:
            cp.wait_send()
        for cp in arrivals:
            cp.wait_recv()
        for cp in own:
            cp.wait()

    outs = pl.pallas_call(
        body, name=name,
        out_shape=(*[pltpu.HBM(a.shape, a.dtype) for a in arrays], *[pltpu.HBM(z.shape, z.dtype) for z in zones]),
        in_specs=[HBM] * (2 * n) + [SEM, SEM] + [ANY] * len(after),
        out_specs=[HBM] * (2 * n),
        input_output_aliases={i: i for i in range(2 * n)},
        scratch_shapes=[pltpu.SemaphoreType.DMA((n,))],
        compiler_params=pltpu.CompilerParams(has_side_effects=DATAFLOW),
    )(*arrays, *zones, send_sems, recv_sems, *after)
    return outs[n:]


def run_ride(ride, name):
    n_in, n_out = len(ride.ins), len(ride.out_shapes)

    def body(*refs):
        ins, outs, sems = refs[:n_in], refs[n_in:n_in + n_out], refs[n_in + n_out:]
        ride.start(ins, outs, sems)
        ride.finish(ins, outs, sems)

    return pl.pallas_call(
        body, name=name, in_specs=[ANY] * n_in, out_specs=[ANY] * n_out, out_shape=ride.out_shapes,
        scratch_shapes=ride.sems,
    )(*ride.ins)


def _call(body, *, name, grid, in_specs, out_specs, out_shape, args, scratch_shapes=(), ride=None):
    params = pltpu.CompilerParams(dimension_semantics=("arbitrary",) * len(grid), vmem_limit_bytes=VMEM_LIMIT)
    if ride is None:
        outs = pl.pallas_call(body, name=name, grid=grid, in_specs=in_specs, out_specs=out_specs, out_shape=out_shape,
                              scratch_shapes=list(scratch_shapes), compiler_params=params)(*args)
        return outs, []
    n_in, n_out, n_scr = len(in_specs), len(out_specs), len(scratch_shapes)
    r_in, r_out = len(ride.ins), len(ride.out_shapes)

    def hosted(*refs):
        ins, refs = refs[:n_in], refs[n_in:]
        rins, refs = refs[:r_in], refs[r_in:]
        outs, refs = refs[:n_out], refs[n_out:]
        routs, refs = refs[:r_out], refs[r_out:]
        scratch, sems = refs[:n_scr], refs[n_scr:]
        step, n_steps = pl.program_id(0), grid[0]
        for d in range(1, len(grid)):
            step, n_steps = step * grid[d] + pl.program_id(d), n_steps * grid[d]

        @pl.when(step == 0)
        def _():
            ride.start(rins, routs, sems)

        body(*ins, *outs, *scratch)

        @pl.when(step == n_steps - 1)
        def _():
            ride.finish(rins, routs, sems)

    outs = pl.pallas_call(
        hosted, name=name + "_ride", grid=grid,
        in_specs=list(in_specs) + [ANY] * r_in, out_specs=list(out_specs) + [ANY] * r_out,
        out_shape=list(out_shape) + ride.out_shapes,
        scratch_shapes=list(scratch_shapes) + ride.sems, compiler_params=params,
    )(*args, *ride.ins)
    return outs[:n_out], outs[n_out:]


def ffn_fwd(h, gain, wgu, wd, ride=None):
    t = h.shape[0]

    def body(h_ref, g_ref, wgu_ref, wd_ref, hn_ref, z_ref, u_ref, acc):
        k = pl.program_id(1)

        @pl.when(k == 0)
        def _():
            u_ref[...] = _rms_fwd(h_ref[...], g_ref[...]).astype(BF16)
            acc[...] = jnp.zeros_like(acc)

        u = u_ref[...]
        g = _dot_nt(u, wgu_ref[0, 0])
        up = _dot_nt(u, wgu_ref[1, 0])
        z_ref[0, 0] = g.astype(BF16)
        z_ref[1, 0] = up.astype(BF16)
        a = g * jax.nn.sigmoid(g) * up
        acc[...] += _dot(a.astype(BF16), wd_ref[0])

        @pl.when(k == 3)
        def _():
            hn_ref[...] = h_ref[...] + acc[...]

    return _call(
        body, name="ffn_fwd", grid=(t // TM, 4), ride=ride,
        in_specs=[pl.BlockSpec((TM, D), lambda i, k: (i, 0)),
                  pl.BlockSpec((1, D), lambda i, k: (0, 0)),
                  pl.BlockSpec((2, 1, FC, D), lambda i, k: (0, k, 0, 0)),
                  pl.BlockSpec((1, FC, D), lambda i, k: (k, 0, 0))],
        out_specs=[pl.BlockSpec((TM, D), lambda i, k: (i, 0)),
                   pl.BlockSpec((2, 1, TM, FC), lambda i, k: (0, k, i, 0)),
                   pl.BlockSpec((TM, D), lambda i, k: (i, 0))],
        out_shape=[jax.ShapeDtypeStruct((t, D), F32),
                   jax.ShapeDtypeStruct((2, 4, t, FC), BF16),
                   jax.ShapeDtypeStruct((t, D), BF16)],
        scratch_shapes=[pltpu.VMEM((TM, D), F32)],
        args=(h, gain, wgu, wd))


def ffn_bwd_x(dh, h, gain, z, wgu, wd, ride=None):
    t = h.shape[0]

    def body(dh_ref, h_ref, g_ref, z_ref, wgu_ref, wd_ref, dhp_ref, dz_ref, a_ref, dgain_ref, dhb, du):
        i, k = pl.program_id(0), pl.program_id(1)

        @pl.when(k == 0)
        def _():
            dhb[...] = dh_ref[...].astype(BF16)
            du[...] = jnp.zeros_like(du)

        @pl.when((k == 0) & (i == 0))
        def _():
            dgain_ref[...] = jnp.zeros_like(dgain_ref)

        da = _dot_nt(dhb[...], wd_ref[0])
        g = z_ref[0, 0].astype(F32)
        up = z_ref[1, 0].astype(F32)
        sg = jax.nn.sigmoid(g)
        silu = g * sg
        a_ref[0] = (silu * up).astype(BF16)
        dg = (da * up * (sg * (1.0 + g * (1.0 - sg)))).astype(BF16)
        dup = (da * silu).astype(BF16)
        dz_ref[0, 0] = dg
        dz_ref[1, 0] = dup
        for n in range(2):
            cols = slice(n * (D // 2), (n + 1) * (D // 2))
            du[:, cols] += _dot(dg, wgu_ref[0, 0, :, cols]) + _dot(dup, wgu_ref[1, 0, :, cols])

        @pl.when(k == 3)
        def _():
            dx, dgain = _rms_bwd(h_ref[...], g_ref[...], du[...])
            dhp_ref[...] = dh_ref[...] + dx
            dgain_ref[...] += dgain

    return _call(
        body, name="ffn_bwd_x", grid=(t // TM, 4), ride=ride,
        in_specs=[pl.BlockSpec((TM, D), lambda i, k: (i, 0)),
                  pl.BlockSpec((TM, D), lambda i, k: (i, 0)),
                  pl.BlockSpec((1, D), lambda i, k: (0, 0)),
                  pl.BlockSpec((2, 1, TM, FC), lambda i, k: (0, k, i, 0)),
                  pl.BlockSpec((2, 1, FC, D), lambda i, k: (0, k, 0, 0)),
                  pl.BlockSpec((1, FC, D), lambda i, k: (k, 0, 0))],
        out_specs=[pl.BlockSpec((TM, D), lambda i, k: (i, 0)),
                   pl.BlockSpec((2, 1, TM, FC), lambda i, k: (0, k, i, 0)),
                   pl.BlockSpec((1, TM, FC), lambda i, k: (k, i, 0)),
                   pl.BlockSpec((1, D), lambda i, k: (0, 0)),
                   pl.BlockSpec((TM, D), lambda i, k: (i, 0))],
        out_shape=[jax.ShapeDtypeStruct((t, D), F32),
                   jax.ShapeDtypeStruct((2, 4, t, FC), BF16),
                   jax.ShapeDtypeStruct((4, t, FC), BF16),
                   jax.ShapeDtypeStruct((1, D), F32),
                   jax.ShapeDtypeStruct((t, D), BF16)],
        scratch_shapes=[pltpu.VMEM((TM, D), F32)],
        args=(dh, h, gain, z, wgu, wd))


TW = 2048


def ffn_bwd_w(u, dz, a, dhb, ride=None):
    t = u.shape[0]
    tw = min(TW, t)
    steps = t // tw

    def body(u_ref, dz_ref, a_ref, dh_ref, dwgu_ref, dwd_ref, acc_gu, acc_d):
        j = pl.program_id(1)

        @pl.when(j == 0)
        def _():
            acc_gu[...] = jnp.zeros_like(acc_gu)
            acc_d[...] = jnp.zeros_like(acc_d)

        ub = u_ref[...]
        acc_gu[0] += _dot_tn(dz_ref[0, 0], ub)
        acc_gu[1] += _dot_tn(dz_ref[1, 0], ub)
        acc_d[...] += _dot_tn(a_ref[0], dh_ref[...])

        @pl.when(j == steps - 1)
        def _():
            dwgu_ref[:, 0] = acc_gu[...].astype(BF16)
            dwd_ref[0] = acc_d[...].astype(BF16)

    return _call(
        body, name="ffn_bwd_w", grid=(4, steps), ride=ride,
        in_specs=[pl.BlockSpec((tw, D), lambda k, j: (j, 0)),
                  pl.BlockSpec((2, 1, tw, FC), lambda k, j: (0, k, j, 0)),
                  pl.BlockSpec((1, tw, FC), lambda k, j: (k, j, 0)),
                  pl.BlockSpec((tw, D), lambda k, j: (j, 0))],
        out_specs=[pl.BlockSpec((2, 1, FC, D), lambda k, j: (0, k, 0, 0)),
                   pl.BlockSpec((1, FC, D), lambda k, j: (k, 0, 0))],
        out_shape=[jax.ShapeDtypeStruct((2, 4, FC, D), BF16),
                   jax.ShapeDtypeStruct((4, FC, D), BF16)],
        scratch_shapes=[pltpu.VMEM((2, FC, D), F32), pltpu.VMEM((FC, D), F32)],
        args=(u, dz, a, dhb))


def _prev_halo(i, tile=TM):
    return jnp.maximum(i * (tile // HALO) - 1, 0)


def _next_halo(i, t, tile=TM):
    return jnp.minimum((i + 1) * (tile // HALO), t // HALO - 1)


def rms_matmul(h, gain, w, bias, ride=None):
    t = h.shape[0]
    n = w.shape[1]

    def body(h_ref, g_ref, w_ref, b_ref, z_ref, u_ref):
        u = _rms_fwd(h_ref[...], g_ref[...]).astype(BF16)
        u_ref[...] = u
        z_ref[...] = (_dot(u, w_ref[...]) + b_ref[...]).astype(BF16)

    return _call(
        body, name=f"rms_matmul_{n}", grid=(t // TM,), ride=ride,
        in_specs=[pl.BlockSpec((TM, D), lambda i: (i, 0)),
                  pl.BlockSpec((1, D), lambda i: (0, 0)),
                  pl.BlockSpec((D, n), lambda i: (0, 0)),
                  pl.BlockSpec((1, n), lambda i: (0, 0))],
        out_specs=[pl.BlockSpec((TM, n), lambda i: (i, 0)),
                   pl.BlockSpec((TM, D), lambda i: (i, 0))],
        out_shape=[jax.ShapeDtypeStruct((t, n), BF16),
                   jax.ShapeDtypeStruct((t, D), BF16)],
        args=(h, gain, w, bias))


def in_proj_bwd_x(dz, w, h, gain, dh, ride=None):
    t = h.shape[0]
    n = w.shape[1]

    def body(dz_ref, w_ref, h_ref, g_ref, dh_ref, dhp_ref, dgain_ref, dbias_ref):
        @pl.when(pl.program_id(0) == 0)
        def _():
            dgain_ref[...] = jnp.zeros_like(dgain_ref)
            dbias_ref[...] = jnp.zeros_like(dbias_ref)

        du = _dot_nt(dz_ref[...], w_ref[...])
        dx, dgain = _rms_bwd(h_ref[...], g_ref[...], du)
        dhp_ref[...] = dh_ref[...] + dx
        dgain_ref[...] += dgain
        dbias_ref[...] += jnp.sum(dz_ref[...].astype(F32), axis=0, keepdims=True)

    return _call(
        body, name=f"in_proj_bwd_x_{n}", grid=(t // TM,), ride=ride,
        in_specs=[pl.BlockSpec((TM, n), lambda i: (i, 0)),
                  pl.BlockSpec((D, n), lambda i: (0, 0)),
                  pl.BlockSpec((TM, D), lambda i: (i, 0)),
                  pl.BlockSpec((1, D), lambda i: (0, 0)),
                  pl.BlockSpec((TM, D), lambda i: (i, 0))],
        out_specs=[pl.BlockSpec((TM, D), lambda i: (i, 0)),
                   pl.BlockSpec((1, D), lambda i: (0, 0)),
                   pl.BlockSpec((1, n), lambda i: (0, 0))],
        out_shape=[jax.ShapeDtypeStruct((t, D), F32),
                   jax.ShapeDtypeStruct((1, D), F32),
                   jax.ShapeDtypeStruct((1, n), F32)],
        args=(dz, w, h, gain, dh))


def in_proj_bwd_w(u, dz, ride=None):
    t = u.shape[0]
    n = dz.shape[1]
    steps = t // TM

    def body(u_ref, dz_ref, dw_ref, acc):
        s = pl.program_id(0)

        @pl.when(s == 0)
        def _():
            acc[...] = jnp.zeros_like(acc)

        acc[...] += _dot_tn(u_ref[...], dz_ref[...])

        @pl.when(s == steps - 1)
        def _():
            dw_ref[...] = acc[...].astype(BF16)

    return _call(
        body, name=f"in_proj_bwd_w_{n}", grid=(steps,), ride=ride,
        in_specs=[pl.BlockSpec((TM, D), lambda s: (s, 0)),
                  pl.BlockSpec((TM, n), lambda s: (s, 0))],
        out_specs=[pl.BlockSpec((D, n), lambda s: (0, 0))],
        out_shape=[jax.ShapeDtypeStruct((D, n), BF16)],
        scratch_shapes=[pltpu.VMEM((D, n), F32)],
        args=(u, dz))


A_TAPS = 3


def a_mix_fwd(z, h, conv, wout, ride=None):
    t = h.shape[0]

    def body(z_ref, zp_ref, h_ref, cw_ref, wo_ref, hn_ref, pad):
        i = pl.program_id(0)
        ph = zp_ref[:, D:2 * D].astype(F32) * zp_ref[:, 2 * D:].astype(F32)
        pad[0:HALO, :] = jnp.where(i == 0, 0.0, ph)
        pad[HALO:, :] = z_ref[:, D:2 * D].astype(F32) * z_ref[:, 2 * D:].astype(F32)
        q = jnp.zeros((TM, D), F32)
        for k in range(A_TAPS):
            off = HALO - (A_TAPS - 1) + k
            q += cw_ref[k:k + 1, :] * pad[off:off + TM, :]
        r = z_ref[:, 0:D].astype(F32) * q
        hn_ref[...] = h_ref[...] + _dot(r.astype(BF16), wo_ref[...])

    return _call(
        body, name="a_mix_fwd", grid=(t // TM,), ride=ride,
        in_specs=[pl.BlockSpec((TM, 3 * D), lambda i: (i, 0)),
                  pl.BlockSpec((HALO, 3 * D), lambda i: (_prev_halo(i), 0)),
                  pl.BlockSpec((TM, D), lambda i: (i, 0)),
                  pl.BlockSpec((A_TAPS, D), lambda i: (0, 0)),
                  pl.BlockSpec((D, D), lambda i: (0, 0))],
        out_specs=[pl.BlockSpec((TM, D), lambda i: (i, 0))],
        out_shape=[jax.ShapeDtypeStruct((t, D), F32)],
        scratch_shapes=[pltpu.VMEM((HALO + TM, D), F32)],
        args=(z, z, h, conv, wout))


def a_mix_bwd(dh, z, conv, wout, ride=None):
    t = dh.shape[0]
    steps = t // TM

    def body(dh_ref, dhn_ref, z_ref, zp_ref, zn_ref, cw_ref, wo_ref, dz_ref, dwo_ref, dcw_ref, pad, dqpad, dwo):
        i = pl.program_id(0)
        last = i == steps - 1

        @pl.when(i == 0)
        def _():
            dwo[...] = jnp.zeros_like(dwo)
            dcw_ref[...] = jnp.zeros_like(dcw_ref)

        ph = zp_ref[:, D:2 * D].astype(F32) * zp_ref[:, 2 * D:].astype(F32)
        pad[0:HALO, :] = jnp.where(i == 0, 0.0, ph)
        c = z_ref[:, D:2 * D].astype(F32)
        v = z_ref[:, 2 * D:].astype(F32)
        pad[HALO:, :] = c * v
        q = jnp.zeros((TM, D), F32)
        for k in range(A_TAPS):
            off = HALO - (A_TAPS - 1) + k
            q += cw_ref[k:k + 1, :] * pad[off:off + TM, :]
        b = z_ref[:, 0:D].astype(F32)
        dhb = dh_ref[...].astype(BF16)
        dwo[...] += _dot_tn((b * q).astype(BF16), dhb)
        dr = _dot_nt(dhb, wo_ref[...])
        dz_ref[:, 0:D] = (dr * q).astype(BF16)
        dq = dr * b
        drn = _dot_nt(dhn_ref[...].astype(BF16), wo_ref[...])
        dqpad[0:TM, :] = dq
        dqpad[TM:, :] = jnp.where(last, 0.0, drn * zn_ref[:, 0:D].astype(F32))
        dp = jnp.zeros((TM, D), F32)
        for k in range(A_TAPS):
            off = A_TAPS - 1 - k
            dp += cw_ref[k:k + 1, :] * dqpad[off:off + TM, :]
            poff = HALO - (A_TAPS - 1) + k
            dcw_ref[k:k + 1, :] += jnp.sum(dq * pad[poff:poff + TM, :], axis=0, keepdims=True)
        dz_ref[:, D:2 * D] = (dp * v).astype(BF16)
        dz_ref[:, 2 * D:] = (dp * c).astype(BF16)

        @pl.when(last)
        def _():
            dwo_ref[...] = dwo[...].astype(BF16)

    return _call(
        body, name="a_mix_bwd", grid=(steps,), ride=ride,
        in_specs=[pl.BlockSpec((TM, D), lambda i: (i, 0)),
                  pl.BlockSpec((HALO, D), lambda i: (_next_halo(i, t), 0)),
                  pl.BlockSpec((TM, 3 * D), lambda i: (i, 0)),
                  pl.BlockSpec((HALO, 3 * D), lambda i: (_prev_halo(i), 0)),
                  pl.BlockSpec((HALO, 3 * D), lambda i: (_next_halo(i, t), 0)),
                  pl.BlockSpec((A_TAPS, D), lambda i: (0, 0)),
                  pl.BlockSpec((D, D), lambda i: (0, 0))],
        out_specs=[pl.BlockSpec((TM, 3 * D), lambda i: (i, 0)),
                   pl.BlockSpec((D, D), lambda i: (0, 0)),
                   pl.BlockSpec((A_TAPS, D), lambda i: (0, 0))],
        out_shape=[jax.ShapeDtypeStruct((t, 3 * D), BF16),
                   jax.ShapeDtypeStruct((D, D), BF16),
                   jax.ShapeDtypeStruct((A_TAPS, D), F32)],
        scratch_shapes=[pltpu.VMEM((HALO + TM, D), F32), pltpu.VMEM((TM + HALO, D), F32), pltpu.VMEM((D, D), F32)],
        args=(dh, dh, z, z, z, conv, wout))


C_TAPS = 31


def _glu(zr):
    return zr[:, 0:D].astype(F32) * jax.nn.sigmoid(zr[:, D:].astype(F32))


def _ln_silu(h2, lg, lb):
    mu = jnp.mean(h2, axis=-1, keepdims=True)
    xc = h2 - mu
    rstd = lax.rsqrt(jnp.mean(xc * xc, axis=-1, keepdims=True) + LN_EPS)
    xn = xc * rstd
    h3 = xn * lg + lb
    s3 = jax.nn.sigmoid(h3)
    return xn, rstd, h3, s3


def _ln_silu_bwd(h2, lg, lb, dh4):
    xn, rstd, h3, s3 = _ln_silu(h2, lg, lb)
    dh3 = dh4 * (s3 * (1.0 + h3 * (1.0 - s3)))
    dxn = dh3 * lg
    dh2 = rstd * (dxn - jnp.mean(dxn, axis=-1, keepdims=True) - xn * jnp.mean(dxn * xn, axis=-1, keepdims=True))
    return dh2, dh3, xn, h3 * s3


TC = 256
RB = 64
LANES = 128
SHIFTS = 7


def _shifted_copies(src, sh, rows):
    for b in range(1, SHIFTS + 1):
        sh[b - 1, 0:rows, :] = src[b:b + rows, :]


def _window(src, sh, o, r0, lanes):
    a, b = divmod(o, 8)
    ref = src if b == 0 else sh.at[b - 1]
    return ref[8 * a + r0:8 * a + r0 + RB, lanes]


def c_mix_fwd(z, h, dw, bdw, lg, lb, w2, b2, ride=None):
    t = h.shape[0]

    def body(z_ref, zp_ref, h_ref, dw_ref, bdw_ref, lg_ref, lb_ref, w2_ref, b2_ref, hn_ref, h2_ref, pad, sh):
        i = pl.program_id(0)
        pad[0:HALO, :] = jnp.where(i == 0, 0.0, _glu(zp_ref))
        pad[HALO:, :] = _glu(z_ref)
        _shifted_copies(pad, sh, TC + 24)
        for l in range(D // LANES):
            lanes = slice(l * LANES, (l + 1) * LANES)
            for r0 in range(0, TC, RB):
                acc = jnp.zeros((RB, LANES), F32) + bdw_ref[:, lanes]
                for k in range(C_TAPS):
                    acc += dw_ref[k:k + 1, lanes] * _window(pad, sh, HALO - (C_TAPS - 1) + k, r0, lanes)
                h2_ref[r0:r0 + RB, lanes] = acc
        _, _, h3, s3 = _ln_silu(h2_ref[...], lg_ref[...], lb_ref[...])
        hn_ref[...] = h_ref[...] + _dot((h3 * s3).astype(BF16), w2_ref[...]) + b2_ref[...]

    vec = pl.BlockSpec((1, D), lambda i: (0, 0))
    return _call(
        body, name="c_mix_fwd", grid=(t // TC,), ride=ride,
        in_specs=[pl.BlockSpec((TC, 2 * D), lambda i: (i, 0)),
                  pl.BlockSpec((HALO, 2 * D), lambda i: (_prev_halo(i, TC), 0)),
                  pl.BlockSpec((TC, D), lambda i: (i, 0)),
                  pl.BlockSpec((C_TAPS, D), lambda i: (0, 0)),
                  vec, vec, vec,
                  pl.BlockSpec((D, D), lambda i: (0, 0)),
                  vec],
        out_specs=[pl.BlockSpec((TC, D), lambda i: (i, 0)),
                   pl.BlockSpec((TC, D), lambda i: (i, 0))],
        out_shape=[jax.ShapeDtypeStruct((t, D), F32),
                   jax.ShapeDtypeStruct((t, D), F32)],
        scratch_shapes=[pltpu.VMEM((HALO + TC, D), F32), pltpu.VMEM((SHIFTS, TC + 24, D), F32)],
        args=(z, z, h, dw, bdw, lg, lb, w2, b2))


def c_mix_bwd(dh, z, h2, dw, lg, lb, w2, ride=None):
    t = dh.shape[0]
    steps = t // TC

    def body(dh_ref, dhn_ref, z_ref, zp_ref, h2_ref, h2n_ref, dw_ref, lg_ref, lb_ref, w2_ref,
             dz_ref, dw2_ref, db2_ref, dlg_ref, dlb_ref, dbdw_ref, ddw_ref, pad, dpad, dw2, sh, dh1):
        i = pl.program_id(0)
        last = i == steps - 1

        @pl.when(i == 0)
        def _():
            for r in (dw2, db2_ref, dlg_ref, dlb_ref, dbdw_ref, ddw_ref):
                r[...] = jnp.zeros_like(r)

        lg, lb = lg_ref[...], lb_ref[...]
        dh = dh_ref[...]
        dhb = dh.astype(BF16)
        dh2, dh3, xn, h4 = _ln_silu_bwd(h2_ref[...], lg, lb, _dot_nt(dhb, w2_ref[...]))
        dw2[...] += _dot_tn(h4.astype(BF16), dhb)
        db2_ref[...] += jnp.sum(dh, axis=0, keepdims=True)
        dlg_ref[...] += jnp.sum(dh3 * xn, axis=0, keepdims=True)
        dlb_ref[...] += jnp.sum(dh3, axis=0, keepdims=True)
        dbdw_ref[...] += jnp.sum(dh2, axis=0, keepdims=True)
        dh2n, _, _, _ = _ln_silu_bwd(h2n_ref[...], lg, lb, _dot_nt(dhn_ref[...].astype(BF16), w2_ref[...]))
        dpad[0:TC, :] = dh2
        dpad[TC:, :] = jnp.where(last, 0.0, dh2n)
        _shifted_copies(dpad, sh, TC + 24)
        for l in range(D // LANES):
            lanes = slice(l * LANES, (l + 1) * LANES)
            for r0 in range(0, TC, RB):
                acc = jnp.zeros((RB, LANES), F32)
                for k in range(C_TAPS):
                    acc += dw_ref[k:k + 1, lanes] * _window(dpad, sh, C_TAPS - 1 - k, r0, lanes)
                dh1[r0:r0 + RB, lanes] = acc
        pad[0:HALO, :] = jnp.where(i == 0, 0.0, _glu(zp_ref))
        pad[HALO:, :] = _glu(z_ref)
        _shifted_copies(pad, sh, TC + 24)
        for l in range(D // LANES):
            lanes = slice(l * LANES, (l + 1) * LANES)
            accs = [jnp.zeros((8, LANES), F32) for _ in range(C_TAPS)]
            for r0 in range(0, TC, RB):
                d = dpad[r0:r0 + RB, lanes]
                for k in range(C_TAPS):
                    prod = d * _window(pad, sh, HALO - (C_TAPS - 1) + k, r0, lanes)
                    accs[k] += jnp.sum(prod.reshape(RB // 8, 8, LANES), axis=0)
            for k in range(C_TAPS):
                ddw_ref[k:k + 1, lanes] += jnp.sum(accs[k], axis=0, keepdims=True)
        a = z_ref[:, 0:D].astype(F32)
        sg = jax.nn.sigmoid(z_ref[:, D:].astype(F32))
        d1 = dh1[...]
        dz_ref[:, 0:D] = (d1 * sg).astype(BF16)
        dz_ref[:, D:] = (d1 * a * sg * (1.0 - sg)).astype(BF16)

        @pl.when(last)
        def _():
            dw2_ref[...] = dw2[...].astype(BF16)

    vec = pl.BlockSpec((1, D), lambda i: (0, 0))
    return _call(
        body, name="c_mix_bwd", grid=(steps,), ride=ride,
        in_specs=[pl.BlockSpec((TC, D), lambda i: (i, 0)),
                  pl.BlockSpec((HALO, D), lambda i: (_next_halo(i, t, TC), 0)),
                  pl.BlockSpec((TC, 2 * D), lambda i: (i, 0)),
                  pl.BlockSpec((HALO, 2 * D), lambda i: (_prev_halo(i, TC), 0)),
                  pl.BlockSpec((TC, D), lambda i: (i, 0)),
                  pl.BlockSpec((HALO, D), lambda i: (_next_halo(i, t, TC), 0)),
                  pl.BlockSpec((C_TAPS, D), lambda i: (0, 0)),
                  vec, vec,
                  pl.BlockSpec((D, D), lambda i: (0, 0))],
        out_specs=[pl.BlockSpec((TC, 2 * D), lambda i: (i, 0)),
                   pl.BlockSpec((D, D), lambda i: (0, 0)),
                   vec, vec, vec, vec,
                   pl.BlockSpec((C_TAPS, D), lambda i: (0, 0))],
        out_shape=[jax.ShapeDtypeStruct((t, 2 * D), BF16),
                   jax.ShapeDtypeStruct((D, D), BF16)]
                  + [jax.ShapeDtypeStruct((1, D), F32)] * 4
                  + [jax.ShapeDtypeStruct((C_TAPS, D), F32)],
        scratch_shapes=[pltpu.VMEM((HALO + TC, D), F32), pltpu.VMEM((TC + HALO, D), F32), pltpu.VMEM((D, D), F32),
                        pltpu.VMEM((SHIFTS, TC + 24, D), F32), pltpu.VMEM((TC, D), F32)],
        args=(dh, dh, z, z, h2, h2, dw, lg, lb, w2))


POOL_WINDOWS = (2, 4, 8, 16)
GW = D // len(POOL_WINDOWS)


def _pool_mixed(pad, g, w, inv_cnt):
    cols = slice(g * GW, (g + 1) * GW)
    s = pad[HALO:HALO + TM, cols]
    u = s
    for j in range(1, w):
        s = s + pad[HALO - j:HALO - j + TM, cols]
    return s * inv_cnt - u


def _inv_cnt(i, w):
    row = i * TM + lax.broadcasted_iota(jnp.int32, (TM, 1), 0)
    return 1.0 / jnp.minimum(row + 1, w).astype(F32)


def b_mix_fwd(h, gain, wg, scale, ride=None):
    t = h.shape[0]

    def body(h_ref, hp_ref, g_ref, wg_ref, sc_ref, hn_ref, pad):
        i = pl.program_id(0)
        gain = g_ref[...]
        pad[0:HALO, :] = jnp.where(i == 0, 0.0, _rms_fwd(hp_ref[...], gain))
        pad[HALO:, :] = _rms_fwd(h_ref[...], gain)
        for g, w in enumerate(POOL_WINDOWS):
            cols = slice(g * GW, (g + 1) * GW)
            mixed = _pool_mixed(pad, g, w, _inv_cnt(i, w))
            y = _dot(mixed.astype(BF16), wg_ref[g])
            hn_ref[:, cols] = h_ref[:, cols] + y * sc_ref[:, cols]

    return _call(
        body, name="b_mix_fwd", grid=(t // TM,), ride=ride,
        in_specs=[pl.BlockSpec((TM, D), lambda i: (i, 0)),
                  pl.BlockSpec((HALO, D), lambda i: (_prev_halo(i), 0)),
                  pl.BlockSpec((1, D), lambda i: (0, 0)),
                  pl.BlockSpec((4, GW, GW), lambda i: (0, 0, 0)),
                  pl.BlockSpec((1, D), lambda i: (0, 0))],
        out_specs=[pl.BlockSpec((TM, D), lambda i: (i, 0))],
        out_shape=[jax.ShapeDtypeStruct((t, D), F32)],
        scratch_shapes=[pltpu.VMEM((HALO + TM, D), F32)],
        args=(h, h, gain, wg, scale))


def b_mix_bwd(dh, h, gain, wg, scale, ride=None):
    t = h.shape[0]
    steps = t // TM

    def body(dh_ref, dhn_ref, h_ref, hp_ref, g_ref, wg_ref, sc_ref, dhp_ref, dgain_ref, dwg_ref, dsc_ref, pad, dpad, du):
        i = pl.program_id(0)
        last = i == steps - 1

        @pl.when(i == 0)
        def _():
            for r in (dgain_ref, dwg_ref, dsc_ref):
                r[...] = jnp.zeros_like(r)

        gain = g_ref[...]
        pad[0:HALO, :] = jnp.where(i == 0, 0.0, _rms_fwd(hp_ref[...], gain))
        pad[HALO:, :] = _rms_fwd(h_ref[...], gain)
        for g, w in enumerate(POOL_WINDOWS):
            cols = slice(g * GW, (g + 1) * GW)
            inv_cnt = _inv_cnt(i, w)
            mixed = _pool_mixed(pad, g, w, inv_cnt).astype(BF16)
            dh = dh_ref[:, cols]
            dsc_ref[:, cols] += jnp.sum(dh * _dot(mixed, wg_ref[g]), axis=0, keepdims=True)
            dy = (dh * sc_ref[:, cols]).astype(BF16)
            dwg_ref[g] += _dot_tn(mixed, dy)
            dm = _dot_nt(dy, wg_ref[g])
            dmn = _dot_nt((dhn_ref[:, cols] * sc_ref[:, cols]).astype(BF16), wg_ref[g])
            dpad[0:TM, cols] = dm * inv_cnt
            dpad[TM:, cols] = jnp.where(last, 0.0, dmn * (1.0 / w))
            s = dpad[0:TM, cols]
            for j in range(1, w):
                s = s + dpad[j:j + TM, cols]
            du[:, cols] = s - dm
        dx, dgain = _rms_bwd(h_ref[...], gain, du[...])
        dhp_ref[...] = dh_ref[...] + dx
        dgain_ref[...] += dgain

    return _call(
        body, name="b_mix_bwd", grid=(steps,), ride=ride,
        in_specs=[pl.BlockSpec((TM, D), lambda i: (i, 0)),
                  pl.BlockSpec((HALO, D), lambda i: (_next_halo(i, t), 0)),
                  pl.BlockSpec((TM, D), lambda i: (i, 0)),
                  pl.BlockSpec((HALO, D), lambda i: (_prev_halo(i), 0)),
                  pl.BlockSpec((1, D), lambda i: (0, 0)),
                  pl.BlockSpec((4, GW, GW), lambda i: (0, 0, 0)),
                  pl.BlockSpec((1, D), lambda i: (0, 0))],
        out_specs=[pl.BlockSpec((TM, D), lambda i: (i, 0)),
                   pl.BlockSpec((1, D), lambda i: (0, 0)),
                   pl.BlockSpec((4, GW, GW), lambda i: (0, 0, 0)),
                   pl.BlockSpec((1, D), lambda i: (0, 0))],
        out_shape=[jax.ShapeDtypeStruct((t, D), F32),
                   jax.ShapeDtypeStruct((1, D), F32),
                   jax.ShapeDtypeStruct((4, GW, GW), F32),
                   jax.ShapeDtypeStruct((1, D), F32)],
        scratch_shapes=[pltpu.VMEM((HALO + TM, D), F32), pltpu.VMEM((TM + HALO, D), F32), pltpu.VMEM((TM, D), F32)],
        args=(dh, dh, h, h, gain, wg, scale))


LOSS_LANES = 128


def loss_head(h, gain, target):
    t = h.shape[0]

    def body(h_ref, g_ref, tg_ref, loss_ref, dh_ref, dgain_ref):
        @pl.when(pl.program_id(0) == 0)
        def _():
            loss_ref[...] = jnp.zeros_like(loss_ref)
            dgain_ref[...] = jnp.zeros_like(dgain_ref)

        x, gain = h_ref[...], g_ref[...]
        err = _rms_fwd(x, gain) - tg_ref[...]
        per_row = jnp.mean(err * err, axis=-1, keepdims=True)
        loss_ref[...] += jnp.broadcast_to(0.5 * jnp.sum(per_row, axis=0, keepdims=True), (1, LOSS_LANES))
        dx, dgain = _rms_bwd(x, gain, err * (1.0 / D))
        dh_ref[...] = dx
        dgain_ref[...] += dgain

    outs, _ = _call(
        body, name="loss_head", grid=(t // TM,),
        in_specs=[pl.BlockSpec((TM, D), lambda i: (i, 0)),
                  pl.BlockSpec((1, D), lambda i: (0, 0)),
                  pl.BlockSpec((TM, D), lambda i: (i, 0))],
        out_specs=[pl.BlockSpec((1, LOSS_LANES), lambda i: (0, 0)),
                   pl.BlockSpec((TM, D), lambda i: (i, 0)),
                   pl.BlockSpec((1, D), lambda i: (0, 0))],
        out_shape=[jax.ShapeDtypeStruct((1, LOSS_LANES), F32),
                   jax.ShapeDtypeStruct((t, D), F32),
                   jax.ShapeDtypeStruct((1, D), F32)],
        args=(h, gain, target))
    return outs


ADAM_LR = 0.001
ADAM_B1 = 0.9
ADAM_B2 = 0.999
ADAM_EPS = 1e-08
ADAM_WD = 0.01
ADAM_STEP = 10
ADAM_VMEM = 40 * 1024 * 1024


def cast_all(arrays):
    def body(*refs):
        for src, dst in zip(refs[:len(arrays)], refs[len(arrays):]):
            dst[...] = src[...].astype(BF16)

    return pl.pallas_call(
        body, name="cast_all", out_shape=[jax.ShapeDtypeStruct(a.shape, BF16) for a in arrays],
        compiler_params=pltpu.CompilerParams(vmem_limit_bytes=VMEM_LIMIT),
    )(*arrays)


def _adam_math(w, m, v, g):
    m = ADAM_B1 * m + (1.0 - ADAM_B1) * g
    v = ADAM_B2 * v + (1.0 - ADAM_B2) * (g * g)
    m_hat = m / (1.0 - ADAM_B1 ** ADAM_STEP)
    v_hat = v / (1.0 - ADAM_B2 ** ADAM_STEP)
    return -ADAM_LR * (m_hat / (jnp.sqrt(v_hat) + ADAM_EPS) + ADAM_WD * w), m, v


def adamw(ws, ms, vs, gps, rb, tokens=()):
    n = len(ws)
    r, c = ws[0].shape
    nb = r // rb

    def body(*refs):
        i = pl.program_id(0)
        outs = refs[4 * n + len(tokens):]
        for j in range(n):
            w_ref, m_ref, v_ref, gp_ref = (refs[q * n + j] for q in range(4))
            g_ref, d_ref, nm_ref, nv_ref = (outs[q * n + j] for q in range(4))

            @pl.when(i // nb == j)
            def _():
                g = gp_ref[0].astype(F32)
                for s in range(1, N_DEV):
                    g = g + gp_ref[s].astype(F32)
                g_ref[...] = g
                d_ref[...], nm_ref[...], nv_ref[...] = _adam_math(w_ref[...], m_ref[...], v_ref[...], g)

    def blk(j):
        return pl.BlockSpec((rb, c), lambda i: (jnp.clip(i - j * nb, 0, nb - 1), 0))

    def gblk(j):
        return pl.BlockSpec((N_DEV, rb, c), lambda i: (0, jnp.clip(i - j * nb, 0, nb - 1), 0))

    outs, _ = _call(
        body, name=f"adamw_{n}x{r}x{c}", grid=(n * nb,),
        in_specs=[blk(j) for _ in range(3) for j in range(n)] + [gblk(j) for j in range(n)] + [ANY] * len(tokens),
        out_specs=[blk(j) for _ in range(4) for j in range(n)],
        out_shape=[jax.ShapeDtypeStruct((r, c), F32)] * (4 * n),
        args=(*ws, *ms, *vs, *gps, *tokens))
    return outs[:n], outs[n:2 * n], outs[2 * n:3 * n], outs[3 * n:]


def adamw_vectors(ws, ms, vs, gparts):
    nv = len(ws)

    def body(*refs):
        w_refs, m_refs, v_refs = refs[:nv], refs[nv:2 * nv], refs[2 * nv:3 * nv]
        gp_ref = refs[3 * nv]
        outs = refs[3 * nv + 1:]
        g_refs, d_refs, nm_refs, nv_refs = outs[:nv], outs[nv:2 * nv], outs[2 * nv:3 * nv], outs[3 * nv:]
        row = 0
        for i in range(nv):
            for part in range(w_refs[i].shape[1] // D):
                cols = slice(part * D, (part + 1) * D)
                g = gp_ref[0, row:row + 1, :]
                for s in range(1, N_DEV):
                    g = g + gp_ref[s, row:row + 1, :]
                g_refs[i][:, cols] = g
                d_refs[i][:, cols], nm_refs[i][:, cols], nv_refs[i][:, cols] = _adam_math(
                    w_refs[i][:, cols], m_refs[i][:, cols], v_refs[i][:, cols], g)
                row += 1

    shapes = [jax.ShapeDtypeStruct(w.shape, F32) for w in ws]
    outs = pl.pallas_call(body, name="adamw_vectors", out_shape=shapes * 4)(*ws, *ms, *vs, gparts)
    return outs[:nv], outs[nv:2 * nv], outs[2 * nv:3 * nv], outs[3 * nv:]


WEIGHTS = ["ln1_0", "a0_w_in", "a0_conv", "a0_w_out", "ln2_0", "ffn0_w_gu", "ffn0_w_down",
           "ln1_1", "b1_w_grp", "b1_scale", "ln2_1", "ffn1_w_gu", "ffn1_w_down",
           "ln1_2", "c2_w_pw1", "c2_b_pw1", "c2_dw", "c2_b_dw", "c2_ln_g", "c2_ln_b", "c2_w_pw2", "c2_b_pw2",
           "ln2_2", "ffn2_w_gu", "ffn2_w_down",
           "ln1_3", "a3_w_in", "a3_conv", "a3_w_out", "ln2_3", "ffn3_w_gu", "ffn3_w_down", "ln_f"]
SHARDED = {"a0_w_in": ("cols", 256), "a0_conv": ("cols", A_TAPS), "a0_w_out": ("rows", 128),
           "ffn0_w_gu": ("lead", 176), "ffn0_w_down": ("rows", 176),
           "b1_w_grp": ("mid", 128),
           "ffn1_w_gu": ("lead", 176), "ffn1_w_down": ("rows", 176),
           "c2_w_pw1": ("cols", 256), "c2_dw": ("cols", C_TAPS), "c2_w_pw2": ("rows", 128),
           "ffn2_w_gu": ("lead", 176), "ffn2_w_down": ("rows", 176),
           "a3_w_in": ("cols", 256), "a3_conv": ("cols", A_TAPS), "a3_w_out": ("rows", 128),
           "ffn3_w_gu": ("lead", 176), "ffn3_w_down": ("rows", 176)}
IN_PROJ = ("a0_w_in", "c2_w_pw1", "a3_w_in")
REPL = [n for n in WEIGHTS if n not in SHARDED]
REPL_ROWS = 16
GATHER_PLAN = {"first": ["a0_w_in", "a0_w_out", "a0_conv"],
               "in0": ["ffn0_w_gu"], "mix0": ["ffn0_w_down"],
               "ffn0": ["b1_w_grp", "ffn1_w_gu", "ffn1_w_down"],
               "ffn1": ["c2_w_pw1", "c2_w_pw2", "c2_dw", "ffn2_w_gu"],
               "in2": ["ffn2_w_down"],
               "mix2": ["a3_w_in", "a3_w_out", "a3_conv"],
               "ffn2": ["ffn3_w_gu", "ffn3_w_down"]}
SCATTER_PLAN = {"mixb3": ["ffn3_w_down"], "inw3": ["a3_w_out", "a3_conv"],
                "ffnx2": ["ffn3_w_gu"], "ffnw2": ["a3_w_in"],
                "mixb2": ["ffn2_w_gu", "ffn2_w_down"], "inw2": ["c2_w_pw2", "c2_dw"],
                "ffnx1": ["c2_w_pw1"],
                "ffnx0": ["ffn1_w_gu"], "ffnw0": ["ffn1_w_down", "b1_w_grp"],
                "last": ["repl"]}
LATE_FFN = ["ffn0_w_gu", "ffn0_w_down"]
LATE_MIX = ["a0_w_in", "a0_w_out", "a0_conv"]


def _step(p):
    vec = lambda n: p[n].reshape(1, -1)
    x, target = p["x"][0], p["loss_target"][0]

    names = list(SHARDED)
    stored = lambda n, a: a.T if n.endswith("w_gu") else a
    shard = dict(zip(names, cast_all([stored(n, p[n]) for n in names])))
    full = {}

    def gather(slot):
        names = GATHER_PLAN[slot]
        return gather_ride([shard[n] for n in names], ["cols" if n in IN_PROJ else "lead" for n in names])

    def landed(slot, outs):
        full.update(zip(GATHER_PLAN[slot], outs))

    def conv_full(n):
        k = full[n].shape[1]
        return full[n].transpose(1, 0, 2).reshape(k, D).astype(F32)

    def wgu(i):
        return full[f"ffn{i}_w_gu"].reshape(2, 4, FC, D)

    def wd(i):
        return full[f"ffn{i}_w_down"].reshape(4, FC, D)

    landed("first", run_ride(gather("first"), "gather_first"))
    no_bias = jnp.zeros((1, 3 * D), F32)
    h = [x]
    saved = {}
    conv, wout = {}, {}

    (z, u), got = rms_matmul(h[-1], vec("ln1_0"), full["a0_w_in"], no_bias, ride=gather("in0"))
    landed("in0", got)
    conv[0], wout[0] = conv_full("a0_conv"), full["a0_w_out"].reshape(D, D)
    (hm,), got = a_mix_fwd(z, h[-1], conv[0], wout[0], ride=gather("mix0"))
    landed("mix0", got)
    saved["mix0"] = (z, u)
    h.append(hm)
    (hn, zf, uf), got = ffn_fwd(hm, vec("ln2_0"), wgu(0), wd(0), ride=gather("ffn0"))
    landed("ffn0", got)
    saved["ffn0"] = (zf, uf)
    h.append(hn)

    wgrp = full["b1_w_grp"].transpose(1, 0, 2, 3).reshape(4, GW, GW)
    (hm,), _ = b_mix_fwd(h[-1], vec("ln1_1"), wgrp, vec("b1_scale"))
    h.append(hm)
    (hn, zf, uf), got = ffn_fwd(hm, vec("ln2_1"), wgu(1), wd(1), ride=gather("ffn1"))
    landed("ffn1", got)
    saved["ffn1"] = (zf, uf)
    h.append(hn)

    (z, u), got = rms_matmul(h[-1], vec("ln1_2"), full["c2_w_pw1"], vec("c2_b_pw1"), ride=gather("in2"))
    landed("in2", got)
    cdw, wpw2 = conv_full("c2_dw"), full["c2_w_pw2"].reshape(D, D)
    (hm, h2), got = c_mix_fwd(z, h[-1], cdw, vec("c2_b_dw"), vec("c2_ln_g"), vec("c2_ln_b"), wpw2, vec("c2_b_pw2"),
                              ride=gather("mix2"))
    landed("mix2", got)
    saved["mix2"] = (z, u, h2)
    h.append(hm)
    (hn, zf, uf), got = ffn_fwd(hm, vec("ln2_2"), wgu(2), wd(2), ride=gather("ffn2"))
    landed("ffn2", got)
    saved["ffn2"] = (zf, uf)
    h.append(hn)

    (z, u), _ = rms_matmul(h[-1], vec("ln1_3"), full["a3_w_in"], no_bias)
    conv[3], wout[3] = conv_full("a3_conv"), full["a3_w_out"].reshape(D, D)
    (hm,), _ = a_mix_fwd(z, h[-1], conv[3], wout[3])
    saved["mix3"] = (z, u)
    h.append(hm)
    (hn, zf, uf), _ = ffn_fwd(hm, vec("ln2_3"), wgu(3), wd(3))
    saved["ffn3"] = (zf, uf)
    h.append(hn)

    loss_lanes, dh, g_lnf = loss_head(h[-1], vec("ln_f"), target)

    g = {"ln_f": g_lnf}
    recv = {}

    def repl_rows():
        loss_row = jnp.pad(loss_lanes, ((0, 0), (0, D - LOSS_LANES)))
        return jnp.concatenate([g[n].reshape(-1, D) for n in REPL] + [loss_row], axis=0)

    def scatter(slot):
        parts = []
        for n in SCATTER_PLAN.get(slot, []):
            parts.append((repl_rows(), "all") if n == "repl" else (g[n], SHARDED[n][0]))
        return scatter_ride(parts) if parts else None

    def arrived(slot, outs):
        recv.update(zip(SCATTER_PLAN.get(slot, []), outs))

    for i in (3, 2, 1, 0):
        zf, uf = saved[f"ffn{i}"]
        (dh_prev, dzf, a, g[f"ln2_{i}"], dhb), got = ffn_bwd_x(dh, h[2 * i + 1], vec(f"ln2_{i}"), zf, wgu(i), wd(i),
                                                             ride=scatter(f"ffnx{i}"))
        arrived(f"ffnx{i}", got)
        dh = dh_prev
        (dwgu, dwd), got = ffn_bwd_w(uf, dzf, a, dhb, ride=scatter(f"ffnw{i}"))
        arrived(f"ffnw{i}", got)
        g[f"ffn{i}_w_gu"], g[f"ffn{i}_w_down"] = dwgu.reshape(N_DEV, FC, D), dwd.reshape(FF, D)
        if i == 0:
            late_ffn = scatter_start([(g[n], SHARDED[n][0]) for n in LATE_FFN], "late_ffn_start")
        hin = h[2 * i]
        if i in (0, 3):
            z, u = saved[f"mix{i}"]
            (dz, g[f"a{i}_w_out"], g[f"a{i}_conv"]), got = a_mix_bwd(dh, z, conv[i], wout[i], ride=scatter(f"mixb{i}"))
            arrived(f"mixb{i}", got)
            (g[f"a{i}_w_in"],), got = in_proj_bwd_w(u, dz, ride=scatter(f"inw{i}"))
            arrived(f"inw{i}", got)
            if i == 0:
                late_mix = scatter_start([(g[n], SHARDED[n][0]) for n in LATE_MIX], "late_mix_start")
            (dh, g[f"ln1_{i}"], _), got = in_proj_bwd_x(dz, full[f"a{i}_w_in"], hin, vec(f"ln1_{i}"), dh,
                                                       ride=scatter(f"inx{i}"))
            arrived(f"inx{i}", got)
        elif i == 1:
            (dh, g["ln1_1"], g["b1_w_grp"], g["b1_scale"]), got = b_mix_bwd(dh, hin, vec("ln1_1"), wgrp, vec("b1_scale"),
                                                                             ride=scatter("mixb1"))
            arrived("mixb1", got)
        else:
            z, u, h2 = saved["mix2"]
            (dz, g["c2_w_pw2"], g["c2_b_pw2"], g["c2_ln_g"], g["c2_ln_b"], g["c2_b_dw"], g["c2_dw"]), got = c_mix_bwd(
                dh, z, h2, cdw, vec("c2_ln_g"), vec("c2_ln_b"), wpw2, ride=scatter("mixb2"))
            arrived("mixb2", got)
            (g["c2_w_pw1"],), got = in_proj_bwd_w(u, dz, ride=scatter("inw2"))
            arrived("inw2", got)
            (dh, g["ln1_2"], g["c2_b_pw1"]), got = in_proj_bwd_x(dz, full["c2_w_pw1"], hin, vec("ln1_2"), dh,
                                                                ride=scatter("inx2"))
            arrived("inx2", got)
    grad_x = dh[None]
    arrived("last", run_ride(scatter("last"), "scatter_last"))

    grad, delta, new_m, new_v = {}, {}, {}, {}
    two_d = lambda n, a: stored(n, a.reshape(-1, p[n].shape[-1]))

    def adam_calls(names_, tokens):
        groups, last = {}, None
        for n in names_:
            groups.setdefault((two_d(n, p[n]).shape, SHARDED[n][1]), []).append(n)
        for (shape, rb), members in groups.items():
            per_weight = 2 * rb * shape[1] * (7 * 4 + N_DEV * recv[members[0]].dtype.itemsize)
            at_once = max(1, (ADAM_VMEM // per_weight))
            for lo in range(0, len(members), at_once):
                ns = members[lo:lo + at_once]
                outs = adamw([two_d(n, p[n]) for n in ns], [two_d(n, p["m_" + n]) for n in ns],
                             [two_d(n, p["v_" + n]) for n in ns], [recv[n].reshape(N_DEV, *shape) for n in ns], rb, tokens)
                for res, o in zip((grad, delta, new_m, new_v), outs):
                    res.update({n: stored(n, a).reshape(p[n].shape) for n, a in zip(ns, o)})
                last = outs[0][0]
        return last

    early_done = adam_calls([n for n in SHARDED if n not in LATE_FFN + LATE_MIX], (late_ffn[-1], late_mix[-1]))
    outs = adamw_vectors([vec(n) for n in REPL], [vec("m_" + n) for n in REPL], [vec("v_" + n) for n in REPL], recv["repl"])
    for res, o in zip((grad, delta, new_m, new_v), outs):
        res.update({n: a.reshape(p[n].shape) for n, a in zip(REPL, o)})
    behind = [early_done, outs[0][0]]
    recv.update(zip(LATE_FFN, scatter_wait(late_ffn, behind, "late_ffn_wait")))
    recv.update(zip(LATE_MIX, scatter_wait(late_mix, behind, "late_mix_wait")))
    adam_calls(LATE_FFN + LATE_MIX, ())

    loss = jnp.sum(recv["repl"][:, REPL_ROWS, 0])
    return (loss, grad_x, *[grad[n] for n in WEIGHTS], *[delta[n] for n in WEIGHTS],
            *[new_m[n] for n in WEIGHTS], *[new_v[n] for n in WEIGHTS])


def kernel(x, ln1_0, a0_w_in, a0_conv, a0_w_out, ln2_0, ffn0_w_gu, ffn0_w_down, ln1_1, b1_w_grp, b1_scale, ln2_1, ffn1_w_gu, ffn1_w_down, ln1_2, c2_w_pw1, c2_b_pw1, c2_dw, c2_b_dw, c2_ln_g, c2_ln_b, c2_w_pw2, c2_b_pw2, ln2_2, ffn2_w_gu, ffn2_w_down, ln1_3, a3_w_in, a3_conv, a3_w_out, ln2_3, ffn3_w_gu, ffn3_w_down, ln_f, loss_target, m_ln1_0, m_a0_w_in, m_a0_conv, m_a0_w_out, m_ln2_0, m_ffn0_w_gu, m_ffn0_w_down, m_ln1_1, m_b1_w_grp, m_b1_scale, m_ln2_1, m_ffn1_w_gu, m_ffn1_w_down, m_ln1_2, m_c2_w_pw1, m_c2_b_pw1, m_c2_dw, m_c2_b_dw, m_c2_ln_g, m_c2_ln_b, m_c2_w_pw2, m_c2_b_pw2, m_ln2_2, m_ffn2_w_gu, m_ffn2_w_down, m_ln1_3, m_a3_w_in, m_a3_conv, m_a3_w_out, m_ln2_3, m_ffn3_w_gu, m_ffn3_w_down, m_ln_f, v_ln1_0, v_a0_w_in, v_a0_conv, v_a0_w_out, v_ln2_0, v_ffn0_w_gu, v_ffn0_w_down, v_ln1_1, v_b1_w_grp, v_b1_scale, v_ln2_1, v_ffn1_w_gu, v_ffn1_w_down, v_ln1_2, v_c2_w_pw1, v_c2_b_pw1, v_c2_dw, v_c2_b_dw, v_c2_ln_g, v_c2_ln_b, v_c2_w_pw2, v_c2_b_pw2, v_ln2_2, v_ffn2_w_gu, v_ffn2_w_down, v_ln1_3, v_a3_w_in, v_a3_conv, v_a3_w_out, v_ln2_3, v_ffn3_w_gu, v_ffn3_w_down, v_ln_f):
    return _step(dict(locals()))
```

```python
import jax
import jax.numpy as jnp
from jax import lax
from jax.experimental import pallas as pl
from jax.experimental.pallas import tpu as pltpu

F32 = jnp.float32
BF16 = jnp.bfloat16

N_DEV = 8
D = 1024
FF = 2816
FC = FF // 4
RMS_EPS = 1e-6
LN_EPS = 1e-5
TM = 512
HALO = 32
VMEM_LIMIT = 60 * 1024 * 1024

NT = (((1,), (1,)), ((), ()))
TN = (((0,), (0,)), ((), ()))
MESH = pl.DeviceIdType.MESH
ANY = pl.BlockSpec(memory_space=pl.ANY)
N_PEERS = N_DEV - 1


def _dot(a, b):
    return jnp.dot(a, b, preferred_element_type=F32)


def _dot_nt(a, b):
    return lax.dot_general(a, b, NT, preferred_element_type=F32)


def _dot_tn(a, b):
    return lax.dot_general(a, b, TN, preferred_element_type=F32)


def _rms_fwd(x, gain):
    r = lax.rsqrt(jnp.mean(x * x, axis=-1, keepdims=True) + RMS_EPS)
    return x * r * gain


def _rms_bwd(x, gain, du):
    r = lax.rsqrt(jnp.mean(x * x, axis=-1, keepdims=True) + RMS_EPS)
    xhat = x * r
    dgain = jnp.sum(du * xhat, axis=0, keepdims=True)
    dxhat = du * gain
    dx = r * (dxhat - xhat * jnp.mean(dxhat * xhat, axis=-1, keepdims=True))
    return dx, dgain


def _dev_index(p):
    return 4 * p[0] + 2 * p[1] + p[2]


def _place():
    return lax.axis_index("x"), lax.axis_index("y"), lax.axis_index("c")


class Ride:
    def __init__(self, ins, out_shapes, start, finish):
        self.ins, self.out_shapes, self.start, self.finish = list(ins), list(out_shapes), start, finish
        n = len(self.ins)
        self.sems = [pltpu.SemaphoreType.DMA((n * N_PEERS,)), pltpu.SemaphoreType.DMA((n * N_PEERS,)),
                     pltpu.SemaphoreType.DMA((n,))]


def gather_ride(shards, kinds):
    n = len(shards)

    def setup(ins, outs, sems):
        send_sems, recv_sems, local_sems = sems
        x, y, c = _place()
        chips = [(1 - x, y), (x, 1 - y), (1 - x, 1 - y)]

        def copy(a, k, block, to, src=None):
            slot = _chunk(outs[a], kinds[a], _dev_index(block))
            return pltpu.make_async_remote_copy(
                src_ref=slot if src is None else src, dst_ref=slot,
                send_sem=send_sems.at[a * N_PEERS + k], recv_sem=recv_sems.at[a * N_PEERS + k],
                device_id=to, device_id_type=MESH)

        def mine(a):
            return pltpu.make_async_copy(ins[a], _chunk(outs[a], kinds[a], _dev_index((x, y, c))), local_sems.at[a])

        def first(a):
            return [copy(a, 0, (x, y, c), (x, y, 1 - c), src=ins[a])] + [
                copy(a, 1 + j, (x, y, c), (*chip, c), src=ins[a]) for j, chip in enumerate(chips)]

        return (x, y, c), chips, copy, mine, first

    def start(ins, outs, sems):
        _, _, _, mine, first = setup(ins, outs, sems)
        for a in range(n):
            mine(a).start()
            for cp in first(a):
                cp.start()

    def finish(ins, outs, sems):
        (x, y, c), chips, copy, mine, first = setup(ins, outs, sems)
        me, sibling = (x, y, c), (x, y, 1 - c)
        for a in range(n):
            for j, chip in enumerate(chips):
                copy(a, 1 + j, (*chip, c), me).wait_recv()
                copy(a, 4 + j, (*chip, c), sibling).start()
        for a in range(n):
            copy(a, 0, sibling, me).wait_recv()
            for j, chip in enumerate(chips):
                copy(a, 4 + j, (*chip, 1 - c), me).wait_recv()
        for a in range(n):
            for cp in first(a):
                cp.wait_send()
            for j, chip in enumerate(chips):
                copy(a, 4 + j, (*chip, c), sibling).wait_send()
        for a in range(n):
            mine(a).wait()

    shapes = [(N_DEV, *s.shape) if kind == "lead" else (s.shape[0], N_DEV * s.shape[1]) for s, kind in zip(shards, kinds)]
    return Ride(shards, [jax.ShapeDtypeStruct(shape, s.dtype) for shape, s in zip(shapes, shards)], start, finish)


def _chunk(ref, kind, j):
    if kind == "lead":
        return ref.at[j]
    if kind == "rows":
        r = ref.shape[0] // N_DEV
        return ref.at[pl.ds(j * r, r)]
    if kind == "mid":
        r = ref.shape[1] // N_DEV
        return ref.at[:, pl.ds(j * r, r), :]
    if kind == "cols":
        c = ref.shape[1] // N_DEV
        return ref.at[:, pl.ds(j * c, c)]
    return ref


def _chunk_shape(shape, kind):
    if kind == "lead":
        return tuple(shape[1:])
    if kind == "rows":
        return (shape[0] // N_DEV, *shape[1:])
    if kind == "mid":
        return (shape[0], shape[1] // N_DEV, shape[2])
    if kind == "cols":
        return (shape[0], shape[1] // N_DEV)
    return tuple(shape)


def scatter_ride(parts):
    n = len(parts)
    kinds = [k for _, k in parts]

    def setup(ins, outs, sems):
        send_sems, recv_sems, local_sems = sems
        x, y, c = _place()
        me = _dev_index((x, y, c))
        peers = []
        for k in range(1, N_DEV):
            kx, ky, kc = (k >> 2) & 1, (k >> 1) & 1, k & 1
            peers.append((1 - x if kx else x, 1 - y if ky else y, 1 - c if kc else c))

        def copy(a, k, peer):
            return pltpu.make_async_remote_copy(
                src_ref=_chunk(ins[a], kinds[a], _dev_index(peer)), dst_ref=outs[a].at[me],
                send_sem=send_sems.at[a * N_PEERS + k], recv_sem=recv_sems.at[a * N_PEERS + k],
                device_id=peer, device_id_type=MESH)

        def arrival(a, k, peer):
            slot = outs[a].at[_dev_index(peer)]
            return pltpu.make_async_remote_copy(
                src_ref=slot, dst_ref=slot,
                send_sem=send_sems.at[a * N_PEERS + k], recv_sem=recv_sems.at[a * N_PEERS + k],
                device_id=peer, device_id_type=MESH)

        def mine(a):
            return pltpu.make_async_copy(_chunk(ins[a], kinds[a], me), outs[a].at[me], local_sems.at[a])

        return peers, copy, arrival, mine

    def start(ins, outs, sems):
        peers, copy, _, mine = setup(ins, outs, sems)
        for a in range(n):
            mine(a).start()
            for k, peer in enumerate(peers):
                copy(a, k, peer).start()

    def finish(ins, outs, sems):
        peers, copy, arrival, mine = setup(ins, outs, sems)
        for a in range(n):
            for k, peer in enumerate(peers):
                arrival(a, k, peer).wait_recv()
        for a in range(n):
            for k, peer in enumerate(peers):
                copy(a, k, peer).wait_send()
            mine(a).wait()

    shapes = [jax.ShapeDtypeStruct((N_DEV, *_chunk_shape(arr.shape, kind)), arr.dtype) for arr, kind in parts]
    return Ride([arr for arr, _ in parts], shapes, start, finish)


HBM = pl.BlockSpec(memory_space=pltpu.HBM)
SEM = pl.BlockSpec(memory_space=pltpu.SEMAPHORE)
DATAFLOW = pltpu.SideEffectType.DATAFLOW_SIDE_EFFECTING
TOKEN = (8, 128)


def _scatter_copies(kinds, ins, lands, send_sems, recv_sems):
    x, y, c = _place()
    me = _dev_index((x, y, c))
    sends, arrivals = [], []
    for a, kind in enumerate(kinds):
        for k in range(1, N_DEV):
            kx, ky, kc = (k >> 2) & 1, (k >> 1) & 1, k & 1
            peer = (1 - x if kx else x, 1 - y if ky else y, 1 - c if kc else c)
            sem = a * N_PEERS + k - 1
            sends.append(pltpu.make_async_remote_copy(
                src_ref=_chunk(ins[a], kind, _dev_index(peer)), dst_ref=lands[a].at[me],
                send_sem=send_sems.at[sem], recv_sem=recv_sems.at[sem], device_id=peer, device_id_type=MESH))
            slot = lands[a].at[_dev_index(peer)]
            arrivals.append(pltpu.make_async_remote_copy(
                src_ref=slot, dst_ref=slot, send_sem=send_sems.at[sem], recv_sem=recv_sems.at[sem],
                device_id=peer, device_id_type=MESH))
    return me, sends, arrivals


def scatter_start(parts, name):
    n = len(parts)
    kinds = [k for _, k in parts]
    arrays = [pltpu.with_memory_space_constraint(a, pltpu.HBM) for a, _ in parts]
    zones = [pltpu.with_memory_space_constraint(lax.empty((N_DEV, *_chunk_shape(a.shape, k)), a.dtype), pltpu.HBM)
             for a, k in parts]

    def body(*refs):
        ins, lands = refs[:n], refs[n:2 * n]
        send_sems, recv_sems = refs[2 * n], refs[2 * n + 1]
        token = refs[4 * n + 2]
        _, sends, _ = _scatter_copies(kinds, ins, lands, send_sems, recv_sems)
        for cp in sends:
            cp.start()
        token[...] = jnp.zeros_like(token)

    outs = pl.pallas_call(
        body, name=name,
        out_shape=(pltpu.SemaphoreType.DMA((n * N_PEERS,)), pltpu.SemaphoreType.DMA((n * N_PEERS,)),
                   *[pltpu.HBM(a.shape, a.dtype) for a in arrays], *[pltpu.HBM(z.shape, z.dtype) for z in zones],
                   jax.ShapeDtypeStruct(TOKEN, F32)),
        in_specs=[HBM] * (2 * n),
        out_specs=(SEM, SEM, *[HBM] * (2 * n), pl.BlockSpec(memory_space=pltpu.VMEM)),
        input_output_aliases={i: 2 + i for i in range(2 * n)},
        compiler_params=pltpu.CompilerParams(has_side_effects=DATAFLOW),
    )(*arrays, *zones)
    return kinds, outs[0], outs[1], outs[2:2 + n], outs[2 + n:2 + 2 * n], outs[2 + 2 * n]


def scatter_wait(started, after, name):
    kinds, send_sems, recv_sems, arrays, zones, _ = started
    n = len(kinds)

    def body(*refs):
        ins, lands = refs[:n], refs[n:2 * n]
        local_sems = refs[-1]
        me, sends, arrivals = _scatter_copies(kinds, ins, lands, refs[2 * n], refs[2 * n + 1])
        own = [pltpu.make_async_copy(_chunk(ins[a], kinds[a], me), lands[a].at[me], local_sems.at[a]) for a in range(n)]
        for cp in own:
            cp.start()
        for cp in sends:
            cp.wait_send()
        for cp in arrivals:
            cp.wait_recv()
        for cp in own:
            cp.wait()

    outs = pl.pallas_call(
        body, name=name,
        out_shape=(*[pltpu.HBM(a.shape, a.dtype) for a in arrays], *[pltpu.HBM(z.shape, z.dtype) for z in zones]),
        in_specs=[HBM] * (2 * n) + [SEM, SEM] + [ANY] * len(after),
        out_specs=[HBM] * (2 * n),
        input_output_aliases={i: i for i in range(2 * n)},
        scratch_shapes=[pltpu.SemaphoreType.DMA((n,))],
        compiler_params=pltpu.CompilerParams(has_side_effects=DATAFLOW),
    )(*arrays, *zones, send_sems, recv_sems, *after)
    return outs[n:]


def run_ride(ride, name):
    n_in, n_out = len(ride.ins), len(ride.out_shapes)

    def body(*refs):
        ins, outs, sems = refs[:n_in], refs[n_in:n_in + n_out], refs[n_in + n_out:]
        ride.start(ins, outs, sems)
        ride.finish(ins, outs, sems)

    return pl.pallas_call(
        body, name=name, in_specs=[ANY] * n_in, out_specs=[ANY] * n_out, out_shape=ride.out_shapes,
        scratch_shapes=ride.sems,
    )(*ride.ins)


def _call(body, *, name, grid, in_specs, out_specs, out_shape, args, scratch_shapes=(), ride=None):
    params = pltpu.CompilerParams(dimension_semantics=("arbitrary",) * len(grid), vmem_limit_bytes=VMEM_LIMIT)
    if ride is None:
        outs = pl.pallas_call(body, name=name, grid=grid, in_specs=in_specs, out_specs=out_specs, out_shape=out_shape,
                              scratch_shapes=list(scratch_shapes), compiler_params=params)(*args)
        return outs, []
    n_in, n_out, n_scr = len(in_specs), len(out_specs), len(scratch_shapes)
    r_in, r_out = len(ride.ins), len(ride.out_shapes)

    def hosted(*refs):
        ins, refs = refs[:n_in], refs[n_in:]
        rins, refs = refs[:r_in], refs[r_in:]
        outs, refs = refs[:n_out], refs[n_out:]
        routs, refs = refs[:r_out], refs[r_out:]
        scratch, sems = refs[:n_scr], refs[n_scr:]
        step, n_steps = pl.program_id(0), grid[0]
        for d in range(1, len(grid)):
            step, n_steps = step * grid[d] + pl.program_id(d), n_steps * grid[d]

        @pl.when(step == 0)
        def _():
            ride.start(rins, routs, sems)

        body(*ins, *outs, *scratch)

        @pl.when(step == n_steps - 1)
        def _():
            ride.finish(rins, routs, sems)

    outs = pl.pallas_call(
        hosted, name=name + "_ride", grid=grid,
        in_specs=list(in_specs) + [ANY] * r_in, out_specs=list(out_specs) + [ANY] * r_out,
        out_shape=list(out_shape) + ride.out_shapes,
        scratch_shapes=list(scratch_shapes) + ride.sems, compiler_params=params,
    )(*args, *ride.ins)
    return outs[:n_out], outs[n_out:]


def ffn_fwd(h, gain, wgu, wd, ride=None):
    t = h.shape[0]
    tf = min(TF, t)

    def body(h_ref, g_ref, wgu_ref, wd_ref, hn_ref, z_ref, u_ref, acc):
        k = pl.program_id(1)

        @pl.when(k == 0)
        def _():
            u_ref[...] = _rms_fwd(h_ref[...], g_ref[...]).astype(BF16)
            acc[...] = jnp.zeros_like(acc)

        u = u_ref[...]
        g = _dot_nt(u, wgu_ref[0, 0])
        up = _dot_nt(u, wgu_ref[1, 0])
        z_ref[0, 0] = g.astype(BF16)
        z_ref[1, 0] = up.astype(BF16)
        a = g * jax.nn.sigmoid(g) * up
        acc[...] += _dot(a.astype(BF16), wd_ref[0])

        @pl.when(k == 3)
        def _():
            hn_ref[...] = h_ref[...] + acc[...]

    return _call(
        body, name="ffn_fwd", grid=(t // tf, 4), ride=ride,
        in_specs=[pl.BlockSpec((tf, D), lambda i, k: (i, 0)),
                  pl.BlockSpec((1, D), lambda i, k: (0, 0)),
                  pl.BlockSpec((2, 1, FC, D), lambda i, k: (0, k, 0, 0)),
                  pl.BlockSpec((1, FC, D), lambda i, k: (k, 0, 0))],
        out_specs=[pl.BlockSpec((tf, D), lambda i, k: (i, 0)),
                   pl.BlockSpec((2, 1, tf, FC), lambda i, k: (0, k, i, 0)),
                   pl.BlockSpec((tf, D), lambda i, k: (i, 0))],
        out_shape=[jax.ShapeDtypeStruct((t, D), F32),
                   jax.ShapeDtypeStruct((2, 4, t, FC), BF16),
                   jax.ShapeDtypeStruct((t, D), BF16)],
        scratch_shapes=[pltpu.VMEM((tf, D), F32)],
        args=(h, gain, wgu, wd))


def ffn_bwd_x(dh, h, gain, z, wgu, wd, ride=None):
    t = h.shape[0]

    def body(dh_ref, h_ref, g_ref, z_ref, wgu_ref, wd_ref, dhp_ref, dz_ref, a_ref, dgain_ref, dhb, du):
        i, k = pl.program_id(0), pl.program_id(1)

        @pl.when(k == 0)
        def _():
            dhb[...] = dh_ref[...].astype(BF16)
            du[...] = jnp.zeros_like(du)

        @pl.when((k == 0) & (i == 0))
        def _():
            dgain_ref[...] = jnp.zeros_like(dgain_ref)

        da = _dot_nt(dhb[...], wd_ref[0])
        g = z_ref[0, 0].astype(F32)
        up = z_ref[1, 0].astype(F32)
        sg = jax.nn.sigmoid(g)
        silu = g * sg
        a_ref[0] = (silu * up).astype(BF16)
        dg = (da * up * (sg * (1.0 + g * (1.0 - sg)))).astype(BF16)
        dup = (da * silu).astype(BF16)
        dz_ref[0, 0] = dg
        dz_ref[1, 0] = dup
        for n in range(2):
            cols = slice(n * (D // 2), (n + 1) * (D // 2))
            du[:, cols] += _dot(dg, wgu_ref[0, 0, :, cols]) + _dot(dup, wgu_ref[1, 0, :, cols])

        @pl.when(k == 3)
        def _():
            dx, dgain = _rms_bwd(h_ref[...], g_ref[...], du[...])
            dhp_ref[...] = dh_ref[...] + dx
            dgain_ref[...] += dgain

    return _call(
        body, name="ffn_bwd_x", grid=(t // TM, 4), ride=ride,
        in_specs=[pl.BlockSpec((TM, D), lambda i, k: (i, 0)),
                  pl.BlockSpec((TM, D), lambda i, k: (i, 0)),
                  pl.BlockSpec((1, D), lambda i, k: (0, 0)),
                  pl.BlockSpec((2, 1, TM, FC), lambda i, k: (0, k, i, 0)),
                  pl.BlockSpec((2, 1, FC, D), lambda i, k: (0, k, 0, 0)),
                  pl.BlockSpec((1, FC, D), lambda i, k: (k, 0, 0))],
        out_specs=[pl.BlockSpec((TM, D), lambda i, k: (i, 0)),
                   pl.BlockSpec((2, 1, TM, FC), lambda i, k: (0, k, i, 0)),
                   pl.BlockSpec((1, TM, FC), lambda i, k: (k, i, 0)),
                   pl.BlockSpec((1, D), lambda i, k: (0, 0)),
                   pl.BlockSpec((TM, D), lambda i, k: (i, 0))],
        out_shape=[jax.ShapeDtypeStruct((t, D), F32),
                   jax.ShapeDtypeStruct((2, 4, t, FC), BF16),
                   jax.ShapeDtypeStruct((4, t, FC), BF16),
                   jax.ShapeDtypeStruct((1, D), F32),
                   jax.ShapeDtypeStruct((t, D), BF16)],
        scratch_shapes=[pltpu.VMEM((TM, D), F32)],
        args=(dh, h, gain, z, wgu, wd))


TF = 1024
TW = 2048


def ffn_bwd_w(u, dz, a, dhb, ride=None):
    t = u.shape[0]
    tw = min(TW, t)
    steps = t // tw

    def body(u_ref, dz_ref, a_ref, dh_ref, dwgu_ref, dwd_ref, acc_gu, acc_d):
        j = pl.program_id(1)

        @pl.when(j == 0)
        def _():
            acc_gu[...] = jnp.zeros_like(acc_gu)
            acc_d[...] = jnp.zeros_like(acc_d)

        ub = u_ref[...]
        acc_gu[0] += _dot_tn(dz_ref[0, 0], ub)
        acc_gu[1] += _dot_tn(dz_ref[1, 0], ub)
        acc_d[...] += _dot_tn(a_ref[0], dh_ref[...])

        @pl.when(j == steps - 1)
        def _():
            dwgu_ref[:, 0] = acc_gu[...].astype(BF16)
            dwd_ref[0] = acc_d[...].astype(BF16)

    return _call(
        body, name="ffn_bwd_w", grid=(4, steps), ride=ride,
        in_specs=[pl.BlockSpec((tw, D), lambda k, j: (j, 0)),
                  pl.BlockSpec((2, 1, tw, FC), lambda k, j: (0, k, j, 0)),
                  pl.BlockSpec((1, tw, FC), lambda k, j: (k, j, 0)),
                  pl.BlockSpec((tw, D), lambda k, j: (j, 0))],
        out_specs=[pl.BlockSpec((2, 1, FC, D), lambda k, j: (0, k, 0, 0)),
                   pl.BlockSpec((1, FC, D), lambda k, j: (k, 0, 0))],
        out_shape=[jax.ShapeDtypeStruct((2, 4, FC, D), BF16),
                   jax.ShapeDtypeStruct((4, FC, D), BF16)],
        scratch_shapes=[pltpu.VMEM((2, FC, D), F32), pltpu.VMEM((FC, D), F32)],
        args=(u, dz, a, dhb))


def _prev_halo(i, tile=TM):
    return jnp.maximum(i * (tile // HALO) - 1, 0)


def _next_halo(i, t, tile=TM):
    return jnp.minimum((i + 1) * (tile // HALO), t // HALO - 1)


def rms_matmul(h, gain, w, bias, ride=None):
    t = h.shape[0]
    n = w.shape[1]

    def body(h_ref, g_ref, w_ref, b_ref, z_ref, u_ref):
        u = _rms_fwd(h_ref[...], g_ref[...]).astype(BF16)
        u_ref[...] = u
        z_ref[...] = (_dot(u, w_ref[...]) + b_ref[...]).astype(BF16)

    return _call(
        body, name=f"rms_matmul_{n}", grid=(t // TM,), ride=ride,
        in_specs=[pl.BlockSpec((TM, D), lambda i: (i, 0)),
                  pl.BlockSpec((1, D), lambda i: (0, 0)),
                  pl.BlockSpec((D, n), lambda i: (0, 0)),
                  pl.BlockSpec((1, n), lambda i: (0, 0))],
        out_specs=[pl.BlockSpec((TM, n), lambda i: (i, 0)),
                   pl.BlockSpec((TM, D), lambda i: (i, 0))],
        out_shape=[jax.ShapeDtypeStruct((t, n), BF16),
                   jax.ShapeDtypeStruct((t, D), BF16)],
        args=(h, gain, w, bias))


def in_proj_bwd_x(dz, w, h, gain, dh, ride=None):
    t = h.shape[0]
    n = w.shape[1]

    def body(dz_ref, w_ref, h_ref, g_ref, dh_ref, dhp_ref, dgain_ref, dbias_ref):
        @pl.when(pl.program_id(0) == 0)
        def _():
            dgain_ref[...] = jnp.zeros_like(dgain_ref)
            dbias_ref[...] = jnp.zeros_like(dbias_ref)

        du = _dot_nt(dz_ref[...], w_ref[...])
        dx, dgain = _rms_bwd(h_ref[...], g_ref[...], du)
        dhp_ref[...] = dh_ref[...] + dx
        dgain_ref[...] += dgain
        dbias_ref[...] += jnp.sum(dz_ref[...].astype(F32), axis=0, keepdims=True)

    return _call(
        body, name=f"in_proj_bwd_x_{n}", grid=(t // TM,), ride=ride,
        in_specs=[pl.BlockSpec((TM, n), lambda i: (i, 0)),
                  pl.BlockSpec((D, n), lambda i: (0, 0)),
                  pl.BlockSpec((TM, D), lambda i: (i, 0)),
                  pl.BlockSpec((1, D), lambda i: (0, 0)),
                  pl.BlockSpec((TM, D), lambda i: (i, 0))],
        out_specs=[pl.BlockSpec((TM, D), lambda i: (i, 0)),
                   pl.BlockSpec((1, D), lambda i: (0, 0)),
                   pl.BlockSpec((1, n), lambda i: (0, 0))],
        out_shape=[jax.ShapeDtypeStruct((t, D), F32),
                   jax.ShapeDtypeStruct((1, D), F32),
                   jax.ShapeDtypeStruct((1, n), F32)],
        args=(dz, w, h, gain, dh))


def in_proj_bwd_w(u, dz, ride=None):
    t = u.shape[0]
    n = dz.shape[1]
    steps = t // TM

    def body(u_ref, dz_ref, dw_ref, acc):
        s = pl.program_id(0)

        @pl.when(s == 0)
        def _():
            acc[...] = jnp.zeros_like(acc)

        acc[...] += _dot_tn(u_ref[...], dz_ref[...])

        @pl.when(s == steps - 1)
        def _():
            dw_ref[...] = acc[...].astype(BF16)

    return _call(
        body, name=f"in_proj_bwd_w_{n}", grid=(steps,), ride=ride,
        in_specs=[pl.BlockSpec((TM, D), lambda s: (s, 0)),
                  pl.BlockSpec((TM, n), lambda s: (s, 0))],
        out_specs=[pl.BlockSpec((D, n), lambda s: (0, 0))],
        out_shape=[jax.ShapeDtypeStruct((D, n), BF16)],
        scratch_shapes=[pltpu.VMEM((D, n), F32)],
        args=(u, dz))


A_TAPS = 3


def a_mix_fwd(z, h, conv, wout, ride=None):
    t = h.shape[0]

    def body(z_ref, zp_ref, h_ref, cw_ref, wo_ref, hn_ref, pad):
        i = pl.program_id(0)
        ph = zp_ref[:, D:2 * D].astype(F32) * zp_ref[:, 2 * D:].astype(F32)
        pad[0:HALO, :] = jnp.where(i == 0, 0.0, ph)
        pad[HALO:, :] = z_ref[:, D:2 * D].astype(F32) * z_ref[:, 2 * D:].astype(F32)
        q = jnp.zeros((TM, D), F32)
        for k in range(A_TAPS):
            off = HALO - (A_TAPS - 1) + k
            q += cw_ref[k:k + 1, :] * pad[off:off + TM, :]
        r = z_ref[:, 0:D].astype(F32) * q
        hn_ref[...] = h_ref[...] + _dot(r.astype(BF16), wo_ref[...])

    return _call(
        body, name="a_mix_fwd", grid=(t // TM,), ride=ride,
        in_specs=[pl.BlockSpec((TM, 3 * D), lambda i: (i, 0)),
                  pl.BlockSpec((HALO, 3 * D), lambda i: (_prev_halo(i), 0)),
                  pl.BlockSpec((TM, D), lambda i: (i, 0)),
                  pl.BlockSpec((A_TAPS, D), lambda i: (0, 0)),
                  pl.BlockSpec((D, D), lambda i: (0, 0))],
        out_specs=[pl.BlockSpec((TM, D), lambda i: (i, 0))],
        out_shape=[jax.ShapeDtypeStruct((t, D), F32)],
        scratch_shapes=[pltpu.VMEM((HALO + TM, D), F32)],
        args=(z, z, h, conv, wout))


def a_mix_bwd(dh, z, conv, wout, ride=None):
    t = dh.shape[0]
    steps = t // TM

    def body(dh_ref, dhn_ref, z_ref, zp_ref, zn_ref, cw_ref, wo_ref, dz_ref, dwo_ref, dcw_ref, pad, dqpad, dwo):
        i = pl.program_id(0)
        last = i == steps - 1

        @pl.when(i == 0)
        def _():
            dwo[...] = jnp.zeros_like(dwo)
            dcw_ref[...] = jnp.zeros_like(dcw_ref)

        ph = zp_ref[:, D:2 * D].astype(F32) * zp_ref[:, 2 * D:].astype(F32)
        pad[0:HALO, :] = jnp.where(i == 0, 0.0, ph)
        c = z_ref[:, D:2 * D].astype(F32)
        v = z_ref[:, 2 * D:].astype(F32)
        pad[HALO:, :] = c * v
        q = jnp.zeros((TM, D), F32)
        for k in range(A_TAPS):
            off = HALO - (A_TAPS - 1) + k
            q += cw_ref[k:k + 1, :] * pad[off:off + TM, :]
        b = z_ref[:, 0:D].astype(F32)
        dhb = dh_ref[...].astype(BF16)
        dwo[...] += _dot_tn((b * q).astype(BF16), dhb)
        dr = _dot_nt(dhb, wo_ref[...])
        dz_ref[:, 0:D] = (dr * q).astype(BF16)
        dq = dr * b
        drn = _dot_nt(dhn_ref[...].astype(BF16), wo_ref[...])
        dqpad[0:TM, :] = dq
        dqpad[TM:, :] = jnp.where(last, 0.0, drn * zn_ref[:, 0:D].astype(F32))
        dp = jnp.zeros((TM, D), F32)
        for k in range(A_TAPS):
            off = A_TAPS - 1 - k
            dp += cw_ref[k:k + 1, :] * dqpad[off:off + TM, :]
            poff = HALO - (A_TAPS - 1) + k
            dcw_ref[k:k + 1, :] += jnp.sum(dq * pad[poff:poff + TM, :], axis=0, keepdims=True)
        dz_ref[:, D:2 * D] = (dp * v).astype(BF16)
        dz_ref[:, 2 * D:] = (dp * c).astype(BF16)

        @pl.when(last)
        def _():
            dwo_ref[...] = dwo[...].astype(BF16)

    return _call(
        body, name="a_mix_bwd", grid=(steps,), ride=ride,
        in_specs=[pl.BlockSpec((TM, D), lambda i: (i, 0)),
                  pl.BlockSpec((HALO, D), lambda i: (_next_halo(i, t), 0)),
                  pl.BlockSpec((TM, 3 * D), lambda i: (i, 0)),
                  pl.BlockSpec((HALO, 3 * D), lambda i: (_prev_halo(i), 0)),
                  pl.BlockSpec((HALO, 3 * D), lambda i: (_next_halo(i, t), 0)),
                  pl.BlockSpec((A_TAPS, D), lambda i: (0, 0)),
                  pl.BlockSpec((D, D), lambda i: (0, 0))],
        out_specs=[pl.BlockSpec((TM, 3 * D), lambda i: (i, 0)),
                   pl.BlockSpec((D, D), lambda i: (0, 0)),
                   pl.BlockSpec((A_TAPS, D), lambda i: (0, 0))],
        out_shape=[jax.ShapeDtypeStruct((t, 3 * D), BF16),
                   jax.ShapeDtypeStruct((D, D), BF16),
                   jax.ShapeDtypeStruct((A_TAPS, D), F32)],
        scratch_shapes=[pltpu.VMEM((HALO + TM, D), F32), pltpu.VMEM((TM + HALO, D), F32), pltpu.VMEM((D, D), F32)],
        args=(dh, dh, z, z, z, conv, wout))


C_TAPS = 31


def _glu(zr):
    return zr[:, 0:D].astype(F32) * jax.nn.sigmoid(zr[:, D:].astype(F32))


def _ln_silu(h2, lg, lb):
    mu = jnp.mean(h2, axis=-1, keepdims=True)
    xc = h2 - mu
    rstd = lax.rsqrt(jnp.mean(xc * xc, axis=-1, keepdims=True) + LN_EPS)
    xn = xc * rstd
    h3 = xn * lg + lb
    s3 = jax.nn.sigmoid(h3)
    return xn, rstd, h3, s3


def _ln_silu_bwd(h2, lg, lb, dh4):
    xn, rstd, h3, s3 = _ln_silu(h2, lg, lb)
    dh3 = dh4 * (s3 * (1.0 + h3 * (1.0 - s3)))
    dxn = dh3 * lg
    dh2 = rstd * (dxn - jnp.mean(dxn, axis=-1, keepdims=True) - xn * jnp.mean(dxn * xn, axis=-1, keepdims=True))
    return dh2, dh3, xn, h3 * s3


TC = 256
RB = 64
LANES = 128
SHIFTS = 7


def _shifted_copies(src, sh, rows):
    for b in range(1, SHIFTS + 1):
        sh[b - 1, 0:rows, :] = src[b:b + rows, :]


def _window(src, sh, o, r0, lanes):
    a, b = divmod(o, 8)
    ref = src if b == 0 else sh.at[b - 1]
    return ref[8 * a + r0:8 * a + r0 + RB, lanes]


def c_mix_fwd(z, h, dw, bdw, lg, lb, w2, b2, ride=None):
    t = h.shape[0]

    def body(z_ref, zp_ref, h_ref, dw_ref, bdw_ref, lg_ref, lb_ref, w2_ref, b2_ref, hn_ref, h2_ref, pad, sh):
        i = pl.program_id(0)
        pad[0:HALO, :] = jnp.where(i == 0, 0.0, _glu(zp_ref))
        pad[HALO:, :] = _glu(z_ref)
        _shifted_copies(pad, sh, TC + 24)
        for l in range(D // LANES):
            lanes = slice(l * LANES, (l + 1) * LANES)
            for r0 in range(0, TC, RB):
                acc = jnp.zeros((RB, LANES), F32) + bdw_ref[:, lanes]
                for k in range(C_TAPS):
                    acc += dw_ref[k:k + 1, lanes] * _window(pad, sh, HALO - (C_TAPS - 1) + k, r0, lanes)
                h2_ref[r0:r0 + RB, lanes] = acc
        _, _, h3, s3 = _ln_silu(h2_ref[...], lg_ref[...], lb_ref[...])
        hn_ref[...] = h_ref[...] + _dot((h3 * s3).astype(BF16), w2_ref[...]) + b2_ref[...]

    vec = pl.BlockSpec((1, D), lambda i: (0, 0))
    return _call(
        body, name="c_mix_fwd", grid=(t // TC,), ride=ride,
        in_specs=[pl.BlockSpec((TC, 2 * D), lambda i: (i, 0)),
                  pl.BlockSpec((HALO, 2 * D), lambda i: (_prev_halo(i, TC), 0)),
                  pl.BlockSpec((TC, D), lambda i: (i, 0)),
                  pl.BlockSpec((C_TAPS, D), lambda i: (0, 0)),
                  vec, vec, vec,
                  pl.BlockSpec((D, D), lambda i: (0, 0)),
                  vec],
        out_specs=[pl.BlockSpec((TC, D), lambda i: (i, 0)),
                   pl.BlockSpec((TC, D), lambda i: (i, 0))],
        out_shape=[jax.ShapeDtypeStruct((t, D), F32),
                   jax.ShapeDtypeStruct((t, D), F32)],
        scratch_shapes=[pltpu.VMEM((HALO + TC, D), F32), pltpu.VMEM((SHIFTS, TC + 24, D), F32)],
        args=(z, z, h, dw, bdw, lg, lb, w2, b2))


def c_mix_bwd(dh, z, h2, dw, lg, lb, w2, ride=None):
    t = dh.shape[0]
    steps = t // TC

    def body(dh_ref, dhn_ref, z_ref, zp_ref, h2_ref, h2n_ref, dw_ref, lg_ref, lb_ref, w2_ref,
             dz_ref, dw2_ref, db2_ref, dlg_ref, dlb_ref, dbdw_ref, ddw_ref, pad, dpad, dw2, sh, dh1):
        i = pl.program_id(0)
        last = i == steps - 1

        @pl.when(i == 0)
        def _():
            for r in (dw2, db2_ref, dlg_ref, dlb_ref, dbdw_ref, ddw_ref):
                r[...] = jnp.zeros_like(r)

        lg, lb = lg_ref[...], lb_ref[...]
        dh = dh_ref[...]
        dhb = dh.astype(BF16)
        dh2, dh3, xn, h4 = _ln_silu_bwd(h2_ref[...], lg, lb, _dot_nt(dhb, w2_ref[...]))
        dw2[...] += _dot_tn(h4.astype(BF16), dhb)
        db2_ref[...] += jnp.sum(dh, axis=0, keepdims=True)
        dlg_ref[...] += jnp.sum(dh3 * xn, axis=0, keepdims=True)
        dlb_ref[...] += jnp.sum(dh3, axis=0, keepdims=True)
        dbdw_ref[...] += jnp.sum(dh2, axis=0, keepdims=True)
        dh2n, _, _, _ = _ln_silu_bwd(h2n_ref[...], lg, lb, _dot_nt(dhn_ref[...].astype(BF16), w2_ref[...]))
        dpad[0:TC, :] = dh2
        dpad[TC:, :] = jnp.where(last, 0.0, dh2n)
        _shifted_copies(dpad, sh, TC + 24)
        for l in range(D // LANES):
            lanes = slice(l * LANES, (l + 1) * LANES)
            for r0 in range(0, TC, RB):
                acc = jnp.zeros((RB, LANES), F32)
                for k in range(C_TAPS):
                    acc += dw_ref[k:k + 1, lanes] * _window(dpad, sh, C_TAPS - 1 - k, r0, lanes)
                dh1[r0:r0 + RB, lanes] = acc
        pad[0:HALO, :] = jnp.where(i == 0, 0.0, _glu(zp_ref))
        pad[HALO:, :] = _glu(z_ref)
        _shifted_copies(pad, sh, TC + 24)
        for l in range(D // LANES):
            lanes = slice(l * LANES, (l + 1) * LANES)
            accs = [jnp.zeros((8, LANES), F32) for _ in range(C_TAPS)]
            for r0 in range(0, TC, RB):
                d = dpad[r0:r0 + RB, lanes]
                for k in range(C_TAPS):
                    prod = d * _window(pad, sh, HALO - (C_TAPS - 1) + k, r0, lanes)
                    accs[k] += jnp.sum(prod.reshape(RB // 8, 8, LANES), axis=0)
            for k in range(C_TAPS):
                ddw_ref[k:k + 1, lanes] += jnp.sum(accs[k], axis=0, keepdims=True)
        a = z_ref[:, 0:D].astype(F32)
        sg = jax.nn.sigmoid(z_ref[:, D:].astype(F32))
        d1 = dh1[...]
        dz_ref[:, 0:D] = (d1 * sg).astype(BF16)
        dz_ref[:, D:] = (d1 * a * sg * (1.0 - sg)).astype(BF16)

        @pl.when(last)
        def _():
            dw2_ref[...] = dw2[...].astype(BF16)

    vec = pl.BlockSpec((1, D), lambda i: (0, 0))
    return _call(
        body, name="c_mix_bwd", grid=(steps,), ride=ride,
        in_specs=[pl.BlockSpec((TC, D), lambda i: (i, 0)),
                  pl.BlockSpec((HALO, D), lambda i: (_next_halo(i, t, TC), 0)),
                  pl.BlockSpec((TC, 2 * D), lambda i: (i, 0)),
                  pl.BlockSpec((HALO, 2 * D), lambda i: (_prev_halo(i, TC), 0)),
                  pl.BlockSpec((TC, D), lambda i: (i, 0)),
                  pl.BlockSpec((HALO, D), lambda i: (_next_halo(i, t, TC), 0)),
                  pl.BlockSpec((C_TAPS, D), lambda i: (0, 0)),
                  vec, vec,
                  pl.BlockSpec((D, D), lambda i: (0, 0))],
        out_specs=[pl.BlockSpec((TC, 2 * D), lambda i: (i, 0)),
                   pl.BlockSpec((D, D), lambda i: (0, 0)),
                   vec, vec, vec, vec,
                   pl.BlockSpec((C_TAPS, D), lambda i: (0, 0))],
        out_shape=[jax.ShapeDtypeStruct((t, 2 * D), BF16),
                   jax.ShapeDtypeStruct((D, D), BF16)]
                  + [jax.ShapeDtypeStruct((1, D), F32)] * 4
                  + [jax.ShapeDtypeStruct((C_TAPS, D), F32)],
        scratch_shapes=[pltpu.VMEM((HALO + TC, D), F32), pltpu.VMEM((TC + HALO, D), F32), pltpu.VMEM((D, D), F32),
                        pltpu.VMEM((SHIFTS, TC + 24, D), F32), pltpu.VMEM((TC, D), F32)],
        args=(dh, dh, z, z, h2, h2, dw, lg, lb, w2))


POOL_WINDOWS = (2, 4, 8, 16)
GW = D // len(POOL_WINDOWS)


def _pool_mixed(pad, g, w, inv_cnt):
    cols = slice(g * GW, (g + 1) * GW)
    s = pad[HALO:HALO + TM, cols]
    u = s
    for j in range(1, w):
        s = s + pad[HALO - j:HALO - j + TM, cols]
    return s * inv_cnt - u


def _inv_cnt(i, w):
    row = i * TM + lax.broadcasted_iota(jnp.int32, (TM, 1), 0)
    return 1.0 / jnp.minimum(row + 1, w).astype(F32)


def b_mix_fwd(h, gain, wg, scale, ride=None):
    t = h.shape[0]

    def body(h_ref, hp_ref, g_ref, wg_ref, sc_ref, hn_ref, pad):
        i = pl.program_id(0)
        gain = g_ref[...]
        pad[0:HALO, :] = jnp.where(i == 0, 0.0, _rms_fwd(hp_ref[...], gain))
        pad[HALO:, :] = _rms_fwd(h_ref[...], gain)
        for g, w in enumerate(POOL_WINDOWS):
            cols = slice(g * GW, (g + 1) * GW)
            mixed = _pool_mixed(pad, g, w, _inv_cnt(i, w))
            y = _dot(mixed.astype(BF16), wg_ref[g])
            hn_ref[:, cols] = h_ref[:, cols] + y * sc_ref[:, cols]

    return _call(
        body, name="b_mix_fwd", grid=(t // TM,), ride=ride,
        in_specs=[pl.BlockSpec((TM, D), lambda i: (i, 0)),
                  pl.BlockSpec((HALO, D), lambda i: (_prev_halo(i), 0)),
                  pl.BlockSpec((1, D), lambda i: (0, 0)),
                  pl.BlockSpec((4, GW, GW), lambda i: (0, 0, 0)),
                  pl.BlockSpec((1, D), lambda i: (0, 0))],
        out_specs=[pl.BlockSpec((TM, D), lambda i: (i, 0))],
        out_shape=[jax.ShapeDtypeStruct((t, D), F32)],
        scratch_shapes=[pltpu.VMEM((HALO + TM, D), F32)],
        args=(h, h, gain, wg, scale))


def b_mix_bwd(dh, h, gain, wg, scale, ride=None):
    t = h.shape[0]
    steps = t // TM

    def body(dh_ref, dhn_ref, h_ref, hp_ref, g_ref, wg_ref, sc_ref, dhp_ref, dgain_ref, dwg_ref, dsc_ref, pad, dpad, du):
        i = pl.program_id(0)
        last = i == steps - 1

        @pl.when(i == 0)
        def _():
            for r in (dgain_ref, dwg_ref, dsc_ref):
                r[...] = jnp.zeros_like(r)

        gain = g_ref[...]
        pad[0:HALO, :] = jnp.where(i == 0, 0.0, _rms_fwd(hp_ref[...], gain))
        pad[HALO:, :] = _rms_fwd(h_ref[...], gain)
        for g, w in enumerate(POOL_WINDOWS):
            cols = slice(g * GW, (g + 1) * GW)
            inv_cnt = _inv_cnt(i, w)
            mixed = _pool_mixed(pad, g, w, inv_cnt).astype(BF16)
            dh = dh_ref[:, cols]
            dsc_ref[:, cols] += jnp.sum(dh * _dot(mixed, wg_ref[g]), axis=0, keepdims=True)
            dy = (dh * sc_ref[:, cols]).astype(BF16)
            dwg_ref[g] += _dot_tn(mixed, dy)
            dm = _dot_nt(dy, wg_ref[g])
            dmn = _dot_nt((dhn_ref[:, cols] * sc_ref[:, cols]).astype(BF16), wg_ref[g])
            dpad[0:TM, cols] = dm * inv_cnt
            dpad[TM:, cols] = jnp.where(last, 0.0, dmn * (1.0 / w))
            s = dpad[0:TM, cols]
            for j in range(1, w):
                s = s + dpad[j:j + TM, cols]
            du[:, cols] = s - dm
        dx, dgain = _rms_bwd(h_ref[...], gain, du[...])
        dhp_ref[...] = dh_ref[...] + dx
        dgain_ref[...] += dgain

    return _call(
        body, name="b_mix_bwd", grid=(steps,), ride=ride,
        in_specs=[pl.BlockSpec((TM, D), lambda i: (i, 0)),
                  pl.BlockSpec((HALO, D), lambda i: (_next_halo(i, t), 0)),
                  pl.BlockSpec((TM, D), lambda i: (i, 0)),
                  pl.BlockSpec((HALO, D), lambda i: (_prev_halo(i), 0)),
                  pl.BlockSpec((1, D), lambda i: (0, 0)),
                  pl.BlockSpec((4, GW, GW), lambda i: (0, 0, 0)),
                  pl.BlockSpec((1, D), lambda i: (0, 0))],
        out_specs=[pl.BlockSpec((TM, D), lambda i: (i, 0)),
                   pl.BlockSpec((1, D), lambda i: (0, 0)),
                   pl.BlockSpec((4, GW, GW), lambda i: (0, 0, 0)),
                   pl.BlockSpec((1, D), lambda i: (0, 0))],
        out_shape=[jax.ShapeDtypeStruct((t, D), F32),
                   jax.ShapeDtypeStruct((1, D), F32),
                   jax.ShapeDtypeStruct((4, GW, GW), F32),
                   jax.ShapeDtypeStruct((1, D), F32)],
        scratch_shapes=[pltpu.VMEM((HALO + TM, D), F32), pltpu.VMEM((TM + HALO, D), F32), pltpu.VMEM((TM, D), F32)],
        args=(dh, dh, h, h, gain, wg, scale))


LOSS_LANES = 128


def loss_head(h, gain, target):
    t = h.shape[0]

    def body(h_ref, g_ref, tg_ref, loss_ref, dh_ref, dgain_ref):
        @pl.when(pl.program_id(0) == 0)
        def _():
            loss_ref[...] = jnp.zeros_like(loss_ref)
            dgain_ref[...] = jnp.zeros_like(dgain_ref)

        x, gain = h_ref[...], g_ref[...]
        err = _rms_fwd(x, gain) - tg_ref[...]
        per_row = jnp.mean(err * err, axis=-1, keepdims=True)
        loss_ref[...] += jnp.broadcast_to(0.5 * jnp.sum(per_row, axis=0, keepdims=True), (1, LOSS_LANES))
        dx, dgain = _rms_bwd(x, gain, err * (1.0 / D))
        dh_ref[...] = dx
        dgain_ref[...] += dgain

    outs, _ = _call(
        body, name="loss_head", grid=(t // TM,),
        in_specs=[pl.BlockSpec((TM, D), lambda i: (i, 0)),
                  pl.BlockSpec((1, D), lambda i: (0, 0)),
                  pl.BlockSpec((TM, D), lambda i: (i, 0))],
        out_specs=[pl.BlockSpec((1, LOSS_LANES), lambda i: (0, 0)),
                   pl.BlockSpec((TM, D), lambda i: (i, 0)),
                   pl.BlockSpec((1, D), lambda i: (0, 0))],
        out_shape=[jax.ShapeDtypeStruct((1, LOSS_LANES), F32),
                   jax.ShapeDtypeStruct((t, D), F32),
                   jax.ShapeDtypeStruct((1, D), F32)],
        args=(h, gain, target))
    return outs


ADAM_LR = 0.001
ADAM_B1 = 0.9
ADAM_B2 = 0.999
ADAM_EPS = 1e-08
ADAM_WD = 0.01
ADAM_STEP = 10
ADAM_VMEM = 40 * 1024 * 1024


def cast_all(arrays):
    def body(*refs):
        for src, dst in zip(refs[:len(arrays)], refs[len(arrays):]):
            dst[...] = src[...].astype(BF16)

    return pl.pallas_call(
        body, name="cast_all", out_shape=[jax.ShapeDtypeStruct(a.shape, BF16) for a in arrays],
        compiler_params=pltpu.CompilerParams(vmem_limit_bytes=VMEM_LIMIT),
    )(*arrays)


def _adam_math(w, m, v, g):
    m = ADAM_B1 * m + (1.0 - ADAM_B1) * g
    v = ADAM_B2 * v + (1.0 - ADAM_B2) * (g * g)
    m_hat = m / (1.0 - ADAM_B1 ** ADAM_STEP)
    v_hat = v / (1.0 - ADAM_B2 ** ADAM_STEP)
    return -ADAM_LR * (m_hat / (jnp.sqrt(v_hat) + ADAM_EPS) + ADAM_WD * w), m, v


def adamw(ws, ms, vs, gps, rb, tokens=()):
    n = len(ws)
    r, c = ws[0].shape
    nb = r // rb

    def body(*refs):
        i = pl.program_id(0)
        outs = refs[4 * n + len(tokens):]
        for j in range(n):
            w_ref, m_ref, v_ref, gp_ref = (refs[q * n + j] for q in range(4))
            g_ref, d_ref, nm_ref, nv_ref = (outs[q * n + j] for q in range(4))

            @pl.when(i // nb == j)
            def _():
                g = gp_ref[0].astype(F32)
                for s in range(1, N_DEV):
                    g = g + gp_ref[s].astype(F32)
                g_ref[...] = g
                d_ref[...], nm_ref[...], nv_ref[...] = _adam_math(w_ref[...], m_ref[...], v_ref[...], g)

    def blk(j):
        return pl.BlockSpec((rb, c), lambda i: (jnp.clip(i - j * nb, 0, nb - 1), 0))

    def gblk(j):
        return pl.BlockSpec((N_DEV, rb, c), lambda i: (0, jnp.clip(i - j * nb, 0, nb - 1), 0))

    outs, _ = _call(
        body, name=f"adamw_{n}x{r}x{c}", grid=(n * nb,),
        in_specs=[blk(j) for _ in range(3) for j in range(n)] + [gblk(j) for j in range(n)] + [ANY] * len(tokens),
        out_specs=[blk(j) for _ in range(4) for j in range(n)],
        out_shape=[jax.ShapeDtypeStruct((r, c), F32)] * (4 * n),
        args=(*ws, *ms, *vs, *gps, *tokens))
    return outs[:n], outs[n:2 * n], outs[2 * n:3 * n], outs[3 * n:]


def adamw_vectors(ws, ms, vs, gparts):
    nv = len(ws)

    def body(*refs):
        w_refs, m_refs, v_refs = refs[:nv], refs[nv:2 * nv], refs[2 * nv:3 * nv]
        gp_ref = refs[3 * nv]
        outs = refs[3 * nv + 1:]
        g_refs, d_refs, nm_refs, nv_refs = outs[:nv], outs[nv:2 * nv], outs[2 * nv:3 * nv], outs[3 * nv:]
        row = 0
        for i in range(nv):
            for part in range(w_refs[i].shape[1] // D):
                cols = slice(part * D, (part + 1) * D)
                g = gp_ref[0, row:row + 1, :]
                for s in range(1, N_DEV):
                    g = g + gp_ref[s, row:row + 1, :]
                g_refs[i][:, cols] = g
                d_refs[i][:, cols], nm_refs[i][:, cols], nv_refs[i][:, cols] = _adam_math(
                    w_refs[i][:, cols], m_refs[i][:, cols], v_refs[i][:, cols], g)
                row += 1

    shapes = [jax.ShapeDtypeStruct(w.shape, F32) for w in ws]
    outs = pl.pallas_call(body, name="adamw_vectors", out_shape=shapes * 4)(*ws, *ms, *vs, gparts)
    return outs[:nv], outs[nv:2 * nv], outs[2 * nv:3 * nv], outs[3 * nv:]


WEIGHTS = ["ln1_0", "a0_w_in", "a0_conv", "a0_w_out", "ln2_0", "ffn0_w_gu", "ffn0_w_down",
           "ln1_1", "b1_w_grp", "b1_scale", "ln2_1", "ffn1_w_gu", "ffn1_w_down",
           "ln1_2", "c2_w_pw1", "c2_b_pw1", "c2_dw", "c2_b_dw", "c2_ln_g", "c2_ln_b", "c2_w_pw2", "c2_b_pw2",
           "ln2_2", "ffn2_w_gu", "ffn2_w_down",
           "ln1_3", "a3_w_in", "a3_conv", "a3_w_out", "ln2_3", "ffn3_w_gu", "ffn3_w_down", "ln_f"]
SHARDED = {"a0_w_in": ("cols", 256), "a0_conv": ("cols", A_TAPS), "a0_w_out": ("rows", 128),
           "ffn0_w_gu": ("lead", 176), "ffn0_w_down": ("rows", 176),
           "b1_w_grp": ("mid", 128),
           "ffn1_w_gu": ("lead", 176), "ffn1_w_down": ("rows", 176),
           "c2_w_pw1": ("cols", 256), "c2_dw": ("cols", C_TAPS), "c2_w_pw2": ("rows", 128),
           "ffn2_w_gu": ("lead", 176), "ffn2_w_down": ("rows", 176),
           "a3_w_in": ("cols", 256), "a3_conv": ("cols", A_TAPS), "a3_w_out": ("rows", 128),
           "ffn3_w_gu": ("lead", 176), "ffn3_w_down": ("rows", 176)}
IN_PROJ = ("a0_w_in", "c2_w_pw1", "a3_w_in")
REPL = [n for n in WEIGHTS if n not in SHARDED]
REPL_ROWS = 16
GATHER_PLAN = {"first": ["a0_w_in", "a0_w_out", "a0_conv"],
               "in0": ["ffn0_w_gu"], "mix0": ["ffn0_w_down"],
               "ffn0": ["b1_w_grp", "ffn1_w_gu", "ffn1_w_down"],
               "ffn1": ["c2_w_pw1", "c2_w_pw2", "c2_dw", "ffn2_w_gu"],
               "in2": ["ffn2_w_down"],
               "mix2": ["a3_w_in", "a3_w_out", "a3_conv"],
               "ffn2": ["ffn3_w_gu", "ffn3_w_down"]}
SCATTER_PLAN = {"mixb3": ["ffn3_w_down"], "inw3": ["a3_w_out", "a3_conv"],
                "ffnx2": ["ffn3_w_gu"], "ffnw2": ["a3_w_in"],
                "mixb2": ["ffn2_w_gu", "ffn2_w_down"], "inw2": ["c2_w_pw2", "c2_dw"],
                "ffnx1": ["c2_w_pw1"],
                "ffnx0": ["ffn1_w_gu"], "ffnw0": ["ffn1_w_down", "b1_w_grp"],
                "last": ["repl"]}
LATE_FFN = ["ffn0_w_gu", "ffn0_w_down"]
LATE_MIX = ["a0_w_in", "a0_w_out", "a0_conv"]


def _step(p):
    vec = lambda n: p[n].reshape(1, -1)
    x, target = p["x"][0], p["loss_target"][0]

    names = list(SHARDED)
    stored = lambda n, a: a.T if n.endswith("w_gu") else a
    shard = dict(zip(names, cast_all([stored(n, p[n]) for n in names])))
    full = {}

    def gather(slot):
        names = GATHER_PLAN[slot]
        return gather_ride([shard[n] for n in names], ["cols" if n in IN_PROJ else "lead" for n in names])

    def landed(slot, outs):
        full.update(zip(GATHER_PLAN[slot], outs))

    def conv_full(n):
        k = full[n].shape[1]
        return full[n].transpose(1, 0, 2).reshape(k, D).astype(F32)

    def wgu(i):
        return full[f"ffn{i}_w_gu"].reshape(2, 4, FC, D)

    def wd(i):
        return full[f"ffn{i}_w_down"].reshape(4, FC, D)

    landed("first", run_ride(gather("first"), "gather_first"))
    no_bias = jnp.zeros((1, 3 * D), F32)
    h = [x]
    saved = {}
    conv, wout = {}, {}

    (z, u), got = rms_matmul(h[-1], vec("ln1_0"), full["a0_w_in"], no_bias, ride=gather("in0"))
    landed("in0", got)
    conv[0], wout[0] = conv_full("a0_conv"), full["a0_w_out"].reshape(D, D)
    (hm,), got = a_mix_fwd(z, h[-1], conv[0], wout[0], ride=gather("mix0"))
    landed("mix0", got)
    saved["mix0"] = (z, u)
    h.append(hm)
    (hn, zf, uf), got = ffn_fwd(hm, vec("ln2_0"), wgu(0), wd(0), ride=gather("ffn0"))
    landed("ffn0", got)
    saved["ffn0"] = (zf, uf)
    h.append(hn)

    wgrp = full["b1_w_grp"].transpose(1, 0, 2, 3).reshape(4, GW, GW)
    (hm,), _ = b_mix_fwd(h[-1], vec("ln1_1"), wgrp, vec("b1_scale"))
    h.append(hm)
    (hn, zf, uf), got = ffn_fwd(hm, vec("ln2_1"), wgu(1), wd(1), ride=gather("ffn1"))
    landed("ffn1", got)
    saved["ffn1"] = (zf, uf)
    h.append(hn)

    (z, u), got = rms_matmul(h[-1], vec("ln1_2"), full["c2_w_pw1"], vec("c2_b_pw1"), ride=gather("in2"))
    landed("in2", got)
    cdw, wpw2 = conv_full("c2_dw"), full["c2_w_pw2"].reshape(D, D)
    (hm, h2), got = c_mix_fwd(z, h[-1], cdw, vec("c2_b_dw"), vec("c2_ln_g"), vec("c2_ln_b"), wpw2, vec("c2_b_pw2"),
                              ride=gather("mix2"))
    landed("mix2", got)
    saved["mix2"] = (z, u, h2)
    h.append(hm)
    (hn, zf, uf), got = ffn_fwd(hm, vec("ln2_2"), wgu(2), wd(2), ride=gather("ffn2"))
    landed("ffn2", got)
    saved["ffn2"] = (zf, uf)
    h.append(hn)

    (z, u), _ = rms_matmul(h[-1], vec("ln1_3"), full["a3_w_in"], no_bias)
    conv[3], wout[3] = conv_full("a3_conv"), full["a3_w_out"].reshape(D, D)
    (hm,), _ = a_mix_fwd(z, h[-1], conv[3], wout[3])
    saved["mix3"] = (z, u)
    h.append(hm)
    (hn, zf, uf), _ = ffn_fwd(hm, vec("ln2_3"), wgu(3), wd(3))
    saved["ffn3"] = (zf, uf)
    h.append(hn)

    loss_lanes, dh, g_lnf = loss_head(h[-1], vec("ln_f"), target)

    g = {"ln_f": g_lnf}
    recv = {}

    def repl_rows():
        loss_row = jnp.pad(loss_lanes, ((0, 0), (0, D - LOSS_LANES)))
        return jnp.concatenate([g[n].reshape(-1, D) for n in REPL] + [loss_row], axis=0)

    def scatter(slot):
        parts = []
        for n in SCATTER_PLAN.get(slot, []):
            parts.append((repl_rows(), "all") if n == "repl" else (g[n], SHARDED[n][0]))
        return scatter_ride(parts) if parts else None

    def arrived(slot, outs):
        recv.update(zip(SCATTER_PLAN.get(slot, []), outs))

    for i in (3, 2, 1, 0):
        zf, uf = saved[f"ffn{i}"]
        (dh_prev, dzf, a, g[f"ln2_{i}"], dhb), got = ffn_bwd_x(dh, h[2 * i + 1], vec(f"ln2_{i}"), zf, wgu(i), wd(i),
                                                             ride=scatter(f"ffnx{i}"))
        arrived(f"ffnx{i}", got)
        dh = dh_prev
        (dwgu, dwd), got = ffn_bwd_w(uf, dzf, a, dhb, ride=scatter(f"ffnw{i}"))
        arrived(f"ffnw{i}", got)
        g[f"ffn{i}_w_gu"], g[f"ffn{i}_w_down"] = dwgu.reshape(N_DEV, FC, D), dwd.reshape(FF, D)
        if i == 0:
            late_ffn = scatter_start([(g[n], SHARDED[n][0]) for n in LATE_FFN], "late_ffn_start")
        hin = h[2 * i]
        if i in (0, 3):
            z, u = saved[f"mix{i}"]
            (dz, g[f"a{i}_w_out"], g[f"a{i}_conv"]), got = a_mix_bwd(dh, z, conv[i], wout[i], ride=scatter(f"mixb{i}"))
            arrived(f"mixb{i}", got)
            (g[f"a{i}_w_in"],), got = in_proj_bwd_w(u, dz, ride=scatter(f"inw{i}"))
            arrived(f"inw{i}", got)
            if i == 0:
                late_mix = scatter_start([(g[n], SHARDED[n][0]) for n in LATE_MIX], "late_mix_start")
            (dh, g[f"ln1_{i}"], _), got = in_proj_bwd_x(dz, full[f"a{i}_w_in"], hin, vec(f"ln1_{i}"), dh,
                                                       ride=scatter(f"inx{i}"))
            arrived(f"inx{i}", got)
        elif i == 1:
            (dh, g["ln1_1"], g["b1_w_grp"], g["b1_scale"]), got = b_mix_bwd(dh, hin, vec("ln1_1"), wgrp, vec("b1_scale"),
                                                                             ride=scatter("mixb1"))
            arrived("mixb1", got)
        else:
            z, u, h2 = saved["mix2"]
            (dz, g["c2_w_pw2"], g["c2_b_pw2"], g["c2_ln_g"], g["c2_ln_b"], g["c2_b_dw"], g["c2_dw"]), got = c_mix_bwd(
                dh, z, h2, cdw, vec("c2_ln_g"), vec("c2_ln_b"), wpw2, ride=scatter("mixb2"))
            arrived("mixb2", got)
            (g["c2_w_pw1"],), got = in_proj_bwd_w(u, dz, ride=scatter("inw2"))
            arrived("inw2", got)
            (dh, g["ln1_2"], g["c2_b_pw1"]), got = in_proj_bwd_x(dz, full["c2_w_pw1"], hin, vec("ln1_2"), dh,
                                                                ride=scatter("inx2"))
            arrived("inx2", got)
    grad_x = dh[None]
    arrived("last", run_ride(scatter("last"), "scatter_last"))

    grad, delta, new_m, new_v = {}, {}, {}, {}
    two_d = lambda n, a: stored(n, a.reshape(-1, p[n].shape[-1]))

    def adam_calls(names_, tokens):
        groups, last = {}, None
        for n in names_:
            groups.setdefault((two_d(n, p[n]).shape, SHARDED[n][1]), []).append(n)
        for (shape, rb), members in groups.items():
            per_weight = 2 * rb * shape[1] * (7 * 4 + N_DEV * recv[members[0]].dtype.itemsize)
            at_once = max(1, (ADAM_VMEM // per_weight))
            for lo in range(0, len(members), at_once):
                ns = members[lo:lo + at_once]
                outs = adamw([two_d(n, p[n]) for n in ns], [two_d(n, p["m_" + n]) for n in ns],
                             [two_d(n, p["v_" + n]) for n in ns], [recv[n].reshape(N_DEV, *shape) for n in ns], rb, tokens)
                for res, o in zip((grad, delta, new_m, new_v), outs):
                    res.update({n: stored(n, a).reshape(p[n].shape) for n, a in zip(ns, o)})
                last = outs[0][0]
        return last

    early_done = adam_calls([n for n in SHARDED if n not in LATE_FFN + LATE_MIX], (late_ffn[-1], late_mix[-1]))
    outs = adamw_vectors([vec(n) for n in REPL], [vec("m_" + n) for n in REPL], [vec("v_" + n) for n in REPL], recv["repl"])
    for res, o in zip((grad, delta, new_m, new_v), outs):
        res.update({n: a.reshape(p[n].shape) for n, a in zip(REPL, o)})
    recv.update(zip(LATE_FFN, scatter_wait(late_ffn, [early_done], "late_ffn_wait")))
    recv.update(zip(LATE_MIX, scatter_wait(late_mix, [early_done], "late_mix_wait")))
    adam_calls(LATE_FFN + LATE_MIX, ())

    loss = jnp.sum(recv["repl"][:, REPL_ROWS, 0])
    return (loss, grad_x, *[grad[n] for n in WEIGHTS], *[delta[n] for n in WEIGHTS],
            *[new_m[n] for n in WEIGHTS], *[new_v[n] for n in WEIGHTS])


def kernel(x, ln1_0, a0_w_in, a0_conv, a0_w_out, ln2_0, ffn0_w_gu, ffn0_w_down, ln1_1, b1_w_grp, b1_scale, ln2_1, ffn1_w_gu, ffn1_w_down, ln1_2, c2_w_pw1, c2_b_pw1, c2_dw, c2_b_dw, c2_ln_g, c2_ln_b, c2_w_pw2, c2_b_pw2, ln2_2, ffn2_w_gu, ffn2_w_down, ln1_3, a3_w_in, a3_conv, a3_w_out, ln2_3, ffn3_w_gu, ffn3_w_down, ln_f, loss_target, m_ln1_0, m_a0_w_in, m_a0_conv, m_a0_w_out, m_ln2_0, m_ffn0_w_gu, m_ffn0_w_down, m_ln1_1, m_b1_w_grp, m_b1_scale, m_ln2_1, m_ffn1_w_gu, m_ffn1_w_down, m_ln1_2, m_c2_w_pw1, m_c2_b_pw1, m_c2_dw, m_c2_b_dw, m_c2_ln_g, m_c2_ln_b, m_c2_w_pw2, m_c2_b_pw2, m_ln2_2, m_ffn2_w_gu, m_ffn2_w_down, m_ln1_3, m_a3_w_in, m_a3_conv, m_a3_w_out, m_ln2_3, m_ffn3_w_gu, m_ffn3_w_down, m_ln_f, v_ln1_0, v_a0_w_in, v_a0_conv, v_a0_w_out, v_ln2_0, v_ffn0_w_gu, v_ffn0_w_down, v_ln1_1, v_b1_w_grp, v_b1_scale, v_ln2_1, v_ffn1_w_gu, v_ffn1_w_down, v_ln1_2, v_c2_w_pw1, v_c2_b_pw1, v_c2_dw, v_c2_b_dw, v_c2_ln_g, v_c2_ln_b, v_c2_w_pw2, v_c2_b_pw2, v_ln2_2, v_ffn2_w_gu, v_ffn2_w_down, v_ln1_3, v_a3_w_in, v_a3_conv, v_a3_w_out, v_ln2_3, v_ffn3_w_gu, v_ffn3_w_down, v_ln_f):
    return _step(dict(locals()))
```

```python
import jax
import jax.numpy as jnp
from jax import lax
from jax.experimental import pallas as pl
from jax.experimental.pallas import tpu as pltpu

F32 = jnp.float32
BF16 = jnp.bfloat16

N_DEV = 8
D = 1024
FF = 2816
FC = FF // 4
RMS_EPS = 1e-6
LN_EPS = 1e-5
TM = 512
HALO = 32
VMEM_LIMIT = 60 * 1024 * 1024

NT = (((1,), (1,)), ((), ()))
TN = (((0,), (0,)), ((), ()))
MESH = pl.DeviceIdType.MESH
ANY = pl.BlockSpec(memory_space=pl.ANY)
N_PEERS = N_DEV - 1


def _dot(a, b):
    return jnp.dot(a, b, preferred_element_type=F32)


def _dot_nt(a, b):
    return lax.dot_general(a, b, NT, preferred_element_type=F32)


def _dot_tn(a, b):
    return lax.dot_general(a, b, TN, preferred_element_type=F32)


def _rms_fwd(x, gain):
    r = lax.rsqrt(jnp.mean(x * x, axis=-1, keepdims=True) + RMS_EPS)
    return x * r * gain


def _rms_bwd(x, gain, du):
    r = lax.rsqrt(jnp.mean(x * x, axis=-1, keepdims=True) + RMS_EPS)
    xhat = x * r
    dgain = jnp.sum(du * xhat, axis=0, keepdims=True)
    dxhat = du * gain
    dx = r * (dxhat - xhat * jnp.mean(dxhat * xhat, axis=-1, keepdims=True))
    return dx, dgain


def _dev_index(p):
    return 4 * p[0] + 2 * p[1] + p[2]


def _place():
    return lax.axis_index("x"), lax.axis_index("y"), lax.axis_index("c")


class Ride:
    def __init__(self, ins, out_shapes, start, finish):
        self.ins, self.out_shapes, self.start, self.finish = list(ins), list(out_shapes), start, finish
        n = len(self.ins)
        self.sems = [pltpu.SemaphoreType.DMA((n * N_PEERS,)), pltpu.SemaphoreType.DMA((n * N_PEERS,)),
                     pltpu.SemaphoreType.DMA((n,))]


def gather_ride(shards, kinds):
    n = len(shards)

    def setup(ins, outs, sems):
        send_sems, recv_sems, local_sems = sems
        x, y, c = _place()
        chips = [(1 - x, y), (x, 1 - y), (1 - x, 1 - y)]

        def copy(a, k, block, to, src=None):
            slot = _chunk(outs[a], kinds[a], _dev_index(block))
            return pltpu.make_async_remote_copy(
                src_ref=slot if src is None else src, dst_ref=slot,
                send_sem=send_sems.at[a * N_PEERS + k], recv_sem=recv_sems.at[a * N_PEERS + k],
                device_id=to, device_id_type=MESH)

        def mine(a):
            return pltpu.make_async_copy(ins[a], _chunk(outs[a], kinds[a], _dev_index((x, y, c))), local_sems.at[a])

        def first(a):
            return [copy(a, 0, (x, y, c), (x, y, 1 - c), src=ins[a])] + [
                copy(a, 1 + j, (x, y, c), (*chip, c), src=ins[a]) for j, chip in enumerate(chips)]

        return (x, y, c), chips, copy, mine, first

    def start(ins, outs, sems):
        _, _, _, mine, first = setup(ins, outs, sems)
        for a in range(n):
            mine(a).start()
            for cp in first(a):
                cp.start()

    def finish(ins, outs, sems):
        (x, y, c), chips, copy, mine, first = setup(ins, outs, sems)
        me, sibling = (x, y, c), (x, y, 1 - c)
        for a in range(n):
            for j, chip in enumerate(chips):
                copy(a, 1 + j, (*chip, c), me).wait_recv()
                copy(a, 4 + j, (*chip, c), sibling).start()
        for a in range(n):
            copy(a, 0, sibling, me).wait_recv()
            for j, chip in enumerate(chips):
                copy(a, 4 + j, (*chip, 1 - c), me).wait_recv()
        for a in range(n):
            for cp in first(a):
                cp.wait_send()
            for j, chip in enumerate(chips):
                copy(a, 4 + j, (*chip, c), sibling).wait_send()
        for a in range(n):
            mine(a).wait()

    shapes = [(N_DEV, *s.shape) if kind == "lead" else (s.shape[0], N_DEV * s.shape[1]) for s, kind in zip(shards, kinds)]
    return Ride(shards, [jax.ShapeDtypeStruct(shape, s.dtype) for shape, s in zip(shapes, shards)], start, finish)


def _chunk(ref, kind, j):
    if kind == "lead":
        return ref.at[j]
    if kind == "rows":
        r = ref.shape[0] // N_DEV
        return ref.at[pl.ds(j * r, r)]
    if kind == "mid":
        r = ref.shape[1] // N_DEV
        return ref.at[:, pl.ds(j * r, r), :]
    if kind == "cols":
        c = ref.shape[1] // N_DEV
        return ref.at[:, pl.ds(j * c, c)]
    return ref


def _chunk_shape(shape, kind):
    if kind == "lead":
        return tuple(shape[1:])
    if kind == "rows":
        return (shape[0] // N_DEV, *shape[1:])
    if kind == "mid":
        return (shape[0], shape[1] // N_DEV, shape[2])
    if kind == "cols":
        return (shape[0], shape[1] // N_DEV)
    return tuple(shape)


def scatter_ride(parts):
    n = len(parts)
    kinds = [k for _, k in parts]

    def setup(ins, outs, sems):
        send_sems, recv_sems, local_sems = sems
        x, y, c = _place()
        me = _dev_index((x, y, c))
        peers = []
        for k in range(1, N_DEV):
            kx, ky, kc = (k >> 2) & 1, (k >> 1) & 1, k & 1
            peers.append((1 - x if kx else x, 1 - y if ky else y, 1 - c if kc else c))

        def copy(a, k, peer):
            return pltpu.make_async_remote_copy(
                src_ref=_chunk(ins[a], kinds[a], _dev_index(peer)), dst_ref=outs[a].at[me],
                send_sem=send_sems.at[a * N_PEERS + k], recv_sem=recv_sems.at[a * N_PEERS + k],
                device_id=peer, device_id_type=MESH)

        def arrival(a, k, peer):
            slot = outs[a].at[_dev_index(peer)]
            return pltpu.make_async_remote_copy(
                src_ref=slot, dst_ref=slot,
                send_sem=send_sems.at[a * N_PEERS + k], recv_sem=recv_sems.at[a * N_PEERS + k],
                device_id=peer, device_id_type=MESH)

        def mine(a):
            return pltpu.make_async_copy(_chunk(ins[a], kinds[a], me), outs[a].at[me], local_sems.at[a])

        return peers, copy, arrival, mine

    def start(ins, outs, sems):
        peers, copy, _, mine = setup(ins, outs, sems)
        for a in range(n):
            mine(a).start()
            for k, peer in enumerate(peers):
                copy(a, k, peer).start()

    def finish(ins, outs, sems):
        peers, copy, arrival, mine = setup(ins, outs, sems)
        for a in range(n):
            for k, peer in enumerate(peers):
                arrival(a, k, peer).wait_recv()
        for a in range(n):
            for k, peer in enumerate(peers):
                copy(a, k, peer).wait_send()
            mine(a).wait()

    shapes = [jax.ShapeDtypeStruct((N_DEV, *_chunk_shape(arr.shape, kind)), arr.dtype) for arr, kind in parts]
    return Ride([arr for arr, _ in parts], shapes, start, finish)


HBM = pl.BlockSpec(memory_space=pltpu.HBM)
SEM = pl.BlockSpec(memory_space=pltpu.SEMAPHORE)
DATAFLOW = pltpu.SideEffectType.DATAFLOW_SIDE_EFFECTING
TOKEN = (8, 128)


def _scatter_copies(kinds, ins, lands, send_sems, recv_sems):
    x, y, c = _place()
    me = _dev_index((x, y, c))
    sends, arrivals = [], []
    for a, kind in enumerate(kinds):
        for k in range(1, N_DEV):
            kx, ky, kc = (k >> 2) & 1, (k >> 1) & 1, k & 1
            peer = (1 - x if kx else x, 1 - y if ky else y, 1 - c if kc else c)
            sem = a * N_PEERS + k - 1
            sends.append(pltpu.make_async_remote_copy(
                src_ref=_chunk(ins[a], kind, _dev_index(peer)), dst_ref=lands[a].at[me],
                send_sem=send_sems.at[sem], recv_sem=recv_sems.at[sem], device_id=peer, device_id_type=MESH))
            slot = lands[a].at[_dev_index(peer)]
            arrivals.append(pltpu.make_async_remote_copy(
                src_ref=slot, dst_ref=slot, send_sem=send_sems.at[sem], recv_sem=recv_sems.at[sem],
                device_id=peer, device_id_type=MESH))
    return me, sends, arrivals


def own_blocks(parts, name):
    n = len(parts)
    x, y, c = _place()
    me = jnp.reshape(_dev_index((x, y, c)), (1,)).astype(jnp.int32)

    def block_of(shape, kind):
        blk = _chunk_shape(shape, kind)
        if kind == "lead":
            return pl.BlockSpec((1, *blk), lambda i, me_ref: (me_ref[0], *[0] * len(blk)))
        if kind == "rows":
            return pl.BlockSpec(blk, lambda i, me_ref: (me_ref[0], *[0] * (len(blk) - 1)))
        if kind == "cols":
            return pl.BlockSpec(blk, lambda i, me_ref: (0, me_ref[0]))
        raise NotImplementedError(kind)

    def body(me_ref, *refs):
        for (arr, kind), src, dst in zip(parts, refs[:n], refs[n:]):
            dst[0] = src[0] if kind == "lead" else src[...]

    shapes = [(N_DEV, *_chunk_shape(a.shape, k)) for a, k in parts]
    return pl.pallas_call(
        body, name=name,
        grid_spec=pltpu.PrefetchScalarGridSpec(
            num_scalar_prefetch=1, grid=(1,),
            in_specs=[block_of(a.shape, k) for a, k in parts],
            out_specs=[pl.BlockSpec((1, *s[1:]), lambda i, me_ref, r=len(s) - 1: (me_ref[0], *[0] * r)) for s in shapes]),
        out_shape=[jax.ShapeDtypeStruct(s, a.dtype) for s, (a, _) in zip(shapes, parts)],
        compiler_params=pltpu.CompilerParams(vmem_limit_bytes=VMEM_LIMIT),
    )(me, *[a for a, _ in parts])


def scatter_start(parts, name):
    n = len(parts)
    kinds = [k for _, k in parts]
    arrays = [pltpu.with_memory_space_constraint(a, pltpu.HBM) for a, _ in parts]
    zones = [pltpu.with_memory_space_constraint(z, pltpu.HBM) for z in own_blocks(parts, name + "_own")]

    def body(*refs):
        ins, lands = refs[:n], refs[n:2 * n]
        send_sems, recv_sems = refs[2 * n], refs[2 * n + 1]
        token = refs[4 * n + 2]
        _, sends, _ = _scatter_copies(kinds, ins, lands, send_sems, recv_sems)
        for cp in sends:
            cp.start()
        token[...] = jnp.zeros_like(token)

    outs = pl.pallas_call(
        body, name=name,
        out_shape=(pltpu.SemaphoreType.DMA((n * N_PEERS,)), pltpu.SemaphoreType.DMA((n * N_PEERS,)),
                   *[pltpu.HBM(a.shape, a.dtype) for a in arrays], *[pltpu.HBM(z.shape, z.dtype) for z in zones],
                   jax.ShapeDtypeStruct(TOKEN, F32)),
        in_specs=[HBM] * (2 * n),
        out_specs=(SEM, SEM, *[HBM] * (2 * n), pl.BlockSpec(memory_space=pltpu.VMEM)),
        input_output_aliases={i: 2 + i for i in range(2 * n)},
        compiler_params=pltpu.CompilerParams(has_side_effects=DATAFLOW),
    )(*arrays, *zones)
    return kinds, outs[0], outs[1], outs[2:2 + n], outs[2 + n:2 + 2 * n], outs[2 + 2 * n]


def scatter_wait(started, after, name):
    kinds, send_sems, recv_sems, arrays, zones, _ = started
    n = len(kinds)

    def body(*refs):
        ins, lands = refs[:n], refs[n:2 * n]
        _, sends, arrivals = _scatter_copies(kinds, ins, lands, refs[2 * n], refs[2 * n + 1])
        for cp in sends:
            cp.wait_send()
        for cp in arrivals:
            cp.wait_recv()

    outs = pl.pallas_call(
        body, name=name,
        out_shape=(*[pltpu.HBM(a.shape, a.dtype) for a in arrays], *[pltpu.HBM(z.shape, z.dtype) for z in zones]),
        in_specs=[HBM] * (2 * n) + [SEM, SEM] + [ANY] * len(after),
        out_specs=[HBM] * (2 * n),
        input_output_aliases={i: i for i in range(2 * n)},
        compiler_params=pltpu.CompilerParams(has_side_effects=DATAFLOW),
    )(*arrays, *zones, send_sems, recv_sems, *after)
    return outs[n:]


def run_ride(ride, name):
    n_in, n_out = len(ride.ins), len(ride.out_shapes)

    def body(*refs):
        ins, outs, sems = refs[:n_in], refs[n_in:n_in + n_out], refs[n_in + n_out:]
        ride.start(ins, outs, sems)
        ride.finish(ins, outs, sems)

    return pl.pallas_call(
        body, name=name, in_specs=[ANY] * n_in, out_specs=[ANY] * n_out, out_shape=ride.out_shapes,
        scratch_shapes=ride.sems,
    )(*ride.ins)


def _call(body, *, name, grid, in_specs, out_specs, out_shape, args, scratch_shapes=(), ride=None):
    params = pltpu.CompilerParams(dimension_semantics=("arbitrary",) * len(grid), vmem_limit_bytes=VMEM_LIMIT)
    if ride is None:
        outs = pl.pallas_call(body, name=name, grid=grid, in_specs=in_specs, out_specs=out_specs, out_shape=out_shape,
                              scratch_shapes=list(scratch_shapes), compiler_params=params)(*args)
        return outs, []
    n_in, n_out, n_scr = len(in_specs), len(out_specs), len(scratch_shapes)
    r_in, r_out = len(ride.ins), len(ride.out_shapes)

    def hosted(*refs):
        ins, refs = refs[:n_in], refs[n_in:]
        rins, refs = refs[:r_in], refs[r_in:]
        outs, refs = refs[:n_out], refs[n_out:]
        routs, refs = refs[:r_out], refs[r_out:]
        scratch, sems = refs[:n_scr], refs[n_scr:]
        step, n_steps = pl.program_id(0), grid[0]
        for d in range(1, len(grid)):
            step, n_steps = step * grid[d] + pl.program_id(d), n_steps * grid[d]

        @pl.when(step == 0)
        def _():
            ride.start(rins, routs, sems)

        body(*ins, *outs, *scratch)

        @pl.when(step == n_steps - 1)
        def _():
            ride.finish(rins, routs, sems)

    outs = pl.pallas_call(
        hosted, name=name + "_ride", grid=grid,
        in_specs=list(in_specs) + [ANY] * r_in, out_specs=list(out_specs) + [ANY] * r_out,
        out_shape=list(out_shape) + ride.out_shapes,
        scratch_shapes=list(scratch_shapes) + ride.sems, compiler_params=params,
    )(*args, *ride.ins)
    return outs[:n_out], outs[n_out:]


def ffn_fwd(h, gain, wgu, wd, ride=None):
    t = h.shape[0]
    tf = min(TF, t)

    def body(h_ref, g_ref, wgu_ref, wd_ref, hn_ref, z_ref, u_ref, acc):
        k = pl.program_id(1)

        @pl.when(k == 0)
        def _():
            u_ref[...] = _rms_fwd(h_ref[...], g_ref[...]).astype(BF16)
            acc[...] = jnp.zeros_like(acc)

        u = u_ref[...]
        g = _dot_nt(u, wgu_ref[0, 0])
        up = _dot_nt(u, wgu_ref[1, 0])
        z_ref[0, 0] = g.astype(BF16)
        z_ref[1, 0] = up.astype(BF16)
        a = g * jax.nn.sigmoid(g) * up
        acc[...] += _dot(a.astype(BF16), wd_ref[0])

        @pl.when(k == 3)
        def _():
            hn_ref[...] = h_ref[...] + acc[...]

    return _call(
        body, name="ffn_fwd", grid=(t // tf, 4), ride=ride,
        in_specs=[pl.BlockSpec((tf, D), lambda i, k: (i, 0)),
                  pl.BlockSpec((1, D), lambda i, k: (0, 0)),
                  pl.BlockSpec((2, 1, FC, D), lambda i, k: (0, k, 0, 0)),
                  pl.BlockSpec((1, FC, D), lambda i, k: (k, 0, 0))],
        out_specs=[pl.BlockSpec((tf, D), lambda i, k: (i, 0)),
                   pl.BlockSpec((2, 1, tf, FC), lambda i, k: (0, k, i, 0)),
                   pl.BlockSpec((tf, D), lambda i, k: (i, 0))],
        out_shape=[jax.ShapeDtypeStruct((t, D), F32),
                   jax.ShapeDtypeStruct((2, 4, t, FC), BF16),
                   jax.ShapeDtypeStruct((t, D), BF16)],
        scratch_shapes=[pltpu.VMEM((tf, D), F32)],
        args=(h, gain, wgu, wd))


def ffn_bwd_x(dh, h, gain, z, wgu, wd, ride=None):
    t = h.shape[0]

    def body(dh_ref, h_ref, g_ref, z_ref, wgu_ref, wd_ref, dhp_ref, dz_ref, a_ref, dgain_ref, dhb, du):
        i, k = pl.program_id(0), pl.program_id(1)

        @pl.when(k == 0)
        def _():
            dhb[...] = dh_ref[...].astype(BF16)
            du[...] = jnp.zeros_like(du)

        @pl.when((k == 0) & (i == 0))
        def _():
            dgain_ref[...] = jnp.zeros_like(dgain_ref)

        da = _dot_nt(dhb[...], wd_ref[0])
        g = z_ref[0, 0].astype(F32)
        up = z_ref[1, 0].astype(F32)
        sg = jax.nn.sigmoid(g)
        silu = g * sg
        a_ref[0] = (silu * up).astype(BF16)
        dg = (da * up * (sg * (1.0 + g * (1.0 - sg)))).astype(BF16)
        dup = (da * silu).astype(BF16)
        dz_ref[0, 0] = dg
        dz_ref[1, 0] = dup
        for n in range(2):
            cols = slice(n * (D // 2), (n + 1) * (D // 2))
            du[:, cols] += _dot(dg, wgu_ref[0, 0, :, cols]) + _dot(dup, wgu_ref[1, 0, :, cols])

        @pl.when(k == 3)
        def _():
            dx, dgain = _rms_bwd(h_ref[...], g_ref[...], du[...])
            dhp_ref[...] = dh_ref[...] + dx
            dgain_ref[...] += dgain

    return _call(
        body, name="ffn_bwd_x", grid=(t // TM, 4), ride=ride,
        in_specs=[pl.BlockSpec((TM, D), lambda i, k: (i, 0)),
                  pl.BlockSpec((TM, D), lambda i, k: (i, 0)),
                  pl.BlockSpec((1, D), lambda i, k: (0, 0)),
                  pl.BlockSpec((2, 1, TM, FC), lambda i, k: (0, k, i, 0)),
                  pl.BlockSpec((2, 1, FC, D), lambda i, k: (0, k, 0, 0)),
                  pl.BlockSpec((1, FC, D), lambda i, k: (k, 0, 0))],
        out_specs=[pl.BlockSpec((TM, D), lambda i, k: (i, 0)),
                   pl.BlockSpec((2, 1, TM, FC), lambda i, k: (0, k, i, 0)),
                   pl.BlockSpec((1, TM, FC), lambda i, k: (k, i, 0)),
                   pl.BlockSpec((1, D), lambda i, k: (0, 0)),
                   pl.BlockSpec((TM, D), lambda i, k: (i, 0))],
        out_shape=[jax.ShapeDtypeStruct((t, D), F32),
                   jax.ShapeDtypeStruct((2, 4, t, FC), BF16),
                   jax.ShapeDtypeStruct((4, t, FC), BF16),
                   jax.ShapeDtypeStruct((1, D), F32),
                   jax.ShapeDtypeStruct((t, D), BF16)],
        scratch_shapes=[pltpu.VMEM((TM, D), F32)],
        args=(dh, h, gain, z, wgu, wd))


TF = 1024
TW = 2048


def ffn_bwd_w(u, dz, a, dhb, ride=None):
    t = u.shape[0]
    tw = min(TW, t)
    steps = t // tw

    def body(u_ref, dz_ref, a_ref, dh_ref, dwgu_ref, dwd_ref, acc_gu, acc_d):
        j = pl.program_id(1)

        @pl.when(j == 0)
        def _():
            acc_gu[...] = jnp.zeros_like(acc_gu)
            acc_d[...] = jnp.zeros_like(acc_d)

        ub = u_ref[...]
        acc_gu[0] += _dot_tn(dz_ref[0, 0], ub)
        acc_gu[1] += _dot_tn(dz_ref[1, 0], ub)
        acc_d[...] += _dot_tn(a_ref[0], dh_ref[...])

        @pl.when(j == steps - 1)
        def _():
            dwgu_ref[:, 0] = acc_gu[...].astype(BF16)
            dwd_ref[0] = acc_d[...].astype(BF16)

    return _call(
        body, name="ffn_bwd_w", grid=(4, steps), ride=ride,
        in_specs=[pl.BlockSpec((tw, D), lambda k, j: (j, 0)),
                  pl.BlockSpec((2, 1, tw, FC), lambda k, j: (0, k, j, 0)),
                  pl.BlockSpec((1, tw, FC), lambda k, j: (k, j, 0)),
                  pl.BlockSpec((tw, D), lambda k, j: (j, 0))],
        out_specs=[pl.BlockSpec((2, 1, FC, D), lambda k, j: (0, k, 0, 0)),
                   pl.BlockSpec((1, FC, D), lambda k, j: (k, 0, 0))],
        out_shape=[jax.ShapeDtypeStruct((2, 4, FC, D), BF16),
                   jax.ShapeDtypeStruct((4, FC, D), BF16)],
        scratch_shapes=[pltpu.VMEM((2, FC, D), F32), pltpu.VMEM((FC, D), F32)],
        args=(u, dz, a, dhb))


def _prev_halo(i, tile=TM):
    return jnp.maximum(i * (tile // HALO) - 1, 0)


def _next_halo(i, t, tile=TM):
    return jnp.minimum((i + 1) * (tile // HALO), t // HALO - 1)


def rms_matmul(h, gain, w, bias, ride=None):
    t = h.shape[0]
    n = w.shape[1]

    def body(h_ref, g_ref, w_ref, b_ref, z_ref, u_ref):
        u = _rms_fwd(h_ref[...], g_ref[...]).astype(BF16)
        u_ref[...] = u
        z_ref[...] = (_dot(u, w_ref[...]) + b_ref[...]).astype(BF16)

    return _call(
        body, name=f"rms_matmul_{n}", grid=(t // TM,), ride=ride,
        in_specs=[pl.BlockSpec((TM, D), lambda i: (i, 0)),
                  pl.BlockSpec((1, D), lambda i: (0, 0)),
                  pl.BlockSpec((D, n), lambda i: (0, 0)),
                  pl.BlockSpec((1, n), lambda i: (0, 0))],
        out_specs=[pl.BlockSpec((TM, n), lambda i: (i, 0)),
                   pl.BlockSpec((TM, D), lambda i: (i, 0))],
        out_shape=[jax.ShapeDtypeStruct((t, n), BF16),
                   jax.ShapeDtypeStruct((t, D), BF16)],
        args=(h, gain, w, bias))


def in_proj_bwd_x(dz, w, h, gain, dh, ride=None):
    t = h.shape[0]
    n = w.shape[1]

    def body(dz_ref, w_ref, h_ref, g_ref, dh_ref, dhp_ref, dgain_ref, dbias_ref):
        @pl.when(pl.program_id(0) == 0)
        def _():
            dgain_ref[...] = jnp.zeros_like(dgain_ref)
            dbias_ref[...] = jnp.zeros_like(dbias_ref)

        du = _dot_nt(dz_ref[...], w_ref[...])
        dx, dgain = _rms_bwd(h_ref[...], g_ref[...], du)
        dhp_ref[...] = dh_ref[...] + dx
        dgain_ref[...] += dgain
        dbias_ref[...] += jnp.sum(dz_ref[...].astype(F32), axis=0, keepdims=True)

    return _call(
        body, name=f"in_proj_bwd_x_{n}", grid=(t // TM,), ride=ride,
        in_specs=[pl.BlockSpec((TM, n), lambda i: (i, 0)),
                  pl.BlockSpec((D, n), lambda i: (0, 0)),
                  pl.BlockSpec((TM, D), lambda i: (i, 0)),
                  pl.BlockSpec((1, D), lambda i: (0, 0)),
                  pl.BlockSpec((TM, D), lambda i: (i, 0))],
        out_specs=[pl.BlockSpec((TM, D), lambda i: (i, 0)),
                   pl.BlockSpec((1, D), lambda i: (0, 0)),
                   pl.BlockSpec((1, n), lambda i: (0, 0))],
        out_shape=[jax.ShapeDtypeStruct((t, D), F32),
                   jax.ShapeDtypeStruct((1, D), F32),
                   jax.ShapeDtypeStruct((1, n), F32)],
        args=(dz, w, h, gain, dh))


def in_proj_bwd_w(u, dz, ride=None):
    t = u.shape[0]
    n = dz.shape[1]
    steps = t // TM

    def body(u_ref, dz_ref, dw_ref, acc):
        s = pl.program_id(0)

        @pl.when(s == 0)
        def _():
            acc[...] = jnp.zeros_like(acc)

        acc[...] += _dot_tn(u_ref[...], dz_ref[...])

        @pl.when(s == steps - 1)
        def _():
            dw_ref[...] = acc[...].astype(BF16)

    return _call(
        body, name=f"in_proj_bwd_w_{n}", grid=(steps,), ride=ride,
        in_specs=[pl.BlockSpec((TM, D), lambda s: (s, 0)),
                  pl.BlockSpec((TM, n), lambda s: (s, 0))],
        out_specs=[pl.BlockSpec((D, n), lambda s: (0, 0))],
        out_shape=[jax.ShapeDtypeStruct((D, n), BF16)],
        scratch_shapes=[pltpu.VMEM((D, n), F32)],
        args=(u, dz))


A_TAPS = 3


def a_mix_fwd(z, h, conv, wout, ride=None):
    t = h.shape[0]

    def body(z_ref, zp_ref, h_ref, cw_ref, wo_ref, hn_ref, pad):
        i = pl.program_id(0)
        ph = zp_ref[:, D:2 * D].astype(F32) * zp_ref[:, 2 * D:].astype(F32)
        pad[0:HALO, :] = jnp.where(i == 0, 0.0, ph)
        pad[HALO:, :] = z_ref[:, D:2 * D].astype(F32) * z_ref[:, 2 * D:].astype(F32)
        q = jnp.zeros((TM, D), F32)
        for k in range(A_TAPS):
            off = HALO - (A_TAPS - 1) + k
            q += cw_ref[k:k + 1, :] * pad[off:off + TM, :]
        r = z_ref[:, 0:D].astype(F32) * q
        hn_ref[...] = h_ref[...] + _dot(r.astype(BF16), wo_ref[...])

    return _call(
        body, name="a_mix_fwd", grid=(t // TM,), ride=ride,
        in_specs=[pl.BlockSpec((TM, 3 * D), lambda i: (i, 0)),
                  pl.BlockSpec((HALO, 3 * D), lambda i: (_prev_halo(i), 0)),
                  pl.BlockSpec((TM, D), lambda i: (i, 0)),
                  pl.BlockSpec((A_TAPS, D), lambda i: (0, 0)),
                  pl.BlockSpec((D, D), lambda i: (0, 0))],
        out_specs=[pl.BlockSpec((TM, D), lambda i: (i, 0))],
        out_shape=[jax.ShapeDtypeStruct((t, D), F32)],
        scratch_shapes=[pltpu.VMEM((HALO + TM, D), F32)],
        args=(z, z, h, conv, wout))


def a_mix_bwd(dh, z, conv, wout, ride=None):
    t = dh.shape[0]
    steps = t // TM

    def body(dh_ref, dhn_ref, z_ref, zp_ref, zn_ref, cw_ref, wo_ref, dz_ref, dwo_ref, dcw_ref, pad, dqpad, dwo):
        i = pl.program_id(0)
        last = i == steps - 1

        @pl.when(i == 0)
        def _():
            dwo[...] = jnp.zeros_like(dwo)
            dcw_ref[...] = jnp.zeros_like(dcw_ref)

        ph = zp_ref[:, D:2 * D].astype(F32) * zp_ref[:, 2 * D:].astype(F32)
        pad[0:HALO, :] = jnp.where(i == 0, 0.0, ph)
        c = z_ref[:, D:2 * D].astype(F32)
        v = z_ref[:, 2 * D:].astype(F32)
        pad[HALO:, :] = c * v
        q = jnp.zeros((TM, D), F32)
        for k in range(A_TAPS):
            off = HALO - (A_TAPS - 1) + k
            q += cw_ref[k:k + 1, :] * pad[off:off + TM, :]
        b = z_ref[:, 0:D].astype(F32)
        dhb = dh_ref[...].astype(BF16)
        dwo[...] += _dot_tn((b * q).astype(BF16), dhb)
        dr = _dot_nt(dhb, wo_ref[...])
        dz_ref[:, 0:D] = (dr * q).astype(BF16)
        dq = dr * b
        drn = _dot_nt(dhn_ref[...].astype(BF16), wo_ref[...])
        dqpad[0:TM, :] = dq
        dqpad[TM:, :] = jnp.where(last, 0.0, drn * zn_ref[:, 0:D].astype(F32))
        dp = jnp.zeros((TM, D), F32)
        for k in range(A_TAPS):
            off = A_TAPS - 1 - k
            dp += cw_ref[k:k + 1, :] * dqpad[off:off + TM, :]
            poff = HALO - (A_TAPS - 1) + k
            dcw_ref[k:k + 1, :] += jnp.sum(dq * pad[poff:poff + TM, :], axis=0, keepdims=True)
        dz_ref[:, D:2 * D] = (dp * v).astype(BF16)
        dz_ref[:, 2 * D:] = (dp * c).astype(BF16)

        @pl.when(last)
        def _():
            dwo_ref[...] = dwo[...].astype(BF16)

    return _call(
        body, name="a_mix_bwd", grid=(steps,), ride=ride,
        in_specs=[pl.BlockSpec((TM, D), lambda i: (i, 0)),
                  pl.BlockSpec((HALO, D), lambda i: (_next_halo(i, t), 0)),
                  pl.BlockSpec((TM, 3 * D), lambda i: (i, 0)),
                  pl.BlockSpec((HALO, 3 * D), lambda i: (_prev_halo(i), 0)),
                  pl.BlockSpec((HALO, 3 * D), lambda i: (_next_halo(i, t), 0)),
                  pl.BlockSpec((A_TAPS, D), lambda i: (0, 0)),
                  pl.BlockSpec((D, D), lambda i: (0, 0))],
        out_specs=[pl.BlockSpec((TM, 3 * D), lambda i: (i, 0)),
                   pl.BlockSpec((D, D), lambda i: (0, 0)),
                   pl.BlockSpec((A_TAPS, D), lambda i: (0, 0))],
        out_shape=[jax.ShapeDtypeStruct((t, 3 * D), BF16),
                   jax.ShapeDtypeStruct((D, D), BF16),
                   jax.ShapeDtypeStruct((A_TAPS, D), F32)],
        scratch_shapes=[pltpu.VMEM((HALO + TM, D), F32), pltpu.VMEM((TM + HALO, D), F32), pltpu.VMEM((D, D), F32)],
        args=(dh, dh, z, z, z, conv, wout))


C_TAPS = 31


def _glu(zr):
    return zr[:, 0:D].astype(F32) * jax.nn.sigmoid(zr[:, D:].astype(F32))


def _ln_silu(h2, lg, lb):
    mu = jnp.mean(h2, axis=-1, keepdims=True)
    xc = h2 - mu
    rstd = lax.rsqrt(jnp.mean(xc * xc, axis=-1, keepdims=True) + LN_EPS)
    xn = xc * rstd
    h3 = xn * lg + lb
    s3 = jax.nn.sigmoid(h3)
    return xn, rstd, h3, s3


def _ln_silu_bwd(h2, lg, lb, dh4):
    xn, rstd, h3, s3 = _ln_silu(h2, lg, lb)
    dh3 = dh4 * (s3 * (1.0 + h3 * (1.0 - s3)))
    dxn = dh3 * lg
    dh2 = rstd * (dxn - jnp.mean(dxn, axis=-1, keepdims=True) - xn * jnp.mean(dxn * xn, axis=-1, keepdims=True))
    return dh2, dh3, xn, h3 * s3


TC = 256
RB = 64
LANES = 128
SHIFTS = 7


def _shifted_copies(src, sh, rows):
    for b in range(1, SHIFTS + 1):
        sh[b - 1, 0:rows, :] = src[b:b + rows, :]


def _window(src, sh, o, r0, lanes):
    a, b = divmod(o, 8)
    ref = src if b == 0 else sh.at[b - 1]
    return ref[8 * a + r0:8 * a + r0 + RB, lanes]


def c_mix_fwd(z, h, dw, bdw, lg, lb, w2, b2, ride=None):
    t = h.shape[0]

    def body(z_ref, zp_ref, h_ref, dw_ref, bdw_ref, lg_ref, lb_ref, w2_ref, b2_ref, hn_ref, h2_ref, pad, sh):
        i = pl.program_id(0)
        pad[0:HALO, :] = jnp.where(i == 0, 0.0, _glu(zp_ref))
        pad[HALO:, :] = _glu(z_ref)
        _shifted_copies(pad, sh, TC + 24)
        for l in range(D // LANES):
            lanes = slice(l * LANES, (l + 1) * LANES)
            for r0 in range(0, TC, RB):
                acc = jnp.zeros((RB, LANES), F32) + bdw_ref[:, lanes]
                for k in range(C_TAPS):
                    acc += dw_ref[k:k + 1, lanes] * _window(pad, sh, HALO - (C_TAPS - 1) + k, r0, lanes)
                h2_ref[r0:r0 + RB, lanes] = acc
        _, _, h3, s3 = _ln_silu(h2_ref[...], lg_ref[...], lb_ref[...])
        hn_ref[...] = h_ref[...] + _dot((h3 * s3).astype(BF16), w2_ref[...]) + b2_ref[...]

    vec = pl.BlockSpec((1, D), lambda i: (0, 0))
    return _call(
        body, name="c_mix_fwd", grid=(t // TC,), ride=ride,
        in_specs=[pl.BlockSpec((TC, 2 * D), lambda i: (i, 0)),
                  pl.BlockSpec((HALO, 2 * D), lambda i: (_prev_halo(i, TC), 0)),
                  pl.BlockSpec((TC, D), lambda i: (i, 0)),
                  pl.BlockSpec((C_TAPS, D), lambda i: (0, 0)),
                  vec, vec, vec,
                  pl.BlockSpec((D, D), lambda i: (0, 0)),
                  vec],
        out_specs=[pl.BlockSpec((TC, D), lambda i: (i, 0)),
                   pl.BlockSpec((TC, D), lambda i: (i, 0))],
        out_shape=[jax.ShapeDtypeStruct((t, D), F32),
                   jax.ShapeDtypeStruct((t, D), F32)],
        scratch_shapes=[pltpu.VMEM((HALO + TC, D), F32), pltpu.VMEM((SHIFTS, TC + 24, D), F32)],
        args=(z, z, h, dw, bdw, lg, lb, w2, b2))


def c_mix_bwd(dh, z, h2, dw, lg, lb, w2, ride=None):
    t = dh.shape[0]
    steps = t // TC

    def body(dh_ref, dhn_ref, z_ref, zp_ref, h2_ref, h2n_ref, dw_ref, lg_ref, lb_ref, w2_ref,
             dz_ref, dw2_ref, db2_ref, dlg_ref, dlb_ref, dbdw_ref, ddw_ref, pad, dpad, dw2, sh, dh1):
        i = pl.program_id(0)
        last = i == steps - 1

        @pl.when(i == 0)
        def _():
            for r in (dw2, db2_ref, dlg_ref, dlb_ref, dbdw_ref, ddw_ref):
                r[...] = jnp.zeros_like(r)

        lg, lb = lg_ref[...], lb_ref[...]
        dh = dh_ref[...]
        dhb = dh.astype(BF16)
        dh2, dh3, xn, h4 = _ln_silu_bwd(h2_ref[...], lg, lb, _dot_nt(dhb, w2_ref[...]))
        dw2[...] += _dot_tn(h4.astype(BF16), dhb)
        db2_ref[...] += jnp.sum(dh, axis=0, keepdims=True)
        dlg_ref[...] += jnp.sum(dh3 * xn, axis=0, keepdims=True)
        dlb_ref[...] += jnp.sum(dh3, axis=0, keepdims=True)
        dbdw_ref[...] += jnp.sum(dh2, axis=0, keepdims=True)
        dh2n, _, _, _ = _ln_silu_bwd(h2n_ref[...], lg, lb, _dot_nt(dhn_ref[...].astype(BF16), w2_ref[...]))
        dpad[0:TC, :] = dh2
        dpad[TC:, :] = jnp.where(last, 0.0, dh2n)
        _shifted_copies(dpad, sh, TC + 24)
        for l in range(D // LANES):
            lanes = slice(l * LANES, (l + 1) * LANES)
            for r0 in range(0, TC, RB):
                acc = jnp.zeros((RB, LANES), F32)
                for k in range(C_TAPS):
                    acc += dw_ref[k:k + 1, lanes] * _window(dpad, sh, C_TAPS - 1 - k, r0, lanes)
                dh1[r0:r0 + RB, lanes] = acc
        pad[0:HALO, :] = jnp.where(i == 0, 0.0, _glu(zp_ref))
        pad[HALO:, :] = _glu(z_ref)
        _shifted_copies(pad, sh, TC + 24)
        for l in range(D // LANES):
            lanes = slice(l * LANES, (l + 1) * LANES)
            accs = [jnp.zeros((8, LANES), F32) for _ in range(C_TAPS)]
            for r0 in range(0, TC, RB):
                d = dpad[r0:r0 + RB, lanes]
                for k in range(C_TAPS):
                    prod = d * _window(pad, sh, HALO - (C_TAPS - 1) + k, r0, lanes)
                    accs[k] += jnp.sum(prod.reshape(RB // 8, 8, LANES), axis=0)
            for k in range(C_TAPS):
                ddw_ref[k:k + 1, lanes] += jnp.sum(accs[k], axis=0, keepdims=True)
        a = z_ref[:, 0:D].astype(F32)
        sg = jax.nn.sigmoid(z_ref[:, D:].astype(F32))
        d1 = dh1[...]
        dz_ref[:, 0:D] = (d1 * sg).astype(BF16)
        dz_ref[:, D:] = (d1 * a * sg * (1.0 - sg)).astype(BF16)

        @pl.when(last)
        def _():
            dw2_ref[...] = dw2[...].astype(BF16)

    vec = pl.BlockSpec((1, D), lambda i: (0, 0))
    return _call(
        body, name="c_mix_bwd", grid=(steps,), ride=ride,
        in_specs=[pl.BlockSpec((TC, D), lambda i: (i, 0)),
                  pl.BlockSpec((HALO, D), lambda i: (_next_halo(i, t, TC), 0)),
                  pl.BlockSpec((TC, 2 * D), lambda i: (i, 0)),
                  pl.BlockSpec((HALO, 2 * D), lambda i: (_prev_halo(i, TC), 0)),
                  pl.BlockSpec((TC, D), lambda i: (i, 0)),
                  pl.BlockSpec((HALO, D), lambda i: (_next_halo(i, t, TC), 0)),
                  pl.BlockSpec((C_TAPS, D), lambda i: (0, 0)),
                  vec, vec,
                  pl.BlockSpec((D, D), lambda i: (0, 0))],
        out_specs=[pl.BlockSpec((TC, 2 * D), lambda i: (i, 0)),
                   pl.BlockSpec((D, D), lambda i: (0, 0)),
                   vec, vec, vec, vec,
                   pl.BlockSpec((C_TAPS, D), lambda i: (0, 0))],
        out_shape=[jax.ShapeDtypeStruct((t, 2 * D), BF16),
                   jax.ShapeDtypeStruct((D, D), BF16)]
                  + [jax.ShapeDtypeStruct((1, D), F32)] * 4
                  + [jax.ShapeDtypeStruct((C_TAPS, D), F32)],
        scratch_shapes=[pltpu.VMEM((HALO + TC, D), F32), pltpu.VMEM((TC + HALO, D), F32), pltpu.VMEM((D, D), F32),
                        pltpu.VMEM((SHIFTS, TC + 24, D), F32), pltpu.VMEM((TC, D), F32)],
        args=(dh, dh, z, z, h2, h2, dw, lg, lb, w2))


POOL_WINDOWS = (2, 4, 8, 16)
GW = D // len(POOL_WINDOWS)


def _pool_mixed(pad, g, w, inv_cnt):
    cols = slice(g * GW, (g + 1) * GW)
    s = pad[HALO:HALO + TM, cols]
    u = s
    for j in range(1, w):
        s = s + pad[HALO - j:HALO - j + TM, cols]
    return s * inv_cnt - u


def _inv_cnt(i, w):
    row = i * TM + lax.broadcasted_iota(jnp.int32, (TM, 1), 0)
    return 1.0 / jnp.minimum(row + 1, w).astype(F32)


def b_mix_fwd(h, gain, wg, scale, ride=None):
    t = h.shape[0]

    def body(h_ref, hp_ref, g_ref, wg_ref, sc_ref, hn_ref, pad):
        i = pl.program_id(0)
        gain = g_ref[...]
        pad[0:HALO, :] = jnp.where(i == 0, 0.0, _rms_fwd(hp_ref[...], gain))
        pad[HALO:, :] = _rms_fwd(h_ref[...], gain)
        for g, w in enumerate(POOL_WINDOWS):
            cols = slice(g * GW, (g + 1) * GW)
            mixed = _pool_mixed(pad, g, w, _inv_cnt(i, w))
            y = _dot(mixed.astype(BF16), wg_ref[g])
            hn_ref[:, cols] = h_ref[:, cols] + y * sc_ref[:, cols]

    return _call(
        body, name="b_mix_fwd", grid=(t // TM,), ride=ride,
        in_specs=[pl.BlockSpec((TM, D), lambda i: (i, 0)),
                  pl.BlockSpec((HALO, D), lambda i: (_prev_halo(i), 0)),
                  pl.BlockSpec((1, D), lambda i: (0, 0)),
                  pl.BlockSpec((4, GW, GW), lambda i: (0, 0, 0)),
                  pl.BlockSpec((1, D), lambda i: (0, 0))],
        out_specs=[pl.BlockSpec((TM, D), lambda i: (i, 0))],
        out_shape=[jax.ShapeDtypeStruct((t, D), F32)],
        scratch_shapes=[pltpu.VMEM((HALO + TM, D), F32)],
        args=(h, h, gain, wg, scale))


def b_mix_bwd(dh, h, gain, wg, scale, ride=None):
    t = h.shape[0]
    steps = t // TM

    def body(dh_ref, dhn_ref, h_ref, hp_ref, g_ref, wg_ref, sc_ref, dhp_ref, dgain_ref, dwg_ref, dsc_ref, pad, dpad, du):
        i = pl.program_id(0)
        last = i == steps - 1

        @pl.when(i == 0)
        def _():
            for r in (dgain_ref, dwg_ref, dsc_ref):
                r[...] = jnp.zeros_like(r)

        gain = g_ref[...]
        pad[0:HALO, :] = jnp.where(i == 0, 0.0, _rms_fwd(hp_ref[...], gain))
        pad[HALO:, :] = _rms_fwd(h_ref[...], gain)
        for g, w in enumerate(POOL_WINDOWS):
            cols = slice(g * GW, (g + 1) * GW)
            inv_cnt = _inv_cnt(i, w)
            mixed = _pool_mixed(pad, g, w, inv_cnt).astype(BF16)
            dh = dh_ref[:, cols]
            dsc_ref[:, cols] += jnp.sum(dh * _dot(mixed, wg_ref[g]), axis=0, keepdims=True)
            dy = (dh * sc_ref[:, cols]).astype(BF16)
            dwg_ref[g] += _dot_tn(mixed, dy)
            dm = _dot_nt(dy, wg_ref[g])
            dmn = _dot_nt((dhn_ref[:, cols] * sc_ref[:, cols]).astype(BF16), wg_ref[g])
            dpad[0:TM, cols] = dm * inv_cnt
            dpad[TM:, cols] = jnp.where(last, 0.0, dmn * (1.0 / w))
            s = dpad[0:TM, cols]
            for j in range(1, w):
                s = s + dpad[j:j + TM, cols]
            du[:, cols] = s - dm
        dx, dgain = _rms_bwd(h_ref[...], gain, du[...])
        dhp_ref[...] = dh_ref[...] + dx
        dgain_ref[...] += dgain

    return _call(
        body, name="b_mix_bwd", grid=(steps,), ride=ride,
        in_specs=[pl.BlockSpec((TM, D), lambda i: (i, 0)),
                  pl.BlockSpec((HALO, D), lambda i: (_next_halo(i, t), 0)),
                  pl.BlockSpec((TM, D), lambda i: (i, 0)),
                  pl.BlockSpec((HALO, D), lambda i: (_prev_halo(i), 0)),
                  pl.BlockSpec((1, D), lambda i: (0, 0)),
                  pl.BlockSpec((4, GW, GW), lambda i: (0, 0, 0)),
                  pl.BlockSpec((1, D), lambda i: (0, 0))],
        out_specs=[pl.BlockSpec((TM, D), lambda i: (i, 0)),
                   pl.BlockSpec((1, D), lambda i: (0, 0)),
                   pl.BlockSpec((4, GW, GW), lambda i: (0, 0, 0)),
                   pl.BlockSpec((1, D), lambda i: (0, 0))],
        out_shape=[jax.ShapeDtypeStruct((t, D), F32),
                   jax.ShapeDtypeStruct((1, D), F32),
                   jax.ShapeDtypeStruct((4, GW, GW), F32),
                   jax.ShapeDtypeStruct((1, D), F32)],
        scratch_shapes=[pltpu.VMEM((HALO + TM, D), F32), pltpu.VMEM((TM + HALO, D), F32), pltpu.VMEM((TM, D), F32)],
        args=(dh, dh, h, h, gain, wg, scale))


LOSS_LANES = 128


def loss_head(h, gain, target):
    t = h.shape[0]

    def body(h_ref, g_ref, tg_ref, loss_ref, dh_ref, dgain_ref):
        @pl.when(pl.program_id(0) == 0)
        def _():
            loss_ref[...] = jnp.zeros_like(loss_ref)
            dgain_ref[...] = jnp.zeros_like(dgain_ref)

        x, gain = h_ref[...], g_ref[...]
        err = _rms_fwd(x, gain) - tg_ref[...]
        per_row = jnp.mean(err * err, axis=-1, keepdims=True)
        loss_ref[...] += jnp.broadcast_to(0.5 * jnp.sum(per_row, axis=0, keepdims=True), (1, LOSS_LANES))
        dx, dgain = _rms_bwd(x, gain, err * (1.0 / D))
        dh_ref[...] = dx
        dgain_ref[...] += dgain

    outs, _ = _call(
        body, name="loss_head", grid=(t // TM,),
        in_specs=[pl.BlockSpec((TM, D), lambda i: (i, 0)),
                  pl.BlockSpec((1, D), lambda i: (0, 0)),
                  pl.BlockSpec((TM, D), lambda i: (i, 0))],
        out_specs=[pl.BlockSpec((1, LOSS_LANES), lambda i: (0, 0)),
                   pl.BlockSpec((TM, D), lambda i: (i, 0)),
                   pl.BlockSpec((1, D), lambda i: (0, 0))],
        out_shape=[jax.ShapeDtypeStruct((1, LOSS_LANES), F32),
                   jax.ShapeDtypeStruct((t, D), F32),
                   jax.ShapeDtypeStruct((1, D), F32)],
        args=(h, gain, target))
    return outs


ADAM_LR = 0.001
ADAM_B1 = 0.9
ADAM_B2 = 0.999
ADAM_EPS = 1e-08
ADAM_WD = 0.01
ADAM_STEP = 10
ADAM_VMEM = 40 * 1024 * 1024


def cast_all(arrays):
    def body(*refs):
        for src, dst in zip(refs[:len(arrays)], refs[len(arrays):]):
            dst[...] = src[...].astype(BF16)

    return pl.pallas_call(
        body, name="cast_all", out_shape=[jax.ShapeDtypeStruct(a.shape, BF16) for a in arrays],
        compiler_params=pltpu.CompilerParams(vmem_limit_bytes=VMEM_LIMIT),
    )(*arrays)


def _adam_math(w, m, v, g):
    m = ADAM_B1 * m + (1.0 - ADAM_B1) * g
    v = ADAM_B2 * v + (1.0 - ADAM_B2) * (g * g)
    m_hat = m / (1.0 - ADAM_B1 ** ADAM_STEP)
    v_hat = v / (1.0 - ADAM_B2 ** ADAM_STEP)
    return -ADAM_LR * (m_hat / (jnp.sqrt(v_hat) + ADAM_EPS) + ADAM_WD * w), m, v


def adamw(ws, ms, vs, gps, rb, tokens=()):
    n = len(ws)
    r, c = ws[0].shape
    nb = r // rb

    def body(*refs):
        i = pl.program_id(0)
        outs = refs[4 * n + len(tokens):]
        for j in range(n):
            w_ref, m_ref, v_ref, gp_ref = (refs[q * n + j] for q in range(4))
            g_ref, d_ref, nm_ref, nv_ref = (outs[q * n + j] for q in range(4))

            @pl.when(i // nb == j)
            def _():
                g = gp_ref[0].astype(F32)
                for s in range(1, N_DEV):
                    g = g + gp_ref[s].astype(F32)
                g_ref[...] = g
                d_ref[...], nm_ref[...], nv_ref[...] = _adam_math(w_ref[...], m_ref[...], v_ref[...], g)

    def blk(j):
        return pl.BlockSpec((rb, c), lambda i: (jnp.clip(i - j * nb, 0, nb - 1), 0))

    def gblk(j):
        return pl.BlockSpec((N_DEV, rb, c), lambda i: (0, jnp.clip(i - j * nb, 0, nb - 1), 0))

    outs, _ = _call(
        body, name=f"adamw_{n}x{r}x{c}", grid=(n * nb,),
        in_specs=[blk(j) for _ in range(3) for j in range(n)] + [gblk(j) for j in range(n)] + [ANY] * len(tokens),
        out_specs=[blk(j) for _ in range(4) for j in range(n)],
        out_shape=[jax.ShapeDtypeStruct((r, c), F32)] * (4 * n),
        args=(*ws, *ms, *vs, *gps, *tokens))
    return outs[:n], outs[n:2 * n], outs[2 * n:3 * n], outs[3 * n:]


def adamw_vectors(ws, ms, vs, gparts):
    nv = len(ws)

    def body(*refs):
        w_refs, m_refs, v_refs = refs[:nv], refs[nv:2 * nv], refs[2 * nv:3 * nv]
        gp_ref = refs[3 * nv]
        outs = refs[3 * nv + 1:]
        g_refs, d_refs, nm_refs, nv_refs = outs[:nv], outs[nv:2 * nv], outs[2 * nv:3 * nv], outs[3 * nv:]
        row = 0
        for i in range(nv):
            for part in range(w_refs[i].shape[1] // D):
                cols = slice(part * D, (part + 1) * D)
                g = gp_ref[0, row:row + 1, :]
                for s in range(1, N_DEV):
                    g = g + gp_ref[s, row:row + 1, :]
                g_refs[i][:, cols] = g
                d_refs[i][:, cols], nm_refs[i][:, cols], nv_refs[i][:, cols] = _adam_math(
                    w_refs[i][:, cols], m_refs[i][:, cols], v_refs[i][:, cols], g)
                row += 1

    shapes = [jax.ShapeDtypeStruct(w.shape, F32) for w in ws]
    outs = pl.pallas_call(body, name="adamw_vectors", out_shape=shapes * 4)(*ws, *ms, *vs, gparts)
    return outs[:nv], outs[nv:2 * nv], outs[2 * nv:3 * nv], outs[3 * nv:]


WEIGHTS = ["ln1_0", "a0_w_in", "a0_conv", "a0_w_out", "ln2_0", "ffn0_w_gu", "ffn0_w_down",
           "ln1_1", "b1_w_grp", "b1_scale", "ln2_1", "ffn1_w_gu", "ffn1_w_down",
           "ln1_2", "c2_w_pw1", "c2_b_pw1", "c2_dw", "c2_b_dw", "c2_ln_g", "c2_ln_b", "c2_w_pw2", "c2_b_pw2",
           "ln2_2", "ffn2_w_gu", "ffn2_w_down",
           "ln1_3", "a3_w_in", "a3_conv", "a3_w_out", "ln2_3", "ffn3_w_gu", "ffn3_w_down", "ln_f"]
SHARDED = {"a0_w_in": ("cols", 256), "a0_conv": ("cols", A_TAPS), "a0_w_out": ("rows", 128),
           "ffn0_w_gu": ("lead", 176), "ffn0_w_down": ("rows", 176),
           "b1_w_grp": ("mid", 128),
           "ffn1_w_gu": ("lead", 176), "ffn1_w_down": ("rows", 176),
           "c2_w_pw1": ("cols", 256), "c2_dw": ("cols", C_TAPS), "c2_w_pw2": ("rows", 128),
           "ffn2_w_gu": ("lead", 176), "ffn2_w_down": ("rows", 176),
           "a3_w_in": ("cols", 256), "a3_conv": ("cols", A_TAPS), "a3_w_out": ("rows", 128),
           "ffn3_w_gu": ("lead", 176), "ffn3_w_down": ("rows", 176)}
IN_PROJ = ("a0_w_in", "c2_w_pw1", "a3_w_in")
REPL = [n for n in WEIGHTS if n not in SHARDED]
REPL_ROWS = 16
GATHER_PLAN = {"first": ["a0_w_in", "a0_w_out", "a0_conv"],
               "in0": ["ffn0_w_gu"], "mix0": ["ffn0_w_down"],
               "ffn0": ["b1_w_grp", "ffn1_w_gu", "ffn1_w_down"],
               "ffn1": ["c2_w_pw1", "c2_w_pw2", "c2_dw", "ffn2_w_gu"],
               "in2": ["ffn2_w_down"],
               "mix2": ["a3_w_in", "a3_w_out", "a3_conv"],
               "ffn2": ["ffn3_w_gu", "ffn3_w_down"]}
SCATTER_PLAN = {"mixb3": ["ffn3_w_down"], "inw3": ["a3_w_out", "a3_conv"],
                "ffnx2": ["ffn3_w_gu"], "ffnw2": ["a3_w_in"],
                "mixb2": ["ffn2_w_gu", "ffn2_w_down"], "inw2": ["c2_w_pw2", "c2_dw"],
                "ffnx1": ["c2_w_pw1"],
                "ffnx0": ["ffn1_w_gu"], "ffnw0": ["ffn1_w_down", "b1_w_grp"],
                "last": ["repl"]}
LATE_FFN = ["ffn0_w_gu", "ffn0_w_down"]
LATE_MIX = ["a0_w_in", "a0_w_out", "a0_conv"]


def _step(p):
    vec = lambda n: p[n].reshape(1, -1)
    x, target = p["x"][0], p["loss_target"][0]

    names = list(SHARDED)
    stored = lambda n, a: a.T if n.endswith("w_gu") else a
    shard = dict(zip(names, cast_all([stored(n, p[n]) for n in names])))
    full = {}

    def gather(slot):
        names = GATHER_PLAN[slot]
        return gather_ride([shard[n] for n in names], ["cols" if n in IN_PROJ else "lead" for n in names])

    def landed(slot, outs):
        full.update(zip(GATHER_PLAN[slot], outs))

    def conv_full(n):
        k = full[n].shape[1]
        return full[n].transpose(1, 0, 2).reshape(k, D).astype(F32)

    def wgu(i):
        return full[f"ffn{i}_w_gu"].reshape(2, 4, FC, D)

    def wd(i):
        return full[f"ffn{i}_w_down"].reshape(4, FC, D)

    landed("first", run_ride(gather("first"), "gather_first"))
    no_bias = jnp.zeros((1, 3 * D), F32)
    h = [x]
    saved = {}
    conv, wout = {}, {}

    (z, u), got = rms_matmul(h[-1], vec("ln1_0"), full["a0_w_in"], no_bias, ride=gather("in0"))
    landed("in0", got)
    conv[0], wout[0] = conv_full("a0_conv"), full["a0_w_out"].reshape(D, D)
    (hm,), got = a_mix_fwd(z, h[-1], conv[0], wout[0], ride=gather("mix0"))
    landed("mix0", got)
    saved["mix0"] = (z, u)
    h.append(hm)
    (hn, zf, uf), got = ffn_fwd(hm, vec("ln2_0"), wgu(0), wd(0), ride=gather("ffn0"))
    landed("ffn0", got)
    saved["ffn0"] = (zf, uf)
    h.append(hn)

    wgrp = full["b1_w_grp"].transpose(1, 0, 2, 3).reshape(4, GW, GW)
    (hm,), _ = b_mix_fwd(h[-1], vec("ln1_1"), wgrp, vec("b1_scale"))
    h.append(hm)
    (hn, zf, uf), got = ffn_fwd(hm, vec("ln2_1"), wgu(1), wd(1), ride=gather("ffn1"))
    landed("ffn1", got)
    saved["ffn1"] = (zf, uf)
    h.append(hn)

    (z, u), got = rms_matmul(h[-1], vec("ln1_2"), full["c2_w_pw1"], vec("c2_b_pw1"), ride=gather("in2"))
    landed("in2", got)
    cdw, wpw2 = conv_full("c2_dw"), full["c2_w_pw2"].reshape(D, D)
    (hm, h2), got = c_mix_fwd(z, h[-1], cdw, vec("c2_b_dw"), vec("c2_ln_g"), vec("c2_ln_b"), wpw2, vec("c2_b_pw2"),
                              ride=gather("mix2"))
    landed("mix2", got)
    saved["mix2"] = (z, u, h2)
    h.append(hm)
    (hn, zf, uf), got = ffn_fwd(hm, vec("ln2_2"), wgu(2), wd(2), ride=gather("ffn2"))
    landed("ffn2", got)
    saved["ffn2"] = (zf, uf)
    h.append(hn)

    (z, u), _ = rms_matmul(h[-1], vec("ln1_3"), full["a3_w_in"], no_bias)
    conv[3], wout[3] = conv_full("a3_conv"), full["a3_w_out"].reshape(D, D)
    (hm,), _ = a_mix_fwd(z, h[-1], conv[3], wout[3])
    saved["mix3"] = (z, u)
    h.append(hm)
    (hn, zf, uf), _ = ffn_fwd(hm, vec("ln2_3"), wgu(3), wd(3))
    saved["ffn3"] = (zf, uf)
    h.append(hn)

    loss_lanes, dh, g_lnf = loss_head(h[-1], vec("ln_f"), target)

    g = {"ln_f": g_lnf}
    recv = {}

    def repl_rows():
        loss_row = jnp.pad(loss_lanes, ((0, 0), (0, D - LOSS_LANES)))
        return jnp.concatenate([g[n].reshape(-1, D) for n in REPL] + [loss_row], axis=0)

    def scatter(slot):
        parts = []
        for n in SCATTER_PLAN.get(slot, []):
            parts.append((repl_rows(), "all") if n == "repl" else (g[n], SHARDED[n][0]))
        return scatter_ride(parts) if parts else None

    def arrived(slot, outs):
        recv.update(zip(SCATTER_PLAN.get(slot, []), outs))

    for i in (3, 2, 1, 0):
        zf, uf = saved[f"ffn{i}"]
        (dh_prev, dzf, a, g[f"ln2_{i}"], dhb), got = ffn_bwd_x(dh, h[2 * i + 1], vec(f"ln2_{i}"), zf, wgu(i), wd(i),
                                                             ride=scatter(f"ffnx{i}"))
        arrived(f"ffnx{i}", got)
        dh = dh_prev
        (dwgu, dwd), got = ffn_bwd_w(uf, dzf, a, dhb, ride=scatter(f"ffnw{i}"))
        arrived(f"ffnw{i}", got)
        g[f"ffn{i}_w_gu"], g[f"ffn{i}_w_down"] = dwgu.reshape(N_DEV, FC, D), dwd.reshape(FF, D)
        if i == 0:
            late_ffn = scatter_start([(g[n], SHARDED[n][0]) for n in LATE_FFN], "late_ffn_start")
        hin = h[2 * i]
        if i in (0, 3):
            z, u = saved[f"mix{i}"]
            (dz, g[f"a{i}_w_out"], g[f"a{i}_conv"]), got = a_mix_bwd(dh, z, conv[i], wout[i], ride=scatter(f"mixb{i}"))
            arrived(f"mixb{i}", got)
            (g[f"a{i}_w_in"],), got = in_proj_bwd_w(u, dz, ride=scatter(f"inw{i}"))
            arrived(f"inw{i}", got)
            if i == 0:
                late_mix = scatter_start([(g[n], SHARDED[n][0]) for n in LATE_MIX], "late_mix_start")
            (dh, g[f"ln1_{i}"], _), got = in_proj_bwd_x(dz, full[f"a{i}_w_in"], hin, vec(f"ln1_{i}"), dh,
                                                       ride=scatter(f"inx{i}"))
            arrived(f"inx{i}", got)
        elif i == 1:
            (dh, g["ln1_1"], g["b1_w_grp"], g["b1_scale"]), got = b_mix_bwd(dh, hin, vec("ln1_1"), wgrp, vec("b1_scale"),
                                                                             ride=scatter("mixb1"))
            arrived("mixb1", got)
        else:
            z, u, h2 = saved["mix2"]
            (dz, g["c2_w_pw2"], g["c2_b_pw2"], g["c2_ln_g"], g["c2_ln_b"], g["c2_b_dw"], g["c2_dw"]), got = c_mix_bwd(
                dh, z, h2, cdw, vec("c2_ln_g"), vec("c2_ln_b"), wpw2, ride=scatter("mixb2"))
            arrived("mixb2", got)
            (g["c2_w_pw1"],), got = in_proj_bwd_w(u, dz, ride=scatter("inw2"))
            arrived("inw2", got)
            (dh, g["ln1_2"], g["c2_b_pw1"]), got = in_proj_bwd_x(dz, full["c2_w_pw1"], hin, vec("ln1_2"), dh,
                                                                ride=scatter("inx2"))
            arrived("inx2", got)
    grad_x = dh[None]
    arrived("last", run_ride(scatter("last"), "scatter_last"))

    grad, delta, new_m, new_v = {}, {}, {}, {}
    two_d = lambda n, a: stored(n, a.reshape(-1, p[n].shape[-1]))

    def adam_calls(names_, tokens):
        groups, last = {}, None
        for n in names_:
            groups.setdefault((two_d(n, p[n]).shape, SHARDED[n][1]), []).append(n)
        for (shape, rb), members in groups.items():
            per_weight = 2 * rb * shape[1] * (7 * 4 + N_DEV * recv[members[0]].dtype.itemsize)
            at_once = max(1, (ADAM_VMEM // per_weight))
            for lo in range(0, len(members), at_once):
                ns = members[lo:lo + at_once]
                outs = adamw([two_d(n, p[n]) for n in ns], [two_d(n, p["m_" + n]) for n in ns],
                             [two_d(n, p["v_" + n]) for n in ns], [recv[n].reshape(N_DEV, *shape) for n in ns], rb, tokens)
                for res, o in zip((grad, delta, new_m, new_v), outs):
                    res.update({n: stored(n, a).reshape(p[n].shape) for n, a in zip(ns, o)})
                last = outs[0][0]
        return last

    early_done = adam_calls([n for n in SHARDED if n not in LATE_FFN + LATE_MIX], (late_ffn[-1], late_mix[-1]))
    outs = adamw_vectors([vec(n) for n in REPL], [vec("m_" + n) for n in REPL], [vec("v_" + n) for n in REPL], recv["repl"])
    for res, o in zip((grad, delta, new_m, new_v), outs):
        res.update({n: a.reshape(p[n].shape) for n, a in zip(REPL, o)})
    recv.update(zip(LATE_FFN, scatter_wait(late_ffn, [early_done], "late_ffn_wait")))
    recv.update(zip(LATE_MIX, scatter_wait(late_mix, [early_done], "late_mix_wait")))
    adam_calls(LATE_FFN + LATE_MIX, ())

    loss = jnp.sum(recv["repl"][:, REPL_ROWS, 0])
    return (loss, grad_x, *[grad[n] for n in WEIGHTS], *[delta[n] for n in WEIGHTS],
            *[new_m[n] for n in WEIGHTS], *[new_v[n] for n in WEIGHTS])


def kernel(x, ln1_0, a0_w_in, a0_conv, a0_w_out, ln2_0, ffn0_w_gu, ffn0_w_down, ln1_1, b1_w_grp, b1_scale, ln2_1, ffn1_w_gu, ffn1_w_down, ln1_2, c2_w_pw1, c2_b_pw1, c2_dw, c2_b_dw, c2_ln_g, c2_ln_b, c2_w_pw2, c2_b_pw2, ln2_2, ffn2_w_gu, ffn2_w_down, ln1_3, a3_w_in, a3_conv, a3_w_out, ln2_3, ffn3_w_gu, ffn3_w_down, ln_f, loss_target, m_ln1_0, m_a0_w_in, m_a0_conv, m_a0_w_out, m_ln2_0, m_ffn0_w_gu, m_ffn0_w_down, m_ln1_1, m_b1_w_grp, m_b1_scale, m_ln2_1, m_ffn1_w_gu, m_ffn1_w_down, m_ln1_2, m_c2_w_pw1, m_c2_b_pw1, m_c2_dw, m_c2_b_dw, m_c2_ln_g, m_c2_ln_b, m_c2_w_pw2, m_c2_b_pw2, m_ln2_2, m_ffn2_w_gu, m_ffn2_w_down, m_ln1_3, m_a3_w_in, m_a3_conv, m_a3_w_out, m_ln2_3, m_ffn3_w_gu, m_ffn3_w_down, m_ln_f, v_ln1_0, v_a0_w_in, v_a0_conv, v_a0_w_out, v_ln2_0, v_ffn0_w_gu, v_ffn0_w_down, v_ln1_1, v_b1_w_grp, v_b1_scale, v_ln2_1, v_ffn1_w_gu, v_ffn1_w_down, v_ln1_2, v_c2_w_pw1, v_c2_b_pw1, v_c2_dw, v_c2_b_dw, v_c2_ln_g, v_c2_ln_b, v_c2_w_pw2, v_c2_b_pw2, v_ln2_2, v_ffn2_w_gu, v_ffn2_w_down, v_ln1_3, v_a3_w_in, v_a3_conv, v_a3_w_out, v_ln2_3, v_ffn3_w_gu, v_ffn3_w_down, v_ln_f):
    return _step(dict(locals()))
```

```python
import jax
import jax.numpy as jnp
from jax import lax
from jax.experimental import pallas as pl
from jax.experimental.pallas import tpu as pltpu

F32 = jnp.float32
BF16 = jnp.bfloat16

N_DEV = 8
D = 1024
FF = 2816
FC = FF // 4
RMS_EPS = 1e-6
LN_EPS = 1e-5
TM = 512
HALO = 32
VMEM_LIMIT = 60 * 1024 * 1024

NT = (((1,), (1,)), ((), ()))
TN = (((0,), (0,)), ((), ()))
MESH = pl.DeviceIdType.MESH
ANY = pl.BlockSpec(memory_space=pl.ANY)
N_PEERS = N_DEV - 1


def _dot(a, b):
    return jnp.dot(a, b, preferred_element_type=F32)


def _dot_nt(a, b):
    return lax.dot_general(a, b, NT, preferred_element_type=F32)


def _dot_tn(a, b):
    return lax.dot_general(a, b, TN, preferred_element_type=F32)


def _rms_fwd(x, gain):
    r = lax.rsqrt(jnp.mean(x * x, axis=-1, keepdims=True) + RMS_EPS)
    return x * r * gain


def _rms_bwd(x, gain, du):
    r = lax.rsqrt(jnp.mean(x * x, axis=-1, keepdims=True) + RMS_EPS)
    xhat = x * r
    dgain = jnp.sum(du * xhat, axis=0, keepdims=True)
    dxhat = du * gain
    dx = r * (dxhat - xhat * jnp.mean(dxhat * xhat, axis=-1, keepdims=True))
    return dx, dgain


def _dev_index(p):
    return 4 * p[0] + 2 * p[1] + p[2]


def _place():
    return lax.axis_index("x"), lax.axis_index("y"), lax.axis_index("c")


class Ride:
    def __init__(self, ins, out_shapes, start, finish):
        self.ins, self.out_shapes, self.start, self.finish = list(ins), list(out_shapes), start, finish
        n = len(self.ins)
        self.sems = [pltpu.SemaphoreType.DMA((n * N_PEERS,)), pltpu.SemaphoreType.DMA((n * N_PEERS,)),
                     pltpu.SemaphoreType.DMA((n,))]


def gather_ride(shards, kinds):
    n = len(shards)

    def setup(ins, outs, sems):
        send_sems, recv_sems, local_sems = sems
        x, y, c = _place()
        chips = [(1 - x, y), (x, 1 - y), (1 - x, 1 - y)]

        def copy(a, k, block, to, src=None):
            slot = _chunk(outs[a], kinds[a], _dev_index(block))
            return pltpu.make_async_remote_copy(
                src_ref=slot if src is None else src, dst_ref=slot,
                send_sem=send_sems.at[a * N_PEERS + k], recv_sem=recv_sems.at[a * N_PEERS + k],
                device_id=to, device_id_type=MESH)

        def mine(a):
            return pltpu.make_async_copy(ins[a], _chunk(outs[a], kinds[a], _dev_index((x, y, c))), local_sems.at[a])

        def first(a):
            return [copy(a, 0, (x, y, c), (x, y, 1 - c), src=ins[a])] + [
                copy(a, 1 + j, (x, y, c), (*chip, c), src=ins[a]) for j, chip in enumerate(chips)]

        return (x, y, c), chips, copy, mine, first

    def start(ins, outs, sems):
        _, _, _, mine, first = setup(ins, outs, sems)
        for a in range(n):
            mine(a).start()
            for cp in first(a):
                cp.start()

    def finish(ins, outs, sems):
        (x, y, c), chips, copy, mine, first = setup(ins, outs, sems)
        me, sibling = (x, y, c), (x, y, 1 - c)
        for a in range(n):
            for j, chip in enumerate(chips):
                copy(a, 1 + j, (*chip, c), me).wait_recv()
                copy(a, 4 + j, (*chip, c), sibling).start()
        for a in range(n):
            copy(a, 0, sibling, me).wait_recv()
            for j, chip in enumerate(chips):
                copy(a, 4 + j, (*chip, 1 - c), me).wait_recv()
        for a in range(n):
            for cp in first(a):
                cp.wait_send()
            for j, chip in enumerate(chips):
                copy(a, 4 + j, (*chip, c), sibling).wait_send()
        for a in range(n):
            mine(a).wait()

    shapes = [(N_DEV, *s.shape) if kind == "lead" else (s.shape[0], N_DEV * s.shape[1]) for s, kind in zip(shards, kinds)]
    return Ride(shards, [jax.ShapeDtypeStruct(shape, s.dtype) for shape, s in zip(shapes, shards)], start, finish)


def _chunk(ref, kind, j):
    if kind == "lead":
        return ref.at[j]
    if kind == "rows":
        r = ref.shape[0] // N_DEV
        return ref.at[pl.ds(j * r, r)]
    if kind == "mid":
        r = ref.shape[1] // N_DEV
        return ref.at[:, pl.ds(j * r, r), :]
    if kind == "cols":
        c = ref.shape[1] // N_DEV
        return ref.at[:, pl.ds(j * c, c)]
    return ref


def _chunk_shape(shape, kind):
    if kind == "lead":
        return tuple(shape[1:])
    if kind == "rows":
        return (shape[0] // N_DEV, *shape[1:])
    if kind == "mid":
        return (shape[0], shape[1] // N_DEV, shape[2])
    if kind == "cols":
        return (shape[0], shape[1] // N_DEV)
    return tuple(shape)


def scatter_ride(parts):
    n = len(parts)
    kinds = [k for _, k in parts]

    def setup(ins, outs, sems):
        send_sems, recv_sems, local_sems = sems
        x, y, c = _place()
        me = _dev_index((x, y, c))
        peers = []
        for k in range(1, N_DEV):
            kx, ky, kc = (k >> 2) & 1, (k >> 1) & 1, k & 1
            peers.append((1 - x if kx else x, 1 - y if ky else y, 1 - c if kc else c))

        def copy(a, k, peer):
            return pltpu.make_async_remote_copy(
                src_ref=_chunk(ins[a], kinds[a], _dev_index(peer)), dst_ref=outs[a].at[me],
                send_sem=send_sems.at[a * N_PEERS + k], recv_sem=recv_sems.at[a * N_PEERS + k],
                device_id=peer, device_id_type=MESH)

        def arrival(a, k, peer):
            slot = outs[a].at[_dev_index(peer)]
            return pltpu.make_async_remote_copy(
                src_ref=slot, dst_ref=slot,
                send_sem=send_sems.at[a * N_PEERS + k], recv_sem=recv_sems.at[a * N_PEERS + k],
                device_id=peer, device_id_type=MESH)

        def mine(a):
            return pltpu.make_async_copy(_chunk(ins[a], kinds[a], me), outs[a].at[me], local_sems.at[a])

        return peers, copy, arrival, mine

    def start(ins, outs, sems):
        peers, copy, _, mine = setup(ins, outs, sems)
        for a in range(n):
            mine(a).start()
            for k, peer in enumerate(peers):
                copy(a, k, peer).start()

    def finish(ins, outs, sems):
        peers, copy, arrival, mine = setup(ins, outs, sems)
        for a in range(n):
            for k, peer in enumerate(peers):
                arrival(a, k, peer).wait_recv()
        for a in range(n):
            for k, peer in enumerate(peers):
                copy(a, k, peer).wait_send()
            mine(a).wait()

    shapes = [jax.ShapeDtypeStruct((N_DEV, *_chunk_shape(arr.shape, kind)), arr.dtype) for arr, kind in parts]
    return Ride([arr for arr, _ in parts], shapes, start, finish)


HBM = pl.BlockSpec(memory_space=pltpu.HBM)
SEM = pl.BlockSpec(memory_space=pltpu.SEMAPHORE)
DATAFLOW = pltpu.SideEffectType.DATAFLOW_SIDE_EFFECTING
TOKEN = (8, 128)


def _scatter_copies(kinds, ins, lands, send_sems, recv_sems):
    x, y, c = _place()
    me = _dev_index((x, y, c))
    sends, arrivals = [], []
    for a, kind in enumerate(kinds):
        for k in range(1, N_DEV):
            kx, ky, kc = (k >> 2) & 1, (k >> 1) & 1, k & 1
            peer = (1 - x if kx else x, 1 - y if ky else y, 1 - c if kc else c)
            sem = a * N_PEERS + k - 1
            sends.append(pltpu.make_async_remote_copy(
                src_ref=_chunk(ins[a], kind, _dev_index(peer)), dst_ref=lands[a].at[me],
                send_sem=send_sems.at[sem], recv_sem=recv_sems.at[sem], device_id=peer, device_id_type=MESH))
            slot = lands[a].at[_dev_index(peer)]
            arrivals.append(pltpu.make_async_remote_copy(
                src_ref=slot, dst_ref=slot, send_sem=send_sems.at[sem], recv_sem=recv_sems.at[sem],
                device_id=peer, device_id_type=MESH))
    return me, sends, arrivals


def own_blocks(parts, name):
    n = len(parts)
    x, y, c = _place()
    me = jnp.reshape(_dev_index((x, y, c)), (1,)).astype(jnp.int32)

    def block_of(shape, kind):
        blk = _chunk_shape(shape, kind)
        if kind == "lead":
            return pl.BlockSpec((1, *blk), lambda i, me_ref: (me_ref[0], *[0] * len(blk)))
        if kind == "rows":
            return pl.BlockSpec(blk, lambda i, me_ref: (me_ref[0], *[0] * (len(blk) - 1)))
        if kind == "cols":
            return pl.BlockSpec(blk, lambda i, me_ref: (0, me_ref[0]))
        return pl.BlockSpec(blk, lambda i, me_ref: (0,) * len(blk))

    def body(me_ref, *refs):
        for (arr, kind), src, dst in zip(parts, refs[:n], refs[n:]):
            dst[0] = src[0] if kind == "lead" else src[...]

    shapes = [(N_DEV, *_chunk_shape(a.shape, k)) for a, k in parts]
    return pl.pallas_call(
        body, name=name,
        grid_spec=pltpu.PrefetchScalarGridSpec(
            num_scalar_prefetch=1, grid=(1,),
            in_specs=[block_of(a.shape, k) for a, k in parts],
            out_specs=[pl.BlockSpec((1, *s[1:]), lambda i, me_ref, r=len(s) - 1: (me_ref[0], *[0] * r)) for s in shapes]),
        out_shape=[jax.ShapeDtypeStruct(s, a.dtype) for s, (a, _) in zip(shapes, parts)],
        compiler_params=pltpu.CompilerParams(vmem_limit_bytes=VMEM_LIMIT),
    )(me, *[a for a, _ in parts])


def scatter_start(parts, name):
    n = len(parts)
    kinds = [k for _, k in parts]
    arrays = [pltpu.with_memory_space_constraint(a, pltpu.HBM) for a, _ in parts]
    zones = [pltpu.with_memory_space_constraint(z, pltpu.HBM) for z in own_blocks(parts, name + "_own")]

    def body(*refs):
        ins, lands = refs[:n], refs[n:2 * n]
        send_sems, recv_sems = refs[2 * n], refs[2 * n + 1]
        token = refs[4 * n + 2]
        _, sends, _ = _scatter_copies(kinds, ins, lands, send_sems, recv_sems)
        for cp in sends:
            cp.start()
        token[...] = jnp.zeros_like(token)

    outs = pl.pallas_call(
        body, name=name,
        out_shape=(pltpu.SemaphoreType.DMA((n * N_PEERS,)), pltpu.SemaphoreType.DMA((n * N_PEERS,)),
                   *[pltpu.HBM(a.shape, a.dtype) for a in arrays], *[pltpu.HBM(z.shape, z.dtype) for z in zones],
                   jax.ShapeDtypeStruct(TOKEN, F32)),
        in_specs=[HBM] * (2 * n),
        out_specs=(SEM, SEM, *[HBM] * (2 * n), pl.BlockSpec(memory_space=pltpu.VMEM)),
        input_output_aliases={i: 2 + i for i in range(2 * n)},
        compiler_params=pltpu.CompilerParams(has_side_effects=DATAFLOW),
    )(*arrays, *zones)
    return kinds, outs[0], outs[1], outs[2:2 + n], outs[2 + n:2 + 2 * n], outs[2 + 2 * n]


def scatter_wait(started, after, name):
    kinds, send_sems, recv_sems, arrays, zones, _ = started
    n = len(kinds)

    def body(*refs):
        ins, lands = refs[:n], refs[n:2 * n]
        _, sends, arrivals = _scatter_copies(kinds, ins, lands, refs[2 * n], refs[2 * n + 1])
        for cp in sends:
            cp.wait_send()
        for cp in arrivals:
            cp.wait_recv()

    outs = pl.pallas_call(
        body, name=name,
        out_shape=(*[pltpu.HBM(a.shape, a.dtype) for a in arrays], *[pltpu.HBM(z.shape, z.dtype) for z in zones]),
        in_specs=[HBM] * (2 * n) + [SEM, SEM] + [ANY] * len(after),
        out_specs=[HBM] * (2 * n),
        input_output_aliases={i: i for i in range(2 * n)},
        compiler_params=pltpu.CompilerParams(has_side_effects=DATAFLOW),
    )(*arrays, *zones, send_sems, recv_sems, *after)
    return outs[n:]


def run_ride(ride, name):
    n_in, n_out = len(ride.ins), len(ride.out_shapes)

    def body(*refs):
        ins, outs, sems = refs[:n_in], refs[n_in:n_in + n_out], refs[n_in + n_out:]
        ride.start(ins, outs, sems)
        ride.finish(ins, outs, sems)

    return pl.pallas_call(
        body, name=name, in_specs=[ANY] * n_in, out_specs=[ANY] * n_out, out_shape=ride.out_shapes,
        scratch_shapes=ride.sems,
    )(*ride.ins)


def _call(body, *, name, grid, in_specs, out_specs, out_shape, args, scratch_shapes=(), ride=None):
    params = pltpu.CompilerParams(dimension_semantics=("arbitrary",) * len(grid), vmem_limit_bytes=VMEM_LIMIT)
    if ride is None:
        outs = pl.pallas_call(body, name=name, grid=grid, in_specs=in_specs, out_specs=out_specs, out_shape=out_shape,
                              scratch_shapes=list(scratch_shapes), compiler_params=params)(*args)
        return outs, []
    n_in, n_out, n_scr = len(in_specs), len(out_specs), len(scratch_shapes)
    r_in, r_out = len(ride.ins), len(ride.out_shapes)

    def hosted(*refs):
        ins, refs = refs[:n_in], refs[n_in:]
        rins, refs = refs[:r_in], refs[r_in:]
        outs, refs = refs[:n_out], refs[n_out:]
        routs, refs = refs[:r_out], refs[r_out:]
        scratch, sems = refs[:n_scr], refs[n_scr:]
        step, n_steps = pl.program_id(0), grid[0]
        for d in range(1, len(grid)):
            step, n_steps = step * grid[d] + pl.program_id(d), n_steps * grid[d]

        @pl.when(step == 0)
        def _():
            ride.start(rins, routs, sems)

        body(*ins, *outs, *scratch)

        @pl.when(step == n_steps - 1)
        def _():
            ride.finish(rins, routs, sems)

    outs = pl.pallas_call(
        hosted, name=name + "_ride", grid=grid,
        in_specs=list(in_specs) + [ANY] * r_in, out_specs=list(out_specs) + [ANY] * r_out,
        out_shape=list(out_shape) + ride.out_shapes,
        scratch_shapes=list(scratch_shapes) + ride.sems, compiler_params=params,
    )(*args, *ride.ins)
    return outs[:n_out], outs[n_out:]


def ffn_fwd(h, gain, wgu, wd, ride=None):
    t = h.shape[0]
    tf = min(TF, t)

    def body(h_ref, g_ref, wgu_ref, wd_ref, hn_ref, z_ref, u_ref, acc):
        k = pl.program_id(1)

        @pl.when(k == 0)
        def _():
            u_ref[...] = _rms_fwd(h_ref[...], g_ref[...]).astype(BF16)
            acc[...] = jnp.zeros_like(acc)

        u = u_ref[...]
        g = _dot_nt(u, wgu_ref[0, 0])
        up = _dot_nt(u, wgu_ref[1, 0])
        z_ref[0, 0] = g.astype(BF16)
        z_ref[1, 0] = up.astype(BF16)
        a = g * jax.nn.sigmoid(g) * up
        acc[...] += _dot(a.astype(BF16), wd_ref[0])

        @pl.when(k == 3)
        def _():
            hn_ref[...] = h_ref[...] + acc[...]

    return _call(
        body, name="ffn_fwd", grid=(t // tf, 4), ride=ride,
        in_specs=[pl.BlockSpec((tf, D), lambda i, k: (i, 0)),
                  pl.BlockSpec((1, D), lambda i, k: (0, 0)),
                  pl.BlockSpec((2, 1, FC, D), lambda i, k: (0, k, 0, 0)),
                  pl.BlockSpec((1, FC, D), lambda i, k: (k, 0, 0))],
        out_specs=[pl.BlockSpec((tf, D), lambda i, k: (i, 0)),
                   pl.BlockSpec((2, 1, tf, FC), lambda i, k: (0, k, i, 0)),
                   pl.BlockSpec((tf, D), lambda i, k: (i, 0))],
        out_shape=[jax.ShapeDtypeStruct((t, D), F32),
                   jax.ShapeDtypeStruct((2, 4, t, FC), BF16),
                   jax.ShapeDtypeStruct((t, D), BF16)],
        scratch_shapes=[pltpu.VMEM((tf, D), F32)],
        args=(h, gain, wgu, wd))


def ffn_bwd_x(dh, h, gain, z, wgu, wd, ride=None):
    t = h.shape[0]

    def body(dh_ref, h_ref, g_ref, z_ref, wgu_ref, wd_ref, dhp_ref, dz_ref, a_ref, dgain_ref, dhb, du):
        i, k = pl.program_id(0), pl.program_id(1)

        @pl.when(k == 0)
        def _():
            dhb[...] = dh_ref[...].astype(BF16)
            du[...] = jnp.zeros_like(du)

        @pl.when((k == 0) & (i == 0))
        def _():
            dgain_ref[...] = jnp.zeros_like(dgain_ref)

        da = _dot_nt(dhb[...], wd_ref[0])
        g = z_ref[0, 0].astype(F32)
        up = z_ref[1, 0].astype(F32)
        sg = jax.nn.sigmoid(g)
        silu = g * sg
        a_ref[0] = (silu * up).astype(BF16)
        dg = (da * up * (sg * (1.0 + g * (1.0 - sg)))).astype(BF16)
        dup = (da * silu).astype(BF16)
        dz_ref[0, 0] = dg
        dz_ref[1, 0] = dup
        for n in range(2):
            cols = slice(n * (D // 2), (n + 1) * (D // 2))
            du[:, cols] += _dot(dg, wgu_ref[0, 0, :, cols]) + _dot(dup, wgu_ref[1, 0, :, cols])

        @pl.when(k == 3)
        def _():
            dx, dgain = _rms_bwd(h_ref[...], g_ref[...], du[...])
            dhp_ref[...] = dh_ref[...] + dx
            dgain_ref[...] += dgain

    return _call(
        body, name="ffn_bwd_x", grid=(t // TM, 4), ride=ride,
        in_specs=[pl.BlockSpec((TM, D), lambda i, k: (i, 0)),
                  pl.BlockSpec((TM, D), lambda i, k: (i, 0)),
                  pl.BlockSpec((1, D), lambda i, k: (0, 0)),
                  pl.BlockSpec((2, 1, TM, FC), lambda i, k: (0, k, i, 0)),
                  pl.BlockSpec((2, 1, FC, D), lambda i, k: (0, k, 0, 0)),
                  pl.BlockSpec((1, FC, D), lambda i, k: (k, 0, 0))],
        out_specs=[pl.BlockSpec((TM, D), lambda i, k: (i, 0)),
                   pl.BlockSpec((2, 1, TM, FC), lambda i, k: (0, k, i, 0)),
                   pl.BlockSpec((1, TM, FC), lambda i, k: (k, i, 0)),
                   pl.BlockSpec((1, D), lambda i, k: (0, 0)),
                   pl.BlockSpec((TM, D), lambda i, k: (i, 0))],
        out_shape=[jax.ShapeDtypeStruct((t, D), F32),
                   jax.ShapeDtypeStruct((2, 4, t, FC), BF16),
                   jax.ShapeDtypeStruct((4, t, FC), BF16),
                   jax.ShapeDtypeStruct((1, D), F32),
                   jax.ShapeDtypeStruct((t, D), BF16)],
        scratch_shapes=[pltpu.VMEM((TM, D), F32)],
        args=(dh, h, gain, z, wgu, wd))


TF = 1024
TW = 2048


def ffn_bwd_w(u, dz, a, dhb, ride=None):
    t = u.shape[0]
    tw = min(TW, t)
    steps = t // tw

    def body(u_ref, dz_ref, a_ref, dh_ref, dwgu_ref, dwd_ref, acc_gu, acc_d):
        j = pl.program_id(1)

        @pl.when(j == 0)
        def _():
            acc_gu[...] = jnp.zeros_like(acc_gu)
            acc_d[...] = jnp.zeros_like(acc_d)

        ub = u_ref[...]
        acc_gu[0] += _dot_tn(dz_ref[0, 0], ub)
        acc_gu[1] += _dot_tn(dz_ref[1, 0], ub)
        acc_d[...] += _dot_tn(a_ref[0], dh_ref[...])

        @pl.when(j == steps - 1)
        def _():
            dwgu_ref[:, 0] = acc_gu[...].astype(BF16)
            dwd_ref[0] = acc_d[...].astype(BF16)

    return _call(
        body, name="ffn_bwd_w", grid=(4, steps), ride=ride,
        in_specs=[pl.BlockSpec((tw, D), lambda k, j: (j, 0)),
                  pl.BlockSpec((2, 1, tw, FC), lambda k, j: (0, k, j, 0)),
                  pl.BlockSpec((1, tw, FC), lambda k, j: (k, j, 0)),
                  pl.BlockSpec((tw, D), lambda k, j: (j, 0))],
        out_specs=[pl.BlockSpec((2, 1, FC, D), lambda k, j: (0, k, 0, 0)),
                   pl.BlockSpec((1, FC, D), lambda k, j: (k, 0, 0))],
        out_shape=[jax.ShapeDtypeStruct((2, 4, FC, D), BF16),
                   jax.ShapeDtypeStruct((4, FC, D), BF16)],
        scratch_shapes=[pltpu.VMEM((2, FC, D), F32), pltpu.VMEM((FC, D), F32)],
        args=(u, dz, a, dhb))


def _prev_halo(i, tile=TM):
    return jnp.maximum(i * (tile // HALO) - 1, 0)


def _next_halo(i, t, tile=TM):
    return jnp.minimum((i + 1) * (tile // HALO), t // HALO - 1)


def rms_matmul(h, gain, w, bias, ride=None):
    t = h.shape[0]
    n = w.shape[1]

    def body(h_ref, g_ref, w_ref, b_ref, z_ref, u_ref):
        u = _rms_fwd(h_ref[...], g_ref[...]).astype(BF16)
        u_ref[...] = u
        z_ref[...] = (_dot(u, w_ref[...]) + b_ref[...]).astype(BF16)

    return _call(
        body, name=f"rms_matmul_{n}", grid=(t // TM,), ride=ride,
        in_specs=[pl.BlockSpec((TM, D), lambda i: (i, 0)),
                  pl.BlockSpec((1, D), lambda i: (0, 0)),
                  pl.BlockSpec((D, n), lambda i: (0, 0)),
                  pl.BlockSpec((1, n), lambda i: (0, 0))],
        out_specs=[pl.BlockSpec((TM, n), lambda i: (i, 0)),
                   pl.BlockSpec((TM, D), lambda i: (i, 0))],
        out_shape=[jax.ShapeDtypeStruct((t, n), BF16),
                   jax.ShapeDtypeStruct((t, D), BF16)],
        args=(h, gain, w, bias))


def in_proj_bwd_x(dz, w, h, gain, dh, ride=None):
    t = h.shape[0]
    n = w.shape[1]

    def body(dz_ref, w_ref, h_ref, g_ref, dh_ref, dhp_ref, dgain_ref, dbias_ref):
        @pl.when(pl.program_id(0) == 0)
        def _():
            dgain_ref[...] = jnp.zeros_like(dgain_ref)
            dbias_ref[...] = jnp.zeros_like(dbias_ref)

        du = _dot_nt(dz_ref[...], w_ref[...])
        dx, dgain = _rms_bwd(h_ref[...], g_ref[...], du)
        dhp_ref[...] = dh_ref[...] + dx
        dgain_ref[...] += dgain
        dbias_ref[...] += jnp.sum(dz_ref[...].astype(F32), axis=0, keepdims=True)

    return _call(
        body, name=f"in_proj_bwd_x_{n}", grid=(t // TM,), ride=ride,
        in_specs=[pl.BlockSpec((TM, n), lambda i: (i, 0)),
                  pl.BlockSpec((D, n), lambda i: (0, 0)),
                  pl.BlockSpec((TM, D), lambda i: (i, 0)),
                  pl.BlockSpec((1, D), lambda i: (0, 0)),
                  pl.BlockSpec((TM, D), lambda i: (i, 0))],
        out_specs=[pl.BlockSpec((TM, D), lambda i: (i, 0)),
                   pl.BlockSpec((1, D), lambda i: (0, 0)),
                   pl.BlockSpec((1, n), lambda i: (0, 0))],
        out_shape=[jax.ShapeDtypeStruct((t, D), F32),
                   jax.ShapeDtypeStruct((1, D), F32),
                   jax.ShapeDtypeStruct((1, n), F32)],
        args=(dz, w, h, gain, dh))


def in_proj_bwd_w(u, dz, ride=None):
    t = u.shape[0]
    n = dz.shape[1]
    steps = t // TM

    def body(u_ref, dz_ref, dw_ref, acc):
        s = pl.program_id(0)

        @pl.when(s == 0)
        def _():
            acc[...] = jnp.zeros_like(acc)

        acc[...] += _dot_tn(u_ref[...], dz_ref[...])

        @pl.when(s == steps - 1)
        def _():
            dw_ref[...] = acc[...].astype(BF16)

    return _call(
        body, name=f"in_proj_bwd_w_{n}", grid=(steps,), ride=ride,
        in_specs=[pl.BlockSpec((TM, D), lambda s: (s, 0)),
                  pl.BlockSpec((TM, n), lambda s: (s, 0))],
        out_specs=[pl.BlockSpec((D, n), lambda s: (0, 0))],
        out_shape=[jax.ShapeDtypeStruct((D, n), BF16)],
        scratch_shapes=[pltpu.VMEM((D, n), F32)],
        args=(u, dz))


A_TAPS = 3


def a_mix_fwd(z, h, conv, wout, ride=None):
    t = h.shape[0]

    def body(z_ref, zp_ref, h_ref, cw_ref, wo_ref, hn_ref, pad):
        i = pl.program_id(0)
        ph = zp_ref[:, D:2 * D].astype(F32) * zp_ref[:, 2 * D:].astype(F32)
        pad[0:HALO, :] = jnp.where(i == 0, 0.0, ph)
        pad[HALO:, :] = z_ref[:, D:2 * D].astype(F32) * z_ref[:, 2 * D:].astype(F32)
        q = jnp.zeros((TM, D), F32)
        for k in range(A_TAPS):
            off = HALO - (A_TAPS - 1) + k
            q += cw_ref[k:k + 1, :] * pad[off:off + TM, :]
        r = z_ref[:, 0:D].astype(F32) * q
        hn_ref[...] = h_ref[...] + _dot(r.astype(BF16), wo_ref[...])

    return _call(
        body, name="a_mix_fwd", grid=(t // TM,), ride=ride,
        in_specs=[pl.BlockSpec((TM, 3 * D), lambda i: (i, 0)),
                  pl.BlockSpec((HALO, 3 * D), lambda i: (_prev_halo(i), 0)),
                  pl.BlockSpec((TM, D), lambda i: (i, 0)),
                  pl.BlockSpec((A_TAPS, D), lambda i: (0, 0)),
                  pl.BlockSpec((D, D), lambda i: (0, 0))],
        out_specs=[pl.BlockSpec((TM, D), lambda i: (i, 0))],
        out_shape=[jax.ShapeDtypeStruct((t, D), F32)],
        scratch_shapes=[pltpu.VMEM((HALO + TM, D), F32)],
        args=(z, z, h, conv, wout))


def a_mix_bwd(dh, z, conv, wout, ride=None):
    t = dh.shape[0]
    steps = t // TM

    def body(dh_ref, dhn_ref, z_ref, zp_ref, zn_ref, cw_ref, wo_ref, dz_ref, dwo_ref, dcw_ref, pad, dqpad, dwo):
        i = pl.program_id(0)
        last = i == steps - 1

        @pl.when(i == 0)
        def _():
            dwo[...] = jnp.zeros_like(dwo)
            dcw_ref[...] = jnp.zeros_like(dcw_ref)

        ph = zp_ref[:, D:2 * D].astype(F32) * zp_ref[:, 2 * D:].astype(F32)
        pad[0:HALO, :] = jnp.where(i == 0, 0.0, ph)
        c = z_ref[:, D:2 * D].astype(F32)
        v = z_ref[:, 2 * D:].astype(F32)
        pad[HALO:, :] = c * v
        q = jnp.zeros((TM, D), F32)
        for k in range(A_TAPS):
            off = HALO - (A_TAPS - 1) + k
            q += cw_ref[k:k + 1, :] * pad[off:off + TM, :]
        b = z_ref[:, 0:D].astype(F32)
        dhb = dh_ref[...].astype(BF16)
        dwo[...] += _dot_tn((b * q).astype(BF16), dhb)
        dr = _dot_nt(dhb, wo_ref[...])
        dz_ref[:, 0:D] = (dr * q).astype(BF16)
        dq = dr * b
        drn = _dot_nt(dhn_ref[...].astype(BF16), wo_ref[...])
        dqpad[0:TM, :] = dq
        dqpad[TM:, :] = jnp.where(last, 0.0, drn * zn_ref[:, 0:D].astype(F32))
        dp = jnp.zeros((TM, D), F32)
        for k in range(A_TAPS):
            off = A_TAPS - 1 - k
            dp += cw_ref[k:k + 1, :] * dqpad[off:off + TM, :]
            poff = HALO - (A_TAPS - 1) + k
            dcw_ref[k:k + 1, :] += jnp.sum(dq * pad[poff:poff + TM, :], axis=0, keepdims=True)
        dz_ref[:, D:2 * D] = (dp * v).astype(BF16)
        dz_ref[:, 2 * D:] = (dp * c).astype(BF16)

        @pl.when(last)
        def _():
            dwo_ref[...] = dwo[...].astype(BF16)

    return _call(
        body, name="a_mix_bwd", grid=(steps,), ride=ride,
        in_specs=[pl.BlockSpec((TM, D), lambda i: (i, 0)),
                  pl.BlockSpec((HALO, D), lambda i: (_next_halo(i, t), 0)),
                  pl.BlockSpec((TM, 3 * D), lambda i: (i, 0)),
                  pl.BlockSpec((HALO, 3 * D), lambda i: (_prev_halo(i), 0)),
                  pl.BlockSpec((HALO, 3 * D), lambda i: (_next_halo(i, t), 0)),
                  pl.BlockSpec((A_TAPS, D), lambda i: (0, 0)),
                  pl.BlockSpec((D, D), lambda i: (0, 0))],
        out_specs=[pl.BlockSpec((TM, 3 * D), lambda i: (i, 0)),
                   pl.BlockSpec((D, D), lambda i: (0, 0)),
                   pl.BlockSpec((A_TAPS, D), lambda i: (0, 0))],
        out_shape=[jax.ShapeDtypeStruct((t, 3 * D), BF16),
                   jax.ShapeDtypeStruct((D, D), BF16),
                   jax.ShapeDtypeStruct((A_TAPS, D), F32)],
        scratch_shapes=[pltpu.VMEM((HALO + TM, D), F32), pltpu.VMEM((TM + HALO, D), F32), pltpu.VMEM((D, D), F32)],
        args=(dh, dh, z, z, z, conv, wout))


C_TAPS = 31


def _glu(zr):
    return zr[:, 0:D].astype(F32) * jax.nn.sigmoid(zr[:, D:].astype(F32))


def _ln_silu(h2, lg, lb):
    mu = jnp.mean(h2, axis=-1, keepdims=True)
    xc = h2 - mu
    rstd = lax.rsqrt(jnp.mean(xc * xc, axis=-1, keepdims=True) + LN_EPS)
    xn = xc * rstd
    h3 = xn * lg + lb
    s3 = jax.nn.sigmoid(h3)
    return xn, rstd, h3, s3


def _ln_silu_bwd(h2, lg, lb, dh4):
    xn, rstd, h3, s3 = _ln_silu(h2, lg, lb)
    dh3 = dh4 * (s3 * (1.0 + h3 * (1.0 - s3)))
    dxn = dh3 * lg
    dh2 = rstd * (dxn - jnp.mean(dxn, axis=-1, keepdims=True) - xn * jnp.mean(dxn * xn, axis=-1, keepdims=True))
    return dh2, dh3, xn, h3 * s3


TC = 256
RB = 64
LANES = 128
SHIFTS = 7


def _shifted_copies(src, sh, rows):
    for b in range(1, SHIFTS + 1):
        sh[b - 1, 0:rows, :] = src[b:b + rows, :]


def _window(src, sh, o, r0, lanes):
    a, b = divmod(o, 8)
    ref = src if b == 0 else sh.at[b - 1]
    return ref[8 * a + r0:8 * a + r0 + RB, lanes]


def c_mix_fwd(z, h, dw, bdw, lg, lb, w2, b2, ride=None):
    t = h.shape[0]

    def body(z_ref, zp_ref, h_ref, dw_ref, bdw_ref, lg_ref, lb_ref, w2_ref, b2_ref, hn_ref, h2_ref, pad, sh):
        i = pl.program_id(0)
        pad[0:HALO, :] = jnp.where(i == 0, 0.0, _glu(zp_ref))
        pad[HALO:, :] = _glu(z_ref)
        _shifted_copies(pad, sh, TC + 24)
        for l in range(D // LANES):
            lanes = slice(l * LANES, (l + 1) * LANES)
            for r0 in range(0, TC, RB):
                acc = jnp.zeros((RB, LANES), F32) + bdw_ref[:, lanes]
                for k in range(C_TAPS):
                    acc += dw_ref[k:k + 1, lanes] * _window(pad, sh, HALO - (C_TAPS - 1) + k, r0, lanes)
                h2_ref[r0:r0 + RB, lanes] = acc
        _, _, h3, s3 = _ln_silu(h2_ref[...], lg_ref[...], lb_ref[...])
        hn_ref[...] = h_ref[...] + _dot((h3 * s3).astype(BF16), w2_ref[...]) + b2_ref[...]

    vec = pl.BlockSpec((1, D), lambda i: (0, 0))
    return _call(
        body, name="c_mix_fwd", grid=(t // TC,), ride=ride,
        in_specs=[pl.BlockSpec((TC, 2 * D), lambda i: (i, 0)),
                  pl.BlockSpec((HALO, 2 * D), lambda i: (_prev_halo(i, TC), 0)),
                  pl.BlockSpec((TC, D), lambda i: (i, 0)),
                  pl.BlockSpec((C_TAPS, D), lambda i: (0, 0)),
                  vec, vec, vec,
                  pl.BlockSpec((D, D), lambda i: (0, 0)),
                  vec],
        out_specs=[pl.BlockSpec((TC, D), lambda i: (i, 0)),
                   pl.BlockSpec((TC, D), lambda i: (i, 0))],
        out_shape=[jax.ShapeDtypeStruct((t, D), F32),
                   jax.ShapeDtypeStruct((t, D), F32)],
        scratch_shapes=[pltpu.VMEM((HALO + TC, D), F32), pltpu.VMEM((SHIFTS, TC + 24, D), F32)],
        args=(z, z, h, dw, bdw, lg, lb, w2, b2))


def c_mix_bwd(dh, z, h2, dw, lg, lb, w2, ride=None):
    t = dh.shape[0]
    steps = t // TC

    def body(dh_ref, dhn_ref, z_ref, zp_ref, h2_ref, h2n_ref, dw_ref, lg_ref, lb_ref, w2_ref,
             dz_ref, dw2_ref, db2_ref, dlg_ref, dlb_ref, dbdw_ref, ddw_ref, pad, dpad, dw2, sh, dh1):
        i = pl.program_id(0)
        last = i == steps - 1

        @pl.when(i == 0)
        def _():
            for r in (dw2, db2_ref, dlg_ref, dlb_ref, dbdw_ref, ddw_ref):
                r[...] = jnp.zeros_like(r)

        lg, lb = lg_ref[...], lb_ref[...]
        dh = dh_ref[...]
        dhb = dh.astype(BF16)
        dh2, dh3, xn, h4 = _ln_silu_bwd(h2_ref[...], lg, lb, _dot_nt(dhb, w2_ref[...]))
        dw2[...] += _dot_tn(h4.astype(BF16), dhb)
        db2_ref[...] += jnp.sum(dh, axis=0, keepdims=True)
        dlg_ref[...] += jnp.sum(dh3 * xn, axis=0, keepdims=True)
        dlb_ref[...] += jnp.sum(dh3, axis=0, keepdims=True)
        dbdw_ref[...] += jnp.sum(dh2, axis=0, keepdims=True)
        dh2n, _, _, _ = _ln_silu_bwd(h2n_ref[...], lg, lb, _dot_nt(dhn_ref[...].astype(BF16), w2_ref[...]))
        dpad[0:TC, :] = dh2
        dpad[TC:, :] = jnp.where(last, 0.0, dh2n)
        _shifted_copies(dpad, sh, TC + 24)
        for l in range(D // LANES):
            lanes = slice(l * LANES, (l + 1) * LANES)
            for r0 in range(0, TC, RB):
                acc = jnp.zeros((RB, LANES), F32)
                for k in range(C_TAPS):
                    acc += dw_ref[k:k + 1, lanes] * _window(dpad, sh, C_TAPS - 1 - k, r0, lanes)
                dh1[r0:r0 + RB, lanes] = acc
        pad[0:HALO, :] = jnp.where(i == 0, 0.0, _glu(zp_ref))
        pad[HALO:, :] = _glu(z_ref)
        _shifted_copies(pad, sh, TC + 24)
        for l in range(D // LANES):
            lanes = slice(l * LANES, (l + 1) * LANES)
            accs = [jnp.zeros((8, LANES), F32) for _ in range(C_TAPS)]
            for r0 in range(0, TC, RB):
                d = dpad[r0:r0 + RB, lanes]
                for k in range(C_TAPS):
                    prod = d * _window(pad, sh, HALO - (C_TAPS - 1) + k, r0, lanes)
                    accs[k] += jnp.sum(prod.reshape(RB // 8, 8, LANES), axis=0)
            for k in range(C_TAPS):
                ddw_ref[k:k + 1, lanes] += jnp.sum(accs[k], axis=0, keepdims=True)
        a = z_ref[:, 0:D].astype(F32)
        sg = jax.nn.sigmoid(z_ref[:, D:].astype(F32))
        d1 = dh1[...]
        dz_ref[:, 0:D] = (d1 * sg).astype(BF16)
        dz_ref[:, D:] = (d1 * a * sg * (1.0 - sg)).astype(BF16)

        @pl.when(last)
        def _():
            dw2_ref[...] = dw2[...].astype(BF16)

    vec = pl.BlockSpec((1, D), lambda i: (0, 0))
    return _call(
        body, name="c_mix_bwd", grid=(steps,), ride=ride,
        in_specs=[pl.BlockSpec((TC, D), lambda i: (i, 0)),
                  pl.BlockSpec((HALO, D), lambda i: (_next_halo(i, t, TC), 0)),
                  pl.BlockSpec((TC, 2 * D), lambda i: (i, 0)),
                  pl.BlockSpec((HALO, 2 * D), lambda i: (_prev_halo(i, TC), 0)),
                  pl.BlockSpec((TC, D), lambda i: (i, 0)),
                  pl.BlockSpec((HALO, D), lambda i: (_next_halo(i, t, TC), 0)),
                  pl.BlockSpec((C_TAPS, D), lambda i: (0, 0)),
                  vec, vec,
                  pl.BlockSpec((D, D), lambda i: (0, 0))],
        out_specs=[pl.BlockSpec((TC, 2 * D), lambda i: (i, 0)),
                   pl.BlockSpec((D, D), lambda i: (0, 0)),
                   vec, vec, vec, vec,
                   pl.BlockSpec((C_TAPS, D), lambda i: (0, 0))],
        out_shape=[jax.ShapeDtypeStruct((t, 2 * D), BF16),
                   jax.ShapeDtypeStruct((D, D), BF16)]
                  + [jax.ShapeDtypeStruct((1, D), F32)] * 4
                  + [jax.ShapeDtypeStruct((C_TAPS, D), F32)],
        scratch_shapes=[pltpu.VMEM((HALO + TC, D), F32), pltpu.VMEM((TC + HALO, D), F32), pltpu.VMEM((D, D), F32),
                        pltpu.VMEM((SHIFTS, TC + 24, D), F32), pltpu.VMEM((TC, D), F32)],
        args=(dh, dh, z, z, h2, h2, dw, lg, lb, w2))


POOL_WINDOWS = (2, 4, 8, 16)
GW = D // len(POOL_WINDOWS)


def _pool_mixed(pad, g, w, inv_cnt):
    cols = slice(g * GW, (g + 1) * GW)
    s = pad[HALO:HALO + TM, cols]
    u = s
    for j in range(1, w):
        s = s + pad[HALO - j:HALO - j + TM, cols]
    return s * inv_cnt - u


def _inv_cnt(i, w):
    row = i * TM + lax.broadcasted_iota(jnp.int32, (TM, 1), 0)
    return 1.0 / jnp.minimum(row + 1, w).astype(F32)


def b_mix_fwd(h, gain, wg, scale, ride=None):
    t = h.shape[0]

    def body(h_ref, hp_ref, g_ref, wg_ref, sc_ref, hn_ref, pad):
        i = pl.program_id(0)
        gain = g_ref[...]
        pad[0:HALO, :] = jnp.where(i == 0, 0.0, _rms_fwd(hp_ref[...], gain))
        pad[HALO:, :] = _rms_fwd(h_ref[...], gain)
        for g, w in enumerate(POOL_WINDOWS):
            cols = slice(g * GW, (g + 1) * GW)
            mixed = _pool_mixed(pad, g, w, _inv_cnt(i, w))
            y = _dot(mixed.astype(BF16), wg_ref[g])
            hn_ref[:, cols] = h_ref[:, cols] + y * sc_ref[:, cols]

    return _call(
        body, name="b_mix_fwd", grid=(t // TM,), ride=ride,
        in_specs=[pl.BlockSpec((TM, D), lambda i: (i, 0)),
                  pl.BlockSpec((HALO, D), lambda i: (_prev_halo(i), 0)),
                  pl.BlockSpec((1, D), lambda i: (0, 0)),
                  pl.BlockSpec((4, GW, GW), lambda i: (0, 0, 0)),
                  pl.BlockSpec((1, D), lambda i: (0, 0))],
        out_specs=[pl.BlockSpec((TM, D), lambda i: (i, 0))],
        out_shape=[jax.ShapeDtypeStruct((t, D), F32)],
        scratch_shapes=[pltpu.VMEM((HALO + TM, D), F32)],
        args=(h, h, gain, wg, scale))


def b_mix_bwd(dh, h, gain, wg, scale, ride=None):
    t = h.shape[0]
    steps = t // TM

    def body(dh_ref, dhn_ref, h_ref, hp_ref, g_ref, wg_ref, sc_ref, dhp_ref, dgain_ref, dwg_ref, dsc_ref, pad, dpad, du):
        i = pl.program_id(0)
        last = i == steps - 1

        @pl.when(i == 0)
        def _():
            for r in (dgain_ref, dwg_ref, dsc_ref):
                r[...] = jnp.zeros_like(r)

        gain = g_ref[...]
        pad[0:HALO, :] = jnp.where(i == 0, 0.0, _rms_fwd(hp_ref[...], gain))
        pad[HALO:, :] = _rms_fwd(h_ref[...], gain)
        for g, w in enumerate(POOL_WINDOWS):
            cols = slice(g * GW, (g + 1) * GW)
            inv_cnt = _inv_cnt(i, w)
            mixed = _pool_mixed(pad, g, w, inv_cnt).astype(BF16)
            dh = dh_ref[:, cols]
            dsc_ref[:, cols] += jnp.sum(dh * _dot(mixed, wg_ref[g]), axis=0, keepdims=True)
            dy = (dh * sc_ref[:, cols]).astype(BF16)
            dwg_ref[g] += _dot_tn(mixed, dy)
            dm = _dot_nt(dy, wg_ref[g])
            dmn = _dot_nt((dhn_ref[:, cols] * sc_ref[:, cols]).astype(BF16), wg_ref[g])
            dpad[0:TM, cols] = dm * inv_cnt
            dpad[TM:, cols] = jnp.where(last, 0.0, dmn * (1.0 / w))
            s = dpad[0:TM, cols]
            for j in range(1, w):
                s = s + dpad[j:j + TM, cols]
            du[:, cols] = s - dm
        dx, dgain = _rms_bwd(h_ref[...], gain, du[...])
        dhp_ref[...] = dh_ref[...] + dx
        dgain_ref[...] += dgain

    return _call(
        body, name="b_mix_bwd", grid=(steps,), ride=ride,
        in_specs=[pl.BlockSpec((TM, D), lambda i: (i, 0)),
                  pl.BlockSpec((HALO, D), lambda i: (_next_halo(i, t), 0)),
                  pl.BlockSpec((TM, D), lambda i: (i, 0)),
                  pl.BlockSpec((HALO, D), lambda i: (_prev_halo(i), 0)),
                  pl.BlockSpec((1, D), lambda i: (0, 0)),
                  pl.BlockSpec((4, GW, GW), lambda i: (0, 0, 0)),
                  pl.BlockSpec((1, D), lambda i: (0, 0))],
        out_specs=[pl.BlockSpec((TM, D), lambda i: (i, 0)),
                   pl.BlockSpec((1, D), lambda i: (0, 0)),
                   pl.BlockSpec((4, GW, GW), lambda i: (0, 0, 0)),
                   pl.BlockSpec((1, D), lambda i: (0, 0))],
        out_shape=[jax.ShapeDtypeStruct((t, D), F32),
                   jax.ShapeDtypeStruct((1, D), F32),
                   jax.ShapeDtypeStruct((4, GW, GW), F32),
                   jax.ShapeDtypeStruct((1, D), F32)],
        scratch_shapes=[pltpu.VMEM((HALO + TM, D), F32), pltpu.VMEM((TM + HALO, D), F32), pltpu.VMEM((TM, D), F32)],
        args=(dh, dh, h, h, gain, wg, scale))


LOSS_LANES = 128


def loss_head(h, gain, target):
    t = h.shape[0]

    def body(h_ref, g_ref, tg_ref, loss_ref, dh_ref, dgain_ref):
        @pl.when(pl.program_id(0) == 0)
        def _():
            loss_ref[...] = jnp.zeros_like(loss_ref)
            dgain_ref[...] = jnp.zeros_like(dgain_ref)

        x, gain = h_ref[...], g_ref[...]
        err = _rms_fwd(x, gain) - tg_ref[...]
        per_row = jnp.mean(err * err, axis=-1, keepdims=True)
        loss_ref[...] += jnp.broadcast_to(0.5 * jnp.sum(per_row, axis=0, keepdims=True), (1, LOSS_LANES))
        dx, dgain = _rms_bwd(x, gain, err * (1.0 / D))
        dh_ref[...] = dx
        dgain_ref[...] += dgain

    outs, _ = _call(
        body, name="loss_head", grid=(t // TM,),
        in_specs=[pl.BlockSpec((TM, D), lambda i: (i, 0)),
                  pl.BlockSpec((1, D), lambda i: (0, 0)),
                  pl.BlockSpec((TM, D), lambda i: (i, 0))],
        out_specs=[pl.BlockSpec((1, LOSS_LANES), lambda i: (0, 0)),
                   pl.BlockSpec((TM, D), lambda i: (i, 0)),
                   pl.BlockSpec((1, D), lambda i: (0, 0))],
        out_shape=[jax.ShapeDtypeStruct((1, LOSS_LANES), F32),
                   jax.ShapeDtypeStruct((t, D), F32),
                   jax.ShapeDtypeStruct((1, D), F32)],
        args=(h, gain, target))
    return outs


ADAM_LR = 0.001
ADAM_B1 = 0.9
ADAM_B2 = 0.999
ADAM_EPS = 1e-08
ADAM_WD = 0.01
ADAM_STEP = 10
ADAM_VMEM = 40 * 1024 * 1024


def cast_all(arrays):
    def body(*refs):
        for src, dst in zip(refs[:len(arrays)], refs[len(arrays):]):
            dst[...] = src[...].astype(BF16)

    return pl.pallas_call(
        body, name="cast_all", out_shape=[jax.ShapeDtypeStruct(a.shape, BF16) for a in arrays],
        compiler_params=pltpu.CompilerParams(vmem_limit_bytes=VMEM_LIMIT),
    )(*arrays)


def _adam_math(w, m, v, g):
    m = ADAM_B1 * m + (1.0 - ADAM_B1) * g
    v = ADAM_B2 * v + (1.0 - ADAM_B2) * (g * g)
    m_hat = m / (1.0 - ADAM_B1 ** ADAM_STEP)
    v_hat = v / (1.0 - ADAM_B2 ** ADAM_STEP)
    return -ADAM_LR * (m_hat / (jnp.sqrt(v_hat) + ADAM_EPS) + ADAM_WD * w), m, v


def adamw(ws, ms, vs, gps, rb, tokens=()):
    n = len(ws)
    r, c = ws[0].shape
    nb = r // rb

    def body(*refs):
        i = pl.program_id(0)
        outs = refs[4 * n + len(tokens):]
        for j in range(n):
            w_ref, m_ref, v_ref, gp_ref = (refs[q * n + j] for q in range(4))
            g_ref, d_ref, nm_ref, nv_ref = (outs[q * n + j] for q in range(4))

            @pl.when(i // nb == j)
            def _():
                g = gp_ref[0].astype(F32)
                for s in range(1, N_DEV):
                    g = g + gp_ref[s].astype(F32)
                g_ref[...] = g
                d_ref[...], nm_ref[...], nv_ref[...] = _adam_math(w_ref[...], m_ref[...], v_ref[...], g)

    def blk(j):
        return pl.BlockSpec((rb, c), lambda i: (jnp.clip(i - j * nb, 0, nb - 1), 0))

    def gblk(j):
        return pl.BlockSpec((N_DEV, rb, c), lambda i: (0, jnp.clip(i - j * nb, 0, nb - 1), 0))

    outs, _ = _call(
        body, name=f"adamw_{n}x{r}x{c}", grid=(n * nb,),
        in_specs=[blk(j) for _ in range(3) for j in range(n)] + [gblk(j) for j in range(n)] + [ANY] * len(tokens),
        out_specs=[blk(j) for _ in range(4) for j in range(n)],
        out_shape=[jax.ShapeDtypeStruct((r, c), F32)] * (4 * n),
        args=(*ws, *ms, *vs, *gps, *tokens))
    return outs[:n], outs[n:2 * n], outs[2 * n:3 * n], outs[3 * n:]


def adamw_vectors(ws, ms, vs, gparts):
    nv = len(ws)

    def body(*refs):
        w_refs, m_refs, v_refs = refs[:nv], refs[nv:2 * nv], refs[2 * nv:3 * nv]
        gp_ref = refs[3 * nv]
        outs = refs[3 * nv + 1:]
        g_refs, d_refs, nm_refs, nv_refs = outs[:nv], outs[nv:2 * nv], outs[2 * nv:3 * nv], outs[3 * nv:]
        row = 0
        for i in range(nv):
            for part in range(w_refs[i].shape[1] // D):
                cols = slice(part * D, (part + 1) * D)
                g = gp_ref[0, row:row + 1, :]
                for s in range(1, N_DEV):
                    g = g + gp_ref[s, row:row + 1, :]
                g_refs[i][:, cols] = g
                d_refs[i][:, cols], nm_refs[i][:, cols], nv_refs[i][:, cols] = _adam_math(
                    w_refs[i][:, cols], m_refs[i][:, cols], v_refs[i][:, cols], g)
                row += 1

    shapes = [jax.ShapeDtypeStruct(w.shape, F32) for w in ws]
    outs = pl.pallas_call(body, name="adamw_vectors", out_shape=shapes * 4)(*ws, *ms, *vs, gparts)
    return outs[:nv], outs[nv:2 * nv], outs[2 * nv:3 * nv], outs[3 * nv:]


WEIGHTS = ["ln1_0", "a0_w_in", "a0_conv", "a0_w_out", "ln2_0", "ffn0_w_gu", "ffn0_w_down",
           "ln1_1", "b1_w_grp", "b1_scale", "ln2_1", "ffn1_w_gu", "ffn1_w_down",
           "ln1_2", "c2_w_pw1", "c2_b_pw1", "c2_dw", "c2_b_dw", "c2_ln_g", "c2_ln_b", "c2_w_pw2", "c2_b_pw2",
           "ln2_2", "ffn2_w_gu", "ffn2_w_down",
           "ln1_3", "a3_w_in", "a3_conv", "a3_w_out", "ln2_3", "ffn3_w_gu", "ffn3_w_down", "ln_f"]
SHARDED = {"a0_w_in": ("cols", 256), "a0_conv": ("cols", A_TAPS), "a0_w_out": ("rows", 128),
           "ffn0_w_gu": ("lead", 176), "ffn0_w_down": ("rows", 176),
           "b1_w_grp": ("mid", 128),
           "ffn1_w_gu": ("lead", 176), "ffn1_w_down": ("rows", 176),
           "c2_w_pw1": ("cols", 256), "c2_dw": ("cols", C_TAPS), "c2_w_pw2": ("rows", 128),
           "ffn2_w_gu": ("lead", 176), "ffn2_w_down": ("rows", 176),
           "a3_w_in": ("cols", 256), "a3_conv": ("cols", A_TAPS), "a3_w_out": ("rows", 128),
           "ffn3_w_gu": ("lead", 176), "ffn3_w_down": ("rows", 176)}
IN_PROJ = ("a0_w_in", "c2_w_pw1", "a3_w_in")
REPL = [n for n in WEIGHTS if n not in SHARDED]
REPL_ROWS = 16
GATHER_PLAN = {"first": ["a0_w_in", "a0_w_out", "a0_conv"],
               "in0": ["ffn0_w_gu"], "mix0": ["ffn0_w_down"],
               "ffn0": ["b1_w_grp", "ffn1_w_gu", "ffn1_w_down"],
               "ffn1": ["c2_w_pw1", "c2_w_pw2", "c2_dw", "ffn2_w_gu"],
               "in2": ["ffn2_w_down"],
               "mix2": ["a3_w_in", "a3_w_out", "a3_conv"],
               "ffn2": ["ffn3_w_gu", "ffn3_w_down"]}
SCATTER_PLAN = {"mixb3": ["ffn3_w_down"], "inw3": ["a3_w_out", "a3_conv"],
                "ffnx2": ["ffn3_w_gu"], "ffnw2": ["a3_w_in"],
                "mixb2": ["ffn2_w_gu", "ffn2_w_down"], "inw2": ["c2_w_pw2", "c2_dw"],
                "ffnx1": ["c2_w_pw1"],
                "ffnx0": ["ffn1_w_gu"], "ffnw0": ["ffn1_w_down", "b1_w_grp"]}
LATE_FFN = ["ffn0_w_gu", "ffn0_w_down"]
LATE_MIX = ["a0_w_in", "a0_w_out", "a0_conv"]


def _step(p):
    vec = lambda n: p[n].reshape(1, -1)
    x, target = p["x"][0], p["loss_target"][0]

    names = list(SHARDED)
    stored = lambda n, a: a.T if n.endswith("w_gu") else a
    shard = dict(zip(names, cast_all([stored(n, p[n]) for n in names])))
    full = {}

    def gather(slot):
        names = GATHER_PLAN[slot]
        return gather_ride([shard[n] for n in names], ["cols" if n in IN_PROJ else "lead" for n in names])

    def landed(slot, outs):
        full.update(zip(GATHER_PLAN[slot], outs))

    def conv_full(n):
        k = full[n].shape[1]
        return full[n].transpose(1, 0, 2).reshape(k, D).astype(F32)

    def wgu(i):
        return full[f"ffn{i}_w_gu"].reshape(2, 4, FC, D)

    def wd(i):
        return full[f"ffn{i}_w_down"].reshape(4, FC, D)

    landed("first", run_ride(gather("first"), "gather_first"))
    no_bias = jnp.zeros((1, 3 * D), F32)
    h = [x]
    saved = {}
    conv, wout = {}, {}

    (z, u), got = rms_matmul(h[-1], vec("ln1_0"), full["a0_w_in"], no_bias, ride=gather("in0"))
    landed("in0", got)
    conv[0], wout[0] = conv_full("a0_conv"), full["a0_w_out"].reshape(D, D)
    (hm,), got = a_mix_fwd(z, h[-1], conv[0], wout[0], ride=gather("mix0"))
    landed("mix0", got)
    saved["mix0"] = (z, u)
    h.append(hm)
    (hn, zf, uf), got = ffn_fwd(hm, vec("ln2_0"), wgu(0), wd(0), ride=gather("ffn0"))
    landed("ffn0", got)
    saved["ffn0"] = (zf, uf)
    h.append(hn)

    wgrp = full["b1_w_grp"].transpose(1, 0, 2, 3).reshape(4, GW, GW)
    (hm,), _ = b_mix_fwd(h[-1], vec("ln1_1"), wgrp, vec("b1_scale"))
    h.append(hm)
    (hn, zf, uf), got = ffn_fwd(hm, vec("ln2_1"), wgu(1), wd(1), ride=gather("ffn1"))
    landed("ffn1", got)
    saved["ffn1"] = (zf, uf)
    h.append(hn)

    (z, u), got = rms_matmul(h[-1], vec("ln1_2"), full["c2_w_pw1"], vec("c2_b_pw1"), ride=gather("in2"))
    landed("in2", got)
    cdw, wpw2 = conv_full("c2_dw"), full["c2_w_pw2"].reshape(D, D)
    (hm, h2), got = c_mix_fwd(z, h[-1], cdw, vec("c2_b_dw"), vec("c2_ln_g"), vec("c2_ln_b"), wpw2, vec("c2_b_pw2"),
                              ride=gather("mix2"))
    landed("mix2", got)
    saved["mix2"] = (z, u, h2)
    h.append(hm)
    (hn, zf, uf), got = ffn_fwd(hm, vec("ln2_2"), wgu(2), wd(2), ride=gather("ffn2"))
    landed("ffn2", got)
    saved["ffn2"] = (zf, uf)
    h.append(hn)

    (z, u), _ = rms_matmul(h[-1], vec("ln1_3"), full["a3_w_in"], no_bias)
    conv[3], wout[3] = conv_full("a3_conv"), full["a3_w_out"].reshape(D, D)
    (hm,), _ = a_mix_fwd(z, h[-1], conv[3], wout[3])
    saved["mix3"] = (z, u)
    h.append(hm)
    (hn, zf, uf), _ = ffn_fwd(hm, vec("ln2_3"), wgu(3), wd(3))
    saved["ffn3"] = (zf, uf)
    h.append(hn)

    loss_lanes, dh, g_lnf = loss_head(h[-1], vec("ln_f"), target)

    g = {"ln_f": g_lnf}
    recv = {}

    def repl_rows():
        loss_row = jnp.pad(loss_lanes, ((0, 0), (0, D - LOSS_LANES)))
        return jnp.concatenate([g[n].reshape(-1, D) for n in REPL] + [loss_row], axis=0)

    def scatter(slot):
        parts = []
        for n in SCATTER_PLAN.get(slot, []):
            parts.append((repl_rows(), "all") if n == "repl" else (g[n], SHARDED[n][0]))
        return scatter_ride(parts) if parts else None

    def arrived(slot, outs):
        recv.update(zip(SCATTER_PLAN.get(slot, []), outs))

    for i in (3, 2, 1, 0):
        zf, uf = saved[f"ffn{i}"]
        (dh_prev, dzf, a, g[f"ln2_{i}"], dhb), got = ffn_bwd_x(dh, h[2 * i + 1], vec(f"ln2_{i}"), zf, wgu(i), wd(i),
                                                             ride=scatter(f"ffnx{i}"))
        arrived(f"ffnx{i}", got)
        dh = dh_prev
        (dwgu, dwd), got = ffn_bwd_w(uf, dzf, a, dhb, ride=scatter(f"ffnw{i}"))
        arrived(f"ffnw{i}", got)
        g[f"ffn{i}_w_gu"], g[f"ffn{i}_w_down"] = dwgu.reshape(N_DEV, FC, D), dwd.reshape(FF, D)
        if i == 0:
            late_ffn = scatter_start([(g[n], SHARDED[n][0]) for n in LATE_FFN], "late_ffn_start")
        hin = h[2 * i]
        if i in (0, 3):
            z, u = saved[f"mix{i}"]
            (dz, g[f"a{i}_w_out"], g[f"a{i}_conv"]), got = a_mix_bwd(dh, z, conv[i], wout[i], ride=scatter(f"mixb{i}"))
            arrived(f"mixb{i}", got)
            (g[f"a{i}_w_in"],), got = in_proj_bwd_w(u, dz, ride=scatter(f"inw{i}"))
            arrived(f"inw{i}", got)
            if i == 0:
                late_mix = scatter_start([(g[n], SHARDED[n][0]) for n in LATE_MIX], "late_mix_start")
            (dh, g[f"ln1_{i}"], _), got = in_proj_bwd_x(dz, full[f"a{i}_w_in"], hin, vec(f"ln1_{i}"), dh,
                                                       ride=scatter(f"inx{i}"))
            arrived(f"inx{i}", got)
        elif i == 1:
            (dh, g["ln1_1"], g["b1_w_grp"], g["b1_scale"]), got = b_mix_bwd(dh, hin, vec("ln1_1"), wgrp, vec("b1_scale"),
                                                                             ride=scatter("mixb1"))
            arrived("mixb1", got)
        else:
            z, u, h2 = saved["mix2"]
            (dz, g["c2_w_pw2"], g["c2_b_pw2"], g["c2_ln_g"], g["c2_ln_b"], g["c2_b_dw"], g["c2_dw"]), got = c_mix_bwd(
                dh, z, h2, cdw, vec("c2_ln_g"), vec("c2_ln_b"), wpw2, ride=scatter("mixb2"))
            arrived("mixb2", got)
            (g["c2_w_pw1"],), got = in_proj_bwd_w(u, dz, ride=scatter("inw2"))
            arrived("inw2", got)
            (dh, g["ln1_2"], g["c2_b_pw1"]), got = in_proj_bwd_x(dz, full["c2_w_pw1"], hin, vec("ln1_2"), dh,
                                                                ride=scatter("inx2"))
            arrived("inx2", got)
    grad_x = dh[None]
    late_repl = scatter_start([(repl_rows(), "all")], "late_repl_start")

    grad, delta, new_m, new_v = {}, {}, {}, {}
    two_d = lambda n, a: stored(n, a.reshape(-1, p[n].shape[-1]))

    def adam_calls(names_, tokens):
        groups, last = {}, None
        for n in names_:
            groups.setdefault((two_d(n, p[n]).shape, SHARDED[n][1]), []).append(n)
        for (shape, rb), members in groups.items():
            per_weight = 2 * rb * shape[1] * (7 * 4 + N_DEV * recv[members[0]].dtype.itemsize)
            at_once = max(1, (ADAM_VMEM // per_weight))
            for lo in range(0, len(members), at_once):
                ns = members[lo:lo + at_once]
                outs = adamw([two_d(n, p[n]) for n in ns], [two_d(n, p["m_" + n]) for n in ns],
                             [two_d(n, p["v_" + n]) for n in ns], [recv[n].reshape(N_DEV, *shape) for n in ns], rb, tokens)
                for res, o in zip((grad, delta, new_m, new_v), outs):
                    res.update({n: stored(n, a).reshape(p[n].shape) for n, a in zip(ns, o)})
                last = outs[0][0]
        return last

    early_done = adam_calls([n for n in SHARDED if n not in LATE_FFN + LATE_MIX], (late_ffn[-1], late_mix[-1]))
    recv.update(zip(LATE_FFN, scatter_wait(late_ffn, [early_done], "late_ffn_wait")))
    recv.update(zip(LATE_MIX, scatter_wait(late_mix, [early_done, dh], "late_mix_wait")))
    late_done = adam_calls(LATE_FFN + LATE_MIX, ())
    recv["repl"] = scatter_wait(late_repl, [late_done], "late_repl_wait")[0]
    outs = adamw_vectors([vec(n) for n in REPL], [vec("m_" + n) for n in REPL], [vec("v_" + n) for n in REPL], recv["repl"])
    for res, o in zip((grad, delta, new_m, new_v), outs):
        res.update({n: a.reshape(p[n].shape) for n, a in zip(REPL, o)})

    loss = jnp.sum(recv["repl"][:, REPL_ROWS, 0])
    return (loss, grad_x, *[grad[n] for n in WEIGHTS], *[delta[n] for n in WEIGHTS],
            *[new_m[n] for n in WEIGHTS], *[new_v[n] for n in WEIGHTS])


def kernel(x, ln1_0, a0_w_in, a0_conv, a0_w_out, ln2_0, ffn0_w_gu, ffn0_w_down, ln1_1, b1_w_grp, b1_scale, ln2_1, ffn1_w_gu, ffn1_w_down, ln1_2, c2_w_pw1, c2_b_pw1, c2_dw, c2_b_dw, c2_ln_g, c2_ln_b, c2_w_pw2, c2_b_pw2, ln2_2, ffn2_w_gu, ffn2_w_down, ln1_3, a3_w_in, a3_conv, a3_w_out, ln2_3, ffn3_w_gu, ffn3_w_down, ln_f, loss_target, m_ln1_0, m_a0_w_in, m_a0_conv, m_a0_w_out, m_ln2_0, m_ffn0_w_gu, m_ffn0_w_down, m_ln1_1, m_b1_w_grp, m_b1_scale, m_ln2_1, m_ffn1_w_gu, m_ffn1_w_down, m_ln1_2, m_c2_w_pw1, m_c2_b_pw1, m_c2_dw, m_c2_b_dw, m_c2_ln_g, m_c2_ln_b, m_c2_w_pw2, m_c2_b_pw2, m_ln2_2, m_ffn2_w_gu, m_ffn2_w_down, m_ln1_3, m_a3_w_in, m_a3_conv, m_a3_w_out, m_ln2_3, m_ffn3_w_gu, m_ffn3_w_down, m_ln_f, v_ln1_0, v_a0_w_in, v_a0_conv, v_a0_w_out, v_ln2_0, v_ffn0_w_gu, v_ffn0_w_down, v_ln1_1, v_b1_w_grp, v_b1_scale, v_ln2_1, v_ffn1_w_gu, v_ffn1_w_down, v_ln1_2, v_c2_w_pw1, v_c2_b_pw1, v_c2_dw, v_c2_b_dw, v_c2_ln_g, v_c2_ln_b, v_c2_w_pw2, v_c2_b_pw2, v_ln2_2, v_ffn2_w_gu, v_ffn2_w_down, v_ln1_3, v_a3_w_in, v_a3_conv, v_a3_w_out, v_ln2_3, v_ffn3_w_gu, v_ffn3_w_down, v_ln_f):
    return _step(dict(locals()))
```

```python
import jax
import jax.numpy as jnp
from jax import lax
from jax.experimental import pallas as pl
from jax.experimental.pallas import tpu as pltpu

F32 = jnp.float32
BF16 = jnp.bfloat16

N_DEV = 8
D = 1024
FF = 2816
FC = FF // 4
RMS_EPS = 1e-6
LN_EPS = 1e-5
TM = 512
HALO = 32
VMEM_LIMIT = 60 * 1024 * 1024

NT = (((1,), (1,)), ((), ()))
TN = (((0,), (0,)), ((), ()))
MESH = pl.DeviceIdType.MESH
ANY = pl.BlockSpec(memory_space=pl.ANY)
N_PEERS = N_DEV - 1


def _dot(a, b):
    return jnp.dot(a, b, preferred_element_type=F32)


def _dot_nt(a, b):
    return lax.dot_general(a, b, NT, preferred_element_type=F32)


def _dot_tn(a, b):
    return lax.dot_general(a, b, TN, preferred_element_type=F32)


def _rms_fwd(x, gain):
    r = lax.rsqrt(jnp.mean(x * x, axis=-1, keepdims=True) + RMS_EPS)
    return x * r * gain


def _rms_bwd(x, gain, du):
    r = lax.rsqrt(jnp.mean(x * x, axis=-1, keepdims=True) + RMS_EPS)
    xhat = x * r
    dgain = jnp.sum(du * xhat, axis=0, keepdims=True)
    dxhat = du * gain
    dx = r * (dxhat - xhat * jnp.mean(dxhat * xhat, axis=-1, keepdims=True))
    return dx, dgain


def _dev_index(p):
    return 4 * p[0] + 2 * p[1] + p[2]


def _place():
    return lax.axis_index("x"), lax.axis_index("y"), lax.axis_index("c")


class Ride:
    def __init__(self, ins, out_shapes, start, finish):
        self.ins, self.out_shapes, self.start, self.finish = list(ins), list(out_shapes), start, finish
        n = len(self.ins)
        self.sems = [pltpu.SemaphoreType.DMA((n * N_PEERS,)), pltpu.SemaphoreType.DMA((n * N_PEERS,)),
                     pltpu.SemaphoreType.DMA((n,))]


def gather_ride(shards, kinds):
    n = len(shards)

    def setup(ins, outs, sems):
        send_sems, recv_sems, local_sems = sems
        x, y, c = _place()
        chips = [(1 - x, y), (x, 1 - y), (1 - x, 1 - y)]

        def copy(a, k, block, to, src=None):
            slot = _chunk(outs[a], kinds[a], _dev_index(block))
            return pltpu.make_async_remote_copy(
                src_ref=slot if src is None else src, dst_ref=slot,
                send_sem=send_sems.at[a * N_PEERS + k], recv_sem=recv_sems.at[a * N_PEERS + k],
                device_id=to, device_id_type=MESH)

        def mine(a):
            return pltpu.make_async_copy(ins[a], _chunk(outs[a], kinds[a], _dev_index((x, y, c))), local_sems.at[a])

        def first(a):
            return [copy(a, 0, (x, y, c), (x, y, 1 - c), src=ins[a])] + [
                copy(a, 1 + j, (x, y, c), (*chip, c), src=ins[a]) for j, chip in enumerate(chips)]

        return (x, y, c), chips, copy, mine, first

    def start(ins, outs, sems):
        _, _, _, mine, first = setup(ins, outs, sems)
        for a in range(n):
            mine(a).start()
            for cp in first(a):
                cp.start()

    def finish(ins, outs, sems):
        (x, y, c), chips, copy, mine, first = setup(ins, outs, sems)
        me, sibling = (x, y, c), (x, y, 1 - c)
        for a in range(n):
            for j, chip in enumerate(chips):
                copy(a, 1 + j, (*chip, c), me).wait_recv()
                copy(a, 4 + j, (*chip, c), sibling).start()
        for a in range(n):
            copy(a, 0, sibling, me).wait_recv()
            for j, chip in enumerate(chips):
                copy(a, 4 + j, (*chip, 1 - c), me).wait_recv()
        for a in range(n):
            for cp in first(a):
                cp.wait_send()
            for j, chip in enumerate(chips):
                copy(a, 4 + j, (*chip, c), sibling).wait_send()
        for a in range(n):
            mine(a).wait()

    shapes = [(N_DEV, *s.shape) if kind == "lead" else (s.shape[0], N_DEV * s.shape[1]) for s, kind in zip(shards, kinds)]
    return Ride(shards, [jax.ShapeDtypeStruct(shape, s.dtype) for shape, s in zip(shapes, shards)], start, finish)


def _chunk(ref, kind, j):
    if kind == "lead":
        return ref.at[j]
    if kind == "rows":
        r = ref.shape[0] // N_DEV
        return ref.at[pl.ds(j * r, r)]
    if kind == "mid":
        r = ref.shape[1] // N_DEV
        return ref.at[:, pl.ds(j * r, r), :]
    if kind == "cols":
        c = ref.shape[1] // N_DEV
        return ref.at[:, pl.ds(j * c, c)]
    return ref


def _chunk_shape(shape, kind):
    if kind == "lead":
        return tuple(shape[1:])
    if kind == "rows":
        return (shape[0] // N_DEV, *shape[1:])
    if kind == "mid":
        return (shape[0], shape[1] // N_DEV, shape[2])
    if kind == "cols":
        return (shape[0], shape[1] // N_DEV)
    return tuple(shape)


def scatter_ride(parts):
    n = len(parts)
    kinds = [k for _, k in parts]

    def setup(ins, outs, sems):
        send_sems, recv_sems, local_sems = sems
        x, y, c = _place()
        me = _dev_index((x, y, c))
        peers = []
        for k in range(1, N_DEV):
            kx, ky, kc = (k >> 2) & 1, (k >> 1) & 1, k & 1
            peers.append((1 - x if kx else x, 1 - y if ky else y, 1 - c if kc else c))

        def copy(a, k, peer):
            return pltpu.make_async_remote_copy(
                src_ref=_chunk(ins[a], kinds[a], _dev_index(peer)), dst_ref=outs[a].at[me],
                send_sem=send_sems.at[a * N_PEERS + k], recv_sem=recv_sems.at[a * N_PEERS + k],
                device_id=peer, device_id_type=MESH)

        def arrival(a, k, peer):
            slot = outs[a].at[_dev_index(peer)]
            return pltpu.make_async_remote_copy(
                src_ref=slot, dst_ref=slot,
                send_sem=send_sems.at[a * N_PEERS + k], recv_sem=recv_sems.at[a * N_PEERS + k],
                device_id=peer, device_id_type=MESH)

        def mine(a):
            return pltpu.make_async_copy(_chunk(ins[a], kinds[a], me), outs[a].at[me], local_sems.at[a])

        return peers, copy, arrival, mine

    def start(ins, outs, sems):
        peers, copy, _, mine = setup(ins, outs, sems)
        for a in range(n):
            mine(a).start()
            for k, peer in enumerate(peers):
                copy(a, k, peer).start()

    def finish(ins, outs, sems):
        peers, copy, arrival, mine = setup(ins, outs, sems)
        for a in range(n):
            for k, peer in enumerate(peers):
                arrival(a, k, peer).wait_recv()
        for a in range(n):
            for k, peer in enumerate(peers):
                copy(a, k, peer).wait_send()
            mine(a).wait()

    shapes = [jax.ShapeDtypeStruct((N_DEV, *_chunk_shape(arr.shape, kind)), arr.dtype) for arr, kind in parts]
    return Ride([arr for arr, _ in parts], shapes, start, finish)


HBM = pl.BlockSpec(memory_space=pltpu.HBM)
SEM = pl.BlockSpec(memory_space=pltpu.SEMAPHORE)
DATAFLOW = pltpu.SideEffectType.DATAFLOW_SIDE_EFFECTING
TOKEN = (8, 128)


def _scatter_copies(kinds, ins, lands, send_sems, recv_sems):
    x, y, c = _place()
    me = _dev_index((x, y, c))
    sends, arrivals = [], []
    for a, kind in enumerate(kinds):
        for k in range(1, N_DEV):
            kx, ky, kc = (k >> 2) & 1, (k >> 1) & 1, k & 1
            peer = (1 - x if kx else x, 1 - y if ky else y, 1 - c if kc else c)
            sem = a * N_PEERS + k - 1
            sends.append(pltpu.make_async_remote_copy(
                src_ref=_chunk(ins[a], kind, _dev_index(peer)), dst_ref=lands[a].at[me],
                send_sem=send_sems.at[sem], recv_sem=recv_sems.at[sem], device_id=peer, device_id_type=MESH))
            slot = lands[a].at[_dev_index(peer)]
            arrivals.append(pltpu.make_async_remote_copy(
                src_ref=slot, dst_ref=slot, send_sem=send_sems.at[sem], recv_sem=recv_sems.at[sem],
                device_id=peer, device_id_type=MESH))
    return me, sends, arrivals


def own_blocks(parts, name):
    n = len(parts)
    x, y, c = _place()
    me = jnp.reshape(_dev_index((x, y, c)), (1,)).astype(jnp.int32)

    def block_of(shape, kind):
        blk = _chunk_shape(shape, kind)
        if kind == "lead":
            return pl.BlockSpec((1, *blk), lambda i, me_ref: (me_ref[0], *[0] * len(blk)))
        if kind == "rows":
            return pl.BlockSpec(blk, lambda i, me_ref: (me_ref[0], *[0] * (len(blk) - 1)))
        if kind == "cols":
            return pl.BlockSpec(blk, lambda i, me_ref: (0, me_ref[0]))
        return pl.BlockSpec(blk, lambda i, me_ref: (0,) * len(blk))

    def body(me_ref, *refs):
        for (arr, kind), src, dst in zip(parts, refs[:n], refs[n:]):
            dst[0] = src[0] if kind == "lead" else src[...]

    shapes = [(N_DEV, *_chunk_shape(a.shape, k)) for a, k in parts]
    return pl.pallas_call(
        body, name=name,
        grid_spec=pltpu.PrefetchScalarGridSpec(
            num_scalar_prefetch=1, grid=(1,),
            in_specs=[block_of(a.shape, k) for a, k in parts],
            out_specs=[pl.BlockSpec((1, *s[1:]), lambda i, me_ref, r=len(s) - 1: (me_ref[0], *[0] * r)) for s in shapes]),
        out_shape=[jax.ShapeDtypeStruct(s, a.dtype) for s, (a, _) in zip(shapes, parts)],
        compiler_params=pltpu.CompilerParams(vmem_limit_bytes=VMEM_LIMIT),
    )(me, *[a for a, _ in parts])


def scatter_start(parts, name):
    n = len(parts)
    kinds = [k for _, k in parts]
    arrays = [pltpu.with_memory_space_constraint(a, pltpu.HBM) for a, _ in parts]
    zones = [pltpu.with_memory_space_constraint(z, pltpu.HBM) for z in own_blocks(parts, name + "_own")]

    def body(*refs):
        ins, lands = refs[:n], refs[n:2 * n]
        send_sems, recv_sems = refs[2 * n], refs[2 * n + 1]
        token = refs[4 * n + 2]
        _, sends, _ = _scatter_copies(kinds, ins, lands, send_sems, recv_sems)
        for cp in sends:
            cp.start()
        token[...] = jnp.zeros_like(token)

    outs = pl.pallas_call(
        body, name=name,
        out_shape=(pltpu.SemaphoreType.DMA((n * N_PEERS,)), pltpu.SemaphoreType.DMA((n * N_PEERS,)),
                   *[pltpu.HBM(a.shape, a.dtype) for a in arrays], *[pltpu.HBM(z.shape, z.dtype) for z in zones],
                   jax.ShapeDtypeStruct(TOKEN, F32)),
        in_specs=[HBM] * (2 * n),
        out_specs=(SEM, SEM, *[HBM] * (2 * n), pl.BlockSpec(memory_space=pltpu.VMEM)),
        input_output_aliases={i: 2 + i for i in range(2 * n)},
        compiler_params=pltpu.CompilerParams(has_side_effects=DATAFLOW),
    )(*arrays, *zones)
    return kinds, outs[0], outs[1], outs[2:2 + n], outs[2 + n:2 + 2 * n], outs[2 + 2 * n]


def scatter_wait(started, after, name):
    kinds, send_sems, recv_sems, arrays, zones, _ = started
    n = len(kinds)

    def body(*refs):
        ins, lands = refs[:n], refs[n:2 * n]
        _, sends, arrivals = _scatter_copies(kinds, ins, lands, refs[2 * n], refs[2 * n + 1])
        for cp in sends:
            cp.wait_send()
        for cp in arrivals:
            cp.wait_recv()

    outs = pl.pallas_call(
        body, name=name,
        out_shape=(*[pltpu.HBM(a.shape, a.dtype) for a in arrays], *[pltpu.HBM(z.shape, z.dtype) for z in zones]),
        in_specs=[HBM] * (2 * n) + [SEM, SEM] + [ANY] * len(after),
        out_specs=[HBM] * (2 * n),
        input_output_aliases={i: i for i in range(2 * n)},
        compiler_params=pltpu.CompilerParams(has_side_effects=DATAFLOW),
    )(*arrays, *zones, send_sems, recv_sems, *after)
    return outs[n:]


def run_ride(ride, name):
    n_in, n_out = len(ride.ins), len(ride.out_shapes)

    def body(*refs):
        ins, outs, sems = refs[:n_in], refs[n_in:n_in + n_out], refs[n_in + n_out:]
        ride.start(ins, outs, sems)
        ride.finish(ins, outs, sems)

    return pl.pallas_call(
        body, name=name, in_specs=[ANY] * n_in, out_specs=[ANY] * n_out, out_shape=ride.out_shapes,
        scratch_shapes=ride.sems,
    )(*ride.ins)


def _call(body, *, name, grid, in_specs, out_specs, out_shape, args, scratch_shapes=(), ride=None):
    params = pltpu.CompilerParams(dimension_semantics=("arbitrary",) * len(grid), vmem_limit_bytes=VMEM_LIMIT)
    if ride is None:
        outs = pl.pallas_call(body, name=name, grid=grid, in_specs=in_specs, out_specs=out_specs, out_shape=out_shape,
                              scratch_shapes=list(scratch_shapes), compiler_params=params)(*args)
        return outs, []
    n_in, n_out, n_scr = len(in_specs), len(out_specs), len(scratch_shapes)
    r_in, r_out = len(ride.ins), len(ride.out_shapes)

    def hosted(*refs):
        ins, refs = refs[:n_in], refs[n_in:]
        rins, refs = refs[:r_in], refs[r_in:]
        outs, refs = refs[:n_out], refs[n_out:]
        routs, refs = refs[:r_out], refs[r_out:]
        scratch, sems = refs[:n_scr], refs[n_scr:]
        step, n_steps = pl.program_id(0), grid[0]
        for d in range(1, len(grid)):
            step, n_steps = step * grid[d] + pl.program_id(d), n_steps * grid[d]

        @pl.when(step == 0)
        def _():
            ride.start(rins, routs, sems)

        body(*ins, *outs, *scratch)

        @pl.when(step == n_steps - 1)
        def _():
            ride.finish(rins, routs, sems)

    outs = pl.pallas_call(
        hosted, name=name + "_ride", grid=grid,
        in_specs=list(in_specs) + [ANY] * r_in, out_specs=list(out_specs) + [ANY] * r_out,
        out_shape=list(out_shape) + ride.out_shapes,
        scratch_shapes=list(scratch_shapes) + ride.sems, compiler_params=params,
    )(*args, *ride.ins)
    return outs[:n_out], outs[n_out:]


def ffn_fwd(h, gain, wgu, wd, ride=None):
    t = h.shape[0]
    tf = min(TF, t)

    def body(h_ref, g_ref, wgu_ref, wd_ref, hn_ref, z_ref, u_ref, acc):
        k = pl.program_id(1)

        @pl.when(k == 0)
        def _():
            u_ref[...] = _rms_fwd(h_ref[...], g_ref[...]).astype(BF16)
            acc[...] = jnp.zeros_like(acc)

        u = u_ref[...]
        g = _dot_nt(u, wgu_ref[0, 0])
        up = _dot_nt(u, wgu_ref[1, 0])
        z_ref[0, 0] = g.astype(BF16)
        z_ref[1, 0] = up.astype(BF16)
        a = g * jax.nn.sigmoid(g) * up
        acc[...] += _dot(a.astype(BF16), wd_ref[0])

        @pl.when(k == 3)
        def _():
            hn_ref[...] = h_ref[...] + acc[...]

    return _call(
        body, name="ffn_fwd", grid=(t // tf, 4), ride=ride,
        in_specs=[pl.BlockSpec((tf, D), lambda i, k: (i, 0)),
                  pl.BlockSpec((1, D), lambda i, k: (0, 0)),
                  pl.BlockSpec((2, 1, FC, D), lambda i, k: (0, k, 0, 0)),
                  pl.BlockSpec((1, FC, D), lambda i, k: (k, 0, 0))],
        out_specs=[pl.BlockSpec((tf, D), lambda i, k: (i, 0)),
                   pl.BlockSpec((2, 1, tf, FC), lambda i, k: (0, k, i, 0)),
                   pl.BlockSpec((tf, D), lambda i, k: (i, 0))],
        out_shape=[jax.ShapeDtypeStruct((t, D), F32),
                   jax.ShapeDtypeStruct((2, 4, t, FC), BF16),
                   jax.ShapeDtypeStruct((t, D), BF16)],
        scratch_shapes=[pltpu.VMEM((tf, D), F32)],
        args=(h, gain, wgu, wd))


def ffn_bwd_x(dh, h, gain, z, wgu, wd, ride=None):
    t = h.shape[0]

    def body(dh_ref, h_ref, g_ref, z_ref, wgu_ref, wd_ref, dhp_ref, dz_ref, a_ref, dgain_ref, dhb, du):
        i, k = pl.program_id(0), pl.program_id(1)

        @pl.when(k == 0)
        def _():
            dhb[...] = dh_ref[...].astype(BF16)
            du[...] = jnp.zeros_like(du)

        @pl.when((k == 0) & (i == 0))
        def _():
            dgain_ref[...] = jnp.zeros_like(dgain_ref)

        da = _dot_nt(dhb[...], wd_ref[0])
        g = z_ref[0, 0].astype(F32)
        up = z_ref[1, 0].astype(F32)
        sg = jax.nn.sigmoid(g)
        silu = g * sg
        a_ref[0] = (silu * up).astype(BF16)
        dg = (da * up * (sg * (1.0 + g * (1.0 - sg)))).astype(BF16)
        dup = (da * silu).astype(BF16)
        dz_ref[0, 0] = dg
        dz_ref[1, 0] = dup
        for n in range(2):
            cols = slice(n * (D // 2), (n + 1) * (D // 2))
            du[:, cols] += _dot(dg, wgu_ref[0, 0, :, cols]) + _dot(dup, wgu_ref[1, 0, :, cols])

        @pl.when(k == 3)
        def _():
            dx, dgain = _rms_bwd(h_ref[...], g_ref[...], du[...])
            dhp_ref[...] = dh_ref[...] + dx
            dgain_ref[...] += dgain

    return _call(
        body, name="ffn_bwd_x", grid=(t // TM, 4), ride=ride,
        in_specs=[pl.BlockSpec((TM, D), lambda i, k: (i, 0)),
                  pl.BlockSpec((TM, D), lambda i, k: (i, 0)),
                  pl.BlockSpec((1, D), lambda i, k: (0, 0)),
                  pl.BlockSpec((2, 1, TM, FC), lambda i, k: (0, k, i, 0)),
                  pl.BlockSpec((2, 1, FC, D), lambda i, k: (0, k, 0, 0)),
                  pl.BlockSpec((1, FC, D), lambda i, k: (k, 0, 0))],
        out_specs=[pl.BlockSpec((TM, D), lambda i, k: (i, 0)),
                   pl.BlockSpec((2, 1, TM, FC), lambda i, k: (0, k, i, 0)),
                   pl.BlockSpec((1, TM, FC), lambda i, k: (k, i, 0)),
                   pl.BlockSpec((1, D), lambda i, k: (0, 0)),
                   pl.BlockSpec((TM, D), lambda i, k: (i, 0))],
        out_shape=[jax.ShapeDtypeStruct((t, D), F32),
                   jax.ShapeDtypeStruct((2, 4, t, FC), BF16),
                   jax.ShapeDtypeStruct((4, t, FC), BF16),
                   jax.ShapeDtypeStruct((1, D), F32),
                   jax.ShapeDtypeStruct((t, D), BF16)],
        scratch_shapes=[pltpu.VMEM((TM, D), F32)],
        args=(dh, h, gain, z, wgu, wd))


TF = 1024
TW = 2048


def ffn_bwd_w(u, dz, a, dhb, ride=None):
    t = u.shape[0]
    tw = min(TW, t)
    steps = t // tw

    def body(u_ref, dz_ref, a_ref, dh_ref, dwgu_ref, dwd_ref, acc_gu, acc_d):
        j = pl.program_id(1)

        @pl.when(j == 0)
        def _():
            acc_gu[...] = jnp.zeros_like(acc_gu)
            acc_d[...] = jnp.zeros_like(acc_d)

        ub = u_ref[...]
        acc_gu[0] += _dot_tn(dz_ref[0, 0], ub)
        acc_gu[1] += _dot_tn(dz_ref[1, 0], ub)
        acc_d[...] += _dot_tn(a_ref[0], dh_ref[...])

        @pl.when(j == steps - 1)
        def _():
            dwgu_ref[:, 0] = acc_gu[...].astype(BF16)
            dwd_ref[0] = acc_d[...].astype(BF16)

    return _call(
        body, name="ffn_bwd_w", grid=(4, steps), ride=ride,
        in_specs=[pl.BlockSpec((tw, D), lambda k, j: (j, 0)),
                  pl.BlockSpec((2, 1, tw, FC), lambda k, j: (0, k, j, 0)),
                  pl.BlockSpec((1, tw, FC), lambda k, j: (k, j, 0)),
                  pl.BlockSpec((tw, D), lambda k, j: (j, 0))],
        out_specs=[pl.BlockSpec((2, 1, FC, D), lambda k, j: (0, k, 0, 0)),
                   pl.BlockSpec((1, FC, D), lambda k, j: (k, 0, 0))],
        out_shape=[jax.ShapeDtypeStruct((2, 4, FC, D), BF16),
                   jax.ShapeDtypeStruct((4, FC, D), BF16)],
        scratch_shapes=[pltpu.VMEM((2, FC, D), F32), pltpu.VMEM((FC, D), F32)],
        args=(u, dz, a, dhb))


def _prev_halo(i, tile=TM):
    return jnp.maximum(i * (tile // HALO) - 1, 0)


def _next_halo(i, t, tile=TM):
    return jnp.minimum((i + 1) * (tile // HALO), t // HALO - 1)


def rms_matmul(h, gain, w, bias, ride=None):
    t = h.shape[0]
    n = w.shape[1]

    def body(h_ref, g_ref, w_ref, b_ref, z_ref, u_ref):
        u = _rms_fwd(h_ref[...], g_ref[...]).astype(BF16)
        u_ref[...] = u
        z_ref[...] = (_dot(u, w_ref[...]) + b_ref[...]).astype(BF16)

    return _call(
        body, name=f"rms_matmul_{n}", grid=(t // TM,), ride=ride,
        in_specs=[pl.BlockSpec((TM, D), lambda i: (i, 0)),
                  pl.BlockSpec((1, D), lambda i: (0, 0)),
                  pl.BlockSpec((D, n), lambda i: (0, 0)),
                  pl.BlockSpec((1, n), lambda i: (0, 0))],
        out_specs=[pl.BlockSpec((TM, n), lambda i: (i, 0)),
                   pl.BlockSpec((TM, D), lambda i: (i, 0))],
        out_shape=[jax.ShapeDtypeStruct((t, n), BF16),
                   jax.ShapeDtypeStruct((t, D), BF16)],
        args=(h, gain, w, bias))


def in_proj_bwd_x(dz, w, h, gain, dh, ride=None):
    t = h.shape[0]
    n = w.shape[1]

    def body(dz_ref, w_ref, h_ref, g_ref, dh_ref, dhp_ref, dgain_ref, dbias_ref):
        @pl.when(pl.program_id(0) == 0)
        def _():
            dgain_ref[...] = jnp.zeros_like(dgain_ref)
            dbias_ref[...] = jnp.zeros_like(dbias_ref)

        du = _dot_nt(dz_ref[...], w_ref[...])
        dx, dgain = _rms_bwd(h_ref[...], g_ref[...], du)
        dhp_ref[...] = dh_ref[...] + dx
        dgain_ref[...] += dgain
        dbias_ref[...] += jnp.sum(dz_ref[...].astype(F32), axis=0, keepdims=True)

    return _call(
        body, name=f"in_proj_bwd_x_{n}", grid=(t // TM,), ride=ride,
        in_specs=[pl.BlockSpec((TM, n), lambda i: (i, 0)),
                  pl.BlockSpec((D, n), lambda i: (0, 0)),
                  pl.BlockSpec((TM, D), lambda i: (i, 0)),
                  pl.BlockSpec((1, D), lambda i: (0, 0)),
                  pl.BlockSpec((TM, D), lambda i: (i, 0))],
        out_specs=[pl.BlockSpec((TM, D), lambda i: (i, 0)),
                   pl.BlockSpec((1, D), lambda i: (0, 0)),
                   pl.BlockSpec((1, n), lambda i: (0, 0))],
        out_shape=[jax.ShapeDtypeStruct((t, D), F32),
                   jax.ShapeDtypeStruct((1, D), F32),
                   jax.ShapeDtypeStruct((1, n), F32)],
        args=(dz, w, h, gain, dh))


def in_proj_bwd_w(u, dz, ride=None):
    t = u.shape[0]
    n = dz.shape[1]
    steps = t // TM

    def body(u_ref, dz_ref, dw_ref, acc):
        s = pl.program_id(0)

        @pl.when(s == 0)
        def _():
            acc[...] = jnp.zeros_like(acc)

        acc[...] += _dot_tn(u_ref[...], dz_ref[...])

        @pl.when(s == steps - 1)
        def _():
            dw_ref[...] = acc[...].astype(BF16)

    return _call(
        body, name=f"in_proj_bwd_w_{n}", grid=(steps,), ride=ride,
        in_specs=[pl.BlockSpec((TM, D), lambda s: (s, 0)),
                  pl.BlockSpec((TM, n), lambda s: (s, 0))],
        out_specs=[pl.BlockSpec((D, n), lambda s: (0, 0))],
        out_shape=[jax.ShapeDtypeStruct((D, n), BF16)],
        scratch_shapes=[pltpu.VMEM((D, n), F32)],
        args=(u, dz))


A_TAPS = 3


def a_mix_fwd(z, h, conv, wout, ride=None):
    t = h.shape[0]

    def body(z_ref, zp_ref, h_ref, cw_ref, wo_ref, hn_ref, pad):
        i = pl.program_id(0)
        ph = zp_ref[:, D:2 * D].astype(F32) * zp_ref[:, 2 * D:].astype(F32)
        pad[0:HALO, :] = jnp.where(i == 0, 0.0, ph)
        pad[HALO:, :] = z_ref[:, D:2 * D].astype(F32) * z_ref[:, 2 * D:].astype(F32)
        q = jnp.zeros((TM, D), F32)
        for k in range(A_TAPS):
            off = HALO - (A_TAPS - 1) + k
            q += cw_ref[k:k + 1, :] * pad[off:off + TM, :]
        r = z_ref[:, 0:D].astype(F32) * q
        hn_ref[...] = h_ref[...] + _dot(r.astype(BF16), wo_ref[...])

    return _call(
        body, name="a_mix_fwd", grid=(t // TM,), ride=ride,
        in_specs=[pl.BlockSpec((TM, 3 * D), lambda i: (i, 0)),
                  pl.BlockSpec((HALO, 3 * D), lambda i: (_prev_halo(i), 0)),
                  pl.BlockSpec((TM, D), lambda i: (i, 0)),
                  pl.BlockSpec((A_TAPS, D), lambda i: (0, 0)),
                  pl.BlockSpec((D, D), lambda i: (0, 0))],
        out_specs=[pl.BlockSpec((TM, D), lambda i: (i, 0))],
        out_shape=[jax.ShapeDtypeStruct((t, D), F32)],
        scratch_shapes=[pltpu.VMEM((HALO + TM, D), F32)],
        args=(z, z, h, conv, wout))


def a_mix_bwd(dh, z, conv, wout, ride=None):
    t = dh.shape[0]
    steps = t // TM

    def body(dh_ref, dhn_ref, z_ref, zp_ref, zn_ref, cw_ref, wo_ref, dz_ref, dwo_ref, dcw_ref, pad, dqpad, dwo):
        i = pl.program_id(0)
        last = i == steps - 1

        @pl.when(i == 0)
        def _():
            dwo[...] = jnp.zeros_like(dwo)
            dcw_ref[...] = jnp.zeros_like(dcw_ref)

        ph = zp_ref[:, D:2 * D].astype(F32) * zp_ref[:, 2 * D:].astype(F32)
        pad[0:HALO, :] = jnp.where(i == 0, 0.0, ph)
        c = z_ref[:, D:2 * D].astype(F32)
        v = z_ref[:, 2 * D:].astype(F32)
        pad[HALO:, :] = c * v
        q = jnp.zeros((TM, D), F32)
        for k in range(A_TAPS):
            off = HALO - (A_TAPS - 1) + k
            q += cw_ref[k:k + 1, :] * pad[off:off + TM, :]
        b = z_ref[:, 0:D].astype(F32)
        dhb = dh_ref[...].astype(BF16)
        dwo[...] += _dot_tn((b * q).astype(BF16), dhb)
        dr = _dot_nt(dhb, wo_ref[...])
        dz_ref[:, 0:D] = (dr * q).astype(BF16)
        dq = dr * b
        drn = _dot_nt(dhn_ref[...].astype(BF16), wo_ref[...])
        dqpad[0:TM, :] = dq
        dqpad[TM:, :] = jnp.where(last, 0.0, drn * zn_ref[:, 0:D].astype(F32))
        dp = jnp.zeros((TM, D), F32)
        for k in range(A_TAPS):
            off = A_TAPS - 1 - k
            dp += cw_ref[k:k + 1, :] * dqpad[off:off + TM, :]
            poff = HALO - (A_TAPS - 1) + k
            dcw_ref[k:k + 1, :] += jnp.sum(dq * pad[poff:poff + TM, :], axis=0, keepdims=True)
        dz_ref[:, D:2 * D] = (dp * v).astype(BF16)
        dz_ref[:, 2 * D:] = (dp * c).astype(BF16)

        @pl.when(last)
        def _():
            dwo_ref[...] = dwo[...].astype(BF16)

    return _call(
        body, name="a_mix_bwd", grid=(steps,), ride=ride,
        in_specs=[pl.BlockSpec((TM, D), lambda i: (i, 0)),
                  pl.BlockSpec((HALO, D), lambda i: (_next_halo(i, t), 0)),
                  pl.BlockSpec((TM, 3 * D), lambda i: (i, 0)),
                  pl.BlockSpec((HALO, 3 * D), lambda i: (_prev_halo(i), 0)),
                  pl.BlockSpec((HALO, 3 * D), lambda i: (_next_halo(i, t), 0)),
                  pl.BlockSpec((A_TAPS, D), lambda i: (0, 0)),
                  pl.BlockSpec((D, D), lambda i: (0, 0))],
        out_specs=[pl.BlockSpec((TM, 3 * D), lambda i: (i, 0)),
                   pl.BlockSpec((D, D), lambda i: (0, 0)),
                   pl.BlockSpec((A_TAPS, D), lambda i: (0, 0))],
        out_shape=[jax.ShapeDtypeStruct((t, 3 * D), BF16),
                   jax.ShapeDtypeStruct((D, D), BF16),
                   jax.ShapeDtypeStruct((A_TAPS, D), F32)],
        scratch_shapes=[pltpu.VMEM((HALO + TM, D), F32), pltpu.VMEM((TM + HALO, D), F32), pltpu.VMEM((D, D), F32)],
        args=(dh, dh, z, z, z, conv, wout))


C_TAPS = 31


def _glu(zr):
    return zr[:, 0:D].astype(F32) * jax.nn.sigmoid(zr[:, D:].astype(F32))


def _ln_silu(h2, lg, lb):
    mu = jnp.mean(h2, axis=-1, keepdims=True)
    xc = h2 - mu
    rstd = lax.rsqrt(jnp.mean(xc * xc, axis=-1, keepdims=True) + LN_EPS)
    xn = xc * rstd
    h3 = xn * lg + lb
    s3 = jax.nn.sigmoid(h3)
    return xn, rstd, h3, s3


def _ln_silu_bwd(h2, lg, lb, dh4):
    xn, rstd, h3, s3 = _ln_silu(h2, lg, lb)
    dh3 = dh4 * (s3 * (1.0 + h3 * (1.0 - s3)))
    dxn = dh3 * lg
    dh2 = rstd * (dxn - jnp.mean(dxn, axis=-1, keepdims=True) - xn * jnp.mean(dxn * xn, axis=-1, keepdims=True))
    return dh2, dh3, xn, h3 * s3


TC = 256
RB = 64
LANES = 128
SHIFTS = 7


def _shifted_copies(src, sh, rows):
    for b in range(1, SHIFTS + 1):
        sh[b - 1, 0:rows, :] = src[b:b + rows, :]


def _window(src, sh, o, r0, lanes):
    a, b = divmod(o, 8)
    ref = src if b == 0 else sh.at[b - 1]
    return ref[8 * a + r0:8 * a + r0 + RB, lanes]


def c_mix_fwd(z, h, dw, bdw, lg, lb, w2, b2, ride=None):
    t = h.shape[0]

    def body(z_ref, zp_ref, h_ref, dw_ref, bdw_ref, lg_ref, lb_ref, w2_ref, b2_ref, hn_ref, h2_ref, pad, sh):
        i = pl.program_id(0)
        pad[0:HALO, :] = jnp.where(i == 0, 0.0, _glu(zp_ref))
        pad[HALO:, :] = _glu(z_ref)
        _shifted_copies(pad, sh, TC + 24)
        for l in range(D // LANES):
            lanes = slice(l * LANES, (l + 1) * LANES)
            for r0 in range(0, TC, RB):
                acc = jnp.zeros((RB, LANES), F32) + bdw_ref[:, lanes]
                for k in range(C_TAPS):
                    acc += dw_ref[k:k + 1, lanes] * _window(pad, sh, HALO - (C_TAPS - 1) + k, r0, lanes)
                h2_ref[r0:r0 + RB, lanes] = acc
        _, _, h3, s3 = _ln_silu(h2_ref[...], lg_ref[...], lb_ref[...])
        hn_ref[...] = h_ref[...] + _dot((h3 * s3).astype(BF16), w2_ref[...]) + b2_ref[...]

    vec = pl.BlockSpec((1, D), lambda i: (0, 0))
    return _call(
        body, name="c_mix_fwd", grid=(t // TC,), ride=ride,
        in_specs=[pl.BlockSpec((TC, 2 * D), lambda i: (i, 0)),
                  pl.BlockSpec((HALO, 2 * D), lambda i: (_prev_halo(i, TC), 0)),
                  pl.BlockSpec((TC, D), lambda i: (i, 0)),
                  pl.BlockSpec((C_TAPS, D), lambda i: (0, 0)),
                  vec, vec, vec,
                  pl.BlockSpec((D, D), lambda i: (0, 0)),
                  vec],
        out_specs=[pl.BlockSpec((TC, D), lambda i: (i, 0)),
                   pl.BlockSpec((TC, D), lambda i: (i, 0))],
        out_shape=[jax.ShapeDtypeStruct((t, D), F32),
                   jax.ShapeDtypeStruct((t, D), F32)],
        scratch_shapes=[pltpu.VMEM((HALO + TC, D), F32), pltpu.VMEM((SHIFTS, TC + 24, D), F32)],
        args=(z, z, h, dw, bdw, lg, lb, w2, b2))


def c_mix_bwd(dh, z, h2, dw, lg, lb, w2, ride=None):
    t = dh.shape[0]
    steps = t // TC

    def body(dh_ref, dhn_ref, z_ref, zp_ref, h2_ref, h2n_ref, dw_ref, lg_ref, lb_ref, w2_ref,
             dz_ref, dw2_ref, db2_ref, dlg_ref, dlb_ref, dbdw_ref, ddw_ref, pad, dpad, dw2, sh, dh1):
        i = pl.program_id(0)
        last = i == steps - 1

        @pl.when(i == 0)
        def _():
            for r in (dw2, db2_ref, dlg_ref, dlb_ref, dbdw_ref, ddw_ref):
                r[...] = jnp.zeros_like(r)

        lg, lb = lg_ref[...], lb_ref[...]
        dh = dh_ref[...]
        dhb = dh.astype(BF16)
        dh2, dh3, xn, h4 = _ln_silu_bwd(h2_ref[...], lg, lb, _dot_nt(dhb, w2_ref[...]))
        dw2[...] += _dot_tn(h4.astype(BF16), dhb)
        db2_ref[...] += jnp.sum(dh, axis=0, keepdims=True)
        dlg_ref[...] += jnp.sum(dh3 * xn, axis=0, keepdims=True)
        dlb_ref[...] += jnp.sum(dh3, axis=0, keepdims=True)
        dbdw_ref[...] += jnp.sum(dh2, axis=0, keepdims=True)
        dh2n, _, _, _ = _ln_silu_bwd(h2n_ref[...], lg, lb, _dot_nt(dhn_ref[...].astype(BF16), w2_ref[...]))
        dpad[0:TC, :] = dh2
        dpad[TC:, :] = jnp.where(last, 0.0, dh2n)
        _shifted_copies(dpad, sh, TC + 24)
        for l in range(D // LANES):
            lanes = slice(l * LANES, (l + 1) * LANES)
            for r0 in range(0, TC, RB):
                acc = jnp.zeros((RB, LANES), F32)
                for k in range(C_TAPS):
                    acc += dw_ref[k:k + 1, lanes] * _window(dpad, sh, C_TAPS - 1 - k, r0, lanes)
                dh1[r0:r0 + RB, lanes] = acc
        pad[0:HALO, :] = jnp.where(i == 0, 0.0, _glu(zp_ref))
        pad[HALO:, :] = _glu(z_ref)
        _shifted_copies(pad, sh, TC + 24)
        for l in range(D // LANES):
            lanes = slice(l * LANES, (l + 1) * LANES)
            accs = [jnp.zeros((8, LANES), F32) for _ in range(C_TAPS)]
            for r0 in range(0, TC, RB):
                d = dpad[r0:r0 + RB, lanes]
                for k in range(C_TAPS):
                    prod = d * _window(pad, sh, HALO - (C_TAPS - 1) + k, r0, lanes)
                    accs[k] += jnp.sum(prod.reshape(RB // 8, 8, LANES), axis=0)
            for k in range(C_TAPS):
                ddw_ref[k:k + 1, lanes] += jnp.sum(accs[k], axis=0, keepdims=True)
        a = z_ref[:, 0:D].astype(F32)
        sg = jax.nn.sigmoid(z_ref[:, D:].astype(F32))
        d1 = dh1[...]
        dz_ref[:, 0:D] = (d1 * sg).astype(BF16)
        dz_ref[:, D:] = (d1 * a * sg * (1.0 - sg)).astype(BF16)

        @pl.when(last)
        def _():
            dw2_ref[...] = dw2[...].astype(BF16)

    vec = pl.BlockSpec((1, D), lambda i: (0, 0))
    return _call(
        body, name="c_mix_bwd", grid=(steps,), ride=ride,
        in_specs=[pl.BlockSpec((TC, D), lambda i: (i, 0)),
                  pl.BlockSpec((HALO, D), lambda i: (_next_halo(i, t, TC), 0)),
                  pl.BlockSpec((TC, 2 * D), lambda i: (i, 0)),
                  pl.BlockSpec((HALO, 2 * D), lambda i: (_prev_halo(i, TC), 0)),
                  pl.BlockSpec((TC, D), lambda i: (i, 0)),
                  pl.BlockSpec((HALO, D), lambda i: (_next_halo(i, t, TC), 0)),
                  pl.BlockSpec((C_TAPS, D), lambda i: (0, 0)),
                  vec, vec,
                  pl.BlockSpec((D, D), lambda i: (0, 0))],
        out_specs=[pl.BlockSpec((TC, 2 * D), lambda i: (i, 0)),
                   pl.BlockSpec((D, D), lambda i: (0, 0)),
                   vec, vec, vec, vec,
                   pl.BlockSpec((C_TAPS, D), lambda i: (0, 0))],
        out_shape=[jax.ShapeDtypeStruct((t, 2 * D), BF16),
                   jax.ShapeDtypeStruct((D, D), BF16)]
                  + [jax.ShapeDtypeStruct((1, D), F32)] * 4
                  + [jax.ShapeDtypeStruct((C_TAPS, D), F32)],
        scratch_shapes=[pltpu.VMEM((HALO + TC, D), F32), pltpu.VMEM((TC + HALO, D), F32), pltpu.VMEM((D, D), F32),
                        pltpu.VMEM((SHIFTS, TC + 24, D), F32), pltpu.VMEM((TC, D), F32)],
        args=(dh, dh, z, z, h2, h2, dw, lg, lb, w2))


POOL_WINDOWS = (2, 4, 8, 16)
GW = D // len(POOL_WINDOWS)


def _pool_mixed(pad, g, w, inv_cnt):
    cols = slice(g * GW, (g + 1) * GW)
    s = pad[HALO:HALO + TM, cols]
    u = s
    for j in range(1, w):
        s = s + pad[HALO - j:HALO - j + TM, cols]
    return s * inv_cnt - u


def _inv_cnt(i, w):
    row = i * TM + lax.broadcasted_iota(jnp.int32, (TM, 1), 0)
    return 1.0 / jnp.minimum(row + 1, w).astype(F32)


def b_mix_fwd(h, gain, wg, scale, ride=None):
    t = h.shape[0]

    def body(h_ref, hp_ref, g_ref, wg_ref, sc_ref, hn_ref, pad):
        i = pl.program_id(0)
        gain = g_ref[...]
        pad[0:HALO, :] = jnp.where(i == 0, 0.0, _rms_fwd(hp_ref[...], gain))
        pad[HALO:, :] = _rms_fwd(h_ref[...], gain)
        for g, w in enumerate(POOL_WINDOWS):
            cols = slice(g * GW, (g + 1) * GW)
            mixed = _pool_mixed(pad, g, w, _inv_cnt(i, w))
            y = _dot(mixed.astype(BF16), wg_ref[g])
            hn_ref[:, cols] = h_ref[:, cols] + y * sc_ref[:, cols]

    return _call(
        body, name="b_mix_fwd", grid=(t // TM,), ride=ride,
        in_specs=[pl.BlockSpec((TM, D), lambda i: (i, 0)),
                  pl.BlockSpec((HALO, D), lambda i: (_prev_halo(i), 0)),
                  pl.BlockSpec((1, D), lambda i: (0, 0)),
                  pl.BlockSpec((4, GW, GW), lambda i: (0, 0, 0)),
                  pl.BlockSpec((1, D), lambda i: (0, 0))],
        out_specs=[pl.BlockSpec((TM, D), lambda i: (i, 0))],
        out_shape=[jax.ShapeDtypeStruct((t, D), F32)],
        scratch_shapes=[pltpu.VMEM((HALO + TM, D), F32)],
        args=(h, h, gain, wg, scale))


def b_mix_bwd(dh, h, gain, wg, scale, ride=None):
    t = h.shape[0]
    steps = t // TM

    def body(dh_ref, dhn_ref, h_ref, hp_ref, g_ref, wg_ref, sc_ref, dhp_ref, dgain_ref, dwg_ref, dsc_ref, pad, dpad, du):
        i = pl.program_id(0)
        last = i == steps - 1

        @pl.when(i == 0)
        def _():
            for r in (dgain_ref, dwg_ref, dsc_ref):
                r[...] = jnp.zeros_like(r)

        gain = g_ref[...]
        pad[0:HALO, :] = jnp.where(i == 0, 0.0, _rms_fwd(hp_ref[...], gain))
        pad[HALO:, :] = _rms_fwd(h_ref[...], gain)
        for g, w in enumerate(POOL_WINDOWS):
            cols = slice(g * GW, (g + 1) * GW)
            inv_cnt = _inv_cnt(i, w)
            mixed = _pool_mixed(pad, g, w, inv_cnt).astype(BF16)
            dh = dh_ref[:, cols]
            dsc_ref[:, cols] += jnp.sum(dh * _dot(mixed, wg_ref[g]), axis=0, keepdims=True)
            dy = (dh * sc_ref[:, cols]).astype(BF16)
            dwg_ref[g] += _dot_tn(mixed, dy)
            dm = _dot_nt(dy, wg_ref[g])
            dmn = _dot_nt((dhn_ref[:, cols] * sc_ref[:, cols]).astype(BF16), wg_ref[g])
            dpad[0:TM, cols] = dm * inv_cnt
            dpad[TM:, cols] = jnp.where(last, 0.0, dmn * (1.0 / w))
            s = dpad[0:TM, cols]
            for j in range(1, w):
                s = s + dpad[j:j + TM, cols]
            du[:, cols] = s - dm
        dx, dgain = _rms_bwd(h_ref[...], gain, du[...])
        dhp_ref[...] = dh_ref[...] + dx
        dgain_ref[...] += dgain

    return _call(
        body, name="b_mix_bwd", grid=(steps,), ride=ride,
        in_specs=[pl.BlockSpec((TM, D), lambda i: (i, 0)),
                  pl.BlockSpec((HALO, D), lambda i: (_next_halo(i, t), 0)),
                  pl.BlockSpec((TM, D), lambda i: (i, 0)),
                  pl.BlockSpec((HALO, D), lambda i: (_prev_halo(i), 0)),
                  pl.BlockSpec((1, D), lambda i: (0, 0)),
                  pl.BlockSpec((4, GW, GW), lambda i: (0, 0, 0)),
                  pl.BlockSpec((1, D), lambda i: (0, 0))],
        out_specs=[pl.BlockSpec((TM, D), lambda i: (i, 0)),
                   pl.BlockSpec((1, D), lambda i: (0, 0)),
                   pl.BlockSpec((4, GW, GW), lambda i: (0, 0, 0)),
                   pl.BlockSpec((1, D), lambda i: (0, 0))],
        out_shape=[jax.ShapeDtypeStruct((t, D), F32),
                   jax.ShapeDtypeStruct((1, D), F32),
                   jax.ShapeDtypeStruct((4, GW, GW), F32),
                   jax.ShapeDtypeStruct((1, D), F32)],
        scratch_shapes=[pltpu.VMEM((HALO + TM, D), F32), pltpu.VMEM((TM + HALO, D), F32), pltpu.VMEM((TM, D), F32)],
        args=(dh, dh, h, h, gain, wg, scale))


LOSS_LANES = 128


def loss_head(h, gain, target):
    t = h.shape[0]

    def body(h_ref, g_ref, tg_ref, loss_ref, dh_ref, dgain_ref):
        @pl.when(pl.program_id(0) == 0)
        def _():
            loss_ref[...] = jnp.zeros_like(loss_ref)
            dgain_ref[...] = jnp.zeros_like(dgain_ref)

        x, gain = h_ref[...], g_ref[...]
        err = _rms_fwd(x, gain) - tg_ref[...]
        per_row = jnp.mean(err * err, axis=-1, keepdims=True)
        loss_ref[...] += jnp.broadcast_to(0.5 * jnp.sum(per_row, axis=0, keepdims=True), (1, LOSS_LANES))
        dx, dgain = _rms_bwd(x, gain, err * (1.0 / D))
        dh_ref[...] = dx
        dgain_ref[...] += dgain

    outs, _ = _call(
        body, name="loss_head", grid=(t // TM,),
        in_specs=[pl.BlockSpec((TM, D), lambda i: (i, 0)),
                  pl.BlockSpec((1, D), lambda i: (0, 0)),
                  pl.BlockSpec((TM, D), lambda i: (i, 0))],
        out_specs=[pl.BlockSpec((1, LOSS_LANES), lambda i: (0, 0)),
                   pl.BlockSpec((TM, D), lambda i: (i, 0)),
                   pl.BlockSpec((1, D), lambda i: (0, 0))],
        out_shape=[jax.ShapeDtypeStruct((1, LOSS_LANES), F32),
                   jax.ShapeDtypeStruct((t, D), F32),
                   jax.ShapeDtypeStruct((1, D), F32)],
        args=(h, gain, target))
    return outs


ADAM_LR = 0.001
ADAM_B1 = 0.9
ADAM_B2 = 0.999
ADAM_EPS = 1e-08
ADAM_WD = 0.01
ADAM_STEP = 10
ADAM_VMEM = 40 * 1024 * 1024


def cast_all(arrays):
    def body(*refs):
        for src, dst in zip(refs[:len(arrays)], refs[len(arrays):]):
            dst[...] = src[...].astype(BF16)

    return pl.pallas_call(
        body, name="cast_all", out_shape=[jax.ShapeDtypeStruct(a.shape, BF16) for a in arrays],
        compiler_params=pltpu.CompilerParams(vmem_limit_bytes=VMEM_LIMIT),
    )(*arrays)


def _adam_math(w, m, v, g):
    m = ADAM_B1 * m + (1.0 - ADAM_B1) * g
    v = ADAM_B2 * v + (1.0 - ADAM_B2) * (g * g)
    m_hat = m / (1.0 - ADAM_B1 ** ADAM_STEP)
    v_hat = v / (1.0 - ADAM_B2 ** ADAM_STEP)
    return -ADAM_LR * (m_hat / (jnp.sqrt(v_hat) + ADAM_EPS) + ADAM_WD * w), m, v


def adamw(ws, ms, vs, gps, rb, tokens=()):
    n = len(ws)
    r, c = ws[0].shape
    nb = r // rb

    def body(*refs):
        i = pl.program_id(0)
        outs = refs[4 * n + len(tokens):]
        for j in range(n):
            w_ref, m_ref, v_ref, gp_ref = (refs[q * n + j] for q in range(4))
            g_ref, d_ref, nm_ref, nv_ref = (outs[q * n + j] for q in range(4))

            @pl.when(i // nb == j)
            def _():
                g = gp_ref[0].astype(F32)
                for s in range(1, N_DEV):
                    g = g + gp_ref[s].astype(F32)
                g_ref[...] = g
                d_ref[...], nm_ref[...], nv_ref[...] = _adam_math(w_ref[...], m_ref[...], v_ref[...], g)

    def blk(j):
        return pl.BlockSpec((rb, c), lambda i: (jnp.clip(i - j * nb, 0, nb - 1), 0))

    def gblk(j):
        return pl.BlockSpec((N_DEV, rb, c), lambda i: (0, jnp.clip(i - j * nb, 0, nb - 1), 0))

    outs, _ = _call(
        body, name=f"adamw_{n}x{r}x{c}", grid=(n * nb,),
        in_specs=[blk(j) for _ in range(3) for j in range(n)] + [gblk(j) for j in range(n)] + [ANY] * len(tokens),
        out_specs=[blk(j) for _ in range(4) for j in range(n)],
        out_shape=[jax.ShapeDtypeStruct((r, c), F32)] * (4 * n),
        args=(*ws, *ms, *vs, *gps, *tokens))
    return outs[:n], outs[n:2 * n], outs[2 * n:3 * n], outs[3 * n:]


def adamw_vectors(ws, ms, vs, gparts):
    nv = len(ws)

    def body(*refs):
        w_refs, m_refs, v_refs = refs[:nv], refs[nv:2 * nv], refs[2 * nv:3 * nv]
        gp_ref = refs[3 * nv]
        outs = refs[3 * nv + 1:]
        g_refs, d_refs, nm_refs, nv_refs = outs[:nv], outs[nv:2 * nv], outs[2 * nv:3 * nv], outs[3 * nv:]
        row = 0
        for i in range(nv):
            for part in range(w_refs[i].shape[1] // D):
                cols = slice(part * D, (part + 1) * D)
                g = gp_ref[0, row:row + 1, :]
                for s in range(1, N_DEV):
                    g = g + gp_ref[s, row:row + 1, :]
                g_refs[i][:, cols] = g
                d_refs[i][:, cols], nm_refs[i][:, cols], nv_refs[i][:, cols] = _adam_math(
                    w_refs[i][:, cols], m_refs[i][:, cols], v_refs[i][:, cols], g)
                row += 1

    shapes = [jax.ShapeDtypeStruct(w.shape, F32) for w in ws]
    outs = pl.pallas_call(body, name="adamw_vectors", out_shape=shapes * 4)(*ws, *ms, *vs, gparts)
    return outs[:nv], outs[nv:2 * nv], outs[2 * nv:3 * nv], outs[3 * nv:]


WEIGHTS = ["ln1_0", "a0_w_in", "a0_conv", "a0_w_out", "ln2_0", "ffn0_w_gu", "ffn0_w_down",
           "ln1_1", "b1_w_grp", "b1_scale", "ln2_1", "ffn1_w_gu", "ffn1_w_down",
           "ln1_2", "c2_w_pw1", "c2_b_pw1", "c2_dw", "c2_b_dw", "c2_ln_g", "c2_ln_b", "c2_w_pw2", "c2_b_pw2",
           "ln2_2", "ffn2_w_gu", "ffn2_w_down",
           "ln1_3", "a3_w_in", "a3_conv", "a3_w_out", "ln2_3", "ffn3_w_gu", "ffn3_w_down", "ln_f"]
SHARDED = {"a0_w_in": ("cols", 256), "a0_conv": ("cols", A_TAPS), "a0_w_out": ("rows", 128),
           "ffn0_w_gu": ("lead", 176), "ffn0_w_down": ("rows", 176),
           "b1_w_grp": ("mid", 128),
           "ffn1_w_gu": ("lead", 176), "ffn1_w_down": ("rows", 176),
           "c2_w_pw1": ("cols", 256), "c2_dw": ("cols", C_TAPS), "c2_w_pw2": ("rows", 128),
           "ffn2_w_gu": ("lead", 176), "ffn2_w_down": ("rows", 176),
           "a3_w_in": ("cols", 256), "a3_conv": ("cols", A_TAPS), "a3_w_out": ("rows", 128),
           "ffn3_w_gu": ("lead", 176), "ffn3_w_down": ("rows", 176)}
IN_PROJ = ("a0_w_in", "c2_w_pw1", "a3_w_in")
REPL = [n for n in WEIGHTS if n not in SHARDED]
REPL_ROWS = 16
GATHER_PLAN = {"first": ["a0_w_in", "a0_w_out", "a0_conv"],
               "in0": ["ffn0_w_gu"], "mix0": ["ffn0_w_down"],
               "ffn0": ["b1_w_grp", "ffn1_w_gu", "ffn1_w_down"],
               "ffn1": ["c2_w_pw1", "c2_w_pw2", "c2_dw", "ffn2_w_gu"],
               "in2": ["ffn2_w_down"],
               "mix2": ["a3_w_in", "a3_w_out", "a3_conv"],
               "ffn2": ["ffn3_w_gu", "ffn3_w_down"]}
SCATTER_PLAN = {"mixb3": ["ffn3_w_down"], "inw3": ["a3_w_out", "a3_conv"],
                "ffnx2": ["ffn3_w_gu"], "ffnw2": ["a3_w_in"],
                "mixb2": ["ffn2_w_gu", "ffn2_w_down"], "inw2": ["c2_w_pw2", "c2_dw"],
                "ffnx1": ["c2_w_pw1"],
                "ffnx0": ["ffn1_w_gu"], "ffnw0": ["ffn1_w_down", "b1_w_grp"]}
LATE_FFN = ["ffn0_w_gu", "ffn0_w_down"]
LATE_MIX = ["a0_w_in", "a0_w_out", "a0_conv"]


def _step(p):
    vec = lambda n: p[n].reshape(1, -1)
    x, target = p["x"][0], p["loss_target"][0]

    names = list(SHARDED)
    stored = lambda n, a: a.T if n.endswith("w_gu") else a
    shard = dict(zip(names, cast_all([stored(n, p[n]) for n in names])))
    full = {}

    def gather(slot):
        names = GATHER_PLAN[slot]
        return gather_ride([shard[n] for n in names], ["cols" if n in IN_PROJ else "lead" for n in names])

    def landed(slot, outs):
        full.update(zip(GATHER_PLAN[slot], outs))

    def conv_full(n):
        k = full[n].shape[1]
        return full[n].transpose(1, 0, 2).reshape(k, D).astype(F32)

    def wgu(i):
        return full[f"ffn{i}_w_gu"].reshape(2, 4, FC, D)

    def wd(i):
        return full[f"ffn{i}_w_down"].reshape(4, FC, D)

    landed("first", run_ride(gather("first"), "gather_first"))
    no_bias = jnp.zeros((1, 3 * D), F32)
    h = [x]
    saved = {}
    conv, wout = {}, {}

    (z, u), got = rms_matmul(h[-1], vec("ln1_0"), full["a0_w_in"], no_bias, ride=gather("in0"))
    landed("in0", got)
    conv[0], wout[0] = conv_full("a0_conv"), full["a0_w_out"].reshape(D, D)
    (hm,), got = a_mix_fwd(z, h[-1], conv[0], wout[0], ride=gather("mix0"))
    landed("mix0", got)
    saved["mix0"] = (z, u)
    h.append(hm)
    (hn, zf, uf), got = ffn_fwd(hm, vec("ln2_0"), wgu(0), wd(0), ride=gather("ffn0"))
    landed("ffn0", got)
    saved["ffn0"] = (zf, uf)
    h.append(hn)

    wgrp = full["b1_w_grp"].transpose(1, 0, 2, 3).reshape(4, GW, GW)
    (hm,), _ = b_mix_fwd(h[-1], vec("ln1_1"), wgrp, vec("b1_scale"))
    h.append(hm)
    (hn, zf, uf), got = ffn_fwd(hm, vec("ln2_1"), wgu(1), wd(1), ride=gather("ffn1"))
    landed("ffn1", got)
    saved["ffn1"] = (zf, uf)
    h.append(hn)

    (z, u), got = rms_matmul(h[-1], vec("ln1_2"), full["c2_w_pw1"], vec("c2_b_pw1"), ride=gather("in2"))
    landed("in2", got)
    cdw, wpw2 = conv_full("c2_dw"), full["c2_w_pw2"].reshape(D, D)
    (hm, h2), got = c_mix_fwd(z, h[-1], cdw, vec("c2_b_dw"), vec("c2_ln_g"), vec("c2_ln_b"), wpw2, vec("c2_b_pw2"),
                              ride=gather("mix2"))
    landed("mix2", got)
    saved["mix2"] = (z, u, h2)
    h.append(hm)
    (hn, zf, uf), got = ffn_fwd(hm, vec("ln2_2"), wgu(2), wd(2), ride=gather("ffn2"))
    landed("ffn2", got)
    saved["ffn2"] = (zf, uf)
    h.append(hn)

    (z, u), _ = rms_matmul(h[-1], vec("ln1_3"), full["a3_w_in"], no_bias)
    conv[3], wout[3] = conv_full("a3_conv"), full["a3_w_out"].reshape(D, D)
    (hm,), _ = a_mix_fwd(z, h[-1], conv[3], wout[3])
    saved["mix3"] = (z, u)
    h.append(hm)
    (hn, zf, uf), _ = ffn_fwd(hm, vec("ln2_3"), wgu(3), wd(3))
    saved["ffn3"] = (zf, uf)
    h.append(hn)

    loss_lanes, dh, g_lnf = loss_head(h[-1], vec("ln_f"), target)

    g = {"ln_f": g_lnf}
    recv = {}

    def repl_rows():
        loss_row = jnp.pad(loss_lanes, ((0, 0), (0, D - LOSS_LANES)))
        return jnp.concatenate([g[n].reshape(-1, D) for n in REPL] + [loss_row], axis=0)

    def scatter(slot):
        parts = []
        for n in SCATTER_PLAN.get(slot, []):
            parts.append((repl_rows(), "all") if n == "repl" else (g[n], SHARDED[n][0]))
        return scatter_ride(parts) if parts else None

    def arrived(slot, outs):
        recv.update(zip(SCATTER_PLAN.get(slot, []), outs))

    for i in (3, 2, 1, 0):
        zf, uf = saved[f"ffn{i}"]
        (dh_prev, dzf, a, g[f"ln2_{i}"], dhb), got = ffn_bwd_x(dh, h[2 * i + 1], vec(f"ln2_{i}"), zf, wgu(i), wd(i),
                                                             ride=scatter(f"ffnx{i}"))
        arrived(f"ffnx{i}", got)
        dh = dh_prev
        (dwgu, dwd), got = ffn_bwd_w(uf, dzf, a, dhb, ride=scatter(f"ffnw{i}"))
        arrived(f"ffnw{i}", got)
        g[f"ffn{i}_w_gu"], g[f"ffn{i}_w_down"] = dwgu.reshape(N_DEV, FC, D), dwd.reshape(FF, D)
        if i == 0:
            late_ffn = scatter_start([(g[n], SHARDED[n][0]) for n in LATE_FFN], "late_ffn_start")
        hin = h[2 * i]
        if i in (0, 3):
            z, u = saved[f"mix{i}"]
            (dz, g[f"a{i}_w_out"], g[f"a{i}_conv"]), got = a_mix_bwd(dh, z, conv[i], wout[i], ride=scatter(f"mixb{i}"))
            arrived(f"mixb{i}", got)
            (g[f"a{i}_w_in"],), got = in_proj_bwd_w(u, dz, ride=scatter(f"inw{i}"))
            arrived(f"inw{i}", got)
            if i == 0:
                late_mix = scatter_start([(g[n], SHARDED[n][0]) for n in LATE_MIX], "late_mix_start")
            (dh, g[f"ln1_{i}"], _), got = in_proj_bwd_x(dz, full[f"a{i}_w_in"], hin, vec(f"ln1_{i}"), dh,
                                                       ride=scatter(f"inx{i}"))
            arrived(f"inx{i}", got)
        elif i == 1:
            (dh, g["ln1_1"], g["b1_w_grp"], g["b1_scale"]), got = b_mix_bwd(dh, hin, vec("ln1_1"), wgrp, vec("b1_scale"),
                                                                             ride=scatter("mixb1"))
            arrived("mixb1", got)
        else:
            z, u, h2 = saved["mix2"]
            (dz, g["c2_w_pw2"], g["c2_b_pw2"], g["c2_ln_g"], g["c2_ln_b"], g["c2_b_dw"], g["c2_dw"]), got = c_mix_bwd(
                dh, z, h2, cdw, vec("c2_ln_g"), vec("c2_ln_b"), wpw2, ride=scatter("mixb2"))
            arrived("mixb2", got)
            (g["c2_w_pw1"],), got = in_proj_bwd_w(u, dz, ride=scatter("inw2"))
            arrived("inw2", got)
            (dh, g["ln1_2"], g["c2_b_pw1"]), got = in_proj_bwd_x(dz, full["c2_w_pw1"], hin, vec("ln1_2"), dh,
                                                                ride=scatter("inx2"))
            arrived("inx2", got)
    grad_x = dh[None]
    late_repl = scatter_start([(repl_rows(), "all")], "late_repl_start")

    grad, delta, new_m, new_v = {}, {}, {}, {}
    two_d = lambda n, a: stored(n, a.reshape(-1, p[n].shape[-1]))

    def adam_calls(names_, tokens):
        groups, last = {}, None
        for n in names_:
            groups.setdefault((two_d(n, p[n]).shape, SHARDED[n][1]), []).append(n)
        for (shape, rb), members in groups.items():
            per_weight = 2 * rb * shape[1] * (7 * 4 + N_DEV * recv[members[0]].dtype.itemsize)
            at_once = max(1, (ADAM_VMEM // per_weight))
            for lo in range(0, len(members), at_once):
                ns = members[lo:lo + at_once]
                outs = adamw([two_d(n, p[n]) for n in ns], [two_d(n, p["m_" + n]) for n in ns],
                             [two_d(n, p["v_" + n]) for n in ns], [recv[n].reshape(N_DEV, *shape) for n in ns], rb, tokens)
                for res, o in zip((grad, delta, new_m, new_v), outs):
                    res.update({n: stored(n, a).reshape(p[n].shape) for n, a in zip(ns, o)})
                last = outs[0][0]
        return last

    early_done = adam_calls([n for n in SHARDED if n not in LATE_FFN + LATE_MIX],
                            (late_ffn[-1], late_mix[-1], late_repl[-1]))
    recv.update(zip(LATE_FFN, scatter_wait(late_ffn, [early_done], "late_ffn_wait")))
    recv.update(zip(LATE_MIX, scatter_wait(late_mix, [early_done, dh], "late_mix_wait")))
    late_done = adam_calls(LATE_FFN + LATE_MIX, ())
    recv["repl"] = scatter_wait(late_repl, [late_done], "late_repl_wait")[0]
    outs = adamw_vectors([vec(n) for n in REPL], [vec("m_" + n) for n in REPL], [vec("v_" + n) for n in REPL], recv["repl"])
    for res, o in zip((grad, delta, new_m, new_v), outs):
        res.update({n: a.reshape(p[n].shape) for n, a in zip(REPL, o)})

    loss = jnp.sum(recv["repl"][:, REPL_ROWS, 0])
    return (loss, grad_x, *[grad[n] for n in WEIGHTS], *[delta[n] for n in WEIGHTS],
            *[new_m[n] for n in WEIGHTS], *[new_v[n] for n in WEIGHTS])


def kernel(x, ln1_0, a0_w_in, a0_conv, a0_w_out, ln2_0, ffn0_w_gu, ffn0_w_down, ln1_1, b1_w_grp, b1_scale, ln2_1, ffn1_w_gu, ffn1_w_down, ln1_2, c2_w_pw1, c2_b_pw1, c2_dw, c2_b_dw, c2_ln_g, c2_ln_b, c2_w_pw2, c2_b_pw2, ln2_2, ffn2_w_gu, ffn2_w_down, ln1_3, a3_w_in, a3_conv, a3_w_out, ln2_3, ffn3_w_gu, ffn3_w_down, ln_f, loss_target, m_ln1_0, m_a0_w_in, m_a0_conv, m_a0_w_out, m_ln2_0, m_ffn0_w_gu, m_ffn0_w_down, m_ln1_1, m_b1_w_grp, m_b1_scale, m_ln2_1, m_ffn1_w_gu, m_ffn1_w_down, m_ln1_2, m_c2_w_pw1, m_c2_b_pw1, m_c2_dw, m_c2_b_dw, m_c2_ln_g, m_c2_ln_b, m_c2_w_pw2, m_c2_b_pw2, m_ln2_2, m_ffn2_w_gu, m_ffn2_w_down, m_ln1_3, m_a3_w_in, m_a3_conv, m_a3_w_out, m_ln2_3, m_ffn3_w_gu, m_ffn3_w_down, m_ln_f, v_ln1_0, v_a0_w_in, v_a0_conv, v_a0_w_out, v_ln2_0, v_ffn0_w_gu, v_ffn0_w_down, v_ln1_1, v_b1_w_grp, v_b1_scale, v_ln2_1, v_ffn1_w_gu, v_ffn1_w_down, v_ln1_2, v_c2_w_pw1, v_c2_b_pw1, v_c2_dw, v_c2_b_dw, v_c2_ln_g, v_c2_ln_b, v_c2_w_pw2, v_c2_b_pw2, v_ln2_2, v_ffn2_w_gu, v_ffn2_w_down, v_ln1_3, v_a3_w_in, v_a3_conv, v_a3_w_out, v_ln2_3, v_ffn3_w_gu, v_ffn3_w_down, v_ln_f):
    return _step(dict(locals()))
```

```python
import jax
import jax.numpy as jnp
from jax import lax
from jax.experimental import pallas as pl
from jax.experimental.pallas import tpu as pltpu

F32 = jnp.float32
BF16 = jnp.bfloat16

N_DEV = 8
D = 1024
FF = 2816
FC = FF // 4
RMS_EPS = 1e-6
LN_EPS = 1e-5
TM = 512
HALO = 32
VMEM_LIMIT = 60 * 1024 * 1024

NT = (((1,), (1,)), ((), ()))
TN = (((0,), (0,)), ((), ()))
MESH = pl.DeviceIdType.MESH
ANY = pl.BlockSpec(memory_space=pl.ANY)
N_PEERS = N_DEV - 1


def _dot(a, b):
    return jnp.dot(a, b, preferred_element_type=F32)


def _dot_nt(a, b):
    return lax.dot_general(a, b, NT, preferred_element_type=F32)


def _dot_tn(a, b):
    return lax.dot_general(a, b, TN, preferred_element_type=F32)


def _rms_fwd(x, gain):
    r = lax.rsqrt(jnp.mean(x * x, axis=-1, keepdims=True) + RMS_EPS)
    return x * r * gain


def _rms_bwd(x, gain, du):
    r = lax.rsqrt(jnp.mean(x * x, axis=-1, keepdims=True) + RMS_EPS)
    xhat = x * r
    dgain = jnp.sum(du * xhat, axis=0, keepdims=True)
    dxhat = du * gain
    dx = r * (dxhat - xhat * jnp.mean(dxhat * xhat, axis=-1, keepdims=True))
    return dx, dgain


def _dev_index(p):
    return 4 * p[0] + 2 * p[1] + p[2]


def _place():
    return lax.axis_index("x"), lax.axis_index("y"), lax.axis_index("c")


class Ride:
    def __init__(self, ins, out_shapes, start, finish):
        self.ins, self.out_shapes, self.start, self.finish = list(ins), list(out_shapes), start, finish
        n = len(self.ins)
        self.sems = [pltpu.SemaphoreType.DMA((n * N_PEERS,)), pltpu.SemaphoreType.DMA((n * N_PEERS,)),
                     pltpu.SemaphoreType.DMA((n,))]


def gather_ride(shards, kinds):
    n = len(shards)

    def setup(ins, outs, sems):
        send_sems, recv_sems, local_sems = sems
        x, y, c = _place()
        chips = [(1 - x, y), (x, 1 - y), (1 - x, 1 - y)]

        def copy(a, k, block, to, src=None):
            slot = _chunk(outs[a], kinds[a], _dev_index(block))
            return pltpu.make_async_remote_copy(
                src_ref=slot if src is None else src, dst_ref=slot,
                send_sem=send_sems.at[a * N_PEERS + k], recv_sem=recv_sems.at[a * N_PEERS + k],
                device_id=to, device_id_type=MESH)

        def mine(a):
            return pltpu.make_async_copy(ins[a], _chunk(outs[a], kinds[a], _dev_index((x, y, c))), local_sems.at[a])

        def first(a):
            return [copy(a, 0, (x, y, c), (x, y, 1 - c), src=ins[a])] + [
                copy(a, 1 + j, (x, y, c), (*chip, c), src=ins[a]) for j, chip in enumerate(chips)]

        return (x, y, c), chips, copy, mine, first

    def start(ins, outs, sems):
        _, _, _, mine, first = setup(ins, outs, sems)
        for a in range(n):
            mine(a).start()
            for cp in first(a):
                cp.start()

    def finish(ins, outs, sems):
        (x, y, c), chips, copy, mine, first = setup(ins, outs, sems)
        me, sibling = (x, y, c), (x, y, 1 - c)
        for a in range(n):
            for j, chip in enumerate(chips):
                copy(a, 1 + j, (*chip, c), me).wait_recv()
                copy(a, 4 + j, (*chip, c), sibling).start()
        for a in range(n):
            copy(a, 0, sibling, me).wait_recv()
            for j, chip in enumerate(chips):
                copy(a, 4 + j, (*chip, 1 - c), me).wait_recv()
        for a in range(n):
            for cp in first(a):
                cp.wait_send()
            for j, chip in enumerate(chips):
                copy(a, 4 + j, (*chip, c), sibling).wait_send()
        for a in range(n):
            mine(a).wait()

    shapes = [(N_DEV, *s.shape) if kind == "lead" else (s.shape[0], N_DEV * s.shape[1]) for s, kind in zip(shards, kinds)]
    return Ride(shards, [jax.ShapeDtypeStruct(shape, s.dtype) for shape, s in zip(shapes, shards)], start, finish)


def _chunk(ref, kind, j):
    if kind == "lead":
        return ref.at[j]
    if kind == "rows":
        r = ref.shape[0] // N_DEV
        return ref.at[pl.ds(j * r, r)]
    if kind == "mid":
        r = ref.shape[1] // N_DEV
        return ref.at[:, pl.ds(j * r, r), :]
    if kind == "cols":
        c = ref.shape[1] // N_DEV
        return ref.at[:, pl.ds(j * c, c)]
    return ref


def _chunk_shape(shape, kind):
    if kind == "lead":
        return tuple(shape[1:])
    if kind == "rows":
        return (shape[0] // N_DEV, *shape[1:])
    if kind == "mid":
        return (shape[0], shape[1] // N_DEV, shape[2])
    if kind == "cols":
        return (shape[0], shape[1] // N_DEV)
    return tuple(shape)


def scatter_ride(parts):
    n = len(parts)
    kinds = [k for _, k in parts]

    def setup(ins, outs, sems):
        send_sems, recv_sems, local_sems = sems
        x, y, c = _place()
        me = _dev_index((x, y, c))
        peers = []
        for k in range(1, N_DEV):
            kx, ky, kc = (k >> 2) & 1, (k >> 1) & 1, k & 1
            peers.append((1 - x if kx else x, 1 - y if ky else y, 1 - c if kc else c))

        def copy(a, k, peer):
            return pltpu.make_async_remote_copy(
                src_ref=_chunk(ins[a], kinds[a], _dev_index(peer)), dst_ref=outs[a].at[me],
                send_sem=send_sems.at[a * N_PEERS + k], recv_sem=recv_sems.at[a * N_PEERS + k],
                device_id=peer, device_id_type=MESH)

        def arrival(a, k, peer):
            slot = outs[a].at[_dev_index(peer)]
            return pltpu.make_async_remote_copy(
                src_ref=slot, dst_ref=slot,
                send_sem=send_sems.at[a * N_PEERS + k], recv_sem=recv_sems.at[a * N_PEERS + k],
                device_id=peer, device_id_type=MESH)

        def mine(a):
            return pltpu.make_async_copy(_chunk(ins[a], kinds[a], me), outs[a].at[me], local_sems.at[a])

        return peers, copy, arrival, mine

    def start(ins, outs, sems):
        peers, copy, _, mine = setup(ins, outs, sems)
        for a in range(n):
            mine(a).start()
            for k, peer in enumerate(peers):
                copy(a, k, peer).start()

    def finish(ins, outs, sems):
        peers, copy, arrival, mine = setup(ins, outs, sems)
        for a in range(n):
            for k, peer in enumerate(peers):
                arrival(a, k, peer).wait_recv()
        for a in range(n):
            for k, peer in enumerate(peers):
                copy(a, k, peer).wait_send()
            mine(a).wait()

    shapes = [jax.ShapeDtypeStruct((N_DEV, *_chunk_shape(arr.shape, kind)), arr.dtype) for arr, kind in parts]
    return Ride([arr for arr, _ in parts], shapes, start, finish)


HBM = pl.BlockSpec(memory_space=pltpu.HBM)
SEM = pl.BlockSpec(memory_space=pltpu.SEMAPHORE)
DATAFLOW = pltpu.SideEffectType.DATAFLOW_SIDE_EFFECTING
TOKEN = (8, 128)


def _scatter_copies(kinds, ins, lands, send_sems, recv_sems):
    x, y, c = _place()
    me = _dev_index((x, y, c))
    sends, arrivals = [], []
    for a, kind in enumerate(kinds):
        for k in range(1, N_DEV):
            kx, ky, kc = (k >> 2) & 1, (k >> 1) & 1, k & 1
            peer = (1 - x if kx else x, 1 - y if ky else y, 1 - c if kc else c)
            sem = a * N_PEERS + k - 1
            sends.append(pltpu.make_async_remote_copy(
                src_ref=_chunk(ins[a], kind, _dev_index(peer)), dst_ref=lands[a].at[me],
                send_sem=send_sems.at[sem], recv_sem=recv_sems.at[sem], device_id=peer, device_id_type=MESH))
            slot = lands[a].at[_dev_index(peer)]
            arrivals.append(pltpu.make_async_remote_copy(
                src_ref=slot, dst_ref=slot, send_sem=send_sems.at[sem], recv_sem=recv_sems.at[sem],
                device_id=peer, device_id_type=MESH))
    return me, sends, arrivals


def own_blocks(parts, name):
    n = len(parts)
    x, y, c = _place()
    me = jnp.reshape(_dev_index((x, y, c)), (1,)).astype(jnp.int32)

    def block_of(shape, kind):
        blk = _chunk_shape(shape, kind)
        if kind == "lead":
            return pl.BlockSpec((1, *blk), lambda i, me_ref: (me_ref[0], *[0] * len(blk)))
        if kind == "rows":
            return pl.BlockSpec(blk, lambda i, me_ref: (me_ref[0], *[0] * (len(blk) - 1)))
        if kind == "cols":
            return pl.BlockSpec(blk, lambda i, me_ref: (0, me_ref[0]))
        return pl.BlockSpec(blk, lambda i, me_ref: (0,) * len(blk))

    def body(me_ref, *refs):
        for (arr, kind), src, dst in zip(parts, refs[:n], refs[n:]):
            dst[0] = src[0] if kind == "lead" else src[...]

    shapes = [(N_DEV, *_chunk_shape(a.shape, k)) for a, k in parts]
    return pl.pallas_call(
        body, name=name,
        grid_spec=pltpu.PrefetchScalarGridSpec(
            num_scalar_prefetch=1, grid=(1,),
            in_specs=[block_of(a.shape, k) for a, k in parts],
            out_specs=[pl.BlockSpec((1, *s[1:]), lambda i, me_ref, r=len(s) - 1: (me_ref[0], *[0] * r)) for s in shapes]),
        out_shape=[jax.ShapeDtypeStruct(s, a.dtype) for s, (a, _) in zip(shapes, parts)],
        compiler_params=pltpu.CompilerParams(vmem_limit_bytes=VMEM_LIMIT),
    )(me, *[a for a, _ in parts])


def scatter_start(parts, name):
    n = len(parts)
    kinds = [k for _, k in parts]
    arrays = [pltpu.with_memory_space_constraint(a, pltpu.HBM) for a, _ in parts]
    zones = [pltpu.with_memory_space_constraint(z, pltpu.HBM) for z in own_blocks(parts, name + "_own")]

    def body(*refs):
        ins, lands = refs[:n], refs[n:2 * n]
        send_sems, recv_sems = refs[2 * n], refs[2 * n + 1]
        token = refs[4 * n + 2]
        _, sends, _ = _scatter_copies(kinds, ins, lands, send_sems, recv_sems)
        for cp in sends:
            cp.start()
        token[...] = jnp.zeros_like(token)

    outs = pl.pallas_call(
        body, name=name,
        out_shape=(pltpu.SemaphoreType.DMA((n * N_PEERS,)), pltpu.SemaphoreType.DMA((n * N_PEERS,)),
                   *[pltpu.HBM(a.shape, a.dtype) for a in arrays], *[pltpu.HBM(z.shape, z.dtype) for z in zones],
                   jax.ShapeDtypeStruct(TOKEN, F32)),
        in_specs=[HBM] * (2 * n),
        out_specs=(SEM, SEM, *[HBM] * (2 * n), pl.BlockSpec(memory_space=pltpu.VMEM)),
        input_output_aliases={i: 2 + i for i in range(2 * n)},
        compiler_params=pltpu.CompilerParams(has_side_effects=DATAFLOW),
    )(*arrays, *zones)
    return kinds, outs[0], outs[1], outs[2:2 + n], outs[2 + n:2 + 2 * n], outs[2 + 2 * n]


def scatter_wait(started, after, name):
    kinds, send_sems, recv_sems, arrays, zones, _ = started
    n = len(kinds)

    def body(*refs):
        ins, lands = refs[:n], refs[n:2 * n]
        _, sends, arrivals = _scatter_copies(kinds, ins, lands, refs[2 * n], refs[2 * n + 1])
        for cp in sends:
            cp.wait_send()
        for cp in arrivals:
            cp.wait_recv()

    outs = pl.pallas_call(
        body, name=name,
        out_shape=(*[pltpu.HBM(a.shape, a.dtype) for a in arrays], *[pltpu.HBM(z.shape, z.dtype) for z in zones]),
        in_specs=[HBM] * (2 * n) + [SEM, SEM] + [ANY] * len(after),
        out_specs=[HBM] * (2 * n),
        input_output_aliases={i: i for i in range(2 * n)},
        compiler_params=pltpu.CompilerParams(has_side_effects=DATAFLOW),
    )(*arrays, *zones, send_sems, recv_sems, *after)
    return outs[n:]


def run_ride(ride, name):
    n_in, n_out = len(ride.ins), len(ride.out_shapes)

    def body(*refs):
        ins, outs, sems = refs[:n_in], refs[n_in:n_in + n_out], refs[n_in + n_out:]
        ride.start(ins, outs, sems)
        ride.finish(ins, outs, sems)

    return pl.pallas_call(
        body, name=name, in_specs=[ANY] * n_in, out_specs=[ANY] * n_out, out_shape=ride.out_shapes,
        scratch_shapes=ride.sems,
    )(*ride.ins)


def _call(body, *, name, grid, in_specs, out_specs, out_shape, args, scratch_shapes=(), ride=None):
    params = pltpu.CompilerParams(dimension_semantics=("arbitrary",) * len(grid), vmem_limit_bytes=VMEM_LIMIT)
    if ride is None:
        outs = pl.pallas_call(body, name=name, grid=grid, in_specs=in_specs, out_specs=out_specs, out_shape=out_shape,
                              scratch_shapes=list(scratch_shapes), compiler_params=params)(*args)
        return outs, []
    n_in, n_out, n_scr = len(in_specs), len(out_specs), len(scratch_shapes)
    r_in, r_out = len(ride.ins), len(ride.out_shapes)

    def hosted(*refs):
        ins, refs = refs[:n_in], refs[n_in:]
        rins, refs = refs[:r_in], refs[r_in:]
        outs, refs = refs[:n_out], refs[n_out:]
        routs, refs = refs[:r_out], refs[r_out:]
        scratch, sems = refs[:n_scr], refs[n_scr:]
        step, n_steps = pl.program_id(0), grid[0]
        for d in range(1, len(grid)):
            step, n_steps = step * grid[d] + pl.program_id(d), n_steps * grid[d]

        @pl.when(step == 0)
        def _():
            ride.start(rins, routs, sems)

        body(*ins, *outs, *scratch)

        @pl.when(step == n_steps - 1)
        def _():
            ride.finish(rins, routs, sems)

    outs = pl.pallas_call(
        hosted, name=name + "_ride", grid=grid,
        in_specs=list(in_specs) + [ANY] * r_in, out_specs=list(out_specs) + [ANY] * r_out,
        out_shape=list(out_shape) + ride.out_shapes,
        scratch_shapes=list(scratch_shapes) + ride.sems, compiler_params=params,
    )(*args, *ride.ins)
    return outs[:n_out], outs[n_out:]


def ffn_fwd(h, gain, wgu, wd, ride=None):
    t = h.shape[0]
    tf = min(TF, t)

    def body(h_ref, g_ref, wgu_ref, wd_ref, hn_ref, z_ref, u_ref, acc):
        k = pl.program_id(1)

        @pl.when(k == 0)
        def _():
            u_ref[...] = _rms_fwd(h_ref[...], g_ref[...]).astype(BF16)
            acc[...] = jnp.zeros_like(acc)

        u = u_ref[...]
        g = _dot_nt(u, wgu_ref[0, 0])
        up = _dot_nt(u, wgu_ref[1, 0])
        z_ref[0, 0] = g.astype(BF16)
        z_ref[1, 0] = up.astype(BF16)
        a = g * jax.nn.sigmoid(g) * up
        acc[...] += _dot(a.astype(BF16), wd_ref[0])

        @pl.when(k == 3)
        def _():
            hn_ref[...] = h_ref[...] + acc[...]

    return _call(
        body, name="ffn_fwd", grid=(t // tf, 4), ride=ride,
        in_specs=[pl.BlockSpec((tf, D), lambda i, k: (i, 0)),
                  pl.BlockSpec((1, D), lambda i, k: (0, 0)),
                  pl.BlockSpec((2, 1, FC, D), lambda i, k: (0, k, 0, 0)),
                  pl.BlockSpec((1, FC, D), lambda i, k: (k, 0, 0))],
        out_specs=[pl.BlockSpec((tf, D), lambda i, k: (i, 0)),
                   pl.BlockSpec((2, 1, tf, FC), lambda i, k: (0, k, i, 0)),
                   pl.BlockSpec((tf, D), lambda i, k: (i, 0))],
        out_shape=[jax.ShapeDtypeStruct((t, D), F32),
                   jax.ShapeDtypeStruct((2, 4, t, FC), BF16),
                   jax.ShapeDtypeStruct((t, D), BF16)],
        scratch_shapes=[pltpu.VMEM((tf, D), F32)],
        args=(h, gain, wgu, wd))


def ffn_bwd_x(dh, h, gain, z, wgu, wd, ride=None):
    t = h.shape[0]

    def body(dh_ref, h_ref, g_ref, z_ref, wgu_ref, wd_ref, dhp_ref, dz_ref, a_ref, dgain_ref, dhb, du):
        i, k = pl.program_id(0), pl.program_id(1)

        @pl.when(k == 0)
        def _():
            dhb[...] = dh_ref[...].astype(BF16)
            du[...] = jnp.zeros_like(du)

        @pl.when((k == 0) & (i == 0))
        def _():
            dgain_ref[...] = jnp.zeros_like(dgain_ref)

        da = _dot_nt(dhb[...], wd_ref[0])
        g = z_ref[0, 0].astype(F32)
        up = z_ref[1, 0].astype(F32)
        sg = jax.nn.sigmoid(g)
        silu = g * sg
        a_ref[0] = (silu * up).astype(BF16)
        dg = (da * up * (sg * (1.0 + g * (1.0 - sg)))).astype(BF16)
        dup = (da * silu).astype(BF16)
        dz_ref[0, 0] = dg
        dz_ref[1, 0] = dup
        for n in range(2):
            cols = slice(n * (D // 2), (n + 1) * (D // 2))
            du[:, cols] += _dot(dg, wgu_ref[0, 0, :, cols]) + _dot(dup, wgu_ref[1, 0, :, cols])

        @pl.when(k == 3)
        def _():
            dx, dgain = _rms_bwd(h_ref[...], g_ref[...], du[...])
            dhp_ref[...] = dh_ref[...] + dx
            dgain_ref[...] += dgain

    return _call(
        body, name="ffn_bwd_x", grid=(t // TM, 4), ride=ride,
        in_specs=[pl.BlockSpec((TM, D), lambda i, k: (i, 0)),
                  pl.BlockSpec((TM, D), lambda i, k: (i, 0)),
                  pl.BlockSpec((1, D), lambda i, k: (0, 0)),
                  pl.BlockSpec((2, 1, TM, FC), lambda i, k: (0, k, i, 0)),
                  pl.BlockSpec((2, 1, FC, D), lambda i, k: (0, k, 0, 0)),
                  pl.BlockSpec((1, FC, D), lambda i, k: (k, 0, 0))],
        out_specs=[pl.BlockSpec((TM, D), lambda i, k: (i, 0)),
                   pl.BlockSpec((2, 1, TM, FC), lambda i, k: (0, k, i, 0)),
                   pl.BlockSpec((1, TM, FC), lambda i, k: (k, i, 0)),
                   pl.BlockSpec((1, D), lambda i, k: (0, 0)),
                   pl.BlockSpec((TM, D), lambda i, k: (i, 0))],
        out_shape=[jax.ShapeDtypeStruct((t, D), F32),
                   jax.ShapeDtypeStruct((2, 4, t, FC), BF16),
                   jax.ShapeDtypeStruct((4, t, FC), BF16),
                   jax.ShapeDtypeStruct((1, D), F32),
                   jax.ShapeDtypeStruct((t, D), BF16)],
        scratch_shapes=[pltpu.VMEM((TM, D), F32)],
        args=(dh, h, gain, z, wgu, wd))


TF = 1024
TW = 2048


def ffn_bwd_w(u, dz, a, dhb, ride=None):
    t = u.shape[0]
    tw = min(TW, t)
    steps = t // tw

    def body(u_ref, dz_ref, a_ref, dh_ref, dwgu_ref, dwd_ref, acc_gu, acc_d):
        j = pl.program_id(1)

        @pl.when(j == 0)
        def _():
            acc_gu[...] = jnp.zeros_like(acc_gu)
            acc_d[...] = jnp.zeros_like(acc_d)

        ub = u_ref[...]
        acc_gu[0] += _dot_tn(dz_ref[0, 0], ub)
        acc_gu[1] += _dot_tn(dz_ref[1, 0], ub)
        acc_d[...] += _dot_tn(a_ref[0], dh_ref[...])

        @pl.when(j == steps - 1)
        def _():
            dwgu_ref[:, 0] = acc_gu[...].astype(BF16)
            dwd_ref[0] = acc_d[...].astype(BF16)

    return _call(
        body, name="ffn_bwd_w", grid=(4, steps), ride=ride,
        in_specs=[pl.BlockSpec((tw, D), lambda k, j: (j, 0)),
                  pl.BlockSpec((2, 1, tw, FC), lambda k, j: (0, k, j, 0)),
                  pl.BlockSpec((1, tw, FC), lambda k, j: (k, j, 0)),
                  pl.BlockSpec((tw, D), lambda k, j: (j, 0))],
        out_specs=[pl.BlockSpec((2, 1, FC, D), lambda k, j: (0, k, 0, 0)),
                   pl.BlockSpec((1, FC, D), lambda k, j: (k, 0, 0))],
        out_shape=[jax.ShapeDtypeStruct((2, 4, FC, D), BF16),
                   jax.ShapeDtypeStruct((4, FC, D), BF16)],
        scratch_shapes=[pltpu.VMEM((2, FC, D), F32), pltpu.VMEM((FC, D), F32)],
        args=(u, dz, a, dhb))


def _prev_halo(i, tile=TM):
    return jnp.maximum(i * (tile // HALO) - 1, 0)


def _next_halo(i, t, tile=TM):
    return jnp.minimum((i + 1) * (tile // HALO), t // HALO - 1)


def rms_matmul(h, gain, w, bias, ride=None):
    t = h.shape[0]
    n = w.shape[1]

    def body(h_ref, g_ref, w_ref, b_ref, z_ref, u_ref):
        u = _rms_fwd(h_ref[...], g_ref[...]).astype(BF16)
        u_ref[...] = u
        z_ref[...] = (_dot(u, w_ref[...]) + b_ref[...]).astype(BF16)

    return _call(
        body, name=f"rms_matmul_{n}", grid=(t // TM,), ride=ride,
        in_specs=[pl.BlockSpec((TM, D), lambda i: (i, 0)),
                  pl.BlockSpec((1, D), lambda i: (0, 0)),
                  pl.BlockSpec((D, n), lambda i: (0, 0)),
                  pl.BlockSpec((1, n), lambda i: (0, 0))],
        out_specs=[pl.BlockSpec((TM, n), lambda i: (i, 0)),
                   pl.BlockSpec((TM, D), lambda i: (i, 0))],
        out_shape=[jax.ShapeDtypeStruct((t, n), BF16),
                   jax.ShapeDtypeStruct((t, D), BF16)],
        args=(h, gain, w, bias))


def in_proj_bwd_x(dz, w, h, gain, dh, ride=None):
    t = h.shape[0]
    n = w.shape[1]

    def body(dz_ref, w_ref, h_ref, g_ref, dh_ref, dhp_ref, dgain_ref, dbias_ref):
        @pl.when(pl.program_id(0) == 0)
        def _():
            dgain_ref[...] = jnp.zeros_like(dgain_ref)
            dbias_ref[...] = jnp.zeros_like(dbias_ref)

        du = _dot_nt(dz_ref[...], w_ref[...])
        dx, dgain = _rms_bwd(h_ref[...], g_ref[...], du)
        dhp_ref[...] = dh_ref[...] + dx
        dgain_ref[...] += dgain
        dbias_ref[...] += jnp.sum(dz_ref[...].astype(F32), axis=0, keepdims=True)

    return _call(
        body, name=f"in_proj_bwd_x_{n}", grid=(t // TM,), ride=ride,
        in_specs=[pl.BlockSpec((TM, n), lambda i: (i, 0)),
                  pl.BlockSpec((D, n), lambda i: (0, 0)),
                  pl.BlockSpec((TM, D), lambda i: (i, 0)),
                  pl.BlockSpec((1, D), lambda i: (0, 0)),
                  pl.BlockSpec((TM, D), lambda i: (i, 0))],
        out_specs=[pl.BlockSpec((TM, D), lambda i: (i, 0)),
                   pl.BlockSpec((1, D), lambda i: (0, 0)),
                   pl.BlockSpec((1, n), lambda i: (0, 0))],
        out_shape=[jax.ShapeDtypeStruct((t, D), F32),
                   jax.ShapeDtypeStruct((1, D), F32),
                   jax.ShapeDtypeStruct((1, n), F32)],
        args=(dz, w, h, gain, dh))


def in_proj_bwd_w(u, dz, ride=None):
    t = u.shape[0]
    n = dz.shape[1]
    ti = min(TF, t)
    steps = t // ti

    def body(u_ref, dz_ref, dw_ref, acc):
        s = pl.program_id(0)

        @pl.when(s == 0)
        def _():
            acc[...] = jnp.zeros_like(acc)

        acc[...] += _dot_tn(u_ref[...], dz_ref[...])

        @pl.when(s == steps - 1)
        def _():
            dw_ref[...] = acc[...].astype(BF16)

    return _call(
        body, name=f"in_proj_bwd_w_{n}", grid=(steps,), ride=ride,
        in_specs=[pl.BlockSpec((ti, D), lambda s: (s, 0)),
                  pl.BlockSpec((ti, n), lambda s: (s, 0))],
        out_specs=[pl.BlockSpec((D, n), lambda s: (0, 0))],
        out_shape=[jax.ShapeDtypeStruct((D, n), BF16)],
        scratch_shapes=[pltpu.VMEM((D, n), F32)],
        args=(u, dz))


A_TAPS = 3


def a_mix_fwd(z, h, conv, wout, ride=None):
    t = h.shape[0]

    def body(z_ref, zp_ref, h_ref, cw_ref, wo_ref, hn_ref, pad):
        i = pl.program_id(0)
        ph = zp_ref[:, D:2 * D].astype(F32) * zp_ref[:, 2 * D:].astype(F32)
        pad[0:HALO, :] = jnp.where(i == 0, 0.0, ph)
        pad[HALO:, :] = z_ref[:, D:2 * D].astype(F32) * z_ref[:, 2 * D:].astype(F32)
        q = jnp.zeros((TM, D), F32)
        for k in range(A_TAPS):
            off = HALO - (A_TAPS - 1) + k
            q += cw_ref[k:k + 1, :] * pad[off:off + TM, :]
        r = z_ref[:, 0:D].astype(F32) * q
        hn_ref[...] = h_ref[...] + _dot(r.astype(BF16), wo_ref[...])

    return _call(
        body, name="a_mix_fwd", grid=(t // TM,), ride=ride,
        in_specs=[pl.BlockSpec((TM, 3 * D), lambda i: (i, 0)),
                  pl.BlockSpec((HALO, 3 * D), lambda i: (_prev_halo(i), 0)),
                  pl.BlockSpec((TM, D), lambda i: (i, 0)),
                  pl.BlockSpec((A_TAPS, D), lambda i: (0, 0)),
                  pl.BlockSpec((D, D), lambda i: (0, 0))],
        out_specs=[pl.BlockSpec((TM, D), lambda i: (i, 0))],
        out_shape=[jax.ShapeDtypeStruct((t, D), F32)],
        scratch_shapes=[pltpu.VMEM((HALO + TM, D), F32)],
        args=(z, z, h, conv, wout))


def a_mix_bwd(dh, z, conv, wout, ride=None):
    t = dh.shape[0]
    steps = t // TM

    def body(dh_ref, dhn_ref, z_ref, zp_ref, zn_ref, cw_ref, wo_ref, dz_ref, dwo_ref, dcw_ref, pad, dqpad, dwo):
        i = pl.program_id(0)
        last = i == steps - 1

        @pl.when(i == 0)
        def _():
            dwo[...] = jnp.zeros_like(dwo)
            dcw_ref[...] = jnp.zeros_like(dcw_ref)

        ph = zp_ref[:, D:2 * D].astype(F32) * zp_ref[:, 2 * D:].astype(F32)
        pad[0:HALO, :] = jnp.where(i == 0, 0.0, ph)
        c = z_ref[:, D:2 * D].astype(F32)
        v = z_ref[:, 2 * D:].astype(F32)
        pad[HALO:, :] = c * v
        q = jnp.zeros((TM, D), F32)
        for k in range(A_TAPS):
            off = HALO - (A_TAPS - 1) + k
            q += cw_ref[k:k + 1, :] * pad[off:off + TM, :]
        b = z_ref[:, 0:D].astype(F32)
        dhb = dh_ref[...].astype(BF16)
        dwo[...] += _dot_tn((b * q).astype(BF16), dhb)
        dr = _dot_nt(dhb, wo_ref[...])
        dz_ref[:, 0:D] = (dr * q).astype(BF16)
        dq = dr * b
        drn = _dot_nt(dhn_ref[...].astype(BF16), wo_ref[...])
        dqpad[0:TM, :] = dq
        dqpad[TM:, :] = jnp.where(last, 0.0, drn * zn_ref[:, 0:D].astype(F32))
        dp = jnp.zeros((TM, D), F32)
        for k in range(A_TAPS):
            off = A_TAPS - 1 - k
            dp += cw_ref[k:k + 1, :] * dqpad[off:off + TM, :]
            poff = HALO - (A_TAPS - 1) + k
            dcw_ref[k:k + 1, :] += jnp.sum(dq * pad[poff:poff + TM, :], axis=0, keepdims=True)
        dz_ref[:, D:2 * D] = (dp * v).astype(BF16)
        dz_ref[:, 2 * D:] = (dp * c).astype(BF16)

        @pl.when(last)
        def _():
            dwo_ref[...] = dwo[...].astype(BF16)

    return _call(
        body, name="a_mix_bwd", grid=(steps,), ride=ride,
        in_specs=[pl.BlockSpec((TM, D), lambda i: (i, 0)),
                  pl.BlockSpec((HALO, D), lambda i: (_next_halo(i, t), 0)),
                  pl.BlockSpec((TM, 3 * D), lambda i: (i, 0)),
                  pl.BlockSpec((HALO, 3 * D), lambda i: (_prev_halo(i), 0)),
                  pl.BlockSpec((HALO, 3 * D), lambda i: (_next_halo(i, t), 0)),
                  pl.BlockSpec((A_TAPS, D), lambda i: (0, 0)),
                  pl.BlockSpec((D, D), lambda i: (0, 0))],
        out_specs=[pl.BlockSpec((TM, 3 * D), lambda i: (i, 0)),
                   pl.BlockSpec((D, D), lambda i: (0, 0)),
                   pl.BlockSpec((A_TAPS, D), lambda i: (0, 0))],
        out_shape=[jax.ShapeDtypeStruct((t, 3 * D), BF16),
                   jax.ShapeDtypeStruct((D, D), BF16),
                   jax.ShapeDtypeStruct((A_TAPS, D), F32)],
        scratch_shapes=[pltpu.VMEM((HALO + TM, D), F32), pltpu.VMEM((TM + HALO, D), F32), pltpu.VMEM((D, D), F32)],
        args=(dh, dh, z, z, z, conv, wout))


C_TAPS = 31


def _glu(zr):
    return zr[:, 0:D].astype(F32) * jax.nn.sigmoid(zr[:, D:].astype(F32))


def _ln_silu(h2, lg, lb):
    mu = jnp.mean(h2, axis=-1, keepdims=True)
    xc = h2 - mu
    rstd = lax.rsqrt(jnp.mean(xc * xc, axis=-1, keepdims=True) + LN_EPS)
    xn = xc * rstd
    h3 = xn * lg + lb
    s3 = jax.nn.sigmoid(h3)
    return xn, rstd, h3, s3


def _ln_silu_bwd(h2, lg, lb, dh4):
    xn, rstd, h3, s3 = _ln_silu(h2, lg, lb)
    dh3 = dh4 * (s3 * (1.0 + h3 * (1.0 - s3)))
    dxn = dh3 * lg
    dh2 = rstd * (dxn - jnp.mean(dxn, axis=-1, keepdims=True) - xn * jnp.mean(dxn * xn, axis=-1, keepdims=True))
    return dh2, dh3, xn, h3 * s3


TC = 256
RB = 64
LANES = 128
SHIFTS = 7


def _shifted_copies(src, sh, rows):
    for b in range(1, SHIFTS + 1):
        sh[b - 1, 0:rows, :] = src[b:b + rows, :]


def _window(src, sh, o, r0, lanes):
    a, b = divmod(o, 8)
    ref = src if b == 0 else sh.at[b - 1]
    return ref[8 * a + r0:8 * a + r0 + RB, lanes]


def c_mix_fwd(z, h, dw, bdw, lg, lb, w2, b2, ride=None):
    t = h.shape[0]

    def body(z_ref, zp_ref, h_ref, dw_ref, bdw_ref, lg_ref, lb_ref, w2_ref, b2_ref, hn_ref, h2_ref, pad, sh):
        i = pl.program_id(0)
        pad[0:HALO, :] = jnp.where(i == 0, 0.0, _glu(zp_ref))
        pad[HALO:, :] = _glu(z_ref)
        _shifted_copies(pad, sh, TC + 24)
        for l in range(D // LANES):
            lanes = slice(l * LANES, (l + 1) * LANES)
            for r0 in range(0, TC, RB):
                acc = jnp.zeros((RB, LANES), F32) + bdw_ref[:, lanes]
                for k in range(C_TAPS):
                    acc += dw_ref[k:k + 1, lanes] * _window(pad, sh, HALO - (C_TAPS - 1) + k, r0, lanes)
                h2_ref[r0:r0 + RB, lanes] = acc
        _, _, h3, s3 = _ln_silu(h2_ref[...], lg_ref[...], lb_ref[...])
        hn_ref[...] = h_ref[...] + _dot((h3 * s3).astype(BF16), w2_ref[...]) + b2_ref[...]

    vec = pl.BlockSpec((1, D), lambda i: (0, 0))
    return _call(
        body, name="c_mix_fwd", grid=(t // TC,), ride=ride,
        in_specs=[pl.BlockSpec((TC, 2 * D), lambda i: (i, 0)),
                  pl.BlockSpec((HALO, 2 * D), lambda i: (_prev_halo(i, TC), 0)),
                  pl.BlockSpec((TC, D), lambda i: (i, 0)),
                  pl.BlockSpec((C_TAPS, D), lambda i: (0, 0)),
                  vec, vec, vec,
                  pl.BlockSpec((D, D), lambda i: (0, 0)),
                  vec],
        out_specs=[pl.BlockSpec((TC, D), lambda i: (i, 0)),
                   pl.BlockSpec((TC, D), lambda i: (i, 0))],
        out_shape=[jax.ShapeDtypeStruct((t, D), F32),
                   jax.ShapeDtypeStruct((t, D), F32)],
        scratch_shapes=[pltpu.VMEM((HALO + TC, D), F32), pltpu.VMEM((SHIFTS, TC + 24, D), F32)],
        args=(z, z, h, dw, bdw, lg, lb, w2, b2))


def c_mix_bwd(dh, z, h2, dw, lg, lb, w2, ride=None):
    t = dh.shape[0]
    steps = t // TC

    def body(dh_ref, dhn_ref, z_ref, zp_ref, h2_ref, h2n_ref, dw_ref, lg_ref, lb_ref, w2_ref,
             dz_ref, dw2_ref, db2_ref, dlg_ref, dlb_ref, dbdw_ref, ddw_ref, pad, dpad, dw2, sh, dh1):
        i = pl.program_id(0)
        last = i == steps - 1

        @pl.when(i == 0)
        def _():
            for r in (dw2, db2_ref, dlg_ref, dlb_ref, dbdw_ref, ddw_ref):
                r[...] = jnp.zeros_like(r)

        lg, lb = lg_ref[...], lb_ref[...]
        dh = dh_ref[...]
        dhb = dh.astype(BF16)
        dh2, dh3, xn, h4 = _ln_silu_bwd(h2_ref[...], lg, lb, _dot_nt(dhb, w2_ref[...]))
        dw2[...] += _dot_tn(h4.astype(BF16), dhb)
        db2_ref[...] += jnp.sum(dh, axis=0, keepdims=True)
        dlg_ref[...] += jnp.sum(dh3 * xn, axis=0, keepdims=True)
        dlb_ref[...] += jnp.sum(dh3, axis=0, keepdims=True)
        dbdw_ref[...] += jnp.sum(dh2, axis=0, keepdims=True)
        dh2n, _, _, _ = _ln_silu_bwd(h2n_ref[...], lg, lb, _dot_nt(dhn_ref[...].astype(BF16), w2_ref[...]))
        dpad[0:TC, :] = dh2
        dpad[TC:, :] = jnp.where(last, 0.0, dh2n)
        _shifted_copies(dpad, sh, TC + 24)
        for l in range(D // LANES):
            lanes = slice(l * LANES, (l + 1) * LANES)
            for r0 in range(0, TC, RB):
                acc = jnp.zeros((RB, LANES), F32)
                for k in range(C_TAPS):
                    acc += dw_ref[k:k + 1, lanes] * _window(dpad, sh, C_TAPS - 1 - k, r0, lanes)
                dh1[r0:r0 + RB, lanes] = acc
        pad[0:HALO, :] = jnp.where(i == 0, 0.0, _glu(zp_ref))
        pad[HALO:, :] = _glu(z_ref)
        _shifted_copies(pad, sh, TC + 24)
        for l in range(D // LANES):
            lanes = slice(l * LANES, (l + 1) * LANES)
            accs = [jnp.zeros((8, LANES), F32) for _ in range(C_TAPS)]
            for r0 in range(0, TC, RB):
                d = dpad[r0:r0 + RB, lanes]
                for k in range(C_TAPS):
                    prod = d * _window(pad, sh, HALO - (C_TAPS - 1) + k, r0, lanes)
                    accs[k] += jnp.sum(prod.reshape(RB // 8, 8, LANES), axis=0)
            for k in range(C_TAPS):
                ddw_ref[k:k + 1, lanes] += jnp.sum(accs[k], axis=0, keepdims=True)
        a = z_ref[:, 0:D].astype(F32)
        sg = jax.nn.sigmoid(z_ref[:, D:].astype(F32))
        d1 = dh1[...]
        dz_ref[:, 0:D] = (d1 * sg).astype(BF16)
        dz_ref[:, D:] = (d1 * a * sg * (1.0 - sg)).astype(BF16)

        @pl.when(last)
        def _():
            dw2_ref[...] = dw2[...].astype(BF16)

    vec = pl.BlockSpec((1, D), lambda i: (0, 0))
    return _call(
        body, name="c_mix_bwd", grid=(steps,), ride=ride,
        in_specs=[pl.BlockSpec((TC, D), lambda i: (i, 0)),
                  pl.BlockSpec((HALO, D), lambda i: (_next_halo(i, t, TC), 0)),
                  pl.BlockSpec((TC, 2 * D), lambda i: (i, 0)),
                  pl.BlockSpec((HALO, 2 * D), lambda i: (_prev_halo(i, TC), 0)),
                  pl.BlockSpec((TC, D), lambda i: (i, 0)),
                  pl.BlockSpec((HALO, D), lambda i: (_next_halo(i, t, TC), 0)),
                  pl.BlockSpec((C_TAPS, D), lambda i: (0, 0)),
                  vec, vec,
                  pl.BlockSpec((D, D), lambda i: (0, 0))],
        out_specs=[pl.BlockSpec((TC, 2 * D), lambda i: (i, 0)),
                   pl.BlockSpec((D, D), lambda i: (0, 0)),
                   vec, vec, vec, vec,
                   pl.BlockSpec((C_TAPS, D), lambda i: (0, 0))],
        out_shape=[jax.ShapeDtypeStruct((t, 2 * D), BF16),
                   jax.ShapeDtypeStruct((D, D), BF16)]
                  + [jax.ShapeDtypeStruct((1, D), F32)] * 4
                  + [jax.ShapeDtypeStruct((C_TAPS, D), F32)],
        scratch_shapes=[pltpu.VMEM((HALO + TC, D), F32), pltpu.VMEM((TC + HALO, D), F32), pltpu.VMEM((D, D), F32),
                        pltpu.VMEM((SHIFTS, TC + 24, D), F32), pltpu.VMEM((TC, D), F32)],
        args=(dh, dh, z, z, h2, h2, dw, lg, lb, w2))


POOL_WINDOWS = (2, 4, 8, 16)
GW = D // len(POOL_WINDOWS)


def _pool_mixed(pad, g, w, inv_cnt):
    cols = slice(g * GW, (g + 1) * GW)
    s = pad[HALO:HALO + TM, cols]
    u = s
    for j in range(1, w):
        s = s + pad[HALO - j:HALO - j + TM, cols]
    return s * inv_cnt - u


def _inv_cnt(i, w):
    row = i * TM + lax.broadcasted_iota(jnp.int32, (TM, 1), 0)
    return 1.0 / jnp.minimum(row + 1, w).astype(F32)


def b_mix_fwd(h, gain, wg, scale, ride=None):
    t = h.shape[0]

    def body(h_ref, hp_ref, g_ref, wg_ref, sc_ref, hn_ref, pad):
        i = pl.program_id(0)
        gain = g_ref[...]
        pad[0:HALO, :] = jnp.where(i == 0, 0.0, _rms_fwd(hp_ref[...], gain))
        pad[HALO:, :] = _rms_fwd(h_ref[...], gain)
        for g, w in enumerate(POOL_WINDOWS):
            cols = slice(g * GW, (g + 1) * GW)
            mixed = _pool_mixed(pad, g, w, _inv_cnt(i, w))
            y = _dot(mixed.astype(BF16), wg_ref[g])
            hn_ref[:, cols] = h_ref[:, cols] + y * sc_ref[:, cols]

    return _call(
        body, name="b_mix_fwd", grid=(t // TM,), ride=ride,
        in_specs=[pl.BlockSpec((TM, D), lambda i: (i, 0)),
                  pl.BlockSpec((HALO, D), lambda i: (_prev_halo(i), 0)),
                  pl.BlockSpec((1, D), lambda i: (0, 0)),
                  pl.BlockSpec((4, GW, GW), lambda i: (0, 0, 0)),
                  pl.BlockSpec((1, D), lambda i: (0, 0))],
        out_specs=[pl.BlockSpec((TM, D), lambda i: (i, 0))],
        out_shape=[jax.ShapeDtypeStruct((t, D), F32)],
        scratch_shapes=[pltpu.VMEM((HALO + TM, D), F32)],
        args=(h, h, gain, wg, scale))


def b_mix_bwd(dh, h, gain, wg, scale, ride=None):
    t = h.shape[0]
    steps = t // TM

    def body(dh_ref, dhn_ref, h_ref, hp_ref, g_ref, wg_ref, sc_ref, dhp_ref, dgain_ref, dwg_ref, dsc_ref, pad, dpad, du):
        i = pl.program_id(0)
        last = i == steps - 1

        @pl.when(i == 0)
        def _():
            for r in (dgain_ref, dwg_ref, dsc_ref):
                r[...] = jnp.zeros_like(r)

        gain = g_ref[...]
        pad[0:HALO, :] = jnp.where(i == 0, 0.0, _rms_fwd(hp_ref[...], gain))
        pad[HALO:, :] = _rms_fwd(h_ref[...], gain)
        for g, w in enumerate(POOL_WINDOWS):
            cols = slice(g * GW, (g + 1) * GW)
            inv_cnt = _inv_cnt(i, w)
            mixed = _pool_mixed(pad, g, w, inv_cnt).astype(BF16)
            dh = dh_ref[:, cols]
            dsc_ref[:, cols] += jnp.sum(dh * _dot(mixed, wg_ref[g]), axis=0, keepdims=True)
            dy = (dh * sc_ref[:, cols]).astype(BF16)
            dwg_ref[g] += _dot_tn(mixed, dy)
            dm = _dot_nt(dy, wg_ref[g])
            dmn = _dot_nt((dhn_ref[:, cols] * sc_ref[:, cols]).astype(BF16), wg_ref[g])
            dpad[0:TM, cols] = dm * inv_cnt
            dpad[TM:, cols] = jnp.where(last, 0.0, dmn * (1.0 / w))
            s = dpad[0:TM, cols]
            for j in range(1, w):
                s = s + dpad[j:j + TM, cols]
            du[:, cols] = s - dm
        dx, dgain = _rms_bwd(h_ref[...], gain, du[...])
        dhp_ref[...] = dh_ref[...] + dx
        dgain_ref[...] += dgain

    return _call(
        body, name="b_mix_bwd", grid=(steps,), ride=ride,
        in_specs=[pl.BlockSpec((TM, D), lambda i: (i, 0)),
                  pl.BlockSpec((HALO, D), lambda i: (_next_halo(i, t), 0)),
                  pl.BlockSpec((TM, D), lambda i: (i, 0)),
                  pl.BlockSpec((HALO, D), lambda i: (_prev_halo(i), 0)),
                  pl.BlockSpec((1, D), lambda i: (0, 0)),
                  pl.BlockSpec((4, GW, GW), lambda i: (0, 0, 0)),
                  pl.BlockSpec((1, D), lambda i: (0, 0))],
        out_specs=[pl.BlockSpec((TM, D), lambda i: (i, 0)),
                   pl.BlockSpec((1, D), lambda i: (0, 0)),
                   pl.BlockSpec((4, GW, GW), lambda i: (0, 0, 0)),
                   pl.BlockSpec((1, D), lambda i: (0, 0))],
        out_shape=[jax.ShapeDtypeStruct((t, D), F32),
                   jax.ShapeDtypeStruct((1, D), F32),
                   jax.ShapeDtypeStruct((4, GW, GW), F32),
                   jax.ShapeDtypeStruct((1, D), F32)],
        scratch_shapes=[pltpu.VMEM((HALO + TM, D), F32), pltpu.VMEM((TM + HALO, D), F32), pltpu.VMEM((TM, D), F32)],
        args=(dh, dh, h, h, gain, wg, scale))


LOSS_LANES = 128


def loss_head(h, gain, target):
    t = h.shape[0]

    def body(h_ref, g_ref, tg_ref, loss_ref, dh_ref, dgain_ref):
        @pl.when(pl.program_id(0) == 0)
        def _():
            loss_ref[...] = jnp.zeros_like(loss_ref)
            dgain_ref[...] = jnp.zeros_like(dgain_ref)

        x, gain = h_ref[...], g_ref[...]
        err = _rms_fwd(x, gain) - tg_ref[...]
        per_row = jnp.mean(err * err, axis=-1, keepdims=True)
        loss_ref[...] += jnp.broadcast_to(0.5 * jnp.sum(per_row, axis=0, keepdims=True), (1, LOSS_LANES))
        dx, dgain = _rms_bwd(x, gain, err * (1.0 / D))
        dh_ref[...] = dx
        dgain_ref[...] += dgain

    outs, _ = _call(
        body, name="loss_head", grid=(t // TM,),
        in_specs=[pl.BlockSpec((TM, D), lambda i: (i, 0)),
                  pl.BlockSpec((1, D), lambda i: (0, 0)),
                  pl.BlockSpec((TM, D), lambda i: (i, 0))],
        out_specs=[pl.BlockSpec((1, LOSS_LANES), lambda i: (0, 0)),
                   pl.BlockSpec((TM, D), lambda i: (i, 0)),
                   pl.BlockSpec((1, D), lambda i: (0, 0))],
        out_shape=[jax.ShapeDtypeStruct((1, LOSS_LANES), F32),
                   jax.ShapeDtypeStruct((t, D), F32),
                   jax.ShapeDtypeStruct((1, D), F32)],
        args=(h, gain, target))
    return outs


ADAM_LR = 0.001
ADAM_B1 = 0.9
ADAM_B2 = 0.999
ADAM_EPS = 1e-08
ADAM_WD = 0.01
ADAM_STEP = 10
ADAM_VMEM = 40 * 1024 * 1024


def cast_all(arrays):
    def body(*refs):
        for src, dst in zip(refs[:len(arrays)], refs[len(arrays):]):
            dst[...] = src[...].astype(BF16)

    return pl.pallas_call(
        body, name="cast_all", out_shape=[jax.ShapeDtypeStruct(a.shape, BF16) for a in arrays],
        compiler_params=pltpu.CompilerParams(vmem_limit_bytes=VMEM_LIMIT),
    )(*arrays)


def _adam_math(w, m, v, g):
    m = ADAM_B1 * m + (1.0 - ADAM_B1) * g
    v = ADAM_B2 * v + (1.0 - ADAM_B2) * (g * g)
    m_hat = m / (1.0 - ADAM_B1 ** ADAM_STEP)
    v_hat = v / (1.0 - ADAM_B2 ** ADAM_STEP)
    return -ADAM_LR * (m_hat / (jnp.sqrt(v_hat) + ADAM_EPS) + ADAM_WD * w), m, v


def adamw(ws, ms, vs, gps, rb, tokens=()):
    n = len(ws)
    r, c = ws[0].shape
    nb = r // rb

    def body(*refs):
        i = pl.program_id(0)
        outs = refs[4 * n + len(tokens):]
        for j in range(n):
            w_ref, m_ref, v_ref, gp_ref = (refs[q * n + j] for q in range(4))
            g_ref, d_ref, nm_ref, nv_ref = (outs[q * n + j] for q in range(4))

            @pl.when(i // nb == j)
            def _():
                g = gp_ref[0].astype(F32)
                for s in range(1, N_DEV):
                    g = g + gp_ref[s].astype(F32)
                g_ref[...] = g
                d_ref[...], nm_ref[...], nv_ref[...] = _adam_math(w_ref[...], m_ref[...], v_ref[...], g)

    def blk(j):
        return pl.BlockSpec((rb, c), lambda i: (jnp.clip(i - j * nb, 0, nb - 1), 0))

    def gblk(j):
        return pl.BlockSpec((N_DEV, rb, c), lambda i: (0, jnp.clip(i - j * nb, 0, nb - 1), 0))

    outs, _ = _call(
        body, name=f"adamw_{n}x{r}x{c}", grid=(n * nb,),
        in_specs=[blk(j) for _ in range(3) for j in range(n)] + [gblk(j) for j in range(n)] + [ANY] * len(tokens),
        out_specs=[blk(j) for _ in range(4) for j in range(n)],
        out_shape=[jax.ShapeDtypeStruct((r, c), F32)] * (4 * n),
        args=(*ws, *ms, *vs, *gps, *tokens))
    return outs[:n], outs[n:2 * n], outs[2 * n:3 * n], outs[3 * n:]


def adamw_vectors(ws, ms, vs, gparts):
    nv = len(ws)

    def body(*refs):
        w_refs, m_refs, v_refs = refs[:nv], refs[nv:2 * nv], refs[2 * nv:3 * nv]
        gp_ref = refs[3 * nv]
        outs = refs[3 * nv + 1:]
        g_refs, d_refs, nm_refs, nv_refs = outs[:nv], outs[nv:2 * nv], outs[2 * nv:3 * nv], outs[3 * nv:]
        row = 0
        for i in range(nv):
            for part in range(w_refs[i].shape[1] // D):
                cols = slice(part * D, (part + 1) * D)
                g = gp_ref[0, row:row + 1, :]
                for s in range(1, N_DEV):
                    g = g + gp_ref[s, row:row + 1, :]
                g_refs[i][:, cols] = g
                d_refs[i][:, cols], nm_refs[i][:, cols], nv_refs[i][:, cols] = _adam_math(
                    w_refs[i][:, cols], m_refs[i][:, cols], v_refs[i][:, cols], g)
                row += 1

    shapes = [jax.ShapeDtypeStruct(w.shape, F32) for w in ws]
    outs = pl.pallas_call(body, name="adamw_vectors", out_shape=shapes * 4)(*ws, *ms, *vs, gparts)
    return outs[:nv], outs[nv:2 * nv], outs[2 * nv:3 * nv], outs[3 * nv:]


WEIGHTS = ["ln1_0", "a0_w_in", "a0_conv", "a0_w_out", "ln2_0", "ffn0_w_gu", "ffn0_w_down",
           "ln1_1", "b1_w_grp", "b1_scale", "ln2_1", "ffn1_w_gu", "ffn1_w_down",
           "ln1_2", "c2_w_pw1", "c2_b_pw1", "c2_dw", "c2_b_dw", "c2_ln_g", "c2_ln_b", "c2_w_pw2", "c2_b_pw2",
           "ln2_2", "ffn2_w_gu", "ffn2_w_down",
           "ln1_3", "a3_w_in", "a3_conv", "a3_w_out", "ln2_3", "ffn3_w_gu", "ffn3_w_down", "ln_f"]
SHARDED = {"a0_w_in": ("cols", 256), "a0_conv": ("cols", A_TAPS), "a0_w_out": ("rows", 128),
           "ffn0_w_gu": ("lead", 176), "ffn0_w_down": ("rows", 176),
           "b1_w_grp": ("mid", 128),
           "ffn1_w_gu": ("lead", 176), "ffn1_w_down": ("rows", 176),
           "c2_w_pw1": ("cols", 256), "c2_dw": ("cols", C_TAPS), "c2_w_pw2": ("rows", 128),
           "ffn2_w_gu": ("lead", 176), "ffn2_w_down": ("rows", 176),
           "a3_w_in": ("cols", 256), "a3_conv": ("cols", A_TAPS), "a3_w_out": ("rows", 128),
           "ffn3_w_gu": ("lead", 176), "ffn3_w_down": ("rows", 176)}
IN_PROJ = ("a0_w_in", "c2_w_pw1", "a3_w_in")
REPL = [n for n in WEIGHTS if n not in SHARDED]
REPL_ROWS = 16
GATHER_PLAN = {"first": ["a0_w_in", "a0_w_out", "a0_conv"],
               "in0": ["ffn0_w_gu"], "mix0": ["ffn0_w_down"],
               "ffn0": ["b1_w_grp", "ffn1_w_gu", "ffn1_w_down"],
               "ffn1": ["c2_w_pw1", "c2_w_pw2", "c2_dw", "ffn2_w_gu"],
               "in2": ["ffn2_w_down"],
               "mix2": ["a3_w_in", "a3_w_out", "a3_conv"],
               "ffn2": ["ffn3_w_gu", "ffn3_w_down"]}
SCATTER_PLAN = {"mixb3": ["ffn3_w_down"], "inw3": ["a3_w_out", "a3_conv"],
                "ffnx2": ["ffn3_w_gu"], "ffnw2": ["a3_w_in"],
                "mixb2": ["ffn2_w_gu", "ffn2_w_down"], "inw2": ["c2_w_pw2", "c2_dw"],
                "ffnx1": ["c2_w_pw1"],
                "ffnx0": ["ffn1_w_gu"], "ffnw0": ["ffn1_w_down", "b1_w_grp"]}
LATE_FFN = ["ffn0_w_gu", "ffn0_w_down"]
LATE_MIX = ["a0_w_in", "a0_w_out", "a0_conv"]


def _step(p):
    vec = lambda n: p[n].reshape(1, -1)
    x, target = p["x"][0], p["loss_target"][0]

    names = list(SHARDED)
    stored = lambda n, a: a.T if n.endswith("w_gu") else a
    shard = dict(zip(names, cast_all([stored(n, p[n]) for n in names])))
    full = {}

    def gather(slot):
        names = GATHER_PLAN[slot]
        return gather_ride([shard[n] for n in names], ["cols" if n in IN_PROJ else "lead" for n in names])

    def landed(slot, outs):
        full.update(zip(GATHER_PLAN[slot], outs))

    def conv_full(n):
        k = full[n].shape[1]
        return full[n].transpose(1, 0, 2).reshape(k, D).astype(F32)

    def wgu(i):
        return full[f"ffn{i}_w_gu"].reshape(2, 4, FC, D)

    def wd(i):
        return full[f"ffn{i}_w_down"].reshape(4, FC, D)

    landed("first", run_ride(gather("first"), "gather_first"))
    no_bias = jnp.zeros((1, 3 * D), F32)
    h = [x]
    saved = {}
    conv, wout = {}, {}

    (z, u), got = rms_matmul(h[-1], vec("ln1_0"), full["a0_w_in"], no_bias, ride=gather("in0"))
    landed("in0", got)
    conv[0], wout[0] = conv_full("a0_conv"), full["a0_w_out"].reshape(D, D)
    (hm,), got = a_mix_fwd(z, h[-1], conv[0], wout[0], ride=gather("mix0"))
    landed("mix0", got)
    saved["mix0"] = (z, u)
    h.append(hm)
    (hn, zf, uf), got = ffn_fwd(hm, vec("ln2_0"), wgu(0), wd(0), ride=gather("ffn0"))
    landed("ffn0", got)
    saved["ffn0"] = (zf, uf)
    h.append(hn)

    wgrp = full["b1_w_grp"].transpose(1, 0, 2, 3).reshape(4, GW, GW)
    (hm,), _ = b_mix_fwd(h[-1], vec("ln1_1"), wgrp, vec("b1_scale"))
    h.append(hm)
    (hn, zf, uf), got = ffn_fwd(hm, vec("ln2_1"), wgu(1), wd(1), ride=gather("ffn1"))
    landed("ffn1", got)
    saved["ffn1"] = (zf, uf)
    h.append(hn)

    (z, u), got = rms_matmul(h[-1], vec("ln1_2"), full["c2_w_pw1"], vec("c2_b_pw1"), ride=gather("in2"))
    landed("in2", got)
    cdw, wpw2 = conv_full("c2_dw"), full["c2_w_pw2"].reshape(D, D)
    (hm, h2), got = c_mix_fwd(z, h[-1], cdw, vec("c2_b_dw"), vec("c2_ln_g"), vec("c2_ln_b"), wpw2, vec("c2_b_pw2"),
                              ride=gather("mix2"))
    landed("mix2", got)
    saved["mix2"] = (z, u, h2)
    h.append(hm)
    (hn, zf, uf), got = ffn_fwd(hm, vec("ln2_2"), wgu(2), wd(2), ride=gather("ffn2"))
    landed("ffn2", got)
    saved["ffn2"] = (zf, uf)
    h.append(hn)

    (z, u), _ = rms_matmul(h[-1], vec("ln1_3"), full["a3_w_in"], no_bias)
    conv[3], wout[3] = conv_full("a3_conv"), full["a3_w_out"].reshape(D, D)
    (hm,), _ = a_mix_fwd(z, h[-1], conv[3], wout[3])
    saved["mix3"] = (z, u)
    h.append(hm)
    (hn, zf, uf), _ = ffn_fwd(hm, vec("ln2_3"), wgu(3), wd(3))
    saved["ffn3"] = (zf, uf)
    h.append(hn)

    loss_lanes, dh, g_lnf = loss_head(h[-1], vec("ln_f"), target)

    g = {"ln_f": g_lnf}
    recv = {}

    def repl_rows():
        loss_row = jnp.pad(loss_lanes, ((0, 0), (0, D - LOSS_LANES)))
        return jnp.concatenate([g[n].reshape(-1, D) for n in REPL] + [loss_row], axis=0)

    def scatter(slot):
        parts = []
        for n in SCATTER_PLAN.get(slot, []):
            parts.append((repl_rows(), "all") if n == "repl" else (g[n], SHARDED[n][0]))
        return scatter_ride(parts) if parts else None

    def arrived(slot, outs):
        recv.update(zip(SCATTER_PLAN.get(slot, []), outs))

    for i in (3, 2, 1, 0):
        zf, uf = saved[f"ffn{i}"]
        (dh_prev, dzf, a, g[f"ln2_{i}"], dhb), got = ffn_bwd_x(dh, h[2 * i + 1], vec(f"ln2_{i}"), zf, wgu(i), wd(i),
                                                             ride=scatter(f"ffnx{i}"))
        arrived(f"ffnx{i}", got)
        dh = dh_prev
        (dwgu, dwd), got = ffn_bwd_w(uf, dzf, a, dhb, ride=scatter(f"ffnw{i}"))
        arrived(f"ffnw{i}", got)
        g[f"ffn{i}_w_gu"], g[f"ffn{i}_w_down"] = dwgu.reshape(N_DEV, FC, D), dwd.reshape(FF, D)
        if i == 0:
            late_ffn = scatter_start([(g[n], SHARDED[n][0]) for n in LATE_FFN], "late_ffn_start")
        hin = h[2 * i]
        if i in (0, 3):
            z, u = saved[f"mix{i}"]
            (dz, g[f"a{i}_w_out"], g[f"a{i}_conv"]), got = a_mix_bwd(dh, z, conv[i], wout[i], ride=scatter(f"mixb{i}"))
            arrived(f"mixb{i}", got)
            (g[f"a{i}_w_in"],), got = in_proj_bwd_w(u, dz, ride=scatter(f"inw{i}"))
            arrived(f"inw{i}", got)
            if i == 0:
                late_mix = scatter_start([(g[n], SHARDED[n][0]) for n in LATE_MIX], "late_mix_start")
            (dh, g[f"ln1_{i}"], _), got = in_proj_bwd_x(dz, full[f"a{i}_w_in"], hin, vec(f"ln1_{i}"), dh,
                                                       ride=scatter(f"inx{i}"))
            arrived(f"inx{i}", got)
        elif i == 1:
            (dh, g["ln1_1"], g["b1_w_grp"], g["b1_scale"]), got = b_mix_bwd(dh, hin, vec("ln1_1"), wgrp, vec("b1_scale"),
                                                                             ride=scatter("mixb1"))
            arrived("mixb1", got)
        else:
            z, u, h2 = saved["mix2"]
            (dz, g["c2_w_pw2"], g["c2_b_pw2"], g["c2_ln_g"], g["c2_ln_b"], g["c2_b_dw"], g["c2_dw"]), got = c_mix_bwd(
                dh, z, h2, cdw, vec("c2_ln_g"), vec("c2_ln_b"), wpw2, ride=scatter("mixb2"))
            arrived("mixb2", got)
            (g["c2_w_pw1"],), got = in_proj_bwd_w(u, dz, ride=scatter("inw2"))
            arrived("inw2", got)
            (dh, g["ln1_2"], g["c2_b_pw1"]), got = in_proj_bwd_x(dz, full["c2_w_pw1"], hin, vec("ln1_2"), dh,
                                                                ride=scatter("inx2"))
            arrived("inx2", got)
    grad_x = dh[None]
    late_repl = scatter_start([(repl_rows(), "all")], "late_repl_start")

    grad, delta, new_m, new_v = {}, {}, {}, {}
    two_d = lambda n, a: stored(n, a.reshape(-1, p[n].shape[-1]))

    def adam_calls(names_, tokens):
        groups, last = {}, None
        for n in names_:
            groups.setdefault((two_d(n, p[n]).shape, SHARDED[n][1]), []).append(n)
        for (shape, rb), members in groups.items():
            per_weight = 2 * rb * shape[1] * (7 * 4 + N_DEV * recv[members[0]].dtype.itemsize)
            at_once = max(1, (ADAM_VMEM // per_weight))
            for lo in range(0, len(members), at_once):
                ns = members[lo:lo + at_once]
                outs = adamw([two_d(n, p[n]) for n in ns], [two_d(n, p["m_" + n]) for n in ns],
                             [two_d(n, p["v_" + n]) for n in ns], [recv[n].reshape(N_DEV, *shape) for n in ns], rb, tokens)
                for res, o in zip((grad, delta, new_m, new_v), outs):
                    res.update({n: stored(n, a).reshape(p[n].shape) for n, a in zip(ns, o)})
                last = outs[0][0]
        return last

    early_done = adam_calls([n for n in SHARDED if n not in LATE_FFN + LATE_MIX],
                            (late_ffn[-1], late_mix[-1], late_repl[-1]))
    recv.update(zip(LATE_FFN, scatter_wait(late_ffn, [early_done], "late_ffn_wait")))
    recv.update(zip(LATE_MIX, scatter_wait(late_mix, [early_done, dh], "late_mix_wait")))
    late_done = adam_calls(LATE_FFN + LATE_MIX, ())
    recv["repl"] = scatter_wait(late_repl, [late_done], "late_repl_wait")[0]
    outs = adamw_vectors([vec(n) for n in REPL], [vec("m_" + n) for n in REPL], [vec("v_" + n) for n in REPL], recv["repl"])
    for res, o in zip((grad, delta, new_m, new_v), outs):
        res.update({n: a.reshape(p[n].shape) for n, a in zip(REPL, o)})

    loss = jnp.sum(recv["repl"][:, REPL_ROWS, 0])
    return (loss, grad_x, *[grad[n] for n in WEIGHTS], *[delta[n] for n in WEIGHTS],
            *[new_m[n] for n in WEIGHTS], *[new_v[n] for n in WEIGHTS])


def kernel(x, ln1_0, a0_w_in, a0_conv, a0_w_out, ln2_0, ffn0_w_gu, ffn0_w_down, ln1_1, b1_w_grp, b1_scale, ln2_1, ffn1_w_gu, ffn1_w_down, ln1_2, c2_w_pw1, c2_b_pw1, c2_dw, c2_b_dw, c2_ln_g, c2_ln_b, c2_w_pw2, c2_b_pw2, ln2_2, ffn2_w_gu, ffn2_w_down, ln1_3, a3_w_in, a3_conv, a3_w_out, ln2_3, ffn3_w_gu, ffn3_w_down, ln_f, loss_target, m_ln1_0, m_a0_w_in, m_a0_conv, m_a0_w_out, m_ln2_0, m_ffn0_w_gu, m_ffn0_w_down, m_ln1_1, m_b1_w_grp, m_b1_scale, m_ln2_1, m_ffn1_w_gu, m_ffn1_w_down, m_ln1_2, m_c2_w_pw1, m_c2_b_pw1, m_c2_dw, m_c2_b_dw, m_c2_ln_g, m_c2_ln_b, m_c2_w_pw2, m_c2_b_pw2, m_ln2_2, m_ffn2_w_gu, m_ffn2_w_down, m_ln1_3, m_a3_w_in, m_a3_conv, m_a3_w_out, m_ln2_3, m_ffn3_w_gu, m_ffn3_w_down, m_ln_f, v_ln1_0, v_a0_w_in, v_a0_conv, v_a0_w_out, v_ln2_0, v_ffn0_w_gu, v_ffn0_w_down, v_ln1_1, v_b1_w_grp, v_b1_scale, v_ln2_1, v_ffn1_w_gu, v_ffn1_w_down, v_ln1_2, v_c2_w_pw1, v_c2_b_pw1, v_c2_dw, v_c2_b_dw, v_c2_ln_g, v_c2_ln_b, v_c2_w_pw2, v_c2_b_pw2, v_ln2_2, v_ffn2_w_gu, v_ffn2_w_down, v_ln1_3, v_a3_w_in, v_a3_conv, v_a3_w_out, v_ln2_3, v_ffn3_w_gu, v_ffn3_w_down, v_ln_f):
    return _step(dict(locals()))
```

```python
import jax
import jax.numpy as jnp
from jax import lax
from jax.experimental import pallas as pl
from jax.experimental.pallas import tpu as pltpu

F32 = jnp.float32
BF16 = jnp.bfloat16

N_DEV = 8
D = 1024
FF = 2816
FC = FF // 4
RMS_EPS = 1e-6
LN_EPS = 1e-5
TM = 512
HALO = 32
VMEM_LIMIT = 60 * 1024 * 1024

NT = (((1,), (1,)), ((), ()))
TN = (((0,), (0,)), ((), ()))
MESH = pl.DeviceIdType.MESH
ANY = pl.BlockSpec(memory_space=pl.ANY)
N_PEERS = N_DEV - 1


def _dot(a, b):
    return jnp.dot(a, b, preferred_element_type=F32)


def _dot_nt(a, b):
    return lax.dot_general(a, b, NT, preferred_element_type=F32)


def _dot_tn(a, b):
    return lax.dot_general(a, b, TN, preferred_element_type=F32)


def _rms_fwd(x, gain):
    r = lax.rsqrt(jnp.mean(x * x, axis=-1, keepdims=True) + RMS_EPS)
    return x * r * gain


def _rms_bwd(x, gain, du):
    r = lax.rsqrt(jnp.mean(x * x, axis=-1, keepdims=True) + RMS_EPS)
    xhat = x * r
    dgain = jnp.sum(du * xhat, axis=0, keepdims=True)
    dxhat = du * gain
    dx = r * (dxhat - xhat * jnp.mean(dxhat * xhat, axis=-1, keepdims=True))
    return dx, dgain


def _dev_index(p):
    return 4 * p[0] + 2 * p[1] + p[2]


def _place():
    return lax.axis_index("x"), lax.axis_index("y"), lax.axis_index("c")


class Ride:
    def __init__(self, ins, out_shapes, start, finish):
        self.ins, self.out_shapes, self.start, self.finish = list(ins), list(out_shapes), start, finish
        n = len(self.ins)
        self.sems = [pltpu.SemaphoreType.DMA((n * N_PEERS,)), pltpu.SemaphoreType.DMA((n * N_PEERS,)),
                     pltpu.SemaphoreType.DMA((n,))]


def gather_ride(shards, kinds):
    n = len(shards)

    def setup(ins, outs, sems):
        send_sems, recv_sems, local_sems = sems
        x, y, c = _place()
        chips = [(1 - x, y), (x, 1 - y), (1 - x, 1 - y)]

        def copy(a, k, block, to, src=None):
            slot = _chunk(outs[a], kinds[a], _dev_index(block))
            return pltpu.make_async_remote_copy(
                src_ref=slot if src is None else src, dst_ref=slot,
                send_sem=send_sems.at[a * N_PEERS + k], recv_sem=recv_sems.at[a * N_PEERS + k],
                device_id=to, device_id_type=MESH)

        def mine(a):
            return pltpu.make_async_copy(ins[a], _chunk(outs[a], kinds[a], _dev_index((x, y, c))), local_sems.at[a])

        def first(a):
            return [copy(a, 0, (x, y, c), (x, y, 1 - c), src=ins[a])] + [
                copy(a, 1 + j, (x, y, c), (*chip, c), src=ins[a]) for j, chip in enumerate(chips)]

        return (x, y, c), chips, copy, mine, first

    def start(ins, outs, sems):
        _, _, _, mine, first = setup(ins, outs, sems)
        for a in range(n):
            mine(a).start()
            for cp in first(a):
                cp.start()

    def finish(ins, outs, sems):
        (x, y, c), chips, copy, mine, first = setup(ins, outs, sems)
        me, sibling = (x, y, c), (x, y, 1 - c)
        for a in range(n):
            for j, chip in enumerate(chips):
                copy(a, 1 + j, (*chip, c), me).wait_recv()
                copy(a, 4 + j, (*chip, c), sibling).start()
        for a in range(n):
            copy(a, 0, sibling, me).wait_recv()
            for j, chip in enumerate(chips):
                copy(a, 4 + j, (*chip, 1 - c), me).wait_recv()
        for a in range(n):
            for cp in first(a):
                cp.wait_send()
            for j, chip in enumerate(chips):
                copy(a, 4 + j, (*chip, c), sibling).wait_send()
        for a in range(n):
            mine(a).wait()

    shapes = [(N_DEV, *s.shape) if kind == "lead" else (s.shape[0], N_DEV * s.shape[1]) for s, kind in zip(shards, kinds)]
    return Ride(shards, [jax.ShapeDtypeStruct(shape, s.dtype) for shape, s in zip(shapes, shards)], start, finish)


def _chunk(ref, kind, j):
    if kind == "lead":
        return ref.at[j]
    if kind == "rows":
        r = ref.shape[0] // N_DEV
        return ref.at[pl.ds(j * r, r)]
    if kind == "mid":
        r = ref.shape[1] // N_DEV
        return ref.at[:, pl.ds(j * r, r), :]
    if kind == "cols":
        c = ref.shape[1] // N_DEV
        return ref.at[:, pl.ds(j * c, c)]
    return ref


def _chunk_shape(shape, kind):
    if kind == "lead":
        return tuple(shape[1:])
    if kind == "rows":
        return (shape[0] // N_DEV, *shape[1:])
    if kind == "mid":
        return (shape[0], shape[1] // N_DEV, shape[2])
    if kind == "cols":
        return (shape[0], shape[1] // N_DEV)
    return tuple(shape)


def scatter_ride(parts):
    n = len(parts)
    kinds = [k for _, k in parts]

    def setup(ins, outs, sems):
        send_sems, recv_sems, local_sems = sems
        x, y, c = _place()
        me = _dev_index((x, y, c))
        peers = []
        for k in range(1, N_DEV):
            kx, ky, kc = (k >> 2) & 1, (k >> 1) & 1, k & 1
            peers.append((1 - x if kx else x, 1 - y if ky else y, 1 - c if kc else c))

        def copy(a, k, peer):
            return pltpu.make_async_remote_copy(
                src_ref=_chunk(ins[a], kinds[a], _dev_index(peer)), dst_ref=outs[a].at[me],
                send_sem=send_sems.at[a * N_PEERS + k], recv_sem=recv_sems.at[a * N_PEERS + k],
                device_id=peer, device_id_type=MESH)

        def arrival(a, k, peer):
            slot = outs[a].at[_dev_index(peer)]
            return pltpu.make_async_remote_copy(
                src_ref=slot, dst_ref=slot,
                send_sem=send_sems.at[a * N_PEERS + k], recv_sem=recv_sems.at[a * N_PEERS + k],
                device_id=peer, device_id_type=MESH)

        def mine(a):
            return pltpu.make_async_copy(_chunk(ins[a], kinds[a], me), outs[a].at[me], local_sems.at[a])

        return peers, copy, arrival, mine

    def start(ins, outs, sems):
        peers, copy, _, mine = setup(ins, outs, sems)
        for a in range(n):
            mine(a).start()
            for k, peer in enumerate(peers):
                copy(a, k, peer).start()

    def finish(ins, outs, sems):
        peers, copy, arrival, mine = setup(ins, outs, sems)
        for a in range(n):
            for k, peer in enumerate(peers):
                arrival(a, k, peer).wait_recv()
        for a in range(n):
            for k, peer in enumerate(peers):
                copy(a, k, peer).wait_send()
            mine(a).wait()

    shapes = [jax.ShapeDtypeStruct((N_DEV, *_chunk_shape(arr.shape, kind)), arr.dtype) for arr, kind in parts]
    return Ride([arr for arr, _ in parts], shapes, start, finish)


HBM = pl.BlockSpec(memory_space=pltpu.HBM)
SEM = pl.BlockSpec(memory_space=pltpu.SEMAPHORE)
DATAFLOW = pltpu.SideEffectType.DATAFLOW_SIDE_EFFECTING
TOKEN = (8, 128)


def _scatter_copies(kinds, ins, lands, send_sems, recv_sems):
    x, y, c = _place()
    me = _dev_index((x, y, c))
    sends, arrivals = [], []
    for a, kind in enumerate(kinds):
        for k in range(1, N_DEV):
            kx, ky, kc = (k >> 2) & 1, (k >> 1) & 1, k & 1
            peer = (1 - x if kx else x, 1 - y if ky else y, 1 - c if kc else c)
            sem = a * N_PEERS + k - 1
            sends.append(pltpu.make_async_remote_copy(
                src_ref=_chunk(ins[a], kind, _dev_index(peer)), dst_ref=lands[a].at[me],
                send_sem=send_sems.at[sem], recv_sem=recv_sems.at[sem], device_id=peer, device_id_type=MESH))
            slot = lands[a].at[_dev_index(peer)]
            arrivals.append(pltpu.make_async_remote_copy(
                src_ref=slot, dst_ref=slot, send_sem=send_sems.at[sem], recv_sem=recv_sems.at[sem],
                device_id=peer, device_id_type=MESH))
    return me, sends, arrivals


def own_blocks(parts, name):
    n = len(parts)
    x, y, c = _place()
    me = jnp.reshape(_dev_index((x, y, c)), (1,)).astype(jnp.int32)

    def block_of(shape, kind):
        blk = _chunk_shape(shape, kind)
        if kind == "lead":
            return pl.BlockSpec((1, *blk), lambda i, me_ref: (me_ref[0], *[0] * len(blk)))
        if kind == "rows":
            return pl.BlockSpec(blk, lambda i, me_ref: (me_ref[0], *[0] * (len(blk) - 1)))
        if kind == "cols":
            return pl.BlockSpec(blk, lambda i, me_ref: (0, me_ref[0]))
        return pl.BlockSpec(blk, lambda i, me_ref: (0,) * len(blk))

    def body(me_ref, *refs):
        for (arr, kind), src, dst in zip(parts, refs[:n], refs[n:]):
            dst[0] = src[0] if kind == "lead" else src[...]

    shapes = [(N_DEV, *_chunk_shape(a.shape, k)) for a, k in parts]
    return pl.pallas_call(
        body, name=name,
        grid_spec=pltpu.PrefetchScalarGridSpec(
            num_scalar_prefetch=1, grid=(1,),
            in_specs=[block_of(a.shape, k) for a, k in parts],
            out_specs=[pl.BlockSpec((1, *s[1:]), lambda i, me_ref, r=len(s) - 1: (me_ref[0], *[0] * r)) for s in shapes]),
        out_shape=[jax.ShapeDtypeStruct(s, a.dtype) for s, (a, _) in zip(shapes, parts)],
        compiler_params=pltpu.CompilerParams(vmem_limit_bytes=VMEM_LIMIT),
    )(me, *[a for a, _ in parts])


def scatter_start(parts, name):
    n = len(parts)
    kinds = [k for _, k in parts]
    arrays = [pltpu.with_memory_space_constraint(a, pltpu.HBM) for a, _ in parts]
    zones = [pltpu.with_memory_space_constraint(z, pltpu.HBM) for z in own_blocks(parts, name + "_own")]

    def body(*refs):
        ins, lands = refs[:n], refs[n:2 * n]
        send_sems, recv_sems = refs[2 * n], refs[2 * n + 1]
        token = refs[4 * n + 2]
        _, sends, _ = _scatter_copies(kinds, ins, lands, send_sems, recv_sems)
        for cp in sends:
            cp.start()
        token[...] = jnp.zeros_like(token)

    outs = pl.pallas_call(
        body, name=name,
        out_shape=(pltpu.SemaphoreType.DMA((n * N_PEERS,)), pltpu.SemaphoreType.DMA((n * N_PEERS,)),
                   *[pltpu.HBM(a.shape, a.dtype) for a in arrays], *[pltpu.HBM(z.shape, z.dtype) for z in zones],
                   jax.ShapeDtypeStruct(TOKEN, F32)),
        in_specs=[HBM] * (2 * n),
        out_specs=(SEM, SEM, *[HBM] * (2 * n), pl.BlockSpec(memory_space=pltpu.VMEM)),
        input_output_aliases={i: 2 + i for i in range(2 * n)},
        compiler_params=pltpu.CompilerParams(has_side_effects=DATAFLOW),
    )(*arrays, *zones)
    return kinds, outs[0], outs[1], outs[2:2 + n], outs[2 + n:2 + 2 * n], outs[2 + 2 * n]


def scatter_wait(started, after, name):
    kinds, send_sems, recv_sems, arrays, zones, _ = started
    n = len(kinds)

    def body(*refs):
        ins, lands = refs[:n], refs[n:2 * n]
        _, sends, arrivals = _scatter_copies(kinds, ins, lands, refs[2 * n], refs[2 * n + 1])
        for cp in sends:
            cp.wait_send()
        for cp in arrivals:
            cp.wait_recv()

    outs = pl.pallas_call(
        body, name=name,
        out_shape=(*[pltpu.HBM(a.shape, a.dtype) for a in arrays], *[pltpu.HBM(z.shape, z.dtype) for z in zones]),
        in_specs=[HBM] * (2 * n) + [SEM, SEM] + [ANY] * len(after),
        out_specs=[HBM] * (2 * n),
        input_output_aliases={i: i for i in range(2 * n)},
        compiler_params=pltpu.CompilerParams(has_side_effects=DATAFLOW),
    )(*arrays, *zones, send_sems, recv_sems, *after)
    return outs[n:]


def run_ride(ride, name):
    n_in, n_out = len(ride.ins), len(ride.out_shapes)

    def body(*refs):
        ins, outs, sems = refs[:n_in], refs[n_in:n_in + n_out], refs[n_in + n_out:]
        ride.start(ins, outs, sems)
        ride.finish(ins, outs, sems)

    return pl.pallas_call(
        body, name=name, in_specs=[ANY] * n_in, out_specs=[ANY] * n_out, out_shape=ride.out_shapes,
        scratch_shapes=ride.sems,
    )(*ride.ins)


def _call(body, *, name, grid, in_specs, out_specs, out_shape, args, scratch_shapes=(), ride=None):
    params = pltpu.CompilerParams(dimension_semantics=("arbitrary",) * len(grid), vmem_limit_bytes=VMEM_LIMIT)
    if ride is None:
        outs = pl.pallas_call(body, name=name, grid=grid, in_specs=in_specs, out_specs=out_specs, out_shape=out_shape,
                              scratch_shapes=list(scratch_shapes), compiler_params=params)(*args)
        return outs, []
    n_in, n_out, n_scr = len(in_specs), len(out_specs), len(scratch_shapes)
    r_in, r_out = len(ride.ins), len(ride.out_shapes)

    def hosted(*refs):
        ins, refs = refs[:n_in], refs[n_in:]
        rins, refs = refs[:r_in], refs[r_in:]
        outs, refs = refs[:n_out], refs[n_out:]
        routs, refs = refs[:r_out], refs[r_out:]
        scratch, sems = refs[:n_scr], refs[n_scr:]
        step, n_steps = pl.program_id(0), grid[0]
        for d in range(1, len(grid)):
            step, n_steps = step * grid[d] + pl.program_id(d), n_steps * grid[d]

        @pl.when(step == 0)
        def _():
            ride.start(rins, routs, sems)

        body(*ins, *outs, *scratch)

        @pl.when(step == n_steps - 1)
        def _():
            ride.finish(rins, routs, sems)

    outs = pl.pallas_call(
        hosted, name=name + "_ride", grid=grid,
        in_specs=list(in_specs) + [ANY] * r_in, out_specs=list(out_specs) + [ANY] * r_out,
        out_shape=list(out_shape) + ride.out_shapes,
        scratch_shapes=list(scratch_shapes) + ride.sems, compiler_params=params,
    )(*args, *ride.ins)
    return outs[:n_out], outs[n_out:]


def ffn_fwd(h, gain, wgu, wd, ride=None):
    t = h.shape[0]
    tf = min(TF, t)

    def body(h_ref, g_ref, wgu_ref, wd_ref, hn_ref, z_ref, u_ref, acc):
        k = pl.program_id(1)

        @pl.when(k == 0)
        def _():
            u_ref[...] = _rms_fwd(h_ref[...], g_ref[...]).astype(BF16)
            acc[...] = jnp.zeros_like(acc)

        u = u_ref[...]
        g = _dot_nt(u, wgu_ref[0, 0])
        up = _dot_nt(u, wgu_ref[1, 0])
        z_ref[0, 0] = g.astype(BF16)
        z_ref[1, 0] = up.astype(BF16)
        a = g * jax.nn.sigmoid(g) * up
        acc[...] += _dot(a.astype(BF16), wd_ref[0])

        @pl.when(k == 3)
        def _():
            hn_ref[...] = h_ref[...] + acc[...]

    return _call(
        body, name="ffn_fwd", grid=(t // tf, 4), ride=ride,
        in_specs=[pl.BlockSpec((tf, D), lambda i, k: (i, 0)),
                  pl.BlockSpec((1, D), lambda i, k: (0, 0)),
                  pl.BlockSpec((2, 1, FC, D), lambda i, k: (0, k, 0, 0)),
                  pl.BlockSpec((1, FC, D), lambda i, k: (k, 0, 0))],
        out_specs=[pl.BlockSpec((tf, D), lambda i, k: (i, 0)),
                   pl.BlockSpec((2, 1, tf, FC), lambda i, k: (0, k, i, 0)),
                   pl.BlockSpec((tf, D), lambda i, k: (i, 0))],
        out_shape=[jax.ShapeDtypeStruct((t, D), F32),
                   jax.ShapeDtypeStruct((2, 4, t, FC), BF16),
                   jax.ShapeDtypeStruct((t, D), BF16)],
        scratch_shapes=[pltpu.VMEM((tf, D), F32)],
        args=(h, gain, wgu, wd))


def ffn_bwd_x(dh, h, gain, z, wgu, wd, ride=None):
    t = h.shape[0]

    def body(dh_ref, h_ref, g_ref, z_ref, wgu_ref, wd_ref, dhp_ref, dz_ref, a_ref, dgain_ref, dhb, du):
        i, k = pl.program_id(0), pl.program_id(1)

        @pl.when(k == 0)
        def _():
            dhb[...] = dh_ref[...].astype(BF16)
            du[...] = jnp.zeros_like(du)

        @pl.when((k == 0) & (i == 0))
        def _():
            dgain_ref[...] = jnp.zeros_like(dgain_ref)

        da = _dot_nt(dhb[...], wd_ref[0])
        g = z_ref[0, 0].astype(F32)
        up = z_ref[1, 0].astype(F32)
        sg = jax.nn.sigmoid(g)
        silu = g * sg
        a_ref[0] = (silu * up).astype(BF16)
        dg = (da * up * (sg * (1.0 + g * (1.0 - sg)))).astype(BF16)
        dup = (da * silu).astype(BF16)
        dz_ref[0, 0] = dg
        dz_ref[1, 0] = dup
        for n in range(2):
            cols = slice(n * (D // 2), (n + 1) * (D // 2))
            du[:, cols] += _dot(dg, wgu_ref[0, 0, :, cols]) + _dot(dup, wgu_ref[1, 0, :, cols])

        @pl.when(k == 3)
        def _():
            dx, dgain = _rms_bwd(h_ref[...], g_ref[...], du[...])
            dhp_ref[...] = dh_ref[...] + dx
            dgain_ref[...] += dgain

    return _call(
        body, name="ffn_bwd_x", grid=(t // TM, 4), ride=ride,
        in_specs=[pl.BlockSpec((TM, D), lambda i, k: (i, 0)),
                  pl.BlockSpec((TM, D), lambda i, k: (i, 0)),
                  pl.BlockSpec((1, D), lambda i, k: (0, 0)),
                  pl.BlockSpec((2, 1, TM, FC), lambda i, k: (0, k, i, 0)),
                  pl.BlockSpec((2, 1, FC, D), lambda i, k: (0, k, 0, 0)),
                  pl.BlockSpec((1, FC, D), lambda i, k: (k, 0, 0))],
        out_specs=[pl.BlockSpec((TM, D), lambda i, k: (i, 0)),
                   pl.BlockSpec((2, 1, TM, FC), lambda i, k: (0, k, i, 0)),
                   pl.BlockSpec((1, TM, FC), lambda i, k: (k, i, 0)),
                   pl.BlockSpec((1, D), lambda i, k: (0, 0)),
                   pl.BlockSpec((TM, D), lambda i, k: (i, 0))],
        out_shape=[jax.ShapeDtypeStruct((t, D), F32),
                   jax.ShapeDtypeStruct((2, 4, t, FC), BF16),
                   jax.ShapeDtypeStruct((4, t, FC), BF16),
                   jax.ShapeDtypeStruct((1, D), F32),
                   jax.ShapeDtypeStruct((t, D), BF16)],
        scratch_shapes=[pltpu.VMEM((TM, D), F32)],
        args=(dh, h, gain, z, wgu, wd))


TF = 1024
TW = 2048


def ffn_bwd_w(u, dz, a, dhb, ride=None):
    t = u.shape[0]
    tw = min(TW, t)
    steps = t // tw

    def body(u_ref, dz_ref, a_ref, dh_ref, dwgu_ref, dwd_ref, acc_gu, acc_d):
        j = pl.program_id(1)

        @pl.when(j == 0)
        def _():
            acc_gu[...] = jnp.zeros_like(acc_gu)
            acc_d[...] = jnp.zeros_like(acc_d)

        ub = u_ref[...]
        acc_gu[0] += _dot_tn(dz_ref[0, 0], ub)
        acc_gu[1] += _dot_tn(dz_ref[1, 0], ub)
        acc_d[...] += _dot_tn(a_ref[0], dh_ref[...])

        @pl.when(j == steps - 1)
        def _():
            dwgu_ref[:, 0] = acc_gu[...].astype(BF16)
            dwd_ref[0] = acc_d[...].astype(BF16)

    return _call(
        body, name="ffn_bwd_w", grid=(4, steps), ride=ride,
        in_specs=[pl.BlockSpec((tw, D), lambda k, j: (j, 0)),
                  pl.BlockSpec((2, 1, tw, FC), lambda k, j: (0, k, j, 0)),
                  pl.BlockSpec((1, tw, FC), lambda k, j: (k, j, 0)),
                  pl.BlockSpec((tw, D), lambda k, j: (j, 0))],
        out_specs=[pl.BlockSpec((2, 1, FC, D), lambda k, j: (0, k, 0, 0)),
                   pl.BlockSpec((1, FC, D), lambda k, j: (k, 0, 0))],
        out_shape=[jax.ShapeDtypeStruct((2, 4, FC, D), BF16),
                   jax.ShapeDtypeStruct((4, FC, D), BF16)],
        scratch_shapes=[pltpu.VMEM((2, FC, D), F32), pltpu.VMEM((FC, D), F32)],
        args=(u, dz, a, dhb))


def _prev_halo(i, tile=TM):
    return jnp.maximum(i * (tile // HALO) - 1, 0)


def _next_halo(i, t, tile=TM):
    return jnp.minimum((i + 1) * (tile // HALO), t // HALO - 1)


def rms_matmul(h, gain, w, bias, ride=None):
    t = h.shape[0]
    n = w.shape[1]

    def body(h_ref, g_ref, w_ref, b_ref, z_ref, u_ref):
        u = _rms_fwd(h_ref[...], g_ref[...]).astype(BF16)
        u_ref[...] = u
        z_ref[...] = (_dot(u, w_ref[...]) + b_ref[...]).astype(BF16)

    return _call(
        body, name=f"rms_matmul_{n}", grid=(t // TM,), ride=ride,
        in_specs=[pl.BlockSpec((TM, D), lambda i: (i, 0)),
                  pl.BlockSpec((1, D), lambda i: (0, 0)),
                  pl.BlockSpec((D, n), lambda i: (0, 0)),
                  pl.BlockSpec((1, n), lambda i: (0, 0))],
        out_specs=[pl.BlockSpec((TM, n), lambda i: (i, 0)),
                   pl.BlockSpec((TM, D), lambda i: (i, 0))],
        out_shape=[jax.ShapeDtypeStruct((t, n), BF16),
                   jax.ShapeDtypeStruct((t, D), BF16)],
        args=(h, gain, w, bias))


def in_proj_bwd_x(dz, w, h, gain, dh, ride=None):
    t = h.shape[0]
    n = w.shape[1]

    def body(dz_ref, w_ref, h_ref, g_ref, dh_ref, dhp_ref, dgain_ref, dbias_ref):
        @pl.when(pl.program_id(0) == 0)
        def _():
            dgain_ref[...] = jnp.zeros_like(dgain_ref)
            dbias_ref[...] = jnp.zeros_like(dbias_ref)

        du = _dot_nt(dz_ref[...], w_ref[...])
        dx, dgain = _rms_bwd(h_ref[...], g_ref[...], du)
        dhp_ref[...] = dh_ref[...] + dx
        dgain_ref[...] += dgain
        dbias_ref[...] += jnp.sum(dz_ref[...].astype(F32), axis=0, keepdims=True)

    return _call(
        body, name=f"in_proj_bwd_x_{n}", grid=(t // TM,), ride=ride,
        in_specs=[pl.BlockSpec((TM, n), lambda i: (i, 0)),
                  pl.BlockSpec((D, n), lambda i: (0, 0)),
                  pl.BlockSpec((TM, D), lambda i: (i, 0)),
                  pl.BlockSpec((1, D), lambda i: (0, 0)),
                  pl.BlockSpec((TM, D), lambda i: (i, 0))],
        out_specs=[pl.BlockSpec((TM, D), lambda i: (i, 0)),
                   pl.BlockSpec((1, D), lambda i: (0, 0)),
                   pl.BlockSpec((1, n), lambda i: (0, 0))],
        out_shape=[jax.ShapeDtypeStruct((t, D), F32),
                   jax.ShapeDtypeStruct((1, D), F32),
                   jax.ShapeDtypeStruct((1, n), F32)],
        args=(dz, w, h, gain, dh))


def in_proj_bwd_w(u, dz, ride=None):
    t = u.shape[0]
    n = dz.shape[1]
    steps = t // TM

    def body(u_ref, dz_ref, dw_ref, acc):
        s = pl.program_id(0)

        @pl.when(s == 0)
        def _():
            acc[...] = jnp.zeros_like(acc)

        acc[...] += _dot_tn(u_ref[...], dz_ref[...])

        @pl.when(s == steps - 1)
        def _():
            dw_ref[...] = acc[...].astype(BF16)

    return _call(
        body, name=f"in_proj_bwd_w_{n}", grid=(steps,), ride=ride,
        in_specs=[pl.BlockSpec((TM, D), lambda s: (s, 0)),
                  pl.BlockSpec((TM, n), lambda s: (s, 0))],
        out_specs=[pl.BlockSpec((D, n), lambda s: (0, 0))],
        out_shape=[jax.ShapeDtypeStruct((D, n), BF16)],
        scratch_shapes=[pltpu.VMEM((D, n), F32)],
        args=(u, dz))


A_TAPS = 3


def a_mix_fwd(z, h, conv, wout, ride=None):
    t = h.shape[0]

    def body(z_ref, zp_ref, h_ref, cw_ref, wo_ref, hn_ref, pad):
        i = pl.program_id(0)
        ph = zp_ref[:, D:2 * D].astype(F32) * zp_ref[:, 2 * D:].astype(F32)
        pad[0:HALO, :] = jnp.where(i == 0, 0.0, ph)
        pad[HALO:, :] = z_ref[:, D:2 * D].astype(F32) * z_ref[:, 2 * D:].astype(F32)
        q = jnp.zeros((TM, D), F32)
        for k in range(A_TAPS):
            off = HALO - (A_TAPS - 1) + k
            q += cw_ref[k:k + 1, :] * pad[off:off + TM, :]
        r = z_ref[:, 0:D].astype(F32) * q
        hn_ref[...] = h_ref[...] + _dot(r.astype(BF16), wo_ref[...])

    return _call(
        body, name="a_mix_fwd", grid=(t // TM,), ride=ride,
        in_specs=[pl.BlockSpec((TM, 3 * D), lambda i: (i, 0)),
                  pl.BlockSpec((HALO, 3 * D), lambda i: (_prev_halo(i), 0)),
                  pl.BlockSpec((TM, D), lambda i: (i, 0)),
                  pl.BlockSpec((A_TAPS, D), lambda i: (0, 0)),
                  pl.BlockSpec((D, D), lambda i: (0, 0))],
        out_specs=[pl.BlockSpec((TM, D), lambda i: (i, 0))],
        out_shape=[jax.ShapeDtypeStruct((t, D), F32)],
        scratch_shapes=[pltpu.VMEM((HALO + TM, D), F32)],
        args=(z, z, h, conv, wout))


def a_mix_bwd(dh, z, conv, wout, ride=None):
    t = dh.shape[0]
    steps = t // TM

    def body(dh_ref, dhn_ref, z_ref, zp_ref, zn_ref, cw_ref, wo_ref, dz_ref, dwo_ref, dcw_ref, pad, dqpad, dwo):
        i = pl.program_id(0)
        last = i == steps - 1

        @pl.when(i == 0)
        def _():
            dwo[...] = jnp.zeros_like(dwo)
            dcw_ref[...] = jnp.zeros_like(dcw_ref)

        ph = zp_ref[:, D:2 * D].astype(F32) * zp_ref[:, 2 * D:].astype(F32)
        pad[0:HALO, :] = jnp.where(i == 0, 0.0, ph)
        c = z_ref[:, D:2 * D].astype(F32)
        v = z_ref[:, 2 * D:].astype(F32)
        pad[HALO:, :] = c * v
        q = jnp.zeros((TM, D), F32)
        for k in range(A_TAPS):
            off = HALO - (A_TAPS - 1) + k
            q += cw_ref[k:k + 1, :] * pad[off:off + TM, :]
        b = z_ref[:, 0:D].astype(F32)
        dhb = dh_ref[...].astype(BF16)
        dwo[...] += _dot_tn((b * q).astype(BF16), dhb)
        dr = _dot_nt(dhb, wo_ref[...])
        dz_ref[:, 0:D] = (dr * q).astype(BF16)
        dq = dr * b
        drn = _dot_nt(dhn_ref[...].astype(BF16), wo_ref[...])
        dqpad[0:TM, :] = dq
        dqpad[TM:, :] = jnp.where(last, 0.0, drn * zn_ref[:, 0:D].astype(F32))
        dp = jnp.zeros((TM, D), F32)
        for k in range(A_TAPS):
            off = A_TAPS - 1 - k
            dp += cw_ref[k:k + 1, :] * dqpad[off:off + TM, :]
            poff = HALO - (A_TAPS - 1) + k
            dcw_ref[k:k + 1, :] += jnp.sum(dq * pad[poff:poff + TM, :], axis=0, keepdims=True)
        dz_ref[:, D:2 * D] = (dp * v).astype(BF16)
        dz_ref[:, 2 * D:] = (dp * c).astype(BF16)

        @pl.when(last)
        def _():
            dwo_ref[...] = dwo[...].astype(BF16)

    return _call(
        body, name="a_mix_bwd", grid=(steps,), ride=ride,
        in_specs=[pl.BlockSpec((TM, D), lambda i: (i, 0)),
                  pl.BlockSpec((HALO, D), lambda i: (_next_halo(i, t), 0)),
                  pl.BlockSpec((TM, 3 * D), lambda i: (i, 0)),
                  pl.BlockSpec((HALO, 3 * D), lambda i: (_prev_halo(i), 0)),
                  pl.BlockSpec((HALO, 3 * D), lambda i: (_next_halo(i, t), 0)),
                  pl.BlockSpec((A_TAPS, D), lambda i: (0, 0)),
                  pl.BlockSpec((D, D), lambda i: (0, 0))],
        out_specs=[pl.BlockSpec((TM, 3 * D), lambda i: (i, 0)),
                   pl.BlockSpec((D, D), lambda i: (0, 0)),
                   pl.BlockSpec((A_TAPS, D), lambda i: (0, 0))],
        out_shape=[jax.ShapeDtypeStruct((t, 3 * D), BF16),
                   jax.ShapeDtypeStruct((D, D), BF16),
                   jax.ShapeDtypeStruct((A_TAPS, D), F32)],
        scratch_shapes=[pltpu.VMEM((HALO + TM, D), F32), pltpu.VMEM((TM + HALO, D), F32), pltpu.VMEM((D, D), F32)],
        args=(dh, dh, z, z, z, conv, wout))


C_TAPS = 31


def _glu(zr):
    return zr[:, 0:D].astype(F32) * jax.nn.sigmoid(zr[:, D:].astype(F32))


def _ln_silu(h2, lg, lb):
    mu = jnp.mean(h2, axis=-1, keepdims=True)
    xc = h2 - mu
    rstd = lax.rsqrt(jnp.mean(xc * xc, axis=-1, keepdims=True) + LN_EPS)
    xn = xc * rstd
    h3 = xn * lg + lb
    s3 = jax.nn.sigmoid(h3)
    return xn, rstd, h3, s3


def _ln_silu_bwd(h2, lg, lb, dh4):
    xn, rstd, h3, s3 = _ln_silu(h2, lg, lb)
    dh3 = dh4 * (s3 * (1.0 + h3 * (1.0 - s3)))
    dxn = dh3 * lg
    dh2 = rstd * (dxn - jnp.mean(dxn, axis=-1, keepdims=True) - xn * jnp.mean(dxn * xn, axis=-1, keepdims=True))
    return dh2, dh3, xn, h3 * s3


TC = 256
RB = 64
LANES = 128
SHIFTS = 7


def _shifted_copies(src, sh, rows):
    for b in range(1, SHIFTS + 1):
        sh[b - 1, 0:rows, :] = src[b:b + rows, :]


def _window(src, sh, o, r0, lanes):
    a, b = divmod(o, 8)
    ref = src if b == 0 else sh.at[b - 1]
    return ref[8 * a + r0:8 * a + r0 + RB, lanes]


def c_mix_fwd(z, h, dw, bdw, lg, lb, w2, b2, ride=None):
    t = h.shape[0]

    def body(z_ref, zp_ref, h_ref, dw_ref, bdw_ref, lg_ref, lb_ref, w2_ref, b2_ref, hn_ref, h2_ref, pad, sh):
        i = pl.program_id(0)
        pad[0:HALO, :] = jnp.where(i == 0, 0.0, _glu(zp_ref))
        pad[HALO:, :] = _glu(z_ref)
        _shifted_copies(pad, sh, TC + 24)
        for l in range(D // LANES):
            lanes = slice(l * LANES, (l + 1) * LANES)
            for r0 in range(0, TC, RB):
                acc = jnp.zeros((RB, LANES), F32) + bdw_ref[:, lanes]
                for k in range(C_TAPS):
                    acc += dw_ref[k:k + 1, lanes] * _window(pad, sh, HALO - (C_TAPS - 1) + k, r0, lanes)
                h2_ref[r0:r0 + RB, lanes] = acc
        _, _, h3, s3 = _ln_silu(h2_ref[...], lg_ref[...], lb_ref[...])
        hn_ref[...] = h_ref[...] + _dot((h3 * s3).astype(BF16), w2_ref[...]) + b2_ref[...]

    vec = pl.BlockSpec((1, D), lambda i: (0, 0))
    return _call(
        body, name="c_mix_fwd", grid=(t // TC,), ride=ride,
        in_specs=[pl.BlockSpec((TC, 2 * D), lambda i: (i, 0)),
                  pl.BlockSpec((HALO, 2 * D), lambda i: (_prev_halo(i, TC), 0)),
                  pl.BlockSpec((TC, D), lambda i: (i, 0)),
                  pl.BlockSpec((C_TAPS, D), lambda i: (0, 0)),
                  vec, vec, vec,
                  pl.BlockSpec((D, D), lambda i: (0, 0)),
                  vec],
        out_specs=[pl.BlockSpec((TC, D), lambda i: (i, 0)),
                   pl.BlockSpec((TC, D), lambda i: (i, 0))],
        out_shape=[jax.ShapeDtypeStruct((t, D), F32),
                   jax.ShapeDtypeStruct((t, D), F32)],
        scratch_shapes=[pltpu.VMEM((HALO + TC, D), F32), pltpu.VMEM((SHIFTS, TC + 24, D), F32)],
        args=(z, z, h, dw, bdw, lg, lb, w2, b2))


def c_mix_bwd(dh, z, h2, dw, lg, lb, w2, ride=None):
    t = dh.shape[0]
    steps = t // TC

    def body(dh_ref, dhn_ref, z_ref, zp_ref, h2_ref, h2n_ref, dw_ref, lg_ref, lb_ref, w2_ref,
             dz_ref, dw2_ref, db2_ref, dlg_ref, dlb_ref, dbdw_ref, ddw_ref, pad, dpad, dw2, sh, dh1):
        i = pl.program_id(0)
        last = i == steps - 1

        @pl.when(i == 0)
        def _():
            for r in (dw2, db2_ref, dlg_ref, dlb_ref, dbdw_ref, ddw_ref):
                r[...] = jnp.zeros_like(r)

        lg, lb = lg_ref[...], lb_ref[...]
        dh = dh_ref[...]
        dhb = dh.astype(BF16)
        dh2, dh3, xn, h4 = _ln_silu_bwd(h2_ref[...], lg, lb, _dot_nt(dhb, w2_ref[...]))
        dw2[...] += _dot_tn(h4.astype(BF16), dhb)
        db2_ref[...] += jnp.sum(dh, axis=0, keepdims=True)
        dlg_ref[...] += jnp.sum(dh3 * xn, axis=0, keepdims=True)
        dlb_ref[...] += jnp.sum(dh3, axis=0, keepdims=True)
        dbdw_ref[...] += jnp.sum(dh2, axis=0, keepdims=True)
        dh2n, _, _, _ = _ln_silu_bwd(h2n_ref[...], lg, lb, _dot_nt(dhn_ref[...].astype(BF16), w2_ref[...]))
        dpad[0:TC, :] = dh2
        dpad[TC:, :] = jnp.where(last, 0.0, dh2n)
        _shifted_copies(dpad, sh, TC + 24)
        for l in range(D // LANES):
            lanes = slice(l * LANES, (l + 1) * LANES)
            for r0 in range(0, TC, RB):
                acc = jnp.zeros((RB, LANES), F32)
                for k in range(C_TAPS):
                    acc += dw_ref[k:k + 1, lanes] * _window(dpad, sh, C_TAPS - 1 - k, r0, lanes)
                dh1[r0:r0 + RB, lanes] = acc
        pad[0:HALO, :] = jnp.where(i == 0, 0.0, _glu(zp_ref))
        pad[HALO:, :] = _glu(z_ref)
        _shifted_copies(pad, sh, TC + 24)
        for l in range(D // LANES):
            lanes = slice(l * LANES, (l + 1) * LANES)
            accs = [jnp.zeros((8, LANES), F32) for _ in range(C_TAPS)]
            for r0 in range(0, TC, RB):
                d = dpad[r0:r0 + RB, lanes]
                for k in range(C_TAPS):
                    prod = d * _window(pad, sh, HALO - (C_TAPS - 1) + k, r0, lanes)
                    accs[k] += jnp.sum(prod.reshape(RB // 8, 8, LANES), axis=0)
            for k in range(C_TAPS):
                ddw_ref[k:k + 1, lanes] += jnp.sum(accs[k], axis=0, keepdims=True)
        a = z_ref[:, 0:D].astype(F32)
        sg = jax.nn.sigmoid(z_ref[:, D:].astype(F32))
        d1 = dh1[...]
        dz_ref[:, 0:D] = (d1 * sg).astype(BF16)
        dz_ref[:, D:] = (d1 * a * sg * (1.0 - sg)).astype(BF16)

        @pl.when(last)
        def _():
            dw2_ref[...] = dw2[...].astype(BF16)

    vec = pl.BlockSpec((1, D), lambda i: (0, 0))
    return _call(
        body, name="c_mix_bwd", grid=(steps,), ride=ride,
        in_specs=[pl.BlockSpec((TC, D), lambda i: (i, 0)),
                  pl.BlockSpec((HALO, D), lambda i: (_next_halo(i, t, TC), 0)),
                  pl.BlockSpec((TC, 2 * D), lambda i: (i, 0)),
                  pl.BlockSpec((HALO, 2 * D), lambda i: (_prev_halo(i, TC), 0)),
                  pl.BlockSpec((TC, D), lambda i: (i, 0)),
                  pl.BlockSpec((HALO, D), lambda i: (_next_halo(i, t, TC), 0)),
                  pl.BlockSpec((C_TAPS, D), lambda i: (0, 0)),
                  vec, vec,
                  pl.BlockSpec((D, D), lambda i: (0, 0))],
        out_specs=[pl.BlockSpec((TC, 2 * D), lambda i: (i, 0)),
                   pl.BlockSpec((D, D), lambda i: (0, 0)),
                   vec, vec, vec, vec,
                   pl.BlockSpec((C_TAPS, D), lambda i: (0, 0))],
        out_shape=[jax.ShapeDtypeStruct((t, 2 * D), BF16),
                   jax.ShapeDtypeStruct((D, D), BF16)]
                  + [jax.ShapeDtypeStruct((1, D), F32)] * 4
                  + [jax.ShapeDtypeStruct((C_TAPS, D), F32)],
        scratch_shapes=[pltpu.VMEM((HALO + TC, D), F32), pltpu.VMEM((TC + HALO, D), F32), pltpu.VMEM((D, D), F32),
                        pltpu.VMEM((SHIFTS, TC + 24, D), F32), pltpu.VMEM((TC, D), F32)],
        args=(dh, dh, z, z, h2, h2, dw, lg, lb, w2))


POOL_WINDOWS = (2, 4, 8, 16)
GW = D // len(POOL_WINDOWS)


def _pool_mixed(pad, g, w, inv_cnt):
    cols = slice(g * GW, (g + 1) * GW)
    s = pad[HALO:HALO + TM, cols]
    u = s
    for j in range(1, w):
        s = s + pad[HALO - j:HALO - j + TM, cols]
    return s * inv_cnt - u


def _inv_cnt(i, w):
    row = i * TM + lax.broadcasted_iota(jnp.int32, (TM, 1), 0)
    return 1.0 / jnp.minimum(row + 1, w).astype(F32)


def b_mix_fwd(h, gain, wg, scale, ride=None):
    t = h.shape[0]

    def body(h_ref, hp_ref, g_ref, wg_ref, sc_ref, hn_ref, pad):
        i = pl.program_id(0)
        gain = g_ref[...]
        pad[0:HALO, :] = jnp.where(i == 0, 0.0, _rms_fwd(hp_ref[...], gain))
        pad[HALO:, :] = _rms_fwd(h_ref[...], gain)
        for g, w in enumerate(POOL_WINDOWS):
            cols = slice(g * GW, (g + 1) * GW)
            mixed = _pool_mixed(pad, g, w, _inv_cnt(i, w))
            y = _dot(mixed.astype(BF16), wg_ref[g])
            hn_ref[:, cols] = h_ref[:, cols] + y * sc_ref[:, cols]

    return _call(
        body, name="b_mix_fwd", grid=(t // TM,), ride=ride,
        in_specs=[pl.BlockSpec((TM, D), lambda i: (i, 0)),
                  pl.BlockSpec((HALO, D), lambda i: (_prev_halo(i), 0)),
                  pl.BlockSpec((1, D), lambda i: (0, 0)),
                  pl.BlockSpec((4, GW, GW), lambda i: (0, 0, 0)),
                  pl.BlockSpec((1, D), lambda i: (0, 0))],
        out_specs=[pl.BlockSpec((TM, D), lambda i: (i, 0))],
        out_shape=[jax.ShapeDtypeStruct((t, D), F32)],
        scratch_shapes=[pltpu.VMEM((HALO + TM, D), F32)],
        args=(h, h, gain, wg, scale))


def b_mix_bwd(dh, h, gain, wg, scale, ride=None):
    t = h.shape[0]
    steps = t // TM

    def body(dh_ref, dhn_ref, h_ref, hp_ref, g_ref, wg_ref, sc_ref, dhp_ref, dgain_ref, dwg_ref, dsc_ref, pad, dpad, du):
        i = pl.program_id(0)
        last = i == steps - 1

        @pl.when(i == 0)
        def _():
            for r in (dgain_ref, dwg_ref, dsc_ref):
                r[...] = jnp.zeros_like(r)

        gain = g_ref[...]
        pad[0:HALO, :] = jnp.where(i == 0, 0.0, _rms_fwd(hp_ref[...], gain))
        pad[HALO:, :] = _rms_fwd(h_ref[...], gain)
        for g, w in enumerate(POOL_WINDOWS):
            cols = slice(g * GW, (g + 1) * GW)
            inv_cnt = _inv_cnt(i, w)
            mixed = _pool_mixed(pad, g, w, inv_cnt).astype(BF16)
            dh = dh_ref[:, cols]
            dsc_ref[:, cols] += jnp.sum(dh * _dot(mixed, wg_ref[g]), axis=0, keepdims=True)
            dy = (dh * sc_ref[:, cols]).astype(BF16)
            dwg_ref[g] += _dot_tn(mixed, dy)
            dm = _dot_nt(dy, wg_ref[g])
            dmn = _dot_nt((dhn_ref[:, cols] * sc_ref[:, cols]).astype(BF16), wg_ref[g])
            dpad[0:TM, cols] = dm * inv_cnt
            dpad[TM:, cols] = jnp.where(last, 0.0, dmn * (1.0 / w))
            s = dpad[0:TM, cols]
            for j in range(1, w):
                s = s + dpad[j:j + TM, cols]
            du[:, cols] = s - dm
        dx, dgain = _rms_bwd(h_ref[...], gain, du[...])
        dhp_ref[...] = dh_ref[...] + dx
        dgain_ref[...] += dgain

    return _call(
        body, name="b_mix_bwd", grid=(steps,), ride=ride,
        in_specs=[pl.BlockSpec((TM, D), lambda i: (i, 0)),
                  pl.BlockSpec((HALO, D), lambda i: (_next_halo(i, t), 0)),
                  pl.BlockSpec((TM, D), lambda i: (i, 0)),
                  pl.BlockSpec((HALO, D), lambda i: (_prev_halo(i), 0)),
                  pl.BlockSpec((1, D), lambda i: (0, 0)),
                  pl.BlockSpec((4, GW, GW), lambda i: (0, 0, 0)),
                  pl.BlockSpec((1, D), lambda i: (0, 0))],
        out_specs=[pl.BlockSpec((TM, D), lambda i: (i, 0)),
                   pl.BlockSpec((1, D), lambda i: (0, 0)),
                   pl.BlockSpec((4, GW, GW), lambda i: (0, 0, 0)),
                   pl.BlockSpec((1, D), lambda i: (0, 0))],
        out_shape=[jax.ShapeDtypeStruct((t, D), F32),
                   jax.ShapeDtypeStruct((1, D), F32),
                   jax.ShapeDtypeStruct((4, GW, GW), F32),
                   jax.ShapeDtypeStruct((1, D), F32)],
        scratch_shapes=[pltpu.VMEM((HALO + TM, D), F32), pltpu.VMEM((TM + HALO, D), F32), pltpu.VMEM((TM, D), F32)],
        args=(dh, dh, h, h, gain, wg, scale))


LOSS_LANES = 128


def loss_head(h, gain, target):
    t = h.shape[0]

    def body(h_ref, g_ref, tg_ref, loss_ref, dh_ref, dgain_ref):
        @pl.when(pl.program_id(0) == 0)
        def _():
            loss_ref[...] = jnp.zeros_like(loss_ref)
            dgain_ref[...] = jnp.zeros_like(dgain_ref)

        x, gain = h_ref[...], g_ref[...]
        err = _rms_fwd(x, gain) - tg_ref[...]
        per_row = jnp.mean(err * err, axis=-1, keepdims=True)
        loss_ref[...] += jnp.broadcast_to(0.5 * jnp.sum(per_row, axis=0, keepdims=True), (1, LOSS_LANES))
        dx, dgain = _rms_bwd(x, gain, err * (1.0 / D))
        dh_ref[...] = dx
        dgain_ref[...] += dgain

    outs, _ = _call(
        body, name="loss_head", grid=(t // TM,),
        in_specs=[pl.BlockSpec((TM, D), lambda i: (i, 0)),
                  pl.BlockSpec((1, D), lambda i: (0, 0)),
                  pl.BlockSpec((TM, D), lambda i: (i, 0))],
        out_specs=[pl.BlockSpec((1, LOSS_LANES), lambda i: (0, 0)),
                   pl.BlockSpec((TM, D), lambda i: (i, 0)),
                   pl.BlockSpec((1, D), lambda i: (0, 0))],
        out_shape=[jax.ShapeDtypeStruct((1, LOSS_LANES), F32),
                   jax.ShapeDtypeStruct((t, D), F32),
                   jax.ShapeDtypeStruct((1, D), F32)],
        args=(h, gain, target))
    return outs


ADAM_LR = 0.001
ADAM_B1 = 0.9
ADAM_B2 = 0.999
ADAM_EPS = 1e-08
ADAM_WD = 0.01
ADAM_STEP = 10
ADAM_VMEM = 40 * 1024 * 1024


def cast_all(arrays):
    def body(*refs):
        for src, dst in zip(refs[:len(arrays)], refs[len(arrays):]):
            dst[...] = src[...].astype(BF16)

    return pl.pallas_call(
        body, name="cast_all", out_shape=[jax.ShapeDtypeStruct(a.shape, BF16) for a in arrays],
        compiler_params=pltpu.CompilerParams(vmem_limit_bytes=VMEM_LIMIT),
    )(*arrays)


def _adam_math(w, m, v, g):
    m = ADAM_B1 * m + (1.0 - ADAM_B1) * g
    v = ADAM_B2 * v + (1.0 - ADAM_B2) * (g * g)
    m_hat = m / (1.0 - ADAM_B1 ** ADAM_STEP)
    v_hat = v / (1.0 - ADAM_B2 ** ADAM_STEP)
    return -ADAM_LR * (m_hat / (jnp.sqrt(v_hat) + ADAM_EPS) + ADAM_WD * w), m, v


def adamw(ws, ms, vs, gps, rb, tokens=()):
    n = len(ws)
    r, c = ws[0].shape
    nb = r // rb

    def body(*refs):
        i = pl.program_id(0)
        outs = refs[4 * n + len(tokens):]
        for j in range(n):
            w_ref, m_ref, v_ref, gp_ref = (refs[q * n + j] for q in range(4))
            g_ref, d_ref, nm_ref, nv_ref = (outs[q * n + j] for q in range(4))

            @pl.when(i // nb == j)
            def _():
                g = gp_ref[0].astype(F32)
                for s in range(1, N_DEV):
                    g = g + gp_ref[s].astype(F32)
                g_ref[...] = g
                d_ref[...], nm_ref[...], nv_ref[...] = _adam_math(w_ref[...], m_ref[...], v_ref[...], g)

    def blk(j):
        return pl.BlockSpec((rb, c), lambda i: (jnp.clip(i - j * nb, 0, nb - 1), 0))

    def gblk(j):
        return pl.BlockSpec((N_DEV, rb, c), lambda i: (0, jnp.clip(i - j * nb, 0, nb - 1), 0))

    outs, _ = _call(
        body, name=f"adamw_{n}x{r}x{c}", grid=(n * nb,),
        in_specs=[blk(j) for _ in range(3) for j in range(n)] + [gblk(j) for j in range(n)] + [ANY] * len(tokens),
        out_specs=[blk(j) for _ in range(4) for j in range(n)],
        out_shape=[jax.ShapeDtypeStruct((r, c), F32)] * (4 * n),
        args=(*ws, *ms, *vs, *gps, *tokens))
    return outs[:n], outs[n:2 * n], outs[2 * n:3 * n], outs[3 * n:]


def adamw_vectors(ws, ms, vs, gparts):
    nv = len(ws)

    def body(*refs):
        w_refs, m_refs, v_refs = refs[:nv], refs[nv:2 * nv], refs[2 * nv:3 * nv]
        gp_ref = refs[3 * nv]
        outs = refs[3 * nv + 1:]
        g_refs, d_refs, nm_refs, nv_refs = outs[:nv], outs[nv:2 * nv], outs[2 * nv:3 * nv], outs[3 * nv:]
        row = 0
        for i in range(nv):
            for part in range(w_refs[i].shape[1] // D):
                cols = slice(part * D, (part + 1) * D)
                g = gp_ref[0, row:row + 1, :]
                for s in range(1, N_DEV):
                    g = g + gp_ref[s, row:row + 1, :]
                g_refs[i][:, cols] = g
                d_refs[i][:, cols], nm_refs[i][:, cols], nv_refs[i][:, cols] = _adam_math(
                    w_refs[i][:, cols], m_refs[i][:, cols], v_refs[i][:, cols], g)
                row += 1

    shapes = [jax.ShapeDtypeStruct(w.shape, F32) for w in ws]
    outs = pl.pallas_call(body, name="adamw_vectors", out_shape=shapes * 4)(*ws, *ms, *vs, gparts)
    return outs[:nv], outs[nv:2 * nv], outs[2 * nv:3 * nv], outs[3 * nv:]


WEIGHTS = ["ln1_0", "a0_w_in", "a0_conv", "a0_w_out", "ln2_0", "ffn0_w_gu", "ffn0_w_down",
           "ln1_1", "b1_w_grp", "b1_scale", "ln2_1", "ffn1_w_gu", "ffn1_w_down",
           "ln1_2", "c2_w_pw1", "c2_b_pw1", "c2_dw", "c2_b_dw", "c2_ln_g", "c2_ln_b", "c2_w_pw2", "c2_b_pw2",
           "ln2_2", "ffn2_w_gu", "ffn2_w_down",
           "ln1_3", "a3_w_in", "a3_conv", "a3_w_out", "ln2_3", "ffn3_w_gu", "ffn3_w_down", "ln_f"]
SHARDED = {"a0_w_in": ("cols", 256), "a0_conv": ("cols", A_TAPS), "a0_w_out": ("rows", 128),
           "ffn0_w_gu": ("lead", 176), "ffn0_w_down": ("rows", 176),
           "b1_w_grp": ("mid", 128),
           "ffn1_w_gu": ("lead", 176), "ffn1_w_down": ("rows", 176),
           "c2_w_pw1": ("cols", 256), "c2_dw": ("cols", C_TAPS), "c2_w_pw2": ("rows", 128),
           "ffn2_w_gu": ("lead", 176), "ffn2_w_down": ("rows", 176),
           "a3_w_in": ("cols", 256), "a3_conv": ("cols", A_TAPS), "a3_w_out": ("rows", 128),
           "ffn3_w_gu": ("lead", 176), "ffn3_w_down": ("rows", 176)}
IN_PROJ = ("a0_w_in", "c2_w_pw1", "a3_w_in")
REPL = [n for n in WEIGHTS if n not in SHARDED]
REPL_ROWS = 16
GATHER_PLAN = {"first": ["a0_w_in", "a0_w_out", "a0_conv"],
               "in0": ["ffn0_w_gu"], "mix0": ["ffn0_w_down"],
               "ffn0": ["b1_w_grp", "ffn1_w_gu", "ffn1_w_down"],
               "ffn1": ["c2_w_pw1", "c2_w_pw2", "c2_dw", "ffn2_w_gu"],
               "mix2": ["ffn2_w_down", "a3_w_in", "a3_w_out", "a3_conv"],
               "ffn2": ["ffn3_w_gu", "ffn3_w_down"]}
SCATTER_PLAN = {"mixb3": ["ffn3_w_down"], "inw3": ["a3_w_out", "a3_conv"],
                "ffnx2": ["ffn3_w_gu"], "ffnw2": ["a3_w_in"],
                "mixb2": ["ffn2_w_gu", "ffn2_w_down"], "inw2": ["c2_w_pw2", "c2_dw"],
                "ffnx1": ["c2_w_pw1"],
                "ffnx0": ["ffn1_w_gu"], "ffnw0": ["ffn1_w_down", "b1_w_grp"]}
LATE_FFN = ["ffn0_w_gu", "ffn0_w_down"]
LATE_MIX = ["a0_w_in", "a0_w_out", "a0_conv"]


def _step(p):
    vec = lambda n: p[n].reshape(1, -1)
    x, target = p["x"][0], p["loss_target"][0]

    names = list(SHARDED)
    stored = lambda n, a: a.T if n.endswith("w_gu") else a
    shard = dict(zip(names, cast_all([stored(n, p[n]) for n in names])))
    full = {}

    def gather(slot):
        names = GATHER_PLAN.get(slot, [])
        if not names:
            return None
        return gather_ride([shard[n] for n in names], ["cols" if n in IN_PROJ else "lead" for n in names])

    def landed(slot, outs):
        full.update(zip(GATHER_PLAN.get(slot, []), outs))

    def conv_full(n):
        k = full[n].shape[1]
        return full[n].transpose(1, 0, 2).reshape(k, D).astype(F32)

    def wgu(i):
        return full[f"ffn{i}_w_gu"].reshape(2, 4, FC, D)

    def wd(i):
        return full[f"ffn{i}_w_down"].reshape(4, FC, D)

    landed("first", run_ride(gather("first"), "gather_first"))
    no_bias = jnp.zeros((1, 3 * D), F32)
    h = [x]
    saved = {}
    conv, wout = {}, {}

    (z, u), got = rms_matmul(h[-1], vec("ln1_0"), full["a0_w_in"], no_bias, ride=gather("in0"))
    landed("in0", got)
    conv[0], wout[0] = conv_full("a0_conv"), full["a0_w_out"].reshape(D, D)
    (hm,), got = a_mix_fwd(z, h[-1], conv[0], wout[0], ride=gather("mix0"))
    landed("mix0", got)
    saved["mix0"] = (z, u)
    h.append(hm)
    (hn, zf, uf), got = ffn_fwd(hm, vec("ln2_0"), wgu(0), wd(0), ride=gather("ffn0"))
    landed("ffn0", got)
    saved["ffn0"] = (zf, uf)
    h.append(hn)

    wgrp = full["b1_w_grp"].transpose(1, 0, 2, 3).reshape(4, GW, GW)
    (hm,), _ = b_mix_fwd(h[-1], vec("ln1_1"), wgrp, vec("b1_scale"))
    h.append(hm)
    (hn, zf, uf), got = ffn_fwd(hm, vec("ln2_1"), wgu(1), wd(1), ride=gather("ffn1"))
    landed("ffn1", got)
    saved["ffn1"] = (zf, uf)
    h.append(hn)

    (z, u), got = rms_matmul(h[-1], vec("ln1_2"), full["c2_w_pw1"], vec("c2_b_pw1"), ride=gather("in2"))
    landed("in2", got)
    cdw, wpw2 = conv_full("c2_dw"), full["c2_w_pw2"].reshape(D, D)
    (hm, h2), got = c_mix_fwd(z, h[-1], cdw, vec("c2_b_dw"), vec("c2_ln_g"), vec("c2_ln_b"), wpw2, vec("c2_b_pw2"),
                              ride=gather("mix2"))
    landed("mix2", got)
    saved["mix2"] = (z, u, h2)
    h.append(hm)
    (hn, zf, uf), got = ffn_fwd(hm, vec("ln2_2"), wgu(2), wd(2), ride=gather("ffn2"))
    landed("ffn2", got)
    saved["ffn2"] = (zf, uf)
    h.append(hn)

    (z, u), _ = rms_matmul(h[-1], vec("ln1_3"), full["a3_w_in"], no_bias)
    conv[3], wout[3] = conv_full("a3_conv"), full["a3_w_out"].reshape(D, D)
    (hm,), _ = a_mix_fwd(z, h[-1], conv[3], wout[3])
    saved["mix3"] = (z, u)
    h.append(hm)
    (hn, zf, uf), _ = ffn_fwd(hm, vec("ln2_3"), wgu(3), wd(3))
    saved["ffn3"] = (zf, uf)
    h.append(hn)

    loss_lanes, dh, g_lnf = loss_head(h[-1], vec("ln_f"), target)

    g = {"ln_f": g_lnf}
    recv = {}

    def repl_rows():
        loss_row = jnp.pad(loss_lanes, ((0, 0), (0, D - LOSS_LANES)))
        return jnp.concatenate([g[n].reshape(-1, D) for n in REPL] + [loss_row], axis=0)

    def scatter(slot):
        parts = []
        for n in SCATTER_PLAN.get(slot, []):
            parts.append((repl_rows(), "all") if n == "repl" else (g[n], SHARDED[n][0]))
        return scatter_ride(parts) if parts else None

    def arrived(slot, outs):
        recv.update(zip(SCATTER_PLAN.get(slot, []), outs))

    for i in (3, 2, 1, 0):
        zf, uf = saved[f"ffn{i}"]
        (dh_prev, dzf, a, g[f"ln2_{i}"], dhb), got = ffn_bwd_x(dh, h[2 * i + 1], vec(f"ln2_{i}"), zf, wgu(i), wd(i),
                                                             ride=scatter(f"ffnx{i}"))
        arrived(f"ffnx{i}", got)
        dh = dh_prev
        (dwgu, dwd), got = ffn_bwd_w(uf, dzf, a, dhb, ride=scatter(f"ffnw{i}"))
        arrived(f"ffnw{i}", got)
        g[f"ffn{i}_w_gu"], g[f"ffn{i}_w_down"] = dwgu.reshape(N_DEV, FC, D), dwd.reshape(FF, D)
        if i == 0:
            late_ffn = scatter_start([(g[n], SHARDED[n][0]) for n in LATE_FFN], "late_ffn_start")
        hin = h[2 * i]
        if i in (0, 3):
            z, u = saved[f"mix{i}"]
            (dz, g[f"a{i}_w_out"], g[f"a{i}_conv"]), got = a_mix_bwd(dh, z, conv[i], wout[i], ride=scatter(f"mixb{i}"))
            arrived(f"mixb{i}", got)
            (g[f"a{i}_w_in"],), got = in_proj_bwd_w(u, dz, ride=scatter(f"inw{i}"))
            arrived(f"inw{i}", got)
            if i == 0:
                late_mix = scatter_start([(g[n], SHARDED[n][0]) for n in LATE_MIX], "late_mix_start")
            (dh, g[f"ln1_{i}"], _), got = in_proj_bwd_x(dz, full[f"a{i}_w_in"], hin, vec(f"ln1_{i}"), dh,
                                                       ride=scatter(f"inx{i}"))
            arrived(f"inx{i}", got)
        elif i == 1:
            (dh, g["ln1_1"], g["b1_w_grp"], g["b1_scale"]), got = b_mix_bwd(dh, hin, vec("ln1_1"), wgrp, vec("b1_scale"),
                                                                             ride=scatter("mixb1"))
            arrived("mixb1", got)
        else:
            z, u, h2 = saved["mix2"]
            (dz, g["c2_w_pw2"], g["c2_b_pw2"], g["c2_ln_g"], g["c2_ln_b"], g["c2_b_dw"], g["c2_dw"]), got = c_mix_bwd(
                dh, z, h2, cdw, vec("c2_ln_g"), vec("c2_ln_b"), wpw2, ride=scatter("mixb2"))
            arrived("mixb2", got)
            (g["c2_w_pw1"],), got = in_proj_bwd_w(u, dz, ride=scatter("inw2"))
            arrived("inw2", got)
            (dh, g["ln1_2"], g["c2_b_pw1"]), got = in_proj_bwd_x(dz, full["c2_w_pw1"], hin, vec("ln1_2"), dh,
                                                                ride=scatter("inx2"))
            arrived("inx2", got)
    grad_x = dh[None]
    late_repl = scatter_start([(repl_rows(), "all")], "late_repl_start")

    grad, delta, new_m, new_v = {}, {}, {}, {}
    two_d = lambda n, a: stored(n, a.reshape(-1, p[n].shape[-1]))

    def adam_calls(names_, tokens):
        groups, last = {}, None
        for n in names_:
            groups.setdefault((two_d(n, p[n]).shape, SHARDED[n][1]), []).append(n)
        for (shape, rb), members in groups.items():
            per_weight = 2 * rb * shape[1] * (7 * 4 + N_DEV * recv[members[0]].dtype.itemsize)
            at_once = max(1, (ADAM_VMEM // per_weight))
            for lo in range(0, len(members), at_once):
                ns = members[lo:lo + at_once]
                outs = adamw([two_d(n, p[n]) for n in ns], [two_d(n, p["m_" + n]) for n in ns],
                             [two_d(n, p["v_" + n]) for n in ns], [recv[n].reshape(N_DEV, *shape) for n in ns], rb, tokens)
                for res, o in zip((grad, delta, new_m, new_v), outs):
                    res.update({n: stored(n, a).reshape(p[n].shape) for n, a in zip(ns, o)})
                last = outs[0][0]
        return last

    early_done = adam_calls([n for n in SHARDED if n not in LATE_FFN + LATE_MIX],
                            (late_ffn[-1], late_mix[-1], late_repl[-1]))
    recv.update(zip(LATE_FFN, scatter_wait(late_ffn, [early_done], "late_ffn_wait")))
    recv.update(zip(LATE_MIX, scatter_wait(late_mix, [early_done, dh], "late_mix_wait")))
    late_done = adam_calls(LATE_FFN + LATE_MIX, ())
    recv["repl"] = scatter_wait(late_repl, [late_done], "late_repl_wait")[0]
    outs = adamw_vectors([vec(n) for n in REPL], [vec("m_" + n) for n in REPL], [vec("v_" + n) for n in REPL], recv["repl"])
    for res, o in zip((grad, delta, new_m, new_v), outs):
        res.update({n: a.reshape(p[n].shape) for n, a in zip(REPL, o)})

    loss = jnp.sum(recv["repl"][:, REPL_ROWS, 0])
    return (loss, grad_x, *[grad[n] for n in WEIGHTS], *[delta[n] for n in WEIGHTS],
            *[new_m[n] for n in WEIGHTS], *[new_v[n] for n in WEIGHTS])


def kernel(x, ln1_0, a0_w_in, a0_conv, a0_w_out, ln2_0, ffn0_w_gu, ffn0_w_down, ln1_1, b1_w_grp, b1_scale, ln2_1, ffn1_w_gu, ffn1_w_down, ln1_2, c2_w_pw1, c2_b_pw1, c2_dw, c2_b_dw, c2_ln_g, c2_ln_b, c2_w_pw2, c2_b_pw2, ln2_2, ffn2_w_gu, ffn2_w_down, ln1_3, a3_w_in, a3_conv, a3_w_out, ln2_3, ffn3_w_gu, ffn3_w_down, ln_f, loss_target, m_ln1_0, m_a0_w_in, m_a0_conv, m_a0_w_out, m_ln2_0, m_ffn0_w_gu, m_ffn0_w_down, m_ln1_1, m_b1_w_grp, m_b1_scale, m_ln2_1, m_ffn1_w_gu, m_ffn1_w_down, m_ln1_2, m_c2_w_pw1, m_c2_b_pw1, m_c2_dw, m_c2_b_dw, m_c2_ln_g, m_c2_ln_b, m_c2_w_pw2, m_c2_b_pw2, m_ln2_2, m_ffn2_w_gu, m_ffn2_w_down, m_ln1_3, m_a3_w_in, m_a3_conv, m_a3_w_out, m_ln2_3, m_ffn3_w_gu, m_ffn3_w_down, m_ln_f, v_ln1_0, v_a0_w_in, v_a0_conv, v_a0_w_out, v_ln2_0, v_ffn0_w_gu, v_ffn0_w_down, v_ln1_1, v_b1_w_grp, v_b1_scale, v_ln2_1, v_ffn1_w_gu, v_ffn1_w_down, v_ln1_2, v_c2_w_pw1, v_c2_b_pw1, v_c2_dw, v_c2_b_dw, v_c2_ln_g, v_c2_ln_b, v_c2_w_pw2, v_c2_b_pw2, v_ln2_2, v_ffn2_w_gu, v_ffn2_w_down, v_ln1_3, v_a3_w_in, v_a3_conv, v_a3_w_out, v_ln2_3, v_ffn3_w_gu, v_ffn3_w_down, v_ln_f):
    return _step(dict(locals()))
```
